```python
import math
import jax, jax.numpy as jnp
from jax import lax
import numpy as np

D_MODEL = 2048
BATCH = 4
SEQ = 2048
DEPTH = 2
DEC_BATCH = 128
DEC_SEQ = 4
PAST_LEN = 16384
PAGE_SIZE = 128

F32 = jnp.float32
EPS = 1e-6
N_EVEN = (DEPTH + 1) // 2
N_ODD = DEPTH // 2

GLA_H = 4
GLA_DV = D_MODEL // 8
GLA_DK = GLA_DV // 2
GLA_LR = 16
GLA_GATE_NORM = 16.0
GLA_CHUNK = 16
GLA_QK = GLA_H * GLA_DK
GLA_V = GLA_H * GLA_DV

GDN_H = 8
GDN_DK = 128
GDN_DV = (D_MODEL // 2) // GDN_H
GDN_CHUNK = 64
GDN_QK = GDN_H * GDN_DK
GDN_V = GDN_H * GDN_DV
CONV_K = 4
GDN_CONV_W = 2 * GDN_QK + GDN_V

SSD_P = 64
SSD_H = (D_MODEL // 2) // SSD_P
SSD_G = 2
SSD_N = 128
SSD_CHUNK = 64
SSD_W = SSD_H * SSD_P
SSD_CONV_W = SSD_W + 2 * SSD_G * SSD_N

S5_W = D_MODEL // 2
S5_GS = 16
S5_G = S5_W // S5_GS
S5_P = 64

D_FF = 5632
FFN_K = 3

DT_MIN = 1e-3
DT_MAX = 1e-1

AB_SIZES = (GLA_QK, GLA_QK, GLA_V, GLA_LR, GLA_V, GDN_CONV_W, GDN_H, GDN_H, GDN_V)
CD_SIZES = (SSD_W, SSD_CONV_W, SSD_H, S5_W)
IN_AB = sum(AB_SIZES)
IN_CD = sum(CD_SIZES)
OUT_AB = GLA_V + GDN_V
OUT_CD = SSD_W + S5_W

kernel_name = 'hybrid_gla_gdn_ssd_s5_convffn_step'


def _split(t, sizes):
    return jnp.split(t, np.cumsum(sizes)[:-1].tolist(), axis=-1)


def _rms(x, g):
    xf = x.astype(F32)
    y = xf * lax.rsqrt(jnp.mean(xf * xf, -1, keepdims=True) + EPS) * g.astype(F32)
    return y.astype(x.dtype)


def _head_rms(o, g):
    return o * lax.rsqrt(jnp.mean(o * o, -1, keepdims=True) + EPS) * g.astype(F32)


def _l2n(t):
    return t * lax.rsqrt(jnp.sum(t * t, -1, keepdims=True) + EPS)


def _causal_dwconv(x, buf, w, b):
    k = w.shape[0]
    L = x.shape[1]
    xp = jnp.concatenate([buf.astype(x.dtype), x], axis=1)
    y = sum(xp[:, j:j + L] * w[j] for j in range(k)) + b
    return y, xp[:, L:]


def _gla_chunked(q, k, v, log_a, s0):
    bsz, L, H, DK = q.shape
    DV = v.shape[-1]
    C = math.gcd(L, GLA_CHUNK)
    n = L // C
    r = lambda t: t.reshape(bsz, n, C, H, t.shape[-1])
    q, k, v, log_a = r(q * DK ** -0.5), r(k), r(v), r(log_a)
    b = jnp.cumsum(log_a, axis=2)
    causal = jnp.tril(jnp.ones((C, C), bool))
    diff = b[:, :, :, None] - b[:, :, None, :]
    dec = jnp.exp(jnp.where(causal[:, :, None, None], diff, -jnp.inf))
    att = jnp.einsum('bnihd,bnjhd,bnijhd->bnhij', q, k, dec)
    o_intra = jnp.einsum('bnhij,bnjhv->bnihv', att, v)
    b_last = b[:, :, -1]
    q_in = q * jnp.exp(b)
    k_out = k * jnp.exp(b_last[:, :, None] - b)

    def step(s, inp):
        qc, kc, vc, dc = inp
        o = jnp.einsum('bihd,bhdv->bihv', qc, s)
        s = s * dc[..., None] + jnp.einsum('bjhd,bjhv->bhdv', kc, vc)
        return s, o

    sw = lambda t: jnp.moveaxis(t, 1, 0)
    s_fin, o_inter = lax.scan(step, s0, (sw(q_in), sw(k_out), sw(v), sw(jnp.exp(b_last))))
    o = o_intra + jnp.moveaxis(o_inter, 0, 1)
    return o.reshape(bsz, L, H, DV), s_fin


def _gdn_chunked(q, k, v, g, beta, s0):
    bsz, L, H, DK = q.shape
    DV = v.shape[-1]
    C = math.gcd(L, GDN_CHUNK)
    n = L // C
    r4 = lambda t: t.reshape(bsz, n, C, H, t.shape[-1]).transpose(0, 1, 3, 2, 4)
    r3 = lambda t: t.reshape(bsz, n, C, H).transpose(0, 1, 3, 2)
    q, k, v = r4(q * DK ** -0.5), r4(k), r4(v)
    g, beta = r3(g), r3(beta)
    gc = jnp.cumsum(g, axis=-1)
    causal = jnp.tril(jnp.ones((C, C), bool))
    strict = jnp.tril(jnp.ones((C, C), bool), -1)
    dec = jnp.exp(jnp.where(causal, gc[..., :, None] - gc[..., None, :], -jnp.inf))
    kb = k * beta[..., None]
    m = jnp.where(strict, jnp.einsum('bnhid,bnhjd->bnhij', kb, k) * dec, 0.0)
    eye = jnp.eye(C, dtype=F32)
    rhs = jnp.concatenate([v * beta[..., None], kb * jnp.exp(gc)[..., None]], axis=-1)
    sol = lax.linalg.triangular_solve(eye + m, rhs, left_side=True, lower=True, unit_diagonal=True)
    u, w = sol[..., :DV], sol[..., DV:]
    att = jnp.einsum('bnhid,bnhjd->bnhij', q, k) * dec
    q_in = q * jnp.exp(gc)[..., None]
    g_last = gc[..., -1]
    k_out = k * jnp.exp(g_last[..., None] - gc)[..., None]

    def step(s, inp):
        uc, wc, qc, kc, ac, dc = inp
        v_new = uc - jnp.einsum('bhid,bhdv->bhiv', wc, s)
        o = jnp.einsum('bhid,bhdv->bhiv', qc, s) + jnp.einsum('bhij,bhjv->bhiv', ac, v_new)
        s = s * dc[..., None, None] + jnp.einsum('bhjd,bhjv->bhdv', kc, v_new)
        return s, o

    sw = lambda t: jnp.moveaxis(t, 1, 0)
    s_fin, o = lax.scan(step, s0, (sw(u), sw(w), sw(q_in), sw(k_out), sw(att), sw(jnp.exp(g_last))))
    o = jnp.transpose(o, (1, 0, 3, 2, 4))
    return o.reshape(bsz, L, H, DV), s_fin


def _ssd_chunked(x, dt, A, bm, cm, h0):
    bsz, L, H, P = x.shape
    N = bm.shape[-1]
    C = math.gcd(L, SSD_CHUNK)
    n = L // C
    x = x.reshape(bsz, n, C, H, P)
    dt = dt.reshape(bsz, n, C, H)
    bm = bm.reshape(bsz, n, C, H, N)
    cm = cm.reshape(bsz, n, C, H, N)
    acs = jnp.cumsum(dt * A, axis=2)
    causal = jnp.tril(jnp.ones((C, C), bool))
    diff = acs[:, :, :, None] - acs[:, :, None, :]
    dec = jnp.exp(jnp.where(causal[:, :, None], diff, -jnp.inf))
    scores = jnp.einsum('bnihs,bnjhs->bnijh', cm, bm) * dec * dt[:, :, None]
    y_intra = jnp.einsum('bnijh,bnjhp->bnihp', scores, x)
    acs_last = acs[:, :, -1]
    c_in = cm * jnp.exp(acs)[..., None]
    b_out = bm * (jnp.exp(acs_last[:, :, None] - acs) * dt)[..., None]

    def step(h, inp):
        cc, bc, xc, dc = inp
        y = jnp.einsum('bihs,bhps->bihp', cc, h)
        h = h * dc[:, :, None, None] + jnp.einsum('bjhs,bjhp->bhps', bc, xc)
        return h, y

    sw = lambda t: jnp.moveaxis(t, 1, 0)
    h_fin, y_inter = lax.scan(step, h0, (sw(c_in), sw(b_out), sw(x), sw(jnp.exp(acs_last))))
    y = y_intra + jnp.moveaxis(y_inter, 0, 1)
    return y.reshape(bsz, L, H, P), h_fin


def _s5_scan(u, a_re, a_im, b_re, b_im, c_re, c_im, d, log_dt, x0_re, x0_im):
    bsz, L, W = u.shape
    ug = u.reshape(bsz, L, S5_G, S5_GS)
    a_re, a_im = a_re.astype(F32), a_im.astype(F32)
    dt = jnp.exp(log_dt.astype(F32))[:, None]
    mag = jnp.exp(a_re * dt)
    lb_re, lb_im = mag * jnp.cos(a_im * dt), mag * jnp.sin(a_im * dt)
    nr, ni = lb_re - 1.0, lb_im
    den = a_re * a_re + a_im * a_im
    f_re = (nr * a_re + ni * a_im) / den
    f_im = (ni * a_re - nr * a_im) / den
    b_re, b_im = b_re.astype(F32), b_im.astype(F32)
    bb_re = f_re[..., None] * b_re - f_im[..., None] * b_im
    bb_im = f_re[..., None] * b_im + f_im[..., None] * b_re
    bu_re = jnp.einsum('gph,blgh->blgp', bb_re, ug)
    bu_im = jnp.einsum('gph,blgh->blgp', bb_im, ug)
    bu_re = bu_re.at[:, 0].add(lb_re * x0_re - lb_im * x0_im)
    bu_im = bu_im.at[:, 0].add(lb_re * x0_im + lb_im * x0_re)
    ar = jnp.broadcast_to(lb_re, bu_re.shape)
    ai = jnp.broadcast_to(lb_im, bu_im.shape)

    def combine(e1, e2):
        a1r, a1i, b1r, b1i = e1
        a2r, a2i, b2r, b2i = e2
        return (a2r * a1r - a2i * a1i, a2r * a1i + a2i * a1r,
                a2r * b1r - a2i * b1i + b2r, a2r * b1i + a2i * b1r + b2i)

    _, _, xr, xi = lax.associative_scan(combine, (ar, ai, bu_re, bu_im), axis=1)
    y = jnp.einsum('ghp,blgp->blgh', c_re.astype(F32), xr) - jnp.einsum('ghp,blgp->blgh', c_im.astype(F32), xi)
    y = y.reshape(bsz, L, W) + d.astype(F32) * u
    return y, xr[:, -1], xi[:, -1]


def _mixer_ab(h, s_gla, s_gdn, s_conv, w_in, gla_w2, gla_b2, gla_ng, conv_w, conv_b, A_log, dt_bias, gdn_ng, w_out):
    bsz, L, _ = h.shape
    q_a, k_a, v_a, lr_a, r_a, qkv_b, a_b, b_b, g_b = _split(h @ w_in, AB_SIZES)
    heads = lambda t, dd: t.astype(F32).reshape(bsz, L, -1, dd)
    log_a = jax.nn.log_sigmoid(lr_a.astype(F32) @ gla_w2.astype(F32) + gla_b2.astype(F32)) / GLA_GATE_NORM
    o_a, s_gla_new = _gla_chunked(heads(q_a, GLA_DK), heads(k_a, GLA_DK), heads(v_a, GLA_DV),
                                  heads(log_a, GLA_DK), s_gla.astype(F32))
    o_a = _head_rms(o_a, gla_ng) * jax.nn.silu(heads(r_a, GLA_DV))
    qkv_c, s_conv_new = _causal_dwconv(qkv_b, s_conv, conv_w, conv_b)
    q_b, k_b, v_b = _split(jax.nn.silu(qkv_c.astype(F32)), (GDN_QK, GDN_QK, GDN_V))
    g = -jnp.exp(A_log.astype(F32)) * jax.nn.softplus(a_b.astype(F32) + dt_bias.astype(F32))
    beta = jax.nn.sigmoid(b_b.astype(F32))
    o_b, s_gdn_new = _gdn_chunked(_l2n(heads(q_b, GDN_DK)), _l2n(heads(k_b, GDN_DK)), heads(v_b, GDN_DV),
                                  g, beta, s_gdn.astype(F32))
    o_b = _head_rms(o_b, gdn_ng) * jax.nn.silu(heads(g_b, GDN_DV))
    o = jnp.concatenate([o_a.reshape(bsz, L, -1), o_b.reshape(bsz, L, -1)], axis=-1).astype(h.dtype)
    return o @ w_out, s_gla_new, s_gdn_new, s_conv_new


def _mixer_cd(h, s_ssd, s_conv, s_re, s_im, w_in, conv_w, conv_b, A_log, dt_bias, ssd_d, ssd_ng,
              a_re, a_im, b_re, b_im, c_re, c_im, s5_d, log_dt, glu_w, glu_b, w_out):
    bsz, L, _ = h.shape
    z, xbc, dt_raw, u = _split(h @ w_in, CD_SIZES)
    xbc_c, s_conv_new = _causal_dwconv(xbc, s_conv, conv_w, conv_b)
    xs, bm, cm = _split(jax.nn.silu(xbc_c.astype(F32)), (SSD_W, SSD_G * SSD_N, SSD_G * SSD_N))
    rep = SSD_H // SSD_G
    bm = jnp.repeat(bm.reshape(bsz, L, SSD_G, SSD_N), rep, axis=2)
    cm = jnp.repeat(cm.reshape(bsz, L, SSD_G, SSD_N), rep, axis=2)
    xs = xs.reshape(bsz, L, SSD_H, SSD_P)
    dt = jax.nn.softplus(dt_raw.astype(F32) + dt_bias.astype(F32))
    A = -jnp.exp(A_log.astype(F32))
    y_c, s_ssd_new = _ssd_chunked(xs, dt, A, bm, cm, s_ssd.astype(F32))
    y_c = y_c + ssd_d.astype(F32)[:, None] * xs
    gsz = SSD_W // SSD_G
    y_c = y_c.reshape(bsz, L, SSD_G, gsz) * jax.nn.silu(z.astype(F32)).reshape(bsz, L, SSD_G, gsz)
    y_c = _head_rms(y_c, ssd_ng.reshape(SSD_G, gsz)).reshape(bsz, L, SSD_W)
    y_d, s_re_new, s_im_new = _s5_scan(u.astype(F32), a_re, a_im, b_re, b_im, c_re, c_im, s5_d, log_dt,
                                       s_re.astype(F32), s_im.astype(F32))
    z5 = jax.nn.gelu(y_d)
    y_d = z5 * jax.nn.sigmoid(z5 @ glu_w.astype(F32) + glu_b.astype(F32))
    o = jnp.concatenate([y_c, y_d], axis=-1).astype(h.dtype)
    return o @ w_out, s_ssd_new, s_conv_new, s_re_new, s_im_new


def _conv_ffn(h, buf, w_up, conv_w, conv_b, w_down):
    up, new_buf = _causal_dwconv(h @ w_up, buf, conv_w, conv_b)
    a, g = jnp.split(up, 2, axis=-1)
    return (jax.nn.silu(g) * a) @ w_down, new_buf


def _trunk(x, c, state, p):
    s_gla, s_gdn, s_gdnc, s_ssd, s_ssdc, s_re, s_im, s_ffn = state
    names = ('gla', 'gdn', 'gdnc', 'ssd', 'ssdc', 're', 'im', 'ffn')
    new = {k: [] for k in names}
    cond = jax.nn.silu(c.astype(F32)).astype(x.dtype)
    for layer in range(DEPTH):
        mod = (cond @ p['w_ada'][layer] + p['b_ada'][layer])[:, None, :]
        sh1, sc1, g1, sh2, sc2, g2 = jnp.split(mod, 6, axis=-1)
        h = _rms(x, p['g_mix'][layer]) * (1 + sc1) + sh1
        i = layer // 2
        if layer % 2 == 0:
            y, a, b, cv = _mixer_ab(h, s_gla[i], s_gdn[i], s_gdnc[i], p['w_in_ab'][i], p['gla_w2'][i],
                                    p['gla_b2'][i], p['gla_norm_g'][i], p['gdn_conv_w'][i], p['gdn_conv_b'][i],
                                    p['gdn_A_log'][i], p['gdn_dt_bias'][i], p['gdn_norm_g'][i], p['w_out_ab'][i])
            new['gla'].append(a)
            new['gdn'].append(b)
            new['gdnc'].append(cv)
        else:
            y, a, cv, sr, si = _mixer_cd(h, s_ssd[i], s_ssdc[i], s_re[i], s_im[i], p['w_in_cd'][i],
                                         p['ssd_conv_w'][i], p['ssd_conv_b'][i], p['ssd_A_log'][i],
                                         p['ssd_dt_bias'][i], p['ssd_D'][i], p['ssd_norm_g'][i],
                                         p['s5_A_re'][i], p['s5_A_im'][i], p['s5_B_re'][i], p['s5_B_im'][i],
                                         p['s5_C_re'][i], p['s5_C_im'][i], p['s5_D'][i], p['s5_log_dt'][i],
                                         p['s5_glu_w'][i], p['s5_glu_b'][i], p['w_out_cd'][i])
            new['ssd'].append(a)
            new['ssdc'].append(cv)
            new['re'].append(sr)
            new['im'].append(si)
        x = x + g1 * y
        h = _rms(x, p['g_ffn'][layer]) * (1 + sc2) + sh2
        y, fb = _conv_ffn(h, s_ffn[layer], p['w_ffn_up'][layer], p['ffn_conv_w'][layer],
                          p['ffn_conv_b'][layer], p['w_ffn_down'][layer])
        new['ffn'].append(fb)
        x = x + g2 * y
    out = tuple(jnp.stack(new[k]).astype(s.dtype) for k, s in zip(names, state))
    return _rms(x, p['g_final']), out


def setup_inputs(seed: int = 0) -> dict:
    key = jax.random.key(seed)
    cnt = [0]

    def nk():
        cnt[0] += 1
        return jax.random.fold_in(key, cnt[0])

    def nrm(shape, scale=1.0):
        return scale * jax.random.normal(nk(), shape, F32)

    def unif(shape, lo, hi):
        return jax.random.uniform(nk(), shape, F32, lo, hi)

    def dt_bias(shape):
        dt = jnp.exp(unif(shape, math.log(DT_MIN), math.log(DT_MAX)))
        return dt + jnp.log(-jnp.expm1(-dt))

    def gain(shape):
        return 1.0 + nrm(shape, 0.01)

    D = D_MODEL
    a_im = jnp.pi * jnp.arange(S5_P, dtype=F32)
    return {
        'x_prompt': nrm((BATCH, SEQ, D)),
        'x_sample': nrm((DEC_BATCH, DEC_SEQ, D)),
        'c_prompt': nrm((BATCH, D)),
        'c_sample': nrm((DEC_BATCH, D)),
        'state_gla': nrm((N_EVEN, DEC_BATCH, GLA_H, GLA_DK, GLA_DV), 0.1),
        'state_gdn': nrm((N_EVEN, DEC_BATCH, GDN_H, GDN_DK, GDN_DV), 0.1),
        'state_gdn_conv': nrm((N_EVEN, DEC_BATCH, CONV_K - 1, GDN_CONV_W)),
        'state_ssd': nrm((N_ODD, DEC_BATCH, SSD_H, SSD_P, SSD_N), 0.1),
        'state_ssd_conv': nrm((N_ODD, DEC_BATCH, CONV_K - 1, SSD_CONV_W)),
        'state_s5_re': nrm((N_ODD, DEC_BATCH, S5_G, S5_P), 0.1),
        'state_s5_im': nrm((N_ODD, DEC_BATCH, S5_G, S5_P), 0.1),
        'state_ffn_conv': nrm((DEPTH, DEC_BATCH, FFN_K - 1, 2 * D_FF)),
        'w_ada': nrm((DEPTH, D, 6 * D), 0.5 * D ** -0.5),
        'b_ada': nrm((DEPTH, 6 * D), 0.01),
        'g_mix': gain((DEPTH, D)),
        'g_ffn': gain((DEPTH, D)),
        'w_in_ab': nrm((N_EVEN, D, IN_AB), D ** -0.5),
        'gla_w2': nrm((N_EVEN, GLA_LR, GLA_QK), GLA_LR ** -0.5),
        'gla_b2': nrm((N_EVEN, GLA_QK), 0.01),
        'gla_norm_g': gain((N_EVEN, GLA_DV)),
        'gdn_conv_w': nrm((N_EVEN, CONV_K, GDN_CONV_W), CONV_K ** -0.5),
        'gdn_conv_b': nrm((N_EVEN, GDN_CONV_W), 0.01),
        'gdn_A_log': jnp.log(unif((N_EVEN, GDN_H), 1.0, 16.0)),
        'gdn_dt_bias': dt_bias((N_EVEN, GDN_H)),
        'gdn_norm_g': gain((N_EVEN, GDN_DV)),
        'w_out_ab': nrm((N_EVEN, OUT_AB, D), OUT_AB ** -0.5),
        'w_in_cd': nrm((N_ODD, D, IN_CD), D ** -0.5),
        'ssd_conv_w': nrm((N_ODD, CONV_K, SSD_CONV_W), CONV_K ** -0.5),
        'ssd_conv_b': nrm((N_ODD, SSD_CONV_W), 0.01),
        'ssd_A_log': jnp.log(unif((N_ODD, SSD_H), 1.0, 16.0)),
        'ssd_dt_bias': dt_bias((N_ODD, SSD_H)),
        'ssd_D': 1.0 + nrm((N_ODD, SSD_H), 0.1),
        'ssd_norm_g': gain((N_ODD, SSD_W)),
        's5_A_re': -0.5 + nrm((N_ODD, S5_G, S5_P), 0.01),
        's5_A_im': a_im + nrm((N_ODD, S5_G, S5_P), 0.01),
        's5_B_re': nrm((N_ODD, S5_G, S5_P, S5_GS), (2 * S5_GS) ** -0.5),
        's5_B_im': nrm((N_ODD, S5_G, S5_P, S5_GS), (2 * S5_GS) ** -0.5),
        's5_C_re': nrm((N_ODD, S5_G, S5_GS, S5_P), (2 * S5_P) ** -0.5),
        's5_C_im': nrm((N_ODD, S5_G, S5_GS, S5_P), (2 * S5_P) ** -0.5),
        's5_D': nrm((N_ODD, S5_W)),
        's5_log_dt': unif((N_ODD, S5_G), math.log(DT_MIN), math.log(DT_MAX)),
        's5_glu_w': nrm((N_ODD, S5_W, S5_W), S5_W ** -0.5),
        's5_glu_b': nrm((N_ODD, S5_W), 0.01),
        'w_out_cd': nrm((N_ODD, OUT_CD, D), OUT_CD ** -0.5),
        'w_ffn_up': nrm((DEPTH, D, 2 * D_FF), D ** -0.5),
        'ffn_conv_w': nrm((DEPTH, FFN_K, 2 * D_FF), FFN_K ** -0.5),
        'ffn_conv_b': nrm((DEPTH, 2 * D_FF), 0.01),
        'w_ffn_down': nrm((DEPTH, D_FF, D), D_FF ** -0.5),
        'g_final': gain((D,)),
    }


def reference(x_prompt, x_sample, c_prompt, c_sample, state_gla, state_gdn, state_gdn_conv, state_ssd,
              state_ssd_conv, state_s5_re, state_s5_im, state_ffn_conv, w_ada, b_ada, g_mix, g_ffn, w_in_ab,
              gla_w2, gla_b2, gla_norm_g, gdn_conv_w, gdn_conv_b, gdn_A_log, gdn_dt_bias, gdn_norm_g, w_out_ab,
              w_in_cd, ssd_conv_w, ssd_conv_b, ssd_A_log, ssd_dt_bias, ssd_D, ssd_norm_g, s5_A_re, s5_A_im,
              s5_B_re, s5_B_im, s5_C_re, s5_C_im, s5_D, s5_log_dt, s5_glu_w, s5_glu_b, w_out_cd, w_ffn_up,
              ffn_conv_w, ffn_conv_b, w_ffn_down, g_final):
    params = dict(w_ada=w_ada, b_ada=b_ada, g_mix=g_mix, g_ffn=g_ffn, w_in_ab=w_in_ab, gla_w2=gla_w2,
                  gla_b2=gla_b2, gla_norm_g=gla_norm_g, gdn_conv_w=gdn_conv_w, gdn_conv_b=gdn_conv_b,
                  gdn_A_log=gdn_A_log, gdn_dt_bias=gdn_dt_bias, gdn_norm_g=gdn_norm_g, w_out_ab=w_out_ab,
                  w_in_cd=w_in_cd, ssd_conv_w=ssd_conv_w, ssd_conv_b=ssd_conv_b, ssd_A_log=ssd_A_log,
                  ssd_dt_bias=ssd_dt_bias, ssd_D=ssd_D, ssd_norm_g=ssd_norm_g, s5_A_re=s5_A_re, s5_A_im=s5_A_im,
                  s5_B_re=s5_B_re, s5_B_im=s5_B_im, s5_C_re=s5_C_re, s5_C_im=s5_C_im, s5_D=s5_D,
                  s5_log_dt=s5_log_dt, s5_glu_w=s5_glu_w, s5_glu_b=s5_glu_b, w_out_cd=w_out_cd,
                  w_ffn_up=w_ffn_up, ffn_conv_w=ffn_conv_w, ffn_conv_b=ffn_conv_b, w_ffn_down=w_ffn_down,
                  g_final=g_final)
    sample_state = (state_gla, state_gdn, state_gdn_conv, state_ssd, state_ssd_conv, state_s5_re,
                    state_s5_im, state_ffn_conv)
    bp = x_prompt.shape[0]
    prompt_state = tuple(jnp.zeros(s.shape[:1] + (bp,) + s.shape[2:], s.dtype) for s in sample_state)
    y_prompt, (gla_p, gdn_p, gdn_conv_p, ssd_p, ssd_conv_p, s5_re_p, s5_im_p, ffn_conv_p) = _trunk(
        x_prompt, c_prompt, prompt_state, params)
    y_sample, (gla_s, gdn_s, gdn_conv_s, ssd_s, ssd_conv_s, s5_re_s, s5_im_s, ffn_conv_s) = _trunk(
        x_sample, c_sample, sample_state, params)
    return (y_prompt, y_sample, gla_p, gla_s, gdn_p, gdn_s, gdn_conv_p, gdn_conv_s, ssd_p, ssd_s,
            ssd_conv_p, ssd_conv_s, s5_re_p, s5_re_s, s5_im_p, s5_im_s, ffn_conv_p, ffn_conv_s)
```

```python
import functools
import math

import jax
import jax.numpy as jnp
from jax import lax
from jax.experimental import pallas as pl
from jax.experimental.pallas import tpu as pltpu

F32 = jnp.float32
BF16 = jnp.bfloat16
HIGHEST = lax.Precision.HIGHEST
EPS = 1e-6

D_MODEL = 2048
GLA_H, GLA_DK, GLA_DV, GLA_LR = 4, 128, 256, 16
GLA_GATE_NORM = 16.0
GLA_QK, GLA_V = GLA_H * GLA_DK, GLA_H * GLA_DV
GDN_H, GDN_DK, GDN_DV = 8, 128, 128
GDN_QK, GDN_V = GDN_H * GDN_DK, GDN_H * GDN_DV
CONV_K = 4
GDN_CONV_W = 2 * GDN_QK + GDN_V
SSD_P, SSD_H, SSD_G, SSD_N = 64, 16, 2, 128
SSD_W = SSD_H * SSD_P
SSD_CONV_W = SSD_W + 2 * SSD_G * SSD_N
S5_W, S5_GS, S5_G, S5_P = 1024, 16, 64, 64
S5_STATE = S5_G * S5_P
D_FF = 5632
FFN_K = 3

LANE = 128
SUBLANE = 8
VMEM_LIMIT = 48 * 1024 * 1024

AB_Q, AB_K, AB_V, AB_R, AB_QKV, AB_G, AB_SMALL, AB_N = 0, 512, 1024, 2048, 3072, 6144, 7168, 7296
AB_LR_LANE, AB_A_LANE, AB_B_LANE = 0, 16, 24
CD_XBC, CD_SMALL, CD_Z, CD_U, CD_N = 0, 1536, 2048, 3072, 4096

MIX_BLOCK = 256
GLA_CHUNK, GDN_CHUNK, SSD_CHUNK = 16, 64, 64
INV_BLOCK = 16
SAMPLE_PAD = 8
S5_STEPS = 256


def _cparams(n_axes):
    return pltpu.CompilerParams(dimension_semantics=("arbitrary",) * n_axes, vmem_limit_bytes=VMEM_LIMIT)


def _sigmoid(x):
    return 1.0 / (1.0 + jnp.exp(-x))


def _silu(x):
    return x * _sigmoid(x)


def _softplus(x):
    return jnp.maximum(x, 0.0) + jnp.log(1.0 + jnp.exp(-jnp.abs(x)))


def _gelu_tanh(x):
    return 0.5 * x * (1.0 + jnp.tanh(math.sqrt(2.0 / math.pi) * (x + 0.044715 * (x * x * x))))


def _dot(a, b, precision=None):
    return jnp.dot(a, b, precision=precision, preferred_element_type=F32)


def _dot_nt(a, b, precision=None):
    return lax.dot_general(a, b, (((1,), (1,)), ((), ())), precision=precision, preferred_element_type=F32)


def _dot_tn(a, b, precision=None):
    return lax.dot_general(a, b, (((0,), (0,)), ((), ())), precision=precision, preferred_element_type=F32)


def _head_rms(o, g):
    return o * lax.rsqrt(jnp.mean(o * o, -1, keepdims=True) + EPS) * g


class _Group:
    def __init__(self, n_tok, tile, per_token_mod, tiles_per_seq, conv_shift):
        self.n_tok = n_tok
        self.tile = tile
        self.n_tiles = n_tok // tile
        self.per_token_mod = per_token_mod
        self.tiles_per_seq = tiles_per_seq
        self.conv_shift = conv_shift

    def mod_spec(self, width, col_block, m_axis):
        if self.per_token_mod:
            return pl.BlockSpec((1, self.tile, width), lambda *g: (0, g[m_axis], col_block(*g)))
        tps = self.tiles_per_seq
        return pl.BlockSpec((1, 1, width), lambda *g: (g[m_axis] // tps, 0, col_block(*g)))


def _ada_kernel(c_ref, w_ref, b_ref, o_ref):
    cs = _silu(c_ref[...]).astype(BF16)
    o_ref[0] = _dot(cs, w_ref[0].astype(BF16)) + b_ref[0]


def _ada_mod(c, w_ada, b_ada):
    depth, d, n = w_ada.shape
    rows = c.shape[0]
    tn = 1024
    return pl.pallas_call(
        _ada_kernel,
        grid=(depth, n // tn),
        in_specs=[pl.BlockSpec((rows, d), lambda l, j: (0, 0)),
                  pl.BlockSpec((1, d, tn), lambda l, j: (l, 0, j)),
                  pl.BlockSpec((1, 1, tn), lambda l, j: (l, 0, j))],
        out_specs=pl.BlockSpec((1, rows, tn), lambda l, j: (l, 0, j)),
        out_shape=jax.ShapeDtypeStruct((depth, rows, n), F32),
        compiler_params=_cparams(2), name="ada_mod",
    )(c, w_ada, b_ada.reshape(depth, 1, n))


def _norm_mod_kernel(x_ref, g_ref, sc_ref, sh_ref, o_ref):
    x = x_ref[...]
    y = x * lax.rsqrt(jnp.mean(x * x, -1, keepdims=True) + EPS) * g_ref[...]
    o_ref[...] = (y * (1.0 + sc_ref[0]) + sh_ref[0]).astype(BF16)


def _norm_mod(x, g, mod, grp, sc_blk, sh_blk):
    d = x.shape[1]
    return pl.pallas_call(
        _norm_mod_kernel,
        grid=(grp.n_tiles,),
        in_specs=[pl.BlockSpec((grp.tile, d), lambda i: (i, 0)),
                  pl.BlockSpec((1, d), lambda i: (0, 0)),
                  grp.mod_spec(d, lambda i: sc_blk, 0),
                  grp.mod_spec(d, lambda i: sh_blk, 0)],
        out_specs=pl.BlockSpec((grp.tile, d), lambda i: (i, 0)),
        out_shape=jax.ShapeDtypeStruct(x.shape, BF16),
        compiler_params=_cparams(1), name="norm_mod",
    )(x, g.reshape(1, d), mod, mod)


def _rms_kernel(x_ref, g_ref, o_ref):
    x = x_ref[...]
    o_ref[...] = x * lax.rsqrt(jnp.mean(x * x, -1, keepdims=True) + EPS) * g_ref[...]


def _final_rms(x, g, grp):
    d = x.shape[1]
    return pl.pallas_call(
        _rms_kernel,
        grid=(grp.n_tiles,),
        in_specs=[pl.BlockSpec((grp.tile, d), lambda i: (i, 0)), pl.BlockSpec((1, d), lambda i: (0, 0))],
        out_specs=pl.BlockSpec((grp.tile, d), lambda i: (i, 0)),
        out_shape=jax.ShapeDtypeStruct(x.shape, F32),
        compiler_params=_cparams(1), name="final_rms",
    )(x, g.reshape(1, d))


def _matmul_kernel(a_ref, w_ref, o_ref):
    o_ref[...] = _dot(a_ref[...], w_ref[...])


def _matmul(a, w, grp, tn):
    k, n = w.shape
    return pl.pallas_call(
        _matmul_kernel,
        grid=(n // tn, grp.n_tiles),
        in_specs=[pl.BlockSpec((grp.tile, k), lambda j, i: (i, 0)),
                  pl.BlockSpec((k, tn), lambda j, i: (0, j))],
        out_specs=pl.BlockSpec((grp.tile, tn), lambda j, i: (i, j)),
        out_shape=jax.ShapeDtypeStruct((a.shape[0], n), F32),
        compiler_params=_cparams(2), name="matmul",
    )(a, w)


def _mm_res_kernel(*refs, n_pairs):
    x_ref, gate_ref, o_ref = refs[2 * n_pairs:]
    y = _dot(refs[0][...].astype(BF16), refs[1][...])
    for p in range(1, n_pairs):
        y = y + _dot(refs[2 * p][...].astype(BF16), refs[2 * p + 1][...])
    o_ref[...] = x_ref[...] + gate_ref[0] * y


def _mm_residual(pairs, x, mod, grp, gate_blk, tn):
    n = x.shape[1]
    in_specs, args = [], []
    for a, w in pairs:
        k = w.shape[0]
        in_specs += [pl.BlockSpec((grp.tile, k), lambda j, i: (i, 0)), pl.BlockSpec((k, tn), lambda j, i: (0, j))]
        args += [a, w]
    in_specs += [pl.BlockSpec((grp.tile, tn), lambda j, i: (i, j)),
                 grp.mod_spec(tn, lambda j, i: gate_blk * (n // tn) + j, 1)]
    return pl.pallas_call(
        functools.partial(_mm_res_kernel, n_pairs=len(pairs)),
        grid=(n // tn, grp.n_tiles),
        in_specs=in_specs,
        out_specs=pl.BlockSpec((grp.tile, tn), lambda j, i: (i, j)),
        out_shape=jax.ShapeDtypeStruct(x.shape, F32),
        compiler_params=_cparams(2), name="mm_residual",
    )(*args, x, mod)


def _ffn_up_kernel(h_ref, wa_ref, wg_ref, cwa_ref, cwg_ref, cba_ref, cbg_ref, ha_ref, hg_ref,
                   act_ref, sta_ref, stg_ref, scr_a, scr_g, *, shift, tile, tiles_per_seq):
    i = pl.program_id(1)
    hist = (FFN_K - 1) * shift
    base = -(-hist // SUBLANE) * SUBLANE

    @pl.when(i % tiles_per_seq == 0)
    def _():
        scr_a[base - hist:base, :] = ha_ref[0]
        scr_g[base - hist:base, :] = hg_ref[0]

    h = h_ref[...]
    scr_a[base:base + tile, :] = _dot(h, wa_ref[...])
    scr_g[base:base + tile, :] = _dot(h, wg_ref[...])

    def conv(scr, cw_ref, cb_ref):
        y = cb_ref[...]
        for j in range(FFN_K):
            lo = base - (FFN_K - 1 - j) * shift
            y = y + scr[lo:lo + tile, :] * cw_ref[j:j + 1, :]
        return y

    a = conv(scr_a, cwa_ref, cba_ref)
    g = conv(scr_g, cwg_ref, cbg_ref)
    act_ref[...] = (_silu(g) * a).astype(BF16)
    last_a = scr_a[base + tile - hist:base + tile, :]
    last_g = scr_g[base + tile - hist:base + tile, :]
    sta_ref[0] = last_a
    stg_ref[0] = last_g
    scr_a[base - hist:base, :] = last_a
    scr_g[base - hist:base, :] = last_g


def _ffn_up(h, w_up, conv_w, conv_b, hist0, grp):
    d = h.shape[1]
    tn = 512
    nj = D_FF // tn
    shift = grp.conv_shift
    hist = (FFN_K - 1) * shift
    base = -(-hist // SUBLANE) * SUBLANE
    n_seq = grp.n_tiles // grp.tiles_per_seq
    tps = grp.tiles_per_seq
    cb = conv_b.reshape(1, 2 * D_FF)
    kern = functools.partial(_ffn_up_kernel, shift=shift, tile=grp.tile, tiles_per_seq=tps)
    act, st_a, st_g = pl.pallas_call(
        kern,
        grid=(nj, grp.n_tiles),
        in_specs=[pl.BlockSpec((grp.tile, d), lambda j, i: (i, 0)),
                  pl.BlockSpec((d, tn), lambda j, i: (0, j)),
                  pl.BlockSpec((d, tn), lambda j, i: (0, nj + j)),
                  pl.BlockSpec((FFN_K, tn), lambda j, i: (0, j)),
                  pl.BlockSpec((FFN_K, tn), lambda j, i: (0, nj + j)),
                  pl.BlockSpec((1, tn), lambda j, i: (0, j)),
                  pl.BlockSpec((1, tn), lambda j, i: (0, nj + j)),
                  pl.BlockSpec((1, hist, tn), lambda j, i: (i // tps, 0, j)),
                  pl.BlockSpec((1, hist, tn), lambda j, i: (i // tps, 0, nj + j))],
        out_specs=[pl.BlockSpec((grp.tile, tn), lambda j, i: (i, j)),
                   pl.BlockSpec((1, hist, tn), lambda j, i: (i // tps, 0, j)),
                   pl.BlockSpec((1, hist, tn), lambda j, i: (i // tps, 0, j))],
        out_shape=[jax.ShapeDtypeStruct((h.shape[0], D_FF), BF16),
                   jax.ShapeDtypeStruct((n_seq, hist, D_FF), F32),
                   jax.ShapeDtypeStruct((n_seq, hist, D_FF), F32)],
        scratch_shapes=[pltpu.VMEM((base + grp.tile, tn), F32), pltpu.VMEM((base + grp.tile, tn), F32)],
        compiler_params=_cparams(2), name="ffn_up",
    )(h, w_up, w_up, conv_w, conv_w, cb, cb, hist0, hist0)
    return act, jnp.concatenate([st_a, st_g], axis=-1)


def _glu_kernel(y_ref, w_ref, b_ref, o_ref):
    z5 = _gelu_tanh(y_ref[...])
    o_ref[...] = z5 * _sigmoid(_dot(z5.astype(BF16), w_ref[...]) + b_ref[...])


def _s5_glu(yd, w, b, grp):
    n = yd.shape[1]
    return pl.pallas_call(
        _glu_kernel,
        grid=(grp.n_tiles,),
        in_specs=[pl.BlockSpec((grp.tile, n), lambda i: (i, 0)),
                  pl.BlockSpec((n, n), lambda i: (0, 0)),
                  pl.BlockSpec((1, n), lambda i: (0, 0))],
        out_specs=pl.BlockSpec((grp.tile, n), lambda i: (i, 0)),
        out_shape=jax.ShapeDtypeStruct(yd.shape, F32),
        compiler_params=_cparams(1), name="s5_glu",
    )(yd, w, b.reshape(1, n))


def _causal_conv_chunk(x, cv_scr, cw_ref, cb_ref, chunk, valid):
    base = SUBLANE
    cv_scr[base:base + chunk, :] = x
    y = cb_ref[...]
    for j in range(CONV_K):
        lo = base - (CONV_K - 1) + j
        y = y + cv_scr[lo:lo + chunk, :] * cw_ref[j:j + 1, :]
    last = cv_scr[base + valid - (CONV_K - 1):base + valid, :]
    cv_scr[base - (CONV_K - 1):base, :] = last
    return y


def _gla_kernel(q_ref, k_ref, v_ref, r_ref, sm_ref, w2_ref, b2_ref, ng_ref, s0_ref,
                o_ref, sout_ref, s_scr, *, chunk, block, valid):
    blk = pl.program_id(1)

    @pl.when(blk == 0)
    def _():
        s_scr[...] = s0_ref[0]

    row = lax.broadcasted_iota(jnp.int32, (chunk, 1), 0)
    eye = (lax.broadcasted_iota(jnp.int32, (GLA_DK, GLA_DK), 0)
           == lax.broadcasted_iota(jnp.int32, (GLA_DK, GLA_DK), 1))
    n_valid = min(valid, chunk)

    def do_chunk(s, carry):
        rows = pl.ds(pl.multiple_of(s * chunk, chunk), chunk)
        x = _dot(sm_ref[0, rows, :], w2_ref[...], HIGHEST) + b2_ref[...]
        log_a = (jnp.minimum(x, 0.0) - jnp.log(1.0 + jnp.exp(-jnp.abs(x)))) * (1.0 / GLA_GATE_NORM)
        if valid < chunk:
            log_a = jnp.where(row < valid, log_a, 0.0)
        b = jnp.zeros_like(log_a)
        for j in range(n_valid):
            b = b + jnp.where(row >= j, log_a[j:j + 1, :], 0.0)
        q = q_ref[0, rows, :] * GLA_DK ** -0.5
        k = k_ref[0, rows, :]
        v = v_ref[0, rows, :]
        r = r_ref[0, rows, :]
        for h in range(GLA_H):
            ks = slice(h * GLA_DK, (h + 1) * GLA_DK)
            vs = slice(h * GLA_DV, (h + 1) * GLA_DV)
            bh, qh, kh, vh = b[:, ks], q[:, ks], k[:, ks], v[:, vs]
            st = s_scr[h]
            b_last = bh[chunk - 1:chunk, :]
            o = _dot(qh * jnp.exp(bh), st)
            for j in range(n_valid):
                e = jnp.exp(jnp.minimum(bh - bh[j:j + 1, :], 0.0))
                sj = jnp.sum(qh * e * kh[j:j + 1, :], axis=-1, keepdims=True)
                o = o + jnp.where(row >= j, sj, 0.0) * vh[j:j + 1, :]
            k_out = kh * jnp.exp(b_last - bh)
            d_col = jnp.sum(jnp.where(eye, jnp.exp(b_last), 0.0), axis=-1, keepdims=True)
            s_scr[h] = st * d_col + _dot_tn(k_out, vh)
            o_ref[0, rows, vs] = _head_rms(o, ng_ref[...]) * _silu(r[:, vs])
        return carry

    lax.fori_loop(0, block // chunk, do_chunk, 0)

    @pl.when(blk == pl.num_programs(1) - 1)
    def _():
        sout_ref[0] = s_scr[...]


def _gla(proj, w2p, b2, ng, s0, chunk, block, valid):
    nb, seq, _ = proj.shape

    def col(width, off):
        return pl.BlockSpec((1, block, width), lambda b, i: (b, i, off // width))

    st_spec = pl.BlockSpec((1, GLA_H, GLA_DK, GLA_DV), lambda b, i: (b, 0, 0, 0))
    kern = functools.partial(_gla_kernel, chunk=chunk, block=block, valid=valid)
    return pl.pallas_call(
        kern,
        grid=(nb, seq // block),
        in_specs=[col(GLA_QK, AB_Q), col(GLA_QK, AB_K), col(GLA_V, AB_V), col(GLA_V, AB_R), col(LANE, AB_SMALL),
                  pl.BlockSpec((LANE, GLA_QK), lambda b, i: (0, 0)),
                  pl.BlockSpec((1, GLA_QK), lambda b, i: (0, 0)),
                  pl.BlockSpec((1, GLA_DV), lambda b, i: (0, 0)),
                  st_spec],
        out_specs=[pl.BlockSpec((1, block, GLA_V), lambda b, i: (b, i, 0)), st_spec],
        out_shape=[jax.ShapeDtypeStruct((nb, seq, GLA_V), F32),
                   jax.ShapeDtypeStruct((nb, GLA_H, GLA_DK, GLA_DV), F32)],
        scratch_shapes=[pltpu.VMEM((GLA_H, GLA_DK, GLA_DV), F32)],
        compiler_params=_cparams(2), name="gla",
    )(proj, proj, proj, proj, proj, w2p, b2.reshape(1, GLA_QK), ng.reshape(1, GLA_DV), s0)


def _inv_unit_lower(a, n, eye):
    p = eye - a
    a2 = a
    k = 2
    while k < n:
        a2 = _dot(a2, a2, HIGHEST)
        p = p + _dot(p, a2, HIGHEST)
        k *= 2
    return p


def _gdn_kernel(qkv_ref, sm_ref, gb_ref, cw_ref, cb_ref, alog_ref, dtb_ref, ng_ref, s0_ref, c0_ref,
                o_ref, sout_ref, cout_ref, s_scr, cv_scr, *, chunk, block, valid):
    blk = pl.program_id(1)

    @pl.when(blk == 0)
    def _():
        s_scr[...] = s0_ref[0]
        cv_scr[SUBLANE - (CONV_K - 1):SUBLANE, :] = c0_ref[0]

    ri = lax.broadcasted_iota(jnp.int32, (chunk, chunk), 0)
    ci = lax.broadcasted_iota(jnp.int32, (chunk, chunk), 1)
    causal = ri >= ci
    strict = ri > ci
    eye = (ri == ci).astype(F32)
    tri = causal.astype(F32)
    tri_u = (ri <= ci).astype(F32)
    inv_blk = min(INV_BLOCK, chunk)
    same_blk = (ri // inv_blk) == (ci // inv_blk)
    row = lax.broadcasted_iota(jnp.int32, (chunk, 1), 0)
    n_valid = min(valid, chunk)

    def do_chunk(s, carry):
        rows = pl.ds(pl.multiple_of(s * chunk, chunk), chunk)
        act = _silu(_causal_conv_chunk(qkv_ref[0, rows, :], cv_scr, cw_ref, cb_ref, chunk, n_valid))
        sm = sm_ref[0, rows, :]
        g_all = -jnp.exp(alog_ref[...]) * _softplus(sm + dtb_ref[...])
        beta_all = _sigmoid(sm)
        if valid < chunk:
            g_all = jnp.where(row < valid, g_all, 0.0)
            beta_all = jnp.where(row < valid, beta_all, 0.0)
        gc = _dot(tri, g_all, HIGHEST)
        gc_r = _dot_tn(g_all, tri_u, HIGHEST)
        gb = gb_ref[0, rows, :]
        for h in range(GDN_H):
            hs = slice(h * GDN_DK, (h + 1) * GDN_DK)
            qh = act[:, hs]
            kh = act[:, GDN_QK + h * GDN_DK:GDN_QK + (h + 1) * GDN_DK]
            vh = act[:, 2 * GDN_QK + h * GDN_DV:2 * GDN_QK + (h + 1) * GDN_DV]
            qh = qh * lax.rsqrt(jnp.sum(qh * qh, -1, keepdims=True) + EPS) * GDN_DK ** -0.5
            kh = kh * lax.rsqrt(jnp.sum(kh * kh, -1, keepdims=True) + EPS)
            beta = beta_all[:, AB_B_LANE + h:AB_B_LANE + h + 1]
            gcc = gc[:, AB_A_LANE + h:AB_A_LANE + h + 1]
            gcr = gc_r[AB_A_LANE + h:AB_A_LANE + h + 1, :]
            dec = jnp.exp(jnp.where(causal, gcc - gcr, -jnp.inf))
            kb = kh * beta
            m = jnp.where(strict, _dot_nt(kb, kh) * dec, 0.0)
            m_diag = jnp.where(same_blk, m, 0.0)
            t = _inv_unit_lower(m_diag, inv_blk, eye)
            if chunk > inv_blk:
                n_off = _dot(t, m - m_diag, HIGHEST)
                t = _dot(_inv_unit_lower(n_off, chunk // inv_blk, eye), t, HIGHEST)
            rhs = jnp.concatenate([vh * beta, kb * jnp.exp(gcc)], axis=1)
            sol = _dot(t, rhs, HIGHEST)
            u, w = sol[:, :GDN_DV], sol[:, GDN_DV:]
            att = _dot_nt(qh, kh) * dec
            g_last = gcc[chunk - 1:chunk, :]
            st = s_scr[h]
            v_new = u - _dot(w, st)
            o = _dot(qh * jnp.exp(gcc), st) + _dot(att, v_new)
            s_scr[h] = st * jnp.exp(g_last) + _dot_tn(kh * jnp.exp(g_last - gcc), v_new)
            o_ref[0, rows, hs] = _head_rms(o, ng_ref[...]) * _silu(gb[:, hs])
        return carry

    lax.fori_loop(0, block // chunk, do_chunk, 0)

    @pl.when(blk == pl.num_programs(1) - 1)
    def _():
        sout_ref[0] = s_scr[...]
        cout_ref[0] = cv_scr[SUBLANE - (CONV_K - 1):SUBLANE, :]


def _gdn(proj, conv_w, conv_b, alog_row, dtb_row, ng, s0, c0, chunk, block, valid):
    nb, seq, _ = proj.shape

    def col(width, off):
        return pl.BlockSpec((1, block, width), lambda b, i: (b, i, off // width))

    def const(shape):
        return pl.BlockSpec(shape, lambda b, i: (0,) * len(shape))

    st_spec = pl.BlockSpec((1, GDN_H, GDN_DK, GDN_DV), lambda b, i: (b, 0, 0, 0))
    cv_spec = pl.BlockSpec((1, CONV_K - 1, GDN_CONV_W), lambda b, i: (b, 0, 0))
    kern = functools.partial(_gdn_kernel, chunk=chunk, block=block, valid=valid)
    return pl.pallas_call(
        kern,
        grid=(nb, seq // block),
        in_specs=[col(GDN_CONV_W, AB_QKV), col(LANE, AB_SMALL), col(GDN_V, AB_G),
                  const((CONV_K, GDN_CONV_W)), const((1, GDN_CONV_W)), const((1, LANE)), const((1, LANE)),
                  const((1, GDN_DV)), st_spec, cv_spec],
        out_specs=[pl.BlockSpec((1, block, GDN_V), lambda b, i: (b, i, 0)), st_spec, cv_spec],
        out_shape=[jax.ShapeDtypeStruct((nb, seq, GDN_V), F32),
                   jax.ShapeDtypeStruct((nb, GDN_H, GDN_DK, GDN_DV), F32),
                   jax.ShapeDtypeStruct((nb, CONV_K - 1, GDN_CONV_W), F32)],
        scratch_shapes=[pltpu.VMEM((GDN_H, GDN_DK, GDN_DV), F32),
                        pltpu.VMEM((SUBLANE + chunk, GDN_CONV_W), F32)],
        compiler_params=_cparams(2), name="gdn",
    )(proj, proj, proj, conv_w, conv_b.reshape(1, GDN_CONV_W), alog_row, dtb_row, ng.reshape(1, GDN_DV), s0, c0)


def _ssd_kernel(xbc_ref, sm_ref, z_ref, cw_ref, cb_ref, alog_ref, dtb_ref, drow_ref, ng_ref, s0_ref, c0_ref,
                o_ref, sout_ref, cout_ref, s_scr, cv_scr, *, chunk, block, valid):
    blk = pl.program_id(1)

    @pl.when(blk == 0)
    def _():
        s_scr[...] = s0_ref[0]
        cv_scr[SUBLANE - (CONV_K - 1):SUBLANE, :] = c0_ref[0]

    ri = lax.broadcasted_iota(jnp.int32, (chunk, chunk), 0)
    ci = lax.broadcasted_iota(jnp.int32, (chunk, chunk), 1)
    causal = ri >= ci
    eye = (ri == ci).astype(F32)
    tri = causal.astype(F32)
    tri_u = (ri <= ci).astype(F32)
    row = lax.broadcasted_iota(jnp.int32, (chunk, 1), 0)
    lane_lo = lax.broadcasted_iota(jnp.int32, (chunk, LANE), 1) < SSD_P
    row_lo = lax.broadcasted_iota(jnp.int32, (2 * SSD_P, 1), 0) < SSD_P
    n_valid = min(valid, chunk)
    heads_per_group = SSD_H // SSD_G
    gsz = SSD_W // SSD_G

    def do_chunk(s, carry):
        rows = pl.ds(pl.multiple_of(s * chunk, chunk), chunk)
        act = _silu(_causal_conv_chunk(xbc_ref[0, rows, :], cv_scr, cw_ref, cb_ref, chunk, n_valid))
        dt = _softplus(sm_ref[0, rows, :] + dtb_ref[...])
        if valid < chunk:
            dt = jnp.where(row < valid, dt, 0.0)
        dta = dt * (-jnp.exp(alog_ref[...]))
        acs = _dot(tri, dta, HIGHEST)
        acs_r = _dot_tn(dta, tri_u, HIGHEST)
        dt_r = _dot_tn(dt, eye, HIGHEST)
        z = z_ref[0, rows, :]
        for g in range(SSD_G):
            bg = act[:, SSD_W + g * SSD_N:SSD_W + (g + 1) * SSD_N]
            cg = act[:, SSD_W + SSD_G * SSD_N + g * SSD_N:SSD_W + SSD_G * SSD_N + (g + 1) * SSD_N]
            cb = _dot_nt(cg, bg)
            parts = []
            for pr in range(heads_per_group // 2):
                pi = g * (heads_per_group // 2) + pr
                xp = act[:, pi * LANE:(pi + 1) * LANE]
                st = s_scr[pi]
                y_in, e_in, w_out, d_last = [], [], [], []
                for hh in range(2):
                    h = 2 * pi + hh
                    ac = acs[:, h:h + 1]
                    dec = jnp.exp(jnp.where(causal, ac - acs_r[h:h + 1, :], -jnp.inf))
                    y_in.append(_dot(cb * dec * dt_r[h:h + 1, :], xp))
                    a_last = ac[chunk - 1:chunk, :]
                    e_in.append(jnp.exp(ac))
                    w_out.append(jnp.exp(a_last - ac) * dt[:, h:h + 1])
                    d_last.append(jnp.exp(a_last))
                y = jnp.where(lane_lo, y_in[0], y_in[1])
                y = y + _dot_nt(cg, st) * jnp.where(lane_lo, e_in[0], e_in[1])
                y = y + drow_ref[:, pi * LANE:(pi + 1) * LANE] * xp
                x_sc = xp * jnp.where(lane_lo, w_out[0], w_out[1])
                s_scr[pi] = st * jnp.where(row_lo, d_last[0], d_last[1]) + _dot_tn(x_sc, bg)
                parts.append(y)
            gs = slice(g * gsz, (g + 1) * gsz)
            yg = jnp.concatenate(parts, axis=1) * _silu(z[:, gs])
            o_ref[0, rows, gs] = _head_rms(yg, ng_ref[:, gs])
        return carry

    lax.fori_loop(0, block // chunk, do_chunk, 0)

    @pl.when(blk == pl.num_programs(1) - 1)
    def _():
        sout_ref[0] = s_scr[...]
        cout_ref[0] = cv_scr[SUBLANE - (CONV_K - 1):SUBLANE, :]


def _ssd(proj, conv_w, conv_b, alog_row, dtb_row, d_row, ng, s0, c0, chunk, block, valid):
    nb, seq, _ = proj.shape
    n_pairs = SSD_H // 2

    def col(width, off):
        return pl.BlockSpec((1, block, width), lambda b, i: (b, i, off // width))

    def const(shape):
        return pl.BlockSpec(shape, lambda b, i: (0,) * len(shape))

    st_spec = pl.BlockSpec((1, n_pairs, 2 * SSD_P, SSD_N), lambda b, i: (b, 0, 0, 0))
    cv_spec = pl.BlockSpec((1, CONV_K - 1, SSD_CONV_W), lambda b, i: (b, 0, 0))
    kern = functools.partial(_ssd_kernel, chunk=chunk, block=block, valid=valid)
    o, s_new, c_new = pl.pallas_call(
        kern,
        grid=(nb, seq // block),
        in_specs=[col(SSD_CONV_W, CD_XBC), col(LANE, CD_SMALL), col(SSD_W, CD_Z),
                  const((CONV_K, SSD_CONV_W)), const((1, SSD_CONV_W)), const((1, LANE)), const((1, LANE)),
                  const((1, SSD_W)), const((1, SSD_W)), st_spec, cv_spec],
        out_specs=[pl.BlockSpec((1, block, SSD_W), lambda b, i: (b, i, 0)), st_spec, cv_spec],
        out_shape=[jax.ShapeDtypeStruct((nb, seq, SSD_W), F32),
                   jax.ShapeDtypeStruct((nb, n_pairs, 2 * SSD_P, SSD_N), F32),
                   jax.ShapeDtypeStruct((nb, CONV_K - 1, SSD_CONV_W), F32)],
        scratch_shapes=[pltpu.VMEM((n_pairs, 2 * SSD_P, SSD_N), F32),
                        pltpu.VMEM((SUBLANE + chunk, SSD_CONV_W), F32)],
        compiler_params=_cparams(2), name="ssd",
    )(proj, proj, proj, conv_w, conv_b.reshape(1, SSD_CONV_W), alog_row, dtb_row, d_row,
      ng.reshape(1, SSD_W), s0.reshape(nb, n_pairs, 2 * SSD_P, SSD_N), c0)
    return o, s_new.reshape(nb, SSD_H, SSD_P, SSD_N), c_new


def _s5_prep_kernel(are_ref, aim_ref, ldt_ref, bre_ref, bim_ref, lbr_ref, lbi_ref, bbr_ref, bbi_ref):
    a_re, a_im = are_ref[...], aim_ref[...]
    dt = jnp.exp(ldt_ref[...])
    mag = jnp.exp(a_re * dt)
    lb_re, lb_im = mag * jnp.cos(a_im * dt), mag * jnp.sin(a_im * dt)
    nr, ni = lb_re - 1.0, lb_im
    den = a_re * a_re + a_im * a_im
    f_re = (nr * a_re + ni * a_im) / den
    f_im = (ni * a_re - nr * a_im) / den
    b_re, b_im = bre_ref[...], bim_ref[...]
    lbr_ref[...] = lb_re
    lbi_ref[...] = lb_im
    bbr_ref[...] = f_re * b_re - f_im * b_im
    bbi_ref[...] = f_re * b_im + f_im * b_re


def _s5_prep(a_re, a_im, log_dt, b_re, b_im):
    g3 = (S5_G, 1, S5_P)
    b3 = (S5_G, S5_GS, S5_P)
    return pl.pallas_call(
        _s5_prep_kernel,
        out_shape=[jax.ShapeDtypeStruct(g3, F32), jax.ShapeDtypeStruct(g3, F32),
                   jax.ShapeDtypeStruct(b3, F32), jax.ShapeDtypeStruct(b3, F32)],
        name="s5_prep",
    )(a_re.reshape(g3), a_im.reshape(g3), log_dt.reshape(S5_G, 1, 1),
      jnp.swapaxes(b_re, 1, 2), jnp.swapaxes(b_im, 1, 2))


def _block_diag(blocks):
    g, r, c = blocks.shape
    per = 8
    b = blocks.reshape(g // per, per, r, 1, c) * jnp.eye(per, dtype=blocks.dtype).reshape(1, per, 1, per, 1)
    return b.reshape(g // per, per * r, per * c)


def _s5_kernel(u_ref, wre_ref, wim_ref, cre_ref, cim_ref, lbr_ref, lbi_ref, d_ref, x0r_ref, x0i_ref,
               y_ref, xfr_ref, xfi_ref, xr_scr, xi_scr, sr_scr, si_scr, *, rows_per_step, steps):
    c = pl.program_id(2)

    @pl.when(c == 0)
    def _():
        xr_scr[...] = x0r_ref[0]
        xi_scr[...] = x0i_ref[0]

    u = u_ref[0]
    sr_scr[...] = _dot(u, wre_ref[0])
    si_scr[...] = _dot(u, wim_ref[0])
    l_re, l_im = lbr_ref[...], lbi_ref[...]

    def step(t, carry):
        rows = pl.ds(pl.multiple_of(t * rows_per_step, rows_per_step), rows_per_step)
        xr, xi = xr_scr[...], xi_scr[...]
        nr = l_re * xr - l_im * xi + sr_scr[rows, :]
        ni = l_re * xi + l_im * xr + si_scr[rows, :]
        xr_scr[...] = nr
        xi_scr[...] = ni
        sr_scr[rows, :] = nr
        si_scr[rows, :] = ni
        return carry

    lax.fori_loop(0, steps, step, 0)
    y_ref[0] = _dot(sr_scr[...], cre_ref[0]) - _dot(si_scr[...], cim_ref[0]) + d_ref[...] * u

    @pl.when(c == pl.num_programs(2) - 1)
    def _():
        xfr_ref[0] = xr_scr[...]
        xfi_ref[0] = xi_scr[...]


def _s5(proj, w_re, w_im, c_re, c_im, lb_re, lb_im, d, x0_re, x0_im, rows_per_step, steps):
    ng, n_tok, _ = proj.shape
    nj = S5_W // LANE
    sw = S5_STATE // nj
    cr = rows_per_step * steps
    kern = functools.partial(_s5_kernel, rows_per_step=rows_per_step, steps=steps)
    x_spec = pl.BlockSpec((1, rows_per_step, sw), lambda g, j, c: (g, 0, j))
    return pl.pallas_call(
        kern,
        grid=(ng, nj, n_tok // cr),
        in_specs=[pl.BlockSpec((1, cr, LANE), lambda g, j, c: (g, c, CD_U // LANE + j)),
                  pl.BlockSpec((1, LANE, sw), lambda g, j, c: (j, 0, 0)),
                  pl.BlockSpec((1, LANE, sw), lambda g, j, c: (j, 0, 0)),
                  pl.BlockSpec((1, sw, LANE), lambda g, j, c: (j, 0, 0)),
                  pl.BlockSpec((1, sw, LANE), lambda g, j, c: (j, 0, 0)),
                  pl.BlockSpec((1, sw), lambda g, j, c: (0, j)),
                  pl.BlockSpec((1, sw), lambda g, j, c: (0, j)),
                  pl.BlockSpec((1, LANE), lambda g, j, c: (0, j)),
                  x_spec, x_spec],
        out_specs=[pl.BlockSpec((1, cr, LANE), lambda g, j, c: (g, c, j)), x_spec, x_spec],
        out_shape=[jax.ShapeDtypeStruct((ng, n_tok, S5_W), F32),
                   jax.ShapeDtypeStruct((ng, rows_per_step, S5_STATE), F32),
                   jax.ShapeDtypeStruct((ng, rows_per_step, S5_STATE), F32)],
        scratch_shapes=[pltpu.VMEM((rows_per_step, sw), F32), pltpu.VMEM((rows_per_step, sw), F32),
                        pltpu.VMEM((cr, sw), F32), pltpu.VMEM((cr, sw), F32)],
        compiler_params=_cparams(3), name="s5_scan",
    )(proj, w_re, w_im, c_re, c_im, lb_re, lb_im, d.reshape(1, S5_W), x0_re, x0_im)


def _lane_row(vec, lane0):
    return jnp.zeros((1, LANE), F32).at[0, lane0:lane0 + vec.shape[0]].set(vec.astype(F32))


def _prep_params(p):
    d = D_MODEL
    q = {}
    w = p['w_in_ab'][0]
    o_lr = 2 * GLA_QK + GLA_V
    o_r = o_lr + GLA_LR
    o_qkv = o_r + GLA_V
    o_a = o_qkv + GDN_CONV_W
    o_g = o_a + 2 * GDN_H
    q['w_in_ab'] = jnp.concatenate(
        [w[:, :o_lr], w[:, o_r:o_qkv], w[:, o_qkv:o_a], w[:, o_g:], w[:, o_lr:o_r], w[:, o_a:o_g],
         jnp.zeros((d, LANE - GLA_LR - 2 * GDN_H), F32)], axis=1).astype(BF16)
    q['gla_w2'] = jnp.zeros((LANE, GLA_QK), F32).at[:GLA_LR].set(p['gla_w2'][0])
    q['gdn_alog'] = _lane_row(p['gdn_A_log'][0], AB_A_LANE)
    q['gdn_dtb'] = _lane_row(p['gdn_dt_bias'][0], AB_A_LANE)
    w_out = p['w_out_ab'][0].astype(BF16)
    q['w_out_a'], q['w_out_b'] = w_out[:GLA_V], w_out[GLA_V:]
    w = p['w_in_cd'][0]
    o_xbc = SSD_W
    o_dt = o_xbc + SSD_CONV_W
    o_u = o_dt + SSD_H
    q['w_in_cd'] = jnp.concatenate(
        [w[:, o_xbc:o_dt], w[:, o_dt:o_u], jnp.zeros((d, CD_Z - CD_SMALL - SSD_H), F32), w[:, :o_xbc], w[:, o_u:]],
        axis=1).astype(BF16)
    q['ssd_alog'] = _lane_row(p['ssd_A_log'][0], 0)
    q['ssd_dtb'] = _lane_row(p['ssd_dt_bias'][0], 0)
    q['ssd_d_row'] = jnp.repeat(p['ssd_D'][0].astype(F32), SSD_P).reshape(1, SSD_W)
    w_out = p['w_out_cd'][0].astype(BF16)
    q['w_out_c'], q['w_out_d'] = w_out[:SSD_W], w_out[SSD_W:]
    lb_re, lb_im, bb_re, bb_im = _s5_prep(p['s5_A_re'][0], p['s5_A_im'][0], p['s5_log_dt'][0],
                                          p['s5_B_re'][0], p['s5_B_im'][0])
    q['s5_lb_re'], q['s5_lb_im'] = lb_re.reshape(1, S5_STATE), lb_im.reshape(1, S5_STATE)
    q['s5_w_re'], q['s5_w_im'] = _block_diag(bb_re), _block_diag(bb_im)
    q['s5_c_re'] = _block_diag(jnp.swapaxes(p['s5_C_re'][0], 1, 2))
    q['s5_c_im'] = _block_diag(jnp.swapaxes(p['s5_C_im'][0], 1, 2))
    q['s5_glu_w'] = p['s5_glu_w'][0].astype(BF16)
    q['w_ffn_up'] = p['w_ffn_up'].astype(BF16)
    q['w_ffn_down'] = p['w_ffn_down'].astype(BF16)
    return q


def _trunk(x, mods, grp, seq_shape, state, p, q):
    nb, seq_len, valid = seq_shape
    s_gla, s_gdn, s_gdnc, s_ssd, s_ssdc, s_re, s_im, s_ffn = state
    prompt = not grp.per_token_mod

    def to_seq(t):
        if prompt:
            return t.reshape(nb, seq_len, t.shape[-1])
        t = jnp.swapaxes(t.reshape(valid, nb, t.shape[-1]), 0, 1)
        return jnp.pad(t, ((0, 0), (0, seq_len - valid), (0, 0)))

    def from_seq(t):
        if prompt:
            return t.reshape(nb * seq_len, t.shape[-1])
        return jnp.swapaxes(t[:, :valid], 0, 1).reshape(valid * nb, t.shape[-1])

    blk = MIX_BLOCK if prompt else seq_len
    chunks = (GLA_CHUNK, GDN_CHUNK, SSD_CHUNK) if prompt else (seq_len,) * 3
    new = {}

    h = _norm_mod(x, p['g_mix'][0], mods[0], grp, 1, 0)
    proj = to_seq(_matmul(h, q['w_in_ab'], grp, AB_N // 3))
    o_a, new['gla'] = _gla(proj, q['gla_w2'], p['gla_b2'][0], p['gla_norm_g'][0], s_gla, chunks[0], blk, valid)
    o_b, new['gdn'], new['gdnc'] = _gdn(proj, p['gdn_conv_w'][0], p['gdn_conv_b'][0], q['gdn_alog'], q['gdn_dtb'],
                                        p['gdn_norm_g'][0], s_gdn, s_gdnc, chunks[1], blk, valid)
    x = _mm_residual([(from_seq(o_a), q['w_out_a']), (from_seq(o_b), q['w_out_b'])], x, mods[0], grp, 2, 1024)
    h = _norm_mod(x, p['g_ffn'][0], mods[0], grp, 4, 3)
    act, new['ffn0'] = _ffn_up(h, q['w_ffn_up'][0], p['ffn_conv_w'][0], p['ffn_conv_b'][0], s_ffn[0], grp)
    x = _mm_residual([(act, q['w_ffn_down'][0])], x, mods[0], grp, 5, 1024)

    h = _norm_mod(x, p['g_mix'][1], mods[1], grp, 1, 0)
    proj2 = _matmul(h, q['w_in_cd'], grp, CD_N // 2)
    proj = to_seq(proj2)
    o_c, new['ssd'], new['ssdc'] = _ssd(proj, p['ssd_conv_w'][0], p['ssd_conv_b'][0], q['ssd_alog'], q['ssd_dtb'],
                                        q['ssd_d_row'], p['ssd_norm_g'][0], s_ssd, s_ssdc, chunks[2], blk, valid)
    if prompt:
        s5_in, rows_per_step, steps = proj, 1, S5_STEPS
    else:
        s5_in, rows_per_step, steps = proj2.reshape(1, grp.n_tok, CD_N), nb, valid
    yd, new['re'], new['im'] = _s5(s5_in, q['s5_w_re'], q['s5_w_im'], q['s5_c_re'], q['s5_c_im'],
                                   q['s5_lb_re'], q['s5_lb_im'], p['s5_D'][0], s_re, s_im, rows_per_step, steps)
    o_d = _s5_glu(yd.reshape(grp.n_tok, S5_W), q['s5_glu_w'], p['s5_glu_b'][0], grp)
    x = _mm_residual([(from_seq(o_c), q['w_out_c']), (o_d, q['w_out_d'])], x, mods[1], grp, 2, 1024)
    h = _norm_mod(x, p['g_ffn'][1], mods[1], grp, 4, 3)
    act, new['ffn1'] = _ffn_up(h, q['w_ffn_up'][1], p['ffn_conv_w'][1], p['ffn_conv_b'][1], s_ffn[1], grp)
    x = _mm_residual([(act, q['w_ffn_down'][1])], x, mods[1], grp, 5, 1024)
    return _final_rms(x, p['g_final'], grp), new


def kernel(x_prompt, x_sample, c_prompt, c_sample, state_gla, state_gdn, state_gdn_conv, state_ssd, state_ssd_conv, state_s5_re, state_s5_im, state_ffn_conv, w_ada, b_ada, g_mix, g_ffn, w_in_ab, gla_w2, gla_b2, gla_norm_g, gdn_conv_w, gdn_conv_b, gdn_A_log, gdn_dt_bias, gdn_norm_g, w_out_ab, w_in_cd, ssd_conv_w, ssd_conv_b, ssd_A_log, ssd_dt_bias, ssd_D, ssd_norm_g, s5_A_re, s5_A_im, s5_B_re, s5_B_im, s5_C_re, s5_C_im, s5_D, s5_log_dt, s5_glu_w, s5_glu_b, w_out_cd, w_ffn_up, ffn_conv_w, ffn_conv_b, w_ffn_down, g_final):
    p = dict(g_mix=g_mix, g_ffn=g_ffn, w_in_ab=w_in_ab, gla_w2=gla_w2, gla_b2=gla_b2, gla_norm_g=gla_norm_g,
             gdn_conv_w=gdn_conv_w, gdn_conv_b=gdn_conv_b, gdn_A_log=gdn_A_log, gdn_dt_bias=gdn_dt_bias,
             gdn_norm_g=gdn_norm_g, w_out_ab=w_out_ab, w_in_cd=w_in_cd, ssd_conv_w=ssd_conv_w,
             ssd_conv_b=ssd_conv_b, ssd_A_log=ssd_A_log, ssd_dt_bias=ssd_dt_bias, ssd_D=ssd_D,
             ssd_norm_g=ssd_norm_g, s5_A_re=s5_A_re, s5_A_im=s5_A_im, s5_B_re=s5_B_re, s5_B_im=s5_B_im,
             s5_C_re=s5_C_re, s5_C_im=s5_C_im, s5_D=s5_D, s5_log_dt=s5_log_dt, s5_glu_w=s5_glu_w,
             s5_glu_b=s5_glu_b, w_out_cd=w_out_cd, w_ffn_up=w_ffn_up, ffn_conv_w=ffn_conv_w,
             ffn_conv_b=ffn_conv_b, w_ffn_down=w_ffn_down, g_final=g_final)
    bp, lp, d = x_prompt.shape
    bs, ls, _ = x_sample.shape
    q = _prep_params(p)

    bp_pad = -(-bp // SUBLANE) * SUBLANE
    c_all = jnp.concatenate([c_prompt, jnp.zeros((bp_pad - bp, d), F32), c_sample], axis=0)
    mod = _ada_mod(c_all, w_ada, b_ada)
    depth = w_ada.shape[0]
    mods_p = [mod[l, :bp].reshape(bp, 1, 6 * d) for l in range(depth)]
    mods_s = [jnp.tile(mod[l, bp_pad:], (ls, 1)).reshape(1, ls * bs, 6 * d) for l in range(depth)]

    tile_p = 512
    grp_p = _Group(bp * lp, tile_p, False, lp // tile_p, 1)
    zeros = lambda *shape: jnp.zeros(shape, F32)
    state_p = (zeros(bp, GLA_H, GLA_DK, GLA_DV), zeros(bp, GDN_H, GDN_DK, GDN_DV),
               zeros(bp, CONV_K - 1, GDN_CONV_W), zeros(bp, SSD_H, SSD_P, SSD_N),
               zeros(bp, CONV_K - 1, SSD_CONV_W), zeros(bp, 1, S5_STATE), zeros(bp, 1, S5_STATE),
               zeros(depth, bp, FFN_K - 1, 2 * D_FF))
    y_p, new_p = _trunk(x_prompt.reshape(bp * lp, d), mods_p, grp_p, (bp, lp, lp), state_p, p, q)

    grp_s = _Group(bs * ls, bs * ls, True, 1, bs)
    ffn_hist_s = jnp.swapaxes(state_ffn_conv, 1, 2).reshape(depth, 1, (FFN_K - 1) * bs, 2 * D_FF)
    state_s = (state_gla[0], state_gdn[0], state_gdn_conv[0], state_ssd[0], state_ssd_conv[0],
               state_s5_re.reshape(1, bs, S5_STATE), state_s5_im.reshape(1, bs, S5_STATE), ffn_hist_s)
    x_s = jnp.swapaxes(x_sample, 0, 1).reshape(ls * bs, d)
    y_s, new_s = _trunk(x_s, mods_s, grp_s, (bs, SAMPLE_PAD, ls), state_s, p, q)
    y_s = jnp.swapaxes(y_s.reshape(ls, bs, d), 0, 1)

    ffn_p = jnp.stack([new_p['ffn0'], new_p['ffn1']])
    ffn_s = jnp.stack([jnp.swapaxes(new_s[k].reshape(FFN_K - 1, bs, 2 * D_FF), 0, 1) for k in ('ffn0', 'ffn1')])
    s5_shape = lambda t, nb: t.reshape(1, nb, S5_G, S5_P)
    return (y_p.reshape(bp, lp, d), y_s,
            new_p['gla'][None], new_s['gla'][None], new_p['gdn'][None], new_s['gdn'][None],
            new_p['gdnc'][None], new_s['gdnc'][None], new_p['ssd'][None], new_s['ssd'][None],
            new_p['ssdc'][None], new_s['ssdc'][None],
            s5_shape(new_p['re'], bp), s5_shape(new_s['re'], bs), s5_shape(new_p['im'], bp), s5_shape(new_s['im'], bs),
            ffn_p, ffn_s)
```

```python
import functools
import math

import jax
import jax.numpy as jnp
from jax import lax
from jax.experimental import pallas as pl
from jax.experimental.pallas import tpu as pltpu

F32 = jnp.float32
BF16 = jnp.bfloat16
HIGHEST = lax.Precision.HIGHEST
EPS = 1e-6

D_MODEL = 2048
GLA_H, GLA_DK, GLA_DV, GLA_LR = 4, 128, 256, 16
GLA_GATE_NORM = 16.0
GLA_QK, GLA_V = GLA_H * GLA_DK, GLA_H * GLA_DV
GDN_H, GDN_DK, GDN_DV = 8, 128, 128
GDN_QK, GDN_V = GDN_H * GDN_DK, GDN_H * GDN_DV
CONV_K = 4
GDN_CONV_W = 2 * GDN_QK + GDN_V
SSD_P, SSD_H, SSD_G, SSD_N = 64, 16, 2, 128
SSD_W = SSD_H * SSD_P
SSD_CONV_W = SSD_W + 2 * SSD_G * SSD_N
S5_W, S5_GS, S5_G, S5_P = 1024, 16, 64, 64
S5_STATE = S5_G * S5_P
D_FF = 5632
FFN_K = 3

LANE = 128
SUBLANE = 8
VMEM_LIMIT = 48 * 1024 * 1024

AB_Q, AB_K, AB_V, AB_R, AB_QKV, AB_G, AB_SMALL, AB_N = 0, 512, 1024, 2048, 3072, 6144, 7168, 7296
AB_LR_LANE, AB_A_LANE, AB_B_LANE = 0, 16, 24
CD_XBC, CD_SMALL, CD_Z, CD_U, CD_N = 0, 1536, 2048, 3072, 4096

MIX_BLOCK = 256
GLA_CHUNK, GDN_CHUNK, SSD_CHUNK = 16, 128, 64
INV_BLOCK = 16
SAMPLE_PAD = 8


def _cparams(n_axes):
    return pltpu.CompilerParams(dimension_semantics=("arbitrary",) * n_axes, vmem_limit_bytes=VMEM_LIMIT)


def _sigmoid(x):
    return 1.0 / (1.0 + jnp.exp(-x))


def _silu(x):
    return x * _sigmoid(x)


def _softplus(x):
    return jnp.maximum(x, 0.0) + jnp.log(1.0 + jnp.exp(-jnp.abs(x)))


def _gelu_tanh(x):
    return 0.5 * x * (1.0 + jnp.tanh(math.sqrt(2.0 / math.pi) * (x + 0.044715 * (x * x * x))))


def _dot(a, b, precision=None):
    return jnp.dot(a, b, precision=precision, preferred_element_type=F32)


def _dot_nt(a, b, precision=None):
    return lax.dot_general(a, b, (((1,), (1,)), ((), ())), precision=precision, preferred_element_type=F32)


def _dot_tn(a, b, precision=None):
    return lax.dot_general(a, b, (((0,), (0,)), ((), ())), precision=precision, preferred_element_type=F32)


def _head_rms(o, g):
    return o * lax.rsqrt(jnp.mean(o * o, -1, keepdims=True) + EPS) * g


class _Group:
    def __init__(self, n_tok, tile, per_token_mod, tiles_per_seq, conv_shift):
        self.n_tok = n_tok
        self.tile = tile
        self.n_tiles = n_tok // tile
        self.per_token_mod = per_token_mod
        self.tiles_per_seq = tiles_per_seq
        self.conv_shift = conv_shift

    def mod_spec(self, width, col_block, m_axis):
        if self.per_token_mod:
            return pl.BlockSpec((1, self.tile, width), lambda *g: (0, g[m_axis], col_block(*g)))
        tps = self.tiles_per_seq
        return pl.BlockSpec((1, 1, width), lambda *g: (g[m_axis] // tps, 0, col_block(*g)))


def _ada_kernel(c_ref, w_ref, b_ref, o_ref):
    cs = _silu(c_ref[...]).astype(BF16)
    o_ref[0] = _dot(cs, w_ref[0].astype(BF16)) + b_ref[0]


def _ada_mod(c, w_ada, b_ada):
    depth, d, n = w_ada.shape
    rows = c.shape[0]
    tn = 1024
    return pl.pallas_call(
        _ada_kernel,
        grid=(depth, n // tn),
        in_specs=[pl.BlockSpec((rows, d), lambda l, j: (0, 0)),
                  pl.BlockSpec((1, d, tn), lambda l, j: (l, 0, j)),
                  pl.BlockSpec((1, 1, tn), lambda l, j: (l, 0, j))],
        out_specs=pl.BlockSpec((1, rows, tn), lambda l, j: (l, 0, j)),
        out_shape=jax.ShapeDtypeStruct((depth, rows, n), F32),
        compiler_params=_cparams(2), name="ada_mod",
    )(c, w_ada, b_ada.reshape(depth, 1, n))


def _norm_mod_kernel(x_ref, g_ref, sc_ref, sh_ref, o_ref):
    x = x_ref[...]
    y = x * lax.rsqrt(jnp.mean(x * x, -1, keepdims=True) + EPS) * g_ref[...]
    o_ref[...] = (y * (1.0 + sc_ref[0]) + sh_ref[0]).astype(BF16)


def _norm_mod(x, g, mod, grp, sc_blk, sh_blk):
    d = x.shape[1]
    return pl.pallas_call(
        _norm_mod_kernel,
        grid=(grp.n_tiles,),
        in_specs=[pl.BlockSpec((grp.tile, d), lambda i: (i, 0)),
                  pl.BlockSpec((1, d), lambda i: (0, 0)),
                  grp.mod_spec(d, lambda i: sc_blk, 0),
                  grp.mod_spec(d, lambda i: sh_blk, 0)],
        out_specs=pl.BlockSpec((grp.tile, d), lambda i: (i, 0)),
        out_shape=jax.ShapeDtypeStruct(x.shape, BF16),
        compiler_params=_cparams(1), name="norm_mod",
    )(x, g.reshape(1, d), mod, mod)


def _rms_kernel(x_ref, g_ref, o_ref):
    x = x_ref[...]
    o_ref[...] = x * lax.rsqrt(jnp.mean(x * x, -1, keepdims=True) + EPS) * g_ref[...]


def _final_rms(x, g, grp):
    d = x.shape[1]
    return pl.pallas_call(
        _rms_kernel,
        grid=(grp.n_tiles,),
        in_specs=[pl.BlockSpec((grp.tile, d), lambda i: (i, 0)), pl.BlockSpec((1, d), lambda i: (0, 0))],
        out_specs=pl.BlockSpec((grp.tile, d), lambda i: (i, 0)),
        out_shape=jax.ShapeDtypeStruct(x.shape, F32),
        compiler_params=_cparams(1), name="final_rms",
    )(x, g.reshape(1, d))


def _matmul_kernel(a_ref, w_ref, o_ref):
    o_ref[...] = _dot(a_ref[...], w_ref[...])


def _matmul(a, w, grp, tn):
    k, n = w.shape
    return pl.pallas_call(
        _matmul_kernel,
        grid=(n // tn, grp.n_tiles),
        in_specs=[pl.BlockSpec((grp.tile, k), lambda j, i: (i, 0)),
                  pl.BlockSpec((k, tn), lambda j, i: (0, j))],
        out_specs=pl.BlockSpec((grp.tile, tn), lambda j, i: (i, j)),
        out_shape=jax.ShapeDtypeStruct((a.shape[0], n), F32),
        compiler_params=_cparams(2), name="matmul",
    )(a, w)


def _mm_res_kernel(*refs, n_pairs):
    x_ref, gate_ref, o_ref = refs[2 * n_pairs:]
    y = _dot(refs[0][...].astype(BF16), refs[1][...])
    for p in range(1, n_pairs):
        y = y + _dot(refs[2 * p][...].astype(BF16), refs[2 * p + 1][...])
    o_ref[...] = x_ref[...] + gate_ref[0] * y


def _mm_residual(pairs, x, mod, grp, gate_blk, tn):
    n = x.shape[1]
    in_specs, args = [], []
    for a, w in pairs:
        k = w.shape[0]
        in_specs += [pl.BlockSpec((grp.tile, k), lambda j, i: (i, 0)), pl.BlockSpec((k, tn), lambda j, i: (0, j))]
        args += [a, w]
    in_specs += [pl.BlockSpec((grp.tile, tn), lambda j, i: (i, j)),
                 grp.mod_spec(tn, lambda j, i: gate_blk * (n // tn) + j, 1)]
    return pl.pallas_call(
        functools.partial(_mm_res_kernel, n_pairs=len(pairs)),
        grid=(n // tn, grp.n_tiles),
        in_specs=in_specs,
        out_specs=pl.BlockSpec((grp.tile, tn), lambda j, i: (i, j)),
        out_shape=jax.ShapeDtypeStruct(x.shape, F32),
        compiler_params=_cparams(2), name="mm_residual",
    )(*args, x, mod)


def _ffn_up_kernel(h_ref, wa_ref, wg_ref, cwa_ref, cwg_ref, cba_ref, cbg_ref, ha_ref, hg_ref,
                   act_ref, sta_ref, stg_ref, scr_a, scr_g, *, shift, tile, tiles_per_seq):
    i = pl.program_id(1)
    hist = (FFN_K - 1) * shift
    base = -(-hist // SUBLANE) * SUBLANE

    @pl.when(i % tiles_per_seq == 0)
    def _():
        scr_a[base - hist:base, :] = ha_ref[0]
        scr_g[base - hist:base, :] = hg_ref[0]

    h = h_ref[...]
    scr_a[base:base + tile, :] = _dot(h, wa_ref[...])
    scr_g[base:base + tile, :] = _dot(h, wg_ref[...])

    def conv(scr, cw_ref, cb_ref):
        y = cb_ref[...]
        for j in range(FFN_K):
            lo = base - (FFN_K - 1 - j) * shift
            y = y + scr[lo:lo + tile, :] * cw_ref[j:j + 1, :]
        return y

    a = conv(scr_a, cwa_ref, cba_ref)
    g = conv(scr_g, cwg_ref, cbg_ref)
    act_ref[...] = (_silu(g) * a).astype(BF16)
    last_a = scr_a[base + tile - hist:base + tile, :]
    last_g = scr_g[base + tile - hist:base + tile, :]
    sta_ref[0] = last_a
    stg_ref[0] = last_g
    scr_a[base - hist:base, :] = last_a
    scr_g[base - hist:base, :] = last_g


def _ffn_up(h, w_up, conv_w, conv_b, hist0, grp):
    d = h.shape[1]
    tn = 512
    nj = D_FF // tn
    shift = grp.conv_shift
    hist = (FFN_K - 1) * shift
    base = -(-hist // SUBLANE) * SUBLANE
    n_seq = grp.n_tiles // grp.tiles_per_seq
    tps = grp.tiles_per_seq
    cb = conv_b.reshape(1, 2 * D_FF)
    kern = functools.partial(_ffn_up_kernel, shift=shift, tile=grp.tile, tiles_per_seq=tps)
    act, st_a, st_g = pl.pallas_call(
        kern,
        grid=(nj, grp.n_tiles),
        in_specs=[pl.BlockSpec((grp.tile, d), lambda j, i: (i, 0)),
                  pl.BlockSpec((d, tn), lambda j, i: (0, j)),
                  pl.BlockSpec((d, tn), lambda j, i: (0, nj + j)),
                  pl.BlockSpec((FFN_K, tn), lambda j, i: (0, j)),
                  pl.BlockSpec((FFN_K, tn), lambda j, i: (0, nj + j)),
                  pl.BlockSpec((1, tn), lambda j, i: (0, j)),
                  pl.BlockSpec((1, tn), lambda j, i: (0, nj + j)),
                  pl.BlockSpec((1, hist, tn), lambda j, i: (i // tps, 0, j)),
                  pl.BlockSpec((1, hist, tn), lambda j, i: (i // tps, 0, nj + j))],
        out_specs=[pl.BlockSpec((grp.tile, tn), lambda j, i: (i, j)),
                   pl.BlockSpec((1, hist, tn), lambda j, i: (i // tps, 0, j)),
                   pl.BlockSpec((1, hist, tn), lambda j, i: (i // tps, 0, j))],
        out_shape=[jax.ShapeDtypeStruct((h.shape[0], D_FF), BF16),
                   jax.ShapeDtypeStruct((n_seq, hist, D_FF), F32),
                   jax.ShapeDtypeStruct((n_seq, hist, D_FF), F32)],
        scratch_shapes=[pltpu.VMEM((base + grp.tile, tn), F32), pltpu.VMEM((base + grp.tile, tn), F32)],
        compiler_params=_cparams(2), name="ffn_up",
    )(h, w_up, w_up, conv_w, conv_w, cb, cb, hist0, hist0)
    return act, jnp.concatenate([st_a, st_g], axis=-1)


def _glu_kernel(y_ref, w_ref, b_ref, o_ref):
    z5 = _gelu_tanh(y_ref[...])
    o_ref[...] = z5 * _sigmoid(_dot(z5.astype(BF16), w_ref[...]) + b_ref[...])


def _s5_glu(yd, w, b, grp):
    n = yd.shape[1]
    return pl.pallas_call(
        _glu_kernel,
        grid=(grp.n_tiles,),
        in_specs=[pl.BlockSpec((grp.tile, n), lambda i: (i, 0)),
                  pl.BlockSpec((n, n), lambda i: (0, 0)),
                  pl.BlockSpec((1, n), lambda i: (0, 0))],
        out_specs=pl.BlockSpec((grp.tile, n), lambda i: (i, 0)),
        out_shape=jax.ShapeDtypeStruct(yd.shape, F32),
        compiler_params=_cparams(1), name="s5_glu",
    )(yd, w, b.reshape(1, n))


def _causal_conv_chunk(x, cv_scr, cw_ref, cb_ref, chunk, valid):
    base = SUBLANE
    cv_scr[base:base + chunk, :] = x
    y = cb_ref[...]
    for j in range(CONV_K):
        lo = base - (CONV_K - 1) + j
        y = y + cv_scr[lo:lo + chunk, :] * cw_ref[j:j + 1, :]
    last = cv_scr[base + valid - (CONV_K - 1):base + valid, :]
    cv_scr[base - (CONV_K - 1):base, :] = last
    return y


def _gla_kernel(q_ref, k_ref, v_ref, r_ref, sm_ref, w2_ref, b2_ref, ng_ref, s0_ref,
                o_ref, sout_ref, s_scr, *, chunk, block, valid):
    blk = pl.program_id(1)

    @pl.when(blk == 0)
    def _():
        s_scr[...] = s0_ref[0]

    row = lax.broadcasted_iota(jnp.int32, (chunk, 1), 0)
    eye = (lax.broadcasted_iota(jnp.int32, (GLA_DK, GLA_DK), 0)
           == lax.broadcasted_iota(jnp.int32, (GLA_DK, GLA_DK), 1))
    n_valid = min(valid, chunk)

    def do_chunk(s, carry):
        rows = pl.ds(pl.multiple_of(s * chunk, chunk), chunk)
        x = _dot(sm_ref[0, rows, :], w2_ref[...], HIGHEST) + b2_ref[...]
        log_a = (jnp.minimum(x, 0.0) - jnp.log(1.0 + jnp.exp(-jnp.abs(x)))) * (1.0 / GLA_GATE_NORM)
        if valid < chunk:
            log_a = jnp.where(row < valid, log_a, 0.0)
        b = jnp.zeros_like(log_a)
        for j in range(n_valid):
            b = b + jnp.where(row >= j, log_a[j:j + 1, :], 0.0)
        q = q_ref[0, rows, :] * GLA_DK ** -0.5
        k = k_ref[0, rows, :]
        v = v_ref[0, rows, :]
        r = r_ref[0, rows, :]
        for h in range(GLA_H):
            ks = slice(h * GLA_DK, (h + 1) * GLA_DK)
            vs = slice(h * GLA_DV, (h + 1) * GLA_DV)
            bh, qh, kh, vh = b[:, ks], q[:, ks], k[:, ks], v[:, vs]
            st = s_scr[h]
            b_last = bh[chunk - 1:chunk, :]
            o = _dot(qh * jnp.exp(bh), st)
            for j in range(n_valid):
                e = jnp.exp(jnp.minimum(bh - bh[j:j + 1, :], 0.0))
                sj = jnp.sum(qh * e * kh[j:j + 1, :], axis=-1, keepdims=True)
                o = o + jnp.where(row >= j, sj, 0.0) * vh[j:j + 1, :]
            k_out = kh * jnp.exp(b_last - bh)
            d_col = jnp.sum(jnp.where(eye, jnp.exp(b_last), 0.0), axis=-1, keepdims=True)
            s_scr[h] = st * d_col + _dot_tn(k_out, vh)
            o_ref[0, rows, vs] = _head_rms(o, ng_ref[...]) * _silu(r[:, vs])
        return carry

    lax.fori_loop(0, block // chunk, do_chunk, 0)

    @pl.when(blk == pl.num_programs(1) - 1)
    def _():
        sout_ref[0] = s_scr[...]


def _gla(proj, w2p, b2, ng, s0, chunk, block, valid):
    nb, seq, _ = proj.shape

    def col(width, off):
        return pl.BlockSpec((1, block, width), lambda b, i: (b, i, off // width))

    st_spec = pl.BlockSpec((1, GLA_H, GLA_DK, GLA_DV), lambda b, i: (b, 0, 0, 0))
    kern = functools.partial(_gla_kernel, chunk=chunk, block=block, valid=valid)
    return pl.pallas_call(
        kern,
        grid=(nb, seq // block),
        in_specs=[col(GLA_QK, AB_Q), col(GLA_QK, AB_K), col(GLA_V, AB_V), col(GLA_V, AB_R), col(LANE, AB_SMALL),
                  pl.BlockSpec((LANE, GLA_QK), lambda b, i: (0, 0)),
                  pl.BlockSpec((1, GLA_QK), lambda b, i: (0, 0)),
                  pl.BlockSpec((1, GLA_DV), lambda b, i: (0, 0)),
                  st_spec],
        out_specs=[pl.BlockSpec((1, block, GLA_V), lambda b, i: (b, i, 0)), st_spec],
        out_shape=[jax.ShapeDtypeStruct((nb, seq, GLA_V), F32),
                   jax.ShapeDtypeStruct((nb, GLA_H, GLA_DK, GLA_DV), F32)],
        scratch_shapes=[pltpu.VMEM((GLA_H, GLA_DK, GLA_DV), F32)],
        compiler_params=_cparams(2), name="gla",
    )(proj, proj, proj, proj, proj, w2p, b2.reshape(1, GLA_QK), ng.reshape(1, GLA_DV), s0)


def _split2(a):
    hi = a.astype(BF16)
    return hi, (a - hi.astype(F32)).astype(BF16)


def _dot3(a, b):
    return _dot(a[0], b[0]) + _dot(a[0], b[1]) + _dot(a[1], b[0])


def _inv_unit_lower_many(mats, n, eye):
    ps = [eye - a for a in mats]
    if n <= 2:
        return ps
    pows = [_split2(a) for a in mats]
    k = 2
    pending = None
    while k < n:
        sq = [_dot3(a, a) for a in pows]
        if pending is not None:
            ps = [p + _dot3(_split2(p), f) for p, f in zip(ps, pending)]
        pows = [_split2(a) for a in sq]
        pending = pows
        k *= 2
    return [p + _dot3(_split2(p), f) for p, f in zip(ps, pending)]


def _gdn_kernel(qkv_ref, sm_ref, gb_ref, cw_ref, cb_ref, alog_ref, dtb_ref, ng_ref, s0_ref, c0_ref,
                o_ref, sout_ref, cout_ref, s_scr, cv_scr, *, chunk, block, valid):
    blk = pl.program_id(1)

    @pl.when(blk == 0)
    def _():
        s_scr[...] = s0_ref[0]
        cv_scr[SUBLANE - (CONV_K - 1):SUBLANE, :] = c0_ref[0]

    ri = lax.broadcasted_iota(jnp.int32, (chunk, chunk), 0)
    ci = lax.broadcasted_iota(jnp.int32, (chunk, chunk), 1)
    causal = ri >= ci
    strict = ri > ci
    eye = (ri == ci).astype(F32)
    tri = causal.astype(F32)
    tri_u = (ri <= ci).astype(F32)
    inv_blk = min(INV_BLOCK, chunk)
    same_blk = (ri // inv_blk) == (ci // inv_blk)
    row = lax.broadcasted_iota(jnp.int32, (chunk, 1), 0)
    n_valid = min(valid, chunk)

    def do_chunk(s, carry):
        rows = pl.ds(pl.multiple_of(s * chunk, chunk), chunk)
        act = _silu(_causal_conv_chunk(qkv_ref[0, rows, :], cv_scr, cw_ref, cb_ref, chunk, n_valid))
        sm = sm_ref[0, rows, :]
        g_all = -jnp.exp(alog_ref[...]) * _softplus(sm + dtb_ref[...])
        beta_all = _sigmoid(sm)
        if valid < chunk:
            g_all = jnp.where(row < valid, g_all, 0.0)
            beta_all = jnp.where(row < valid, beta_all, 0.0)
        gc = _dot(tri, g_all, HIGHEST)
        gc_r = _dot_tn(g_all, tri_u, HIGHEST)
        heads = range(GDN_H)
        q, k, kb, rhs, dec, gcc = [], [], [], [], [], []
        for h in heads:
            qh = act[:, h * GDN_DK:(h + 1) * GDN_DK]
            kh = act[:, GDN_QK + h * GDN_DK:GDN_QK + (h + 1) * GDN_DK]
            vh = act[:, 2 * GDN_QK + h * GDN_DV:2 * GDN_QK + (h + 1) * GDN_DV]
            qh = qh * lax.rsqrt(jnp.sum(qh * qh, -1, keepdims=True) + EPS) * GDN_DK ** -0.5
            kh = kh * lax.rsqrt(jnp.sum(kh * kh, -1, keepdims=True) + EPS)
            beta = beta_all[:, AB_B_LANE + h:AB_B_LANE + h + 1]
            gch = gc[:, AB_A_LANE + h:AB_A_LANE + h + 1]
            gcr = gc_r[AB_A_LANE + h:AB_A_LANE + h + 1, :]
            q.append(qh)
            k.append(kh)
            kb.append(kh * beta)
            rhs.append(_split2(jnp.concatenate([vh * beta, kb[h] * jnp.exp(gch)], axis=1)))
            dec.append(jnp.exp(jnp.where(causal, gch - gcr, -jnp.inf)))
            gcc.append(gch)
        kbf = [a.astype(BF16) for a in k]
        m = [jnp.where(strict, _dot_nt(kb[h].astype(BF16), kbf[h]) * dec[h], 0.0) for h in heads]
        att = [_dot_nt(q[h].astype(BF16), kbf[h]) * dec[h] for h in heads]
        m_diag = [jnp.where(same_blk, a, 0.0) for a in m]
        t = [_split2(a) for a in _inv_unit_lower_many(m_diag, inv_blk, eye)]
        y = [_dot3(t[h], rhs[h]) for h in heads]
        if chunk > inv_blk:
            n_off = [_dot3(t[h], _split2(m[h] - m_diag[h])) for h in heads]
            qn = [_split2(a) for a in _inv_unit_lower_many(n_off, chunk // inv_blk, eye)]
            y = [_dot3(qn[h], _split2(y[h])) for h in heads]
        st = [s_scr[h] for h in heads]
        stb = [a.astype(BF16) for a in st]
        v_new = [y[h][:, :GDN_DV] - _dot(y[h][:, GDN_DV:].astype(BF16), stb[h]) for h in heads]
        vnb = [a.astype(BF16) for a in v_new]
        o = [_dot((q[h] * jnp.exp(gcc[h])).astype(BF16), stb[h]) + _dot(att[h].astype(BF16), vnb[h]) for h in heads]
        gb = gb_ref[0, rows, :]
        for h in heads:
            hs = slice(h * GDN_DV, (h + 1) * GDN_DV)
            g_last = gcc[h][chunk - 1:chunk, :]
            k_out = (k[h] * jnp.exp(g_last - gcc[h])).astype(BF16)
            s_scr[h] = st[h] * jnp.exp(g_last) + _dot_tn(k_out, vnb[h])
            o_ref[0, rows, hs] = _head_rms(o[h], ng_ref[...]) * _silu(gb[:, hs])
        return carry

    lax.fori_loop(0, block // chunk, do_chunk, 0)

    @pl.when(blk == pl.num_programs(1) - 1)
    def _():
        sout_ref[0] = s_scr[...]
        cout_ref[0] = cv_scr[SUBLANE - (CONV_K - 1):SUBLANE, :]


def _gdn(proj, conv_w, conv_b, alog_row, dtb_row, ng, s0, c0, chunk, block, valid):
    nb, seq, _ = proj.shape

    def col(width, off):
        return pl.BlockSpec((1, block, width), lambda b, i: (b, i, off // width))

    def const(shape):
        return pl.BlockSpec(shape, lambda b, i: (0,) * len(shape))

    st_spec = pl.BlockSpec((1, GDN_H, GDN_DK, GDN_DV), lambda b, i: (b, 0, 0, 0))
    cv_spec = pl.BlockSpec((1, CONV_K - 1, GDN_CONV_W), lambda b, i: (b, 0, 0))
    kern = functools.partial(_gdn_kernel, chunk=chunk, block=block, valid=valid)
    return pl.pallas_call(
        kern,
        grid=(nb, seq // block),
        in_specs=[col(GDN_CONV_W, AB_QKV), col(LANE, AB_SMALL), col(GDN_V, AB_G),
                  const((CONV_K, GDN_CONV_W)), const((1, GDN_CONV_W)), const((1, LANE)), const((1, LANE)),
                  const((1, GDN_DV)), st_spec, cv_spec],
        out_specs=[pl.BlockSpec((1, block, GDN_V), lambda b, i: (b, i, 0)), st_spec, cv_spec],
        out_shape=[jax.ShapeDtypeStruct((nb, seq, GDN_V), F32),
                   jax.ShapeDtypeStruct((nb, GDN_H, GDN_DK, GDN_DV), F32),
                   jax.ShapeDtypeStruct((nb, CONV_K - 1, GDN_CONV_W), F32)],
        scratch_shapes=[pltpu.VMEM((GDN_H, GDN_DK, GDN_DV), F32),
                        pltpu.VMEM((SUBLANE + chunk, GDN_CONV_W), F32)],
        compiler_params=_cparams(2), name="gdn",
    )(proj, proj, proj, conv_w, conv_b.reshape(1, GDN_CONV_W), alog_row, dtb_row, ng.reshape(1, GDN_DV), s0, c0)


def _ssd_kernel(xbc_ref, sm_ref, z_ref, cw_ref, cb_ref, alog_ref, dtb_ref, drow_ref, ng_ref, s0_ref, c0_ref,
                o_ref, sout_ref, cout_ref, s_scr, cv_scr, *, chunk, block, valid):
    blk = pl.program_id(1)

    @pl.when(blk == 0)
    def _():
        s_scr[...] = s0_ref[0]
        cv_scr[SUBLANE - (CONV_K - 1):SUBLANE, :] = c0_ref[0]

    ri = lax.broadcasted_iota(jnp.int32, (chunk, chunk), 0)
    ci = lax.broadcasted_iota(jnp.int32, (chunk, chunk), 1)
    causal = ri >= ci
    eye = (ri == ci).astype(F32)
    tri = causal.astype(F32)
    tri_u = (ri <= ci).astype(F32)
    row = lax.broadcasted_iota(jnp.int32, (chunk, 1), 0)
    lane_lo = lax.broadcasted_iota(jnp.int32, (chunk, LANE), 1) < SSD_P
    row_lo = lax.broadcasted_iota(jnp.int32, (2 * SSD_P, 1), 0) < SSD_P
    n_valid = min(valid, chunk)
    heads_per_group = SSD_H // SSD_G
    gsz = SSD_W // SSD_G

    def do_chunk(s, carry):
        rows = pl.ds(pl.multiple_of(s * chunk, chunk), chunk)
        act = _silu(_causal_conv_chunk(xbc_ref[0, rows, :], cv_scr, cw_ref, cb_ref, chunk, n_valid))
        dt = _softplus(sm_ref[0, rows, :] + dtb_ref[...])
        if valid < chunk:
            dt = jnp.where(row < valid, dt, 0.0)
        dta = dt * (-jnp.exp(alog_ref[...]))
        acs = _dot(tri, dta, HIGHEST)
        acs_r = _dot_tn(dta, tri_u, HIGHEST)
        dt_r = _dot_tn(dt, eye, HIGHEST)
        z = z_ref[0, rows, :]
        for g in range(SSD_G):
            bg = act[:, SSD_W + g * SSD_N:SSD_W + (g + 1) * SSD_N]
            cg = act[:, SSD_W + SSD_G * SSD_N + g * SSD_N:SSD_W + SSD_G * SSD_N + (g + 1) * SSD_N]
            cb = _dot_nt(cg, bg)
            parts = []
            for pr in range(heads_per_group // 2):
                pi = g * (heads_per_group // 2) + pr
                xp = act[:, pi * LANE:(pi + 1) * LANE]
                st = s_scr[pi]
                y_in, e_in, w_out, d_last = [], [], [], []
                for hh in range(2):
                    h = 2 * pi + hh
                    ac = acs[:, h:h + 1]
                    dec = jnp.exp(jnp.where(causal, ac - acs_r[h:h + 1, :], -jnp.inf))
                    y_in.append(_dot(cb * dec * dt_r[h:h + 1, :], xp))
                    a_last = ac[chunk - 1:chunk, :]
                    e_in.append(jnp.exp(ac))
                    w_out.append(jnp.exp(a_last - ac) * dt[:, h:h + 1])
                    d_last.append(jnp.exp(a_last))
                y = jnp.where(lane_lo, y_in[0], y_in[1])
                y = y + _dot_nt(cg, st) * jnp.where(lane_lo, e_in[0], e_in[1])
                y = y + drow_ref[:, pi * LANE:(pi + 1) * LANE] * xp
                x_sc = xp * jnp.where(lane_lo, w_out[0], w_out[1])
                s_scr[pi] = st * jnp.where(row_lo, d_last[0], d_last[1]) + _dot_tn(x_sc, bg)
                parts.append(y)
            gs = slice(g * gsz, (g + 1) * gsz)
            yg = jnp.concatenate(parts, axis=1) * _silu(z[:, gs])
            o_ref[0, rows, gs] = _head_rms(yg, ng_ref[:, gs])
        return carry

    lax.fori_loop(0, block // chunk, do_chunk, 0)

    @pl.when(blk == pl.num_programs(1) - 1)
    def _():
        sout_ref[0] = s_scr[...]
        cout_ref[0] = cv_scr[SUBLANE - (CONV_K - 1):SUBLANE, :]


def _ssd(proj, conv_w, conv_b, alog_row, dtb_row, d_row, ng, s0, c0, chunk, block, valid):
    nb, seq, _ = proj.shape
    n_pairs = SSD_H // 2

    def col(width, off):
        return pl.BlockSpec((1, block, width), lambda b, i: (b, i, off // width))

    def const(shape):
        return pl.BlockSpec(shape, lambda b, i: (0,) * len(shape))

    st_spec = pl.BlockSpec((1, n_pairs, 2 * SSD_P, SSD_N), lambda b, i: (b, 0, 0, 0))
    cv_spec = pl.BlockSpec((1, CONV_K - 1, SSD_CONV_W), lambda b, i: (b, 0, 0))
    kern = functools.partial(_ssd_kernel, chunk=chunk, block=block, valid=valid)
    o, s_new, c_new = pl.pallas_call(
        kern,
        grid=(nb, seq // block),
        in_specs=[col(SSD_CONV_W, CD_XBC), col(LANE, CD_SMALL), col(SSD_W, CD_Z),
                  const((CONV_K, SSD_CONV_W)), const((1, SSD_CONV_W)), const((1, LANE)), const((1, LANE)),
                  const((1, SSD_W)), const((1, SSD_W)), st_spec, cv_spec],
        out_specs=[pl.BlockSpec((1, block, SSD_W), lambda b, i: (b, i, 0)), st_spec, cv_spec],
        out_shape=[jax.ShapeDtypeStruct((nb, seq, SSD_W), F32),
                   jax.ShapeDtypeStruct((nb, n_pairs, 2 * SSD_P, SSD_N), F32),
                   jax.ShapeDtypeStruct((nb, CONV_K - 1, SSD_CONV_W), F32)],
        scratch_shapes=[pltpu.VMEM((n_pairs, 2 * SSD_P, SSD_N), F32),
                        pltpu.VMEM((SUBLANE + chunk, SSD_CONV_W), F32)],
        compiler_params=_cparams(2), name="ssd",
    )(proj, proj, proj, conv_w, conv_b.reshape(1, SSD_CONV_W), alog_row, dtb_row, d_row,
      ng.reshape(1, SSD_W), s0.reshape(nb, n_pairs, 2 * SSD_P, SSD_N), c0)
    return o, s_new.reshape(nb, SSD_H, SSD_P, SSD_N), c_new


def _s5_prep_kernel(are_ref, aim_ref, ldt_ref, bre_ref, bim_ref, lbr_ref, lbi_ref, bbr_ref, bbi_ref):
    a_re, a_im = are_ref[...], aim_ref[...]
    dt = jnp.exp(ldt_ref[...])
    mag = jnp.exp(a_re * dt)
    lb_re, lb_im = mag * jnp.cos(a_im * dt), mag * jnp.sin(a_im * dt)
    nr, ni = lb_re - 1.0, lb_im
    den = a_re * a_re + a_im * a_im
    f_re = (nr * a_re + ni * a_im) / den
    f_im = (ni * a_re - nr * a_im) / den
    b_re, b_im = bre_ref[...], bim_ref[...]
    lbr_ref[...] = lb_re
    lbi_ref[...] = lb_im
    bbr_ref[...] = f_re * b_re - f_im * b_im
    bbi_ref[...] = f_re * b_im + f_im * b_re


def _s5_prep(a_re, a_im, log_dt, b_re, b_im):
    g3 = (S5_G, 1, S5_P)
    b3 = (S5_G, S5_GS, S5_P)
    return pl.pallas_call(
        _s5_prep_kernel,
        out_shape=[jax.ShapeDtypeStruct(g3, F32), jax.ShapeDtypeStruct(g3, F32),
                   jax.ShapeDtypeStruct(b3, F32), jax.ShapeDtypeStruct(b3, F32)],
        name="s5_prep",
    )(a_re.reshape(g3), a_im.reshape(g3), log_dt.reshape(S5_G, 1, 1),
      jnp.swapaxes(b_re, 1, 2), jnp.swapaxes(b_im, 1, 2))


def _block_diag(blocks):
    g, r, c = blocks.shape
    per = 8
    b = blocks.reshape(g // per, per, r, 1, c) * jnp.eye(per, dtype=blocks.dtype).reshape(1, per, 1, per, 1)
    return b.reshape(g // per, per * r, per * c)


def _s5_kernel(u_ref, wre_ref, wim_ref, cre_ref, cim_ref, lbr_ref, lbi_ref, d_ref, x0r_ref, x0i_ref,
               y_ref, xfr_ref, xfi_ref, xr_scr, xi_scr, sr_scr, si_scr, *, rows_per_step, steps):
    c = pl.program_id(2)

    @pl.when(c == 0)
    def _():
        xr_scr[...] = x0r_ref[0]
        xi_scr[...] = x0i_ref[0]

    u = u_ref[0]
    sr_scr[...] = _dot(u, wre_ref[0])
    si_scr[...] = _dot(u, wim_ref[0])
    l_re, l_im = lbr_ref[...], lbi_ref[...]

    def step(t, carry):
        rows = pl.ds(pl.multiple_of(t * rows_per_step, rows_per_step), rows_per_step)
        xr, xi = xr_scr[...], xi_scr[...]
        nr = l_re * xr - l_im * xi + sr_scr[rows, :]
        ni = l_re * xi + l_im * xr + si_scr[rows, :]
        xr_scr[...] = nr
        xi_scr[...] = ni
        sr_scr[rows, :] = nr
        si_scr[rows, :] = ni
        return carry

    lax.fori_loop(0, steps, step, 0)
    y_ref[0] = _dot(sr_scr[...], cre_ref[0]) - _dot(si_scr[...], cim_ref[0]) + d_ref[...] * u

    @pl.when(c == pl.num_programs(2) - 1)
    def _():
        xfr_ref[0] = xr_scr[...]
        xfi_ref[0] = xi_scr[...]


def _s5(proj, w_re, w_im, c_re, c_im, lb_re, lb_im, d, x0_re, x0_im, rows_per_step, steps):
    ng, n_tok, _ = proj.shape
    nj = S5_W // LANE
    sw = S5_STATE // nj
    cr = rows_per_step * steps
    kern = functools.partial(_s5_kernel, rows_per_step=rows_per_step, steps=steps)
    x_spec = pl.BlockSpec((1, rows_per_step, sw), lambda g, j, c: (g, 0, j))
    return pl.pallas_call(
        kern,
        grid=(ng, nj, n_tok // cr),
        in_specs=[pl.BlockSpec((1, cr, LANE), lambda g, j, c: (g, c, CD_U // LANE + j)),
                  pl.BlockSpec((1, LANE, sw), lambda g, j, c: (j, 0, 0)),
                  pl.BlockSpec((1, LANE, sw), lambda g, j, c: (j, 0, 0)),
                  pl.BlockSpec((1, sw, LANE), lambda g, j, c: (j, 0, 0)),
                  pl.BlockSpec((1, sw, LANE), lambda g, j, c: (j, 0, 0)),
                  pl.BlockSpec((1, sw), lambda g, j, c: (0, j)),
                  pl.BlockSpec((1, sw), lambda g, j, c: (0, j)),
                  pl.BlockSpec((1, LANE), lambda g, j, c: (0, j)),
                  x_spec, x_spec],
        out_specs=[pl.BlockSpec((1, cr, LANE), lambda g, j, c: (g, c, j)), x_spec, x_spec],
        out_shape=[jax.ShapeDtypeStruct((ng, n_tok, S5_W), F32),
                   jax.ShapeDtypeStruct((ng, rows_per_step, S5_STATE), F32),
                   jax.ShapeDtypeStruct((ng, rows_per_step, S5_STATE), F32)],
        scratch_shapes=[pltpu.VMEM((rows_per_step, sw), F32), pltpu.VMEM((rows_per_step, sw), F32),
                        pltpu.VMEM((cr, sw), F32), pltpu.VMEM((cr, sw), F32)],
        compiler_params=_cparams(3), name="s5_scan",
    )(proj, w_re, w_im, c_re, c_im, lb_re, lb_im, d.reshape(1, S5_W), x0_re, x0_im)


def _s5_pow_kernel(lbr_ref, lbi_ref, pr_ref, pi_ref, *, n_rows):
    l_re, l_im = lbr_ref[...], lbi_ref[...]
    row = lax.broadcasted_iota(jnp.int32, (SUBLANE, 1), 0)
    p_re, p_im = l_re, l_im
    b_re = jnp.broadcast_to(l_re, (SUBLANE, l_re.shape[1]))
    b_im = jnp.broadcast_to(l_im, (SUBLANE, l_re.shape[1]))
    for r in range(1, SUBLANE):
        p_re, p_im = p_re * l_re - p_im * l_im, p_re * l_im + p_im * l_re
        b_re = jnp.where(row >= r, p_re, b_re)
        b_im = jnp.where(row >= r, p_im, b_im)
    q_re, q_im = jnp.ones_like(l_re), jnp.zeros_like(l_re)
    for a in range(n_rows // SUBLANE):
        rows = slice(a * SUBLANE, (a + 1) * SUBLANE)
        pr_ref[rows, :] = b_re * q_re - b_im * q_im
        pi_ref[rows, :] = b_re * q_im + b_im * q_re
        q_re, q_im = q_re * p_re - q_im * p_im, q_re * p_im + q_im * p_re


def _s5_pow_table(lb_re, lb_im, n_rows):
    shape = jax.ShapeDtypeStruct((n_rows, S5_STATE), F32)
    return pl.pallas_call(functools.partial(_s5_pow_kernel, n_rows=n_rows), out_shape=[shape, shape],
                          name="s5_pow")(lb_re, lb_im)


def _s5_seg_kernel(u_ref, wre_ref, wim_ref, cre_ref, cim_ref, pr_ref, pi_ref, d_ref, x0r_ref, x0i_ref,
                   y_ref, xfr_ref, xfi_ref, sr_scr, si_scr, *, seg_len):
    n_seg = SUBLANE
    n_lane_blk = sr_scr.shape[0]
    sw = n_lane_blk * LANE
    lane_blks = [slice(c * LANE, (c + 1) * LANE) for c in range(n_lane_blk)]

    def put(scr, rows, val):
        for c, ls in enumerate(lane_blks):
            scr[c, rows, :] = val[:, ls]

    def get(scr, rows):
        return jnp.concatenate([scr[c, rows, :] for c in range(n_lane_blk)], axis=1)

    for s in range(n_seg):
        us = u_ref[0, s * seg_len:(s + 1) * seg_len, :]
        put(sr_scr, pl.ds(s, seg_len, stride=n_seg), _dot(us, wre_ref[0]))
        put(si_scr, pl.ds(s, seg_len, stride=n_seg), _dot(us, wim_ref[0]))
    l_re, l_im = pr_ref[0:1, :], pi_ref[0:1, :]

    def step(t, carry):
        xr, xi = carry
        rows = pl.ds(pl.multiple_of(t * n_seg, n_seg), n_seg)
        nr = l_re * xr - l_im * xi + get(sr_scr, rows)
        ni = l_re * xi + l_im * xr + get(si_scr, rows)
        put(sr_scr, rows, nr)
        put(si_scr, rows, ni)
        return nr, ni

    zero = jnp.zeros((n_seg, sw), F32)
    end_re, end_im = lax.fori_loop(0, seg_len, step, (zero, zero))
    ln_re, ln_im = pr_ref[seg_len - 1:seg_len, :], pi_ref[seg_len - 1:seg_len, :]
    p_re, p_im = pr_ref[...], pi_ref[...]
    x_re, x_im = x0r_ref[0], x0i_ref[0]
    for s in range(n_seg):
        loc_re = get(sr_scr, pl.ds(s, seg_len, stride=n_seg))
        loc_im = get(si_scr, pl.ds(s, seg_len, stride=n_seg))
        t_re = loc_re + p_re * x_re - p_im * x_im
        t_im = loc_im + p_re * x_im + p_im * x_re
        us = u_ref[0, s * seg_len:(s + 1) * seg_len, :]
        y_ref[0, s * seg_len:(s + 1) * seg_len, :] = (_dot(t_re, cre_ref[0]) - _dot(t_im, cim_ref[0])
                                                      + d_ref[...] * us)
        e_re, e_im = end_re[s:s + 1, :], end_im[s:s + 1, :]
        x_re, x_im = e_re + ln_re * x_re - ln_im * x_im, e_im + ln_re * x_im + ln_im * x_re
    xfr_ref[0] = x_re
    xfi_ref[0] = x_im


def _s5_seg(proj, w_re, w_im, c_re, c_im, pow_re, pow_im, d, x0_re, x0_im):
    nb, seq, _ = proj.shape
    nj = S5_W // LANE
    sw = S5_STATE // nj
    seg_len = seq // SUBLANE
    x_spec = pl.BlockSpec((1, 1, sw), lambda b, j: (b, 0, j))
    return pl.pallas_call(
        functools.partial(_s5_seg_kernel, seg_len=seg_len),
        grid=(nb, nj),
        in_specs=[pl.BlockSpec((1, seq, LANE), lambda b, j: (b, 0, CD_U // LANE + j)),
                  pl.BlockSpec((1, LANE, sw), lambda b, j: (j, 0, 0)),
                  pl.BlockSpec((1, LANE, sw), lambda b, j: (j, 0, 0)),
                  pl.BlockSpec((1, sw, LANE), lambda b, j: (j, 0, 0)),
                  pl.BlockSpec((1, sw, LANE), lambda b, j: (j, 0, 0)),
                  pl.BlockSpec((seg_len, sw), lambda b, j: (0, j)),
                  pl.BlockSpec((seg_len, sw), lambda b, j: (0, j)),
                  pl.BlockSpec((1, LANE), lambda b, j: (0, j)),
                  x_spec, x_spec],
        out_specs=[pl.BlockSpec((1, seq, LANE), lambda b, j: (b, 0, j)), x_spec, x_spec],
        out_shape=[jax.ShapeDtypeStruct((nb, seq, S5_W), F32),
                   jax.ShapeDtypeStruct((nb, 1, S5_STATE), F32),
                   jax.ShapeDtypeStruct((nb, 1, S5_STATE), F32)],
        scratch_shapes=[pltpu.VMEM((sw // LANE, seq, LANE), F32), pltpu.VMEM((sw // LANE, seq, LANE), F32)],
        compiler_params=_cparams(2), name="s5_seg",
    )(proj, w_re, w_im, c_re, c_im, pow_re, pow_im, d.reshape(1, S5_W), x0_re, x0_im)


def _lane_row(vec, lane0):
    return jnp.zeros((1, LANE), F32).at[0, lane0:lane0 + vec.shape[0]].set(vec.astype(F32))


def _prep_params(p):
    d = D_MODEL
    q = {}
    w = p['w_in_ab'][0]
    o_lr = 2 * GLA_QK + GLA_V
    o_r = o_lr + GLA_LR
    o_qkv = o_r + GLA_V
    o_a = o_qkv + GDN_CONV_W
    o_g = o_a + 2 * GDN_H
    q['w_in_ab'] = jnp.concatenate(
        [w[:, :o_lr], w[:, o_r:o_qkv], w[:, o_qkv:o_a], w[:, o_g:], w[:, o_lr:o_r], w[:, o_a:o_g],
         jnp.zeros((d, LANE - GLA_LR - 2 * GDN_H), F32)], axis=1).astype(BF16)
    q['gla_w2'] = jnp.zeros((LANE, GLA_QK), F32).at[:GLA_LR].set(p['gla_w2'][0])
    q['gdn_alog'] = _lane_row(p['gdn_A_log'][0], AB_A_LANE)
    q['gdn_dtb'] = _lane_row(p['gdn_dt_bias'][0], AB_A_LANE)
    w_out = p['w_out_ab'][0].astype(BF16)
    q['w_out_a'], q['w_out_b'] = w_out[:GLA_V], w_out[GLA_V:]
    w = p['w_in_cd'][0]
    o_xbc = SSD_W
    o_dt = o_xbc + SSD_CONV_W
    o_u = o_dt + SSD_H
    q['w_in_cd'] = jnp.concatenate(
        [w[:, o_xbc:o_dt], w[:, o_dt:o_u], jnp.zeros((d, CD_Z - CD_SMALL - SSD_H), F32), w[:, :o_xbc], w[:, o_u:]],
        axis=1).astype(BF16)
    q['ssd_alog'] = _lane_row(p['ssd_A_log'][0], 0)
    q['ssd_dtb'] = _lane_row(p['ssd_dt_bias'][0], 0)
    q['ssd_d_row'] = jnp.repeat(p['ssd_D'][0].astype(F32), SSD_P).reshape(1, SSD_W)
    w_out = p['w_out_cd'][0].astype(BF16)
    q['w_out_c'], q['w_out_d'] = w_out[:SSD_W], w_out[SSD_W:]
    lb_re, lb_im, bb_re, bb_im = _s5_prep(p['s5_A_re'][0], p['s5_A_im'][0], p['s5_log_dt'][0],
                                          p['s5_B_re'][0], p['s5_B_im'][0])
    q['s5_lb_re'], q['s5_lb_im'] = lb_re.reshape(1, S5_STATE), lb_im.reshape(1, S5_STATE)
    q['s5_w_re'], q['s5_w_im'] = _block_diag(bb_re), _block_diag(bb_im)
    q['s5_c_re'] = _block_diag(jnp.swapaxes(p['s5_C_re'][0], 1, 2))
    q['s5_c_im'] = _block_diag(jnp.swapaxes(p['s5_C_im'][0], 1, 2))
    q['s5_glu_w'] = p['s5_glu_w'][0].astype(BF16)
    q['w_ffn_up'] = p['w_ffn_up'].astype(BF16)
    q['w_ffn_down'] = p['w_ffn_down'].astype(BF16)
    return q


def _trunk(x, mods, grp, seq_shape, state, p, q):
    nb, seq_len, valid = seq_shape
    s_gla, s_gdn, s_gdnc, s_ssd, s_ssdc, s_re, s_im, s_ffn = state
    prompt = not grp.per_token_mod

    def to_seq(t):
        if prompt:
            return t.reshape(nb, seq_len, t.shape[-1])
        t = jnp.swapaxes(t.reshape(valid, nb, t.shape[-1]), 0, 1)
        return jnp.pad(t, ((0, 0), (0, seq_len - valid), (0, 0)))

    def from_seq(t):
        if prompt:
            return t.reshape(nb * seq_len, t.shape[-1])
        return jnp.swapaxes(t[:, :valid], 0, 1).reshape(valid * nb, t.shape[-1])

    blk = MIX_BLOCK if prompt else seq_len
    chunks = (GLA_CHUNK, GDN_CHUNK, SSD_CHUNK) if prompt else (seq_len,) * 3
    new = {}

    h = _norm_mod(x, p['g_mix'][0], mods[0], grp, 1, 0)
    proj = to_seq(_matmul(h, q['w_in_ab'], grp, AB_N // 3))
    o_a, new['gla'] = _gla(proj, q['gla_w2'], p['gla_b2'][0], p['gla_norm_g'][0], s_gla, chunks[0], blk, valid)
    o_b, new['gdn'], new['gdnc'] = _gdn(proj, p['gdn_conv_w'][0], p['gdn_conv_b'][0], q['gdn_alog'], q['gdn_dtb'],
                                        p['gdn_norm_g'][0], s_gdn, s_gdnc, chunks[1], blk, valid)
    x = _mm_residual([(from_seq(o_a), q['w_out_a']), (from_seq(o_b), q['w_out_b'])], x, mods[0], grp, 2, 1024)
    h = _norm_mod(x, p['g_ffn'][0], mods[0], grp, 4, 3)
    act, new['ffn0'] = _ffn_up(h, q['w_ffn_up'][0], p['ffn_conv_w'][0], p['ffn_conv_b'][0], s_ffn[0], grp)
    x = _mm_residual([(act, q['w_ffn_down'][0])], x, mods[0], grp, 5, 1024)

    h = _norm_mod(x, p['g_mix'][1], mods[1], grp, 1, 0)
    proj2 = _matmul(h, q['w_in_cd'], grp, CD_N // 2)
    proj = to_seq(proj2)
    o_c, new['ssd'], new['ssdc'] = _ssd(proj, p['ssd_conv_w'][0], p['ssd_conv_b'][0], q['ssd_alog'], q['ssd_dtb'],
                                        q['ssd_d_row'], p['ssd_norm_g'][0], s_ssd, s_ssdc, chunks[2], blk, valid)
    if prompt:
        pow_re, pow_im = _s5_pow_table(q['s5_lb_re'], q['s5_lb_im'], seq_len // SUBLANE)
        yd, new['re'], new['im'] = _s5_seg(proj, q['s5_w_re'], q['s5_w_im'], q['s5_c_re'], q['s5_c_im'],
                                           pow_re, pow_im, p['s5_D'][0], s_re, s_im)
    else:
        yd, new['re'], new['im'] = _s5(proj2.reshape(1, grp.n_tok, CD_N), q['s5_w_re'], q['s5_w_im'],
                                       q['s5_c_re'], q['s5_c_im'], q['s5_lb_re'], q['s5_lb_im'], p['s5_D'][0],
                                       s_re, s_im, nb, valid)
    o_d = _s5_glu(yd.reshape(grp.n_tok, S5_W), q['s5_glu_w'], p['s5_glu_b'][0], grp)
    x = _mm_residual([(from_seq(o_c), q['w_out_c']), (o_d, q['w_out_d'])], x, mods[1], grp, 2, 1024)
    h = _norm_mod(x, p['g_ffn'][1], mods[1], grp, 4, 3)
    act, new['ffn1'] = _ffn_up(h, q['w_ffn_up'][1], p['ffn_conv_w'][1], p['ffn_conv_b'][1], s_ffn[1], grp)
    x = _mm_residual([(act, q['w_ffn_down'][1])], x, mods[1], grp, 5, 1024)
    return _final_rms(x, p['g_final'], grp), new


def kernel(x_prompt, x_sample, c_prompt, c_sample, state_gla, state_gdn, state_gdn_conv, state_ssd, state_ssd_conv, state_s5_re, state_s5_im, state_ffn_conv, w_ada, b_ada, g_mix, g_ffn, w_in_ab, gla_w2, gla_b2, gla_norm_g, gdn_conv_w, gdn_conv_b, gdn_A_log, gdn_dt_bias, gdn_norm_g, w_out_ab, w_in_cd, ssd_conv_w, ssd_conv_b, ssd_A_log, ssd_dt_bias, ssd_D, ssd_norm_g, s5_A_re, s5_A_im, s5_B_re, s5_B_im, s5_C_re, s5_C_im, s5_D, s5_log_dt, s5_glu_w, s5_glu_b, w_out_cd, w_ffn_up, ffn_conv_w, ffn_conv_b, w_ffn_down, g_final):
    p = dict(g_mix=g_mix, g_ffn=g_ffn, w_in_ab=w_in_ab, gla_w2=gla_w2, gla_b2=gla_b2, gla_norm_g=gla_norm_g,
             gdn_conv_w=gdn_conv_w, gdn_conv_b=gdn_conv_b, gdn_A_log=gdn_A_log, gdn_dt_bias=gdn_dt_bias,
             gdn_norm_g=gdn_norm_g, w_out_ab=w_out_ab, w_in_cd=w_in_cd, ssd_conv_w=ssd_conv_w,
             ssd_conv_b=ssd_conv_b, ssd_A_log=ssd_A_log, ssd_dt_bias=ssd_dt_bias, ssd_D=ssd_D,
             ssd_norm_g=ssd_norm_g, s5_A_re=s5_A_re, s5_A_im=s5_A_im, s5_B_re=s5_B_re, s5_B_im=s5_B_im,
             s5_C_re=s5_C_re, s5_C_im=s5_C_im, s5_D=s5_D, s5_log_dt=s5_log_dt, s5_glu_w=s5_glu_w,
             s5_glu_b=s5_glu_b, w_out_cd=w_out_cd, w_ffn_up=w_ffn_up, ffn_conv_w=ffn_conv_w,
             ffn_conv_b=ffn_conv_b, w_ffn_down=w_ffn_down, g_final=g_final)
    bp, lp, d = x_prompt.shape
    bs, ls, _ = x_sample.shape
    q = _prep_params(p)

    bp_pad = -(-bp // SUBLANE) * SUBLANE
    c_all = jnp.concatenate([c_prompt, jnp.zeros((bp_pad - bp, d), F32), c_sample], axis=0)
    mod = _ada_mod(c_all, w_ada, b_ada)
    depth = w_ada.shape[0]
    mods_p = [mod[l, :bp].reshape(bp, 1, 6 * d) for l in range(depth)]
    mods_s = [jnp.tile(mod[l, bp_pad:], (ls, 1)).reshape(1, ls * bs, 6 * d) for l in range(depth)]

    tile_p = 512
    grp_p = _Group(bp * lp, tile_p, False, lp // tile_p, 1)
    zeros = lambda *shape: jnp.zeros(shape, F32)
    state_p = (zeros(bp, GLA_H, GLA_DK, GLA_DV), zeros(bp, GDN_H, GDN_DK, GDN_DV),
               zeros(bp, CONV_K - 1, GDN_CONV_W), zeros(bp, SSD_H, SSD_P, SSD_N),
               zeros(bp, CONV_K - 1, SSD_CONV_W), zeros(bp, 1, S5_STATE), zeros(bp, 1, S5_STATE),
               zeros(depth, bp, FFN_K - 1, 2 * D_FF))
    y_p, new_p = _trunk(x_prompt.reshape(bp * lp, d), mods_p, grp_p, (bp, lp, lp), state_p, p, q)

    grp_s = _Group(bs * ls, bs * ls, True, 1, bs)
    ffn_hist_s = jnp.swapaxes(state_ffn_conv, 1, 2).reshape(depth, 1, (FFN_K - 1) * bs, 2 * D_FF)
    state_s = (state_gla[0], state_gdn[0], state_gdn_conv[0], state_ssd[0], state_ssd_conv[0],
               state_s5_re.reshape(1, bs, S5_STATE), state_s5_im.reshape(1, bs, S5_STATE), ffn_hist_s)
    x_s = jnp.swapaxes(x_sample, 0, 1).reshape(ls * bs, d)
    y_s, new_s = _trunk(x_s, mods_s, grp_s, (bs, SAMPLE_PAD, ls), state_s, p, q)
    y_s = jnp.swapaxes(y_s.reshape(ls, bs, d), 0, 1)

    ffn_p = jnp.stack([new_p['ffn0'], new_p['ffn1']])
    ffn_s = jnp.stack([jnp.swapaxes(new_s[k].reshape(FFN_K - 1, bs, 2 * D_FF), 0, 1) for k in ('ffn0', 'ffn1')])
    s5_shape = lambda t, nb: t.reshape(1, nb, S5_G, S5_P)
    return (y_p.reshape(bp, lp, d), y_s,
            new_p['gla'][None], new_s['gla'][None], new_p['gdn'][None], new_s['gdn'][None],
            new_p['gdnc'][None], new_s['gdnc'][None], new_p['ssd'][None], new_s['ssd'][None],
            new_p['ssdc'][None], new_s['ssdc'][None],
            s5_shape(new_p['re'], bp), s5_shape(new_s['re'], bs), s5_shape(new_p['im'], bp), s5_shape(new_s['im'], bs),
            ffn_p, ffn_s)
```

```python
import functools
import math

import jax
import jax.numpy as jnp
from jax import lax
from jax.experimental import pallas as pl
from jax.experimental.pallas import tpu as pltpu

F32 = jnp.float32
BF16 = jnp.bfloat16
HIGHEST = lax.Precision.HIGHEST
EPS = 1e-6

D_MODEL = 2048
GLA_H, GLA_DK, GLA_DV, GLA_LR = 4, 128, 256, 16
GLA_GATE_NORM = 16.0
GLA_QK, GLA_V = GLA_H * GLA_DK, GLA_H * GLA_DV
GDN_H, GDN_DK, GDN_DV = 8, 128, 128
GDN_QK, GDN_V = GDN_H * GDN_DK, GDN_H * GDN_DV
CONV_K = 4
GDN_CONV_W = 2 * GDN_QK + GDN_V
SSD_P, SSD_H, SSD_G, SSD_N = 64, 16, 2, 128
SSD_W = SSD_H * SSD_P
SSD_CONV_W = SSD_W + 2 * SSD_G * SSD_N
S5_W, S5_GS, S5_G, S5_P = 1024, 16, 64, 64
S5_STATE = S5_G * S5_P
D_FF = 5632
FFN_K = 3

LANE = 128
SUBLANE = 8
VMEM_LIMIT = 48 * 1024 * 1024

AB_Q, AB_K, AB_V, AB_R, AB_QKV, AB_G, AB_SMALL, AB_N = 0, 512, 1024, 2048, 3072, 6144, 7168, 7296
AB_LR_LANE, AB_A_LANE, AB_B_LANE = 0, 16, 24
CD_XBC, CD_SMALL, CD_Z, CD_U, CD_N = 0, 1536, 2048, 3072, 4096

MIX_BLOCK = 256
GLA_CHUNK, GDN_CHUNK, SSD_CHUNK = 16, 128, 128
GLA_GROUP = 4
FFN_TILE = 1024
FFN_SUB = 256
INV_BLOCK = 16
SAMPLE_PAD = 8


def _cparams(n_axes):
    return pltpu.CompilerParams(dimension_semantics=("arbitrary",) * n_axes, vmem_limit_bytes=VMEM_LIMIT)


def _sigmoid(x):
    return 1.0 / (1.0 + jnp.exp(-x))


def _silu(x):
    return x * _sigmoid(x)


def _softplus(x):
    return jnp.maximum(x, 0.0) + jnp.log(1.0 + jnp.exp(-jnp.abs(x)))


def _gelu_tanh(x):
    return 0.5 * x * (1.0 + jnp.tanh(math.sqrt(2.0 / math.pi) * (x + 0.044715 * (x * x * x))))


def _dot(a, b, precision=None):
    return jnp.dot(a, b, precision=precision, preferred_element_type=F32)


def _dot_nt(a, b, precision=None):
    return lax.dot_general(a, b, (((1,), (1,)), ((), ())), precision=precision, preferred_element_type=F32)


def _dot_tn(a, b, precision=None):
    return lax.dot_general(a, b, (((0,), (0,)), ((), ())), precision=precision, preferred_element_type=F32)


def _head_rms(o, g):
    return o * lax.rsqrt(jnp.mean(o * o, -1, keepdims=True) + EPS) * g


class _Group:
    def __init__(self, n_tok, tile, per_token_mod, tiles_per_seq, conv_shift):
        self.n_tok = n_tok
        self.tile = tile
        self.n_tiles = n_tok // tile
        self.per_token_mod = per_token_mod
        self.tiles_per_seq = tiles_per_seq
        self.conv_shift = conv_shift

    def retiled(self, tile):
        seq_rows = self.tile * self.tiles_per_seq
        if seq_rows % tile or self.n_tok % tile:
            return self
        return _Group(self.n_tok, tile, self.per_token_mod, seq_rows // tile, self.conv_shift)

    def mod_spec(self, width, col_block, m_axis):
        if self.per_token_mod:
            return pl.BlockSpec((1, self.tile, width), lambda *g: (0, g[m_axis], col_block(*g)))
        tps = self.tiles_per_seq
        return pl.BlockSpec((1, 1, width), lambda *g: (g[m_axis] // tps, 0, col_block(*g)))


def _ada_kernel(c_ref, w_ref, b_ref, o_ref):
    cs = _silu(c_ref[...]).astype(BF16)
    o_ref[0] = _dot(cs, w_ref[0].astype(BF16)) + b_ref[0]


def _ada_mod(c, w_ada, b_ada):
    depth, d, n = w_ada.shape
    rows = c.shape[0]
    tn = 1024
    return pl.pallas_call(
        _ada_kernel,
        grid=(depth, n // tn),
        in_specs=[pl.BlockSpec((rows, d), lambda l, j: (0, 0)),
                  pl.BlockSpec((1, d, tn), lambda l, j: (l, 0, j)),
                  pl.BlockSpec((1, 1, tn), lambda l, j: (l, 0, j))],
        out_specs=pl.BlockSpec((1, rows, tn), lambda l, j: (l, 0, j)),
        out_shape=jax.ShapeDtypeStruct((depth, rows, n), F32),
        compiler_params=_cparams(2), name="ada_mod",
    )(c, w_ada, b_ada.reshape(depth, 1, n))


def _norm_mod_kernel(x_ref, g_ref, sc_ref, sh_ref, o_ref):
    x = x_ref[...]
    y = x * lax.rsqrt(jnp.mean(x * x, -1, keepdims=True) + EPS) * g_ref[...]
    o_ref[...] = (y * (1.0 + sc_ref[0]) + sh_ref[0]).astype(BF16)


def _norm_mod(x, g, mod, grp, sc_blk, sh_blk):
    d = x.shape[1]
    return pl.pallas_call(
        _norm_mod_kernel,
        grid=(grp.n_tiles,),
        in_specs=[pl.BlockSpec((grp.tile, d), lambda i: (i, 0)),
                  pl.BlockSpec((1, d), lambda i: (0, 0)),
                  grp.mod_spec(d, lambda i: sc_blk, 0),
                  grp.mod_spec(d, lambda i: sh_blk, 0)],
        out_specs=pl.BlockSpec((grp.tile, d), lambda i: (i, 0)),
        out_shape=jax.ShapeDtypeStruct(x.shape, BF16),
        compiler_params=_cparams(1), name="norm_mod",
    )(x, g.reshape(1, d), mod, mod)


def _rms_kernel(x_ref, g_ref, o_ref):
    x = x_ref[...]
    o_ref[...] = x * lax.rsqrt(jnp.mean(x * x, -1, keepdims=True) + EPS) * g_ref[...]


def _final_rms(x, g, grp):
    d = x.shape[1]
    return pl.pallas_call(
        _rms_kernel,
        grid=(grp.n_tiles,),
        in_specs=[pl.BlockSpec((grp.tile, d), lambda i: (i, 0)), pl.BlockSpec((1, d), lambda i: (0, 0))],
        out_specs=pl.BlockSpec((grp.tile, d), lambda i: (i, 0)),
        out_shape=jax.ShapeDtypeStruct(x.shape, F32),
        compiler_params=_cparams(1), name="final_rms",
    )(x, g.reshape(1, d))


def _matmul_kernel(a_ref, w_ref, o_ref):
    o_ref[...] = _dot(a_ref[...], w_ref[...])


def _matmul(a, w, grp, tn):
    k, n = w.shape
    return pl.pallas_call(
        _matmul_kernel,
        grid=(n // tn, grp.n_tiles),
        in_specs=[pl.BlockSpec((grp.tile, k), lambda j, i: (i, 0)),
                  pl.BlockSpec((k, tn), lambda j, i: (0, j))],
        out_specs=pl.BlockSpec((grp.tile, tn), lambda j, i: (i, j)),
        out_shape=jax.ShapeDtypeStruct((a.shape[0], n), F32),
        compiler_params=_cparams(2), name="matmul",
    )(a, w)


def _mm_res_kernel(*refs, n_pairs):
    x_ref, gate_ref, o_ref = refs[2 * n_pairs:]
    y = _dot(refs[0][...].astype(BF16), refs[1][0])
    for p in range(1, n_pairs):
        y = y + _dot(refs[2 * p][...].astype(BF16), refs[2 * p + 1][0])
    o_ref[...] = x_ref[...] + gate_ref[0] * y


def _mm_residual(pairs, x, mod, grp, gate_blk, tn):
    n = x.shape[1]
    in_specs, args = [], []
    for a, w, layer, row_blk in pairs:
        k = a.shape[1]
        in_specs += [pl.BlockSpec((grp.tile, k), lambda j, i: (i, 0)),
                     pl.BlockSpec((1, k, tn), lambda j, i, layer=layer, row_blk=row_blk: (layer, row_blk, j))]
        args += [a, w]
    in_specs += [pl.BlockSpec((grp.tile, tn), lambda j, i: (i, j)),
                 grp.mod_spec(tn, lambda j, i: gate_blk * (n // tn) + j, 1)]
    return pl.pallas_call(
        functools.partial(_mm_res_kernel, n_pairs=len(pairs)),
        grid=(n // tn, grp.n_tiles),
        in_specs=in_specs,
        out_specs=pl.BlockSpec((grp.tile, tn), lambda j, i: (i, j)),
        out_shape=jax.ShapeDtypeStruct(x.shape, F32),
        compiler_params=_cparams(2), name="mm_residual",
    )(*args, x, mod)


def _ffn_up_kernel(h_ref, wa_ref, wg_ref, cwa_ref, cwg_ref, cba_ref, cbg_ref, ha_ref, hg_ref,
                   act_ref, sta_ref, stg_ref, scr_a, scr_g, *, shift, tile, sub, tiles_per_seq):
    i = pl.program_id(1)
    hist = (FFN_K - 1) * shift
    base = -(-hist // SUBLANE) * SUBLANE

    @pl.when(i % tiles_per_seq == 0)
    def _():
        scr_a[base - hist:base, :] = ha_ref[0]
        scr_g[base - hist:base, :] = hg_ref[0]

    def conv(scr, cw_ref, cb_ref, r0):
        y = cb_ref[0]
        for j in range(FFN_K):
            lo = base + r0 - (FFN_K - 1 - j) * shift
            y = y + scr[lo:lo + sub, :] * cw_ref[0, j:j + 1, :]
        return y

    def project(r0):
        h = h_ref[r0:r0 + sub, :]
        scr_a[base + r0:base + r0 + sub, :] = _dot(h, wa_ref[0])
        scr_g[base + r0:base + r0 + sub, :] = _dot(h, wg_ref[0])

    project(0)
    for r0 in range(0, tile, sub):
        if r0 + sub < tile:
            project(r0 + sub)
        a = conv(scr_a, cwa_ref, cba_ref, r0)
        g = conv(scr_g, cwg_ref, cbg_ref, r0)
        act_ref[r0:r0 + sub, :] = (_silu(g) * a).astype(BF16)
    last_a = scr_a[base + tile - hist:base + tile, :]
    last_g = scr_g[base + tile - hist:base + tile, :]
    sta_ref[0] = last_a
    stg_ref[0] = last_g
    scr_a[base - hist:base, :] = last_a
    scr_g[base - hist:base, :] = last_g


def _ffn_up(h, w_up, conv_w, conv_b, layer, hist0, grp):
    d = h.shape[1]
    tn = 512
    nj = D_FF // tn
    shift = grp.conv_shift
    hist = (FFN_K - 1) * shift
    base = -(-hist // SUBLANE) * SUBLANE
    n_seq = grp.n_tiles // grp.tiles_per_seq
    tps = grp.tiles_per_seq
    cb = conv_b.reshape(conv_b.shape[0], 1, 2 * D_FF)
    kern = functools.partial(_ffn_up_kernel, shift=shift, tile=grp.tile, sub=min(FFN_SUB, grp.tile),
                             tiles_per_seq=tps)
    assert grp.tile % min(FFN_SUB, grp.tile) == 0
    act, st_a, st_g = pl.pallas_call(
        kern,
        grid=(nj, grp.n_tiles),
        in_specs=[pl.BlockSpec((grp.tile, d), lambda j, i: (i, 0)),
                  pl.BlockSpec((1, d, tn), lambda j, i: (layer, 0, j)),
                  pl.BlockSpec((1, d, tn), lambda j, i: (layer, 0, nj + j)),
                  pl.BlockSpec((1, FFN_K, tn), lambda j, i: (layer, 0, j)),
                  pl.BlockSpec((1, FFN_K, tn), lambda j, i: (layer, 0, nj + j)),
                  pl.BlockSpec((1, 1, tn), lambda j, i: (layer, 0, j)),
                  pl.BlockSpec((1, 1, tn), lambda j, i: (layer, 0, nj + j)),
                  pl.BlockSpec((1, hist, tn), lambda j, i: (i // tps, 0, j)),
                  pl.BlockSpec((1, hist, tn), lambda j, i: (i // tps, 0, nj + j))],
        out_specs=[pl.BlockSpec((grp.tile, tn), lambda j, i: (i, j)),
                   pl.BlockSpec((1, hist, tn), lambda j, i: (i // tps, 0, j)),
                   pl.BlockSpec((1, hist, tn), lambda j, i: (i // tps, 0, j))],
        out_shape=[jax.ShapeDtypeStruct((h.shape[0], D_FF), BF16),
                   jax.ShapeDtypeStruct((n_seq, hist, D_FF), F32),
                   jax.ShapeDtypeStruct((n_seq, hist, D_FF), F32)],
        scratch_shapes=[pltpu.VMEM((base + grp.tile, tn), F32), pltpu.VMEM((base + grp.tile, tn), F32)],
        compiler_params=_cparams(2), name="ffn_up",
    )(h, w_up, w_up, conv_w, conv_w, cb, cb, hist0, hist0)
    return act, jnp.concatenate([st_a, st_g], axis=-1)


def _glu_kernel(y_ref, w_ref, b_ref, o_ref):
    z5 = _gelu_tanh(y_ref[...])
    o_ref[...] = z5 * _sigmoid(_dot(z5.astype(BF16), w_ref[...]) + b_ref[...])


def _s5_glu(yd, w, b, grp):
    n = yd.shape[1]
    return pl.pallas_call(
        _glu_kernel,
        grid=(grp.n_tiles,),
        in_specs=[pl.BlockSpec((grp.tile, n), lambda i: (i, 0)),
                  pl.BlockSpec((n, n), lambda i: (0, 0)),
                  pl.BlockSpec((1, n), lambda i: (0, 0))],
        out_specs=pl.BlockSpec((grp.tile, n), lambda i: (i, 0)),
        out_shape=jax.ShapeDtypeStruct(yd.shape, F32),
        compiler_params=_cparams(1), name="s5_glu",
    )(yd, w, b.reshape(1, n))


def _causal_conv_chunk(x, cv_scr, cw_ref, cb_ref, chunk, valid):
    base = SUBLANE
    cv_scr[base:base + chunk, :] = x
    y = cb_ref[...]
    for j in range(CONV_K):
        lo = base - (CONV_K - 1) + j
        y = y + cv_scr[lo:lo + chunk, :] * cw_ref[j:j + 1, :]
    last = cv_scr[base + valid - (CONV_K - 1):base + valid, :]
    cv_scr[base - (CONV_K - 1):base, :] = last
    return y


def _gla_kernel(q_ref, k_ref, v_ref, r_ref, sm_ref, w2_ref, b2_ref, ng_ref, s0_ref,
                o_ref, sout_ref, s_scr, *, chunk, group, block, valid):
    blk = pl.program_id(1)

    @pl.when(blk == 0)
    def _():
        s_scr[...] = s0_ref[0]

    span = chunk * group
    row3 = lax.broadcasted_iota(jnp.int32, (1, chunk, 1), 1)
    row_in_chunk = lax.broadcasted_iota(jnp.int32, (span, 1), 0) % chunk
    ri = lax.broadcasted_iota(jnp.int32, (span, span), 0)
    ci = lax.broadcasted_iota(jnp.int32, (span, span), 1)
    tri = ((ri // chunk == ci // chunk) & (ri >= ci)).astype(F32)
    eye = (lax.broadcasted_iota(jnp.int32, (GLA_DK, GLA_DK), 0)
           == lax.broadcasted_iota(jnp.int32, (GLA_DK, GLA_DK), 1))
    n_valid = min(valid, chunk)
    heads = range(GLA_H)
    chunks = range(group)

    def to3(t):
        return t.reshape(group, chunk, t.shape[-1])

    def do_span(s, carry):
        rows = pl.ds(pl.multiple_of(s * span, span), span)
        x = _dot(sm_ref[0, rows, :], w2_ref[...], HIGHEST) + b2_ref[...]
        log_a = (jnp.minimum(x, 0.0) - jnp.log(1.0 + jnp.exp(-jnp.abs(x)))) * (1.0 / GLA_GATE_NORM)
        if valid < chunk:
            log_a = jnp.where(row_in_chunk < valid, log_a, 0.0)
        b = _dot(tri, log_a, HIGHEST)
        q = q_ref[0, rows, :] * GLA_DK ** -0.5
        k = k_ref[0, rows, :]
        v = v_ref[0, rows, :]
        r = r_ref[0, rows, :]
        b3, q_in, kv, d_col = [], [], [], []
        for h in heads:
            ks = slice(h * GLA_DK, (h + 1) * GLA_DK)
            bh3 = to3(b[:, ks])
            b_last = bh3[:, chunk - 1:chunk, :]
            k_out = (to3(k[:, ks]) * jnp.exp(b_last - bh3)).reshape(span, GLA_DK)
            vh = v[:, h * GLA_DV:(h + 1) * GLA_DV]
            b3.append(bh3)
            q_in.append(q[:, ks] * jnp.exp(b[:, ks]))
            kv.append([_dot_tn(k_out[c * chunk:(c + 1) * chunk], vh[c * chunk:(c + 1) * chunk]) for c in chunks])
            d_col.append([jnp.sum(jnp.where(eye, jnp.exp(b_last[c]), 0.0), axis=-1, keepdims=True) for c in chunks])
        st = [s_scr[h] for h in heads]
        o_inter = [[] for _ in heads]
        for c in chunks:
            for h in heads:
                o_inter[h].append(_dot(q_in[h][c * chunk:(c + 1) * chunk], st[h]))
                st[h] = st[h] * d_col[h][c] + kv[h][c]
        for h in heads:
            ks = slice(h * GLA_DK, (h + 1) * GLA_DK)
            vs = slice(h * GLA_DV, (h + 1) * GLA_DV)
            s_scr[h] = st[h]
            bh3, qh3, kh3, vh3 = b3[h], to3(q[:, ks]), to3(k[:, ks]), to3(v[:, vs])
            o3 = jnp.zeros((group, chunk, GLA_DV), F32)
            for j in range(n_valid):
                e = jnp.exp(jnp.minimum(bh3 - bh3[:, j:j + 1, :], 0.0))
                sj = jnp.sum(qh3 * e * kh3[:, j:j + 1, :], axis=-1, keepdims=True)
                o3 = o3 + jnp.where(row3 >= j, sj, 0.0) * vh3[:, j:j + 1, :]
            o = o3.reshape(span, GLA_DV) + jnp.concatenate(o_inter[h], axis=0)
            o_ref[0, rows, vs] = _head_rms(o, ng_ref[...]) * _silu(r[:, vs])
        return carry

    lax.fori_loop(0, block // span, do_span, 0)

    @pl.when(blk == pl.num_programs(1) - 1)
    def _():
        sout_ref[0] = s_scr[...]


def _gla(proj, w2p, b2, ng, s0, chunk, group, block, valid):
    nb, seq, _ = proj.shape

    def col(width, off):
        return pl.BlockSpec((1, block, width), lambda b, i: (b, i, off // width))

    st_spec = pl.BlockSpec((1, GLA_H, GLA_DK, GLA_DV), lambda b, i: (b, 0, 0, 0))
    kern = functools.partial(_gla_kernel, chunk=chunk, group=group, block=block, valid=valid)
    return pl.pallas_call(
        kern,
        grid=(nb, seq // block),
        in_specs=[col(GLA_QK, AB_Q), col(GLA_QK, AB_K), col(GLA_V, AB_V), col(GLA_V, AB_R), col(LANE, AB_SMALL),
                  pl.BlockSpec((LANE, GLA_QK), lambda b, i: (0, 0)),
                  pl.BlockSpec((1, GLA_QK), lambda b, i: (0, 0)),
                  pl.BlockSpec((1, GLA_DV), lambda b, i: (0, 0)),
                  st_spec],
        out_specs=[pl.BlockSpec((1, block, GLA_V), lambda b, i: (b, i, 0)), st_spec],
        out_shape=[jax.ShapeDtypeStruct((nb, seq, GLA_V), F32),
                   jax.ShapeDtypeStruct((nb, GLA_H, GLA_DK, GLA_DV), F32)],
        scratch_shapes=[pltpu.VMEM((GLA_H, GLA_DK, GLA_DV), F32)],
        compiler_params=_cparams(2), name="gla",
    )(proj, proj, proj, proj, proj, w2p, b2.reshape(1, GLA_QK), ng.reshape(1, GLA_DV), s0)


def _split2(a):
    hi = a.astype(BF16)
    return hi, (a - hi.astype(F32)).astype(BF16)


def _dot3(a, b):
    return _dot(a[0], b[0]) + _dot(a[0], b[1]) + _dot(a[1], b[0])


def _inv_unit_lower_many(mats, n, eye):
    ps = [eye - a for a in mats]
    if n <= 2:
        return ps
    pows = [_split2(a) for a in mats]
    k = 2
    pending = None
    while k < n:
        sq = [_dot3(a, a) for a in pows]
        if pending is not None:
            ps = [p + _dot3(_split2(p), f) for p, f in zip(ps, pending)]
        pows = [_split2(a) for a in sq]
        pending = pows
        k *= 2
    return [p + _dot3(_split2(p), f) for p, f in zip(ps, pending)]


def _gdn_kernel(qkv_ref, sm_ref, gb_ref, cw_ref, cb_ref, alog_ref, dtb_ref, ng_ref, s0_ref, c0_ref,
                o_ref, sout_ref, cout_ref, s_scr, cv_scr, *, chunk, block, valid):
    blk = pl.program_id(1)

    @pl.when(blk == 0)
    def _():
        s_scr[...] = s0_ref[0]
        cv_scr[SUBLANE - (CONV_K - 1):SUBLANE, :] = c0_ref[0]

    ri = lax.broadcasted_iota(jnp.int32, (chunk, chunk), 0)
    ci = lax.broadcasted_iota(jnp.int32, (chunk, chunk), 1)
    causal = ri >= ci
    strict = ri > ci
    eye = (ri == ci).astype(F32)
    tri = causal.astype(F32)
    tri_u = (ri <= ci).astype(F32)
    inv_blk = min(INV_BLOCK, chunk)
    same_blk = (ri // inv_blk) == (ci // inv_blk)
    row = lax.broadcasted_iota(jnp.int32, (chunk, 1), 0)
    n_valid = min(valid, chunk)

    def do_chunk(s, carry):
        rows = pl.ds(pl.multiple_of(s * chunk, chunk), chunk)
        act = _silu(_causal_conv_chunk(qkv_ref[0, rows, :], cv_scr, cw_ref, cb_ref, chunk, n_valid))
        sm = sm_ref[0, rows, :]
        g_all = -jnp.exp(alog_ref[...]) * _softplus(sm + dtb_ref[...])
        beta_all = _sigmoid(sm)
        if valid < chunk:
            g_all = jnp.where(row < valid, g_all, 0.0)
            beta_all = jnp.where(row < valid, beta_all, 0.0)
        gc = _dot(tri, g_all, HIGHEST)
        gc_r = _dot_tn(g_all, tri_u, HIGHEST)
        heads = range(GDN_H)
        q, k, kb, rhs, dec, gcc = [], [], [], [], [], []
        for h in heads:
            qh = act[:, h * GDN_DK:(h + 1) * GDN_DK]
            kh = act[:, GDN_QK + h * GDN_DK:GDN_QK + (h + 1) * GDN_DK]
            vh = act[:, 2 * GDN_QK + h * GDN_DV:2 * GDN_QK + (h + 1) * GDN_DV]
            qh = qh * lax.rsqrt(jnp.sum(qh * qh, -1, keepdims=True) + EPS) * GDN_DK ** -0.5
            kh = kh * lax.rsqrt(jnp.sum(kh * kh, -1, keepdims=True) + EPS)
            beta = beta_all[:, AB_B_LANE + h:AB_B_LANE + h + 1]
            gch = gc[:, AB_A_LANE + h:AB_A_LANE + h + 1]
            gcr = gc_r[AB_A_LANE + h:AB_A_LANE + h + 1, :]
            q.append(qh)
            k.append(kh)
            kb.append(kh * beta)
            rhs.append(_split2(jnp.concatenate([vh * beta, kb[h] * jnp.exp(gch)], axis=1)))
            dec.append(jnp.exp(jnp.where(causal, gch - gcr, -jnp.inf)))
            gcc.append(gch)
        kbf = [a.astype(BF16) for a in k]
        m = [jnp.where(strict, _dot_nt(kb[h].astype(BF16), kbf[h]) * dec[h], 0.0) for h in heads]
        att = [_dot_nt(q[h].astype(BF16), kbf[h]) * dec[h] for h in heads]
        m_diag = [jnp.where(same_blk, a, 0.0) for a in m]
        t = [_split2(a) for a in _inv_unit_lower_many(m_diag, inv_blk, eye)]
        y = [_dot3(t[h], rhs[h]) for h in heads]
        if chunk > inv_blk:
            n_off = [_dot3(t[h], _split2(m[h] - m_diag[h])) for h in heads]
            qn = [_split2(a) for a in _inv_unit_lower_many(n_off, chunk // inv_blk, eye)]
            y = [_dot3(qn[h], _split2(y[h])) for h in heads]
        st = [s_scr[h] for h in heads]
        stb = [a.astype(BF16) for a in st]
        v_new = [y[h][:, :GDN_DV] - _dot(y[h][:, GDN_DV:].astype(BF16), stb[h]) for h in heads]
        vnb = [a.astype(BF16) for a in v_new]
        o = [_dot((q[h] * jnp.exp(gcc[h])).astype(BF16), stb[h]) + _dot(att[h].astype(BF16), vnb[h]) for h in heads]
        gb = gb_ref[0, rows, :]
        for h in heads:
            hs = slice(h * GDN_DV, (h + 1) * GDN_DV)
            g_last = gcc[h][chunk - 1:chunk, :]
            k_out = (k[h] * jnp.exp(g_last - gcc[h])).astype(BF16)
            s_scr[h] = st[h] * jnp.exp(g_last) + _dot_tn(k_out, vnb[h])
            o_ref[0, rows, hs] = _head_rms(o[h], ng_ref[...]) * _silu(gb[:, hs])
        return carry

    lax.fori_loop(0, block // chunk, do_chunk, 0)

    @pl.when(blk == pl.num_programs(1) - 1)
    def _():
        sout_ref[0] = s_scr[...]
        cout_ref[0] = cv_scr[SUBLANE - (CONV_K - 1):SUBLANE, :]


def _gdn(proj, conv_w, conv_b, alog_row, dtb_row, ng, s0, c0, chunk, block, valid):
    nb, seq, _ = proj.shape

    def col(width, off):
        return pl.BlockSpec((1, block, width), lambda b, i: (b, i, off // width))

    def const(shape):
        return pl.BlockSpec(shape, lambda b, i: (0,) * len(shape))

    st_spec = pl.BlockSpec((1, GDN_H, GDN_DK, GDN_DV), lambda b, i: (b, 0, 0, 0))
    cv_spec = pl.BlockSpec((1, CONV_K - 1, GDN_CONV_W), lambda b, i: (b, 0, 0))
    kern = functools.partial(_gdn_kernel, chunk=chunk, block=block, valid=valid)
    return pl.pallas_call(
        kern,
        grid=(nb, seq // block),
        in_specs=[col(GDN_CONV_W, AB_QKV), col(LANE, AB_SMALL), col(GDN_V, AB_G),
                  const((CONV_K, GDN_CONV_W)), const((1, GDN_CONV_W)), const((1, LANE)), const((1, LANE)),
                  const((1, GDN_DV)), st_spec, cv_spec],
        out_specs=[pl.BlockSpec((1, block, GDN_V), lambda b, i: (b, i, 0)), st_spec, cv_spec],
        out_shape=[jax.ShapeDtypeStruct((nb, seq, GDN_V), F32),
                   jax.ShapeDtypeStruct((nb, GDN_H, GDN_DK, GDN_DV), F32),
                   jax.ShapeDtypeStruct((nb, CONV_K - 1, GDN_CONV_W), F32)],
        scratch_shapes=[pltpu.VMEM((GDN_H, GDN_DK, GDN_DV), F32),
                        pltpu.VMEM((SUBLANE + chunk, GDN_CONV_W), F32)],
        compiler_params=_cparams(2), name="gdn",
    )(proj, proj, proj, conv_w, conv_b.reshape(1, GDN_CONV_W), alog_row, dtb_row, ng.reshape(1, GDN_DV), s0, c0)


def _ssd_kernel(xbc_ref, sm_ref, z_ref, cw_ref, cb_ref, alog_ref, dtb_ref, drow_ref, ng_ref, s0_ref, c0_ref,
                o_ref, sout_ref, cout_ref, s_scr, cv_scr, *, chunk, block, valid):
    blk = pl.program_id(1)

    @pl.when(blk == 0)
    def _():
        s_scr[...] = s0_ref[0]
        cv_scr[SUBLANE - (CONV_K - 1):SUBLANE, :] = c0_ref[0]

    ri = lax.broadcasted_iota(jnp.int32, (chunk, chunk), 0)
    ci = lax.broadcasted_iota(jnp.int32, (chunk, chunk), 1)
    causal = ri >= ci
    eye = (ri == ci).astype(F32)
    tri = causal.astype(F32)
    tri_u = (ri <= ci).astype(F32)
    row = lax.broadcasted_iota(jnp.int32, (chunk, 1), 0)
    lane_lo = lax.broadcasted_iota(jnp.int32, (chunk, LANE), 1) < SSD_P
    row_lo = lax.broadcasted_iota(jnp.int32, (2 * SSD_P, 1), 0) < SSD_P
    n_valid = min(valid, chunk)
    heads_per_group = SSD_H // SSD_G
    gsz = SSD_W // SSD_G

    def do_chunk(s, carry):
        rows = pl.ds(pl.multiple_of(s * chunk, chunk), chunk)
        act = _silu(_causal_conv_chunk(xbc_ref[0, rows, :], cv_scr, cw_ref, cb_ref, chunk, n_valid))
        dt = _softplus(sm_ref[0, rows, :] + dtb_ref[...])
        if valid < chunk:
            dt = jnp.where(row < valid, dt, 0.0)
        dta = dt * (-jnp.exp(alog_ref[...]))
        acs = _dot(tri, dta, HIGHEST)
        acs_r = _dot_tn(dta, tri_u, HIGHEST)
        dt_r = _dot_tn(dt, eye, HIGHEST)
        z = z_ref[0, rows, :]
        for g in range(SSD_G):
            bg = act[:, SSD_W + g * SSD_N:SSD_W + (g + 1) * SSD_N]
            cg = act[:, SSD_W + SSD_G * SSD_N + g * SSD_N:SSD_W + SSD_G * SSD_N + (g + 1) * SSD_N]
            cb = _dot_nt(cg, bg)
            parts = []
            for pr in range(heads_per_group // 2):
                pi = g * (heads_per_group // 2) + pr
                xp = act[:, pi * LANE:(pi + 1) * LANE]
                st = s_scr[pi]
                y_in, e_in, w_out, d_last = [], [], [], []
                for hh in range(2):
                    h = 2 * pi + hh
                    ac = acs[:, h:h + 1]
                    dec = jnp.exp(jnp.where(causal, ac - acs_r[h:h + 1, :], -jnp.inf))
                    y_in.append(_dot(cb * dec * dt_r[h:h + 1, :], xp))
                    a_last = ac[chunk - 1:chunk, :]
                    e_in.append(jnp.exp(ac))
                    w_out.append(jnp.exp(a_last - ac) * dt[:, h:h + 1])
                    d_last.append(jnp.exp(a_last))
                y = jnp.where(lane_lo, y_in[0], y_in[1])
                y = y + _dot_nt(cg, st) * jnp.where(lane_lo, e_in[0], e_in[1])
                y = y + drow_ref[:, pi * LANE:(pi + 1) * LANE] * xp
                x_sc = xp * jnp.where(lane_lo, w_out[0], w_out[1])
                s_scr[pi] = st * jnp.where(row_lo, d_last[0], d_last[1]) + _dot_tn(x_sc, bg)
                parts.append(y)
            gs = slice(g * gsz, (g + 1) * gsz)
            yg = jnp.concatenate(parts, axis=1) * _silu(z[:, gs])
            o_ref[0, rows, gs] = _head_rms(yg, ng_ref[:, gs])
        return carry

    lax.fori_loop(0, block // chunk, do_chunk, 0)

    @pl.when(blk == pl.num_programs(1) - 1)
    def _():
        sout_ref[0] = s_scr[...]
        cout_ref[0] = cv_scr[SUBLANE - (CONV_K - 1):SUBLANE, :]


def _ssd(proj, conv_w, conv_b, alog_row, dtb_row, d_row, ng, s0, c0, chunk, block, valid):
    nb, seq, _ = proj.shape
    n_pairs = SSD_H // 2

    def col(width, off):
        return pl.BlockSpec((1, block, width), lambda b, i: (b, i, off // width))

    def const(shape):
        return pl.BlockSpec(shape, lambda b, i: (0,) * len(shape))

    st_spec = pl.BlockSpec((1, n_pairs, 2 * SSD_P, SSD_N), lambda b, i: (b, 0, 0, 0))
    cv_spec = pl.BlockSpec((1, CONV_K - 1, SSD_CONV_W), lambda b, i: (b, 0, 0))
    kern = functools.partial(_ssd_kernel, chunk=chunk, block=block, valid=valid)
    o, s_new, c_new = pl.pallas_call(
        kern,
        grid=(nb, seq // block),
        in_specs=[col(SSD_CONV_W, CD_XBC), col(LANE, CD_SMALL), col(SSD_W, CD_Z),
                  const((CONV_K, SSD_CONV_W)), const((1, SSD_CONV_W)), const((1, LANE)), const((1, LANE)),
                  const((1, SSD_W)), const((1, SSD_W)), st_spec, cv_spec],
        out_specs=[pl.BlockSpec((1, block, SSD_W), lambda b, i: (b, i, 0)), st_spec, cv_spec],
        out_shape=[jax.ShapeDtypeStruct((nb, seq, SSD_W), F32),
                   jax.ShapeDtypeStruct((nb, n_pairs, 2 * SSD_P, SSD_N), F32),
                   jax.ShapeDtypeStruct((nb, CONV_K - 1, SSD_CONV_W), F32)],
        scratch_shapes=[pltpu.VMEM((n_pairs, 2 * SSD_P, SSD_N), F32),
                        pltpu.VMEM((SUBLANE + chunk, SSD_CONV_W), F32)],
        compiler_params=_cparams(2), name="ssd",
    )(proj, proj, proj, conv_w, conv_b.reshape(1, SSD_CONV_W), alog_row, dtb_row, d_row,
      ng.reshape(1, SSD_W), s0.reshape(nb, n_pairs, 2 * SSD_P, SSD_N), c0)
    return o, s_new.reshape(nb, SSD_H, SSD_P, SSD_N), c_new


def _s5_prep_kernel(are_ref, aim_ref, ldt_ref, bre_ref, bim_ref, lbr_ref, lbi_ref, bbr_ref, bbi_ref):
    a_re, a_im = are_ref[...], aim_ref[...]
    dt = jnp.exp(ldt_ref[...])
    mag = jnp.exp(a_re * dt)
    lb_re, lb_im = mag * jnp.cos(a_im * dt), mag * jnp.sin(a_im * dt)
    nr, ni = lb_re - 1.0, lb_im
    den = a_re * a_re + a_im * a_im
    f_re = (nr * a_re + ni * a_im) / den
    f_im = (ni * a_re - nr * a_im) / den
    b_re, b_im = bre_ref[...], bim_ref[...]
    lbr_ref[...] = lb_re
    lbi_ref[...] = lb_im
    bbr_ref[...] = f_re * b_re - f_im * b_im
    bbi_ref[...] = f_re * b_im + f_im * b_re


def _s5_prep(a_re, a_im, log_dt, b_re, b_im):
    g3 = (S5_G, 1, S5_P)
    b3 = (S5_G, S5_GS, S5_P)
    return pl.pallas_call(
        _s5_prep_kernel,
        out_shape=[jax.ShapeDtypeStruct(g3, F32), jax.ShapeDtypeStruct(g3, F32),
                   jax.ShapeDtypeStruct(b3, F32), jax.ShapeDtypeStruct(b3, F32)],
        name="s5_prep",
    )(a_re.reshape(g3), a_im.reshape(g3), log_dt.reshape(S5_G, 1, 1),
      jnp.swapaxes(b_re, 1, 2), jnp.swapaxes(b_im, 1, 2))


def _block_diag(blocks):
    g, r, c = blocks.shape
    per = 8
    b = blocks.reshape(g // per, per, r, 1, c) * jnp.eye(per, dtype=blocks.dtype).reshape(1, per, 1, per, 1)
    return b.reshape(g // per, per * r, per * c)


def _s5_kernel(u_ref, wre_ref, wim_ref, cre_ref, cim_ref, lbr_ref, lbi_ref, d_ref, x0r_ref, x0i_ref,
               y_ref, xfr_ref, xfi_ref, xr_scr, xi_scr, sr_scr, si_scr, *, rows_per_step, steps):
    c = pl.program_id(2)

    @pl.when(c == 0)
    def _():
        xr_scr[...] = x0r_ref[0]
        xi_scr[...] = x0i_ref[0]

    u = u_ref[0]
    sr_scr[...] = _dot(u, wre_ref[0])
    si_scr[...] = _dot(u, wim_ref[0])
    l_re, l_im = lbr_ref[...], lbi_ref[...]

    def step(t, carry):
        rows = pl.ds(pl.multiple_of(t * rows_per_step, rows_per_step), rows_per_step)
        xr, xi = xr_scr[...], xi_scr[...]
        nr = l_re * xr - l_im * xi + sr_scr[rows, :]
        ni = l_re * xi + l_im * xr + si_scr[rows, :]
        xr_scr[...] = nr
        xi_scr[...] = ni
        sr_scr[rows, :] = nr
        si_scr[rows, :] = ni
        return carry

    lax.fori_loop(0, steps, step, 0)
    y_ref[0] = _dot(sr_scr[...], cre_ref[0]) - _dot(si_scr[...], cim_ref[0]) + d_ref[...] * u

    @pl.when(c == pl.num_programs(2) - 1)
    def _():
        xfr_ref[0] = xr_scr[...]
        xfi_ref[0] = xi_scr[...]


def _s5(proj, w_re, w_im, c_re, c_im, lb_re, lb_im, d, x0_re, x0_im, rows_per_step, steps):
    ng, n_tok, _ = proj.shape
    nj = S5_W // LANE
    sw = S5_STATE // nj
    cr = rows_per_step * steps
    kern = functools.partial(_s5_kernel, rows_per_step=rows_per_step, steps=steps)
    x_spec = pl.BlockSpec((1, rows_per_step, sw), lambda g, j, c: (g, 0, j))
    return pl.pallas_call(
        kern,
        grid=(ng, nj, n_tok // cr),
        in_specs=[pl.BlockSpec((1, cr, LANE), lambda g, j, c: (g, c, CD_U // LANE + j)),
                  pl.BlockSpec((1, LANE, sw), lambda g, j, c: (j, 0, 0)),
                  pl.BlockSpec((1, LANE, sw), lambda g, j, c: (j, 0, 0)),
                  pl.BlockSpec((1, sw, LANE), lambda g, j, c: (j, 0, 0)),
                  pl.BlockSpec((1, sw, LANE), lambda g, j, c: (j, 0, 0)),
                  pl.BlockSpec((1, sw), lambda g, j, c: (0, j)),
                  pl.BlockSpec((1, sw), lambda g, j, c: (0, j)),
                  pl.BlockSpec((1, LANE), lambda g, j, c: (0, j)),
                  x_spec, x_spec],
        out_specs=[pl.BlockSpec((1, cr, LANE), lambda g, j, c: (g, c, j)), x_spec, x_spec],
        out_shape=[jax.ShapeDtypeStruct((ng, n_tok, S5_W), F32),
                   jax.ShapeDtypeStruct((ng, rows_per_step, S5_STATE), F32),
                   jax.ShapeDtypeStruct((ng, rows_per_step, S5_STATE), F32)],
        scratch_shapes=[pltpu.VMEM((rows_per_step, sw), F32), pltpu.VMEM((rows_per_step, sw), F32),
                        pltpu.VMEM((cr, sw), F32), pltpu.VMEM((cr, sw), F32)],
        compiler_params=_cparams(3), name="s5_scan",
    )(proj, w_re, w_im, c_re, c_im, lb_re, lb_im, d.reshape(1, S5_W), x0_re, x0_im)


def _s5_pow_kernel(lbr_ref, lbi_ref, pr_ref, pi_ref, *, n_rows):
    l_re, l_im = lbr_ref[...], lbi_ref[...]
    row = lax.broadcasted_iota(jnp.int32, (SUBLANE, 1), 0)
    p_re, p_im = l_re, l_im
    b_re = jnp.broadcast_to(l_re, (SUBLANE, l_re.shape[1]))
    b_im = jnp.broadcast_to(l_im, (SUBLANE, l_re.shape[1]))
    for r in range(1, SUBLANE):
        p_re, p_im = p_re * l_re - p_im * l_im, p_re * l_im + p_im * l_re
        b_re = jnp.where(row >= r, p_re, b_re)
        b_im = jnp.where(row >= r, p_im, b_im)
    q_re, q_im = jnp.ones_like(l_re), jnp.zeros_like(l_re)
    for a in range(n_rows // SUBLANE):
        rows = slice(a * SUBLANE, (a + 1) * SUBLANE)
        pr_ref[rows, :] = b_re * q_re - b_im * q_im
        pi_ref[rows, :] = b_re * q_im + b_im * q_re
        q_re, q_im = q_re * p_re - q_im * p_im, q_re * p_im + q_im * p_re


def _s5_pow_table(lb_re, lb_im, n_rows):
    shape = jax.ShapeDtypeStruct((n_rows, S5_STATE), F32)
    return pl.pallas_call(functools.partial(_s5_pow_kernel, n_rows=n_rows), out_shape=[shape, shape],
                          name="s5_pow")(lb_re, lb_im)


def _s5_seg_kernel(u_ref, wre_ref, wim_ref, cre_ref, cim_ref, pr_ref, pi_ref, d_ref, x0r_ref, x0i_ref,
                   y_ref, xfr_ref, xfi_ref, sr_scr, si_scr, *, seg_len):
    n_seg = SUBLANE
    n_lane_blk = sr_scr.shape[0]
    sw = n_lane_blk * LANE
    lane_blks = [slice(c * LANE, (c + 1) * LANE) for c in range(n_lane_blk)]

    def put(scr, rows, val):
        for c, ls in enumerate(lane_blks):
            scr[c, rows, :] = val[:, ls]

    def get(scr, rows):
        return jnp.concatenate([scr[c, rows, :] for c in range(n_lane_blk)], axis=1)

    for s in range(n_seg):
        us = u_ref[0, s * seg_len:(s + 1) * seg_len, :]
        put(sr_scr, pl.ds(s, seg_len, stride=n_seg), _dot(us, wre_ref[0]))
        put(si_scr, pl.ds(s, seg_len, stride=n_seg), _dot(us, wim_ref[0]))
    l_re, l_im = pr_ref[0:1, :], pi_ref[0:1, :]

    def step(t, carry):
        xr, xi = carry
        rows = pl.ds(pl.multiple_of(t * n_seg, n_seg), n_seg)
        nr = l_re * xr - l_im * xi + get(sr_scr, rows)
        ni = l_re * xi + l_im * xr + get(si_scr, rows)
        put(sr_scr, rows, nr)
        put(si_scr, rows, ni)
        return nr, ni

    zero = jnp.zeros((n_seg, sw), F32)
    end_re, end_im = lax.fori_loop(0, seg_len, step, (zero, zero))
    ln_re, ln_im = pr_ref[seg_len - 1:seg_len, :], pi_ref[seg_len - 1:seg_len, :]
    p_re, p_im = pr_ref[...], pi_ref[...]
    x_re, x_im = x0r_ref[0], x0i_ref[0]
    for s in range(n_seg):
        loc_re = get(sr_scr, pl.ds(s, seg_len, stride=n_seg))
        loc_im = get(si_scr, pl.ds(s, seg_len, stride=n_seg))
        t_re = loc_re + p_re * x_re - p_im * x_im
        t_im = loc_im + p_re * x_im + p_im * x_re
        us = u_ref[0, s * seg_len:(s + 1) * seg_len, :]
        y_ref[0, s * seg_len:(s + 1) * seg_len, :] = (_dot(t_re, cre_ref[0]) - _dot(t_im, cim_ref[0])
                                                      + d_ref[...] * us)
        e_re, e_im = end_re[s:s + 1, :], end_im[s:s + 1, :]
        x_re, x_im = e_re + ln_re * x_re - ln_im * x_im, e_im + ln_re * x_im + ln_im * x_re
    xfr_ref[0] = x_re
    xfi_ref[0] = x_im


def _s5_seg(proj, w_re, w_im, c_re, c_im, pow_re, pow_im, d, x0_re, x0_im):
    nb, seq, _ = proj.shape
    nj = S5_W // LANE
    sw = S5_STATE // nj
    seg_len = seq // SUBLANE
    x_spec = pl.BlockSpec((1, 1, sw), lambda b, j: (b, 0, j))
    return pl.pallas_call(
        functools.partial(_s5_seg_kernel, seg_len=seg_len),
        grid=(nb, nj),
        in_specs=[pl.BlockSpec((1, seq, LANE), lambda b, j: (b, 0, CD_U // LANE + j)),
                  pl.BlockSpec((1, LANE, sw), lambda b, j: (j, 0, 0)),
                  pl.BlockSpec((1, LANE, sw), lambda b, j: (j, 0, 0)),
                  pl.BlockSpec((1, sw, LANE), lambda b, j: (j, 0, 0)),
                  pl.BlockSpec((1, sw, LANE), lambda b, j: (j, 0, 0)),
                  pl.BlockSpec((seg_len, sw), lambda b, j: (0, j)),
                  pl.BlockSpec((seg_len, sw), lambda b, j: (0, j)),
                  pl.BlockSpec((1, LANE), lambda b, j: (0, j)),
                  x_spec, x_spec],
        out_specs=[pl.BlockSpec((1, seq, LANE), lambda b, j: (b, 0, j)), x_spec, x_spec],
        out_shape=[jax.ShapeDtypeStruct((nb, seq, S5_W), F32),
                   jax.ShapeDtypeStruct((nb, 1, S5_STATE), F32),
                   jax.ShapeDtypeStruct((nb, 1, S5_STATE), F32)],
        scratch_shapes=[pltpu.VMEM((sw // LANE, seq, LANE), F32), pltpu.VMEM((sw // LANE, seq, LANE), F32)],
        compiler_params=_cparams(2), name="s5_seg",
    )(proj, w_re, w_im, c_re, c_im, pow_re, pow_im, d.reshape(1, S5_W), x0_re, x0_im)


def _lane_row(vec, lane0):
    return jnp.zeros((1, LANE), F32).at[0, lane0:lane0 + vec.shape[0]].set(vec.astype(F32))


def _prep_params(p):
    d = D_MODEL
    q = {}
    w = p['w_in_ab'][0]
    o_lr = 2 * GLA_QK + GLA_V
    o_r = o_lr + GLA_LR
    o_qkv = o_r + GLA_V
    o_a = o_qkv + GDN_CONV_W
    o_g = o_a + 2 * GDN_H
    q['w_in_ab'] = jnp.concatenate(
        [w[:, :o_lr], w[:, o_r:o_qkv], w[:, o_qkv:o_a], w[:, o_g:], w[:, o_lr:o_r], w[:, o_a:o_g],
         jnp.zeros((d, LANE - GLA_LR - 2 * GDN_H), F32)], axis=1).astype(BF16)
    q['gla_w2'] = jnp.zeros((LANE, GLA_QK), F32).at[:GLA_LR].set(p['gla_w2'][0])
    q['gdn_alog'] = _lane_row(p['gdn_A_log'][0], AB_A_LANE)
    q['gdn_dtb'] = _lane_row(p['gdn_dt_bias'][0], AB_A_LANE)
    q['w_out_ab'] = p['w_out_ab'].astype(BF16)
    w = p['w_in_cd'][0]
    o_xbc = SSD_W
    o_dt = o_xbc + SSD_CONV_W
    o_u = o_dt + SSD_H
    q['w_in_cd'] = jnp.concatenate(
        [w[:, o_xbc:o_dt], w[:, o_dt:o_u], jnp.zeros((d, CD_Z - CD_SMALL - SSD_H), F32), w[:, :o_xbc], w[:, o_u:]],
        axis=1).astype(BF16)
    q['ssd_alog'] = _lane_row(p['ssd_A_log'][0], 0)
    q['ssd_dtb'] = _lane_row(p['ssd_dt_bias'][0], 0)
    q['ssd_d_row'] = jnp.repeat(p['ssd_D'][0].astype(F32), SSD_P).reshape(1, SSD_W)
    q['w_out_cd'] = p['w_out_cd'].astype(BF16)
    lb_re, lb_im, bb_re, bb_im = _s5_prep(p['s5_A_re'][0], p['s5_A_im'][0], p['s5_log_dt'][0],
                                          p['s5_B_re'][0], p['s5_B_im'][0])
    q['s5_lb_re'], q['s5_lb_im'] = lb_re.reshape(1, S5_STATE), lb_im.reshape(1, S5_STATE)
    q['s5_w_re'], q['s5_w_im'] = _block_diag(bb_re), _block_diag(bb_im)
    q['s5_c_re'] = _block_diag(jnp.swapaxes(p['s5_C_re'][0], 1, 2))
    q['s5_c_im'] = _block_diag(jnp.swapaxes(p['s5_C_im'][0], 1, 2))
    q['s5_glu_w'] = p['s5_glu_w'][0].astype(BF16)
    q['w_ffn_up'] = p['w_ffn_up'].astype(BF16)
    q['w_ffn_down'] = p['w_ffn_down'].astype(BF16)
    return q


def _trunk(x, mods, grp, seq_shape, state, p, q):
    nb, seq_len, valid = seq_shape
    s_gla, s_gdn, s_gdnc, s_ssd, s_ssdc, s_re, s_im, s_ffn = state
    prompt = not grp.per_token_mod

    def to_seq(t):
        if prompt:
            return t.reshape(nb, seq_len, t.shape[-1])
        t = jnp.swapaxes(t.reshape(valid, nb, t.shape[-1]), 0, 1)
        return jnp.pad(t, ((0, 0), (0, seq_len - valid), (0, 0)))

    def from_seq(t):
        if prompt:
            return t.reshape(nb * seq_len, t.shape[-1])
        return jnp.swapaxes(t[:, :valid], 0, 1).reshape(valid * nb, t.shape[-1])

    blk = MIX_BLOCK if prompt else seq_len
    chunks = (GLA_CHUNK, GDN_CHUNK, SSD_CHUNK) if prompt else (seq_len,) * 3
    new = {}

    h = _norm_mod(x, p['g_mix'][0], mods[0], grp, 1, 0)
    proj = to_seq(_matmul(h, q['w_in_ab'], grp, AB_N // 3))
    o_a, new['gla'] = _gla(proj, q['gla_w2'], p['gla_b2'][0], p['gla_norm_g'][0], s_gla, chunks[0],
                            GLA_GROUP if prompt else 1, blk, valid)
    o_b, new['gdn'], new['gdnc'] = _gdn(proj, p['gdn_conv_w'][0], p['gdn_conv_b'][0], q['gdn_alog'], q['gdn_dtb'],
                                        p['gdn_norm_g'][0], s_gdn, s_gdnc, chunks[1], blk, valid)
    x = _mm_residual([(from_seq(o_a), q['w_out_ab'], 0, 0), (from_seq(o_b), q['w_out_ab'], 0, 1)],
                     x, mods[0], grp, 2, 1024)
    h = _norm_mod(x, p['g_ffn'][0], mods[0], grp, 4, 3)
    act, new['ffn0'] = _ffn_up(h, q['w_ffn_up'], p['ffn_conv_w'], p['ffn_conv_b'], 0, s_ffn[0],
                               grp.retiled(FFN_TILE))
    x = _mm_residual([(act, q['w_ffn_down'], 0, 0)], x, mods[0], grp, 5, 1024)

    h = _norm_mod(x, p['g_mix'][1], mods[1], grp, 1, 0)
    proj2 = _matmul(h, q['w_in_cd'], grp, CD_N // 2)
    proj = to_seq(proj2)
    o_c, new['ssd'], new['ssdc'] = _ssd(proj, p['ssd_conv_w'][0], p['ssd_conv_b'][0], q['ssd_alog'], q['ssd_dtb'],
                                        q['ssd_d_row'], p['ssd_norm_g'][0], s_ssd, s_ssdc, chunks[2], blk, valid)
    if prompt:
        pow_re, pow_im = _s5_pow_table(q['s5_lb_re'], q['s5_lb_im'], seq_len // SUBLANE)
        yd, new['re'], new['im'] = _s5_seg(proj, q['s5_w_re'], q['s5_w_im'], q['s5_c_re'], q['s5_c_im'],
                                           pow_re, pow_im, p['s5_D'][0], s_re, s_im)
    else:
        yd, new['re'], new['im'] = _s5(proj2.reshape(1, grp.n_tok, CD_N), q['s5_w_re'], q['s5_w_im'],
                                       q['s5_c_re'], q['s5_c_im'], q['s5_lb_re'], q['s5_lb_im'], p['s5_D'][0],
                                       s_re, s_im, nb, valid)
    o_d = _s5_glu(yd.reshape(grp.n_tok, S5_W), q['s5_glu_w'], p['s5_glu_b'][0], grp)
    x = _mm_residual([(from_seq(o_c), q['w_out_cd'], 0, 0), (o_d, q['w_out_cd'], 0, 1)], x, mods[1], grp, 2, 1024)
    h = _norm_mod(x, p['g_ffn'][1], mods[1], grp, 4, 3)
    act, new['ffn1'] = _ffn_up(h, q['w_ffn_up'], p['ffn_conv_w'], p['ffn_conv_b'], 1, s_ffn[1],
                               grp.retiled(FFN_TILE))
    x = _mm_residual([(act, q['w_ffn_down'], 1, 0)], x, mods[1], grp, 5, 1024)
    return _final_rms(x, p['g_final'], grp), new


def kernel(x_prompt, x_sample, c_prompt, c_sample, state_gla, state_gdn, state_gdn_conv, state_ssd, state_ssd_conv, state_s5_re, state_s5_im, state_ffn_conv, w_ada, b_ada, g_mix, g_ffn, w_in_ab, gla_w2, gla_b2, gla_norm_g, gdn_conv_w, gdn_conv_b, gdn_A_log, gdn_dt_bias, gdn_norm_g, w_out_ab, w_in_cd, ssd_conv_w, ssd_conv_b, ssd_A_log, ssd_dt_bias, ssd_D, ssd_norm_g, s5_A_re, s5_A_im, s5_B_re, s5_B_im, s5_C_re, s5_C_im, s5_D, s5_log_dt, s5_glu_w, s5_glu_b, w_out_cd, w_ffn_up, ffn_conv_w, ffn_conv_b, w_ffn_down, g_final):
    p = dict(g_mix=g_mix, g_ffn=g_ffn, w_in_ab=w_in_ab, gla_w2=gla_w2, gla_b2=gla_b2, gla_norm_g=gla_norm_g,
             gdn_conv_w=gdn_conv_w, gdn_conv_b=gdn_conv_b, gdn_A_log=gdn_A_log, gdn_dt_bias=gdn_dt_bias,
             gdn_norm_g=gdn_norm_g, w_out_ab=w_out_ab, w_in_cd=w_in_cd, ssd_conv_w=ssd_conv_w,
             ssd_conv_b=ssd_conv_b, ssd_A_log=ssd_A_log, ssd_dt_bias=ssd_dt_bias, ssd_D=ssd_D,
             ssd_norm_g=ssd_norm_g, s5_A_re=s5_A_re, s5_A_im=s5_A_im, s5_B_re=s5_B_re, s5_B_im=s5_B_im,
             s5_C_re=s5_C_re, s5_C_im=s5_C_im, s5_D=s5_D, s5_log_dt=s5_log_dt, s5_glu_w=s5_glu_w,
             s5_glu_b=s5_glu_b, w_out_cd=w_out_cd, w_ffn_up=w_ffn_up, ffn_conv_w=ffn_conv_w,
             ffn_conv_b=ffn_conv_b, w_ffn_down=w_ffn_down, g_final=g_final)
    bp, lp, d = x_prompt.shape
    bs, ls, _ = x_sample.shape
    q = _prep_params(p)

    bp_pad = -(-bp // SUBLANE) * SUBLANE
    c_all = jnp.concatenate([c_prompt, jnp.zeros((bp_pad - bp, d), F32), c_sample], axis=0)
    mod = _ada_mod(c_all, w_ada, b_ada)
    depth = w_ada.shape[0]
    mods_p = [mod[l, :bp].reshape(bp, 1, 6 * d) for l in range(depth)]
    mods_s = [jnp.tile(mod[l, bp_pad:], (ls, 1)).reshape(1, ls * bs, 6 * d) for l in range(depth)]

    tile_p = 512
    grp_p = _Group(bp * lp, tile_p, False, lp // tile_p, 1)
    zeros = lambda *shape: jnp.zeros(shape, F32)
    state_p = (zeros(bp, GLA_H, GLA_DK, GLA_DV), zeros(bp, GDN_H, GDN_DK, GDN_DV),
               zeros(bp, CONV_K - 1, GDN_CONV_W), zeros(bp, SSD_H, SSD_P, SSD_N),
               zeros(bp, CONV_K - 1, SSD_CONV_W), zeros(bp, 1, S5_STATE), zeros(bp, 1, S5_STATE),
               zeros(depth, bp, FFN_K - 1, 2 * D_FF))
    y_p, new_p = _trunk(x_prompt.reshape(bp * lp, d), mods_p, grp_p, (bp, lp, lp), state_p, p, q)

    grp_s = _Group(bs * ls, bs * ls, True, 1, bs)
    ffn_hist_s = jnp.swapaxes(state_ffn_conv, 1, 2).reshape(depth, 1, (FFN_K - 1) * bs, 2 * D_FF)
    state_s = (state_gla[0], state_gdn[0], state_gdn_conv[0], state_ssd[0], state_ssd_conv[0],
               state_s5_re.reshape(1, bs, S5_STATE), state_s5_im.reshape(1, bs, S5_STATE), ffn_hist_s)
    x_s = jnp.swapaxes(x_sample, 0, 1).reshape(ls * bs, d)
    y_s, new_s = _trunk(x_s, mods_s, grp_s, (bs, SAMPLE_PAD, ls), state_s, p, q)
    y_s = jnp.swapaxes(y_s.reshape(ls, bs, d), 0, 1)

    ffn_p = jnp.stack([new_p['ffn0'], new_p['ffn1']])
    ffn_s = jnp.stack([jnp.swapaxes(new_s[k].reshape(FFN_K - 1, bs, 2 * D_FF), 0, 1) for k in ('ffn0', 'ffn1')])
    s5_shape = lambda t, nb: t.reshape(1, nb, S5_G, S5_P)
    return (y_p.reshape(bp, lp, d), y_s,
            new_p['gla'][None], new_s['gla'][None], new_p['gdn'][None], new_s['gdn'][None],
            new_p['gdnc'][None], new_s['gdnc'][None], new_p['ssd'][None], new_s['ssd'][None],
            new_p['ssdc'][None], new_s['ssdc'][None],
            s5_shape(new_p['re'], bp), s5_shape(new_s['re'], bs), s5_shape(new_p['im'], bp), s5_shape(new_s['im'], bs),
            ffn_p, ffn_s)
```

```python
import functools
import math

import jax
import jax.numpy as jnp
from jax import lax
from jax.experimental import pallas as pl
from jax.experimental.pallas import tpu as pltpu

F32 = jnp.float32
BF16 = jnp.bfloat16
HIGHEST = lax.Precision.HIGHEST
EPS = 1e-6

D_MODEL = 2048
GLA_H, GLA_DK, GLA_DV, GLA_LR = 4, 128, 256, 16
GLA_GATE_NORM = 16.0
GLA_QK, GLA_V = GLA_H * GLA_DK, GLA_H * GLA_DV
GDN_H, GDN_DK, GDN_DV = 8, 128, 128
GDN_QK, GDN_V = GDN_H * GDN_DK, GDN_H * GDN_DV
CONV_K = 4
GDN_CONV_W = 2 * GDN_QK + GDN_V
SSD_P, SSD_H, SSD_G, SSD_N = 64, 16, 2, 128
SSD_W = SSD_H * SSD_P
SSD_CONV_W = SSD_W + 2 * SSD_G * SSD_N
S5_W, S5_GS, S5_G, S5_P = 1024, 16, 64, 64
S5_STATE = S5_G * S5_P
D_FF = 5632
FFN_K = 3

LANE = 128
SUBLANE = 8
VMEM_LIMIT = 48 * 1024 * 1024

AB_Q, AB_K, AB_V, AB_R, AB_QKV, AB_G, AB_SMALL, AB_N = 0, 512, 1024, 2048, 3072, 6144, 7168, 7296
AB_LR_LANE, AB_A_LANE, AB_B_LANE = 0, 16, 24
CD_XBC, CD_SMALL, CD_Z, CD_U, CD_N = 0, 1536, 2048, 3072, 4096

MIX_BLOCK = 256
GLA_CHUNK, GDN_CHUNK, SSD_CHUNK = 16, 128, 128
GLA_GROUP = 4
FFN_TILE = 1024
FFN_SUB = 256
INV_BLOCK = 16
SAMPLE_PAD = 8
SAMPLE_SEQS = 4


def _cparams(n_axes):
    return pltpu.CompilerParams(dimension_semantics=("arbitrary",) * n_axes, vmem_limit_bytes=VMEM_LIMIT)


def _sigmoid(x):
    return 1.0 / (1.0 + jnp.exp(-x))


def _silu(x):
    return x * _sigmoid(x)


def _softplus(x):
    return jnp.maximum(x, 0.0) + jnp.log(1.0 + jnp.exp(-jnp.abs(x)))


def _gelu_tanh(x):
    return 0.5 * x * (1.0 + jnp.tanh(math.sqrt(2.0 / math.pi) * (x + 0.044715 * (x * x * x))))


def _dot(a, b, precision=None):
    return jnp.dot(a, b, precision=precision, preferred_element_type=F32)


def _dot_nt(a, b, precision=None):
    return lax.dot_general(a, b, (((1,), (1,)), ((), ())), precision=precision, preferred_element_type=F32)


def _dot_tn(a, b, precision=None):
    return lax.dot_general(a, b, (((0,), (0,)), ((), ())), precision=precision, preferred_element_type=F32)


def _head_rms(o, g):
    return o * lax.rsqrt(jnp.mean(o * o, -1, keepdims=True) + EPS) * g


class _Group:
    def __init__(self, n_tok, tile, per_token_mod, tiles_per_seq, conv_shift):
        self.n_tok = n_tok
        self.tile = tile
        self.n_tiles = n_tok // tile
        self.per_token_mod = per_token_mod
        self.tiles_per_seq = tiles_per_seq
        self.conv_shift = conv_shift

    def retiled(self, tile):
        seq_rows = self.tile * self.tiles_per_seq
        if seq_rows % tile or self.n_tok % tile:
            return self
        return _Group(self.n_tok, tile, self.per_token_mod, seq_rows // tile, self.conv_shift)

    def mod_spec(self, width, col_block, m_axis):
        if self.per_token_mod:
            return pl.BlockSpec((1, self.tile, width), lambda *g: (0, g[m_axis], col_block(*g)))
        tps = self.tiles_per_seq
        return pl.BlockSpec((1, 1, width), lambda *g: (g[m_axis] // tps, 0, col_block(*g)))


def _ada_kernel(c_ref, w_ref, b_ref, o_ref):
    cs = _silu(c_ref[...]).astype(BF16)
    o_ref[0] = _dot(cs, w_ref[0].astype(BF16)) + b_ref[0]


def _ada_mod(c, w_ada, b_ada):
    depth, d, n = w_ada.shape
    rows = c.shape[0]
    tn = 1024
    return pl.pallas_call(
        _ada_kernel,
        grid=(depth, n // tn),
        in_specs=[pl.BlockSpec((rows, d), lambda l, j: (0, 0)),
                  pl.BlockSpec((1, d, tn), lambda l, j: (l, 0, j)),
                  pl.BlockSpec((1, 1, tn), lambda l, j: (l, 0, j))],
        out_specs=pl.BlockSpec((1, rows, tn), lambda l, j: (l, 0, j)),
        out_shape=jax.ShapeDtypeStruct((depth, rows, n), F32),
        compiler_params=_cparams(2), name="ada_mod",
    )(c, w_ada, b_ada.reshape(depth, 1, n))


def _norm_mod_kernel(x_ref, g_ref, sc_ref, sh_ref, o_ref):
    x = x_ref[...]
    y = x * lax.rsqrt(jnp.mean(x * x, -1, keepdims=True) + EPS) * g_ref[...]
    o_ref[...] = (y * (1.0 + sc_ref[0]) + sh_ref[0]).astype(BF16)


def _norm_mod(x, g, mod, grp, sc_blk, sh_blk):
    d = x.shape[1]
    return pl.pallas_call(
        _norm_mod_kernel,
        grid=(grp.n_tiles,),
        in_specs=[pl.BlockSpec((grp.tile, d), lambda i: (i, 0)),
                  pl.BlockSpec((1, d), lambda i: (0, 0)),
                  grp.mod_spec(d, lambda i: sc_blk, 0),
                  grp.mod_spec(d, lambda i: sh_blk, 0)],
        out_specs=pl.BlockSpec((grp.tile, d), lambda i: (i, 0)),
        out_shape=jax.ShapeDtypeStruct(x.shape, BF16),
        compiler_params=_cparams(1), name="norm_mod",
    )(x, g.reshape(1, d), mod, mod)


def _rms_kernel(x_ref, g_ref, o_ref):
    x = x_ref[...]
    o_ref[...] = x * lax.rsqrt(jnp.mean(x * x, -1, keepdims=True) + EPS) * g_ref[...]


def _final_rms(x, g, grp):
    d = x.shape[1]
    return pl.pallas_call(
        _rms_kernel,
        grid=(grp.n_tiles,),
        in_specs=[pl.BlockSpec((grp.tile, d), lambda i: (i, 0)), pl.BlockSpec((1, d), lambda i: (0, 0))],
        out_specs=pl.BlockSpec((grp.tile, d), lambda i: (i, 0)),
        out_shape=jax.ShapeDtypeStruct(x.shape, F32),
        compiler_params=_cparams(1), name="final_rms",
    )(x, g.reshape(1, d))


def _matmul_kernel(a_ref, w_ref, o_ref):
    o_ref[...] = _dot(a_ref[...], w_ref[...])


def _matmul(a, w, grp, tn):
    k, n = w.shape
    return pl.pallas_call(
        _matmul_kernel,
        grid=(n // tn, grp.n_tiles),
        in_specs=[pl.BlockSpec((grp.tile, k), lambda j, i: (i, 0)),
                  pl.BlockSpec((k, tn), lambda j, i: (0, j))],
        out_specs=pl.BlockSpec((grp.tile, tn), lambda j, i: (i, j)),
        out_shape=jax.ShapeDtypeStruct((a.shape[0], n), F32),
        compiler_params=_cparams(2), name="matmul",
    )(a, w)


def _mm_res_kernel(*refs, n_pairs):
    x_ref, gate_ref, o_ref = refs[2 * n_pairs:]
    y = _dot(refs[0][...].astype(BF16), refs[1][0])
    for p in range(1, n_pairs):
        y = y + _dot(refs[2 * p][...].astype(BF16), refs[2 * p + 1][0])
    o_ref[...] = x_ref[...] + gate_ref[0] * y


def _mm_residual(pairs, x, mod, grp, gate_blk, tn):
    n = x.shape[1]
    in_specs, args = [], []
    for a, w, layer, row_blk in pairs:
        k = a.shape[1]
        in_specs += [pl.BlockSpec((grp.tile, k), lambda j, i: (i, 0)),
                     pl.BlockSpec((1, k, tn), lambda j, i, layer=layer, row_blk=row_blk: (layer, row_blk, j))]
        args += [a, w]
    in_specs += [pl.BlockSpec((grp.tile, tn), lambda j, i: (i, j)),
                 grp.mod_spec(tn, lambda j, i: gate_blk * (n // tn) + j, 1)]
    return pl.pallas_call(
        functools.partial(_mm_res_kernel, n_pairs=len(pairs)),
        grid=(n // tn, grp.n_tiles),
        in_specs=in_specs,
        out_specs=pl.BlockSpec((grp.tile, tn), lambda j, i: (i, j)),
        out_shape=jax.ShapeDtypeStruct(x.shape, F32),
        compiler_params=_cparams(2), name="mm_residual",
    )(*args, x, mod)


def _ffn_up_kernel(h_ref, wa_ref, wg_ref, cwa_ref, cwg_ref, cba_ref, cbg_ref, ha_ref, hg_ref,
                   act_ref, sta_ref, stg_ref, scr_a, scr_g, *, shift, tile, sub, tiles_per_seq):
    i = pl.program_id(1)
    hist = (FFN_K - 1) * shift
    base = -(-hist // SUBLANE) * SUBLANE

    @pl.when(i % tiles_per_seq == 0)
    def _():
        scr_a[base - hist:base, :] = ha_ref[0]
        scr_g[base - hist:base, :] = hg_ref[0]

    def conv(scr, cw_ref, cb_ref, r0):
        y = cb_ref[0]
        for j in range(FFN_K):
            lo = base + r0 - (FFN_K - 1 - j) * shift
            y = y + scr[lo:lo + sub, :] * cw_ref[0, j:j + 1, :]
        return y

    def project(r0):
        h = h_ref[r0:r0 + sub, :]
        scr_a[base + r0:base + r0 + sub, :] = _dot(h, wa_ref[0])
        scr_g[base + r0:base + r0 + sub, :] = _dot(h, wg_ref[0])

    project(0)
    for r0 in range(0, tile, sub):
        if r0 + sub < tile:
            project(r0 + sub)
        a = conv(scr_a, cwa_ref, cba_ref, r0)
        g = conv(scr_g, cwg_ref, cbg_ref, r0)
        act_ref[r0:r0 + sub, :] = (_silu(g) * a).astype(BF16)
    last_a = scr_a[base + tile - hist:base + tile, :]
    last_g = scr_g[base + tile - hist:base + tile, :]
    sta_ref[0] = last_a
    stg_ref[0] = last_g
    scr_a[base - hist:base, :] = last_a
    scr_g[base - hist:base, :] = last_g


def _ffn_up(h, w_up, conv_w, conv_b, layer, hist0, grp):
    d = h.shape[1]
    tn = 512
    nj = D_FF // tn
    shift = grp.conv_shift
    hist = (FFN_K - 1) * shift
    base = -(-hist // SUBLANE) * SUBLANE
    n_seq = grp.n_tiles // grp.tiles_per_seq
    tps = grp.tiles_per_seq
    cb = conv_b.reshape(conv_b.shape[0], 1, 2 * D_FF)
    kern = functools.partial(_ffn_up_kernel, shift=shift, tile=grp.tile, sub=min(FFN_SUB, grp.tile),
                             tiles_per_seq=tps)
    assert grp.tile % min(FFN_SUB, grp.tile) == 0
    act, st_a, st_g = pl.pallas_call(
        kern,
        grid=(nj, grp.n_tiles),
        in_specs=[pl.BlockSpec((grp.tile, d), lambda j, i: (i, 0)),
                  pl.BlockSpec((1, d, tn), lambda j, i: (layer, 0, j)),
                  pl.BlockSpec((1, d, tn), lambda j, i: (layer, 0, nj + j)),
                  pl.BlockSpec((1, FFN_K, tn), lambda j, i: (layer, 0, j)),
                  pl.BlockSpec((1, FFN_K, tn), lambda j, i: (layer, 0, nj + j)),
                  pl.BlockSpec((1, 1, tn), lambda j, i: (layer, 0, j)),
                  pl.BlockSpec((1, 1, tn), lambda j, i: (layer, 0, nj + j)),
                  pl.BlockSpec((1, hist, tn), lambda j, i: (i // tps, 0, j)),
                  pl.BlockSpec((1, hist, tn), lambda j, i: (i // tps, 0, nj + j))],
        out_specs=[pl.BlockSpec((grp.tile, tn), lambda j, i: (i, j)),
                   pl.BlockSpec((1, hist, tn), lambda j, i: (i // tps, 0, j)),
                   pl.BlockSpec((1, hist, tn), lambda j, i: (i // tps, 0, j))],
        out_shape=[jax.ShapeDtypeStruct((h.shape[0], D_FF), BF16),
                   jax.ShapeDtypeStruct((n_seq, hist, D_FF), F32),
                   jax.ShapeDtypeStruct((n_seq, hist, D_FF), F32)],
        scratch_shapes=[pltpu.VMEM((base + grp.tile, tn), F32), pltpu.VMEM((base + grp.tile, tn), F32)],
        compiler_params=_cparams(2), name="ffn_up",
    )(h, w_up, w_up, conv_w, conv_w, cb, cb, hist0, hist0)
    return act, jnp.concatenate([st_a, st_g], axis=-1)


def _glu_kernel(y_ref, w_ref, b_ref, o_ref):
    z5 = _gelu_tanh(y_ref[...])
    o_ref[...] = z5 * _sigmoid(_dot(z5.astype(BF16), w_ref[...]) + b_ref[...])


def _s5_glu(yd, w, b, grp):
    n = yd.shape[1]
    return pl.pallas_call(
        _glu_kernel,
        grid=(grp.n_tiles,),
        in_specs=[pl.BlockSpec((grp.tile, n), lambda i: (i, 0)),
                  pl.BlockSpec((n, n), lambda i: (0, 0)),
                  pl.BlockSpec((1, n), lambda i: (0, 0))],
        out_specs=pl.BlockSpec((grp.tile, n), lambda i: (i, 0)),
        out_shape=jax.ShapeDtypeStruct(yd.shape, F32),
        compiler_params=_cparams(1), name="s5_glu",
    )(yd, w, b.reshape(1, n))


def _causal_conv_chunk(x, cv_scr, cw_ref, cb_ref, chunk, valid):
    base = SUBLANE
    cv_scr[base:base + chunk, :] = x
    y = cb_ref[...]
    for j in range(CONV_K):
        lo = base - (CONV_K - 1) + j
        y = y + cv_scr[lo:lo + chunk, :] * cw_ref[j:j + 1, :]
    last = cv_scr[base + valid - (CONV_K - 1):base + valid, :]
    cv_scr[base - (CONV_K - 1):base, :] = last
    return y


def _gla_kernel(q_ref, k_ref, v_ref, r_ref, sm_ref, w2_ref, b2_ref, ng_ref, s0_ref,
                o_ref, sout_ref, s_scr, *, chunk, group, block, valid, nseq):
    blk = pl.program_id(1)

    @pl.when(blk == 0)
    def _():
        s_scr[...] = s0_ref[...]

    span = chunk * group
    row3 = lax.broadcasted_iota(jnp.int32, (1, chunk, 1), 1)
    row_in_chunk = lax.broadcasted_iota(jnp.int32, (span, 1), 0) % chunk
    ri = lax.broadcasted_iota(jnp.int32, (span, span), 0)
    ci = lax.broadcasted_iota(jnp.int32, (span, span), 1)
    tri = ((ri // chunk == ci // chunk) & (ri >= ci)).astype(F32)
    eye = (lax.broadcasted_iota(jnp.int32, (GLA_DK, GLA_DK), 0)
           == lax.broadcasted_iota(jnp.int32, (GLA_DK, GLA_DK), 1))
    n_valid = min(valid, chunk)
    heads = range(GLA_H)
    chunks = range(group)

    def to3(t):
        return t.reshape(group, chunk, t.shape[-1])

    def one_seq(sq, rows):
        x = _dot(sm_ref[sq, rows, :], w2_ref[...], HIGHEST) + b2_ref[...]
        log_a = (jnp.minimum(x, 0.0) - jnp.log(1.0 + jnp.exp(-jnp.abs(x)))) * (1.0 / GLA_GATE_NORM)
        if valid < chunk:
            log_a = jnp.where(row_in_chunk < valid, log_a, 0.0)
        b = _dot(tri, log_a, HIGHEST)
        q = q_ref[sq, rows, :] * GLA_DK ** -0.5
        k = k_ref[sq, rows, :]
        v = v_ref[sq, rows, :]
        r = r_ref[sq, rows, :]
        b3, q_in, kv, d_col = [], [], [], []
        for h in heads:
            ks = slice(h * GLA_DK, (h + 1) * GLA_DK)
            bh3 = to3(b[:, ks])
            b_last = bh3[:, chunk - 1:chunk, :]
            k_out = (to3(k[:, ks]) * jnp.exp(b_last - bh3)).reshape(span, GLA_DK)
            vh = v[:, h * GLA_DV:(h + 1) * GLA_DV]
            b3.append(bh3)
            q_in.append(q[:, ks] * jnp.exp(b[:, ks]))
            kv.append([_dot_tn(k_out[c * chunk:(c + 1) * chunk], vh[c * chunk:(c + 1) * chunk]) for c in chunks])
            d_col.append([jnp.sum(jnp.where(eye, jnp.exp(b_last[c]), 0.0), axis=-1, keepdims=True) for c in chunks])
        st = [s_scr[sq, h] for h in heads]
        o_inter = [[] for _ in heads]
        for c in chunks:
            for h in heads:
                o_inter[h].append(_dot(q_in[h][c * chunk:(c + 1) * chunk], st[h]))
                st[h] = st[h] * d_col[h][c] + kv[h][c]
        for h in heads:
            ks = slice(h * GLA_DK, (h + 1) * GLA_DK)
            vs = slice(h * GLA_DV, (h + 1) * GLA_DV)
            s_scr[sq, h] = st[h]
            bh3, qh3, kh3, vh3 = b3[h], to3(q[:, ks]), to3(k[:, ks]), to3(v[:, vs])
            o3 = jnp.zeros((group, chunk, GLA_DV), F32)
            for j in range(n_valid):
                e = jnp.exp(jnp.minimum(bh3 - bh3[:, j:j + 1, :], 0.0))
                sj = jnp.sum(qh3 * e * kh3[:, j:j + 1, :], axis=-1, keepdims=True)
                o3 = o3 + jnp.where(row3 >= j, sj, 0.0) * vh3[:, j:j + 1, :]
            o = o3.reshape(span, GLA_DV) + jnp.concatenate(o_inter[h], axis=0)
            o_ref[sq, rows, vs] = _head_rms(o, ng_ref[...]) * _silu(r[:, vs])

    def do_span(s, carry):
        rows = pl.ds(pl.multiple_of(s * span, span), span)
        for sq in range(nseq):
            one_seq(sq, rows)
        return carry

    lax.fori_loop(0, block // span, do_span, 0)

    @pl.when(blk == pl.num_programs(1) - 1)
    def _():
        sout_ref[...] = s_scr[...]


def _gla(proj, w2p, b2, ng, s0, chunk, group, block, valid, nseq):
    nb, seq, _ = proj.shape

    def col(width, off):
        return pl.BlockSpec((nseq, block, width), lambda b, i: (b, i, off // width))

    st_spec = pl.BlockSpec((nseq, GLA_H, GLA_DK, GLA_DV), lambda b, i: (b, 0, 0, 0))
    kern = functools.partial(_gla_kernel, chunk=chunk, group=group, block=block, valid=valid, nseq=nseq)
    return pl.pallas_call(
        kern,
        grid=(nb // nseq, seq // block),
        in_specs=[col(GLA_QK, AB_Q), col(GLA_QK, AB_K), col(GLA_V, AB_V), col(GLA_V, AB_R), col(LANE, AB_SMALL),
                  pl.BlockSpec((LANE, GLA_QK), lambda b, i: (0, 0)),
                  pl.BlockSpec((1, GLA_QK), lambda b, i: (0, 0)),
                  pl.BlockSpec((1, GLA_DV), lambda b, i: (0, 0)),
                  st_spec],
        out_specs=[pl.BlockSpec((nseq, block, GLA_V), lambda b, i: (b, i, 0)), st_spec],
        out_shape=[jax.ShapeDtypeStruct((nb, seq, GLA_V), F32),
                   jax.ShapeDtypeStruct((nb, GLA_H, GLA_DK, GLA_DV), F32)],
        scratch_shapes=[pltpu.VMEM((nseq, GLA_H, GLA_DK, GLA_DV), F32)],
        compiler_params=_cparams(2), name="gla",
    )(proj, proj, proj, proj, proj, w2p, b2.reshape(1, GLA_QK), ng.reshape(1, GLA_DV), s0)


def _split2(a):
    hi = a.astype(BF16)
    return hi, (a - hi.astype(F32)).astype(BF16)


def _dot3(a, b):
    return _dot(a[0], b[0]) + _dot(a[0], b[1]) + _dot(a[1], b[0])


def _inv_unit_lower_many(mats, n, eye):
    ps = [eye - a for a in mats]
    if n <= 2:
        return ps
    pows = [_split2(a) for a in mats]
    k = 2
    pending = None
    while k < n:
        sq = [_dot3(a, a) for a in pows]
        if pending is not None:
            ps = [p + _dot3(_split2(p), f) for p, f in zip(ps, pending)]
        pows = [_split2(a) for a in sq]
        pending = pows
        k *= 2
    return [p + _dot3(_split2(p), f) for p, f in zip(ps, pending)]


def _gdn_kernel(qkv_ref, sm_ref, gb_ref, cw_ref, cb_ref, alog_ref, dtb_ref, ng_ref, s0_ref, c0_ref,
                o_ref, sout_ref, cout_ref, s_scr, cv_scr, *, chunk, block, valid, nseq):
    blk = pl.program_id(1)

    @pl.when(blk == 0)
    def _():
        s_scr[...] = s0_ref[...]
        cv_scr[:, SUBLANE - (CONV_K - 1):SUBLANE, :] = c0_ref[...]

    ri = lax.broadcasted_iota(jnp.int32, (chunk, chunk), 0)
    ci = lax.broadcasted_iota(jnp.int32, (chunk, chunk), 1)
    causal = ri >= ci
    strict = ri > ci
    eye = (ri == ci).astype(F32)
    tri = causal.astype(F32)
    tri_u = (ri <= ci).astype(F32)
    inv_blk = min(INV_BLOCK, chunk)
    same_blk = (ri // inv_blk) == (ci // inv_blk)
    row = lax.broadcasted_iota(jnp.int32, (chunk, 1), 0)
    n_valid = min(valid, chunk)

    def do_chunk(s, carry):
        rows = pl.ds(pl.multiple_of(s * chunk, chunk), chunk)
        units = [(sq, h) for sq in range(nseq) for h in range(GDN_H)]
        heads = range(len(units))
        q, k, kb, rhs, dec, gcc = [], [], [], [], [], []
        for sq, h in units:
            if h == 0:
                act = _silu(_causal_conv_chunk(qkv_ref[sq, rows, :], cv_scr.at[sq], cw_ref, cb_ref, chunk, n_valid))
                sm = sm_ref[sq, rows, :]
                g_all = -jnp.exp(alog_ref[...]) * _softplus(sm + dtb_ref[...])
                beta_all = _sigmoid(sm)
                if valid < chunk:
                    g_all = jnp.where(row < valid, g_all, 0.0)
                    beta_all = jnp.where(row < valid, beta_all, 0.0)
                gc = _dot(tri, g_all, HIGHEST)
                gc_r = _dot_tn(g_all, tri_u, HIGHEST)
            qh = act[:, h * GDN_DK:(h + 1) * GDN_DK]
            kh = act[:, GDN_QK + h * GDN_DK:GDN_QK + (h + 1) * GDN_DK]
            vh = act[:, 2 * GDN_QK + h * GDN_DV:2 * GDN_QK + (h + 1) * GDN_DV]
            qh = qh * lax.rsqrt(jnp.sum(qh * qh, -1, keepdims=True) + EPS) * GDN_DK ** -0.5
            kh = kh * lax.rsqrt(jnp.sum(kh * kh, -1, keepdims=True) + EPS)
            beta = beta_all[:, AB_B_LANE + h:AB_B_LANE + h + 1]
            gch = gc[:, AB_A_LANE + h:AB_A_LANE + h + 1]
            gcr = gc_r[AB_A_LANE + h:AB_A_LANE + h + 1, :]
            q.append(qh)
            k.append(kh)
            kb.append(kh * beta)
            rhs.append(_split2(jnp.concatenate([vh * beta, kb[-1] * jnp.exp(gch)], axis=1)))
            dec.append(jnp.exp(jnp.where(causal, gch - gcr, -jnp.inf)))
            gcc.append(gch)
        kbf = [a.astype(BF16) for a in k]
        m = [jnp.where(strict, _dot_nt(kb[h].astype(BF16), kbf[h]) * dec[h], 0.0) for h in heads]
        att = [_dot_nt(q[h].astype(BF16), kbf[h]) * dec[h] for h in heads]
        m_diag = [jnp.where(same_blk, a, 0.0) for a in m]
        t = [_split2(a) for a in _inv_unit_lower_many(m_diag, inv_blk, eye)]
        y = [_dot3(t[h], rhs[h]) for h in heads]
        if chunk > inv_blk:
            n_off = [_dot3(t[h], _split2(m[h] - m_diag[h])) for h in heads]
            qn = [_split2(a) for a in _inv_unit_lower_many(n_off, chunk // inv_blk, eye)]
            y = [_dot3(qn[h], _split2(y[h])) for h in heads]
        st = [s_scr[sq, h] for sq, h in units]
        stb = [a.astype(BF16) for a in st]
        v_new = [y[h][:, :GDN_DV] - _dot(y[h][:, GDN_DV:].astype(BF16), stb[h]) for h in heads]
        vnb = [a.astype(BF16) for a in v_new]
        o = [_dot((q[h] * jnp.exp(gcc[h])).astype(BF16), stb[h]) + _dot(att[h].astype(BF16), vnb[h]) for h in heads]
        for u, (sq, h) in enumerate(units):
            hs = slice(h * GDN_DV, (h + 1) * GDN_DV)
            g_last = gcc[u][chunk - 1:chunk, :]
            k_out = (k[u] * jnp.exp(g_last - gcc[u])).astype(BF16)
            s_scr[sq, h] = st[u] * jnp.exp(g_last) + _dot_tn(k_out, vnb[u])
            o_ref[sq, rows, hs] = _head_rms(o[u], ng_ref[...]) * _silu(gb_ref[sq, rows, hs])
        return carry

    lax.fori_loop(0, block // chunk, do_chunk, 0)

    @pl.when(blk == pl.num_programs(1) - 1)
    def _():
        sout_ref[...] = s_scr[...]
        cout_ref[...] = cv_scr[:, SUBLANE - (CONV_K - 1):SUBLANE, :]


def _gdn(proj, conv_w, conv_b, alog_row, dtb_row, ng, s0, c0, chunk, block, valid, nseq):
    nb, seq, _ = proj.shape

    def col(width, off):
        return pl.BlockSpec((nseq, block, width), lambda b, i: (b, i, off // width))

    def const(shape):
        return pl.BlockSpec(shape, lambda b, i: (0,) * len(shape))

    st_spec = pl.BlockSpec((nseq, GDN_H, GDN_DK, GDN_DV), lambda b, i: (b, 0, 0, 0))
    cv_spec = pl.BlockSpec((nseq, CONV_K - 1, GDN_CONV_W), lambda b, i: (b, 0, 0))
    kern = functools.partial(_gdn_kernel, chunk=chunk, block=block, valid=valid, nseq=nseq)
    return pl.pallas_call(
        kern,
        grid=(nb // nseq, seq // block),
        in_specs=[col(GDN_CONV_W, AB_QKV), col(LANE, AB_SMALL), col(GDN_V, AB_G),
                  const((CONV_K, GDN_CONV_W)), const((1, GDN_CONV_W)), const((1, LANE)), const((1, LANE)),
                  const((1, GDN_DV)), st_spec, cv_spec],
        out_specs=[pl.BlockSpec((nseq, block, GDN_V), lambda b, i: (b, i, 0)), st_spec, cv_spec],
        out_shape=[jax.ShapeDtypeStruct((nb, seq, GDN_V), F32),
                   jax.ShapeDtypeStruct((nb, GDN_H, GDN_DK, GDN_DV), F32),
                   jax.ShapeDtypeStruct((nb, CONV_K - 1, GDN_CONV_W), F32)],
        scratch_shapes=[pltpu.VMEM((nseq, GDN_H, GDN_DK, GDN_DV), F32),
                        pltpu.VMEM((nseq, SUBLANE + chunk, GDN_CONV_W), F32)],
        compiler_params=_cparams(2), name="gdn",
    )(proj, proj, proj, conv_w, conv_b.reshape(1, GDN_CONV_W), alog_row, dtb_row, ng.reshape(1, GDN_DV), s0, c0)


def _ssd_kernel(xbc_ref, sm_ref, z_ref, cw_ref, cb_ref, alog_ref, dtb_ref, drow_ref, ng_ref, s0_ref, c0_ref,
                o_ref, sout_ref, cout_ref, s_scr, cv_scr, *, chunk, block, valid, nseq):
    blk = pl.program_id(1)

    @pl.when(blk == 0)
    def _():
        s_scr[...] = s0_ref[...]
        cv_scr[:, SUBLANE - (CONV_K - 1):SUBLANE, :] = c0_ref[...]

    ri = lax.broadcasted_iota(jnp.int32, (chunk, chunk), 0)
    ci = lax.broadcasted_iota(jnp.int32, (chunk, chunk), 1)
    causal = ri >= ci
    eye = (ri == ci).astype(F32)
    tri = causal.astype(F32)
    tri_u = (ri <= ci).astype(F32)
    row = lax.broadcasted_iota(jnp.int32, (chunk, 1), 0)
    lane_lo = lax.broadcasted_iota(jnp.int32, (chunk, LANE), 1) < SSD_P
    row_lo = lax.broadcasted_iota(jnp.int32, (2 * SSD_P, 1), 0) < SSD_P
    n_valid = min(valid, chunk)
    heads_per_group = SSD_H // SSD_G
    gsz = SSD_W // SSD_G

    def one_seq(sq, rows):
        act = _silu(_causal_conv_chunk(xbc_ref[sq, rows, :], cv_scr.at[sq], cw_ref, cb_ref, chunk, n_valid))
        dt = _softplus(sm_ref[sq, rows, :] + dtb_ref[...])
        if valid < chunk:
            dt = jnp.where(row < valid, dt, 0.0)
        dta = dt * (-jnp.exp(alog_ref[...]))
        acs = _dot(tri, dta, HIGHEST)
        acs_r = _dot_tn(dta, tri_u, HIGHEST)
        dt_r = _dot_tn(dt, eye, HIGHEST)
        z = z_ref[sq, rows, :]
        for g in range(SSD_G):
            bg = act[:, SSD_W + g * SSD_N:SSD_W + (g + 1) * SSD_N]
            cg = act[:, SSD_W + SSD_G * SSD_N + g * SSD_N:SSD_W + SSD_G * SSD_N + (g + 1) * SSD_N]
            cb = _dot_nt(cg, bg)
            parts = []
            for pr in range(heads_per_group // 2):
                pi = g * (heads_per_group // 2) + pr
                xp = act[:, pi * LANE:(pi + 1) * LANE]
                st = s_scr[sq, pi]
                y_in, e_in, w_out, d_last = [], [], [], []
                for hh in range(2):
                    h = 2 * pi + hh
                    ac = acs[:, h:h + 1]
                    dec = jnp.exp(jnp.where(causal, ac - acs_r[h:h + 1, :], -jnp.inf))
                    y_in.append(_dot(cb * dec * dt_r[h:h + 1, :], xp))
                    a_last = ac[chunk - 1:chunk, :]
                    e_in.append(jnp.exp(ac))
                    w_out.append(jnp.exp(a_last - ac) * dt[:, h:h + 1])
                    d_last.append(jnp.exp(a_last))
                y = jnp.where(lane_lo, y_in[0], y_in[1])
                y = y + _dot_nt(cg, st) * jnp.where(lane_lo, e_in[0], e_in[1])
                y = y + drow_ref[:, pi * LANE:(pi + 1) * LANE] * xp
                x_sc = xp * jnp.where(lane_lo, w_out[0], w_out[1])
                s_scr[sq, pi] = st * jnp.where(row_lo, d_last[0], d_last[1]) + _dot_tn(x_sc, bg)
                parts.append(y)
            gs = slice(g * gsz, (g + 1) * gsz)
            yg = jnp.concatenate(parts, axis=1) * _silu(z[:, gs])
            o_ref[sq, rows, gs] = _head_rms(yg, ng_ref[:, gs])

    def do_chunk(s, carry):
        rows = pl.ds(pl.multiple_of(s * chunk, chunk), chunk)
        for sq in range(nseq):
            one_seq(sq, rows)
        return carry

    lax.fori_loop(0, block // chunk, do_chunk, 0)

    @pl.when(blk == pl.num_programs(1) - 1)
    def _():
        sout_ref[...] = s_scr[...]
        cout_ref[...] = cv_scr[:, SUBLANE - (CONV_K - 1):SUBLANE, :]


def _ssd(proj, conv_w, conv_b, alog_row, dtb_row, d_row, ng, s0, c0, chunk, block, valid, nseq):
    nb, seq, _ = proj.shape
    n_pairs = SSD_H // 2

    def col(width, off):
        return pl.BlockSpec((nseq, block, width), lambda b, i: (b, i, off // width))

    def const(shape):
        return pl.BlockSpec(shape, lambda b, i: (0,) * len(shape))

    st_spec = pl.BlockSpec((nseq, n_pairs, 2 * SSD_P, SSD_N), lambda b, i: (b, 0, 0, 0))
    cv_spec = pl.BlockSpec((nseq, CONV_K - 1, SSD_CONV_W), lambda b, i: (b, 0, 0))
    kern = functools.partial(_ssd_kernel, chunk=chunk, block=block, valid=valid, nseq=nseq)
    o, s_new, c_new = pl.pallas_call(
        kern,
        grid=(nb // nseq, seq // block),
        in_specs=[col(SSD_CONV_W, CD_XBC), col(LANE, CD_SMALL), col(SSD_W, CD_Z),
                  const((CONV_K, SSD_CONV_W)), const((1, SSD_CONV_W)), const((1, LANE)), const((1, LANE)),
                  const((1, SSD_W)), const((1, SSD_W)), st_spec, cv_spec],
        out_specs=[pl.BlockSpec((nseq, block, SSD_W), lambda b, i: (b, i, 0)), st_spec, cv_spec],
        out_shape=[jax.ShapeDtypeStruct((nb, seq, SSD_W), F32),
                   jax.ShapeDtypeStruct((nb, n_pairs, 2 * SSD_P, SSD_N), F32),
                   jax.ShapeDtypeStruct((nb, CONV_K - 1, SSD_CONV_W), F32)],
        scratch_shapes=[pltpu.VMEM((nseq, n_pairs, 2 * SSD_P, SSD_N), F32),
                        pltpu.VMEM((nseq, SUBLANE + chunk, SSD_CONV_W), F32)],
        compiler_params=_cparams(2), name="ssd",
    )(proj, proj, proj, conv_w, conv_b.reshape(1, SSD_CONV_W), alog_row, dtb_row, d_row,
      ng.reshape(1, SSD_W), s0.reshape(nb, n_pairs, 2 * SSD_P, SSD_N), c0)
    return o, s_new.reshape(nb, SSD_H, SSD_P, SSD_N), c_new


def _s5_prep_kernel(are_ref, aim_ref, ldt_ref, bre_ref, bim_ref, lbr_ref, lbi_ref, bbr_ref, bbi_ref):
    a_re, a_im = are_ref[...], aim_ref[...]
    dt = jnp.exp(ldt_ref[...])
    mag = jnp.exp(a_re * dt)
    lb_re, lb_im = mag * jnp.cos(a_im * dt), mag * jnp.sin(a_im * dt)
    nr, ni = lb_re - 1.0, lb_im
    den = a_re * a_re + a_im * a_im
    f_re = (nr * a_re + ni * a_im) / den
    f_im = (ni * a_re - nr * a_im) / den
    b_re, b_im = bre_ref[...], bim_ref[...]
    lbr_ref[...] = lb_re
    lbi_ref[...] = lb_im
    bbr_ref[...] = f_re * b_re - f_im * b_im
    bbi_ref[...] = f_re * b_im + f_im * b_re


def _s5_prep(a_re, a_im, log_dt, b_re, b_im):
    g3 = (S5_G, 1, S5_P)
    b3 = (S5_G, S5_GS, S5_P)
    return pl.pallas_call(
        _s5_prep_kernel,
        out_shape=[jax.ShapeDtypeStruct(g3, F32), jax.ShapeDtypeStruct(g3, F32),
                   jax.ShapeDtypeStruct(b3, F32), jax.ShapeDtypeStruct(b3, F32)],
        name="s5_prep",
    )(a_re.reshape(g3), a_im.reshape(g3), log_dt.reshape(S5_G, 1, 1),
      jnp.swapaxes(b_re, 1, 2), jnp.swapaxes(b_im, 1, 2))


def _block_diag(blocks):
    g, r, c = blocks.shape
    per = 8
    b = blocks.reshape(g // per, per, r, 1, c) * jnp.eye(per, dtype=blocks.dtype).reshape(1, per, 1, per, 1)
    return b.reshape(g // per, per * r, per * c)


def _s5_kernel(u_ref, wre_ref, wim_ref, cre_ref, cim_ref, lbr_ref, lbi_ref, d_ref, x0r_ref, x0i_ref,
               y_ref, xfr_ref, xfi_ref, xr_scr, xi_scr, sr_scr, si_scr, *, rows_per_step, steps):
    c = pl.program_id(2)

    @pl.when(c == 0)
    def _():
        xr_scr[...] = x0r_ref[0]
        xi_scr[...] = x0i_ref[0]

    u = u_ref[0]
    sr_scr[...] = _dot(u, wre_ref[0])
    si_scr[...] = _dot(u, wim_ref[0])
    l_re, l_im = lbr_ref[...], lbi_ref[...]

    def step(t, carry):
        rows = pl.ds(pl.multiple_of(t * rows_per_step, rows_per_step), rows_per_step)
        xr, xi = xr_scr[...], xi_scr[...]
        nr = l_re * xr - l_im * xi + sr_scr[rows, :]
        ni = l_re * xi + l_im * xr + si_scr[rows, :]
        xr_scr[...] = nr
        xi_scr[...] = ni
        sr_scr[rows, :] = nr
        si_scr[rows, :] = ni
        return carry

    lax.fori_loop(0, steps, step, 0)
    y_ref[0] = _dot(sr_scr[...], cre_ref[0]) - _dot(si_scr[...], cim_ref[0]) + d_ref[...] * u

    @pl.when(c == pl.num_programs(2) - 1)
    def _():
        xfr_ref[0] = xr_scr[...]
        xfi_ref[0] = xi_scr[...]


def _s5(proj, w_re, w_im, c_re, c_im, lb_re, lb_im, d, x0_re, x0_im, rows_per_step, steps):
    ng, n_tok, _ = proj.shape
    nj = S5_W // LANE
    sw = S5_STATE // nj
    cr = rows_per_step * steps
    kern = functools.partial(_s5_kernel, rows_per_step=rows_per_step, steps=steps)
    x_spec = pl.BlockSpec((1, rows_per_step, sw), lambda g, j, c: (g, 0, j))
    return pl.pallas_call(
        kern,
        grid=(ng, nj, n_tok // cr),
        in_specs=[pl.BlockSpec((1, cr, LANE), lambda g, j, c: (g, c, CD_U // LANE + j)),
                  pl.BlockSpec((1, LANE, sw), lambda g, j, c: (j, 0, 0)),
                  pl.BlockSpec((1, LANE, sw), lambda g, j, c: (j, 0, 0)),
                  pl.BlockSpec((1, sw, LANE), lambda g, j, c: (j, 0, 0)),
                  pl.BlockSpec((1, sw, LANE), lambda g, j, c: (j, 0, 0)),
                  pl.BlockSpec((1, sw), lambda g, j, c: (0, j)),
                  pl.BlockSpec((1, sw), lambda g, j, c: (0, j)),
                  pl.BlockSpec((1, LANE), lambda g, j, c: (0, j)),
                  x_spec, x_spec],
        out_specs=[pl.BlockSpec((1, cr, LANE), lambda g, j, c: (g, c, j)), x_spec, x_spec],
        out_shape=[jax.ShapeDtypeStruct((ng, n_tok, S5_W), F32),
                   jax.ShapeDtypeStruct((ng, rows_per_step, S5_STATE), F32),
                   jax.ShapeDtypeStruct((ng, rows_per_step, S5_STATE), F32)],
        scratch_shapes=[pltpu.VMEM((rows_per_step, sw), F32), pltpu.VMEM((rows_per_step, sw), F32),
                        pltpu.VMEM((cr, sw), F32), pltpu.VMEM((cr, sw), F32)],
        compiler_params=_cparams(3), name="s5_scan",
    )(proj, w_re, w_im, c_re, c_im, lb_re, lb_im, d.reshape(1, S5_W), x0_re, x0_im)


def _s5_pow_kernel(lbr_ref, lbi_ref, pr_ref, pi_ref, *, n_rows):
    l_re, l_im = lbr_ref[...], lbi_ref[...]
    row = lax.broadcasted_iota(jnp.int32, (SUBLANE, 1), 0)
    p_re, p_im = l_re, l_im
    b_re = jnp.broadcast_to(l_re, (SUBLANE, l_re.shape[1]))
    b_im = jnp.broadcast_to(l_im, (SUBLANE, l_re.shape[1]))
    for r in range(1, SUBLANE):
        p_re, p_im = p_re * l_re - p_im * l_im, p_re * l_im + p_im * l_re
        b_re = jnp.where(row >= r, p_re, b_re)
        b_im = jnp.where(row >= r, p_im, b_im)
    q_re, q_im = jnp.ones_like(l_re), jnp.zeros_like(l_re)
    for a in range(n_rows // SUBLANE):
        rows = slice(a * SUBLANE, (a + 1) * SUBLANE)
        pr_ref[rows, :] = b_re * q_re - b_im * q_im
        pi_ref[rows, :] = b_re * q_im + b_im * q_re
        q_re, q_im = q_re * p_re - q_im * p_im, q_re * p_im + q_im * p_re


def _s5_pow_table(lb_re, lb_im, n_rows):
    shape = jax.ShapeDtypeStruct((n_rows, S5_STATE), F32)
    return pl.pallas_call(functools.partial(_s5_pow_kernel, n_rows=n_rows), out_shape=[shape, shape],
                          name="s5_pow")(lb_re, lb_im)


def _s5_seg_kernel(u_ref, wre_ref, wim_ref, cre_ref, cim_ref, pr_ref, pi_ref, d_ref, x0r_ref, x0i_ref,
                   y_ref, xfr_ref, xfi_ref, sr_scr, si_scr, *, seg_len):
    n_seg = SUBLANE
    n_lane_blk = sr_scr.shape[0]
    sw = n_lane_blk * LANE
    lane_blks = [slice(c * LANE, (c + 1) * LANE) for c in range(n_lane_blk)]

    def put(scr, rows, val):
        for c, ls in enumerate(lane_blks):
            scr[c, rows, :] = val[:, ls]

    def get(scr, rows):
        return jnp.concatenate([scr[c, rows, :] for c in range(n_lane_blk)], axis=1)

    for s in range(n_seg):
        us = u_ref[0, s * seg_len:(s + 1) * seg_len, :]
        put(sr_scr, pl.ds(s, seg_len, stride=n_seg), _dot(us, wre_ref[0]))
        put(si_scr, pl.ds(s, seg_len, stride=n_seg), _dot(us, wim_ref[0]))
    l_re, l_im = pr_ref[0:1, :], pi_ref[0:1, :]

    def step(t, carry):
        xr, xi = carry
        rows = pl.ds(pl.multiple_of(t * n_seg, n_seg), n_seg)
        nr = l_re * xr - l_im * xi + get(sr_scr, rows)
        ni = l_re * xi + l_im * xr + get(si_scr, rows)
        put(sr_scr, rows, nr)
        put(si_scr, rows, ni)
        return nr, ni

    zero = jnp.zeros((n_seg, sw), F32)
    end_re, end_im = lax.fori_loop(0, seg_len, step, (zero, zero))
    ln_re, ln_im = pr_ref[seg_len - 1:seg_len, :], pi_ref[seg_len - 1:seg_len, :]
    p_re, p_im = pr_ref[...], pi_ref[...]
    x_re, x_im = x0r_ref[0], x0i_ref[0]
    for s in range(n_seg):
        loc_re = get(sr_scr, pl.ds(s, seg_len, stride=n_seg))
        loc_im = get(si_scr, pl.ds(s, seg_len, stride=n_seg))
        t_re = loc_re + p_re * x_re - p_im * x_im
        t_im = loc_im + p_re * x_im + p_im * x_re
        us = u_ref[0, s * seg_len:(s + 1) * seg_len, :]
        y_ref[0, s * seg_len:(s + 1) * seg_len, :] = (_dot(t_re, cre_ref[0]) - _dot(t_im, cim_ref[0])
                                                      + d_ref[...] * us)
        e_re, e_im = end_re[s:s + 1, :], end_im[s:s + 1, :]
        x_re, x_im = e_re + ln_re * x_re - ln_im * x_im, e_im + ln_re * x_im + ln_im * x_re
    xfr_ref[0] = x_re
    xfi_ref[0] = x_im


def _s5_seg(proj, w_re, w_im, c_re, c_im, pow_re, pow_im, d, x0_re, x0_im):
    nb, seq, _ = proj.shape
    nj = S5_W // LANE
    sw = S5_STATE // nj
    seg_len = seq // SUBLANE
    x_spec = pl.BlockSpec((1, 1, sw), lambda b, j: (b, 0, j))
    return pl.pallas_call(
        functools.partial(_s5_seg_kernel, seg_len=seg_len),
        grid=(nb, nj),
        in_specs=[pl.BlockSpec((1, seq, LANE), lambda b, j: (b, 0, CD_U // LANE + j)),
                  pl.BlockSpec((1, LANE, sw), lambda b, j: (j, 0, 0)),
                  pl.BlockSpec((1, LANE, sw), lambda b, j: (j, 0, 0)),
                  pl.BlockSpec((1, sw, LANE), lambda b, j: (j, 0, 0)),
                  pl.BlockSpec((1, sw, LANE), lambda b, j: (j, 0, 0)),
                  pl.BlockSpec((seg_len, sw), lambda b, j: (0, j)),
                  pl.BlockSpec((seg_len, sw), lambda b, j: (0, j)),
                  pl.BlockSpec((1, LANE), lambda b, j: (0, j)),
                  x_spec, x_spec],
        out_specs=[pl.BlockSpec((1, seq, LANE), lambda b, j: (b, 0, j)), x_spec, x_spec],
        out_shape=[jax.ShapeDtypeStruct((nb, seq, S5_W), F32),
                   jax.ShapeDtypeStruct((nb, 1, S5_STATE), F32),
                   jax.ShapeDtypeStruct((nb, 1, S5_STATE), F32)],
        scratch_shapes=[pltpu.VMEM((sw // LANE, seq, LANE), F32), pltpu.VMEM((sw // LANE, seq, LANE), F32)],
        compiler_params=_cparams(2), name="s5_seg",
    )(proj, w_re, w_im, c_re, c_im, pow_re, pow_im, d.reshape(1, S5_W), x0_re, x0_im)


def _lane_row(vec, lane0):
    return jnp.zeros((1, LANE), F32).at[0, lane0:lane0 + vec.shape[0]].set(vec.astype(F32))


def _prep_params(p):
    d = D_MODEL
    q = {}
    w = p['w_in_ab'][0]
    o_lr = 2 * GLA_QK + GLA_V
    o_r = o_lr + GLA_LR
    o_qkv = o_r + GLA_V
    o_a = o_qkv + GDN_CONV_W
    o_g = o_a + 2 * GDN_H
    q['w_in_ab'] = jnp.concatenate(
        [w[:, :o_lr], w[:, o_r:o_qkv], w[:, o_qkv:o_a], w[:, o_g:], w[:, o_lr:o_r], w[:, o_a:o_g],
         jnp.zeros((d, LANE - GLA_LR - 2 * GDN_H), F32)], axis=1).astype(BF16)
    q['gla_w2'] = jnp.zeros((LANE, GLA_QK), F32).at[:GLA_LR].set(p['gla_w2'][0])
    q['gdn_alog'] = _lane_row(p['gdn_A_log'][0], AB_A_LANE)
    q['gdn_dtb'] = _lane_row(p['gdn_dt_bias'][0], AB_A_LANE)
    q['w_out_ab'] = p['w_out_ab'].astype(BF16)
    w = p['w_in_cd'][0]
    o_xbc = SSD_W
    o_dt = o_xbc + SSD_CONV_W
    o_u = o_dt + SSD_H
    q['w_in_cd'] = jnp.concatenate(
        [w[:, o_xbc:o_dt], w[:, o_dt:o_u], jnp.zeros((d, CD_Z - CD_SMALL - SSD_H), F32), w[:, :o_xbc], w[:, o_u:]],
        axis=1).astype(BF16)
    q['ssd_alog'] = _lane_row(p['ssd_A_log'][0], 0)
    q['ssd_dtb'] = _lane_row(p['ssd_dt_bias'][0], 0)
    q['ssd_d_row'] = jnp.repeat(p['ssd_D'][0].astype(F32), SSD_P).reshape(1, SSD_W)
    q['w_out_cd'] = p['w_out_cd'].astype(BF16)
    lb_re, lb_im, bb_re, bb_im = _s5_prep(p['s5_A_re'][0], p['s5_A_im'][0], p['s5_log_dt'][0],
                                          p['s5_B_re'][0], p['s5_B_im'][0])
    q['s5_lb_re'], q['s5_lb_im'] = lb_re.reshape(1, S5_STATE), lb_im.reshape(1, S5_STATE)
    q['s5_w_re'], q['s5_w_im'] = _block_diag(bb_re), _block_diag(bb_im)
    q['s5_c_re'] = _block_diag(jnp.swapaxes(p['s5_C_re'][0], 1, 2))
    q['s5_c_im'] = _block_diag(jnp.swapaxes(p['s5_C_im'][0], 1, 2))
    q['s5_glu_w'] = p['s5_glu_w'][0].astype(BF16)
    q['w_ffn_up'] = p['w_ffn_up'].astype(BF16)
    q['w_ffn_down'] = p['w_ffn_down'].astype(BF16)
    return q


def _trunk(x, mods, grp, seq_shape, state, p, q):
    nb, seq_len, valid = seq_shape
    s_gla, s_gdn, s_gdnc, s_ssd, s_ssdc, s_re, s_im, s_ffn = state
    prompt = not grp.per_token_mod

    def to_seq(t):
        if prompt:
            return t.reshape(nb, seq_len, t.shape[-1])
        t = jnp.swapaxes(t.reshape(valid, nb, t.shape[-1]), 0, 1)
        return jnp.pad(t, ((0, 0), (0, seq_len - valid), (0, 0)))

    def from_seq(t):
        if prompt:
            return t.reshape(nb * seq_len, t.shape[-1])
        return jnp.swapaxes(t[:, :valid], 0, 1).reshape(valid * nb, t.shape[-1])

    blk = MIX_BLOCK if prompt else seq_len
    chunks = (GLA_CHUNK, GDN_CHUNK, SSD_CHUNK) if prompt else (seq_len,) * 3
    new = {}

    h = _norm_mod(x, p['g_mix'][0], mods[0], grp, 1, 0)
    proj = to_seq(_matmul(h, q['w_in_ab'], grp, AB_N // 3))
    o_a, new['gla'] = _gla(proj, q['gla_w2'], p['gla_b2'][0], p['gla_norm_g'][0], s_gla, chunks[0],
                            GLA_GROUP if prompt else 1, blk, valid, 1 if prompt else SAMPLE_SEQS)
    o_b, new['gdn'], new['gdnc'] = _gdn(proj, p['gdn_conv_w'][0], p['gdn_conv_b'][0], q['gdn_alog'], q['gdn_dtb'],
                                        p['gdn_norm_g'][0], s_gdn, s_gdnc, chunks[1], blk, valid,
                                        1 if prompt else SAMPLE_SEQS)
    x = _mm_residual([(from_seq(o_a), q['w_out_ab'], 0, 0), (from_seq(o_b), q['w_out_ab'], 0, 1)],
                     x, mods[0], grp, 2, 1024)
    h = _norm_mod(x, p['g_ffn'][0], mods[0], grp, 4, 3)
    act, new['ffn0'] = _ffn_up(h, q['w_ffn_up'], p['ffn_conv_w'], p['ffn_conv_b'], 0, s_ffn[0],
                               grp.retiled(FFN_TILE))
    x = _mm_residual([(act, q['w_ffn_down'], 0, 0)], x, mods[0], grp, 5, 1024)

    h = _norm_mod(x, p['g_mix'][1], mods[1], grp, 1, 0)
    proj2 = _matmul(h, q['w_in_cd'], grp, CD_N // 2)
    proj = to_seq(proj2)
    o_c, new['ssd'], new['ssdc'] = _ssd(proj, p['ssd_conv_w'][0], p['ssd_conv_b'][0], q['ssd_alog'], q['ssd_dtb'],
                                        q['ssd_d_row'], p['ssd_norm_g'][0], s_ssd, s_ssdc, chunks[2], blk, valid,
                                        1 if prompt else SAMPLE_SEQS)
    if prompt:
        pow_re, pow_im = _s5_pow_table(q['s5_lb_re'], q['s5_lb_im'], seq_len // SUBLANE)
        yd, new['re'], new['im'] = _s5_seg(proj, q['s5_w_re'], q['s5_w_im'], q['s5_c_re'], q['s5_c_im'],
                                           pow_re, pow_im, p['s5_D'][0], s_re, s_im)
    else:
        yd, new['re'], new['im'] = _s5(proj2.reshape(1, grp.n_tok, CD_N), q['s5_w_re'], q['s5_w_im'],
                                       q['s5_c_re'], q['s5_c_im'], q['s5_lb_re'], q['s5_lb_im'], p['s5_D'][0],
                                       s_re, s_im, nb, valid)
    o_d = _s5_glu(yd.reshape(grp.n_tok, S5_W), q['s5_glu_w'], p['s5_glu_b'][0], grp)
    x = _mm_residual([(from_seq(o_c), q['w_out_cd'], 0, 0), (o_d, q['w_out_cd'], 0, 1)], x, mods[1], grp, 2, 1024)
    h = _norm_mod(x, p['g_ffn'][1], mods[1], grp, 4, 3)
    act, new['ffn1'] = _ffn_up(h, q['w_ffn_up'], p['ffn_conv_w'], p['ffn_conv_b'], 1, s_ffn[1],
                               grp.retiled(FFN_TILE))
    x = _mm_residual([(act, q['w_ffn_down'], 1, 0)], x, mods[1], grp, 5, 1024)
    return _final_rms(x, p['g_final'], grp), new


def kernel(x_prompt, x_sample, c_prompt, c_sample, state_gla, state_gdn, state_gdn_conv, state_ssd, state_ssd_conv, state_s5_re, state_s5_im, state_ffn_conv, w_ada, b_ada, g_mix, g_ffn, w_in_ab, gla_w2, gla_b2, gla_norm_g, gdn_conv_w, gdn_conv_b, gdn_A_log, gdn_dt_bias, gdn_norm_g, w_out_ab, w_in_cd, ssd_conv_w, ssd_conv_b, ssd_A_log, ssd_dt_bias, ssd_D, ssd_norm_g, s5_A_re, s5_A_im, s5_B_re, s5_B_im, s5_C_re, s5_C_im, s5_D, s5_log_dt, s5_glu_w, s5_glu_b, w_out_cd, w_ffn_up, ffn_conv_w, ffn_conv_b, w_ffn_down, g_final):
    p = dict(g_mix=g_mix, g_ffn=g_ffn, w_in_ab=w_in_ab, gla_w2=gla_w2, gla_b2=gla_b2, gla_norm_g=gla_norm_g,
             gdn_conv_w=gdn_conv_w, gdn_conv_b=gdn_conv_b, gdn_A_log=gdn_A_log, gdn_dt_bias=gdn_dt_bias,
             gdn_norm_g=gdn_norm_g, w_out_ab=w_out_ab, w_in_cd=w_in_cd, ssd_conv_w=ssd_conv_w,
             ssd_conv_b=ssd_conv_b, ssd_A_log=ssd_A_log, ssd_dt_bias=ssd_dt_bias, ssd_D=ssd_D,
             ssd_norm_g=ssd_norm_g, s5_A_re=s5_A_re, s5_A_im=s5_A_im, s5_B_re=s5_B_re, s5_B_im=s5_B_im,
             s5_C_re=s5_C_re, s5_C_im=s5_C_im, s5_D=s5_D, s5_log_dt=s5_log_dt, s5_glu_w=s5_glu_w,
             s5_glu_b=s5_glu_b, w_out_cd=w_out_cd, w_ffn_up=w_ffn_up, ffn_conv_w=ffn_conv_w,
             ffn_conv_b=ffn_conv_b, w_ffn_down=w_ffn_down, g_final=g_final)
    bp, lp, d = x_prompt.shape
    bs, ls, _ = x_sample.shape
    q = _prep_params(p)

    bp_pad = -(-bp // SUBLANE) * SUBLANE
    c_all = jnp.concatenate([c_prompt, jnp.zeros((bp_pad - bp, d), F32), c_sample], axis=0)
    mod = _ada_mod(c_all, w_ada, b_ada)
    depth = w_ada.shape[0]
    mods_p = [mod[l, :bp].reshape(bp, 1, 6 * d) for l in range(depth)]
    mods_s = [jnp.tile(mod[l, bp_pad:], (ls, 1)).reshape(1, ls * bs, 6 * d) for l in range(depth)]

    tile_p = 512
    grp_p = _Group(bp * lp, tile_p, False, lp // tile_p, 1)
    zeros = lambda *shape: jnp.zeros(shape, F32)
    state_p = (zeros(bp, GLA_H, GLA_DK, GLA_DV), zeros(bp, GDN_H, GDN_DK, GDN_DV),
               zeros(bp, CONV_K - 1, GDN_CONV_W), zeros(bp, SSD_H, SSD_P, SSD_N),
               zeros(bp, CONV_K - 1, SSD_CONV_W), zeros(bp, 1, S5_STATE), zeros(bp, 1, S5_STATE),
               zeros(depth, bp, FFN_K - 1, 2 * D_FF))
    y_p, new_p = _trunk(x_prompt.reshape(bp * lp, d), mods_p, grp_p, (bp, lp, lp), state_p, p, q)

    grp_s = _Group(bs * ls, bs * ls, True, 1, bs)
    ffn_hist_s = jnp.swapaxes(state_ffn_conv, 1, 2).reshape(depth, 1, (FFN_K - 1) * bs, 2 * D_FF)
    state_s = (state_gla[0], state_gdn[0], state_gdn_conv[0], state_ssd[0], state_ssd_conv[0],
               state_s5_re.reshape(1, bs, S5_STATE), state_s5_im.reshape(1, bs, S5_STATE), ffn_hist_s)
    x_s = jnp.swapaxes(x_sample, 0, 1).reshape(ls * bs, d)
    y_s, new_s = _trunk(x_s, mods_s, grp_s, (bs, SAMPLE_PAD, ls), state_s, p, q)
    y_s = jnp.swapaxes(y_s.reshape(ls, bs, d), 0, 1)

    ffn_p = jnp.stack([new_p['ffn0'], new_p['ffn1']])
    ffn_s = jnp.stack([jnp.swapaxes(new_s[k].reshape(FFN_K - 1, bs, 2 * D_FF), 0, 1) for k in ('ffn0', 'ffn1')])
    s5_shape = lambda t, nb: t.reshape(1, nb, S5_G, S5_P)
    return (y_p.reshape(bp, lp, d), y_s,
            new_p['gla'][None], new_s['gla'][None], new_p['gdn'][None], new_s['gdn'][None],
            new_p['gdnc'][None], new_s['gdnc'][None], new_p['ssd'][None], new_s['ssd'][None],
            new_p['ssdc'][None], new_s['ssdc'][None],
            s5_shape(new_p['re'], bp), s5_shape(new_s['re'], bs), s5_shape(new_p['im'], bp), s5_shape(new_s['im'], bs),
            ffn_p, ffn_s)
```

```python
import functools
import math

import jax
import jax.numpy as jnp
from jax import lax
from jax.experimental import pallas as pl
from jax.experimental.pallas import tpu as pltpu

F32 = jnp.float32
BF16 = jnp.bfloat16
HIGHEST = lax.Precision.HIGHEST
EPS = 1e-6

D_MODEL = 2048
GLA_H, GLA_DK, GLA_DV, GLA_LR = 4, 128, 256, 16
GLA_GATE_NORM = 16.0
GLA_QK, GLA_V = GLA_H * GLA_DK, GLA_H * GLA_DV
GDN_H, GDN_DK, GDN_DV = 8, 128, 128
GDN_QK, GDN_V = GDN_H * GDN_DK, GDN_H * GDN_DV
CONV_K = 4
GDN_CONV_W = 2 * GDN_QK + GDN_V
SSD_P, SSD_H, SSD_G, SSD_N = 64, 16, 2, 128
SSD_W = SSD_H * SSD_P
SSD_CONV_W = SSD_W + 2 * SSD_G * SSD_N
S5_W, S5_GS, S5_G, S5_P = 1024, 16, 64, 64
S5_STATE = S5_G * S5_P
D_FF = 5632
FFN_K = 3

LANE = 128
SUBLANE = 8
VMEM_LIMIT = 48 * 1024 * 1024

AB_Q, AB_K, AB_V, AB_R, AB_QKV, AB_G, AB_SMALL, AB_N = 0, 512, 1024, 2048, 3072, 6144, 7168, 7296
AB_LR_LANE, AB_A_LANE, AB_B_LANE = 0, 16, 24
CD_XBC, CD_SMALL, CD_Z, CD_U, CD_N = 0, 1536, 2048, 3072, 4096

MIX_BLOCK = 256
GLA_CHUNK, GDN_CHUNK, SSD_CHUNK = 16, 128, 128
GLA_GROUP = 4
FFN_TILE = 1024
FFN_SUB = 256
INV_BLOCK = 16
SAMPLE_PAD = 8
SAMPLE_SEQS = 4


def _cparams(n_axes):
    return pltpu.CompilerParams(dimension_semantics=("arbitrary",) * n_axes, vmem_limit_bytes=VMEM_LIMIT)


def _sigmoid(x):
    return 1.0 / (1.0 + jnp.exp(-x))


def _silu(x):
    return x * _sigmoid(x)


def _softplus(x):
    return jnp.maximum(x, 0.0) + jnp.log(1.0 + jnp.exp(-jnp.abs(x)))


def _gelu_tanh(x):
    return 0.5 * x * (1.0 + jnp.tanh(math.sqrt(2.0 / math.pi) * (x + 0.044715 * (x * x * x))))


def _dot(a, b, precision=None):
    return jnp.dot(a, b, precision=precision, preferred_element_type=F32)


def _dot_nt(a, b, precision=None):
    return lax.dot_general(a, b, (((1,), (1,)), ((), ())), precision=precision, preferred_element_type=F32)


def _dot_tn(a, b, precision=None):
    return lax.dot_general(a, b, (((0,), (0,)), ((), ())), precision=precision, preferred_element_type=F32)


def _head_rms(o, g):
    return o * lax.rsqrt(jnp.mean(o * o, -1, keepdims=True) + EPS) * g


class _Group:
    def __init__(self, n_tok, tile, per_token_mod, tiles_per_seq, conv_shift):
        self.n_tok = n_tok
        self.tile = tile
        self.n_tiles = n_tok // tile
        self.per_token_mod = per_token_mod
        self.tiles_per_seq = tiles_per_seq
        self.conv_shift = conv_shift

    def retiled(self, tile):
        seq_rows = self.tile * self.tiles_per_seq
        if seq_rows % tile or self.n_tok % tile:
            return self
        return _Group(self.n_tok, tile, self.per_token_mod, seq_rows // tile, self.conv_shift)

    def mod_spec(self, width, col_block, m_axis):
        if self.per_token_mod:
            return pl.BlockSpec((1, self.tile, width), lambda *g: (0, g[m_axis], col_block(*g)))
        tps = self.tiles_per_seq
        return pl.BlockSpec((1, 1, width), lambda *g: (g[m_axis] // tps, 0, col_block(*g)))


def _ada_kernel(c_ref, w_ref, b_ref, o_ref):
    cs = _silu(c_ref[...]).astype(BF16)
    o_ref[0] = _dot(cs, w_ref[0].astype(BF16)) + b_ref[0]


def _ada_mod(c, w_ada, b_ada):
    depth, d, n = w_ada.shape
    rows = c.shape[0]
    tn = 1024
    return pl.pallas_call(
        _ada_kernel,
        grid=(depth, n // tn),
        in_specs=[pl.BlockSpec((rows, d), lambda l, j: (0, 0)),
                  pl.BlockSpec((1, d, tn), lambda l, j: (l, 0, j)),
                  pl.BlockSpec((1, 1, tn), lambda l, j: (l, 0, j))],
        out_specs=pl.BlockSpec((1, rows, tn), lambda l, j: (l, 0, j)),
        out_shape=jax.ShapeDtypeStruct((depth, rows, n), F32),
        compiler_params=_cparams(2), name="ada_mod",
    )(c, w_ada, b_ada.reshape(depth, 1, n))


def _norm_mod_kernel(x_ref, g_ref, sc_ref, sh_ref, o_ref):
    x = x_ref[...]
    y = x * lax.rsqrt(jnp.mean(x * x, -1, keepdims=True) + EPS) * g_ref[...]
    o_ref[...] = (y * (1.0 + sc_ref[0]) + sh_ref[0]).astype(BF16)


def _norm_mod(x, g, mod, grp, sc_blk, sh_blk):
    d = x.shape[1]
    return pl.pallas_call(
        _norm_mod_kernel,
        grid=(grp.n_tiles,),
        in_specs=[pl.BlockSpec((grp.tile, d), lambda i: (i, 0)),
                  pl.BlockSpec((1, d), lambda i: (0, 0)),
                  grp.mod_spec(d, lambda i: sc_blk, 0),
                  grp.mod_spec(d, lambda i: sh_blk, 0)],
        out_specs=pl.BlockSpec((grp.tile, d), lambda i: (i, 0)),
        out_shape=jax.ShapeDtypeStruct(x.shape, BF16),
        compiler_params=_cparams(1), name="norm_mod",
    )(x, g.reshape(1, d), mod, mod)


def _rms_kernel(x_ref, g_ref, o_ref):
    x = x_ref[...]
    o_ref[...] = x * lax.rsqrt(jnp.mean(x * x, -1, keepdims=True) + EPS) * g_ref[...]


def _final_rms(x, g, grp):
    d = x.shape[1]
    return pl.pallas_call(
        _rms_kernel,
        grid=(grp.n_tiles,),
        in_specs=[pl.BlockSpec((grp.tile, d), lambda i: (i, 0)), pl.BlockSpec((1, d), lambda i: (0, 0))],
        out_specs=pl.BlockSpec((grp.tile, d), lambda i: (i, 0)),
        out_shape=jax.ShapeDtypeStruct(x.shape, F32),
        compiler_params=_cparams(1), name="final_rms",
    )(x, g.reshape(1, d))


def _matmul_kernel(a_ref, w_ref, o_ref):
    o_ref[...] = _dot(a_ref[...], w_ref[...])


def _matmul(a, w, grp, tn):
    k, n = w.shape
    return pl.pallas_call(
        _matmul_kernel,
        grid=(n // tn, grp.n_tiles),
        in_specs=[pl.BlockSpec((grp.tile, k), lambda j, i: (i, 0)),
                  pl.BlockSpec((k, tn), lambda j, i: (0, j))],
        out_specs=pl.BlockSpec((grp.tile, tn), lambda j, i: (i, j)),
        out_shape=jax.ShapeDtypeStruct((a.shape[0], n), F32),
        compiler_params=_cparams(2), name="matmul",
    )(a, w)


def _mm_res_kernel(*refs, n_pairs):
    x_ref, gate_ref, o_ref = refs[2 * n_pairs:]
    y = _dot(refs[0][...].astype(BF16), refs[1][0])
    for p in range(1, n_pairs):
        y = y + _dot(refs[2 * p][...].astype(BF16), refs[2 * p + 1][0])
    o_ref[...] = x_ref[...] + gate_ref[0] * y


def _mm_residual(pairs, x, mod, grp, gate_blk, tn):
    n = x.shape[1]
    in_specs, args = [], []
    for a, w, layer, row_blk in pairs:
        k = a.shape[1]
        in_specs += [pl.BlockSpec((grp.tile, k), lambda j, i: (i, 0)),
                     pl.BlockSpec((1, k, tn), lambda j, i, layer=layer, row_blk=row_blk: (layer, row_blk, j))]
        args += [a, w]
    in_specs += [pl.BlockSpec((grp.tile, tn), lambda j, i: (i, j)),
                 grp.mod_spec(tn, lambda j, i: gate_blk * (n // tn) + j, 1)]
    return pl.pallas_call(
        functools.partial(_mm_res_kernel, n_pairs=len(pairs)),
        grid=(n // tn, grp.n_tiles),
        in_specs=in_specs,
        out_specs=pl.BlockSpec((grp.tile, tn), lambda j, i: (i, j)),
        out_shape=jax.ShapeDtypeStruct(x.shape, F32),
        compiler_params=_cparams(2), name="mm_residual",
    )(*args, x, mod)


def _ffn_up_kernel(h_ref, wa_ref, wg_ref, cwa_ref, cwg_ref, cba_ref, cbg_ref, ha_ref, hg_ref,
                   act_ref, sta_ref, stg_ref, scr_a, scr_g, *, shift, tile, sub, tiles_per_seq):
    i = pl.program_id(1)
    hist = (FFN_K - 1) * shift
    base = -(-hist // SUBLANE) * SUBLANE

    @pl.when(i % tiles_per_seq == 0)
    def _():
        scr_a[base - hist:base, :] = ha_ref[0]
        scr_g[base - hist:base, :] = hg_ref[0]

    def conv(scr, cw_ref, cb_ref, r0):
        y = cb_ref[0]
        for j in range(FFN_K):
            lo = base + r0 - (FFN_K - 1 - j) * shift
            y = y + scr[lo:lo + sub, :] * cw_ref[0, j:j + 1, :]
        return y

    def project(r0):
        h = h_ref[r0:r0 + sub, :]
        scr_a[base + r0:base + r0 + sub, :] = _dot(h, wa_ref[0])
        scr_g[base + r0:base + r0 + sub, :] = _dot(h, wg_ref[0])

    project(0)
    for r0 in range(0, tile, sub):
        if r0 + sub < tile:
            project(r0 + sub)
        a = conv(scr_a, cwa_ref, cba_ref, r0)
        g = conv(scr_g, cwg_ref, cbg_ref, r0)
        act_ref[r0:r0 + sub, :] = (_silu(g) * a).astype(BF16)
    last_a = scr_a[base + tile - hist:base + tile, :]
    last_g = scr_g[base + tile - hist:base + tile, :]
    sta_ref[0] = last_a
    stg_ref[0] = last_g
    scr_a[base - hist:base, :] = last_a
    scr_g[base - hist:base, :] = last_g


def _ffn_up(h, w_up, conv_w, conv_b, layer, hist0, grp):
    d = h.shape[1]
    tn = 512
    nj = D_FF // tn
    shift = grp.conv_shift
    hist = (FFN_K - 1) * shift
    base = -(-hist // SUBLANE) * SUBLANE
    n_seq = grp.n_tiles // grp.tiles_per_seq
    tps = grp.tiles_per_seq
    cb = conv_b.reshape(conv_b.shape[0], 1, 2 * D_FF)
    kern = functools.partial(_ffn_up_kernel, shift=shift, tile=grp.tile, sub=min(FFN_SUB, grp.tile),
                             tiles_per_seq=tps)
    assert grp.tile % min(FFN_SUB, grp.tile) == 0
    act, st_a, st_g = pl.pallas_call(
        kern,
        grid=(nj, grp.n_tiles),
        in_specs=[pl.BlockSpec((grp.tile, d), lambda j, i: (i, 0)),
                  pl.BlockSpec((1, d, tn), lambda j, i: (layer, 0, j)),
                  pl.BlockSpec((1, d, tn), lambda j, i: (layer, 0, nj + j)),
                  pl.BlockSpec((1, FFN_K, tn), lambda j, i: (layer, 0, j)),
                  pl.BlockSpec((1, FFN_K, tn), lambda j, i: (layer, 0, nj + j)),
                  pl.BlockSpec((1, 1, tn), lambda j, i: (layer, 0, j)),
                  pl.BlockSpec((1, 1, tn), lambda j, i: (layer, 0, nj + j)),
                  pl.BlockSpec((1, hist, tn), lambda j, i: (i // tps, 0, j)),
                  pl.BlockSpec((1, hist, tn), lambda j, i: (i // tps, 0, nj + j))],
        out_specs=[pl.BlockSpec((grp.tile, tn), lambda j, i: (i, j)),
                   pl.BlockSpec((1, hist, tn), lambda j, i: (i // tps, 0, j)),
                   pl.BlockSpec((1, hist, tn), lambda j, i: (i // tps, 0, j))],
        out_shape=[jax.ShapeDtypeStruct((h.shape[0], D_FF), BF16),
                   jax.ShapeDtypeStruct((n_seq, hist, D_FF), F32),
                   jax.ShapeDtypeStruct((n_seq, hist, D_FF), F32)],
        scratch_shapes=[pltpu.VMEM((base + grp.tile, tn), F32), pltpu.VMEM((base + grp.tile, tn), F32)],
        compiler_params=_cparams(2), name="ffn_up",
    )(h, w_up, w_up, conv_w, conv_w, cb, cb, hist0, hist0)
    return act, jnp.concatenate([st_a, st_g], axis=-1)


def _glu_kernel(y_ref, w_ref, b_ref, o_ref):
    z5 = _gelu_tanh(y_ref[...])
    o_ref[...] = (z5 * _sigmoid(_dot(z5.astype(BF16), w_ref[...]) + b_ref[...])).astype(o_ref.dtype)


def _s5_glu(yd, w, b, grp):
    n = yd.shape[1]
    return pl.pallas_call(
        _glu_kernel,
        grid=(grp.n_tiles,),
        in_specs=[pl.BlockSpec((grp.tile, n), lambda i: (i, 0)),
                  pl.BlockSpec((n, n), lambda i: (0, 0)),
                  pl.BlockSpec((1, n), lambda i: (0, 0))],
        out_specs=pl.BlockSpec((grp.tile, n), lambda i: (i, 0)),
        out_shape=jax.ShapeDtypeStruct(yd.shape, BF16),
        compiler_params=_cparams(1), name="s5_glu",
    )(yd, w, b.reshape(1, n))


def _mixer_out_dtype(block):
    return BF16 if block % (2 * SUBLANE) == 0 else F32


def _causal_conv_chunk(x, cv_scr, cw_ref, cb_ref, chunk, valid):
    base = SUBLANE
    cv_scr[base:base + chunk, :] = x
    y = cb_ref[...]
    for j in range(CONV_K):
        lo = base - (CONV_K - 1) + j
        y = y + cv_scr[lo:lo + chunk, :] * cw_ref[j:j + 1, :]
    last = cv_scr[base + valid - (CONV_K - 1):base + valid, :]
    cv_scr[base - (CONV_K - 1):base, :] = last
    return y


def _gla_kernel(q_ref, k_ref, v_ref, r_ref, sm_ref, w2_ref, b2_ref, ng_ref, s0_ref,
                o_ref, sout_ref, s_scr, *, chunk, group, block, valid, nseq):
    blk = pl.program_id(1)

    @pl.when(blk == 0)
    def _():
        s_scr[...] = s0_ref[...]

    span = chunk * group
    row3 = lax.broadcasted_iota(jnp.int32, (1, chunk, 1), 1)
    row_in_chunk = lax.broadcasted_iota(jnp.int32, (span, 1), 0) % chunk
    ri = lax.broadcasted_iota(jnp.int32, (span, span), 0)
    ci = lax.broadcasted_iota(jnp.int32, (span, span), 1)
    tri = ((ri // chunk == ci // chunk) & (ri >= ci)).astype(F32)
    eye = (lax.broadcasted_iota(jnp.int32, (GLA_DK, GLA_DK), 0)
           == lax.broadcasted_iota(jnp.int32, (GLA_DK, GLA_DK), 1))
    n_valid = min(valid, chunk)
    heads = range(GLA_H)
    chunks = range(group)

    def to3(t):
        return t.reshape(group, chunk, t.shape[-1])

    def one_seq(sq, rows):
        x = _dot(sm_ref[sq, rows, :], w2_ref[...], HIGHEST) + b2_ref[...]
        log_a = (jnp.minimum(x, 0.0) - jnp.log(1.0 + jnp.exp(-jnp.abs(x)))) * (1.0 / GLA_GATE_NORM)
        if valid < chunk:
            log_a = jnp.where(row_in_chunk < valid, log_a, 0.0)
        b = _dot(tri, log_a, HIGHEST)
        q = q_ref[sq, rows, :] * GLA_DK ** -0.5
        k = k_ref[sq, rows, :]
        v = v_ref[sq, rows, :]
        r = r_ref[sq, rows, :]
        b3, q_in, kv, d_col = [], [], [], []
        for h in heads:
            ks = slice(h * GLA_DK, (h + 1) * GLA_DK)
            bh3 = to3(b[:, ks])
            b_last = bh3[:, chunk - 1:chunk, :]
            k_out = (to3(k[:, ks]) * jnp.exp(b_last - bh3)).reshape(span, GLA_DK)
            vh = v[:, h * GLA_DV:(h + 1) * GLA_DV]
            b3.append(bh3)
            q_in.append(q[:, ks] * jnp.exp(b[:, ks]))
            kv.append([_dot_tn(k_out[c * chunk:(c + 1) * chunk], vh[c * chunk:(c + 1) * chunk]) for c in chunks])
            d_col.append([jnp.sum(jnp.where(eye, jnp.exp(b_last[c]), 0.0), axis=-1, keepdims=True) for c in chunks])
        st = [s_scr[sq, h] for h in heads]
        o_inter = [[] for _ in heads]
        for c in chunks:
            for h in heads:
                o_inter[h].append(_dot(q_in[h][c * chunk:(c + 1) * chunk], st[h]))
                st[h] = st[h] * d_col[h][c] + kv[h][c]
        for h in heads:
            ks = slice(h * GLA_DK, (h + 1) * GLA_DK)
            vs = slice(h * GLA_DV, (h + 1) * GLA_DV)
            s_scr[sq, h] = st[h]
            bh3, qh3, kh3, vh3 = b3[h], to3(q[:, ks]), to3(k[:, ks]), to3(v[:, vs])
            o3 = jnp.zeros((group, chunk, GLA_DV), F32)
            for j in range(n_valid):
                e = jnp.exp(jnp.minimum(bh3 - bh3[:, j:j + 1, :], 0.0))
                sj = jnp.sum(qh3 * e * kh3[:, j:j + 1, :], axis=-1, keepdims=True)
                o3 = o3 + jnp.where(row3 >= j, sj, 0.0) * vh3[:, j:j + 1, :]
            o = o3.reshape(span, GLA_DV) + jnp.concatenate(o_inter[h], axis=0)
            o_ref[sq, rows, vs] = (_head_rms(o, ng_ref[...]) * _silu(r[:, vs])).astype(o_ref.dtype)

    def do_span(s, carry):
        rows = pl.ds(pl.multiple_of(s * span, span), span)
        for sq in range(nseq):
            one_seq(sq, rows)
        return carry

    lax.fori_loop(0, block // span, do_span, 0)

    @pl.when(blk == pl.num_programs(1) - 1)
    def _():
        sout_ref[...] = s_scr[...]


def _gla(proj, w2p, b2, ng, s0, chunk, group, block, valid, nseq):
    nb, seq, _ = proj.shape

    def col(width, off):
        return pl.BlockSpec((nseq, block, width), lambda b, i: (b, i, off // width))

    st_spec = pl.BlockSpec((nseq, GLA_H, GLA_DK, GLA_DV), lambda b, i: (b, 0, 0, 0))
    kern = functools.partial(_gla_kernel, chunk=chunk, group=group, block=block, valid=valid, nseq=nseq)
    return pl.pallas_call(
        kern,
        grid=(nb // nseq, seq // block),
        in_specs=[col(GLA_QK, AB_Q), col(GLA_QK, AB_K), col(GLA_V, AB_V), col(GLA_V, AB_R), col(LANE, AB_SMALL),
                  pl.BlockSpec((LANE, GLA_QK), lambda b, i: (0, 0)),
                  pl.BlockSpec((1, GLA_QK), lambda b, i: (0, 0)),
                  pl.BlockSpec((1, GLA_DV), lambda b, i: (0, 0)),
                  st_spec],
        out_specs=[pl.BlockSpec((nseq, block, GLA_V), lambda b, i: (b, i, 0)), st_spec],
        out_shape=[jax.ShapeDtypeStruct((nb, seq, GLA_V), _mixer_out_dtype(block)),
                   jax.ShapeDtypeStruct((nb, GLA_H, GLA_DK, GLA_DV), F32)],
        scratch_shapes=[pltpu.VMEM((nseq, GLA_H, GLA_DK, GLA_DV), F32)],
        compiler_params=_cparams(2), name="gla",
    )(proj, proj, proj, proj, proj, w2p, b2.reshape(1, GLA_QK), ng.reshape(1, GLA_DV), s0)


def _split2(a):
    hi = a.astype(BF16)
    return hi, (a - hi.astype(F32)).astype(BF16)


def _dot3(a, b):
    return _dot(a[0], b[0]) + _dot(a[0], b[1]) + _dot(a[1], b[0])


def _inv_unit_lower_many(mats, n, eye):
    ps = [eye - a for a in mats]
    if n <= 2:
        return ps
    pows = [_split2(a) for a in mats]
    k = 2
    pending = None
    while k < n:
        sq = [_dot3(a, a) for a in pows]
        if pending is not None:
            ps = [p + _dot3(_split2(p), f) for p, f in zip(ps, pending)]
        pows = [_split2(a) for a in sq]
        pending = pows
        k *= 2
    return [p + _dot3(_split2(p), f) for p, f in zip(ps, pending)]


def _gdn_kernel(qkv_ref, sm_ref, gb_ref, cw_ref, cb_ref, alog_ref, dtb_ref, ng_ref, s0_ref, c0_ref,
                o_ref, sout_ref, cout_ref, s_scr, cv_scr, *, chunk, block, valid, nseq):
    blk = pl.program_id(1)

    @pl.when(blk == 0)
    def _():
        s_scr[...] = s0_ref[...]
        cv_scr[:, SUBLANE - (CONV_K - 1):SUBLANE, :] = c0_ref[...]

    ri = lax.broadcasted_iota(jnp.int32, (chunk, chunk), 0)
    ci = lax.broadcasted_iota(jnp.int32, (chunk, chunk), 1)
    causal = ri >= ci
    strict = ri > ci
    eye = (ri == ci).astype(F32)
    tri = causal.astype(F32)
    tri_u = (ri <= ci).astype(F32)
    inv_blk = min(INV_BLOCK, chunk)
    same_blk = (ri // inv_blk) == (ci // inv_blk)
    row = lax.broadcasted_iota(jnp.int32, (chunk, 1), 0)
    n_valid = min(valid, chunk)

    def do_chunk(s, carry):
        rows = pl.ds(pl.multiple_of(s * chunk, chunk), chunk)
        units = [(sq, h) for sq in range(nseq) for h in range(GDN_H)]
        heads = range(len(units))
        q, k, kb, rhs, dec, gcc = [], [], [], [], [], []
        for sq, h in units:
            if h == 0:
                act = _silu(_causal_conv_chunk(qkv_ref[sq, rows, :], cv_scr.at[sq], cw_ref, cb_ref, chunk, n_valid))
                sm = sm_ref[sq, rows, :]
                g_all = -jnp.exp(alog_ref[...]) * _softplus(sm + dtb_ref[...])
                beta_all = _sigmoid(sm)
                if valid < chunk:
                    g_all = jnp.where(row < valid, g_all, 0.0)
                    beta_all = jnp.where(row < valid, beta_all, 0.0)
                gc = _dot(tri, g_all, HIGHEST)
                gc_r = _dot_tn(g_all, tri_u, HIGHEST)
            qh = act[:, h * GDN_DK:(h + 1) * GDN_DK]
            kh = act[:, GDN_QK + h * GDN_DK:GDN_QK + (h + 1) * GDN_DK]
            vh = act[:, 2 * GDN_QK + h * GDN_DV:2 * GDN_QK + (h + 1) * GDN_DV]
            qh = qh * lax.rsqrt(jnp.sum(qh * qh, -1, keepdims=True) + EPS) * GDN_DK ** -0.5
            kh = kh * lax.rsqrt(jnp.sum(kh * kh, -1, keepdims=True) + EPS)
            beta = beta_all[:, AB_B_LANE + h:AB_B_LANE + h + 1]
            gch = gc[:, AB_A_LANE + h:AB_A_LANE + h + 1]
            gcr = gc_r[AB_A_LANE + h:AB_A_LANE + h + 1, :]
            q.append(qh)
            k.append(kh)
            kb.append(kh * beta)
            rhs.append(_split2(jnp.concatenate([vh * beta, kb[-1] * jnp.exp(gch)], axis=1)))
            dec.append(jnp.exp(jnp.where(causal, gch - gcr, -jnp.inf)))
            gcc.append(gch)
        kbf = [a.astype(BF16) for a in k]
        m = [jnp.where(strict, _dot_nt(kb[h].astype(BF16), kbf[h]) * dec[h], 0.0) for h in heads]
        att = [_dot_nt(q[h].astype(BF16), kbf[h]) * dec[h] for h in heads]
        m_diag = [jnp.where(same_blk, a, 0.0) for a in m]
        t = [_split2(a) for a in _inv_unit_lower_many(m_diag, inv_blk, eye)]
        y = [_dot3(t[h], rhs[h]) for h in heads]
        if chunk > inv_blk:
            n_off = [_dot3(t[h], _split2(m[h] - m_diag[h])) for h in heads]
            qn = [_split2(a) for a in _inv_unit_lower_many(n_off, chunk // inv_blk, eye)]
            y = [_dot3(qn[h], _split2(y[h])) for h in heads]
        st = [s_scr[sq, h] for sq, h in units]
        stb = [a.astype(BF16) for a in st]
        v_new = [y[h][:, :GDN_DV] - _dot(y[h][:, GDN_DV:].astype(BF16), stb[h]) for h in heads]
        vnb = [a.astype(BF16) for a in v_new]
        o = [_dot((q[h] * jnp.exp(gcc[h])).astype(BF16), stb[h]) + _dot(att[h].astype(BF16), vnb[h]) for h in heads]
        for u, (sq, h) in enumerate(units):
            hs = slice(h * GDN_DV, (h + 1) * GDN_DV)
            g_last = gcc[u][chunk - 1:chunk, :]
            k_out = (k[u] * jnp.exp(g_last - gcc[u])).astype(BF16)
            s_scr[sq, h] = st[u] * jnp.exp(g_last) + _dot_tn(k_out, vnb[u])
            o_ref[sq, rows, hs] = (_head_rms(o[u], ng_ref[...]) * _silu(gb_ref[sq, rows, hs])).astype(o_ref.dtype)
        return carry

    lax.fori_loop(0, block // chunk, do_chunk, 0)

    @pl.when(blk == pl.num_programs(1) - 1)
    def _():
        sout_ref[...] = s_scr[...]
        cout_ref[...] = cv_scr[:, SUBLANE - (CONV_K - 1):SUBLANE, :]


def _gdn(proj, conv_w, conv_b, alog_row, dtb_row, ng, s0, c0, chunk, block, valid, nseq):
    nb, seq, _ = proj.shape

    def col(width, off):
        return pl.BlockSpec((nseq, block, width), lambda b, i: (b, i, off // width))

    def const(shape):
        return pl.BlockSpec(shape, lambda b, i: (0,) * len(shape))

    st_spec = pl.BlockSpec((nseq, GDN_H, GDN_DK, GDN_DV), lambda b, i: (b, 0, 0, 0))
    cv_spec = pl.BlockSpec((nseq, CONV_K - 1, GDN_CONV_W), lambda b, i: (b, 0, 0))
    kern = functools.partial(_gdn_kernel, chunk=chunk, block=block, valid=valid, nseq=nseq)
    return pl.pallas_call(
        kern,
        grid=(nb // nseq, seq // block),
        in_specs=[col(GDN_CONV_W, AB_QKV), col(LANE, AB_SMALL), col(GDN_V, AB_G),
                  const((CONV_K, GDN_CONV_W)), const((1, GDN_CONV_W)), const((1, LANE)), const((1, LANE)),
                  const((1, GDN_DV)), st_spec, cv_spec],
        out_specs=[pl.BlockSpec((nseq, block, GDN_V), lambda b, i: (b, i, 0)), st_spec, cv_spec],
        out_shape=[jax.ShapeDtypeStruct((nb, seq, GDN_V), _mixer_out_dtype(block)),
                   jax.ShapeDtypeStruct((nb, GDN_H, GDN_DK, GDN_DV), F32),
                   jax.ShapeDtypeStruct((nb, CONV_K - 1, GDN_CONV_W), F32)],
        scratch_shapes=[pltpu.VMEM((nseq, GDN_H, GDN_DK, GDN_DV), F32),
                        pltpu.VMEM((nseq, SUBLANE + chunk, GDN_CONV_W), F32)],
        compiler_params=_cparams(2), name="gdn",
    )(proj, proj, proj, conv_w, conv_b.reshape(1, GDN_CONV_W), alog_row, dtb_row, ng.reshape(1, GDN_DV), s0, c0)


def _ssd_kernel(xbc_ref, sm_ref, z_ref, cw_ref, cb_ref, alog_ref, dtb_ref, drow_ref, ng_ref, s0_ref, c0_ref,
                o_ref, sout_ref, cout_ref, s_scr, cv_scr, *, chunk, block, valid, nseq):
    blk = pl.program_id(1)

    @pl.when(blk == 0)
    def _():
        s_scr[...] = s0_ref[...]
        cv_scr[:, SUBLANE - (CONV_K - 1):SUBLANE, :] = c0_ref[...]

    ri = lax.broadcasted_iota(jnp.int32, (chunk, chunk), 0)
    ci = lax.broadcasted_iota(jnp.int32, (chunk, chunk), 1)
    causal = ri >= ci
    eye = (ri == ci).astype(F32)
    tri = causal.astype(F32)
    tri_u = (ri <= ci).astype(F32)
    row = lax.broadcasted_iota(jnp.int32, (chunk, 1), 0)
    lane_lo = lax.broadcasted_iota(jnp.int32, (chunk, LANE), 1) < SSD_P
    row_lo = lax.broadcasted_iota(jnp.int32, (2 * SSD_P, 1), 0) < SSD_P
    n_valid = min(valid, chunk)
    heads_per_group = SSD_H // SSD_G
    gsz = SSD_W // SSD_G

    def one_seq(sq, rows):
        act = _silu(_causal_conv_chunk(xbc_ref[sq, rows, :], cv_scr.at[sq], cw_ref, cb_ref, chunk, n_valid))
        dt = _softplus(sm_ref[sq, rows, :] + dtb_ref[...])
        if valid < chunk:
            dt = jnp.where(row < valid, dt, 0.0)
        dta = dt * (-jnp.exp(alog_ref[...]))
        acs = _dot(tri, dta, HIGHEST)
        acs_r = _dot_tn(dta, tri_u, HIGHEST)
        dt_r = _dot_tn(dt, eye, HIGHEST)
        z = z_ref[sq, rows, :]
        for g in range(SSD_G):
            bg = act[:, SSD_W + g * SSD_N:SSD_W + (g + 1) * SSD_N]
            cg = act[:, SSD_W + SSD_G * SSD_N + g * SSD_N:SSD_W + SSD_G * SSD_N + (g + 1) * SSD_N]
            cb = _dot_nt(cg, bg)
            parts = []
            for pr in range(heads_per_group // 2):
                pi = g * (heads_per_group // 2) + pr
                xp = act[:, pi * LANE:(pi + 1) * LANE]
                st = s_scr[sq, pi]
                y_in, e_in, w_out, d_last = [], [], [], []
                for hh in range(2):
                    h = 2 * pi + hh
                    ac = acs[:, h:h + 1]
                    dec = jnp.exp(jnp.where(causal, ac - acs_r[h:h + 1, :], -jnp.inf))
                    y_in.append(_dot(cb * dec * dt_r[h:h + 1, :], xp))
                    a_last = ac[chunk - 1:chunk, :]
                    e_in.append(jnp.exp(ac))
                    w_out.append(jnp.exp(a_last - ac) * dt[:, h:h + 1])
                    d_last.append(jnp.exp(a_last))
                y = jnp.where(lane_lo, y_in[0], y_in[1])
                y = y + _dot_nt(cg, st) * jnp.where(lane_lo, e_in[0], e_in[1])
                y = y + drow_ref[:, pi * LANE:(pi + 1) * LANE] * xp
                x_sc = xp * jnp.where(lane_lo, w_out[0], w_out[1])
                s_scr[sq, pi] = st * jnp.where(row_lo, d_last[0], d_last[1]) + _dot_tn(x_sc, bg)
                parts.append(y)
            gs = slice(g * gsz, (g + 1) * gsz)
            yg = jnp.concatenate(parts, axis=1) * _silu(z[:, gs])
            o_ref[sq, rows, gs] = _head_rms(yg, ng_ref[:, gs]).astype(o_ref.dtype)

    def do_chunk(s, carry):
        rows = pl.ds(pl.multiple_of(s * chunk, chunk), chunk)
        for sq in range(nseq):
            one_seq(sq, rows)
        return carry

    lax.fori_loop(0, block // chunk, do_chunk, 0)

    @pl.when(blk == pl.num_programs(1) - 1)
    def _():
        sout_ref[...] = s_scr[...]
        cout_ref[...] = cv_scr[:, SUBLANE - (CONV_K - 1):SUBLANE, :]


def _ssd(proj, conv_w, conv_b, alog_row, dtb_row, d_row, ng, s0, c0, chunk, block, valid, nseq):
    nb, seq, _ = proj.shape
    n_pairs = SSD_H // 2

    def col(width, off):
        return pl.BlockSpec((nseq, block, width), lambda b, i: (b, i, off // width))

    def const(shape):
        return pl.BlockSpec(shape, lambda b, i: (0,) * len(shape))

    st_spec = pl.BlockSpec((nseq, n_pairs, 2 * SSD_P, SSD_N), lambda b, i: (b, 0, 0, 0))
    cv_spec = pl.BlockSpec((nseq, CONV_K - 1, SSD_CONV_W), lambda b, i: (b, 0, 0))
    kern = functools.partial(_ssd_kernel, chunk=chunk, block=block, valid=valid, nseq=nseq)
    o, s_new, c_new = pl.pallas_call(
        kern,
        grid=(nb // nseq, seq // block),
        in_specs=[col(SSD_CONV_W, CD_XBC), col(LANE, CD_SMALL), col(SSD_W, CD_Z),
                  const((CONV_K, SSD_CONV_W)), const((1, SSD_CONV_W)), const((1, LANE)), const((1, LANE)),
                  const((1, SSD_W)), const((1, SSD_W)), st_spec, cv_spec],
        out_specs=[pl.BlockSpec((nseq, block, SSD_W), lambda b, i: (b, i, 0)), st_spec, cv_spec],
        out_shape=[jax.ShapeDtypeStruct((nb, seq, SSD_W), _mixer_out_dtype(block)),
                   jax.ShapeDtypeStruct((nb, n_pairs, 2 * SSD_P, SSD_N), F32),
                   jax.ShapeDtypeStruct((nb, CONV_K - 1, SSD_CONV_W), F32)],
        scratch_shapes=[pltpu.VMEM((nseq, n_pairs, 2 * SSD_P, SSD_N), F32),
                        pltpu.VMEM((nseq, SUBLANE + chunk, SSD_CONV_W), F32)],
        compiler_params=_cparams(2), name="ssd",
    )(proj, proj, proj, conv_w, conv_b.reshape(1, SSD_CONV_W), alog_row, dtb_row, d_row,
      ng.reshape(1, SSD_W), s0.reshape(nb, n_pairs, 2 * SSD_P, SSD_N), c0)
    return o, s_new.reshape(nb, SSD_H, SSD_P, SSD_N), c_new


def _s5_prep_kernel(are_ref, aim_ref, ldt_ref, bre_ref, bim_ref, lbr_ref, lbi_ref, bbr_ref, bbi_ref):
    a_re, a_im = are_ref[...], aim_ref[...]
    dt = jnp.exp(ldt_ref[...])
    mag = jnp.exp(a_re * dt)
    lb_re, lb_im = mag * jnp.cos(a_im * dt), mag * jnp.sin(a_im * dt)
    nr, ni = lb_re - 1.0, lb_im
    den = a_re * a_re + a_im * a_im
    f_re = (nr * a_re + ni * a_im) / den
    f_im = (ni * a_re - nr * a_im) / den
    b_re, b_im = bre_ref[...], bim_ref[...]
    lbr_ref[...] = lb_re
    lbi_ref[...] = lb_im
    bbr_ref[...] = f_re * b_re - f_im * b_im
    bbi_ref[...] = f_re * b_im + f_im * b_re


def _s5_prep(a_re, a_im, log_dt, b_re, b_im):
    g3 = (S5_G, 1, S5_P)
    b3 = (S5_G, S5_GS, S5_P)
    return pl.pallas_call(
        _s5_prep_kernel,
        out_shape=[jax.ShapeDtypeStruct(g3, F32), jax.ShapeDtypeStruct(g3, F32),
                   jax.ShapeDtypeStruct(b3, F32), jax.ShapeDtypeStruct(b3, F32)],
        name="s5_prep",
    )(a_re.reshape(g3), a_im.reshape(g3), log_dt.reshape(S5_G, 1, 1),
      jnp.swapaxes(b_re, 1, 2), jnp.swapaxes(b_im, 1, 2))


def _block_diag(blocks):
    g, r, c = blocks.shape
    per = 8
    b = blocks.reshape(g // per, per, r, 1, c) * jnp.eye(per, dtype=blocks.dtype).reshape(1, per, 1, per, 1)
    return b.reshape(g // per, per * r, per * c)


def _s5_kernel(u_ref, wre_ref, wim_ref, cre_ref, cim_ref, lbr_ref, lbi_ref, d_ref, x0r_ref, x0i_ref,
               y_ref, xfr_ref, xfi_ref, xr_scr, xi_scr, sr_scr, si_scr, *, rows_per_step, steps):
    c = pl.program_id(2)

    @pl.when(c == 0)
    def _():
        xr_scr[...] = x0r_ref[0]
        xi_scr[...] = x0i_ref[0]

    u = u_ref[0]
    sr_scr[...] = _dot(u, wre_ref[0])
    si_scr[...] = _dot(u, wim_ref[0])
    l_re, l_im = lbr_ref[...], lbi_ref[...]

    def step(t, carry):
        rows = pl.ds(pl.multiple_of(t * rows_per_step, rows_per_step), rows_per_step)
        xr, xi = xr_scr[...], xi_scr[...]
        nr = l_re * xr - l_im * xi + sr_scr[rows, :]
        ni = l_re * xi + l_im * xr + si_scr[rows, :]
        xr_scr[...] = nr
        xi_scr[...] = ni
        sr_scr[rows, :] = nr
        si_scr[rows, :] = ni
        return carry

    lax.fori_loop(0, steps, step, 0)
    y_ref[0] = _dot(sr_scr[...], cre_ref[0]) - _dot(si_scr[...], cim_ref[0]) + d_ref[...] * u

    @pl.when(c == pl.num_programs(2) - 1)
    def _():
        xfr_ref[0] = xr_scr[...]
        xfi_ref[0] = xi_scr[...]


def _s5(proj, w_re, w_im, c_re, c_im, lb_re, lb_im, d, x0_re, x0_im, rows_per_step, steps):
    ng, n_tok, _ = proj.shape
    nj = S5_W // LANE
    sw = S5_STATE // nj
    cr = rows_per_step * steps
    kern = functools.partial(_s5_kernel, rows_per_step=rows_per_step, steps=steps)
    x_spec = pl.BlockSpec((1, rows_per_step, sw), lambda g, j, c: (g, 0, j))
    return pl.pallas_call(
        kern,
        grid=(ng, nj, n_tok // cr),
        in_specs=[pl.BlockSpec((1, cr, LANE), lambda g, j, c: (g, c, CD_U // LANE + j)),
                  pl.BlockSpec((1, LANE, sw), lambda g, j, c: (j, 0, 0)),
                  pl.BlockSpec((1, LANE, sw), lambda g, j, c: (j, 0, 0)),
                  pl.BlockSpec((1, sw, LANE), lambda g, j, c: (j, 0, 0)),
                  pl.BlockSpec((1, sw, LANE), lambda g, j, c: (j, 0, 0)),
                  pl.BlockSpec((1, sw), lambda g, j, c: (0, j)),
                  pl.BlockSpec((1, sw), lambda g, j, c: (0, j)),
                  pl.BlockSpec((1, LANE), lambda g, j, c: (0, j)),
                  x_spec, x_spec],
        out_specs=[pl.BlockSpec((1, cr, LANE), lambda g, j, c: (g, c, j)), x_spec, x_spec],
        out_shape=[jax.ShapeDtypeStruct((ng, n_tok, S5_W), F32),
                   jax.ShapeDtypeStruct((ng, rows_per_step, S5_STATE), F32),
                   jax.ShapeDtypeStruct((ng, rows_per_step, S5_STATE), F32)],
        scratch_shapes=[pltpu.VMEM((rows_per_step, sw), F32), pltpu.VMEM((rows_per_step, sw), F32),
                        pltpu.VMEM((cr, sw), F32), pltpu.VMEM((cr, sw), F32)],
        compiler_params=_cparams(3), name="s5_scan",
    )(proj, w_re, w_im, c_re, c_im, lb_re, lb_im, d.reshape(1, S5_W), x0_re, x0_im)


def _s5_pow_kernel(lbr_ref, lbi_ref, pr_ref, pi_ref, *, n_rows):
    l_re, l_im = lbr_ref[...], lbi_ref[...]
    row = lax.broadcasted_iota(jnp.int32, (SUBLANE, 1), 0)
    p_re, p_im = l_re, l_im
    b_re = jnp.broadcast_to(l_re, (SUBLANE, l_re.shape[1]))
    b_im = jnp.broadcast_to(l_im, (SUBLANE, l_re.shape[1]))
    for r in range(1, SUBLANE):
        p_re, p_im = p_re * l_re - p_im * l_im, p_re * l_im + p_im * l_re
        b_re = jnp.where(row >= r, p_re, b_re)
        b_im = jnp.where(row >= r, p_im, b_im)
    q_re, q_im = jnp.ones_like(l_re), jnp.zeros_like(l_re)
    for a in range(n_rows // SUBLANE):
        rows = slice(a * SUBLANE, (a + 1) * SUBLANE)
        pr_ref[rows, :] = b_re * q_re - b_im * q_im
        pi_ref[rows, :] = b_re * q_im + b_im * q_re
        q_re, q_im = q_re * p_re - q_im * p_im, q_re * p_im + q_im * p_re


def _s5_pow_table(lb_re, lb_im, n_rows):
    shape = jax.ShapeDtypeStruct((n_rows, S5_STATE), F32)
    return pl.pallas_call(functools.partial(_s5_pow_kernel, n_rows=n_rows), out_shape=[shape, shape],
                          name="s5_pow")(lb_re, lb_im)


def _s5_seg_kernel(u_ref, wre_ref, wim_ref, cre_ref, cim_ref, pr_ref, pi_ref, d_ref, x0r_ref, x0i_ref,
                   y_ref, xfr_ref, xfi_ref, sr_scr, si_scr, *, seg_len):
    n_seg = SUBLANE
    n_lane_blk = sr_scr.shape[0]
    sw = n_lane_blk * LANE
    lane_blks = [slice(c * LANE, (c + 1) * LANE) for c in range(n_lane_blk)]

    def put(scr, rows, val):
        for c, ls in enumerate(lane_blks):
            scr[c, rows, :] = val[:, ls]

    def get(scr, rows):
        return jnp.concatenate([scr[c, rows, :] for c in range(n_lane_blk)], axis=1)

    for s in range(n_seg):
        us = u_ref[0, s * seg_len:(s + 1) * seg_len, :]
        put(sr_scr, pl.ds(s, seg_len, stride=n_seg), _dot(us, wre_ref[0]))
        put(si_scr, pl.ds(s, seg_len, stride=n_seg), _dot(us, wim_ref[0]))
    l_re, l_im = pr_ref[0:1, :], pi_ref[0:1, :]

    def step(t, carry):
        xr, xi = carry
        rows = pl.ds(pl.multiple_of(t * n_seg, n_seg), n_seg)
        nr = l_re * xr - l_im * xi + get(sr_scr, rows)
        ni = l_re * xi + l_im * xr + get(si_scr, rows)
        put(sr_scr, rows, nr)
        put(si_scr, rows, ni)
        return nr, ni

    zero = jnp.zeros((n_seg, sw), F32)
    end_re, end_im = lax.fori_loop(0, seg_len, step, (zero, zero), unroll=4)
    ln_re, ln_im = pr_ref[seg_len - 1:seg_len, :], pi_ref[seg_len - 1:seg_len, :]
    p_re, p_im = pr_ref[...], pi_ref[...]
    x_re, x_im = x0r_ref[0], x0i_ref[0]
    for s in range(n_seg):
        loc_re = get(sr_scr, pl.ds(s, seg_len, stride=n_seg))
        loc_im = get(si_scr, pl.ds(s, seg_len, stride=n_seg))
        t_re = loc_re + p_re * x_re - p_im * x_im
        t_im = loc_im + p_re * x_im + p_im * x_re
        us = u_ref[0, s * seg_len:(s + 1) * seg_len, :]
        y_ref[0, s * seg_len:(s + 1) * seg_len, :] = (_dot(t_re, cre_ref[0]) - _dot(t_im, cim_ref[0])
                                                      + d_ref[...] * us)
        e_re, e_im = end_re[s:s + 1, :], end_im[s:s + 1, :]
        x_re, x_im = e_re + ln_re * x_re - ln_im * x_im, e_im + ln_re * x_im + ln_im * x_re
    xfr_ref[0] = x_re
    xfi_ref[0] = x_im


def _s5_seg(proj, w_re, w_im, c_re, c_im, pow_re, pow_im, d, x0_re, x0_im):
    nb, seq, _ = proj.shape
    nj = S5_W // LANE
    sw = S5_STATE // nj
    seg_len = seq // SUBLANE
    x_spec = pl.BlockSpec((1, 1, sw), lambda b, j: (b, 0, j))
    return pl.pallas_call(
        functools.partial(_s5_seg_kernel, seg_len=seg_len),
        grid=(nb, nj),
        in_specs=[pl.BlockSpec((1, seq, LANE), lambda b, j: (b, 0, CD_U // LANE + j)),
                  pl.BlockSpec((1, LANE, sw), lambda b, j: (j, 0, 0)),
                  pl.BlockSpec((1, LANE, sw), lambda b, j: (j, 0, 0)),
                  pl.BlockSpec((1, sw, LANE), lambda b, j: (j, 0, 0)),
                  pl.BlockSpec((1, sw, LANE), lambda b, j: (j, 0, 0)),
                  pl.BlockSpec((seg_len, sw), lambda b, j: (0, j)),
                  pl.BlockSpec((seg_len, sw), lambda b, j: (0, j)),
                  pl.BlockSpec((1, LANE), lambda b, j: (0, j)),
                  x_spec, x_spec],
        out_specs=[pl.BlockSpec((1, seq, LANE), lambda b, j: (b, 0, j)), x_spec, x_spec],
        out_shape=[jax.ShapeDtypeStruct((nb, seq, S5_W), F32),
                   jax.ShapeDtypeStruct((nb, 1, S5_STATE), F32),
                   jax.ShapeDtypeStruct((nb, 1, S5_STATE), F32)],
        scratch_shapes=[pltpu.VMEM((sw // LANE, seq, LANE), F32), pltpu.VMEM((sw // LANE, seq, LANE), F32)],
        compiler_params=_cparams(2), name="s5_seg",
    )(proj, w_re, w_im, c_re, c_im, pow_re, pow_im, d.reshape(1, S5_W), x0_re, x0_im)


def _lane_row(vec, lane0):
    return jnp.zeros((1, LANE), F32).at[0, lane0:lane0 + vec.shape[0]].set(vec.astype(F32))


def _prep_params(p):
    d = D_MODEL
    q = {}
    w = p['w_in_ab'][0]
    o_lr = 2 * GLA_QK + GLA_V
    o_r = o_lr + GLA_LR
    o_qkv = o_r + GLA_V
    o_a = o_qkv + GDN_CONV_W
    o_g = o_a + 2 * GDN_H
    q['w_in_ab'] = jnp.concatenate(
        [w[:, :o_lr], w[:, o_r:o_qkv], w[:, o_qkv:o_a], w[:, o_g:], w[:, o_lr:o_r], w[:, o_a:o_g],
         jnp.zeros((d, LANE - GLA_LR - 2 * GDN_H), F32)], axis=1).astype(BF16)
    q['gla_w2'] = jnp.zeros((LANE, GLA_QK), F32).at[:GLA_LR].set(p['gla_w2'][0])
    q['gdn_alog'] = _lane_row(p['gdn_A_log'][0], AB_A_LANE)
    q['gdn_dtb'] = _lane_row(p['gdn_dt_bias'][0], AB_A_LANE)
    q['w_out_ab'] = p['w_out_ab'].astype(BF16)
    w = p['w_in_cd'][0]
    o_xbc = SSD_W
    o_dt = o_xbc + SSD_CONV_W
    o_u = o_dt + SSD_H
    q['w_in_cd'] = jnp.concatenate(
        [w[:, o_xbc:o_dt], w[:, o_dt:o_u], jnp.zeros((d, CD_Z - CD_SMALL - SSD_H), F32), w[:, :o_xbc], w[:, o_u:]],
        axis=1).astype(BF16)
    q['ssd_alog'] = _lane_row(p['ssd_A_log'][0], 0)
    q['ssd_dtb'] = _lane_row(p['ssd_dt_bias'][0], 0)
    q['ssd_d_row'] = jnp.repeat(p['ssd_D'][0].astype(F32), SSD_P).reshape(1, SSD_W)
    q['w_out_cd'] = p['w_out_cd'].astype(BF16)
    lb_re, lb_im, bb_re, bb_im = _s5_prep(p['s5_A_re'][0], p['s5_A_im'][0], p['s5_log_dt'][0],
                                          p['s5_B_re'][0], p['s5_B_im'][0])
    q['s5_lb_re'], q['s5_lb_im'] = lb_re.reshape(1, S5_STATE), lb_im.reshape(1, S5_STATE)
    q['s5_w_re'], q['s5_w_im'] = _block_diag(bb_re), _block_diag(bb_im)
    q['s5_c_re'] = _block_diag(jnp.swapaxes(p['s5_C_re'][0], 1, 2))
    q['s5_c_im'] = _block_diag(jnp.swapaxes(p['s5_C_im'][0], 1, 2))
    q['s5_glu_w'] = p['s5_glu_w'][0].astype(BF16)
    q['w_ffn_up'] = p['w_ffn_up'].astype(BF16)
    q['w_ffn_down'] = p['w_ffn_down'].astype(BF16)
    return q


def _trunk(x, mods, grp, seq_shape, state, p, q):
    nb, seq_len, valid = seq_shape
    s_gla, s_gdn, s_gdnc, s_ssd, s_ssdc, s_re, s_im, s_ffn = state
    prompt = not grp.per_token_mod

    def to_seq(t):
        if prompt:
            return t.reshape(nb, seq_len, t.shape[-1])
        t = jnp.swapaxes(t.reshape(valid, nb, t.shape[-1]), 0, 1)
        return jnp.pad(t, ((0, 0), (0, seq_len - valid), (0, 0)))

    def from_seq(t):
        if prompt:
            return t.reshape(nb * seq_len, t.shape[-1])
        return jnp.swapaxes(t[:, :valid], 0, 1).reshape(valid * nb, t.shape[-1])

    blk = MIX_BLOCK if prompt else seq_len
    chunks = (GLA_CHUNK, GDN_CHUNK, SSD_CHUNK) if prompt else (seq_len,) * 3
    new = {}

    h = _norm_mod(x, p['g_mix'][0], mods[0], grp, 1, 0)
    proj = to_seq(_matmul(h, q['w_in_ab'], grp, AB_N // 3))
    o_a, new['gla'] = _gla(proj, q['gla_w2'], p['gla_b2'][0], p['gla_norm_g'][0], s_gla, chunks[0],
                            GLA_GROUP if prompt else 1, blk, valid, 1 if prompt else SAMPLE_SEQS)
    o_b, new['gdn'], new['gdnc'] = _gdn(proj, p['gdn_conv_w'][0], p['gdn_conv_b'][0], q['gdn_alog'], q['gdn_dtb'],
                                        p['gdn_norm_g'][0], s_gdn, s_gdnc, chunks[1], blk, valid,
                                        1 if prompt else SAMPLE_SEQS)
    x = _mm_residual([(from_seq(o_a), q['w_out_ab'], 0, 0), (from_seq(o_b), q['w_out_ab'], 0, 1)],
                     x, mods[0], grp, 2, D_MODEL)
    h = _norm_mod(x, p['g_ffn'][0], mods[0], grp, 4, 3)
    act, new['ffn0'] = _ffn_up(h, q['w_ffn_up'], p['ffn_conv_w'], p['ffn_conv_b'], 0, s_ffn[0],
                               grp.retiled(FFN_TILE))
    x = _mm_residual([(act, q['w_ffn_down'], 0, 0)], x, mods[0], grp, 5, 1024)

    h = _norm_mod(x, p['g_mix'][1], mods[1], grp, 1, 0)
    proj2 = _matmul(h, q['w_in_cd'], grp, CD_N // 2)
    proj = to_seq(proj2)
    o_c, new['ssd'], new['ssdc'] = _ssd(proj, p['ssd_conv_w'][0], p['ssd_conv_b'][0], q['ssd_alog'], q['ssd_dtb'],
                                        q['ssd_d_row'], p['ssd_norm_g'][0], s_ssd, s_ssdc, chunks[2], blk, valid,
                                        1 if prompt else SAMPLE_SEQS)
    if prompt:
        pow_re, pow_im = _s5_pow_table(q['s5_lb_re'], q['s5_lb_im'], seq_len // SUBLANE)
        yd, new['re'], new['im'] = _s5_seg(proj, q['s5_w_re'], q['s5_w_im'], q['s5_c_re'], q['s5_c_im'],
                                           pow_re, pow_im, p['s5_D'][0], s_re, s_im)
    else:
        yd, new['re'], new['im'] = _s5(proj2.reshape(1, grp.n_tok, CD_N), q['s5_w_re'], q['s5_w_im'],
                                       q['s5_c_re'], q['s5_c_im'], q['s5_lb_re'], q['s5_lb_im'], p['s5_D'][0],
                                       s_re, s_im, nb, valid)
    o_d = _s5_glu(yd.reshape(grp.n_tok, S5_W), q['s5_glu_w'], p['s5_glu_b'][0], grp)
    x = _mm_residual([(from_seq(o_c), q['w_out_cd'], 0, 0), (o_d, q['w_out_cd'], 0, 1)], x, mods[1], grp, 2,
                     D_MODEL)
    h = _norm_mod(x, p['g_ffn'][1], mods[1], grp, 4, 3)
    act, new['ffn1'] = _ffn_up(h, q['w_ffn_up'], p['ffn_conv_w'], p['ffn_conv_b'], 1, s_ffn[1],
                               grp.retiled(FFN_TILE))
    x = _mm_residual([(act, q['w_ffn_down'], 1, 0)], x, mods[1], grp, 5, 1024)
    return _final_rms(x, p['g_final'], grp), new


def kernel(x_prompt, x_sample, c_prompt, c_sample, state_gla, state_gdn, state_gdn_conv, state_ssd, state_ssd_conv, state_s5_re, state_s5_im, state_ffn_conv, w_ada, b_ada, g_mix, g_ffn, w_in_ab, gla_w2, gla_b2, gla_norm_g, gdn_conv_w, gdn_conv_b, gdn_A_log, gdn_dt_bias, gdn_norm_g, w_out_ab, w_in_cd, ssd_conv_w, ssd_conv_b, ssd_A_log, ssd_dt_bias, ssd_D, ssd_norm_g, s5_A_re, s5_A_im, s5_B_re, s5_B_im, s5_C_re, s5_C_im, s5_D, s5_log_dt, s5_glu_w, s5_glu_b, w_out_cd, w_ffn_up, ffn_conv_w, ffn_conv_b, w_ffn_down, g_final):
    p = dict(g_mix=g_mix, g_ffn=g_ffn, w_in_ab=w_in_ab, gla_w2=gla_w2, gla_b2=gla_b2, gla_norm_g=gla_norm_g,
             gdn_conv_w=gdn_conv_w, gdn_conv_b=gdn_conv_b, gdn_A_log=gdn_A_log, gdn_dt_bias=gdn_dt_bias,
             gdn_norm_g=gdn_norm_g, w_out_ab=w_out_ab, w_in_cd=w_in_cd, ssd_conv_w=ssd_conv_w,
             ssd_conv_b=ssd_conv_b, ssd_A_log=ssd_A_log, ssd_dt_bias=ssd_dt_bias, ssd_D=ssd_D,
             ssd_norm_g=ssd_norm_g, s5_A_re=s5_A_re, s5_A_im=s5_A_im, s5_B_re=s5_B_re, s5_B_im=s5_B_im,
             s5_C_re=s5_C_re, s5_C_im=s5_C_im, s5_D=s5_D, s5_log_dt=s5_log_dt, s5_glu_w=s5_glu_w,
             s5_glu_b=s5_glu_b, w_out_cd=w_out_cd, w_ffn_up=w_ffn_up, ffn_conv_w=ffn_conv_w,
             ffn_conv_b=ffn_conv_b, w_ffn_down=w_ffn_down, g_final=g_final)
    bp, lp, d = x_prompt.shape
    bs, ls, _ = x_sample.shape
    q = _prep_params(p)

    bp_pad = -(-bp // SUBLANE) * SUBLANE
    c_all = jnp.concatenate([c_prompt, jnp.zeros((bp_pad - bp, d), F32), c_sample], axis=0)
    mod = _ada_mod(c_all, w_ada, b_ada)
    depth = w_ada.shape[0]
    mods_p = [mod[l, :bp].reshape(bp, 1, 6 * d) for l in range(depth)]
    mods_s = [jnp.tile(mod[l, bp_pad:], (ls, 1)).reshape(1, ls * bs, 6 * d) for l in range(depth)]

    tile_p = 512
    grp_p = _Group(bp * lp, tile_p, False, lp // tile_p, 1)
    zeros = lambda *shape: jnp.zeros(shape, F32)
    state_p = (zeros(bp, GLA_H, GLA_DK, GLA_DV), zeros(bp, GDN_H, GDN_DK, GDN_DV),
               zeros(bp, CONV_K - 1, GDN_CONV_W), zeros(bp, SSD_H, SSD_P, SSD_N),
               zeros(bp, CONV_K - 1, SSD_CONV_W), zeros(bp, 1, S5_STATE), zeros(bp, 1, S5_STATE),
               zeros(depth, bp, FFN_K - 1, 2 * D_FF))
    y_p, new_p = _trunk(x_prompt.reshape(bp * lp, d), mods_p, grp_p, (bp, lp, lp), state_p, p, q)

    grp_s = _Group(bs * ls, bs * ls, True, 1, bs)
    ffn_hist_s = jnp.swapaxes(state_ffn_conv, 1, 2).reshape(depth, 1, (FFN_K - 1) * bs, 2 * D_FF)
    state_s = (state_gla[0], state_gdn[0], state_gdn_conv[0], state_ssd[0], state_ssd_conv[0],
               state_s5_re.reshape(1, bs, S5_STATE), state_s5_im.reshape(1, bs, S5_STATE), ffn_hist_s)
    x_s = jnp.swapaxes(x_sample, 0, 1).reshape(ls * bs, d)
    y_s, new_s = _trunk(x_s, mods_s, grp_s, (bs, SAMPLE_PAD, ls), state_s, p, q)
    y_s = jnp.swapaxes(y_s.reshape(ls, bs, d), 0, 1)

    ffn_p = jnp.stack([new_p['ffn0'], new_p['ffn1']])
    ffn_s = jnp.stack([jnp.swapaxes(new_s[k].reshape(FFN_K - 1, bs, 2 * D_FF), 0, 1) for k in ('ffn0', 'ffn1')])
    s5_shape = lambda t, nb: t.reshape(1, nb, S5_G, S5_P)
    return (y_p.reshape(bp, lp, d), y_s,
            new_p['gla'][None], new_s['gla'][None], new_p['gdn'][None], new_s['gdn'][None],
            new_p['gdnc'][None], new_s['gdnc'][None], new_p['ssd'][None], new_s['ssd'][None],
            new_p['ssdc'][None], new_s['ssdc'][None],
            s5_shape(new_p['re'], bp), s5_shape(new_s['re'], bs), s5_shape(new_p['im'], bp), s5_shape(new_s['im'], bs),
            ffn_p, ffn_s)
```

```python
import functools
import math

import jax
import jax.numpy as jnp
from jax import lax
from jax.experimental import pallas as pl
from jax.experimental.pallas import tpu as pltpu

F32 = jnp.float32
BF16 = jnp.bfloat16
HIGHEST = lax.Precision.HIGHEST
EPS = 1e-6

D_MODEL = 2048
GLA_H, GLA_DK, GLA_DV, GLA_LR = 4, 128, 256, 16
GLA_GATE_NORM = 16.0
GLA_QK, GLA_V = GLA_H * GLA_DK, GLA_H * GLA_DV
GDN_H, GDN_DK, GDN_DV = 8, 128, 128
GDN_QK, GDN_V = GDN_H * GDN_DK, GDN_H * GDN_DV
CONV_K = 4
GDN_CONV_W = 2 * GDN_QK + GDN_V
SSD_P, SSD_H, SSD_G, SSD_N = 64, 16, 2, 128
SSD_W = SSD_H * SSD_P
SSD_CONV_W = SSD_W + 2 * SSD_G * SSD_N
S5_W, S5_GS, S5_G, S5_P = 1024, 16, 64, 64
S5_STATE = S5_G * S5_P
D_FF = 5632
FFN_K = 3

LANE = 128
SUBLANE = 8
VMEM_LIMIT = 48 * 1024 * 1024

AB_Q, AB_K, AB_V, AB_R, AB_QKV, AB_G, AB_SMALL, AB_N = 0, 512, 1024, 2048, 3072, 6144, 7168, 7680
AB_LR_LANE, AB_A_LANE, AB_B_LANE = 0, 16, 24
CD_XBC, CD_SMALL, CD_Z, CD_U, CD_N = 0, 1536, 2048, 3072, 4096

MIX_BLOCK = 256
GLA_CHUNK, GDN_CHUNK, SSD_CHUNK = 16, 128, 128
GLA_GROUP = 4
FFN_TILE = 1024
FFN_SUB = 256
INV_BLOCK = 16
SAMPLE_PAD = 8
SAMPLE_SEQS = 4


def _cparams(n_axes):
    return pltpu.CompilerParams(dimension_semantics=("arbitrary",) * n_axes, vmem_limit_bytes=VMEM_LIMIT)


def _sigmoid(x):
    return 1.0 / (1.0 + jnp.exp(-x))


def _silu(x):
    return x * _sigmoid(x)


def _softplus(x):
    return jnp.maximum(x, 0.0) + jnp.log(1.0 + jnp.exp(-jnp.abs(x)))


def _gelu_tanh(x):
    return 0.5 * x * (1.0 + jnp.tanh(math.sqrt(2.0 / math.pi) * (x + 0.044715 * (x * x * x))))


def _dot(a, b, precision=None):
    return jnp.dot(a, b, precision=precision, preferred_element_type=F32)


def _dot_nt(a, b, precision=None):
    return lax.dot_general(a, b, (((1,), (1,)), ((), ())), precision=precision, preferred_element_type=F32)


def _dot_tn(a, b, precision=None):
    return lax.dot_general(a, b, (((0,), (0,)), ((), ())), precision=precision, preferred_element_type=F32)


def _head_rms(o, g):
    return o * lax.rsqrt(jnp.mean(o * o, -1, keepdims=True) + EPS) * g


class _Group:
    def __init__(self, n_tok, tile, per_token_mod, tiles_per_seq, conv_shift):
        self.n_tok = n_tok
        self.tile = tile
        self.n_tiles = n_tok // tile
        self.per_token_mod = per_token_mod
        self.tiles_per_seq = tiles_per_seq
        self.conv_shift = conv_shift

    def retiled(self, tile):
        seq_rows = self.tile * self.tiles_per_seq
        if seq_rows % tile or self.n_tok % tile:
            return self
        return _Group(self.n_tok, tile, self.per_token_mod, seq_rows // tile, self.conv_shift)

    def mod_spec(self, width, col_block, m_axis):
        if self.per_token_mod:
            return pl.BlockSpec((1, self.tile, width), lambda *g: (0, g[m_axis], col_block(*g)))
        tps = self.tiles_per_seq
        return pl.BlockSpec((1, 1, width), lambda *g: (g[m_axis] // tps, 0, col_block(*g)))


def _ada_kernel(c_ref, w_ref, b_ref, o_ref):
    cs = _silu(c_ref[...]).astype(BF16)
    o_ref[0] = _dot(cs, w_ref[0].astype(BF16)) + b_ref[0]


def _ada_mod(c, w_ada, b_ada):
    depth, d, n = w_ada.shape
    rows = c.shape[0]
    tn = 1024
    return pl.pallas_call(
        _ada_kernel,
        grid=(depth, n // tn),
        in_specs=[pl.BlockSpec((rows, d), lambda l, j: (0, 0)),
                  pl.BlockSpec((1, d, tn), lambda l, j: (l, 0, j)),
                  pl.BlockSpec((1, 1, tn), lambda l, j: (l, 0, j))],
        out_specs=pl.BlockSpec((1, rows, tn), lambda l, j: (l, 0, j)),
        out_shape=jax.ShapeDtypeStruct((depth, rows, n), F32),
        compiler_params=_cparams(2), name="ada_mod",
    )(c, w_ada, b_ada.reshape(depth, 1, n))


def _norm_mod_kernel(x_ref, g_ref, sc_ref, sh_ref, o_ref):
    x = x_ref[...]
    y = x * lax.rsqrt(jnp.mean(x * x, -1, keepdims=True) + EPS) * g_ref[...]
    o_ref[...] = (y * (1.0 + sc_ref[0]) + sh_ref[0]).astype(BF16)


def _norm_mod(x, g, mod, grp, sc_blk, sh_blk):
    d = x.shape[1]
    return pl.pallas_call(
        _norm_mod_kernel,
        grid=(grp.n_tiles,),
        in_specs=[pl.BlockSpec((grp.tile, d), lambda i: (i, 0)),
                  pl.BlockSpec((1, d), lambda i: (0, 0)),
                  grp.mod_spec(d, lambda i: sc_blk, 0),
                  grp.mod_spec(d, lambda i: sh_blk, 0)],
        out_specs=pl.BlockSpec((grp.tile, d), lambda i: (i, 0)),
        out_shape=jax.ShapeDtypeStruct(x.shape, BF16),
        compiler_params=_cparams(1), name="norm_mod",
    )(x, g.reshape(1, d), mod, mod)


def _rms_kernel(x_ref, g_ref, o_ref):
    x = x_ref[...]
    o_ref[...] = x * lax.rsqrt(jnp.mean(x * x, -1, keepdims=True) + EPS) * g_ref[...]


def _final_rms(x, g, grp):
    d = x.shape[1]
    return pl.pallas_call(
        _rms_kernel,
        grid=(grp.n_tiles,),
        in_specs=[pl.BlockSpec((grp.tile, d), lambda i: (i, 0)), pl.BlockSpec((1, d), lambda i: (0, 0))],
        out_specs=pl.BlockSpec((grp.tile, d), lambda i: (i, 0)),
        out_shape=jax.ShapeDtypeStruct(x.shape, F32),
        compiler_params=_cparams(1), name="final_rms",
    )(x, g.reshape(1, d))


def _cast_weights_once(pairs):
    @pl.when(pl.program_id(1) == 0)
    def _():
        for w_ref, wb_scr in pairs:
            wb_scr[...] = (w_ref[0] if len(w_ref.shape) == 3 else w_ref[...]).astype(BF16)


def _matmul_kernel(a_ref, w_ref, o_ref, wb_scr):
    _cast_weights_once([(w_ref, wb_scr)])
    o_ref[...] = _dot(a_ref[...], wb_scr[...])


def _matmul(a, w, grp, tn):
    k, n = w.shape
    return pl.pallas_call(
        _matmul_kernel,
        grid=(n // tn, grp.n_tiles),
        in_specs=[pl.BlockSpec((grp.tile, k), lambda j, i: (i, 0)),
                  pl.BlockSpec((k, tn), lambda j, i: (0, j))],
        out_specs=pl.BlockSpec((grp.tile, tn), lambda j, i: (i, j)),
        out_shape=jax.ShapeDtypeStruct((a.shape[0], n), F32),
        scratch_shapes=[pltpu.VMEM((k, tn), BF16)],
        compiler_params=_cparams(2), name="matmul",
    )(a, w)


def _mm_res_kernel(*refs, n_pairs):
    x_ref, gate_ref, o_ref = refs[2 * n_pairs:2 * n_pairs + 3]
    wb = refs[2 * n_pairs + 3:]
    _cast_weights_once([(refs[2 * p + 1], wb[p]) for p in range(n_pairs)])
    y = _dot(refs[0][...].astype(BF16), wb[0][...])
    for p in range(1, n_pairs):
        y = y + _dot(refs[2 * p][...].astype(BF16), wb[p][...])
    o_ref[...] = x_ref[...] + gate_ref[0] * y


def _mm_residual(pairs, x, mod, grp, gate_blk, tn):
    n = x.shape[1]
    in_specs, args = [], []
    for a, w, layer, row_blk in pairs:
        k = a.shape[1]
        in_specs += [pl.BlockSpec((grp.tile, k), lambda j, i: (i, 0)),
                     pl.BlockSpec((1, k, tn), lambda j, i, layer=layer, row_blk=row_blk: (layer, row_blk, j))]
        args += [a, w]
    in_specs += [pl.BlockSpec((grp.tile, tn), lambda j, i: (i, j)),
                 grp.mod_spec(tn, lambda j, i: gate_blk * (n // tn) + j, 1)]
    return pl.pallas_call(
        functools.partial(_mm_res_kernel, n_pairs=len(pairs)),
        grid=(n // tn, grp.n_tiles),
        in_specs=in_specs,
        out_specs=pl.BlockSpec((grp.tile, tn), lambda j, i: (i, j)),
        out_shape=jax.ShapeDtypeStruct(x.shape, F32),
        scratch_shapes=[pltpu.VMEM((a.shape[1], tn), BF16) for a, _, _, _ in pairs],
        compiler_params=_cparams(2), name="mm_residual",
    )(*args, x, mod)


def _ffn_up_kernel(h_ref, wa_ref, wg_ref, cwa_ref, cwg_ref, cba_ref, cbg_ref, ha_ref, hg_ref,
                   act_ref, sta_ref, stg_ref, scr_a, scr_g, wba_scr, wbg_scr, *, shift, tile, sub, tiles_per_seq):
    i = pl.program_id(1)
    hist = (FFN_K - 1) * shift
    base = -(-hist // SUBLANE) * SUBLANE
    _cast_weights_once([(wa_ref, wba_scr), (wg_ref, wbg_scr)])

    @pl.when(i % tiles_per_seq == 0)
    def _():
        scr_a[base - hist:base, :] = ha_ref[0]
        scr_g[base - hist:base, :] = hg_ref[0]

    def conv(scr, cw_ref, cb_ref, r0):
        y = cb_ref[0]
        for j in range(FFN_K):
            lo = base + r0 - (FFN_K - 1 - j) * shift
            y = y + scr[lo:lo + sub, :] * cw_ref[0, j:j + 1, :]
        return y

    def project(r0):
        h = h_ref[r0:r0 + sub, :]
        scr_a[base + r0:base + r0 + sub, :] = _dot(h, wba_scr[...])
        scr_g[base + r0:base + r0 + sub, :] = _dot(h, wbg_scr[...])

    project(0)
    for r0 in range(0, tile, sub):
        if r0 + sub < tile:
            project(r0 + sub)
        a = conv(scr_a, cwa_ref, cba_ref, r0)
        g = conv(scr_g, cwg_ref, cbg_ref, r0)
        act_ref[r0:r0 + sub, :] = (_silu(g) * a).astype(BF16)
    last_a = scr_a[base + tile - hist:base + tile, :]
    last_g = scr_g[base + tile - hist:base + tile, :]
    sta_ref[0] = last_a
    stg_ref[0] = last_g
    scr_a[base - hist:base, :] = last_a
    scr_g[base - hist:base, :] = last_g


def _ffn_up(h, w_up, conv_w, conv_b, layer, hist0, grp):
    d = h.shape[1]
    tn = 512
    nj = D_FF // tn
    shift = grp.conv_shift
    hist = (FFN_K - 1) * shift
    base = -(-hist // SUBLANE) * SUBLANE
    n_seq = grp.n_tiles // grp.tiles_per_seq
    tps = grp.tiles_per_seq
    cb = conv_b.reshape(conv_b.shape[0], 1, 2 * D_FF)
    kern = functools.partial(_ffn_up_kernel, shift=shift, tile=grp.tile, sub=min(FFN_SUB, grp.tile),
                             tiles_per_seq=tps)
    assert grp.tile % min(FFN_SUB, grp.tile) == 0
    act, st_a, st_g = pl.pallas_call(
        kern,
        grid=(nj, grp.n_tiles),
        in_specs=[pl.BlockSpec((grp.tile, d), lambda j, i: (i, 0)),
                  pl.BlockSpec((1, d, tn), lambda j, i: (layer, 0, j)),
                  pl.BlockSpec((1, d, tn), lambda j, i: (layer, 0, nj + j)),
                  pl.BlockSpec((1, FFN_K, tn), lambda j, i: (layer, 0, j)),
                  pl.BlockSpec((1, FFN_K, tn), lambda j, i: (layer, 0, nj + j)),
                  pl.BlockSpec((1, 1, tn), lambda j, i: (layer, 0, j)),
                  pl.BlockSpec((1, 1, tn), lambda j, i: (layer, 0, nj + j)),
                  pl.BlockSpec((1, hist, tn), lambda j, i: (i // tps, 0, j)),
                  pl.BlockSpec((1, hist, tn), lambda j, i: (i // tps, 0, nj + j))],
        out_specs=[pl.BlockSpec((grp.tile, tn), lambda j, i: (i, j)),
                   pl.BlockSpec((1, hist, tn), lambda j, i: (i // tps, 0, j)),
                   pl.BlockSpec((1, hist, tn), lambda j, i: (i // tps, 0, j))],
        out_shape=[jax.ShapeDtypeStruct((h.shape[0], D_FF), BF16),
                   jax.ShapeDtypeStruct((n_seq, hist, D_FF), F32),
                   jax.ShapeDtypeStruct((n_seq, hist, D_FF), F32)],
        scratch_shapes=[pltpu.VMEM((base + grp.tile, tn), F32), pltpu.VMEM((base + grp.tile, tn), F32),
                        pltpu.VMEM((d, tn), BF16), pltpu.VMEM((d, tn), BF16)],
        compiler_params=_cparams(2), name="ffn_up",
    )(h, w_up, w_up, conv_w, conv_w, cb, cb, hist0, hist0)
    return act, jnp.concatenate([st_a, st_g], axis=-1)


def _glu_kernel(y_ref, w_ref, b_ref, o_ref):
    z5 = _gelu_tanh(y_ref[...])
    o_ref[...] = (z5 * _sigmoid(_dot(z5.astype(BF16), w_ref[...].astype(BF16)) + b_ref[...])).astype(o_ref.dtype)


def _s5_glu(yd, w, b, grp):
    n = yd.shape[1]
    return pl.pallas_call(
        _glu_kernel,
        grid=(grp.n_tiles,),
        in_specs=[pl.BlockSpec((grp.tile, n), lambda i: (i, 0)),
                  pl.BlockSpec((n, n), lambda i: (0, 0)),
                  pl.BlockSpec((1, n), lambda i: (0, 0))],
        out_specs=pl.BlockSpec((grp.tile, n), lambda i: (i, 0)),
        out_shape=jax.ShapeDtypeStruct(yd.shape, BF16),
        compiler_params=_cparams(1), name="s5_glu",
    )(yd, w, b.reshape(1, n))


def _mixer_out_dtype(block):
    return BF16 if block % (2 * SUBLANE) == 0 else F32


def _causal_conv_chunk(x, cv_scr, cw_ref, cb_ref, chunk, valid):
    base = SUBLANE
    cv_scr[base:base + chunk, :] = x
    y = cb_ref[...]
    for j in range(CONV_K):
        lo = base - (CONV_K - 1) + j
        y = y + cv_scr[lo:lo + chunk, :] * cw_ref[j:j + 1, :]
    last = cv_scr[base + valid - (CONV_K - 1):base + valid, :]
    cv_scr[base - (CONV_K - 1):base, :] = last
    return y


def _gla_kernel(q_ref, k_ref, v_ref, r_ref, sm_ref, w2_ref, b2_ref, ng_ref, s0_ref,
                o_ref, sout_ref, s_scr, *, chunk, group, block, valid, nseq):
    blk = pl.program_id(1)

    @pl.when(blk == 0)
    def _():
        s_scr[...] = s0_ref[...]

    span = chunk * group
    row3 = lax.broadcasted_iota(jnp.int32, (1, chunk, 1), 1)
    row_in_chunk = lax.broadcasted_iota(jnp.int32, (span, 1), 0) % chunk
    ri = lax.broadcasted_iota(jnp.int32, (span, span), 0)
    ci = lax.broadcasted_iota(jnp.int32, (span, span), 1)
    tri = ((ri // chunk == ci // chunk) & (ri >= ci)).astype(F32)
    eye = (lax.broadcasted_iota(jnp.int32, (GLA_DK, GLA_DK), 0)
           == lax.broadcasted_iota(jnp.int32, (GLA_DK, GLA_DK), 1))
    n_valid = min(valid, chunk)
    heads = range(GLA_H)
    chunks = range(group)

    def to3(t):
        return t.reshape(group, chunk, t.shape[-1])

    def one_seq(sq, rows):
        x = _dot(sm_ref[sq, rows, :], w2_ref[...], HIGHEST) + b2_ref[...]
        log_a = (jnp.minimum(x, 0.0) - jnp.log(1.0 + jnp.exp(-jnp.abs(x)))) * (1.0 / GLA_GATE_NORM)
        if valid < chunk:
            log_a = jnp.where(row_in_chunk < valid, log_a, 0.0)
        b = _dot(tri, log_a, HIGHEST)
        q = q_ref[sq, rows, :] * GLA_DK ** -0.5
        k = k_ref[sq, rows, :]
        v = v_ref[sq, rows, :]
        r = r_ref[sq, rows, :]
        b3, q_in, kv, d_col = [], [], [], []
        for h in heads:
            ks = slice(h * GLA_DK, (h + 1) * GLA_DK)
            bh3 = to3(b[:, ks])
            b_last = bh3[:, chunk - 1:chunk, :]
            k_out = (to3(k[:, ks]) * jnp.exp(b_last - bh3)).reshape(span, GLA_DK)
            vh = v[:, h * GLA_DV:(h + 1) * GLA_DV]
            b3.append(bh3)
            q_in.append(q[:, ks] * jnp.exp(b[:, ks]))
            kv.append([_dot_tn(k_out[c * chunk:(c + 1) * chunk], vh[c * chunk:(c + 1) * chunk]) for c in chunks])
            d_col.append([jnp.sum(jnp.where(eye, jnp.exp(b_last[c]), 0.0), axis=-1, keepdims=True) for c in chunks])
        st = [s_scr[sq, h] for h in heads]
        o_inter = [[] for _ in heads]
        for c in chunks:
            for h in heads:
                o_inter[h].append(_dot(q_in[h][c * chunk:(c + 1) * chunk], st[h]))
                st[h] = st[h] * d_col[h][c] + kv[h][c]
        for h in heads:
            ks = slice(h * GLA_DK, (h + 1) * GLA_DK)
            vs = slice(h * GLA_DV, (h + 1) * GLA_DV)
            s_scr[sq, h] = st[h]
            bh3, qh3, kh3, vh3 = b3[h], to3(q[:, ks]), to3(k[:, ks]), to3(v[:, vs])
            o3 = jnp.zeros((group, chunk, GLA_DV), F32)
            for j in range(n_valid):
                e = jnp.exp(jnp.minimum(bh3 - bh3[:, j:j + 1, :], 0.0))
                sj = jnp.sum(qh3 * e * kh3[:, j:j + 1, :], axis=-1, keepdims=True)
                o3 = o3 + jnp.where(row3 >= j, sj, 0.0) * vh3[:, j:j + 1, :]
            o = o3.reshape(span, GLA_DV) + jnp.concatenate(o_inter[h], axis=0)
            o_ref[sq, rows, vs] = (_head_rms(o, ng_ref[...]) * _silu(r[:, vs])).astype(o_ref.dtype)

    def do_span(s, carry):
        rows = pl.ds(pl.multiple_of(s * span, span), span)
        for sq in range(nseq):
            one_seq(sq, rows)
        return carry

    lax.fori_loop(0, block // span, do_span, 0)

    @pl.when(blk == pl.num_programs(1) - 1)
    def _():
        sout_ref[...] = s_scr[...]


def _gla(proj, w2p, b2, ng, s0, chunk, group, block, valid, nseq):
    nb, seq, _ = proj.shape

    def col(width, off):
        return pl.BlockSpec((nseq, block, width), lambda b, i: (b, i, off // width))

    st_spec = pl.BlockSpec((nseq, GLA_H, GLA_DK, GLA_DV), lambda b, i: (b, 0, 0, 0))
    kern = functools.partial(_gla_kernel, chunk=chunk, group=group, block=block, valid=valid, nseq=nseq)
    return pl.pallas_call(
        kern,
        grid=(nb // nseq, seq // block),
        in_specs=[col(GLA_QK, AB_Q), col(GLA_QK, AB_K), col(GLA_V, AB_V), col(GLA_V, AB_R), col(LANE, AB_SMALL),
                  pl.BlockSpec((LANE, GLA_QK), lambda b, i: (0, 0)),
                  pl.BlockSpec((1, GLA_QK), lambda b, i: (0, 0)),
                  pl.BlockSpec((1, GLA_DV), lambda b, i: (0, 0)),
                  st_spec],
        out_specs=[pl.BlockSpec((nseq, block, GLA_V), lambda b, i: (b, i, 0)), st_spec],
        out_shape=[jax.ShapeDtypeStruct((nb, seq, GLA_V), _mixer_out_dtype(block)),
                   jax.ShapeDtypeStruct((nb, GLA_H, GLA_DK, GLA_DV), F32)],
        scratch_shapes=[pltpu.VMEM((nseq, GLA_H, GLA_DK, GLA_DV), F32)],
        compiler_params=_cparams(2), name="gla",
    )(proj, proj, proj, proj, proj, w2p, b2.reshape(1, GLA_QK), ng.reshape(1, GLA_DV), s0)


def _split2(a):
    hi = a.astype(BF16)
    return hi, (a - hi.astype(F32)).astype(BF16)


def _dot3(a, b):
    return _dot(a[0], b[0]) + _dot(a[0], b[1]) + _dot(a[1], b[0])


def _inv_unit_lower_many(mats, n, eye):
    ps = [eye - a for a in mats]
    if n <= 2:
        return ps
    pows = [_split2(a) for a in mats]
    k = 2
    pending = None
    while k < n:
        sq = [_dot3(a, a) for a in pows]
        if pending is not None:
            ps = [p + _dot3(_split2(p), f) for p, f in zip(ps, pending)]
        pows = [_split2(a) for a in sq]
        pending = pows
        k *= 2
    return [p + _dot3(_split2(p), f) for p, f in zip(ps, pending)]


def _gdn_kernel(qkv_ref, sm_ref, gb_ref, cw_ref, cb_ref, alog_ref, dtb_ref, ng_ref, s0_ref, c0_ref,
                o_ref, sout_ref, cout_ref, s_scr, cv_scr, *, chunk, block, valid, nseq):
    blk = pl.program_id(1)

    @pl.when(blk == 0)
    def _():
        s_scr[...] = s0_ref[...]
        cv_scr[:, SUBLANE - (CONV_K - 1):SUBLANE, :] = c0_ref[...]

    ri = lax.broadcasted_iota(jnp.int32, (chunk, chunk), 0)
    ci = lax.broadcasted_iota(jnp.int32, (chunk, chunk), 1)
    causal = ri >= ci
    strict = ri > ci
    eye = (ri == ci).astype(F32)
    tri = causal.astype(F32)
    tri_u = (ri <= ci).astype(F32)
    inv_blk = min(INV_BLOCK, chunk)
    same_blk = (ri // inv_blk) == (ci // inv_blk)
    row = lax.broadcasted_iota(jnp.int32, (chunk, 1), 0)
    n_valid = min(valid, chunk)

    def do_chunk(s, carry):
        rows = pl.ds(pl.multiple_of(s * chunk, chunk), chunk)
        units = [(sq, h) for sq in range(nseq) for h in range(GDN_H)]
        heads = range(len(units))
        q, k, kb, rhs, dec, gcc = [], [], [], [], [], []
        for sq, h in units:
            if h == 0:
                act = _silu(_causal_conv_chunk(qkv_ref[sq, rows, :], cv_scr.at[sq], cw_ref, cb_ref, chunk, n_valid))
                sm = sm_ref[sq, rows, :]
                g_all = -jnp.exp(alog_ref[...]) * _softplus(sm + dtb_ref[...])
                beta_all = _sigmoid(sm)
                if valid < chunk:
                    g_all = jnp.where(row < valid, g_all, 0.0)
                    beta_all = jnp.where(row < valid, beta_all, 0.0)
                gc = _dot(tri, g_all, HIGHEST)
                gc_r = _dot_tn(g_all, tri_u, HIGHEST)
            qh = act[:, h * GDN_DK:(h + 1) * GDN_DK]
            kh = act[:, GDN_QK + h * GDN_DK:GDN_QK + (h + 1) * GDN_DK]
            vh = act[:, 2 * GDN_QK + h * GDN_DV:2 * GDN_QK + (h + 1) * GDN_DV]
            qh = qh * lax.rsqrt(jnp.sum(qh * qh, -1, keepdims=True) + EPS) * GDN_DK ** -0.5
            kh = kh * lax.rsqrt(jnp.sum(kh * kh, -1, keepdims=True) + EPS)
            beta = beta_all[:, AB_B_LANE + h:AB_B_LANE + h + 1]
            gch = gc[:, AB_A_LANE + h:AB_A_LANE + h + 1]
            gcr = gc_r[AB_A_LANE + h:AB_A_LANE + h + 1, :]
            q.append(qh)
            k.append(kh)
            kb.append(kh * beta)
            rhs.append(_split2(jnp.concatenate([vh * beta, kb[-1] * jnp.exp(gch)], axis=1)))
            dec.append(jnp.exp(jnp.where(causal, gch - gcr, -jnp.inf)))
            gcc.append(gch)
        kbf = [a.astype(BF16) for a in k]
        m = [jnp.where(strict, _dot_nt(kb[h].astype(BF16), kbf[h]) * dec[h], 0.0) for h in heads]
        att = [_dot_nt(q[h].astype(BF16), kbf[h]) * dec[h] for h in heads]
        m_diag = [jnp.where(same_blk, a, 0.0) for a in m]
        t = [_split2(a) for a in _inv_unit_lower_many(m_diag, inv_blk, eye)]
        y = [_dot3(t[h], rhs[h]) for h in heads]
        if chunk > inv_blk:
            n_off = [_dot3(t[h], _split2(m[h] - m_diag[h])) for h in heads]
            qn = [_split2(a) for a in _inv_unit_lower_many(n_off, chunk // inv_blk, eye)]
            y = [_dot3(qn[h], _split2(y[h])) for h in heads]
        st = [s_scr[sq, h] for sq, h in units]
        stb = [a.astype(BF16) for a in st]
        v_new = [y[h][:, :GDN_DV] - _dot(y[h][:, GDN_DV:].astype(BF16), stb[h]) for h in heads]
        vnb = [a.astype(BF16) for a in v_new]
        o = [_dot((q[h] * jnp.exp(gcc[h])).astype(BF16), stb[h]) + _dot(att[h].astype(BF16), vnb[h]) for h in heads]
        for u, (sq, h) in enumerate(units):
            hs = slice(h * GDN_DV, (h + 1) * GDN_DV)
            g_last = gcc[u][chunk - 1:chunk, :]
            k_out = (k[u] * jnp.exp(g_last - gcc[u])).astype(BF16)
            s_scr[sq, h] = st[u] * jnp.exp(g_last) + _dot_tn(k_out, vnb[u])
            o_ref[sq, rows, hs] = (_head_rms(o[u], ng_ref[...]) * _silu(gb_ref[sq, rows, hs])).astype(o_ref.dtype)
        return carry

    lax.fori_loop(0, block // chunk, do_chunk, 0)

    @pl.when(blk == pl.num_programs(1) - 1)
    def _():
        sout_ref[...] = s_scr[...]
        cout_ref[...] = cv_scr[:, SUBLANE - (CONV_K - 1):SUBLANE, :]


def _gdn(proj, conv_w, conv_b, alog_row, dtb_row, ng, s0, c0, chunk, block, valid, nseq):
    nb, seq, _ = proj.shape

    def col(width, off):
        return pl.BlockSpec((nseq, block, width), lambda b, i: (b, i, off // width))

    def const(shape):
        return pl.BlockSpec(shape, lambda b, i: (0,) * len(shape))

    st_spec = pl.BlockSpec((nseq, GDN_H, GDN_DK, GDN_DV), lambda b, i: (b, 0, 0, 0))
    cv_spec = pl.BlockSpec((nseq, CONV_K - 1, GDN_CONV_W), lambda b, i: (b, 0, 0))
    kern = functools.partial(_gdn_kernel, chunk=chunk, block=block, valid=valid, nseq=nseq)
    return pl.pallas_call(
        kern,
        grid=(nb // nseq, seq // block),
        in_specs=[col(GDN_CONV_W, AB_QKV), col(LANE, AB_SMALL), col(GDN_V, AB_G),
                  const((CONV_K, GDN_CONV_W)), const((1, GDN_CONV_W)), const((1, LANE)), const((1, LANE)),
                  const((1, GDN_DV)), st_spec, cv_spec],
        out_specs=[pl.BlockSpec((nseq, block, GDN_V), lambda b, i: (b, i, 0)), st_spec, cv_spec],
        out_shape=[jax.ShapeDtypeStruct((nb, seq, GDN_V), _mixer_out_dtype(block)),
                   jax.ShapeDtypeStruct((nb, GDN_H, GDN_DK, GDN_DV), F32),
                   jax.ShapeDtypeStruct((nb, CONV_K - 1, GDN_CONV_W), F32)],
        scratch_shapes=[pltpu.VMEM((nseq, GDN_H, GDN_DK, GDN_DV), F32),
                        pltpu.VMEM((nseq, SUBLANE + chunk, GDN_CONV_W), F32)],
        compiler_params=_cparams(2), name="gdn",
    )(proj, proj, proj, conv_w, conv_b.reshape(1, GDN_CONV_W), alog_row, dtb_row, ng.reshape(1, GDN_DV), s0, c0)


def _ssd_kernel(xbc_ref, sm_ref, z_ref, cw_ref, cb_ref, alog_ref, dtb_ref, drow_ref, ng_ref, s0_ref, c0_ref,
                o_ref, sout_ref, cout_ref, s_scr, cv_scr, *, chunk, block, valid, nseq):
    blk = pl.program_id(1)

    @pl.when(blk == 0)
    def _():
        s_scr[...] = s0_ref[...]
        cv_scr[:, SUBLANE - (CONV_K - 1):SUBLANE, :] = c0_ref[...]

    ri = lax.broadcasted_iota(jnp.int32, (chunk, chunk), 0)
    ci = lax.broadcasted_iota(jnp.int32, (chunk, chunk), 1)
    causal = ri >= ci
    eye = (ri == ci).astype(F32)
    tri = causal.astype(F32)
    tri_u = (ri <= ci).astype(F32)
    row = lax.broadcasted_iota(jnp.int32, (chunk, 1), 0)
    lane_lo = lax.broadcasted_iota(jnp.int32, (chunk, LANE), 1) < SSD_P
    row_lo = lax.broadcasted_iota(jnp.int32, (2 * SSD_P, 1), 0) < SSD_P
    n_valid = min(valid, chunk)
    heads_per_group = SSD_H // SSD_G
    gsz = SSD_W // SSD_G

    def one_seq(sq, rows):
        act = _silu(_causal_conv_chunk(xbc_ref[sq, rows, :], cv_scr.at[sq], cw_ref, cb_ref, chunk, n_valid))
        dt = _softplus(sm_ref[sq, rows, :] + dtb_ref[...])
        if valid < chunk:
            dt = jnp.where(row < valid, dt, 0.0)
        dta = dt * (-jnp.exp(alog_ref[...]))
        acs = _dot(tri, dta, HIGHEST)
        acs_r = _dot_tn(dta, tri_u, HIGHEST)
        dt_r = _dot_tn(dt, eye, HIGHEST)
        z = z_ref[sq, rows, :]
        for g in range(SSD_G):
            bg = act[:, SSD_W + g * SSD_N:SSD_W + (g + 1) * SSD_N]
            cg = act[:, SSD_W + SSD_G * SSD_N + g * SSD_N:SSD_W + SSD_G * SSD_N + (g + 1) * SSD_N]
            cb = _dot_nt(cg, bg)
            parts = []
            for pr in range(heads_per_group // 2):
                pi = g * (heads_per_group // 2) + pr
                xp = act[:, pi * LANE:(pi + 1) * LANE]
                st = s_scr[sq, pi]
                y_in, e_in, w_out, d_last = [], [], [], []
                for hh in range(2):
                    h = 2 * pi + hh
                    ac = acs[:, h:h + 1]
                    dec = jnp.exp(jnp.where(causal, ac - acs_r[h:h + 1, :], -jnp.inf))
                    y_in.append(_dot(cb * dec * dt_r[h:h + 1, :], xp))
                    a_last = ac[chunk - 1:chunk, :]
                    e_in.append(jnp.exp(ac))
                    w_out.append(jnp.exp(a_last - ac) * dt[:, h:h + 1])
                    d_last.append(jnp.exp(a_last))
                y = jnp.where(lane_lo, y_in[0], y_in[1])
                y = y + _dot_nt(cg, st) * jnp.where(lane_lo, e_in[0], e_in[1])
                y = y + drow_ref[:, pi * LANE:(pi + 1) * LANE] * xp
                x_sc = xp * jnp.where(lane_lo, w_out[0], w_out[1])
                s_scr[sq, pi] = st * jnp.where(row_lo, d_last[0], d_last[1]) + _dot_tn(x_sc, bg)
                parts.append(y)
            gs = slice(g * gsz, (g + 1) * gsz)
            yg = jnp.concatenate(parts, axis=1) * _silu(z[:, gs])
            o_ref[sq, rows, gs] = _head_rms(yg, ng_ref[:, gs]).astype(o_ref.dtype)

    def do_chunk(s, carry):
        rows = pl.ds(pl.multiple_of(s * chunk, chunk), chunk)
        for sq in range(nseq):
            one_seq(sq, rows)
        return carry

    lax.fori_loop(0, block // chunk, do_chunk, 0)

    @pl.when(blk == pl.num_programs(1) - 1)
    def _():
        sout_ref[...] = s_scr[...]
        cout_ref[...] = cv_scr[:, SUBLANE - (CONV_K - 1):SUBLANE, :]


def _ssd(proj, conv_w, conv_b, alog_row, dtb_row, d_row, ng, s0, c0, chunk, block, valid, nseq):
    nb, seq, _ = proj.shape
    n_pairs = SSD_H // 2

    def col(width, off):
        return pl.BlockSpec((nseq, block, width), lambda b, i: (b, i, off // width))

    def const(shape):
        return pl.BlockSpec(shape, lambda b, i: (0,) * len(shape))

    st_spec = pl.BlockSpec((nseq, n_pairs, 2 * SSD_P, SSD_N), lambda b, i: (b, 0, 0, 0))
    cv_spec = pl.BlockSpec((nseq, CONV_K - 1, SSD_CONV_W), lambda b, i: (b, 0, 0))
    kern = functools.partial(_ssd_kernel, chunk=chunk, block=block, valid=valid, nseq=nseq)
    o, s_new, c_new = pl.pallas_call(
        kern,
        grid=(nb // nseq, seq // block),
        in_specs=[col(SSD_CONV_W, CD_XBC), col(LANE, CD_SMALL), col(SSD_W, CD_Z),
                  const((CONV_K, SSD_CONV_W)), const((1, SSD_CONV_W)), const((1, LANE)), const((1, LANE)),
                  const((1, SSD_W)), const((1, SSD_W)), st_spec, cv_spec],
        out_specs=[pl.BlockSpec((nseq, block, SSD_W), lambda b, i: (b, i, 0)), st_spec, cv_spec],
        out_shape=[jax.ShapeDtypeStruct((nb, seq, SSD_W), _mixer_out_dtype(block)),
                   jax.ShapeDtypeStruct((nb, n_pairs, 2 * SSD_P, SSD_N), F32),
                   jax.ShapeDtypeStruct((nb, CONV_K - 1, SSD_CONV_W), F32)],
        scratch_shapes=[pltpu.VMEM((nseq, n_pairs, 2 * SSD_P, SSD_N), F32),
                        pltpu.VMEM((nseq, SUBLANE + chunk, SSD_CONV_W), F32)],
        compiler_params=_cparams(2), name="ssd",
    )(proj, proj, proj, conv_w, conv_b.reshape(1, SSD_CONV_W), alog_row, dtb_row, d_row,
      ng.reshape(1, SSD_W), s0.reshape(nb, n_pairs, 2 * SSD_P, SSD_N), c0)
    return o, s_new.reshape(nb, SSD_H, SSD_P, SSD_N), c_new


def _s5_prep_kernel(are_ref, aim_ref, ldt_ref, bre_ref, bim_ref, lbr_ref, lbi_ref, bbr_ref, bbi_ref):
    a_re, a_im = are_ref[...], aim_ref[...]
    dt = jnp.exp(ldt_ref[...])
    mag = jnp.exp(a_re * dt)
    lb_re, lb_im = mag * jnp.cos(a_im * dt), mag * jnp.sin(a_im * dt)
    nr, ni = lb_re - 1.0, lb_im
    den = a_re * a_re + a_im * a_im
    f_re = (nr * a_re + ni * a_im) / den
    f_im = (ni * a_re - nr * a_im) / den
    b_re, b_im = bre_ref[...], bim_ref[...]
    lbr_ref[...] = lb_re
    lbi_ref[...] = lb_im
    bbr_ref[...] = f_re * b_re - f_im * b_im
    bbi_ref[...] = f_re * b_im + f_im * b_re


def _s5_prep(a_re, a_im, log_dt, b_re, b_im):
    g3 = (S5_G, 1, S5_P)
    b3 = (S5_G, S5_GS, S5_P)
    return pl.pallas_call(
        _s5_prep_kernel,
        out_shape=[jax.ShapeDtypeStruct(g3, F32), jax.ShapeDtypeStruct(g3, F32),
                   jax.ShapeDtypeStruct(b3, F32), jax.ShapeDtypeStruct(b3, F32)],
        name="s5_prep",
    )(a_re.reshape(g3), a_im.reshape(g3), log_dt.reshape(S5_G, 1, 1),
      jnp.swapaxes(b_re, 1, 2), jnp.swapaxes(b_im, 1, 2))


def _block_diag(blocks):
    g, r, c = blocks.shape
    per = 8
    b = blocks.reshape(g // per, per, r, 1, c) * jnp.eye(per, dtype=blocks.dtype).reshape(1, per, 1, per, 1)
    return b.reshape(g // per, per * r, per * c)


def _s5_kernel(u_ref, wre_ref, wim_ref, cre_ref, cim_ref, lbr_ref, lbi_ref, d_ref, x0r_ref, x0i_ref,
               y_ref, xfr_ref, xfi_ref, xr_scr, xi_scr, sr_scr, si_scr, *, rows_per_step, steps):
    c = pl.program_id(2)

    @pl.when(c == 0)
    def _():
        xr_scr[...] = x0r_ref[0]
        xi_scr[...] = x0i_ref[0]

    u = u_ref[0]
    sr_scr[...] = _dot(u, wre_ref[0])
    si_scr[...] = _dot(u, wim_ref[0])
    l_re, l_im = lbr_ref[...], lbi_ref[...]

    def step(t, carry):
        rows = pl.ds(pl.multiple_of(t * rows_per_step, rows_per_step), rows_per_step)
        xr, xi = xr_scr[...], xi_scr[...]
        nr = l_re * xr - l_im * xi + sr_scr[rows, :]
        ni = l_re * xi + l_im * xr + si_scr[rows, :]
        xr_scr[...] = nr
        xi_scr[...] = ni
        sr_scr[rows, :] = nr
        si_scr[rows, :] = ni
        return carry

    lax.fori_loop(0, steps, step, 0)
    y_ref[0] = _dot(sr_scr[...], cre_ref[0]) - _dot(si_scr[...], cim_ref[0]) + d_ref[...] * u

    @pl.when(c == pl.num_programs(2) - 1)
    def _():
        xfr_ref[0] = xr_scr[...]
        xfi_ref[0] = xi_scr[...]


def _s5(proj, w_re, w_im, c_re, c_im, lb_re, lb_im, d, x0_re, x0_im, rows_per_step, steps):
    ng, n_tok, _ = proj.shape
    nj = S5_W // LANE
    sw = S5_STATE // nj
    cr = rows_per_step * steps
    kern = functools.partial(_s5_kernel, rows_per_step=rows_per_step, steps=steps)
    x_spec = pl.BlockSpec((1, rows_per_step, sw), lambda g, j, c: (g, 0, j))
    return pl.pallas_call(
        kern,
        grid=(ng, nj, n_tok // cr),
        in_specs=[pl.BlockSpec((1, cr, LANE), lambda g, j, c: (g, c, CD_U // LANE + j)),
                  pl.BlockSpec((1, LANE, sw), lambda g, j, c: (j, 0, 0)),
                  pl.BlockSpec((1, LANE, sw), lambda g, j, c: (j, 0, 0)),
                  pl.BlockSpec((1, sw, LANE), lambda g, j, c: (j, 0, 0)),
                  pl.BlockSpec((1, sw, LANE), lambda g, j, c: (j, 0, 0)),
                  pl.BlockSpec((1, sw), lambda g, j, c: (0, j)),
                  pl.BlockSpec((1, sw), lambda g, j, c: (0, j)),
                  pl.BlockSpec((1, LANE), lambda g, j, c: (0, j)),
                  x_spec, x_spec],
        out_specs=[pl.BlockSpec((1, cr, LANE), lambda g, j, c: (g, c, j)), x_spec, x_spec],
        out_shape=[jax.ShapeDtypeStruct((ng, n_tok, S5_W), F32),
                   jax.ShapeDtypeStruct((ng, rows_per_step, S5_STATE), F32),
                   jax.ShapeDtypeStruct((ng, rows_per_step, S5_STATE), F32)],
        scratch_shapes=[pltpu.VMEM((rows_per_step, sw), F32), pltpu.VMEM((rows_per_step, sw), F32),
                        pltpu.VMEM((cr, sw), F32), pltpu.VMEM((cr, sw), F32)],
        compiler_params=_cparams(3), name="s5_scan",
    )(proj, w_re, w_im, c_re, c_im, lb_re, lb_im, d.reshape(1, S5_W), x0_re, x0_im)


def _s5_pow_kernel(lbr_ref, lbi_ref, pr_ref, pi_ref, *, n_rows):
    l_re, l_im = lbr_ref[...], lbi_ref[...]
    row = lax.broadcasted_iota(jnp.int32, (SUBLANE, 1), 0)
    p_re, p_im = l_re, l_im
    b_re = jnp.broadcast_to(l_re, (SUBLANE, l_re.shape[1]))
    b_im = jnp.broadcast_to(l_im, (SUBLANE, l_re.shape[1]))
    for r in range(1, SUBLANE):
        p_re, p_im = p_re * l_re - p_im * l_im, p_re * l_im + p_im * l_re
        b_re = jnp.where(row >= r, p_re, b_re)
        b_im = jnp.where(row >= r, p_im, b_im)
    q_re, q_im = jnp.ones_like(l_re), jnp.zeros_like(l_re)
    for a in range(n_rows // SUBLANE):
        rows = slice(a * SUBLANE, (a + 1) * SUBLANE)
        pr_ref[rows, :] = b_re * q_re - b_im * q_im
        pi_ref[rows, :] = b_re * q_im + b_im * q_re
        q_re, q_im = q_re * p_re - q_im * p_im, q_re * p_im + q_im * p_re


def _s5_pow_table(lb_re, lb_im, n_rows):
    shape = jax.ShapeDtypeStruct((n_rows, S5_STATE), F32)
    return pl.pallas_call(functools.partial(_s5_pow_kernel, n_rows=n_rows), out_shape=[shape, shape],
                          name="s5_pow")(lb_re, lb_im)


def _s5_seg_kernel(u_ref, wre_ref, wim_ref, cre_ref, cim_ref, pr_ref, pi_ref, d_ref, x0r_ref, x0i_ref,
                   y_ref, xfr_ref, xfi_ref, sr_scr, si_scr, *, seg_len):
    n_seg = SUBLANE
    n_lane_blk = sr_scr.shape[0]
    sw = n_lane_blk * LANE
    lane_blks = [slice(c * LANE, (c + 1) * LANE) for c in range(n_lane_blk)]

    def put(scr, rows, val):
        for c, ls in enumerate(lane_blks):
            scr[c, rows, :] = val[:, ls]

    def get(scr, rows):
        return jnp.concatenate([scr[c, rows, :] for c in range(n_lane_blk)], axis=1)

    for s in range(n_seg):
        us = u_ref[0, s * seg_len:(s + 1) * seg_len, :]
        put(sr_scr, pl.ds(s, seg_len, stride=n_seg), _dot(us, wre_ref[0]))
        put(si_scr, pl.ds(s, seg_len, stride=n_seg), _dot(us, wim_ref[0]))
    l_re, l_im = pr_ref[0:1, :], pi_ref[0:1, :]

    def step(t, carry):
        xr, xi = carry
        rows = pl.ds(pl.multiple_of(t * n_seg, n_seg), n_seg)
        nr = l_re * xr - l_im * xi + get(sr_scr, rows)
        ni = l_re * xi + l_im * xr + get(si_scr, rows)
        put(sr_scr, rows, nr)
        put(si_scr, rows, ni)
        return nr, ni

    zero = jnp.zeros((n_seg, sw), F32)
    end_re, end_im = lax.fori_loop(0, seg_len, step, (zero, zero), unroll=4)
    ln_re, ln_im = pr_ref[seg_len - 1:seg_len, :], pi_ref[seg_len - 1:seg_len, :]
    p_re, p_im = pr_ref[...], pi_ref[...]
    x_re, x_im = x0r_ref[0], x0i_ref[0]
    for s in range(n_seg):
        loc_re = get(sr_scr, pl.ds(s, seg_len, stride=n_seg))
        loc_im = get(si_scr, pl.ds(s, seg_len, stride=n_seg))
        t_re = loc_re + p_re * x_re - p_im * x_im
        t_im = loc_im + p_re * x_im + p_im * x_re
        us = u_ref[0, s * seg_len:(s + 1) * seg_len, :]
        y_ref[0, s * seg_len:(s + 1) * seg_len, :] = (_dot(t_re, cre_ref[0]) - _dot(t_im, cim_ref[0])
                                                      + d_ref[...] * us)
        e_re, e_im = end_re[s:s + 1, :], end_im[s:s + 1, :]
        x_re, x_im = e_re + ln_re * x_re - ln_im * x_im, e_im + ln_re * x_im + ln_im * x_re
    xfr_ref[0] = x_re
    xfi_ref[0] = x_im


def _s5_seg(proj, w_re, w_im, c_re, c_im, pow_re, pow_im, d, x0_re, x0_im):
    nb, seq, _ = proj.shape
    nj = S5_W // LANE
    sw = S5_STATE // nj
    seg_len = seq // SUBLANE
    x_spec = pl.BlockSpec((1, 1, sw), lambda b, j: (b, 0, j))
    return pl.pallas_call(
        functools.partial(_s5_seg_kernel, seg_len=seg_len),
        grid=(nb, nj),
        in_specs=[pl.BlockSpec((1, seq, LANE), lambda b, j: (b, 0, CD_U // LANE + j)),
                  pl.BlockSpec((1, LANE, sw), lambda b, j: (j, 0, 0)),
                  pl.BlockSpec((1, LANE, sw), lambda b, j: (j, 0, 0)),
                  pl.BlockSpec((1, sw, LANE), lambda b, j: (j, 0, 0)),
                  pl.BlockSpec((1, sw, LANE), lambda b, j: (j, 0, 0)),
                  pl.BlockSpec((seg_len, sw), lambda b, j: (0, j)),
                  pl.BlockSpec((seg_len, sw), lambda b, j: (0, j)),
                  pl.BlockSpec((1, LANE), lambda b, j: (0, j)),
                  x_spec, x_spec],
        out_specs=[pl.BlockSpec((1, seq, LANE), lambda b, j: (b, 0, j)), x_spec, x_spec],
        out_shape=[jax.ShapeDtypeStruct((nb, seq, S5_W), F32),
                   jax.ShapeDtypeStruct((nb, 1, S5_STATE), F32),
                   jax.ShapeDtypeStruct((nb, 1, S5_STATE), F32)],
        scratch_shapes=[pltpu.VMEM((sw // LANE, seq, LANE), F32), pltpu.VMEM((sw // LANE, seq, LANE), F32)],
        compiler_params=_cparams(2), name="s5_seg",
    )(proj, w_re, w_im, c_re, c_im, pow_re, pow_im, d.reshape(1, S5_W), x0_re, x0_im)


def _lane_row(vec, lane0):
    return jnp.zeros((1, LANE), F32).at[0, lane0:lane0 + vec.shape[0]].set(vec.astype(F32))


def _prep_params(p):
    d = D_MODEL
    q = {}
    w = p['w_in_ab'][0]
    o_lr = 2 * GLA_QK + GLA_V
    o_r = o_lr + GLA_LR
    o_qkv = o_r + GLA_V
    o_a = o_qkv + GDN_CONV_W
    o_g = o_a + 2 * GDN_H
    q['w_in_ab'] = jnp.concatenate(
        [w[:, :o_lr], w[:, o_r:o_qkv], w[:, o_qkv:o_a], w[:, o_g:], w[:, o_lr:o_r], w[:, o_a:o_g],
         jnp.zeros((d, AB_N - AB_SMALL - GLA_LR - 2 * GDN_H), F32)], axis=1)
    q['gla_w2'] = jnp.zeros((LANE, GLA_QK), F32).at[:GLA_LR].set(p['gla_w2'][0])
    q['gdn_alog'] = _lane_row(p['gdn_A_log'][0], AB_A_LANE)
    q['gdn_dtb'] = _lane_row(p['gdn_dt_bias'][0], AB_A_LANE)
    w = p['w_in_cd'][0]
    o_xbc = SSD_W
    o_dt = o_xbc + SSD_CONV_W
    o_u = o_dt + SSD_H
    q['w_in_cd'] = jnp.concatenate(
        [w[:, o_xbc:o_dt], w[:, o_dt:o_u], jnp.zeros((d, CD_Z - CD_SMALL - SSD_H), F32), w[:, :o_xbc], w[:, o_u:]],
        axis=1)
    q['ssd_alog'] = _lane_row(p['ssd_A_log'][0], 0)
    q['ssd_dtb'] = _lane_row(p['ssd_dt_bias'][0], 0)
    q['ssd_d_row'] = jnp.repeat(p['ssd_D'][0].astype(F32), SSD_P).reshape(1, SSD_W)
    lb_re, lb_im, bb_re, bb_im = _s5_prep(p['s5_A_re'][0], p['s5_A_im'][0], p['s5_log_dt'][0],
                                          p['s5_B_re'][0], p['s5_B_im'][0])
    q['s5_lb_re'], q['s5_lb_im'] = lb_re.reshape(1, S5_STATE), lb_im.reshape(1, S5_STATE)
    q['s5_w_re'], q['s5_w_im'] = _block_diag(bb_re), _block_diag(bb_im)
    q['s5_c_re'] = _block_diag(jnp.swapaxes(p['s5_C_re'][0], 1, 2))
    q['s5_c_im'] = _block_diag(jnp.swapaxes(p['s5_C_im'][0], 1, 2))
    return q


def _trunk(x, mods, grp, seq_shape, state, p, q):
    nb, seq_len, valid = seq_shape
    s_gla, s_gdn, s_gdnc, s_ssd, s_ssdc, s_re, s_im, s_ffn = state
    prompt = not grp.per_token_mod

    def to_seq(t):
        if prompt:
            return t.reshape(nb, seq_len, t.shape[-1])
        t = jnp.swapaxes(t.reshape(valid, nb, t.shape[-1]), 0, 1)
        return jnp.pad(t, ((0, 0), (0, seq_len - valid), (0, 0)))

    def from_seq(t):
        if prompt:
            return t.reshape(nb * seq_len, t.shape[-1])
        return jnp.swapaxes(t[:, :valid], 0, 1).reshape(valid * nb, t.shape[-1])

    blk = MIX_BLOCK if prompt else seq_len
    chunks = (GLA_CHUNK, GDN_CHUNK, SSD_CHUNK) if prompt else (seq_len,) * 3
    new = {}

    h = _norm_mod(x, p['g_mix'][0], mods[0], grp, 1, 0)
    proj = to_seq(_matmul(h, q['w_in_ab'], grp, AB_N // 5))
    o_a, new['gla'] = _gla(proj, q['gla_w2'], p['gla_b2'][0], p['gla_norm_g'][0], s_gla, chunks[0],
                            GLA_GROUP if prompt else 1, blk, valid, 1 if prompt else SAMPLE_SEQS)
    o_b, new['gdn'], new['gdnc'] = _gdn(proj, p['gdn_conv_w'][0], p['gdn_conv_b'][0], q['gdn_alog'], q['gdn_dtb'],
                                        p['gdn_norm_g'][0], s_gdn, s_gdnc, chunks[1], blk, valid,
                                        1 if prompt else SAMPLE_SEQS)
    x = _mm_residual([(from_seq(o_a), p['w_out_ab'], 0, 0), (from_seq(o_b), p['w_out_ab'], 0, 1)],
                     x, mods[0], grp, 2, 1024)
    h = _norm_mod(x, p['g_ffn'][0], mods[0], grp, 4, 3)
    act, new['ffn0'] = _ffn_up(h, p['w_ffn_up'], p['ffn_conv_w'], p['ffn_conv_b'], 0, s_ffn[0],
                               grp.retiled(FFN_TILE))
    x = _mm_residual([(act, p['w_ffn_down'], 0, 0)], x, mods[0], grp, 5, 512)

    h = _norm_mod(x, p['g_mix'][1], mods[1], grp, 1, 0)
    proj2 = _matmul(h, q['w_in_cd'], grp, CD_N // 4)
    proj = to_seq(proj2)
    o_c, new['ssd'], new['ssdc'] = _ssd(proj, p['ssd_conv_w'][0], p['ssd_conv_b'][0], q['ssd_alog'], q['ssd_dtb'],
                                        q['ssd_d_row'], p['ssd_norm_g'][0], s_ssd, s_ssdc, chunks[2], blk, valid,
                                        1 if prompt else SAMPLE_SEQS)
    if prompt:
        pow_re, pow_im = _s5_pow_table(q['s5_lb_re'], q['s5_lb_im'], seq_len // SUBLANE)
        yd, new['re'], new['im'] = _s5_seg(proj, q['s5_w_re'], q['s5_w_im'], q['s5_c_re'], q['s5_c_im'],
                                           pow_re, pow_im, p['s5_D'][0], s_re, s_im)
    else:
        yd, new['re'], new['im'] = _s5(proj2.reshape(1, grp.n_tok, CD_N), q['s5_w_re'], q['s5_w_im'],
                                       q['s5_c_re'], q['s5_c_im'], q['s5_lb_re'], q['s5_lb_im'], p['s5_D'][0],
                                       s_re, s_im, nb, valid)
    o_d = _s5_glu(yd.reshape(grp.n_tok, S5_W), p['s5_glu_w'][0], p['s5_glu_b'][0], grp)
    x = _mm_residual([(from_seq(o_c), p['w_out_cd'], 0, 0), (o_d, p['w_out_cd'], 0, 1)], x, mods[1], grp, 2, 1024)
    h = _norm_mod(x, p['g_ffn'][1], mods[1], grp, 4, 3)
    act, new['ffn1'] = _ffn_up(h, p['w_ffn_up'], p['ffn_conv_w'], p['ffn_conv_b'], 1, s_ffn[1],
                               grp.retiled(FFN_TILE))
    x = _mm_residual([(act, p['w_ffn_down'], 1, 0)], x, mods[1], grp, 5, 512)
    return _final_rms(x, p['g_final'], grp), new


def kernel(x_prompt, x_sample, c_prompt, c_sample, state_gla, state_gdn, state_gdn_conv, state_ssd, state_ssd_conv, state_s5_re, state_s5_im, state_ffn_conv, w_ada, b_ada, g_mix, g_ffn, w_in_ab, gla_w2, gla_b2, gla_norm_g, gdn_conv_w, gdn_conv_b, gdn_A_log, gdn_dt_bias, gdn_norm_g, w_out_ab, w_in_cd, ssd_conv_w, ssd_conv_b, ssd_A_log, ssd_dt_bias, ssd_D, ssd_norm_g, s5_A_re, s5_A_im, s5_B_re, s5_B_im, s5_C_re, s5_C_im, s5_D, s5_log_dt, s5_glu_w, s5_glu_b, w_out_cd, w_ffn_up, ffn_conv_w, ffn_conv_b, w_ffn_down, g_final):
    p = dict(g_mix=g_mix, g_ffn=g_ffn, w_in_ab=w_in_ab, gla_w2=gla_w2, gla_b2=gla_b2, gla_norm_g=gla_norm_g,
             gdn_conv_w=gdn_conv_w, gdn_conv_b=gdn_conv_b, gdn_A_log=gdn_A_log, gdn_dt_bias=gdn_dt_bias,
             gdn_norm_g=gdn_norm_g, w_out_ab=w_out_ab, w_in_cd=w_in_cd, ssd_conv_w=ssd_conv_w,
             ssd_conv_b=ssd_conv_b, ssd_A_log=ssd_A_log, ssd_dt_bias=ssd_dt_bias, ssd_D=ssd_D,
             ssd_norm_g=ssd_norm_g, s5_A_re=s5_A_re, s5_A_im=s5_A_im, s5_B_re=s5_B_re, s5_B_im=s5_B_im,
             s5_C_re=s5_C_re, s5_C_im=s5_C_im, s5_D=s5_D, s5_log_dt=s5_log_dt, s5_glu_w=s5_glu_w,
             s5_glu_b=s5_glu_b, w_out_cd=w_out_cd, w_ffn_up=w_ffn_up, ffn_conv_w=ffn_conv_w,
             ffn_conv_b=ffn_conv_b, w_ffn_down=w_ffn_down, g_final=g_final)
    bp, lp, d = x_prompt.shape
    bs, ls, _ = x_sample.shape
    q = _prep_params(p)

    bp_pad = -(-bp // SUBLANE) * SUBLANE
    c_all = jnp.concatenate([c_prompt, jnp.zeros((bp_pad - bp, d), F32), c_sample], axis=0)
    mod = _ada_mod(c_all, w_ada, b_ada)
    depth = w_ada.shape[0]
    mods_p = [mod[l, :bp].reshape(bp, 1, 6 * d) for l in range(depth)]
    mods_s = [jnp.tile(mod[l, bp_pad:], (ls, 1)).reshape(1, ls * bs, 6 * d) for l in range(depth)]

    tile_p = 512
    grp_p = _Group(bp * lp, tile_p, False, lp // tile_p, 1)
    zeros = lambda *shape: jnp.zeros(shape, F32)
    state_p = (zeros(bp, GLA_H, GLA_DK, GLA_DV), zeros(bp, GDN_H, GDN_DK, GDN_DV),
               zeros(bp, CONV_K - 1, GDN_CONV_W), zeros(bp, SSD_H, SSD_P, SSD_N),
               zeros(bp, CONV_K - 1, SSD_CONV_W), zeros(bp, 1, S5_STATE), zeros(bp, 1, S5_STATE),
               zeros(depth, bp, FFN_K - 1, 2 * D_FF))
    y_p, new_p = _trunk(x_prompt.reshape(bp * lp, d), mods_p, grp_p, (bp, lp, lp), state_p, p, q)

    grp_s = _Group(bs * ls, bs * ls, True, 1, bs)
    ffn_hist_s = jnp.swapaxes(state_ffn_conv, 1, 2).reshape(depth, 1, (FFN_K - 1) * bs, 2 * D_FF)
    state_s = (state_gla[0], state_gdn[0], state_gdn_conv[0], state_ssd[0], state_ssd_conv[0],
               state_s5_re.reshape(1, bs, S5_STATE), state_s5_im.reshape(1, bs, S5_STATE), ffn_hist_s)
    x_s = jnp.swapaxes(x_sample, 0, 1).reshape(ls * bs, d)
    y_s, new_s = _trunk(x_s, mods_s, grp_s, (bs, SAMPLE_PAD, ls), state_s, p, q)
    y_s = jnp.swapaxes(y_s.reshape(ls, bs, d), 0, 1)

    ffn_p = jnp.stack([new_p['ffn0'], new_p['ffn1']])
    ffn_s = jnp.stack([jnp.swapaxes(new_s[k].reshape(FFN_K - 1, bs, 2 * D_FF), 0, 1) for k in ('ffn0', 'ffn1')])
    s5_shape = lambda t, nb: t.reshape(1, nb, S5_G, S5_P)
    return (y_p.reshape(bp, lp, d), y_s,
            new_p['gla'][None], new_s['gla'][None], new_p['gdn'][None], new_s['gdn'][None],
            new_p['gdnc'][None], new_s['gdnc'][None], new_p['ssd'][None], new_s['ssd'][None],
            new_p['ssdc'][None], new_s['ssdc'][None],
            s5_shape(new_p['re'], bp), s5_shape(new_s['re'], bs), s5_shape(new_p['im'], bp), s5_shape(new_s['im'], bs),
            ffn_p, ffn_s)
```

```python
import functools
import math

import jax
import jax.numpy as jnp
from jax import lax
from jax.experimental import pallas as pl
from jax.experimental.pallas import tpu as pltpu

F32 = jnp.float32
BF16 = jnp.bfloat16
HIGHEST = lax.Precision.HIGHEST
EPS = 1e-6

D_MODEL = 2048
GLA_H, GLA_DK, GLA_DV, GLA_LR = 4, 128, 256, 16
GLA_GATE_NORM = 16.0
GLA_QK, GLA_V = GLA_H * GLA_DK, GLA_H * GLA_DV
GDN_H, GDN_DK, GDN_DV = 8, 128, 128
GDN_QK, GDN_V = GDN_H * GDN_DK, GDN_H * GDN_DV
CONV_K = 4
GDN_CONV_W = 2 * GDN_QK + GDN_V
SSD_P, SSD_H, SSD_G, SSD_N = 64, 16, 2, 128
SSD_W = SSD_H * SSD_P
SSD_CONV_W = SSD_W + 2 * SSD_G * SSD_N
S5_W, S5_GS, S5_G, S5_P = 1024, 16, 64, 64
S5_STATE = S5_G * S5_P
D_FF = 5632
FFN_K = 3

LANE = 128
SUBLANE = 8
VMEM_LIMIT = 48 * 1024 * 1024

AB_Q, AB_K, AB_V, AB_R, AB_QKV, AB_G, AB_SMALL, AB_N = 0, 512, 1024, 2048, 3072, 6144, 7168, 7296
AB_LR_LANE, AB_A_LANE, AB_B_LANE = 0, 16, 24
CD_XBC, CD_SMALL, CD_Z, CD_U, CD_N = 0, 1536, 2048, 3072, 4096

MIX_BLOCK = 256
GLA_CHUNK, GDN_CHUNK, SSD_CHUNK = 16, 128, 128
GLA_GROUP = 4
FFN_TILE = 1024
FFN_SUB = 256
INV_BLOCK = 16
SAMPLE_PAD = 8
SAMPLE_SEQS = 4


def _cparams(n_axes):
    return pltpu.CompilerParams(dimension_semantics=("arbitrary",) * n_axes, vmem_limit_bytes=VMEM_LIMIT)


def _sigmoid(x):
    return 1.0 / (1.0 + jnp.exp(-x))


def _silu(x):
    return x * _sigmoid(x)


def _softplus(x):
    return jnp.maximum(x, 0.0) + jnp.log(1.0 + jnp.exp(-jnp.abs(x)))


def _gelu_tanh(x):
    return 0.5 * x * (1.0 + jnp.tanh(math.sqrt(2.0 / math.pi) * (x + 0.044715 * (x * x * x))))


def _dot(a, b, precision=None):
    return jnp.dot(a, b, precision=precision, preferred_element_type=F32)


def _dot_nt(a, b, precision=None):
    return lax.dot_general(a, b, (((1,), (1,)), ((), ())), precision=precision, preferred_element_type=F32)


def _dot_tn(a, b, precision=None):
    return lax.dot_general(a, b, (((0,), (0,)), ((), ())), precision=precision, preferred_element_type=F32)


def _head_rms(o, g):
    return o * lax.rsqrt(jnp.mean(o * o, -1, keepdims=True) + EPS) * g


class _Group:
    def __init__(self, n_tok, tile, per_token_mod, tiles_per_seq, conv_shift):
        self.n_tok = n_tok
        self.tile = tile
        self.n_tiles = n_tok // tile
        self.per_token_mod = per_token_mod
        self.tiles_per_seq = tiles_per_seq
        self.conv_shift = conv_shift

    def retiled(self, tile):
        seq_rows = self.tile * self.tiles_per_seq
        if seq_rows % tile or self.n_tok % tile:
            return self
        return _Group(self.n_tok, tile, self.per_token_mod, seq_rows // tile, self.conv_shift)

    def mod_spec(self, width, col_block, m_axis):
        if self.per_token_mod:
            return pl.BlockSpec((1, self.tile, width), lambda *g: (0, g[m_axis], col_block(*g)))
        tps = self.tiles_per_seq
        return pl.BlockSpec((1, 1, width), lambda *g: (g[m_axis] // tps, 0, col_block(*g)))


def _ada_kernel(c_ref, w_ref, b_ref, o_ref):
    cs = _silu(c_ref[...]).astype(BF16)
    o_ref[0] = _dot(cs, w_ref[0].astype(BF16)) + b_ref[0]


def _ada_mod(c, w_ada, b_ada):
    depth, d, n = w_ada.shape
    rows = c.shape[0]
    tn = 1024
    return pl.pallas_call(
        _ada_kernel,
        grid=(depth, n // tn),
        in_specs=[pl.BlockSpec((rows, d), lambda l, j: (0, 0)),
                  pl.BlockSpec((1, d, tn), lambda l, j: (l, 0, j)),
                  pl.BlockSpec((1, 1, tn), lambda l, j: (l, 0, j))],
        out_specs=pl.BlockSpec((1, rows, tn), lambda l, j: (l, 0, j)),
        out_shape=jax.ShapeDtypeStruct((depth, rows, n), F32),
        compiler_params=_cparams(2), name="ada_mod",
    )(c, w_ada, b_ada.reshape(depth, 1, n))


def _norm_mod_kernel(x_ref, g_ref, sc_ref, sh_ref, o_ref):
    x = x_ref[...]
    y = x * lax.rsqrt(jnp.mean(x * x, -1, keepdims=True) + EPS) * g_ref[...]
    o_ref[...] = (y * (1.0 + sc_ref[0]) + sh_ref[0]).astype(BF16)


def _norm_mod(x, g, mod, grp, sc_blk, sh_blk):
    d = x.shape[1]
    return pl.pallas_call(
        _norm_mod_kernel,
        grid=(grp.n_tiles,),
        in_specs=[pl.BlockSpec((grp.tile, d), lambda i: (i, 0)),
                  pl.BlockSpec((1, d), lambda i: (0, 0)),
                  grp.mod_spec(d, lambda i: sc_blk, 0),
                  grp.mod_spec(d, lambda i: sh_blk, 0)],
        out_specs=pl.BlockSpec((grp.tile, d), lambda i: (i, 0)),
        out_shape=jax.ShapeDtypeStruct(x.shape, BF16),
        compiler_params=_cparams(1), name="norm_mod",
    )(x, g.reshape(1, d), mod, mod)


def _rms_kernel(x_ref, g_ref, o_ref):
    x = x_ref[...]
    o_ref[...] = x * lax.rsqrt(jnp.mean(x * x, -1, keepdims=True) + EPS) * g_ref[...]


def _final_rms(x, g, grp):
    d = x.shape[1]
    return pl.pallas_call(
        _rms_kernel,
        grid=(grp.n_tiles,),
        in_specs=[pl.BlockSpec((grp.tile, d), lambda i: (i, 0)), pl.BlockSpec((1, d), lambda i: (0, 0))],
        out_specs=pl.BlockSpec((grp.tile, d), lambda i: (i, 0)),
        out_shape=jax.ShapeDtypeStruct(x.shape, F32),
        compiler_params=_cparams(1), name="final_rms",
    )(x, g.reshape(1, d))


def _matmul_kernel(a_ref, w_ref, o_ref):
    o_ref[...] = _dot(a_ref[...], w_ref[...])


def _matmul(a, w, grp, tn):
    k, n = w.shape
    return pl.pallas_call(
        _matmul_kernel,
        grid=(n // tn, grp.n_tiles),
        in_specs=[pl.BlockSpec((grp.tile, k), lambda j, i: (i, 0)),
                  pl.BlockSpec((k, tn), lambda j, i: (0, j))],
        out_specs=pl.BlockSpec((grp.tile, tn), lambda j, i: (i, j)),
        out_shape=jax.ShapeDtypeStruct((a.shape[0], n), F32),
        compiler_params=_cparams(2), name="matmul",
    )(a, w)


def _mm_res_kernel(*refs, n_pairs):
    x_ref, gate_ref, o_ref = refs[2 * n_pairs:]
    y = _dot(refs[0][...].astype(BF16), refs[1][0])
    for p in range(1, n_pairs):
        y = y + _dot(refs[2 * p][...].astype(BF16), refs[2 * p + 1][0])
    o_ref[...] = x_ref[...] + gate_ref[0] * y


def _mm_residual(pairs, x, mod, grp, gate_blk, tn):
    n = x.shape[1]
    in_specs, args = [], []
    for a, w, layer, row_blk in pairs:
        k = a.shape[1]
        in_specs += [pl.BlockSpec((grp.tile, k), lambda j, i: (i, 0)),
                     pl.BlockSpec((1, k, tn), lambda j, i, layer=layer, row_blk=row_blk: (layer, row_blk, j))]
        args += [a, w]
    in_specs += [pl.BlockSpec((grp.tile, tn), lambda j, i: (i, j)),
                 grp.mod_spec(tn, lambda j, i: gate_blk * (n // tn) + j, 1)]
    return pl.pallas_call(
        functools.partial(_mm_res_kernel, n_pairs=len(pairs)),
        grid=(n // tn, grp.n_tiles),
        in_specs=in_specs,
        out_specs=pl.BlockSpec((grp.tile, tn), lambda j, i: (i, j)),
        out_shape=jax.ShapeDtypeStruct(x.shape, F32),
        compiler_params=_cparams(2), name="mm_residual",
    )(*args, x, mod)


def _ffn_up_kernel(h_ref, wa_ref, wg_ref, cwa_ref, cwg_ref, cba_ref, cbg_ref, ha_ref, hg_ref,
                   act_ref, sta_ref, stg_ref, scr_a, scr_g, *, shift, tile, sub, tiles_per_seq):
    i = pl.program_id(1)
    hist = (FFN_K - 1) * shift
    base = -(-hist // SUBLANE) * SUBLANE

    @pl.when(i % tiles_per_seq == 0)
    def _():
        scr_a[base - hist:base, :] = ha_ref[0]
        scr_g[base - hist:base, :] = hg_ref[0]

    def conv(scr, cw_ref, cb_ref, r0):
        y = cb_ref[0]
        for j in range(FFN_K):
            lo = base + r0 - (FFN_K - 1 - j) * shift
            y = y + scr[lo:lo + sub, :] * cw_ref[0, j:j + 1, :]
        return y

    def project(r0):
        h = h_ref[r0:r0 + sub, :]
        scr_a[base + r0:base + r0 + sub, :] = _dot(h, wa_ref[0])
        scr_g[base + r0:base + r0 + sub, :] = _dot(h, wg_ref[0])

    project(0)
    for r0 in range(0, tile, sub):
        if r0 + sub < tile:
            project(r0 + sub)
        a = conv(scr_a, cwa_ref, cba_ref, r0)
        g = conv(scr_g, cwg_ref, cbg_ref, r0)
        act_ref[r0:r0 + sub, :] = (_silu(g) * a).astype(BF16)
    last_a = scr_a[base + tile - hist:base + tile, :]
    last_g = scr_g[base + tile - hist:base + tile, :]
    sta_ref[0] = last_a
    stg_ref[0] = last_g
    scr_a[base - hist:base, :] = last_a
    scr_g[base - hist:base, :] = last_g


def _ffn_up(h, w_up, conv_w, conv_b, layer, hist0, grp):
    d = h.shape[1]
    tn = 512
    nj = D_FF // tn
    shift = grp.conv_shift
    hist = (FFN_K - 1) * shift
    base = -(-hist // SUBLANE) * SUBLANE
    n_seq = grp.n_tiles // grp.tiles_per_seq
    tps = grp.tiles_per_seq
    cb = conv_b.reshape(conv_b.shape[0], 1, 2 * D_FF)
    kern = functools.partial(_ffn_up_kernel, shift=shift, tile=grp.tile, sub=min(FFN_SUB, grp.tile),
                             tiles_per_seq=tps)
    assert grp.tile % min(FFN_SUB, grp.tile) == 0
    act, st_a, st_g = pl.pallas_call(
        kern,
        grid=(nj, grp.n_tiles),
        in_specs=[pl.BlockSpec((grp.tile, d), lambda j, i: (i, 0)),
                  pl.BlockSpec((1, d, tn), lambda j, i: (layer, 0, j)),
                  pl.BlockSpec((1, d, tn), lambda j, i: (layer, 0, nj + j)),
                  pl.BlockSpec((1, FFN_K, tn), lambda j, i: (layer, 0, j)),
                  pl.BlockSpec((1, FFN_K, tn), lambda j, i: (layer, 0, nj + j)),
                  pl.BlockSpec((1, 1, tn), lambda j, i: (layer, 0, j)),
                  pl.BlockSpec((1, 1, tn), lambda j, i: (layer, 0, nj + j)),
                  pl.BlockSpec((1, hist, tn), lambda j, i: (i // tps, 0, j)),
                  pl.BlockSpec((1, hist, tn), lambda j, i: (i // tps, 0, nj + j))],
        out_specs=[pl.BlockSpec((grp.tile, tn), lambda j, i: (i, j)),
                   pl.BlockSpec((1, hist, tn), lambda j, i: (i // tps, 0, j)),
                   pl.BlockSpec((1, hist, tn), lambda j, i: (i // tps, 0, j))],
        out_shape=[jax.ShapeDtypeStruct((h.shape[0], D_FF), BF16),
                   jax.ShapeDtypeStruct((n_seq, hist, D_FF), F32),
                   jax.ShapeDtypeStruct((n_seq, hist, D_FF), F32)],
        scratch_shapes=[pltpu.VMEM((base + grp.tile, tn), F32), pltpu.VMEM((base + grp.tile, tn), F32)],
        compiler_params=_cparams(2), name="ffn_up",
    )(h, w_up, w_up, conv_w, conv_w, cb, cb, hist0, hist0)
    return act, jnp.concatenate([st_a, st_g], axis=-1)


def _glu_kernel(y_ref, w_ref, b_ref, o_ref):
    z5 = _gelu_tanh(y_ref[...])
    o_ref[...] = (z5 * _sigmoid(_dot(z5.astype(BF16), w_ref[...]) + b_ref[...])).astype(o_ref.dtype)


def _s5_glu(yd, w, b, grp):
    n = yd.shape[1]
    return pl.pallas_call(
        _glu_kernel,
        grid=(grp.n_tiles,),
        in_specs=[pl.BlockSpec((grp.tile, n), lambda i: (i, 0)),
                  pl.BlockSpec((n, n), lambda i: (0, 0)),
                  pl.BlockSpec((1, n), lambda i: (0, 0))],
        out_specs=pl.BlockSpec((grp.tile, n), lambda i: (i, 0)),
        out_shape=jax.ShapeDtypeStruct(yd.shape, BF16),
        compiler_params=_cparams(1), name="s5_glu",
    )(yd, w, b.reshape(1, n))


def _mixer_out_dtype(block):
    return BF16 if block % (2 * SUBLANE) == 0 else F32


def _causal_conv_chunk(x, cv_scr, cw_ref, cb_ref, chunk, valid):
    base = SUBLANE
    cv_scr[base:base + chunk, :] = x
    y = cb_ref[...]
    for j in range(CONV_K):
        lo = base - (CONV_K - 1) + j
        y = y + cv_scr[lo:lo + chunk, :] * cw_ref[j:j + 1, :]
    last = cv_scr[base + valid - (CONV_K - 1):base + valid, :]
    cv_scr[base - (CONV_K - 1):base, :] = last
    return y


def _gla_kernel(q_ref, k_ref, v_ref, r_ref, sm_ref, w2_ref, b2_ref, ng_ref, s0_ref,
                o_ref, sout_ref, s_scr, b_scr, *, chunk, group, block, valid, nseq):
    blk = pl.program_id(1)

    @pl.when(blk == 0)
    def _():
        s_scr[...] = s0_ref[...]

    span = chunk * group
    lane = lax.broadcasted_iota(jnp.int32, (1, GLA_DK), 1)
    ones_kk = jnp.ones((GLA_DK, GLA_DK), BF16)
    row_in_chunk = lax.broadcasted_iota(jnp.int32, (span, 1), 0) % chunk
    eye = (lax.broadcasted_iota(jnp.int32, (GLA_DK, GLA_DK), 0)
           == lax.broadcasted_iota(jnp.int32, (GLA_DK, GLA_DK), 1))
    n_valid = min(valid, chunk)
    heads = range(GLA_H)
    chunks = range(group)

    def to3(t):
        return t.reshape(group, chunk, t.shape[-1])

    ri = lax.broadcasted_iota(jnp.int32, (block, block), 0)
    ci = lax.broadcasted_iota(jnp.int32, (block, block), 1)
    tri = ((ri // chunk == ci // chunk) & (ri >= ci)).astype(BF16)
    row_blk = lax.broadcasted_iota(jnp.int32, (block, 1), 0) % chunk
    for sq in range(nseq):
        x = _dot(sm_ref[sq].astype(BF16), w2_ref[...].astype(BF16)) + b2_ref[...]
        log_a = (jnp.minimum(x, 0.0) - jnp.log(1.0 + jnp.exp(-jnp.abs(x)))) * (1.0 / GLA_GATE_NORM)
        if valid < chunk:
            log_a = jnp.where(row_blk < valid, log_a, 0.0)
        b_scr[sq] = sum(_dot(tri, part) for part in _split3(log_a))

    def one_seq(sq, rows):
        b = b_scr[sq, rows, :]
        q = q_ref[sq, rows, :] * GLA_DK ** -0.5
        k = k_ref[sq, rows, :]
        v = v_ref[sq, rows, :]
        r = r_ref[sq, rows, :]
        b3, q_in, kv, d_col = [], [], [], []
        for h in heads:
            ks = slice(h * GLA_DK, (h + 1) * GLA_DK)
            bh3 = to3(b[:, ks])
            b_last = bh3[:, chunk - 1:chunk, :]
            k_out = (to3(k[:, ks]) * jnp.exp(b_last - bh3)).reshape(span, GLA_DK)
            vh = v[:, h * GLA_DV:(h + 1) * GLA_DV]
            b3.append(bh3)
            q_in.append(q[:, ks] * jnp.exp(b[:, ks]))
            kv.append([_dot_tn(k_out[c * chunk:(c + 1) * chunk], vh[c * chunk:(c + 1) * chunk]) for c in chunks])
            d_col.append([jnp.sum(jnp.where(eye, jnp.exp(b_last[c]), 0.0), axis=-1, keepdims=True) for c in chunks])
        st = [s_scr[sq, h] for h in heads]
        o_inter = [[] for _ in heads]
        for c in chunks:
            for h in heads:
                o_inter[h].append(_dot(q_in[h][c * chunk:(c + 1) * chunk], st[h]))
                st[h] = st[h] * d_col[h][c] + kv[h][c]
        for h in heads:
            ks = slice(h * GLA_DK, (h + 1) * GLA_DK)
            vs = slice(h * GLA_DV, (h + 1) * GLA_DV)
            s_scr[sq, h] = st[h]
            bh3, qh3, kh3, vh3 = b3[h], to3(q[:, ks]), to3(k[:, ks]), to3(v[:, vs])
            prods = []
            for j in range(n_valid):
                e = jnp.exp(bh3 - bh3[:, j:j + 1, :])
                prods.append((qh3 * e * kh3[:, j:j + 1, :]).reshape(span, GLA_DK).astype(BF16))
            sums = _dot(jnp.concatenate(prods, axis=0), ones_kk)
            att = jnp.zeros((span, GLA_DK), F32)
            for j in range(n_valid):
                att = jnp.where(lane == j, sums[j * span:(j + 1) * span], att)
            att = jnp.where(row_in_chunk >= lane, att, 0.0)[:, :chunk]
            vh = v[:, vs]
            o_intra = [_dot(att[c * chunk:(c + 1) * chunk], vh[c * chunk:(c + 1) * chunk]) for c in chunks]
            o = jnp.concatenate(o_intra, axis=0) + jnp.concatenate(o_inter[h], axis=0)
            o_ref[sq, rows, vs] = (_head_rms(o, ng_ref[...]) * _silu(r[:, vs])).astype(o_ref.dtype)

    def do_span(s, carry):
        rows = pl.ds(pl.multiple_of(s * span, span), span)
        for sq in range(nseq):
            one_seq(sq, rows)
        return carry

    lax.fori_loop(0, block // span, do_span, 0)

    @pl.when(blk == pl.num_programs(1) - 1)
    def _():
        sout_ref[...] = s_scr[...]


def _gla(proj, w2p, b2, ng, s0, chunk, group, block, valid, nseq):
    nb, seq, _ = proj.shape

    def col(width, off):
        return pl.BlockSpec((nseq, block, width), lambda b, i: (b, i, off // width))

    st_spec = pl.BlockSpec((nseq, GLA_H, GLA_DK, GLA_DV), lambda b, i: (b, 0, 0, 0))
    kern = functools.partial(_gla_kernel, chunk=chunk, group=group, block=block, valid=valid, nseq=nseq)
    return pl.pallas_call(
        kern,
        grid=(nb // nseq, seq // block),
        in_specs=[col(GLA_QK, AB_Q), col(GLA_QK, AB_K), col(GLA_V, AB_V), col(GLA_V, AB_R), col(LANE, AB_SMALL),
                  pl.BlockSpec((LANE, GLA_QK), lambda b, i: (0, 0)),
                  pl.BlockSpec((1, GLA_QK), lambda b, i: (0, 0)),
                  pl.BlockSpec((1, GLA_DV), lambda b, i: (0, 0)),
                  st_spec],
        out_specs=[pl.BlockSpec((nseq, block, GLA_V), lambda b, i: (b, i, 0)), st_spec],
        out_shape=[jax.ShapeDtypeStruct((nb, seq, GLA_V), _mixer_out_dtype(block)),
                   jax.ShapeDtypeStruct((nb, GLA_H, GLA_DK, GLA_DV), F32)],
        scratch_shapes=[pltpu.VMEM((nseq, GLA_H, GLA_DK, GLA_DV), F32), pltpu.VMEM((nseq, block, GLA_QK), F32)],
        compiler_params=_cparams(2), name="gla",
    )(proj, proj, proj, proj, proj, w2p, b2.reshape(1, GLA_QK), ng.reshape(1, GLA_DV), s0)


def _split2(a):
    hi = a.astype(BF16)
    return hi, (a - hi.astype(F32)).astype(BF16)


def _split3(a):
    hi = a.astype(BF16)
    rest = a - hi.astype(F32)
    mid = rest.astype(BF16)
    return hi, mid, (rest - mid.astype(F32)).astype(BF16)


def _dot3(a, b):
    return _dot(a[0], b[0]) + _dot(a[0], b[1]) + _dot(a[1], b[0])


def _inv_unit_lower_many(mats, n, eye):
    ps = [eye - a for a in mats]
    if n <= 2:
        return ps
    pows = [_split2(a) for a in mats]
    k = 2
    pending = None
    while k < n:
        sq = [_dot3(a, a) for a in pows]
        if pending is not None:
            ps = [p + _dot3(_split2(p), f) for p, f in zip(ps, pending)]
        pows = [_split2(a) for a in sq]
        pending = pows
        k *= 2
    return [p + _dot3(_split2(p), f) for p, f in zip(ps, pending)]


def _gdn_kernel(qkv_ref, sm_ref, gb_ref, cw_ref, cb_ref, alog_ref, dtb_ref, ng_ref, s0_ref, c0_ref,
                o_ref, sout_ref, cout_ref, s_scr, cv_scr, *, chunk, block, valid, nseq):
    blk = pl.program_id(1)

    @pl.when(blk == 0)
    def _():
        s_scr[...] = s0_ref[...]
        cv_scr[:, SUBLANE - (CONV_K - 1):SUBLANE, :] = c0_ref[...]

    ri = lax.broadcasted_iota(jnp.int32, (chunk, chunk), 0)
    ci = lax.broadcasted_iota(jnp.int32, (chunk, chunk), 1)
    causal = ri >= ci
    strict = ri > ci
    eye = (ri == ci).astype(F32)
    tri = causal.astype(F32)
    tri_u = (ri <= ci).astype(F32)
    inv_blk = min(INV_BLOCK, chunk)
    same_blk = (ri // inv_blk) == (ci // inv_blk)
    row = lax.broadcasted_iota(jnp.int32, (chunk, 1), 0)
    n_valid = min(valid, chunk)

    units = [(sq, h) for sq in range(nseq) for h in range(GDN_H)]
    heads = range(len(units))

    def prelude(rows):
        q, k, kb, rhs, dec, gcc = [], [], [], [], [], []
        for sq, h in units:
            if h == 0:
                act = _silu(_causal_conv_chunk(qkv_ref[sq, rows, :], cv_scr.at[sq], cw_ref, cb_ref, chunk, n_valid))
                sm = sm_ref[sq, rows, :]
                g_all = -jnp.exp(alog_ref[...]) * _softplus(sm + dtb_ref[...])
                beta_all = _sigmoid(sm)
                if valid < chunk:
                    g_all = jnp.where(row < valid, g_all, 0.0)
                    beta_all = jnp.where(row < valid, beta_all, 0.0)
                gc = _dot(tri, g_all, HIGHEST)
                gc_r = _dot_tn(g_all, tri_u, HIGHEST)
            qh = act[:, h * GDN_DK:(h + 1) * GDN_DK]
            kh = act[:, GDN_QK + h * GDN_DK:GDN_QK + (h + 1) * GDN_DK]
            vh = act[:, 2 * GDN_QK + h * GDN_DV:2 * GDN_QK + (h + 1) * GDN_DV]
            qh = qh * lax.rsqrt(jnp.sum(qh * qh, -1, keepdims=True) + EPS) * GDN_DK ** -0.5
            kh = kh * lax.rsqrt(jnp.sum(kh * kh, -1, keepdims=True) + EPS)
            beta = beta_all[:, AB_B_LANE + h:AB_B_LANE + h + 1]
            gch = gc[:, AB_A_LANE + h:AB_A_LANE + h + 1]
            gcr = gc_r[AB_A_LANE + h:AB_A_LANE + h + 1, :]
            q.append(qh)
            k.append(kh)
            kb.append(kh * beta)
            rhs.append(_split2(jnp.concatenate([vh * beta, kb[-1] * jnp.exp(gch)], axis=1)))
            dec.append(jnp.exp(jnp.where(causal, gch - gcr, -jnp.inf)))
            gcc.append(gch)
        kbf = [a.astype(BF16) for a in k]
        m = [jnp.where(strict, _dot_nt(kb[h].astype(BF16), kbf[h]) * dec[h], 0.0) for h in heads]
        att = [_dot_nt(q[h].astype(BF16), kbf[h]) * dec[h] for h in heads]
        return q, k, rhs, gcc, m, att

    def finish(pre, rows):
        q, k, rhs, gcc, m, att = pre
        m_diag = [jnp.where(same_blk, a, 0.0) for a in m]
        t = [_split2(a) for a in _inv_unit_lower_many(m_diag, inv_blk, eye)]
        y = [_dot3(t[h], rhs[h]) for h in heads]
        if chunk > inv_blk:
            n_off = [_dot3(t[h], _split2(m[h] - m_diag[h])) for h in heads]
            qn = [_split2(a) for a in _inv_unit_lower_many(n_off, chunk // inv_blk, eye)]
            y = [_dot3(qn[h], _split2(y[h])) for h in heads]
        st = [s_scr[sq, h] for sq, h in units]
        stb = [a.astype(BF16) for a in st]
        v_new = [y[h][:, :GDN_DV] - _dot(y[h][:, GDN_DV:].astype(BF16), stb[h]) for h in heads]
        vnb = [a.astype(BF16) for a in v_new]
        o = [_dot((q[h] * jnp.exp(gcc[h])).astype(BF16), stb[h]) + _dot(att[h].astype(BF16), vnb[h]) for h in heads]
        for u, (sq, h) in enumerate(units):
            hs = slice(h * GDN_DV, (h + 1) * GDN_DV)
            g_last = gcc[u][chunk - 1:chunk, :]
            k_out = (k[u] * jnp.exp(g_last - gcc[u])).astype(BF16)
            s_scr[sq, h] = st[u] * jnp.exp(g_last) + _dot_tn(k_out, vnb[u])
            o_ref[sq, rows, hs] = (_head_rms(o[u], ng_ref[...]) * _silu(gb_ref[sq, rows, hs])).astype(o_ref.dtype)

    n_chunks = block // chunk
    pre = prelude(pl.ds(0, chunk))
    for c in range(n_chunks):
        nxt = prelude(pl.ds((c + 1) * chunk, chunk)) if c + 1 < n_chunks else None
        finish(pre, pl.ds(c * chunk, chunk))
        pre = nxt

    @pl.when(blk == pl.num_programs(1) - 1)
    def _():
        sout_ref[...] = s_scr[...]
        cout_ref[...] = cv_scr[:, SUBLANE - (CONV_K - 1):SUBLANE, :]


def _gdn(proj, conv_w, conv_b, alog_row, dtb_row, ng, s0, c0, chunk, block, valid, nseq):
    nb, seq, _ = proj.shape

    def col(width, off):
        return pl.BlockSpec((nseq, block, width), lambda b, i: (b, i, off // width))

    def const(shape):
        return pl.BlockSpec(shape, lambda b, i: (0,) * len(shape))

    st_spec = pl.BlockSpec((nseq, GDN_H, GDN_DK, GDN_DV), lambda b, i: (b, 0, 0, 0))
    cv_spec = pl.BlockSpec((nseq, CONV_K - 1, GDN_CONV_W), lambda b, i: (b, 0, 0))
    kern = functools.partial(_gdn_kernel, chunk=chunk, block=block, valid=valid, nseq=nseq)
    return pl.pallas_call(
        kern,
        grid=(nb // nseq, seq // block),
        in_specs=[col(GDN_CONV_W, AB_QKV), col(LANE, AB_SMALL), col(GDN_V, AB_G),
                  const((CONV_K, GDN_CONV_W)), const((1, GDN_CONV_W)), const((1, LANE)), const((1, LANE)),
                  const((1, GDN_DV)), st_spec, cv_spec],
        out_specs=[pl.BlockSpec((nseq, block, GDN_V), lambda b, i: (b, i, 0)), st_spec, cv_spec],
        out_shape=[jax.ShapeDtypeStruct((nb, seq, GDN_V), _mixer_out_dtype(block)),
                   jax.ShapeDtypeStruct((nb, GDN_H, GDN_DK, GDN_DV), F32),
                   jax.ShapeDtypeStruct((nb, CONV_K - 1, GDN_CONV_W), F32)],
        scratch_shapes=[pltpu.VMEM((nseq, GDN_H, GDN_DK, GDN_DV), F32),
                        pltpu.VMEM((nseq, SUBLANE + chunk, GDN_CONV_W), F32)],
        compiler_params=_cparams(2), name="gdn",
    )(proj, proj, proj, conv_w, conv_b.reshape(1, GDN_CONV_W), alog_row, dtb_row, ng.reshape(1, GDN_DV), s0, c0)


def _ssd_kernel(xbc_ref, sm_ref, z_ref, cw_ref, cb_ref, alog_ref, dtb_ref, drow_ref, ng_ref, s0_ref, c0_ref,
                o_ref, sout_ref, cout_ref, s_scr, cv_scr, dt_scr, acs_scr, acsr_scr, dtr_scr,
                *, chunk, block, valid, nseq):
    blk = pl.program_id(1)

    @pl.when(blk == 0)
    def _():
        s_scr[...] = s0_ref[...]
        cv_scr[:, SUBLANE - (CONV_K - 1):SUBLANE, :] = c0_ref[...]

    ri = lax.broadcasted_iota(jnp.int32, (chunk, chunk), 0)
    ci = lax.broadcasted_iota(jnp.int32, (chunk, chunk), 1)
    causal = ri >= ci
    eye = (ri == ci).astype(BF16)
    tri = causal.astype(BF16)
    tri_u = (ri <= ci).astype(BF16)
    row = lax.broadcasted_iota(jnp.int32, (chunk, 1), 0)
    lane_lo = lax.broadcasted_iota(jnp.int32, (chunk, LANE), 1) < SSD_P
    row_lo = lax.broadcasted_iota(jnp.int32, (2 * SSD_P, 1), 0) < SSD_P
    n_valid = min(valid, chunk)
    heads_per_group = SSD_H // SSD_G
    gsz = SSD_W // SSD_G

    for sq in range(nseq):
        for c in range(block // chunk):
            dt = _softplus(sm_ref[sq, c * chunk:(c + 1) * chunk, :] + dtb_ref[...])
            if valid < chunk:
                dt = jnp.where(row < valid, dt, 0.0)
            dta = _split3(dt * (-jnp.exp(alog_ref[...])))
            dt_scr[sq, c] = dt
            acs_scr[sq, c] = sum(_dot(tri, part) for part in dta)
            acsr_scr[sq, c] = sum(_dot_tn(part, tri_u) for part in dta)
            dtr_scr[sq, c] = sum(_dot_tn(part, eye) for part in _split3(dt))

    def one_seq(sq, s, rows):
        act = _silu(_causal_conv_chunk(xbc_ref[sq, rows, :], cv_scr.at[sq], cw_ref, cb_ref, chunk, n_valid))
        dt, acs, acs_r, dt_r = dt_scr[sq, s], acs_scr[sq, s], acsr_scr[sq, s], dtr_scr[sq, s]
        z = z_ref[sq, rows, :]
        for g in range(SSD_G):
            bg = act[:, SSD_W + g * SSD_N:SSD_W + (g + 1) * SSD_N]
            cg = act[:, SSD_W + SSD_G * SSD_N + g * SSD_N:SSD_W + SSD_G * SSD_N + (g + 1) * SSD_N]
            cb = _dot_nt(cg, bg)
            parts = []
            for pr in range(heads_per_group // 2):
                pi = g * (heads_per_group // 2) + pr
                xp = act[:, pi * LANE:(pi + 1) * LANE]
                st = s_scr[sq, pi]
                y_in, e_in, w_out, d_last = [], [], [], []
                for hh in range(2):
                    h = 2 * pi + hh
                    ac = acs[:, h:h + 1]
                    dec = jnp.exp(jnp.where(causal, ac - acs_r[h:h + 1, :], -jnp.inf))
                    y_in.append(_dot(cb * dec * dt_r[h:h + 1, :], xp))
                    a_last = ac[chunk - 1:chunk, :]
                    e_in.append(jnp.exp(ac))
                    w_out.append(jnp.exp(a_last - ac) * dt[:, h:h + 1])
                    d_last.append(jnp.exp(a_last))
                y = jnp.where(lane_lo, y_in[0], y_in[1])
                y = y + _dot_nt(cg, st) * jnp.where(lane_lo, e_in[0], e_in[1])
                y = y + drow_ref[:, pi * LANE:(pi + 1) * LANE] * xp
                x_sc = xp * jnp.where(lane_lo, w_out[0], w_out[1])
                s_scr[sq, pi] = st * jnp.where(row_lo, d_last[0], d_last[1]) + _dot_tn(x_sc, bg)
                parts.append(y)
            gs = slice(g * gsz, (g + 1) * gsz)
            yg = jnp.concatenate(parts, axis=1) * _silu(z[:, gs])
            o_ref[sq, rows, gs] = _head_rms(yg, ng_ref[:, gs]).astype(o_ref.dtype)

    def do_chunk(s, carry):
        rows = pl.ds(pl.multiple_of(s * chunk, chunk), chunk)
        for sq in range(nseq):
            one_seq(sq, s, rows)
        return carry

    lax.fori_loop(0, block // chunk, do_chunk, 0)

    @pl.when(blk == pl.num_programs(1) - 1)
    def _():
        sout_ref[...] = s_scr[...]
        cout_ref[...] = cv_scr[:, SUBLANE - (CONV_K - 1):SUBLANE, :]


def _ssd(proj, conv_w, conv_b, alog_row, dtb_row, d_row, ng, s0, c0, chunk, block, valid, nseq):
    nb, seq, _ = proj.shape
    n_pairs = SSD_H // 2

    def col(width, off):
        return pl.BlockSpec((nseq, block, width), lambda b, i: (b, i, off // width))

    def const(shape):
        return pl.BlockSpec(shape, lambda b, i: (0,) * len(shape))

    st_spec = pl.BlockSpec((nseq, n_pairs, 2 * SSD_P, SSD_N), lambda b, i: (b, 0, 0, 0))
    cv_spec = pl.BlockSpec((nseq, CONV_K - 1, SSD_CONV_W), lambda b, i: (b, 0, 0))
    kern = functools.partial(_ssd_kernel, chunk=chunk, block=block, valid=valid, nseq=nseq)
    o, s_new, c_new = pl.pallas_call(
        kern,
        grid=(nb // nseq, seq // block),
        in_specs=[col(SSD_CONV_W, CD_XBC), col(LANE, CD_SMALL), col(SSD_W, CD_Z),
                  const((CONV_K, SSD_CONV_W)), const((1, SSD_CONV_W)), const((1, LANE)), const((1, LANE)),
                  const((1, SSD_W)), const((1, SSD_W)), st_spec, cv_spec],
        out_specs=[pl.BlockSpec((nseq, block, SSD_W), lambda b, i: (b, i, 0)), st_spec, cv_spec],
        out_shape=[jax.ShapeDtypeStruct((nb, seq, SSD_W), _mixer_out_dtype(block)),
                   jax.ShapeDtypeStruct((nb, n_pairs, 2 * SSD_P, SSD_N), F32),
                   jax.ShapeDtypeStruct((nb, CONV_K - 1, SSD_CONV_W), F32)],
        scratch_shapes=[pltpu.VMEM((nseq, n_pairs, 2 * SSD_P, SSD_N), F32),
                        pltpu.VMEM((nseq, SUBLANE + chunk, SSD_CONV_W), F32),
                        pltpu.VMEM((nseq, block // chunk, chunk, LANE), F32),
                        pltpu.VMEM((nseq, block // chunk, chunk, LANE), F32),
                        pltpu.VMEM((nseq, block // chunk, LANE, chunk), F32),
                        pltpu.VMEM((nseq, block // chunk, LANE, chunk), F32)],
        compiler_params=_cparams(2), name="ssd",
    )(proj, proj, proj, conv_w, conv_b.reshape(1, SSD_CONV_W), alog_row, dtb_row, d_row,
      ng.reshape(1, SSD_W), s0.reshape(nb, n_pairs, 2 * SSD_P, SSD_N), c0)
    return o, s_new.reshape(nb, SSD_H, SSD_P, SSD_N), c_new


def _s5_prep_kernel(are_ref, aim_ref, ldt_ref, bre_ref, bim_ref, lbr_ref, lbi_ref, bbr_ref, bbi_ref):
    a_re, a_im = are_ref[...], aim_ref[...]
    dt = jnp.exp(ldt_ref[...])
    mag = jnp.exp(a_re * dt)
    lb_re, lb_im = mag * jnp.cos(a_im * dt), mag * jnp.sin(a_im * dt)
    nr, ni = lb_re - 1.0, lb_im
    den = a_re * a_re + a_im * a_im
    f_re = (nr * a_re + ni * a_im) / den
    f_im = (ni * a_re - nr * a_im) / den
    b_re, b_im = bre_ref[...], bim_ref[...]
    lbr_ref[...] = lb_re
    lbi_ref[...] = lb_im
    bbr_ref[...] = f_re * b_re - f_im * b_im
    bbi_ref[...] = f_re * b_im + f_im * b_re


def _s5_prep(a_re, a_im, log_dt, b_re, b_im):
    g3 = (S5_G, 1, S5_P)
    b3 = (S5_G, S5_GS, S5_P)
    return pl.pallas_call(
        _s5_prep_kernel,
        out_shape=[jax.ShapeDtypeStruct(g3, F32), jax.ShapeDtypeStruct(g3, F32),
                   jax.ShapeDtypeStruct(b3, F32), jax.ShapeDtypeStruct(b3, F32)],
        name="s5_prep",
    )(a_re.reshape(g3), a_im.reshape(g3), log_dt.reshape(S5_G, 1, 1),
      jnp.swapaxes(b_re, 1, 2), jnp.swapaxes(b_im, 1, 2))


def _block_diag(blocks):
    g, r, c = blocks.shape
    per = 8
    b = blocks.reshape(g // per, per, r, 1, c) * jnp.eye(per, dtype=blocks.dtype).reshape(1, per, 1, per, 1)
    return b.reshape(g // per, per * r, per * c)


def _s5_kernel(u_ref, wre_ref, wim_ref, cre_ref, cim_ref, lbr_ref, lbi_ref, d_ref, x0r_ref, x0i_ref,
               y_ref, xfr_ref, xfi_ref, xr_scr, xi_scr, sr_scr, si_scr, *, rows_per_step, steps):
    c = pl.program_id(2)

    @pl.when(c == 0)
    def _():
        xr_scr[...] = x0r_ref[0]
        xi_scr[...] = x0i_ref[0]

    u = u_ref[0]
    sr_scr[...] = _dot(u, wre_ref[0])
    si_scr[...] = _dot(u, wim_ref[0])
    l_re, l_im = lbr_ref[...], lbi_ref[...]

    def step(t, carry):
        rows = pl.ds(pl.multiple_of(t * rows_per_step, rows_per_step), rows_per_step)
        xr, xi = xr_scr[...], xi_scr[...]
        nr = l_re * xr - l_im * xi + sr_scr[rows, :]
        ni = l_re * xi + l_im * xr + si_scr[rows, :]
        xr_scr[...] = nr
        xi_scr[...] = ni
        sr_scr[rows, :] = nr
        si_scr[rows, :] = ni
        return carry

    lax.fori_loop(0, steps, step, 0)
    y_ref[0] = _dot(sr_scr[...], cre_ref[0]) - _dot(si_scr[...], cim_ref[0]) + d_ref[...] * u

    @pl.when(c == pl.num_programs(2) - 1)
    def _():
        xfr_ref[0] = xr_scr[...]
        xfi_ref[0] = xi_scr[...]


def _s5(proj, w_re, w_im, c_re, c_im, lb_re, lb_im, d, x0_re, x0_im, rows_per_step, steps):
    ng, n_tok, _ = proj.shape
    nj = S5_W // LANE
    sw = S5_STATE // nj
    cr = rows_per_step * steps
    kern = functools.partial(_s5_kernel, rows_per_step=rows_per_step, steps=steps)
    x_spec = pl.BlockSpec((1, rows_per_step, sw), lambda g, j, c: (g, 0, j))
    return pl.pallas_call(
        kern,
        grid=(ng, nj, n_tok // cr),
        in_specs=[pl.BlockSpec((1, cr, LANE), lambda g, j, c: (g, c, CD_U // LANE + j)),
                  pl.BlockSpec((1, LANE, sw), lambda g, j, c: (j, 0, 0)),
                  pl.BlockSpec((1, LANE, sw), lambda g, j, c: (j, 0, 0)),
                  pl.BlockSpec((1, sw, LANE), lambda g, j, c: (j, 0, 0)),
                  pl.BlockSpec((1, sw, LANE), lambda g, j, c: (j, 0, 0)),
                  pl.BlockSpec((1, sw), lambda g, j, c: (0, j)),
                  pl.BlockSpec((1, sw), lambda g, j, c: (0, j)),
                  pl.BlockSpec((1, LANE), lambda g, j, c: (0, j)),
                  x_spec, x_spec],
        out_specs=[pl.BlockSpec((1, cr, LANE), lambda g, j, c: (g, c, j)), x_spec, x_spec],
        out_shape=[jax.ShapeDtypeStruct((ng, n_tok, S5_W), F32),
                   jax.ShapeDtypeStruct((ng, rows_per_step, S5_STATE), F32),
                   jax.ShapeDtypeStruct((ng, rows_per_step, S5_STATE), F32)],
        scratch_shapes=[pltpu.VMEM((rows_per_step, sw), F32), pltpu.VMEM((rows_per_step, sw), F32),
                        pltpu.VMEM((cr, sw), F32), pltpu.VMEM((cr, sw), F32)],
        compiler_params=_cparams(3), name="s5_scan",
    )(proj, w_re, w_im, c_re, c_im, lb_re, lb_im, d.reshape(1, S5_W), x0_re, x0_im)


def _s5_pow_kernel(lbr_ref, lbi_ref, pr_ref, pi_ref, *, n_rows):
    l_re, l_im = lbr_ref[...], lbi_ref[...]
    row = lax.broadcasted_iota(jnp.int32, (SUBLANE, 1), 0)
    p_re, p_im = l_re, l_im
    b_re = jnp.broadcast_to(l_re, (SUBLANE, l_re.shape[1]))
    b_im = jnp.broadcast_to(l_im, (SUBLANE, l_re.shape[1]))
    for r in range(1, SUBLANE):
        p_re, p_im = p_re * l_re - p_im * l_im, p_re * l_im + p_im * l_re
        b_re = jnp.where(row >= r, p_re, b_re)
        b_im = jnp.where(row >= r, p_im, b_im)
    q_re, q_im = jnp.ones_like(l_re), jnp.zeros_like(l_re)
    for a in range(n_rows // SUBLANE):
        rows = slice(a * SUBLANE, (a + 1) * SUBLANE)
        pr_ref[rows, :] = b_re * q_re - b_im * q_im
        pi_ref[rows, :] = b_re * q_im + b_im * q_re
        q_re, q_im = q_re * p_re - q_im * p_im, q_re * p_im + q_im * p_re


def _s5_pow_table(lb_re, lb_im, n_rows):
    shape = jax.ShapeDtypeStruct((n_rows, S5_STATE), F32)
    return pl.pallas_call(functools.partial(_s5_pow_kernel, n_rows=n_rows), out_shape=[shape, shape],
                          name="s5_pow")(lb_re, lb_im)


def _s5_seg_kernel(u_ref, wre_ref, wim_ref, cre_ref, cim_ref, pr_ref, pi_ref, d_ref, x0r_ref, x0i_ref,
                   y_ref, xfr_ref, xfi_ref, sr_scr, si_scr, *, seg_len):
    n_seg = SUBLANE
    n_lane_blk = sr_scr.shape[0]
    sw = n_lane_blk * LANE
    lane_blks = [slice(c * LANE, (c + 1) * LANE) for c in range(n_lane_blk)]

    def put(scr, rows, val):
        for c, ls in enumerate(lane_blks):
            scr[c, rows, :] = val[:, ls]

    def get(scr, rows):
        return jnp.concatenate([scr[c, rows, :] for c in range(n_lane_blk)], axis=1)

    for s in range(n_seg):
        us = u_ref[0, s * seg_len:(s + 1) * seg_len, :]
        put(sr_scr, pl.ds(s, seg_len, stride=n_seg), _dot(us, wre_ref[0]))
        put(si_scr, pl.ds(s, seg_len, stride=n_seg), _dot(us, wim_ref[0]))
    l_re, l_im = pr_ref[0:1, :], pi_ref[0:1, :]

    def step(t, carry):
        xr, xi = carry
        rows = pl.ds(pl.multiple_of(t * n_seg, n_seg), n_seg)
        nr = l_re * xr - l_im * xi + get(sr_scr, rows)
        ni = l_re * xi + l_im * xr + get(si_scr, rows)
        put(sr_scr, rows, nr)
        put(si_scr, rows, ni)
        return nr, ni

    zero = jnp.zeros((n_seg, sw), F32)
    end_re, end_im = lax.fori_loop(0, seg_len, step, (zero, zero), unroll=4)
    ln_re, ln_im = pr_ref[seg_len - 1:seg_len, :], pi_ref[seg_len - 1:seg_len, :]
    p_re, p_im = pr_ref[...], pi_ref[...]
    x_re, x_im = x0r_ref[0], x0i_ref[0]
    for s in range(n_seg):
        loc_re = get(sr_scr, pl.ds(s, seg_len, stride=n_seg))
        loc_im = get(si_scr, pl.ds(s, seg_len, stride=n_seg))
        t_re = loc_re + p_re * x_re - p_im * x_im
        t_im = loc_im + p_re * x_im + p_im * x_re
        us = u_ref[0, s * seg_len:(s + 1) * seg_len, :]
        y_ref[0, s * seg_len:(s + 1) * seg_len, :] = (_dot(t_re, cre_ref[0]) - _dot(t_im, cim_ref[0])
                                                      + d_ref[...] * us)
        e_re, e_im = end_re[s:s + 1, :], end_im[s:s + 1, :]
        x_re, x_im = e_re + ln_re * x_re - ln_im * x_im, e_im + ln_re * x_im + ln_im * x_re
    xfr_ref[0] = x_re
    xfi_ref[0] = x_im


def _s5_seg(proj, w_re, w_im, c_re, c_im, pow_re, pow_im, d, x0_re, x0_im):
    nb, seq, _ = proj.shape
    nj = S5_W // LANE
    sw = S5_STATE // nj
    seg_len = seq // SUBLANE
    x_spec = pl.BlockSpec((1, 1, sw), lambda b, j: (b, 0, j))
    return pl.pallas_call(
        functools.partial(_s5_seg_kernel, seg_len=seg_len),
        grid=(nb, nj),
        in_specs=[pl.BlockSpec((1, seq, LANE), lambda b, j: (b, 0, CD_U // LANE + j)),
                  pl.BlockSpec((1, LANE, sw), lambda b, j: (j, 0, 0)),
                  pl.BlockSpec((1, LANE, sw), lambda b, j: (j, 0, 0)),
                  pl.BlockSpec((1, sw, LANE), lambda b, j: (j, 0, 0)),
                  pl.BlockSpec((1, sw, LANE), lambda b, j: (j, 0, 0)),
                  pl.BlockSpec((seg_len, sw), lambda b, j: (0, j)),
                  pl.BlockSpec((seg_len, sw), lambda b, j: (0, j)),
                  pl.BlockSpec((1, LANE), lambda b, j: (0, j)),
                  x_spec, x_spec],
        out_specs=[pl.BlockSpec((1, seq, LANE), lambda b, j: (b, 0, j)), x_spec, x_spec],
        out_shape=[jax.ShapeDtypeStruct((nb, seq, S5_W), F32),
                   jax.ShapeDtypeStruct((nb, 1, S5_STATE), F32),
                   jax.ShapeDtypeStruct((nb, 1, S5_STATE), F32)],
        scratch_shapes=[pltpu.VMEM((sw // LANE, seq, LANE), F32), pltpu.VMEM((sw // LANE, seq, LANE), F32)],
        compiler_params=_cparams(2), name="s5_seg",
    )(proj, w_re, w_im, c_re, c_im, pow_re, pow_im, d.reshape(1, S5_W), x0_re, x0_im)


def _lane_row(vec, lane0):
    return jnp.zeros((1, LANE), F32).at[0, lane0:lane0 + vec.shape[0]].set(vec.astype(F32))


def _prep_params(p):
    d = D_MODEL
    q = {}
    w = p['w_in_ab'][0]
    o_lr = 2 * GLA_QK + GLA_V
    o_r = o_lr + GLA_LR
    o_qkv = o_r + GLA_V
    o_a = o_qkv + GDN_CONV_W
    o_g = o_a + 2 * GDN_H
    q['w_in_ab'] = jnp.concatenate(
        [w[:, :o_lr], w[:, o_r:o_qkv], w[:, o_qkv:o_a], w[:, o_g:], w[:, o_lr:o_r], w[:, o_a:o_g],
         jnp.zeros((d, LANE - GLA_LR - 2 * GDN_H), F32)], axis=1).astype(BF16)
    q['gla_w2'] = jnp.zeros((LANE, GLA_QK), F32).at[:GLA_LR].set(p['gla_w2'][0])
    q['gdn_alog'] = _lane_row(p['gdn_A_log'][0], AB_A_LANE)
    q['gdn_dtb'] = _lane_row(p['gdn_dt_bias'][0], AB_A_LANE)
    q['w_out_ab'] = p['w_out_ab'].astype(BF16)
    w = p['w_in_cd'][0]
    o_xbc = SSD_W
    o_dt = o_xbc + SSD_CONV_W
    o_u = o_dt + SSD_H
    q['w_in_cd'] = jnp.concatenate(
        [w[:, o_xbc:o_dt], w[:, o_dt:o_u], jnp.zeros((d, CD_Z - CD_SMALL - SSD_H), F32), w[:, :o_xbc], w[:, o_u:]],
        axis=1).astype(BF16)
    q['ssd_alog'] = _lane_row(p['ssd_A_log'][0], 0)
    q['ssd_dtb'] = _lane_row(p['ssd_dt_bias'][0], 0)
    q['ssd_d_row'] = jnp.repeat(p['ssd_D'][0].astype(F32), SSD_P).reshape(1, SSD_W)
    q['w_out_cd'] = p['w_out_cd'].astype(BF16)
    lb_re, lb_im, bb_re, bb_im = _s5_prep(p['s5_A_re'][0], p['s5_A_im'][0], p['s5_log_dt'][0],
                                          p['s5_B_re'][0], p['s5_B_im'][0])
    q['s5_lb_re'], q['s5_lb_im'] = lb_re.reshape(1, S5_STATE), lb_im.reshape(1, S5_STATE)
    q['s5_w_re'], q['s5_w_im'] = _block_diag(bb_re), _block_diag(bb_im)
    q['s5_c_re'] = _block_diag(jnp.swapaxes(p['s5_C_re'][0], 1, 2))
    q['s5_c_im'] = _block_diag(jnp.swapaxes(p['s5_C_im'][0], 1, 2))
    q['s5_glu_w'] = p['s5_glu_w'][0].astype(BF16)
    q['w_ffn_up'] = p['w_ffn_up'].astype(BF16)
    q['w_ffn_down'] = p['w_ffn_down'].astype(BF16)
    return q


def _trunk(x, mods, grp, seq_shape, state, p, q):
    nb, seq_len, valid = seq_shape
    s_gla, s_gdn, s_gdnc, s_ssd, s_ssdc, s_re, s_im, s_ffn = state
    prompt = not grp.per_token_mod

    def to_seq(t):
        if prompt:
            return t.reshape(nb, seq_len, t.shape[-1])
        t = jnp.swapaxes(t.reshape(valid, nb, t.shape[-1]), 0, 1)
        return jnp.pad(t, ((0, 0), (0, seq_len - valid), (0, 0)))

    def from_seq(t):
        if prompt:
            return t.reshape(nb * seq_len, t.shape[-1])
        return jnp.swapaxes(t[:, :valid], 0, 1).reshape(valid * nb, t.shape[-1])

    blk = MIX_BLOCK if prompt else seq_len
    chunks = (GLA_CHUNK, GDN_CHUNK, SSD_CHUNK) if prompt else (seq_len,) * 3
    new = {}

    h = _norm_mod(x, p['g_mix'][0], mods[0], grp, 1, 0)
    proj = to_seq(_matmul(h, q['w_in_ab'], grp, AB_N // 3))
    o_a, new['gla'] = _gla(proj, q['gla_w2'], p['gla_b2'][0], p['gla_norm_g'][0], s_gla, chunks[0],
                            GLA_GROUP if prompt else 1, blk, valid, 1 if prompt else SAMPLE_SEQS)
    o_b, new['gdn'], new['gdnc'] = _gdn(proj, p['gdn_conv_w'][0], p['gdn_conv_b'][0], q['gdn_alog'], q['gdn_dtb'],
                                        p['gdn_norm_g'][0], s_gdn, s_gdnc, chunks[1], blk, valid,
                                        1 if prompt else SAMPLE_SEQS)
    x = _mm_residual([(from_seq(o_a), q['w_out_ab'], 0, 0), (from_seq(o_b), q['w_out_ab'], 0, 1)],
                     x, mods[0], grp, 2, D_MODEL)
    h = _norm_mod(x, p['g_ffn'][0], mods[0], grp, 4, 3)
    act, new['ffn0'] = _ffn_up(h, q['w_ffn_up'], p['ffn_conv_w'], p['ffn_conv_b'], 0, s_ffn[0],
                               grp.retiled(FFN_TILE))
    x = _mm_residual([(act, q['w_ffn_down'], 0, 0)], x, mods[0], grp, 5, 1024)

    h = _norm_mod(x, p['g_mix'][1], mods[1], grp, 1, 0)
    proj2 = _matmul(h, q['w_in_cd'], grp, CD_N // 2)
    proj = to_seq(proj2)
    o_c, new['ssd'], new['ssdc'] = _ssd(proj, p['ssd_conv_w'][0], p['ssd_conv_b'][0], q['ssd_alog'], q['ssd_dtb'],
                                        q['ssd_d_row'], p['ssd_norm_g'][0], s_ssd, s_ssdc, chunks[2], blk, valid,
                                        1 if prompt else SAMPLE_SEQS)
    if prompt:
        pow_re, pow_im = _s5_pow_table(q['s5_lb_re'], q['s5_lb_im'], seq_len // SUBLANE)
        yd, new['re'], new['im'] = _s5_seg(proj, q['s5_w_re'], q['s5_w_im'], q['s5_c_re'], q['s5_c_im'],
                                           pow_re, pow_im, p['s5_D'][0], s_re, s_im)
    else:
        yd, new['re'], new['im'] = _s5(proj2.reshape(1, grp.n_tok, CD_N), q['s5_w_re'], q['s5_w_im'],
                                       q['s5_c_re'], q['s5_c_im'], q['s5_lb_re'], q['s5_lb_im'], p['s5_D'][0],
                                       s_re, s_im, nb, valid)
    o_d = _s5_glu(yd.reshape(grp.n_tok, S5_W), q['s5_glu_w'], p['s5_glu_b'][0], grp)
    x = _mm_residual([(from_seq(o_c), q['w_out_cd'], 0, 0), (o_d, q['w_out_cd'], 0, 1)], x, mods[1], grp, 2,
                     D_MODEL)
    h = _norm_mod(x, p['g_ffn'][1], mods[1], grp, 4, 3)
    act, new['ffn1'] = _ffn_up(h, q['w_ffn_up'], p['ffn_conv_w'], p['ffn_conv_b'], 1, s_ffn[1],
                               grp.retiled(FFN_TILE))
    x = _mm_residual([(act, q['w_ffn_down'], 1, 0)], x, mods[1], grp, 5, 1024)
    return _final_rms(x, p['g_final'], grp), new


def kernel(x_prompt, x_sample, c_prompt, c_sample, state_gla, state_gdn, state_gdn_conv, state_ssd, state_ssd_conv, state_s5_re, state_s5_im, state_ffn_conv, w_ada, b_ada, g_mix, g_ffn, w_in_ab, gla_w2, gla_b2, gla_norm_g, gdn_conv_w, gdn_conv_b, gdn_A_log, gdn_dt_bias, gdn_norm_g, w_out_ab, w_in_cd, ssd_conv_w, ssd_conv_b, ssd_A_log, ssd_dt_bias, ssd_D, ssd_norm_g, s5_A_re, s5_A_im, s5_B_re, s5_B_im, s5_C_re, s5_C_im, s5_D, s5_log_dt, s5_glu_w, s5_glu_b, w_out_cd, w_ffn_up, ffn_conv_w, ffn_conv_b, w_ffn_down, g_final):
    p = dict(g_mix=g_mix, g_ffn=g_ffn, w_in_ab=w_in_ab, gla_w2=gla_w2, gla_b2=gla_b2, gla_norm_g=gla_norm_g,
             gdn_conv_w=gdn_conv_w, gdn_conv_b=gdn_conv_b, gdn_A_log=gdn_A_log, gdn_dt_bias=gdn_dt_bias,
             gdn_norm_g=gdn_norm_g, w_out_ab=w_out_ab, w_in_cd=w_in_cd, ssd_conv_w=ssd_conv_w,
             ssd_conv_b=ssd_conv_b, ssd_A_log=ssd_A_log, ssd_dt_bias=ssd_dt_bias, ssd_D=ssd_D,
             ssd_norm_g=ssd_norm_g, s5_A_re=s5_A_re, s5_A_im=s5_A_im, s5_B_re=s5_B_re, s5_B_im=s5_B_im,
             s5_C_re=s5_C_re, s5_C_im=s5_C_im, s5_D=s5_D, s5_log_dt=s5_log_dt, s5_glu_w=s5_glu_w,
             s5_glu_b=s5_glu_b, w_out_cd=w_out_cd, w_ffn_up=w_ffn_up, ffn_conv_w=ffn_conv_w,
             ffn_conv_b=ffn_conv_b, w_ffn_down=w_ffn_down, g_final=g_final)
    bp, lp, d = x_prompt.shape
    bs, ls, _ = x_sample.shape
    q = _prep_params(p)

    bp_pad = -(-bp // SUBLANE) * SUBLANE
    c_all = jnp.concatenate([c_prompt, jnp.zeros((bp_pad - bp, d), F32), c_sample], axis=0)
    mod = _ada_mod(c_all, w_ada, b_ada)
    depth = w_ada.shape[0]
    mods_p = [mod[l, :bp].reshape(bp, 1, 6 * d) for l in range(depth)]
    mods_s = [jnp.tile(mod[l, bp_pad:], (ls, 1)).reshape(1, ls * bs, 6 * d) for l in range(depth)]

    tile_p = 512
    grp_p = _Group(bp * lp, tile_p, False, lp // tile_p, 1)
    zeros = lambda *shape: jnp.zeros(shape, F32)
    state_p = (zeros(bp, GLA_H, GLA_DK, GLA_DV), zeros(bp, GDN_H, GDN_DK, GDN_DV),
               zeros(bp, CONV_K - 1, GDN_CONV_W), zeros(bp, SSD_H, SSD_P, SSD_N),
               zeros(bp, CONV_K - 1, SSD_CONV_W), zeros(bp, 1, S5_STATE), zeros(bp, 1, S5_STATE),
               zeros(depth, bp, FFN_K - 1, 2 * D_FF))
    y_p, new_p = _trunk(x_prompt.reshape(bp * lp, d), mods_p, grp_p, (bp, lp, lp), state_p, p, q)

    grp_s = _Group(bs * ls, bs * ls, True, 1, bs)
    ffn_hist_s = jnp.swapaxes(state_ffn_conv, 1, 2).reshape(depth, 1, (FFN_K - 1) * bs, 2 * D_FF)
    state_s = (state_gla[0], state_gdn[0], state_gdn_conv[0], state_ssd[0], state_ssd_conv[0],
               state_s5_re.reshape(1, bs, S5_STATE), state_s5_im.reshape(1, bs, S5_STATE), ffn_hist_s)
    x_s = jnp.swapaxes(x_sample, 0, 1).reshape(ls * bs, d)
    y_s, new_s = _trunk(x_s, mods_s, grp_s, (bs, SAMPLE_PAD, ls), state_s, p, q)
    y_s = jnp.swapaxes(y_s.reshape(ls, bs, d), 0, 1)

    ffn_p = jnp.stack([new_p['ffn0'], new_p['ffn1']])
    ffn_s = jnp.stack([jnp.swapaxes(new_s[k].reshape(FFN_K - 1, bs, 2 * D_FF), 0, 1) for k in ('ffn0', 'ffn1')])
    s5_shape = lambda t, nb: t.reshape(1, nb, S5_G, S5_P)
    return (y_p.reshape(bp, lp, d), y_s,
            new_p['gla'][None], new_s['gla'][None], new_p['gdn'][None], new_s['gdn'][None],
            new_p['gdnc'][None], new_s['gdnc'][None], new_p['ssd'][None], new_s['ssd'][None],
            new_p['ssdc'][None], new_s['ssdc'][None],
            s5_shape(new_p['re'], bp), s5_shape(new_s['re'], bs), s5_shape(new_p['im'], bp), s5_shape(new_s['im'], bs),
            ffn_p, ffn_s)
```

```python
import functools
import math

import jax
import jax.numpy as jnp
from jax import lax
from jax.experimental import pallas as pl
from jax.experimental.pallas import tpu as pltpu

F32 = jnp.float32
BF16 = jnp.bfloat16
HIGHEST = lax.Precision.HIGHEST
EPS = 1e-6

D_MODEL = 2048
GLA_H, GLA_DK, GLA_DV, GLA_LR = 4, 128, 256, 16
GLA_GATE_NORM = 16.0
GLA_QK, GLA_V = GLA_H * GLA_DK, GLA_H * GLA_DV
GDN_H, GDN_DK, GDN_DV = 8, 128, 128
GDN_QK, GDN_V = GDN_H * GDN_DK, GDN_H * GDN_DV
CONV_K = 4
GDN_CONV_W = 2 * GDN_QK + GDN_V
SSD_P, SSD_H, SSD_G, SSD_N = 64, 16, 2, 128
SSD_W = SSD_H * SSD_P
SSD_CONV_W = SSD_W + 2 * SSD_G * SSD_N
S5_W, S5_GS, S5_G, S5_P = 1024, 16, 64, 64
S5_STATE = S5_G * S5_P
D_FF = 5632
FFN_K = 3

LANE = 128
SUBLANE = 8
VMEM_LIMIT = 48 * 1024 * 1024

AB_Q, AB_K, AB_V, AB_R, AB_QKV, AB_G, AB_SMALL, AB_N = 0, 512, 1024, 2048, 3072, 6144, 7168, 7296
AB_LR_LANE, AB_A_LANE, AB_B_LANE = 0, 16, 24
CD_XBC, CD_SMALL, CD_Z, CD_U, CD_N = 0, 1536, 2048, 3072, 4096

MIX_BLOCK = 256
GLA_CHUNK, GDN_CHUNK, SSD_CHUNK = 16, 128, 128
GLA_GROUP = 4
FFN_TILE = 1024
FFN_SUB = 256
INV_BLOCK = 16
SAMPLE_PAD = 8
SAMPLE_SEQS = 4


def _cparams(n_axes):
    return pltpu.CompilerParams(dimension_semantics=("arbitrary",) * n_axes, vmem_limit_bytes=VMEM_LIMIT)


def _sigmoid(x):
    return 1.0 / (1.0 + jnp.exp(-x))


def _silu(x):
    return x * _sigmoid(x)


def _softplus(x):
    return jnp.maximum(x, 0.0) + jnp.log(1.0 + jnp.exp(-jnp.abs(x)))


def _gelu_tanh(x):
    return 0.5 * x * (1.0 + jnp.tanh(math.sqrt(2.0 / math.pi) * (x + 0.044715 * (x * x * x))))


def _dot(a, b, precision=None):
    return jnp.dot(a, b, precision=precision, preferred_element_type=F32)


def _dot_nt(a, b, precision=None):
    return lax.dot_general(a, b, (((1,), (1,)), ((), ())), precision=precision, preferred_element_type=F32)


def _dot_tn(a, b, precision=None):
    return lax.dot_general(a, b, (((0,), (0,)), ((), ())), precision=precision, preferred_element_type=F32)


def _head_rms(o, g):
    return o * lax.rsqrt(jnp.mean(o * o, -1, keepdims=True) + EPS) * g


class _Group:
    def __init__(self, n_tok, tile, per_token_mod, tiles_per_seq, conv_shift):
        self.n_tok = n_tok
        self.tile = tile
        self.n_tiles = n_tok // tile
        self.per_token_mod = per_token_mod
        self.tiles_per_seq = tiles_per_seq
        self.conv_shift = conv_shift

    def retiled(self, tile):
        seq_rows = self.tile * self.tiles_per_seq
        if seq_rows % tile or self.n_tok % tile:
            return self
        return _Group(self.n_tok, tile, self.per_token_mod, seq_rows // tile, self.conv_shift)

    def mod_spec(self, width, col_block, m_axis):
        if self.per_token_mod:
            return pl.BlockSpec((1, self.tile, width), lambda *g: (0, g[m_axis], col_block(*g)))
        tps = self.tiles_per_seq
        return pl.BlockSpec((1, 1, width), lambda *g: (g[m_axis] // tps, 0, col_block(*g)))


def _ada_kernel(c_ref, w_ref, b_ref, o_ref):
    cs = _silu(c_ref[...]).astype(BF16)
    o_ref[0] = _dot(cs, w_ref[0].astype(BF16)) + b_ref[0]


def _ada_mod(c, w_ada, b_ada):
    depth, d, n = w_ada.shape
    rows = c.shape[0]
    tn = 1024
    return pl.pallas_call(
        _ada_kernel,
        grid=(depth, n // tn),
        in_specs=[pl.BlockSpec((rows, d), lambda l, j: (0, 0)),
                  pl.BlockSpec((1, d, tn), lambda l, j: (l, 0, j)),
                  pl.BlockSpec((1, 1, tn), lambda l, j: (l, 0, j))],
        out_specs=pl.BlockSpec((1, rows, tn), lambda l, j: (l, 0, j)),
        out_shape=jax.ShapeDtypeStruct((depth, rows, n), F32),
        compiler_params=_cparams(2), name="ada_mod",
    )(c, w_ada, b_ada.reshape(depth, 1, n))


def _norm_mod_kernel(x_ref, g_ref, sc_ref, sh_ref, o_ref):
    x = x_ref[...]
    y = x * lax.rsqrt(jnp.mean(x * x, -1, keepdims=True) + EPS) * g_ref[...]
    o_ref[...] = (y * (1.0 + sc_ref[0]) + sh_ref[0]).astype(BF16)


def _norm_mod(x, g, mod, grp, sc_blk, sh_blk):
    d = x.shape[1]
    return pl.pallas_call(
        _norm_mod_kernel,
        grid=(grp.n_tiles,),
        in_specs=[pl.BlockSpec((grp.tile, d), lambda i: (i, 0)),
                  pl.BlockSpec((1, d), lambda i: (0, 0)),
                  grp.mod_spec(d, lambda i: sc_blk, 0),
                  grp.mod_spec(d, lambda i: sh_blk, 0)],
        out_specs=pl.BlockSpec((grp.tile, d), lambda i: (i, 0)),
        out_shape=jax.ShapeDtypeStruct(x.shape, BF16),
        compiler_params=_cparams(1), name="norm_mod",
    )(x, g.reshape(1, d), mod, mod)


def _rms_kernel(x_ref, g_ref, o_ref):
    x = x_ref[...]
    o_ref[...] = x * lax.rsqrt(jnp.mean(x * x, -1, keepdims=True) + EPS) * g_ref[...]


def _final_rms(x, g, grp):
    d = x.shape[1]
    return pl.pallas_call(
        _rms_kernel,
        grid=(grp.n_tiles,),
        in_specs=[pl.BlockSpec((grp.tile, d), lambda i: (i, 0)), pl.BlockSpec((1, d), lambda i: (0, 0))],
        out_specs=pl.BlockSpec((grp.tile, d), lambda i: (i, 0)),
        out_shape=jax.ShapeDtypeStruct(x.shape, F32),
        compiler_params=_cparams(1), name="final_rms",
    )(x, g.reshape(1, d))


def _matmul_kernel(a_ref, w_ref, o_ref):
    o_ref[...] = _dot(a_ref[...], w_ref[...])


def _matmul(a, w, grp, tn):
    k, n = w.shape
    return pl.pallas_call(
        _matmul_kernel,
        grid=(n // tn, grp.n_tiles),
        in_specs=[pl.BlockSpec((grp.tile, k), lambda j, i: (i, 0)),
                  pl.BlockSpec((k, tn), lambda j, i: (0, j))],
        out_specs=pl.BlockSpec((grp.tile, tn), lambda j, i: (i, j)),
        out_shape=jax.ShapeDtypeStruct((a.shape[0], n), F32),
        compiler_params=_cparams(2), name="matmul",
    )(a, w)


def _mm_res_kernel(*refs, n_pairs):
    x_ref, gate_ref, o_ref = refs[2 * n_pairs:]
    y = _dot(refs[0][...].astype(BF16), refs[1][0])
    for p in range(1, n_pairs):
        y = y + _dot(refs[2 * p][...].astype(BF16), refs[2 * p + 1][0])
    o_ref[...] = x_ref[...] + gate_ref[0] * y


def _mm_residual(pairs, x, mod, grp, gate_blk, tn):
    n = x.shape[1]
    in_specs, args = [], []
    for a, w, layer, row_blk in pairs:
        k = a.shape[1]
        in_specs += [pl.BlockSpec((grp.tile, k), lambda j, i: (i, 0)),
                     pl.BlockSpec((1, k, tn), lambda j, i, layer=layer, row_blk=row_blk: (layer, row_blk, j))]
        args += [a, w]
    in_specs += [pl.BlockSpec((grp.tile, tn), lambda j, i: (i, j)),
                 grp.mod_spec(tn, lambda j, i: gate_blk * (n // tn) + j, 1)]
    return pl.pallas_call(
        functools.partial(_mm_res_kernel, n_pairs=len(pairs)),
        grid=(n // tn, grp.n_tiles),
        in_specs=in_specs,
        out_specs=pl.BlockSpec((grp.tile, tn), lambda j, i: (i, j)),
        out_shape=jax.ShapeDtypeStruct(x.shape, F32),
        compiler_params=_cparams(2), name="mm_residual",
    )(*args, x, mod)


def _ffn_up_kernel(h_ref, wa_ref, wg_ref, cwa_ref, cwg_ref, cba_ref, cbg_ref, ha_ref, hg_ref,
                   act_ref, sta_ref, stg_ref, scr_a, scr_g, wba_scr, wbg_scr, *, shift, tile, sub, tiles_per_seq):
    i = pl.program_id(1)
    hist = (FFN_K - 1) * shift
    base = -(-hist // SUBLANE) * SUBLANE

    @pl.when(i == 0)
    def _():
        wba_scr[...] = wa_ref[0].astype(BF16)
        wbg_scr[...] = wg_ref[0].astype(BF16)

    @pl.when(i % tiles_per_seq == 0)
    def _():
        scr_a[base - hist:base, :] = ha_ref[0]
        scr_g[base - hist:base, :] = hg_ref[0]

    def conv(scr, cw_ref, cb_ref, r0):
        y = cb_ref[0]
        for j in range(FFN_K):
            lo = base + r0 - (FFN_K - 1 - j) * shift
            y = y + scr[lo:lo + sub, :] * cw_ref[0, j:j + 1, :]
        return y

    def project(r0):
        h = h_ref[r0:r0 + sub, :]
        scr_a[base + r0:base + r0 + sub, :] = _dot(h, wba_scr[...])
        scr_g[base + r0:base + r0 + sub, :] = _dot(h, wbg_scr[...])

    project(0)
    for r0 in range(0, tile, sub):
        if r0 + sub < tile:
            project(r0 + sub)
        a = conv(scr_a, cwa_ref, cba_ref, r0)
        g = conv(scr_g, cwg_ref, cbg_ref, r0)
        act_ref[r0:r0 + sub, :] = (_silu(g) * a).astype(BF16)
    last_a = scr_a[base + tile - hist:base + tile, :]
    last_g = scr_g[base + tile - hist:base + tile, :]
    sta_ref[0] = last_a
    stg_ref[0] = last_g
    scr_a[base - hist:base, :] = last_a
    scr_g[base - hist:base, :] = last_g


def _ffn_up(h, w_up, conv_w, conv_b, layer, hist0, grp):
    d = h.shape[1]
    tn = 512
    nj = D_FF // tn
    shift = grp.conv_shift
    hist = (FFN_K - 1) * shift
    base = -(-hist // SUBLANE) * SUBLANE
    n_seq = grp.n_tiles // grp.tiles_per_seq
    tps = grp.tiles_per_seq
    cb = conv_b.reshape(conv_b.shape[0], 1, 2 * D_FF)
    kern = functools.partial(_ffn_up_kernel, shift=shift, tile=grp.tile, sub=min(FFN_SUB, grp.tile),
                             tiles_per_seq=tps)
    assert grp.tile % min(FFN_SUB, grp.tile) == 0
    act, st_a, st_g = pl.pallas_call(
        kern,
        grid=(nj, grp.n_tiles),
        in_specs=[pl.BlockSpec((grp.tile, d), lambda j, i: (i, 0)),
                  pl.BlockSpec((1, d, tn), lambda j, i: (layer, 0, j)),
                  pl.BlockSpec((1, d, tn), lambda j, i: (layer, 0, nj + j)),
                  pl.BlockSpec((1, FFN_K, tn), lambda j, i: (layer, 0, j)),
                  pl.BlockSpec((1, FFN_K, tn), lambda j, i: (layer, 0, nj + j)),
                  pl.BlockSpec((1, 1, tn), lambda j, i: (layer, 0, j)),
                  pl.BlockSpec((1, 1, tn), lambda j, i: (layer, 0, nj + j)),
                  pl.BlockSpec((1, hist, tn), lambda j, i: (i // tps, 0, j)),
                  pl.BlockSpec((1, hist, tn), lambda j, i: (i // tps, 0, nj + j))],
        out_specs=[pl.BlockSpec((grp.tile, tn), lambda j, i: (i, j)),
                   pl.BlockSpec((1, hist, tn), lambda j, i: (i // tps, 0, j)),
                   pl.BlockSpec((1, hist, tn), lambda j, i: (i // tps, 0, j))],
        out_shape=[jax.ShapeDtypeStruct((h.shape[0], D_FF), BF16),
                   jax.ShapeDtypeStruct((n_seq, hist, D_FF), F32),
                   jax.ShapeDtypeStruct((n_seq, hist, D_FF), F32)],
        scratch_shapes=[pltpu.VMEM((base + grp.tile, tn), F32), pltpu.VMEM((base + grp.tile, tn), F32),
                        pltpu.VMEM((d, tn), BF16), pltpu.VMEM((d, tn), BF16)],
        compiler_params=_cparams(2), name="ffn_up",
    )(h, w_up, w_up, conv_w, conv_w, cb, cb, hist0, hist0)
    return act, jnp.concatenate([st_a, st_g], axis=-1)


def _glu_kernel(y_ref, w_ref, b_ref, o_ref):
    z5 = _gelu_tanh(y_ref[...])
    o_ref[...] = (z5 * _sigmoid(_dot(z5.astype(BF16), w_ref[...]) + b_ref[...])).astype(o_ref.dtype)


def _s5_glu(yd, w, b, grp):
    n = yd.shape[1]
    return pl.pallas_call(
        _glu_kernel,
        grid=(grp.n_tiles,),
        in_specs=[pl.BlockSpec((grp.tile, n), lambda i: (i, 0)),
                  pl.BlockSpec((n, n), lambda i: (0, 0)),
                  pl.BlockSpec((1, n), lambda i: (0, 0))],
        out_specs=pl.BlockSpec((grp.tile, n), lambda i: (i, 0)),
        out_shape=jax.ShapeDtypeStruct(yd.shape, BF16),
        compiler_params=_cparams(1), name="s5_glu",
    )(yd, w, b.reshape(1, n))


def _mixer_out_dtype(block):
    return BF16 if block % (2 * SUBLANE) == 0 else F32


def _causal_conv_chunk(x, cv_scr, cw_ref, cb_ref, chunk, valid):
    base = SUBLANE
    cv_scr[base:base + chunk, :] = x
    y = cb_ref[...]
    for j in range(CONV_K):
        lo = base - (CONV_K - 1) + j
        y = y + cv_scr[lo:lo + chunk, :] * cw_ref[j:j + 1, :]
    last = cv_scr[base + valid - (CONV_K - 1):base + valid, :]
    cv_scr[base - (CONV_K - 1):base, :] = last
    return y


def _gla_kernel(q_ref, k_ref, v_ref, r_ref, sm_ref, w2_ref, b2_ref, ng_ref, s0_ref,
                o_ref, sout_ref, s_scr, b_scr, *, chunk, group, block, valid, nseq):
    blk = pl.program_id(1)

    @pl.when(blk == 0)
    def _():
        s_scr[...] = s0_ref[...]

    span = chunk * group
    lane = lax.broadcasted_iota(jnp.int32, (1, GLA_DK), 1)
    ones_kk = jnp.ones((GLA_DK, GLA_DK), BF16)
    row_in_chunk = lax.broadcasted_iota(jnp.int32, (span, 1), 0) % chunk
    eye = (lax.broadcasted_iota(jnp.int32, (GLA_DK, GLA_DK), 0)
           == lax.broadcasted_iota(jnp.int32, (GLA_DK, GLA_DK), 1))
    n_valid = min(valid, chunk)
    heads = range(GLA_H)
    chunks = range(group)

    def to3(t):
        return t.reshape(group, chunk, t.shape[-1])

    ri = lax.broadcasted_iota(jnp.int32, (block, block), 0)
    ci = lax.broadcasted_iota(jnp.int32, (block, block), 1)
    tri = ((ri // chunk == ci // chunk) & (ri >= ci)).astype(BF16)
    row_blk = lax.broadcasted_iota(jnp.int32, (block, 1), 0) % chunk
    for sq in range(nseq):
        x = _dot(sm_ref[sq].astype(BF16), w2_ref[...].astype(BF16)) + b2_ref[...]
        log_a = (jnp.minimum(x, 0.0) - jnp.log(1.0 + jnp.exp(-jnp.abs(x)))) * (1.0 / GLA_GATE_NORM)
        if valid < chunk:
            log_a = jnp.where(row_blk < valid, log_a, 0.0)
        b_scr[sq] = sum(_dot(tri, part) for part in _split3(log_a))

    def one_seq(sq, rows):
        b = b_scr[sq, rows, :]
        q = q_ref[sq, rows, :] * GLA_DK ** -0.5
        k = k_ref[sq, rows, :]
        v = v_ref[sq, rows, :]
        r = r_ref[sq, rows, :]
        b3, q_in, kv, d_col = [], [], [], []
        for h in heads:
            ks = slice(h * GLA_DK, (h + 1) * GLA_DK)
            bh3 = to3(b[:, ks])
            b_last = bh3[:, chunk - 1:chunk, :]
            k_out = (to3(k[:, ks]) * jnp.exp(b_last - bh3)).reshape(span, GLA_DK)
            vh = v[:, h * GLA_DV:(h + 1) * GLA_DV]
            b3.append(bh3)
            q_in.append(q[:, ks] * jnp.exp(b[:, ks]))
            kv.append([_dot_tn(k_out[c * chunk:(c + 1) * chunk], vh[c * chunk:(c + 1) * chunk]) for c in chunks])
            d_col.append([jnp.sum(jnp.where(eye, jnp.exp(b_last[c]), 0.0), axis=-1, keepdims=True) for c in chunks])
        st = [s_scr[sq, h] for h in heads]
        o_inter = [[] for _ in heads]
        for c in chunks:
            for h in heads:
                o_inter[h].append(_dot(q_in[h][c * chunk:(c + 1) * chunk], st[h]))
                st[h] = st[h] * d_col[h][c] + kv[h][c]
        for h in heads:
            ks = slice(h * GLA_DK, (h + 1) * GLA_DK)
            vs = slice(h * GLA_DV, (h + 1) * GLA_DV)
            s_scr[sq, h] = st[h]
            bh3, qh3, kh3, vh3 = b3[h], to3(q[:, ks]), to3(k[:, ks]), to3(v[:, vs])
            prods = []
            for j in range(n_valid):
                e = jnp.exp(bh3 - bh3[:, j:j + 1, :])
                prods.append((qh3 * e * kh3[:, j:j + 1, :]).reshape(span, GLA_DK).astype(BF16))
            sums = _dot(jnp.concatenate(prods, axis=0), ones_kk)
            att = jnp.zeros((span, GLA_DK), F32)
            for j in range(n_valid):
                att = jnp.where(lane == j, sums[j * span:(j + 1) * span], att)
            att = jnp.where(row_in_chunk >= lane, att, 0.0)[:, :chunk]
            vh = v[:, vs]
            o_intra = [_dot(att[c * chunk:(c + 1) * chunk], vh[c * chunk:(c + 1) * chunk]) for c in chunks]
            o = jnp.concatenate(o_intra, axis=0) + jnp.concatenate(o_inter[h], axis=0)
            o_ref[sq, rows, vs] = (_head_rms(o, ng_ref[...]) * _silu(r[:, vs])).astype(o_ref.dtype)

    def do_span(s, carry):
        rows = pl.ds(pl.multiple_of(s * span, span), span)
        for sq in range(nseq):
            one_seq(sq, rows)
        return carry

    lax.fori_loop(0, block // span, do_span, 0)

    @pl.when(blk == pl.num_programs(1) - 1)
    def _():
        sout_ref[...] = s_scr[...]


def _gla(proj, w2p, b2, ng, s0, chunk, group, block, valid, nseq):
    nb, seq, _ = proj.shape

    def col(width, off):
        return pl.BlockSpec((nseq, block, width), lambda b, i: (b, i, off // width))

    st_spec = pl.BlockSpec((nseq, GLA_H, GLA_DK, GLA_DV), lambda b, i: (b, 0, 0, 0))
    kern = functools.partial(_gla_kernel, chunk=chunk, group=group, block=block, valid=valid, nseq=nseq)
    return pl.pallas_call(
        kern,
        grid=(nb // nseq, seq // block),
        in_specs=[col(GLA_QK, AB_Q), col(GLA_QK, AB_K), col(GLA_V, AB_V), col(GLA_V, AB_R), col(LANE, AB_SMALL),
                  pl.BlockSpec((LANE, GLA_QK), lambda b, i: (0, 0)),
                  pl.BlockSpec((1, GLA_QK), lambda b, i: (0, 0)),
                  pl.BlockSpec((1, GLA_DV), lambda b, i: (0, 0)),
                  st_spec],
        out_specs=[pl.BlockSpec((nseq, block, GLA_V), lambda b, i: (b, i, 0)), st_spec],
        out_shape=[jax.ShapeDtypeStruct((nb, seq, GLA_V), _mixer_out_dtype(block)),
                   jax.ShapeDtypeStruct((nb, GLA_H, GLA_DK, GLA_DV), F32)],
        scratch_shapes=[pltpu.VMEM((nseq, GLA_H, GLA_DK, GLA_DV), F32), pltpu.VMEM((nseq, block, GLA_QK), F32)],
        compiler_params=_cparams(2), name="gla",
    )(proj, proj, proj, proj, proj, w2p, b2.reshape(1, GLA_QK), ng.reshape(1, GLA_DV), s0)


def _split2(a):
    hi = a.astype(BF16)
    return hi, (a - hi.astype(F32)).astype(BF16)


def _split3(a):
    hi = a.astype(BF16)
    rest = a - hi.astype(F32)
    mid = rest.astype(BF16)
    return hi, mid, (rest - mid.astype(F32)).astype(BF16)


def _dot3(a, b):
    return _dot(a[0], b[0]) + _dot(a[0], b[1]) + _dot(a[1], b[0])


def _inv_unit_lower_many(mats, n, eye):
    ps = [eye - a for a in mats]
    if n <= 2:
        return ps
    pows = [_split2(a) for a in mats]
    k = 2
    pending = None
    while k < n:
        sq = [_dot3(a, a) for a in pows]
        if pending is not None:
            ps = [p + _dot3(_split2(p), f) for p, f in zip(ps, pending)]
        pows = [_split2(a) for a in sq]
        pending = pows
        k *= 2
    return [p + _dot3(_split2(p), f) for p, f in zip(ps, pending)]


def _gdn_kernel(qkv_ref, sm_ref, gb_ref, cw_ref, cb_ref, alog_ref, dtb_ref, ng_ref, s0_ref, c0_ref,
                o_ref, sout_ref, cout_ref, s_scr, cv_scr, *, chunk, block, valid, nseq):
    blk = pl.program_id(1)

    @pl.when(blk == 0)
    def _():
        s_scr[...] = s0_ref[...]
        cv_scr[:, SUBLANE - (CONV_K - 1):SUBLANE, :] = c0_ref[...]

    ri = lax.broadcasted_iota(jnp.int32, (chunk, chunk), 0)
    ci = lax.broadcasted_iota(jnp.int32, (chunk, chunk), 1)
    causal = ri >= ci
    strict = ri > ci
    eye = (ri == ci).astype(F32)
    tri = causal.astype(F32)
    tri_u = (ri <= ci).astype(F32)
    inv_blk = min(INV_BLOCK, chunk)
    same_blk = (ri // inv_blk) == (ci // inv_blk)
    row = lax.broadcasted_iota(jnp.int32, (chunk, 1), 0)
    n_valid = min(valid, chunk)

    units = [(sq, h) for sq in range(nseq) for h in range(GDN_H)]
    heads = range(len(units))

    def prelude(rows):
        q, k, kb, rhs, dec, gcc = [], [], [], [], [], []
        for sq, h in units:
            if h == 0:
                act = _silu(_causal_conv_chunk(qkv_ref[sq, rows, :], cv_scr.at[sq], cw_ref, cb_ref, chunk, n_valid))
                sm = sm_ref[sq, rows, :]
                g_all = -jnp.exp(alog_ref[...]) * _softplus(sm + dtb_ref[...])
                beta_all = _sigmoid(sm)
                if valid < chunk:
                    g_all = jnp.where(row < valid, g_all, 0.0)
                    beta_all = jnp.where(row < valid, beta_all, 0.0)
                gc = _dot(tri, g_all, HIGHEST)
                gc_r = _dot_tn(g_all, tri_u, HIGHEST)
            qh = act[:, h * GDN_DK:(h + 1) * GDN_DK]
            kh = act[:, GDN_QK + h * GDN_DK:GDN_QK + (h + 1) * GDN_DK]
            vh = act[:, 2 * GDN_QK + h * GDN_DV:2 * GDN_QK + (h + 1) * GDN_DV]
            qh = qh * lax.rsqrt(jnp.sum(qh * qh, -1, keepdims=True) + EPS) * GDN_DK ** -0.5
            kh = kh * lax.rsqrt(jnp.sum(kh * kh, -1, keepdims=True) + EPS)
            beta = beta_all[:, AB_B_LANE + h:AB_B_LANE + h + 1]
            gch = gc[:, AB_A_LANE + h:AB_A_LANE + h + 1]
            gcr = gc_r[AB_A_LANE + h:AB_A_LANE + h + 1, :]
            q.append(qh)
            k.append(kh)
            kb.append(kh * beta)
            rhs.append(_split2(jnp.concatenate([vh * beta, kb[-1] * jnp.exp(gch)], axis=1)))
            dec.append(jnp.exp(jnp.where(causal, gch - gcr, -jnp.inf)))
            gcc.append(gch)
        kbf = [a.astype(BF16) for a in k]
        m = [jnp.where(strict, _dot_nt(kb[h].astype(BF16), kbf[h]) * dec[h], 0.0) for h in heads]
        att = [_dot_nt(q[h].astype(BF16), kbf[h]) * dec[h] for h in heads]
        return q, k, rhs, gcc, m, att

    def finish(pre, rows):
        q, k, rhs, gcc, m, att = pre
        m_diag = [jnp.where(same_blk, a, 0.0) for a in m]
        t = [_split2(a) for a in _inv_unit_lower_many(m_diag, inv_blk, eye)]
        y = [_dot3(t[h], rhs[h]) for h in heads]
        if chunk > inv_blk:
            n_off = [_dot3(t[h], _split2(m[h] - m_diag[h])) for h in heads]
            qn = [_split2(a) for a in _inv_unit_lower_many(n_off, chunk // inv_blk, eye)]
            y = [_dot3(qn[h], _split2(y[h])) for h in heads]
        st = [s_scr[sq, h] for sq, h in units]
        stb = [a.astype(BF16) for a in st]
        v_new = [y[h][:, :GDN_DV] - _dot(y[h][:, GDN_DV:].astype(BF16), stb[h]) for h in heads]
        vnb = [a.astype(BF16) for a in v_new]
        o = [_dot((q[h] * jnp.exp(gcc[h])).astype(BF16), stb[h]) + _dot(att[h].astype(BF16), vnb[h]) for h in heads]
        for u, (sq, h) in enumerate(units):
            hs = slice(h * GDN_DV, (h + 1) * GDN_DV)
            g_last = gcc[u][chunk - 1:chunk, :]
            k_out = (k[u] * jnp.exp(g_last - gcc[u])).astype(BF16)
            s_scr[sq, h] = st[u] * jnp.exp(g_last) + _dot_tn(k_out, vnb[u])
            o_ref[sq, rows, hs] = (_head_rms(o[u], ng_ref[...]) * _silu(gb_ref[sq, rows, hs])).astype(o_ref.dtype)

    n_chunks = block // chunk
    pre = prelude(pl.ds(0, chunk))
    for c in range(n_chunks):
        nxt = prelude(pl.ds((c + 1) * chunk, chunk)) if c + 1 < n_chunks else None
        finish(pre, pl.ds(c * chunk, chunk))
        pre = nxt

    @pl.when(blk == pl.num_programs(1) - 1)
    def _():
        sout_ref[...] = s_scr[...]
        cout_ref[...] = cv_scr[:, SUBLANE - (CONV_K - 1):SUBLANE, :]


def _gdn(proj, conv_w, conv_b, alog_row, dtb_row, ng, s0, c0, chunk, block, valid, nseq):
    nb, seq, _ = proj.shape

    def col(width, off):
        return pl.BlockSpec((nseq, block, width), lambda b, i: (b, i, off // width))

    def const(shape):
        return pl.BlockSpec(shape, lambda b, i: (0,) * len(shape))

    st_spec = pl.BlockSpec((nseq, GDN_H, GDN_DK, GDN_DV), lambda b, i: (b, 0, 0, 0))
    cv_spec = pl.BlockSpec((nseq, CONV_K - 1, GDN_CONV_W), lambda b, i: (b, 0, 0))
    kern = functools.partial(_gdn_kernel, chunk=chunk, block=block, valid=valid, nseq=nseq)
    return pl.pallas_call(
        kern,
        grid=(nb // nseq, seq // block),
        in_specs=[col(GDN_CONV_W, AB_QKV), col(LANE, AB_SMALL), col(GDN_V, AB_G),
                  const((CONV_K, GDN_CONV_W)), const((1, GDN_CONV_W)), const((1, LANE)), const((1, LANE)),
                  const((1, GDN_DV)), st_spec, cv_spec],
        out_specs=[pl.BlockSpec((nseq, block, GDN_V), lambda b, i: (b, i, 0)), st_spec, cv_spec],
        out_shape=[jax.ShapeDtypeStruct((nb, seq, GDN_V), _mixer_out_dtype(block)),
                   jax.ShapeDtypeStruct((nb, GDN_H, GDN_DK, GDN_DV), F32),
                   jax.ShapeDtypeStruct((nb, CONV_K - 1, GDN_CONV_W), F32)],
        scratch_shapes=[pltpu.VMEM((nseq, GDN_H, GDN_DK, GDN_DV), F32),
                        pltpu.VMEM((nseq, SUBLANE + chunk, GDN_CONV_W), F32)],
        compiler_params=_cparams(2), name="gdn",
    )(proj, proj, proj, conv_w, conv_b.reshape(1, GDN_CONV_W), alog_row, dtb_row, ng.reshape(1, GDN_DV), s0, c0)


def _ssd_kernel(xbc_ref, sm_ref, z_ref, cw_ref, cb_ref, alog_ref, dtb_ref, drow_ref, ng_ref, s0_ref, c0_ref,
                o_ref, sout_ref, cout_ref, s_scr, cv_scr, dt_scr, acs_scr, acsr_scr, dtr_scr,
                *, chunk, block, valid, nseq):
    blk = pl.program_id(1)

    @pl.when(blk == 0)
    def _():
        s_scr[...] = s0_ref[...]
        cv_scr[:, SUBLANE - (CONV_K - 1):SUBLANE, :] = c0_ref[...]

    ri = lax.broadcasted_iota(jnp.int32, (chunk, chunk), 0)
    ci = lax.broadcasted_iota(jnp.int32, (chunk, chunk), 1)
    causal = ri >= ci
    eye = (ri == ci).astype(BF16)
    tri = causal.astype(BF16)
    tri_u = (ri <= ci).astype(BF16)
    row = lax.broadcasted_iota(jnp.int32, (chunk, 1), 0)
    lane_lo = lax.broadcasted_iota(jnp.int32, (chunk, LANE), 1) < SSD_P
    row_lo = lax.broadcasted_iota(jnp.int32, (2 * SSD_P, 1), 0) < SSD_P
    n_valid = min(valid, chunk)
    heads_per_group = SSD_H // SSD_G
    gsz = SSD_W // SSD_G

    for sq in range(nseq):
        for c in range(block // chunk):
            dt = _softplus(sm_ref[sq, c * chunk:(c + 1) * chunk, :] + dtb_ref[...])
            if valid < chunk:
                dt = jnp.where(row < valid, dt, 0.0)
            dta = _split3(dt * (-jnp.exp(alog_ref[...])))
            dt_scr[sq, c] = dt
            acs_scr[sq, c] = sum(_dot(tri, part) for part in dta)
            acsr_scr[sq, c] = sum(_dot_tn(part, tri_u) for part in dta)
            dtr_scr[sq, c] = sum(_dot_tn(part, eye) for part in _split3(dt))

    def one_seq(sq, s, rows):
        act = _silu(_causal_conv_chunk(xbc_ref[sq, rows, :], cv_scr.at[sq], cw_ref, cb_ref, chunk, n_valid))
        dt, acs, acs_r, dt_r = dt_scr[sq, s], acs_scr[sq, s], acsr_scr[sq, s], dtr_scr[sq, s]
        z = z_ref[sq, rows, :]
        for g in range(SSD_G):
            bg = act[:, SSD_W + g * SSD_N:SSD_W + (g + 1) * SSD_N]
            cg = act[:, SSD_W + SSD_G * SSD_N + g * SSD_N:SSD_W + SSD_G * SSD_N + (g + 1) * SSD_N]
            cb = _dot_nt(cg, bg)
            parts = []
            for pr in range(heads_per_group // 2):
                pi = g * (heads_per_group // 2) + pr
                xp = act[:, pi * LANE:(pi + 1) * LANE]
                st = s_scr[sq, pi]
                y_in, e_in, w_out, d_last = [], [], [], []
                for hh in range(2):
                    h = 2 * pi + hh
                    ac = acs[:, h:h + 1]
                    dec = jnp.exp(jnp.where(causal, ac - acs_r[h:h + 1, :], -jnp.inf))
                    y_in.append(_dot(cb * dec * dt_r[h:h + 1, :], xp))
                    a_last = ac[chunk - 1:chunk, :]
                    e_in.append(jnp.exp(ac))
                    w_out.append(jnp.exp(a_last - ac) * dt[:, h:h + 1])
                    d_last.append(jnp.exp(a_last))
                y = jnp.where(lane_lo, y_in[0], y_in[1])
                y = y + _dot_nt(cg, st) * jnp.where(lane_lo, e_in[0], e_in[1])
                y = y + drow_ref[:, pi * LANE:(pi + 1) * LANE] * xp
                x_sc = xp * jnp.where(lane_lo, w_out[0], w_out[1])
                s_scr[sq, pi] = st * jnp.where(row_lo, d_last[0], d_last[1]) + _dot_tn(x_sc, bg)
                parts.append(y)
            gs = slice(g * gsz, (g + 1) * gsz)
            yg = jnp.concatenate(parts, axis=1) * _silu(z[:, gs])
            o_ref[sq, rows, gs] = _head_rms(yg, ng_ref[:, gs]).astype(o_ref.dtype)

    def do_chunk(s, carry):
        rows = pl.ds(pl.multiple_of(s * chunk, chunk), chunk)
        for sq in range(nseq):
            one_seq(sq, s, rows)
        return carry

    lax.fori_loop(0, block // chunk, do_chunk, 0)

    @pl.when(blk == pl.num_programs(1) - 1)
    def _():
        sout_ref[...] = s_scr[...]
        cout_ref[...] = cv_scr[:, SUBLANE - (CONV_K - 1):SUBLANE, :]


def _ssd(proj, conv_w, conv_b, alog_row, dtb_row, d_row, ng, s0, c0, chunk, block, valid, nseq):
    nb, seq, _ = proj.shape
    n_pairs = SSD_H // 2

    def col(width, off):
        return pl.BlockSpec((nseq, block, width), lambda b, i: (b, i, off // width))

    def const(shape):
        return pl.BlockSpec(shape, lambda b, i: (0,) * len(shape))

    st_spec = pl.BlockSpec((nseq, n_pairs, 2 * SSD_P, SSD_N), lambda b, i: (b, 0, 0, 0))
    cv_spec = pl.BlockSpec((nseq, CONV_K - 1, SSD_CONV_W), lambda b, i: (b, 0, 0))
    kern = functools.partial(_ssd_kernel, chunk=chunk, block=block, valid=valid, nseq=nseq)
    o, s_new, c_new = pl.pallas_call(
        kern,
        grid=(nb // nseq, seq // block),
        in_specs=[col(SSD_CONV_W, CD_XBC), col(LANE, CD_SMALL), col(SSD_W, CD_Z),
                  const((CONV_K, SSD_CONV_W)), const((1, SSD_CONV_W)), const((1, LANE)), const((1, LANE)),
                  const((1, SSD_W)), const((1, SSD_W)), st_spec, cv_spec],
        out_specs=[pl.BlockSpec((nseq, block, SSD_W), lambda b, i: (b, i, 0)), st_spec, cv_spec],
        out_shape=[jax.ShapeDtypeStruct((nb, seq, SSD_W), _mixer_out_dtype(block)),
                   jax.ShapeDtypeStruct((nb, n_pairs, 2 * SSD_P, SSD_N), F32),
                   jax.ShapeDtypeStruct((nb, CONV_K - 1, SSD_CONV_W), F32)],
        scratch_shapes=[pltpu.VMEM((nseq, n_pairs, 2 * SSD_P, SSD_N), F32),
                        pltpu.VMEM((nseq, SUBLANE + chunk, SSD_CONV_W), F32),
                        pltpu.VMEM((nseq, block // chunk, chunk, LANE), F32),
                        pltpu.VMEM((nseq, block // chunk, chunk, LANE), F32),
                        pltpu.VMEM((nseq, block // chunk, LANE, chunk), F32),
                        pltpu.VMEM((nseq, block // chunk, LANE, chunk), F32)],
        compiler_params=_cparams(2), name="ssd",
    )(proj, proj, proj, conv_w, conv_b.reshape(1, SSD_CONV_W), alog_row, dtb_row, d_row,
      ng.reshape(1, SSD_W), s0.reshape(nb, n_pairs, 2 * SSD_P, SSD_N), c0)
    return o, s_new.reshape(nb, SSD_H, SSD_P, SSD_N), c_new


def _s5_prep_kernel(are_ref, aim_ref, ldt_ref, bre_ref, bim_ref, lbr_ref, lbi_ref, bbr_ref, bbi_ref):
    a_re, a_im = are_ref[...], aim_ref[...]
    dt = jnp.exp(ldt_ref[...])
    mag = jnp.exp(a_re * dt)
    lb_re, lb_im = mag * jnp.cos(a_im * dt), mag * jnp.sin(a_im * dt)
    nr, ni = lb_re - 1.0, lb_im
    den = a_re * a_re + a_im * a_im
    f_re = (nr * a_re + ni * a_im) / den
    f_im = (ni * a_re - nr * a_im) / den
    b_re, b_im = bre_ref[...], bim_ref[...]
    lbr_ref[...] = lb_re
    lbi_ref[...] = lb_im
    bbr_ref[...] = f_re * b_re - f_im * b_im
    bbi_ref[...] = f_re * b_im + f_im * b_re


def _s5_prep(a_re, a_im, log_dt, b_re, b_im):
    g3 = (S5_G, 1, S5_P)
    b3 = (S5_G, S5_GS, S5_P)
    return pl.pallas_call(
        _s5_prep_kernel,
        out_shape=[jax.ShapeDtypeStruct(g3, F32), jax.ShapeDtypeStruct(g3, F32),
                   jax.ShapeDtypeStruct(b3, F32), jax.ShapeDtypeStruct(b3, F32)],
        name="s5_prep",
    )(a_re.reshape(g3), a_im.reshape(g3), log_dt.reshape(S5_G, 1, 1),
      jnp.swapaxes(b_re, 1, 2), jnp.swapaxes(b_im, 1, 2))


def _block_diag(blocks):
    g, r, c = blocks.shape
    per = 8
    b = blocks.reshape(g // per, per, r, 1, c) * jnp.eye(per, dtype=blocks.dtype).reshape(1, per, 1, per, 1)
    return b.reshape(g // per, per * r, per * c)


def _s5_kernel(u_ref, wre_ref, wim_ref, cre_ref, cim_ref, lbr_ref, lbi_ref, d_ref, x0r_ref, x0i_ref,
               y_ref, xfr_ref, xfi_ref, xr_scr, xi_scr, sr_scr, si_scr, *, rows_per_step, steps):
    c = pl.program_id(2)

    @pl.when(c == 0)
    def _():
        xr_scr[...] = x0r_ref[0]
        xi_scr[...] = x0i_ref[0]

    u = u_ref[0]
    sr_scr[...] = _dot(u, wre_ref[0])
    si_scr[...] = _dot(u, wim_ref[0])
    l_re, l_im = lbr_ref[...], lbi_ref[...]

    def step(t, carry):
        rows = pl.ds(pl.multiple_of(t * rows_per_step, rows_per_step), rows_per_step)
        xr, xi = xr_scr[...], xi_scr[...]
        nr = l_re * xr - l_im * xi + sr_scr[rows, :]
        ni = l_re * xi + l_im * xr + si_scr[rows, :]
        xr_scr[...] = nr
        xi_scr[...] = ni
        sr_scr[rows, :] = nr
        si_scr[rows, :] = ni
        return carry

    lax.fori_loop(0, steps, step, 0)
    y_ref[0] = _dot(sr_scr[...], cre_ref[0]) - _dot(si_scr[...], cim_ref[0]) + d_ref[...] * u

    @pl.when(c == pl.num_programs(2) - 1)
    def _():
        xfr_ref[0] = xr_scr[...]
        xfi_ref[0] = xi_scr[...]


def _s5(proj, w_re, w_im, c_re, c_im, lb_re, lb_im, d, x0_re, x0_im, rows_per_step, steps):
    ng, n_tok, _ = proj.shape
    nj = S5_W // LANE
    sw = S5_STATE // nj
    cr = rows_per_step * steps
    kern = functools.partial(_s5_kernel, rows_per_step=rows_per_step, steps=steps)
    x_spec = pl.BlockSpec((1, rows_per_step, sw), lambda g, j, c: (g, 0, j))
    return pl.pallas_call(
        kern,
        grid=(ng, nj, n_tok // cr),
        in_specs=[pl.BlockSpec((1, cr, LANE), lambda g, j, c: (g, c, CD_U // LANE + j)),
                  pl.BlockSpec((1, LANE, sw), lambda g, j, c: (j, 0, 0)),
                  pl.BlockSpec((1, LANE, sw), lambda g, j, c: (j, 0, 0)),
                  pl.BlockSpec((1, sw, LANE), lambda g, j, c: (j, 0, 0)),
                  pl.BlockSpec((1, sw, LANE), lambda g, j, c: (j, 0, 0)),
                  pl.BlockSpec((1, sw), lambda g, j, c: (0, j)),
                  pl.BlockSpec((1, sw), lambda g, j, c: (0, j)),
                  pl.BlockSpec((1, LANE), lambda g, j, c: (0, j)),
                  x_spec, x_spec],
        out_specs=[pl.BlockSpec((1, cr, LANE), lambda g, j, c: (g, c, j)), x_spec, x_spec],
        out_shape=[jax.ShapeDtypeStruct((ng, n_tok, S5_W), F32),
                   jax.ShapeDtypeStruct((ng, rows_per_step, S5_STATE), F32),
                   jax.ShapeDtypeStruct((ng, rows_per_step, S5_STATE), F32)],
        scratch_shapes=[pltpu.VMEM((rows_per_step, sw), F32), pltpu.VMEM((rows_per_step, sw), F32),
                        pltpu.VMEM((cr, sw), F32), pltpu.VMEM((cr, sw), F32)],
        compiler_params=_cparams(3), name="s5_scan",
    )(proj, w_re, w_im, c_re, c_im, lb_re, lb_im, d.reshape(1, S5_W), x0_re, x0_im)


def _s5_pow_kernel(lbr_ref, lbi_ref, pr_ref, pi_ref, *, n_rows):
    l_re, l_im = lbr_ref[...], lbi_ref[...]
    row = lax.broadcasted_iota(jnp.int32, (SUBLANE, 1), 0)
    p_re, p_im = l_re, l_im
    b_re = jnp.broadcast_to(l_re, (SUBLANE, l_re.shape[1]))
    b_im = jnp.broadcast_to(l_im, (SUBLANE, l_re.shape[1]))
    for r in range(1, SUBLANE):
        p_re, p_im = p_re * l_re - p_im * l_im, p_re * l_im + p_im * l_re
        b_re = jnp.where(row >= r, p_re, b_re)
        b_im = jnp.where(row >= r, p_im, b_im)
    q_re, q_im = jnp.ones_like(l_re), jnp.zeros_like(l_re)
    for a in range(n_rows // SUBLANE):
        rows = slice(a * SUBLANE, (a + 1) * SUBLANE)
        pr_ref[rows, :] = b_re * q_re - b_im * q_im
        pi_ref[rows, :] = b_re * q_im + b_im * q_re
        q_re, q_im = q_re * p_re - q_im * p_im, q_re * p_im + q_im * p_re


def _s5_pow_table(lb_re, lb_im, n_rows):
    shape = jax.ShapeDtypeStruct((n_rows, S5_STATE), F32)
    return pl.pallas_call(functools.partial(_s5_pow_kernel, n_rows=n_rows), out_shape=[shape, shape],
                          name="s5_pow")(lb_re, lb_im)


def _s5_seg_kernel(u_ref, wre_ref, wim_ref, cre_ref, cim_ref, pr_ref, pi_ref, d_ref, x0r_ref, x0i_ref,
                   y_ref, xfr_ref, xfi_ref, sr_scr, si_scr, *, seg_len):
    n_seg = SUBLANE
    n_lane_blk = sr_scr.shape[0]
    sw = n_lane_blk * LANE
    lane_blks = [slice(c * LANE, (c + 1) * LANE) for c in range(n_lane_blk)]

    def put(scr, rows, val):
        for c, ls in enumerate(lane_blks):
            scr[c, rows, :] = val[:, ls]

    def get(scr, rows):
        return jnp.concatenate([scr[c, rows, :] for c in range(n_lane_blk)], axis=1)

    for s in range(n_seg):
        us = u_ref[0, s * seg_len:(s + 1) * seg_len, :]
        put(sr_scr, pl.ds(s, seg_len, stride=n_seg), _dot(us, wre_ref[0]))
        put(si_scr, pl.ds(s, seg_len, stride=n_seg), _dot(us, wim_ref[0]))
    l_re, l_im = pr_ref[0:1, :], pi_ref[0:1, :]

    def step(t, carry):
        xr, xi = carry
        rows = pl.ds(pl.multiple_of(t * n_seg, n_seg), n_seg)
        nr = l_re * xr - l_im * xi + get(sr_scr, rows)
        ni = l_re * xi + l_im * xr + get(si_scr, rows)
        put(sr_scr, rows, nr)
        put(si_scr, rows, ni)
        return nr, ni

    zero = jnp.zeros((n_seg, sw), F32)
    end_re, end_im = lax.fori_loop(0, seg_len, step, (zero, zero), unroll=4)
    ln_re, ln_im = pr_ref[seg_len - 1:seg_len, :], pi_ref[seg_len - 1:seg_len, :]
    p_re, p_im = pr_ref[...], pi_ref[...]
    x_re, x_im = x0r_ref[0], x0i_ref[0]
    for s in range(n_seg):
        loc_re = get(sr_scr, pl.ds(s, seg_len, stride=n_seg))
        loc_im = get(si_scr, pl.ds(s, seg_len, stride=n_seg))
        t_re = loc_re + p_re * x_re - p_im * x_im
        t_im = loc_im + p_re * x_im + p_im * x_re
        us = u_ref[0, s * seg_len:(s + 1) * seg_len, :]
        y_ref[0, s * seg_len:(s + 1) * seg_len, :] = (_dot(t_re, cre_ref[0]) - _dot(t_im, cim_ref[0])
                                                      + d_ref[...] * us)
        e_re, e_im = end_re[s:s + 1, :], end_im[s:s + 1, :]
        x_re, x_im = e_re + ln_re * x_re - ln_im * x_im, e_im + ln_re * x_im + ln_im * x_re
    xfr_ref[0] = x_re
    xfi_ref[0] = x_im


def _s5_seg(proj, w_re, w_im, c_re, c_im, pow_re, pow_im, d, x0_re, x0_im):
    nb, seq, _ = proj.shape
    nj = S5_W // LANE
    sw = S5_STATE // nj
    seg_len = seq // SUBLANE
    x_spec = pl.BlockSpec((1, 1, sw), lambda b, j: (b, 0, j))
    return pl.pallas_call(
        functools.partial(_s5_seg_kernel, seg_len=seg_len),
        grid=(nb, nj),
        in_specs=[pl.BlockSpec((1, seq, LANE), lambda b, j: (b, 0, CD_U // LANE + j)),
                  pl.BlockSpec((1, LANE, sw), lambda b, j: (j, 0, 0)),
                  pl.BlockSpec((1, LANE, sw), lambda b, j: (j, 0, 0)),
                  pl.BlockSpec((1, sw, LANE), lambda b, j: (j, 0, 0)),
                  pl.BlockSpec((1, sw, LANE), lambda b, j: (j, 0, 0)),
                  pl.BlockSpec((seg_len, sw), lambda b, j: (0, j)),
                  pl.BlockSpec((seg_len, sw), lambda b, j: (0, j)),
                  pl.BlockSpec((1, LANE), lambda b, j: (0, j)),
                  x_spec, x_spec],
        out_specs=[pl.BlockSpec((1, seq, LANE), lambda b, j: (b, 0, j)), x_spec, x_spec],
        out_shape=[jax.ShapeDtypeStruct((nb, seq, S5_W), F32),
                   jax.ShapeDtypeStruct((nb, 1, S5_STATE), F32),
                   jax.ShapeDtypeStruct((nb, 1, S5_STATE), F32)],
        scratch_shapes=[pltpu.VMEM((sw // LANE, seq, LANE), F32), pltpu.VMEM((sw // LANE, seq, LANE), F32)],
        compiler_params=_cparams(2), name="s5_seg",
    )(proj, w_re, w_im, c_re, c_im, pow_re, pow_im, d.reshape(1, S5_W), x0_re, x0_im)


def _lane_row(vec, lane0):
    return jnp.zeros((1, LANE), F32).at[0, lane0:lane0 + vec.shape[0]].set(vec.astype(F32))


def _realign_kernel(w_ref, tail_ref, o_ref, *, plan):
    lane = lax.broadcasted_iota(jnp.int32, (1, LANE), 1)
    sel_r = lax.broadcasted_iota(jnp.int32, (2 * LANE, LANE), 0)
    sel_c = lax.broadcasted_iota(jnp.int32, (2 * LANE, LANE), 1)
    for t, op in enumerate(plan):
        dst = slice(t * LANE, (t + 1) * LANE)
        if op[0] == 'zero':
            o_ref[:, dst] = jnp.zeros((o_ref.shape[0], LANE), BF16)
        elif op[0] == 'copy':
            o_ref[:, dst] = w_ref[0, :, op[1]:op[1] + LANE].astype(BF16)
        elif op[0] == 'mask':
            acc = jnp.zeros((o_ref.shape[0], LANE), F32)
            for src, lo, hi in op[1]:
                acc = acc + jnp.where((lane >= lo) & (lane < hi), w_ref[0, :, src:src + LANE], 0.0)
            o_ref[:, dst] = acc.astype(BF16)
        else:
            a, d = op[1], op[2]
            if op[0] == 'shift':
                slab = w_ref[0, :, a:a + 2 * LANE]
            else:
                slab = jnp.concatenate([w_ref[0, :, a:a + LANE], tail_ref[...]], axis=1)
            sel = (sel_r == sel_c + d).astype(BF16)
            o_ref[:, dst] = _dot(slab.astype(BF16), sel).astype(BF16)


def _realign_weights(w, segments, masks, n_out):
    _, k, n_src = w.shape
    n_al = n_src // LANE * LANE
    tail = jnp.pad(w[0, :, n_al:], ((0, 0), (0, LANE - (n_src - n_al))))
    plan = [('zero',)] * (n_out // LANE)
    for src0, width, dst0 in segments:
        for i in range(width // LANE):
            src = src0 + i * LANE
            a, d = src // LANE * LANE, src % LANE
            if d == 0:
                op = ('copy', src)
            elif a + 2 * LANE <= n_al:
                op = ('shift', a, d)
            else:
                assert a + LANE == n_al
                op = ('shift_tail', a, d)
            plan[dst0 // LANE + i] = op
    for dst0, pieces in masks.items():
        plan[dst0 // LANE] = ('mask', pieces)
    rows = 256
    return pl.pallas_call(
        functools.partial(_realign_kernel, plan=plan),
        grid=(k // rows,),
        in_specs=[pl.BlockSpec((1, rows, n_src), lambda i: (0, i, 0)), pl.BlockSpec((rows, LANE), lambda i: (i, 0))],
        out_specs=pl.BlockSpec((rows, n_out), lambda i: (i, 0)),
        out_shape=jax.ShapeDtypeStruct((k, n_out), BF16),
        compiler_params=_cparams(1), name="realign_weights",
    )(w, tail)


def _prep_params(p):
    q = {}
    o_lr = 2 * GLA_QK + GLA_V
    o_r = o_lr + GLA_LR
    o_qkv = o_r + GLA_V
    o_a = o_qkv + GDN_CONV_W
    o_g = o_a + 2 * GDN_H
    assert o_lr % LANE == AB_LR_LANE and o_a % LANE == AB_A_LANE
    q['w_in_ab'] = _realign_weights(
        p['w_in_ab'], [(0, o_lr, AB_Q), (o_r, GLA_V, AB_R), (o_qkv, GDN_CONV_W, AB_QKV), (o_g, GDN_V, AB_G)],
        {AB_SMALL: [(o_lr, AB_LR_LANE, AB_LR_LANE + GLA_LR),
                    (o_a - AB_A_LANE, AB_A_LANE, AB_A_LANE + 2 * GDN_H)]}, AB_N)
    q['gla_w2'] = jnp.zeros((LANE, GLA_QK), F32).at[:GLA_LR].set(p['gla_w2'][0])
    q['gdn_alog'] = _lane_row(p['gdn_A_log'][0], AB_A_LANE)
    q['gdn_dtb'] = _lane_row(p['gdn_dt_bias'][0], AB_A_LANE)
    q['w_out_ab'] = p['w_out_ab'].astype(BF16)
    o_xbc = SSD_W
    o_dt = o_xbc + SSD_CONV_W
    o_u = o_dt + SSD_H
    assert o_dt % LANE == 0
    q['w_in_cd'] = _realign_weights(
        p['w_in_cd'], [(o_xbc, SSD_CONV_W, CD_XBC), (0, SSD_W, CD_Z), (o_u, S5_W, CD_U)],
        {CD_SMALL: [(o_dt, 0, SSD_H)]}, CD_N)
    q['ssd_alog'] = _lane_row(p['ssd_A_log'][0], 0)
    q['ssd_dtb'] = _lane_row(p['ssd_dt_bias'][0], 0)
    q['ssd_d_row'] = jnp.repeat(p['ssd_D'][0].astype(F32), SSD_P).reshape(1, SSD_W)
    q['w_out_cd'] = p['w_out_cd'].astype(BF16)
    lb_re, lb_im, bb_re, bb_im = _s5_prep(p['s5_A_re'][0], p['s5_A_im'][0], p['s5_log_dt'][0],
                                          p['s5_B_re'][0], p['s5_B_im'][0])
    q['s5_lb_re'], q['s5_lb_im'] = lb_re.reshape(1, S5_STATE), lb_im.reshape(1, S5_STATE)
    q['s5_w_re'], q['s5_w_im'] = _block_diag(bb_re), _block_diag(bb_im)
    q['s5_c_re'] = _block_diag(jnp.swapaxes(p['s5_C_re'][0], 1, 2))
    q['s5_c_im'] = _block_diag(jnp.swapaxes(p['s5_C_im'][0], 1, 2))
    q['s5_glu_w'] = p['s5_glu_w'][0].astype(BF16)
    q['w_ffn_down'] = p['w_ffn_down'].astype(BF16)
    return q


def _trunk(x, mods, grp, seq_shape, state, p, q):
    nb, seq_len, valid = seq_shape
    s_gla, s_gdn, s_gdnc, s_ssd, s_ssdc, s_re, s_im, s_ffn = state
    prompt = not grp.per_token_mod

    def to_seq(t):
        if prompt:
            return t.reshape(nb, seq_len, t.shape[-1])
        t = jnp.swapaxes(t.reshape(valid, nb, t.shape[-1]), 0, 1)
        return jnp.pad(t, ((0, 0), (0, seq_len - valid), (0, 0)))

    def from_seq(t):
        if prompt:
            return t.reshape(nb * seq_len, t.shape[-1])
        return jnp.swapaxes(t[:, :valid], 0, 1).reshape(valid * nb, t.shape[-1])

    blk = MIX_BLOCK if prompt else seq_len
    chunks = (GLA_CHUNK, GDN_CHUNK, SSD_CHUNK) if prompt else (seq_len,) * 3
    new = {}

    h = _norm_mod(x, p['g_mix'][0], mods[0], grp, 1, 0)
    proj = to_seq(_matmul(h, q['w_in_ab'], grp, AB_N // 3))
    o_a, new['gla'] = _gla(proj, q['gla_w2'], p['gla_b2'][0], p['gla_norm_g'][0], s_gla, chunks[0],
                            GLA_GROUP if prompt else 1, blk, valid, 1 if prompt else SAMPLE_SEQS)
    o_b, new['gdn'], new['gdnc'] = _gdn(proj, p['gdn_conv_w'][0], p['gdn_conv_b'][0], q['gdn_alog'], q['gdn_dtb'],
                                        p['gdn_norm_g'][0], s_gdn, s_gdnc, chunks[1], blk, valid,
                                        1 if prompt else SAMPLE_SEQS)
    x = _mm_residual([(from_seq(o_a), q['w_out_ab'], 0, 0), (from_seq(o_b), q['w_out_ab'], 0, 1)],
                     x, mods[0], grp, 2, D_MODEL)
    h = _norm_mod(x, p['g_ffn'][0], mods[0], grp, 4, 3)
    act, new['ffn0'] = _ffn_up(h, p['w_ffn_up'], p['ffn_conv_w'], p['ffn_conv_b'], 0, s_ffn[0],
                               grp.retiled(FFN_TILE))
    x = _mm_residual([(act, q['w_ffn_down'], 0, 0)], x, mods[0], grp, 5, 1024)

    h = _norm_mod(x, p['g_mix'][1], mods[1], grp, 1, 0)
    proj2 = _matmul(h, q['w_in_cd'], grp, CD_N // 2)
    proj = to_seq(proj2)
    o_c, new['ssd'], new['ssdc'] = _ssd(proj, p['ssd_conv_w'][0], p['ssd_conv_b'][0], q['ssd_alog'], q['ssd_dtb'],
                                        q['ssd_d_row'], p['ssd_norm_g'][0], s_ssd, s_ssdc, chunks[2], blk, valid,
                                        1 if prompt else SAMPLE_SEQS)
    if prompt:
        pow_re, pow_im = _s5_pow_table(q['s5_lb_re'], q['s5_lb_im'], seq_len // SUBLANE)
        yd, new['re'], new['im'] = _s5_seg(proj, q['s5_w_re'], q['s5_w_im'], q['s5_c_re'], q['s5_c_im'],
                                           pow_re, pow_im, p['s5_D'][0], s_re, s_im)
    else:
        yd, new['re'], new['im'] = _s5(proj2.reshape(1, grp.n_tok, CD_N), q['s5_w_re'], q['s5_w_im'],
                                       q['s5_c_re'], q['s5_c_im'], q['s5_lb_re'], q['s5_lb_im'], p['s5_D'][0],
                                       s_re, s_im, nb, valid)
    o_d = _s5_glu(yd.reshape(grp.n_tok, S5_W), q['s5_glu_w'], p['s5_glu_b'][0], grp)
    x = _mm_residual([(from_seq(o_c), q['w_out_cd'], 0, 0), (o_d, q['w_out_cd'], 0, 1)], x, mods[1], grp, 2,
                     D_MODEL)
    h = _norm_mod(x, p['g_ffn'][1], mods[1], grp, 4, 3)
    act, new['ffn1'] = _ffn_up(h, p['w_ffn_up'], p['ffn_conv_w'], p['ffn_conv_b'], 1, s_ffn[1],
                               grp.retiled(FFN_TILE))
    x = _mm_residual([(act, q['w_ffn_down'], 1, 0)], x, mods[1], grp, 5, 1024)
    return _final_rms(x, p['g_final'], grp), new


def kernel(x_prompt, x_sample, c_prompt, c_sample, state_gla, state_gdn, state_gdn_conv, state_ssd, state_ssd_conv, state_s5_re, state_s5_im, state_ffn_conv, w_ada, b_ada, g_mix, g_ffn, w_in_ab, gla_w2, gla_b2, gla_norm_g, gdn_conv_w, gdn_conv_b, gdn_A_log, gdn_dt_bias, gdn_norm_g, w_out_ab, w_in_cd, ssd_conv_w, ssd_conv_b, ssd_A_log, ssd_dt_bias, ssd_D, ssd_norm_g, s5_A_re, s5_A_im, s5_B_re, s5_B_im, s5_C_re, s5_C_im, s5_D, s5_log_dt, s5_glu_w, s5_glu_b, w_out_cd, w_ffn_up, ffn_conv_w, ffn_conv_b, w_ffn_down, g_final):
    p = dict(g_mix=g_mix, g_ffn=g_ffn, w_in_ab=w_in_ab, gla_w2=gla_w2, gla_b2=gla_b2, gla_norm_g=gla_norm_g,
             gdn_conv_w=gdn_conv_w, gdn_conv_b=gdn_conv_b, gdn_A_log=gdn_A_log, gdn_dt_bias=gdn_dt_bias,
             gdn_norm_g=gdn_norm_g, w_out_ab=w_out_ab, w_in_cd=w_in_cd, ssd_conv_w=ssd_conv_w,
             ssd_conv_b=ssd_conv_b, ssd_A_log=ssd_A_log, ssd_dt_bias=ssd_dt_bias, ssd_D=ssd_D,
             ssd_norm_g=ssd_norm_g, s5_A_re=s5_A_re, s5_A_im=s5_A_im, s5_B_re=s5_B_re, s5_B_im=s5_B_im,
             s5_C_re=s5_C_re, s5_C_im=s5_C_im, s5_D=s5_D, s5_log_dt=s5_log_dt, s5_glu_w=s5_glu_w,
             s5_glu_b=s5_glu_b, w_out_cd=w_out_cd, w_ffn_up=w_ffn_up, ffn_conv_w=ffn_conv_w,
             ffn_conv_b=ffn_conv_b, w_ffn_down=w_ffn_down, g_final=g_final)
    bp, lp, d = x_prompt.shape
    bs, ls, _ = x_sample.shape
    q = _prep_params(p)

    bp_pad = -(-bp // SUBLANE) * SUBLANE
    c_all = jnp.concatenate([c_prompt, jnp.zeros((bp_pad - bp, d), F32), c_sample], axis=0)
    mod = _ada_mod(c_all, w_ada, b_ada)
    depth = w_ada.shape[0]
    mods_p = [mod[l, :bp].reshape(bp, 1, 6 * d) for l in range(depth)]
    mods_s = [jnp.tile(mod[l, bp_pad:], (ls, 1)).reshape(1, ls * bs, 6 * d) for l in range(depth)]

    tile_p = 512
    grp_p = _Group(bp * lp, tile_p, False, lp // tile_p, 1)
    zeros = lambda *shape: jnp.zeros(shape, F32)
    state_p = (zeros(bp, GLA_H, GLA_DK, GLA_DV), zeros(bp, GDN_H, GDN_DK, GDN_DV),
               zeros(bp, CONV_K - 1, GDN_CONV_W), zeros(bp, SSD_H, SSD_P, SSD_N),
               zeros(bp, CONV_K - 1, SSD_CONV_W), zeros(bp, 1, S5_STATE), zeros(bp, 1, S5_STATE),
               zeros(depth, bp, FFN_K - 1, 2 * D_FF))
    y_p, new_p = _trunk(x_prompt.reshape(bp * lp, d), mods_p, grp_p, (bp, lp, lp), state_p, p, q)

    grp_s = _Group(bs * ls, bs * ls, True, 1, bs)
    ffn_hist_s = jnp.swapaxes(state_ffn_conv, 1, 2).reshape(depth, 1, (FFN_K - 1) * bs, 2 * D_FF)
    state_s = (state_gla[0], state_gdn[0], state_gdn_conv[0], state_ssd[0], state_ssd_conv[0],
               state_s5_re.reshape(1, bs, S5_STATE), state_s5_im.reshape(1, bs, S5_STATE), ffn_hist_s)
    x_s = jnp.swapaxes(x_sample, 0, 1).reshape(ls * bs, d)
    y_s, new_s = _trunk(x_s, mods_s, grp_s, (bs, SAMPLE_PAD, ls), state_s, p, q)
    y_s = jnp.swapaxes(y_s.reshape(ls, bs, d), 0, 1)

    ffn_p = jnp.stack([new_p['ffn0'], new_p['ffn1']])
    ffn_s = jnp.stack([jnp.swapaxes(new_s[k].reshape(FFN_K - 1, bs, 2 * D_FF), 0, 1) for k in ('ffn0', 'ffn1')])
    s5_shape = lambda t, nb: t.reshape(1, nb, S5_G, S5_P)
    return (y_p.reshape(bp, lp, d), y_s,
            new_p['gla'][None], new_s['gla'][None], new_p['gdn'][None], new_s['gdn'][None],
            new_p['gdnc'][None], new_s['gdnc'][None], new_p['ssd'][None], new_s['ssd'][None],
            new_p['ssdc'][None], new_s['ssdc'][None],
            s5_shape(new_p['re'], bp), s5_shape(new_s['re'], bs), s5_shape(new_p['im'], bp), s5_shape(new_s['im'], bs),
            ffn_p, ffn_s)
```

```python
import functools
import math

import jax
import jax.numpy as jnp
from jax import lax
from jax.experimental import pallas as pl
from jax.experimental.pallas import tpu as pltpu

F32 = jnp.float32
BF16 = jnp.bfloat16
HIGHEST = lax.Precision.HIGHEST
EPS = 1e-6

D_MODEL = 2048
GLA_H, GLA_DK, GLA_DV, GLA_LR = 4, 128, 256, 16
GLA_GATE_NORM = 16.0
GLA_QK, GLA_V = GLA_H * GLA_DK, GLA_H * GLA_DV
GDN_H, GDN_DK, GDN_DV = 8, 128, 128
GDN_QK, GDN_V = GDN_H * GDN_DK, GDN_H * GDN_DV
CONV_K = 4
GDN_CONV_W = 2 * GDN_QK + GDN_V
SSD_P, SSD_H, SSD_G, SSD_N = 64, 16, 2, 128
SSD_W = SSD_H * SSD_P
SSD_CONV_W = SSD_W + 2 * SSD_G * SSD_N
S5_W, S5_GS, S5_G, S5_P = 1024, 16, 64, 64
S5_STATE = S5_G * S5_P
D_FF = 5632
FFN_K = 3

LANE = 128
SUBLANE = 8
VMEM_LIMIT = 48 * 1024 * 1024

AB_Q, AB_K, AB_V, AB_R, AB_QKV, AB_G, AB_SMALL, AB_N = 0, 512, 1024, 2048, 3072, 6144, 7168, 7296
AB_LR_LANE, AB_A_LANE, AB_B_LANE = 0, 16, 24
CD_XBC, CD_SMALL, CD_Z, CD_U, CD_N = 0, 1536, 2048, 3072, 4096

MIX_BLOCK = 256
GLA_CHUNK, GDN_CHUNK, SSD_CHUNK = 16, 128, 128
GLA_GROUP = 4
FFN_TILE = 1024
FFN_SUB = 256
INV_BLOCK = 16
SAMPLE_PAD = 8
SAMPLE_SEQS = 8


def _cparams(n_axes):
    return pltpu.CompilerParams(dimension_semantics=("arbitrary",) * n_axes, vmem_limit_bytes=VMEM_LIMIT)


def _sigmoid(x):
    return 1.0 / (1.0 + jnp.exp(-x))


def _silu(x):
    return x * _sigmoid(x)


def _softplus(x):
    return jnp.maximum(x, 0.0) + jnp.log(1.0 + jnp.exp(-jnp.abs(x)))


def _gelu_tanh(x):
    return 0.5 * x * (1.0 + jnp.tanh(math.sqrt(2.0 / math.pi) * (x + 0.044715 * (x * x * x))))


def _dot(a, b, precision=None):
    return jnp.dot(a, b, precision=precision, preferred_element_type=F32)


def _dot_nt(a, b, precision=None):
    return lax.dot_general(a, b, (((1,), (1,)), ((), ())), precision=precision, preferred_element_type=F32)


def _dot_tn(a, b, precision=None):
    return lax.dot_general(a, b, (((0,), (0,)), ((), ())), precision=precision, preferred_element_type=F32)


def _head_rms(o, g):
    return o * lax.rsqrt(jnp.mean(o * o, -1, keepdims=True) + EPS) * g


class _Group:
    def __init__(self, n_tok, tile, per_token_mod, tiles_per_seq, conv_shift):
        self.n_tok = n_tok
        self.tile = tile
        self.n_tiles = n_tok // tile
        self.per_token_mod = per_token_mod
        self.tiles_per_seq = tiles_per_seq
        self.conv_shift = conv_shift

    def retiled(self, tile):
        seq_rows = self.tile * self.tiles_per_seq
        if seq_rows % tile or self.n_tok % tile:
            return self
        return _Group(self.n_tok, tile, self.per_token_mod, seq_rows // tile, self.conv_shift)

    def mod_spec(self, width, col_block, m_axis):
        if self.per_token_mod:
            return pl.BlockSpec((1, self.tile, width), lambda *g: (0, g[m_axis], col_block(*g)))
        tps = self.tiles_per_seq
        return pl.BlockSpec((1, 1, width), lambda *g: (g[m_axis] // tps, 0, col_block(*g)))


def _ada_kernel(c_ref, w_ref, b_ref, o_ref):
    cs = _silu(c_ref[...]).astype(BF16)
    o_ref[0] = _dot(cs, w_ref[0].astype(BF16)) + b_ref[0]


def _ada_mod(c, w_ada, b_ada):
    depth, d, n = w_ada.shape
    rows = c.shape[0]
    tn = 1024
    return pl.pallas_call(
        _ada_kernel,
        grid=(depth, n // tn),
        in_specs=[pl.BlockSpec((rows, d), lambda l, j: (0, 0)),
                  pl.BlockSpec((1, d, tn), lambda l, j: (l, 0, j)),
                  pl.BlockSpec((1, 1, tn), lambda l, j: (l, 0, j))],
        out_specs=pl.BlockSpec((1, rows, tn), lambda l, j: (l, 0, j)),
        out_shape=jax.ShapeDtypeStruct((depth, rows, n), F32),
        compiler_params=_cparams(2), name="ada_mod",
    )(c, w_ada, b_ada.reshape(depth, 1, n))


def _norm_mod_kernel(x_ref, g_ref, sc_ref, sh_ref, o_ref):
    x = x_ref[...]
    y = x * lax.rsqrt(jnp.mean(x * x, -1, keepdims=True) + EPS) * g_ref[...]
    o_ref[...] = (y * (1.0 + sc_ref[0]) + sh_ref[0]).astype(BF16)


def _norm_mod(x, g, mod, grp, sc_blk, sh_blk):
    d = x.shape[1]
    return pl.pallas_call(
        _norm_mod_kernel,
        grid=(grp.n_tiles,),
        in_specs=[pl.BlockSpec((grp.tile, d), lambda i: (i, 0)),
                  pl.BlockSpec((1, d), lambda i: (0, 0)),
                  grp.mod_spec(d, lambda i: sc_blk, 0),
                  grp.mod_spec(d, lambda i: sh_blk, 0)],
        out_specs=pl.BlockSpec((grp.tile, d), lambda i: (i, 0)),
        out_shape=jax.ShapeDtypeStruct(x.shape, BF16),
        compiler_params=_cparams(1), name="norm_mod",
    )(x, g.reshape(1, d), mod, mod)


def _rms_kernel(x_ref, g_ref, o_ref):
    x = x_ref[...]
    o_ref[...] = x * lax.rsqrt(jnp.mean(x * x, -1, keepdims=True) + EPS) * g_ref[...]


def _final_rms(x, g, grp):
    d = x.shape[1]
    return pl.pallas_call(
        _rms_kernel,
        grid=(grp.n_tiles,),
        in_specs=[pl.BlockSpec((grp.tile, d), lambda i: (i, 0)), pl.BlockSpec((1, d), lambda i: (0, 0))],
        out_specs=pl.BlockSpec((grp.tile, d), lambda i: (i, 0)),
        out_shape=jax.ShapeDtypeStruct(x.shape, F32),
        compiler_params=_cparams(1), name="final_rms",
    )(x, g.reshape(1, d))


def _matmul_kernel(a_ref, w_ref, o_ref):
    o_ref[...] = _dot(a_ref[...], w_ref[...])


def _matmul(a, w, grp, tn):
    k, n = w.shape
    return pl.pallas_call(
        _matmul_kernel,
        grid=(n // tn, grp.n_tiles),
        in_specs=[pl.BlockSpec((grp.tile, k), lambda j, i: (i, 0)),
                  pl.BlockSpec((k, tn), lambda j, i: (0, j))],
        out_specs=pl.BlockSpec((grp.tile, tn), lambda j, i: (i, j)),
        out_shape=jax.ShapeDtypeStruct((a.shape[0], n), F32),
        compiler_params=_cparams(2), name="matmul",
    )(a, w)


def _mm_res_kernel(*refs, n_pairs, with_norm):
    x_ref, gate_ref = refs[2 * n_pairs:2 * n_pairs + 2]
    y = _dot(refs[0][...].astype(BF16), refs[1][0])
    for p in range(1, n_pairs):
        y = y + _dot(refs[2 * p][...].astype(BF16), refs[2 * p + 1][0])
    x_new = x_ref[...] + gate_ref[0] * y
    if not with_norm:
        o_ref, = refs[2 * n_pairs + 2:]
        o_ref[...] = x_new
        return
    g_ref, sc_ref, sh_ref, o_ref, h_ref = refs[2 * n_pairs + 2:]
    o_ref[...] = x_new
    normed = x_new * lax.rsqrt(jnp.mean(x_new * x_new, -1, keepdims=True) + EPS) * g_ref[...]
    h_ref[...] = (normed * (1.0 + sc_ref[0]) + sh_ref[0]).astype(BF16)


def _mm_residual(pairs, x, mod, grp, gate_blk, tn, norm=None):
    n = x.shape[1]
    if norm is not None:
        assert tn == n
        return _mm_residual_norm(pairs, x, mod, grp, gate_blk, norm)
    in_specs, args = [], []
    for a, w, layer, row_blk in pairs:
        k = a.shape[1]
        in_specs += [pl.BlockSpec((grp.tile, k), lambda j, i: (i, 0)),
                     pl.BlockSpec((1, k, tn), lambda j, i, layer=layer, row_blk=row_blk: (layer, row_blk, j))]
        args += [a, w]
    in_specs += [pl.BlockSpec((grp.tile, tn), lambda j, i: (i, j)),
                 grp.mod_spec(tn, lambda j, i: gate_blk * (n // tn) + j, 1)]
    return pl.pallas_call(
        functools.partial(_mm_res_kernel, n_pairs=len(pairs), with_norm=False),
        grid=(n // tn, grp.n_tiles),
        in_specs=in_specs,
        out_specs=pl.BlockSpec((grp.tile, tn), lambda j, i: (i, j)),
        out_shape=jax.ShapeDtypeStruct(x.shape, F32),
        compiler_params=_cparams(2), name="mm_residual",
    )(*args, x, mod)


def _mm_residual_norm(pairs, x, mod, grp, gate_blk, norm):
    n = x.shape[1]
    gain, sc_blk, sh_blk = norm
    in_specs, args = [], []
    for a, w, layer, row_blk in pairs:
        k = a.shape[1]
        in_specs += [pl.BlockSpec((grp.tile, k), lambda j, i: (i, 0)),
                     pl.BlockSpec((1, k, n), lambda j, i, layer=layer, row_blk=row_blk: (layer, row_blk, 0))]
        args += [a, w]
    row_spec = pl.BlockSpec((grp.tile, n), lambda j, i: (i, 0))
    in_specs += [row_spec, grp.mod_spec(n, lambda j, i: gate_blk, 1), pl.BlockSpec((1, n), lambda j, i: (0, 0)),
                 grp.mod_spec(n, lambda j, i: sc_blk, 1), grp.mod_spec(n, lambda j, i: sh_blk, 1)]
    return pl.pallas_call(
        functools.partial(_mm_res_kernel, n_pairs=len(pairs), with_norm=True),
        grid=(1, grp.n_tiles),
        in_specs=in_specs,
        out_specs=[row_spec, row_spec],
        out_shape=[jax.ShapeDtypeStruct(x.shape, F32), jax.ShapeDtypeStruct(x.shape, BF16)],
        compiler_params=_cparams(2), name="mm_residual_norm",
    )(*args, x, mod, gain.reshape(1, n), mod, mod)


def _ffn_up_kernel(h_ref, wa_ref, wg_ref, cwa_ref, cwg_ref, cba_ref, cbg_ref, ha_ref, hg_ref,
                   act_ref, sta_ref, stg_ref, scr_a, scr_g, wba_scr, wbg_scr, *, shift, tile, sub, tiles_per_seq):
    i = pl.program_id(1)
    hist = (FFN_K - 1) * shift
    base = -(-hist // SUBLANE) * SUBLANE

    @pl.when(i == 0)
    def _():
        wba_scr[...] = wa_ref[0].astype(BF16)
        wbg_scr[...] = wg_ref[0].astype(BF16)

    @pl.when(i % tiles_per_seq == 0)
    def _():
        scr_a[base - hist:base, :] = ha_ref[0]
        scr_g[base - hist:base, :] = hg_ref[0]

    def conv(scr, cw_ref, cb_ref, r0):
        y = cb_ref[0]
        for j in range(FFN_K):
            lo = base + r0 - (FFN_K - 1 - j) * shift
            y = y + scr[lo:lo + sub, :] * cw_ref[0, j:j + 1, :]
        return y

    def project(r0):
        h = h_ref[r0:r0 + sub, :]
        scr_a[base + r0:base + r0 + sub, :] = _dot(h, wba_scr[...])
        scr_g[base + r0:base + r0 + sub, :] = _dot(h, wbg_scr[...])

    project(0)
    for r0 in range(0, tile, sub):
        if r0 + sub < tile:
            project(r0 + sub)
        a = conv(scr_a, cwa_ref, cba_ref, r0)
        g = conv(scr_g, cwg_ref, cbg_ref, r0)
        act_ref[r0:r0 + sub, :] = (_silu(g) * a).astype(BF16)
    last_a = scr_a[base + tile - hist:base + tile, :]
    last_g = scr_g[base + tile - hist:base + tile, :]
    sta_ref[0] = last_a
    stg_ref[0] = last_g
    scr_a[base - hist:base, :] = last_a
    scr_g[base - hist:base, :] = last_g


def _ffn_up(h, w_up, conv_w, conv_b, layer, hist0, grp):
    d = h.shape[1]
    tn = 512
    nj = D_FF // tn
    shift = grp.conv_shift
    hist = (FFN_K - 1) * shift
    base = -(-hist // SUBLANE) * SUBLANE
    n_seq = grp.n_tiles // grp.tiles_per_seq
    tps = grp.tiles_per_seq
    cb = conv_b.reshape(conv_b.shape[0], 1, 2 * D_FF)
    kern = functools.partial(_ffn_up_kernel, shift=shift, tile=grp.tile, sub=min(FFN_SUB, grp.tile),
                             tiles_per_seq=tps)
    assert grp.tile % min(FFN_SUB, grp.tile) == 0
    act, st_a, st_g = pl.pallas_call(
        kern,
        grid=(nj, grp.n_tiles),
        in_specs=[pl.BlockSpec((grp.tile, d), lambda j, i: (i, 0)),
                  pl.BlockSpec((1, d, tn), lambda j, i: (layer, 0, j)),
                  pl.BlockSpec((1, d, tn), lambda j, i: (layer, 0, nj + j)),
                  pl.BlockSpec((1, FFN_K, tn), lambda j, i: (layer, 0, j)),
                  pl.BlockSpec((1, FFN_K, tn), lambda j, i: (layer, 0, nj + j)),
                  pl.BlockSpec((1, 1, tn), lambda j, i: (layer, 0, j)),
                  pl.BlockSpec((1, 1, tn), lambda j, i: (layer, 0, nj + j)),
                  pl.BlockSpec((1, hist, tn), lambda j, i: (i // tps, 0, j)),
                  pl.BlockSpec((1, hist, tn), lambda j, i: (i // tps, 0, nj + j))],
        out_specs=[pl.BlockSpec((grp.tile, tn), lambda j, i: (i, j)),
                   pl.BlockSpec((1, hist, tn), lambda j, i: (i // tps, 0, j)),
                   pl.BlockSpec((1, hist, tn), lambda j, i: (i // tps, 0, j))],
        out_shape=[jax.ShapeDtypeStruct((h.shape[0], D_FF), BF16),
                   jax.ShapeDtypeStruct((n_seq, hist, D_FF), F32),
                   jax.ShapeDtypeStruct((n_seq, hist, D_FF), F32)],
        scratch_shapes=[pltpu.VMEM((base + grp.tile, tn), F32), pltpu.VMEM((base + grp.tile, tn), F32),
                        pltpu.VMEM((d, tn), BF16), pltpu.VMEM((d, tn), BF16)],
        compiler_params=_cparams(2), name="ffn_up",
    )(h, w_up, w_up, conv_w, conv_w, cb, cb, hist0, hist0)
    return act, jnp.concatenate([st_a, st_g], axis=-1)


def _glu_kernel(y_ref, w_ref, b_ref, o_ref):
    z5 = _gelu_tanh(y_ref[...])
    o_ref[...] = (z5 * _sigmoid(_dot(z5.astype(BF16), w_ref[...]) + b_ref[...])).astype(o_ref.dtype)


def _s5_glu(yd, w, b, grp):
    n = yd.shape[1]
    return pl.pallas_call(
        _glu_kernel,
        grid=(grp.n_tiles,),
        in_specs=[pl.BlockSpec((grp.tile, n), lambda i: (i, 0)),
                  pl.BlockSpec((n, n), lambda i: (0, 0)),
                  pl.BlockSpec((1, n), lambda i: (0, 0))],
        out_specs=pl.BlockSpec((grp.tile, n), lambda i: (i, 0)),
        out_shape=jax.ShapeDtypeStruct(yd.shape, BF16),
        compiler_params=_cparams(1), name="s5_glu",
    )(yd, w, b.reshape(1, n))


def _round_robin(gens):
    gens = list(gens)
    while gens:
        alive = []
        for gen in gens:
            try:
                next(gen)
                alive.append(gen)
            except StopIteration:
                pass
        gens = alive


def _mixer_out_dtype(block):
    return BF16 if block % (2 * SUBLANE) == 0 else F32


def _causal_conv_chunk(x, cv_scr, cw_ref, cb_ref, chunk, valid):
    base = SUBLANE
    cv_scr[base:base + chunk, :] = x
    y = cb_ref[...]
    for j in range(CONV_K):
        lo = base - (CONV_K - 1) + j
        y = y + cv_scr[lo:lo + chunk, :] * cw_ref[j:j + 1, :]
    last = cv_scr[base + valid - (CONV_K - 1):base + valid, :]
    cv_scr[base - (CONV_K - 1):base, :] = last
    return y


def _gla_kernel(q_ref, k_ref, v_ref, r_ref, sm_ref, w2_ref, b2_ref, ng_ref, s0_ref,
                o_ref, sout_ref, s_scr, b_scr, *, chunk, group, block, valid, nseq):
    blk = pl.program_id(1)

    @pl.when(blk == 0)
    def _():
        s_scr[...] = s0_ref[...]

    span = chunk * group
    lane = lax.broadcasted_iota(jnp.int32, (1, GLA_DK), 1)
    ones_kk = jnp.ones((GLA_DK, GLA_DK), BF16)
    row_in_chunk = lax.broadcasted_iota(jnp.int32, (span, 1), 0) % chunk
    eye = (lax.broadcasted_iota(jnp.int32, (GLA_DK, GLA_DK), 0)
           == lax.broadcasted_iota(jnp.int32, (GLA_DK, GLA_DK), 1))
    n_valid = min(valid, chunk)
    heads = range(GLA_H)
    chunks = range(group)

    def to3(t):
        return t.reshape(group, chunk, t.shape[-1])

    ri = lax.broadcasted_iota(jnp.int32, (block, block), 0)
    ci = lax.broadcasted_iota(jnp.int32, (block, block), 1)
    tri = ((ri // chunk == ci // chunk) & (ri >= ci)).astype(BF16)
    row_blk = lax.broadcasted_iota(jnp.int32, (block, 1), 0) % chunk
    for sq in range(nseq):
        x = _dot(sm_ref[sq].astype(BF16), w2_ref[...].astype(BF16)) + b2_ref[...]
        log_a = (jnp.minimum(x, 0.0) - jnp.log(1.0 + jnp.exp(-jnp.abs(x)))) * (1.0 / GLA_GATE_NORM)
        if valid < chunk:
            log_a = jnp.where(row_blk < valid, log_a, 0.0)
        b_scr[sq] = sum(_dot(tri, part) for part in _split3(log_a))

    def one_seq(sq, rows):
        b = b_scr[sq, rows, :]
        q = q_ref[sq, rows, :] * GLA_DK ** -0.5
        k = k_ref[sq, rows, :]
        v = v_ref[sq, rows, :]
        r = r_ref[sq, rows, :]
        b3, q_in, kv, d_col = [], [], [], []
        for h in heads:
            ks = slice(h * GLA_DK, (h + 1) * GLA_DK)
            bh3 = to3(b[:, ks])
            b_last = bh3[:, chunk - 1:chunk, :]
            k_out = (to3(k[:, ks]) * jnp.exp(b_last - bh3)).reshape(span, GLA_DK)
            vh = v[:, h * GLA_DV:(h + 1) * GLA_DV]
            b3.append(bh3)
            q_in.append(q[:, ks] * jnp.exp(b[:, ks]))
            kv.append([_dot_tn(k_out[c * chunk:(c + 1) * chunk], vh[c * chunk:(c + 1) * chunk]) for c in chunks])
            d_col.append([jnp.sum(jnp.where(eye, jnp.exp(b_last[c]), 0.0), axis=-1, keepdims=True) for c in chunks])
        yield
        st = [s_scr[sq, h] for h in heads]
        o_inter = [[] for _ in heads]
        for c in chunks:
            for h in heads:
                o_inter[h].append(_dot(q_in[h][c * chunk:(c + 1) * chunk], st[h]))
                st[h] = st[h] * d_col[h][c] + kv[h][c]
            yield
        for h in heads:
            ks = slice(h * GLA_DK, (h + 1) * GLA_DK)
            vs = slice(h * GLA_DV, (h + 1) * GLA_DV)
            s_scr[sq, h] = st[h]
            bh3, qh3, kh3, vh3 = b3[h], to3(q[:, ks]), to3(k[:, ks]), to3(v[:, vs])
            prods = []
            for j in range(n_valid):
                e = jnp.exp(bh3 - bh3[:, j:j + 1, :])
                prods.append((qh3 * e * kh3[:, j:j + 1, :]).reshape(span, GLA_DK).astype(BF16))
            sums = _dot(jnp.concatenate(prods, axis=0), ones_kk)
            yield
            att = jnp.zeros((span, GLA_DK), F32)
            for j in range(n_valid):
                att = jnp.where(lane == j, sums[j * span:(j + 1) * span], att)
            att = jnp.where(row_in_chunk >= lane, att, 0.0)[:, :chunk]
            vh = v[:, vs]
            o_intra = [_dot(att[c * chunk:(c + 1) * chunk], vh[c * chunk:(c + 1) * chunk]) for c in chunks]
            o = jnp.concatenate(o_intra, axis=0) + jnp.concatenate(o_inter[h], axis=0)
            o_ref[sq, rows, vs] = (_head_rms(o, ng_ref[...]) * _silu(r[:, vs])).astype(o_ref.dtype)
            yield

    def do_span(s, carry):
        rows = pl.ds(pl.multiple_of(s * span, span), span)
        _round_robin(one_seq(sq, rows) for sq in range(nseq))
        return carry

    lax.fori_loop(0, block // span, do_span, 0)

    @pl.when(blk == pl.num_programs(1) - 1)
    def _():
        sout_ref[...] = s_scr[...]


def _gla(proj, w2p, b2, ng, s0, chunk, group, block, valid, nseq):
    nb, seq, _ = proj.shape

    def col(width, off):
        return pl.BlockSpec((nseq, block, width), lambda b, i: (b, i, off // width))

    st_spec = pl.BlockSpec((nseq, GLA_H, GLA_DK, GLA_DV), lambda b, i: (b, 0, 0, 0))
    kern = functools.partial(_gla_kernel, chunk=chunk, group=group, block=block, valid=valid, nseq=nseq)
    return pl.pallas_call(
        kern,
        grid=(nb // nseq, seq // block),
        in_specs=[col(GLA_QK, AB_Q), col(GLA_QK, AB_K), col(GLA_V, AB_V), col(GLA_V, AB_R), col(LANE, AB_SMALL),
                  pl.BlockSpec((LANE, GLA_QK), lambda b, i: (0, 0)),
                  pl.BlockSpec((1, GLA_QK), lambda b, i: (0, 0)),
                  pl.BlockSpec((1, GLA_DV), lambda b, i: (0, 0)),
                  st_spec],
        out_specs=[pl.BlockSpec((nseq, block, GLA_V), lambda b, i: (b, i, 0)), st_spec],
        out_shape=[jax.ShapeDtypeStruct((nb, seq, GLA_V), _mixer_out_dtype(block)),
                   jax.ShapeDtypeStruct((nb, GLA_H, GLA_DK, GLA_DV), F32)],
        scratch_shapes=[pltpu.VMEM((nseq, GLA_H, GLA_DK, GLA_DV), F32), pltpu.VMEM((nseq, block, GLA_QK), F32)],
        compiler_params=_cparams(2), name="gla",
    )(proj, proj, proj, proj, proj, w2p, b2.reshape(1, GLA_QK), ng.reshape(1, GLA_DV), s0)


def _split2(a):
    hi = a.astype(BF16)
    return hi, (a - hi.astype(F32)).astype(BF16)


def _split3(a):
    hi = a.astype(BF16)
    rest = a - hi.astype(F32)
    mid = rest.astype(BF16)
    return hi, mid, (rest - mid.astype(F32)).astype(BF16)


def _dot3(a, b):
    return _dot(a[0], b[0]) + _dot(a[0], b[1]) + _dot(a[1], b[0])


def _inv_unit_lower_many(mats, n, eye):
    ps = [eye - a for a in mats]
    if n <= 2:
        return ps
    pows = [_split2(a) for a in mats]
    k = 2
    pending = None
    while k < n:
        sq = [_dot3(a, a) for a in pows]
        if pending is not None:
            ps = [p + _dot3(_split2(p), f) for p, f in zip(ps, pending)]
        pows = [_split2(a) for a in sq]
        pending = pows
        k *= 2
    return [p + _dot3(_split2(p), f) for p, f in zip(ps, pending)]


def _gdn_kernel(qkv_ref, sm_ref, gb_ref, cw_ref, cb_ref, alog_ref, dtb_ref, ng_ref, s0_ref, c0_ref,
                o_ref, sout_ref, cout_ref, s_scr, cv_scr, *, chunk, block, valid, nseq):
    blk = pl.program_id(1)

    @pl.when(blk == 0)
    def _():
        s_scr[...] = s0_ref[...]
        cv_scr[:, SUBLANE - (CONV_K - 1):SUBLANE, :] = c0_ref[...]

    ri = lax.broadcasted_iota(jnp.int32, (chunk, chunk), 0)
    ci = lax.broadcasted_iota(jnp.int32, (chunk, chunk), 1)
    causal = ri >= ci
    strict = ri > ci
    eye = (ri == ci).astype(F32)
    tri = causal.astype(F32)
    tri_u = (ri <= ci).astype(F32)
    inv_blk = min(INV_BLOCK, chunk)
    same_blk = (ri // inv_blk) == (ci // inv_blk)
    row = lax.broadcasted_iota(jnp.int32, (chunk, 1), 0)
    n_valid = min(valid, chunk)

    units = [(sq, h) for sq in range(nseq) for h in range(GDN_H)]
    heads = range(len(units))

    def prelude(rows):
        q, k, kb, rhs, dec, gcc = [], [], [], [], [], []
        for sq, h in units:
            if h == 0:
                act = _silu(_causal_conv_chunk(qkv_ref[sq, rows, :], cv_scr.at[sq], cw_ref, cb_ref, chunk, n_valid))
                sm = sm_ref[sq, rows, :]
                g_all = -jnp.exp(alog_ref[...]) * _softplus(sm + dtb_ref[...])
                beta_all = _sigmoid(sm)
                if valid < chunk:
                    g_all = jnp.where(row < valid, g_all, 0.0)
                    beta_all = jnp.where(row < valid, beta_all, 0.0)
                gc = _dot(tri, g_all, HIGHEST)
                gc_r = _dot_tn(g_all, tri_u, HIGHEST)
            qh = act[:, h * GDN_DK:(h + 1) * GDN_DK]
            kh = act[:, GDN_QK + h * GDN_DK:GDN_QK + (h + 1) * GDN_DK]
            vh = act[:, 2 * GDN_QK + h * GDN_DV:2 * GDN_QK + (h + 1) * GDN_DV]
            qh = qh * lax.rsqrt(jnp.sum(qh * qh, -1, keepdims=True) + EPS) * GDN_DK ** -0.5
            kh = kh * lax.rsqrt(jnp.sum(kh * kh, -1, keepdims=True) + EPS)
            beta = beta_all[:, AB_B_LANE + h:AB_B_LANE + h + 1]
            gch = gc[:, AB_A_LANE + h:AB_A_LANE + h + 1]
            gcr = gc_r[AB_A_LANE + h:AB_A_LANE + h + 1, :]
            q.append(qh)
            k.append(kh)
            kb.append(kh * beta)
            rhs.append(_split2(jnp.concatenate([vh * beta, kb[-1] * jnp.exp(gch)], axis=1)))
            dec.append(jnp.exp(jnp.where(causal, gch - gcr, -jnp.inf)))
            gcc.append(gch)
        kbf = [a.astype(BF16) for a in k]
        m = [jnp.where(strict, _dot_nt(kb[h].astype(BF16), kbf[h]) * dec[h], 0.0) for h in heads]
        att = [_dot_nt(q[h].astype(BF16), kbf[h]) * dec[h] for h in heads]
        return q, k, rhs, gcc, m, att

    def finish(pre, rows):
        q, k, rhs, gcc, m, att = pre
        m_diag = [jnp.where(same_blk, a, 0.0) for a in m]
        t = [_split2(a) for a in _inv_unit_lower_many(m_diag, inv_blk, eye)]
        y = [_dot3(t[h], rhs[h]) for h in heads]
        if chunk > inv_blk:
            n_off = [_dot3(t[h], _split2(m[h] - m_diag[h])) for h in heads]
            qn = [_split2(a) for a in _inv_unit_lower_many(n_off, chunk // inv_blk, eye)]
            y = [_dot3(qn[h], _split2(y[h])) for h in heads]
        st = [s_scr[sq, h] for sq, h in units]
        stb = [a.astype(BF16) for a in st]
        v_new = [y[h][:, :GDN_DV] - _dot(y[h][:, GDN_DV:].astype(BF16), stb[h]) for h in heads]
        vnb = [a.astype(BF16) for a in v_new]
        o = [_dot((q[h] * jnp.exp(gcc[h])).astype(BF16), stb[h]) + _dot(att[h].astype(BF16), vnb[h]) for h in heads]
        for u, (sq, h) in enumerate(units):
            hs = slice(h * GDN_DV, (h + 1) * GDN_DV)
            g_last = gcc[u][chunk - 1:chunk, :]
            k_out = (k[u] * jnp.exp(g_last - gcc[u])).astype(BF16)
            s_scr[sq, h] = st[u] * jnp.exp(g_last) + _dot_tn(k_out, vnb[u])
            o_ref[sq, rows, hs] = (_head_rms(o[u], ng_ref[...]) * _silu(gb_ref[sq, rows, hs])).astype(o_ref.dtype)

    n_chunks = block // chunk
    pre = prelude(pl.ds(0, chunk))
    for c in range(n_chunks):
        nxt = prelude(pl.ds((c + 1) * chunk, chunk)) if c + 1 < n_chunks else None
        finish(pre, pl.ds(c * chunk, chunk))
        pre = nxt

    @pl.when(blk == pl.num_programs(1) - 1)
    def _():
        sout_ref[...] = s_scr[...]
        cout_ref[...] = cv_scr[:, SUBLANE - (CONV_K - 1):SUBLANE, :]


def _gdn(proj, conv_w, conv_b, alog_row, dtb_row, ng, s0, c0, chunk, block, valid, nseq):
    nb, seq, _ = proj.shape

    def col(width, off):
        return pl.BlockSpec((nseq, block, width), lambda b, i: (b, i, off // width))

    def const(shape):
        return pl.BlockSpec(shape, lambda b, i: (0,) * len(shape))

    st_spec = pl.BlockSpec((nseq, GDN_H, GDN_DK, GDN_DV), lambda b, i: (b, 0, 0, 0))
    cv_spec = pl.BlockSpec((nseq, CONV_K - 1, GDN_CONV_W), lambda b, i: (b, 0, 0))
    kern = functools.partial(_gdn_kernel, chunk=chunk, block=block, valid=valid, nseq=nseq)
    return pl.pallas_call(
        kern,
        grid=(nb // nseq, seq // block),
        in_specs=[col(GDN_CONV_W, AB_QKV), col(LANE, AB_SMALL), col(GDN_V, AB_G),
                  const((CONV_K, GDN_CONV_W)), const((1, GDN_CONV_W)), const((1, LANE)), const((1, LANE)),
                  const((1, GDN_DV)), st_spec, cv_spec],
        out_specs=[pl.BlockSpec((nseq, block, GDN_V), lambda b, i: (b, i, 0)), st_spec, cv_spec],
        out_shape=[jax.ShapeDtypeStruct((nb, seq, GDN_V), _mixer_out_dtype(block)),
                   jax.ShapeDtypeStruct((nb, GDN_H, GDN_DK, GDN_DV), F32),
                   jax.ShapeDtypeStruct((nb, CONV_K - 1, GDN_CONV_W), F32)],
        scratch_shapes=[pltpu.VMEM((nseq, GDN_H, GDN_DK, GDN_DV), F32),
                        pltpu.VMEM((nseq, SUBLANE + chunk, GDN_CONV_W), F32)],
        compiler_params=_cparams(2), name="gdn",
    )(proj, proj, proj, conv_w, conv_b.reshape(1, GDN_CONV_W), alog_row, dtb_row, ng.reshape(1, GDN_DV), s0, c0)


def _ssd_kernel(xbc_ref, sm_ref, z_ref, cw_ref, cb_ref, alog_ref, dtb_ref, drow_ref, ng_ref, s0_ref, c0_ref,
                o_ref, sout_ref, cout_ref, s_scr, cv_scr, dt_scr, acs_scr, acsr_scr, dtr_scr,
                *, chunk, block, valid, nseq):
    blk = pl.program_id(1)

    @pl.when(blk == 0)
    def _():
        s_scr[...] = s0_ref[...]
        cv_scr[:, SUBLANE - (CONV_K - 1):SUBLANE, :] = c0_ref[...]

    ri = lax.broadcasted_iota(jnp.int32, (chunk, chunk), 0)
    ci = lax.broadcasted_iota(jnp.int32, (chunk, chunk), 1)
    causal = ri >= ci
    eye = (ri == ci).astype(BF16)
    tri = causal.astype(BF16)
    tri_u = (ri <= ci).astype(BF16)
    row = lax.broadcasted_iota(jnp.int32, (chunk, 1), 0)
    lane_lo = lax.broadcasted_iota(jnp.int32, (chunk, LANE), 1) < SSD_P
    row_lo = lax.broadcasted_iota(jnp.int32, (2 * SSD_P, 1), 0) < SSD_P
    n_valid = min(valid, chunk)
    heads_per_group = SSD_H // SSD_G
    gsz = SSD_W // SSD_G

    for sq in range(nseq):
        for c in range(block // chunk):
            dt = _softplus(sm_ref[sq, c * chunk:(c + 1) * chunk, :] + dtb_ref[...])
            if valid < chunk:
                dt = jnp.where(row < valid, dt, 0.0)
            dta = _split3(dt * (-jnp.exp(alog_ref[...])))
            dt_scr[sq, c] = dt
            acs_scr[sq, c] = sum(_dot(tri, part) for part in dta)
            acsr_scr[sq, c] = sum(_dot_tn(part, tri_u) for part in dta)
            dtr_scr[sq, c] = sum(_dot_tn(part, eye) for part in _split3(dt))

    def one_seq(sq, s, rows):
        act = _silu(_causal_conv_chunk(xbc_ref[sq, rows, :], cv_scr.at[sq], cw_ref, cb_ref, chunk, n_valid))
        dt, acs, acs_r, dt_r = dt_scr[sq, s], acs_scr[sq, s], acsr_scr[sq, s], dtr_scr[sq, s]
        z = z_ref[sq, rows, :]
        for g in range(SSD_G):
            bg = act[:, SSD_W + g * SSD_N:SSD_W + (g + 1) * SSD_N]
            cg = act[:, SSD_W + SSD_G * SSD_N + g * SSD_N:SSD_W + SSD_G * SSD_N + (g + 1) * SSD_N]
            cb = _dot_nt(cg, bg)
            yield
            parts = []
            for pr in range(heads_per_group // 2):
                pi = g * (heads_per_group // 2) + pr
                xp = act[:, pi * LANE:(pi + 1) * LANE]
                st = s_scr[sq, pi]
                y_in, e_in, w_out, d_last = [], [], [], []
                for hh in range(2):
                    h = 2 * pi + hh
                    ac = acs[:, h:h + 1]
                    dec = jnp.exp(jnp.where(causal, ac - acs_r[h:h + 1, :], -jnp.inf))
                    y_in.append(_dot(cb * dec * dt_r[h:h + 1, :], xp))
                    a_last = ac[chunk - 1:chunk, :]
                    e_in.append(jnp.exp(ac))
                    w_out.append(jnp.exp(a_last - ac) * dt[:, h:h + 1])
                    d_last.append(jnp.exp(a_last))
                y = jnp.where(lane_lo, y_in[0], y_in[1])
                y = y + _dot_nt(cg, st) * jnp.where(lane_lo, e_in[0], e_in[1])
                y = y + drow_ref[:, pi * LANE:(pi + 1) * LANE] * xp
                x_sc = xp * jnp.where(lane_lo, w_out[0], w_out[1])
                s_scr[sq, pi] = st * jnp.where(row_lo, d_last[0], d_last[1]) + _dot_tn(x_sc, bg)
                parts.append(y)
                yield
            gs = slice(g * gsz, (g + 1) * gsz)
            yg = jnp.concatenate(parts, axis=1) * _silu(z[:, gs])
            o_ref[sq, rows, gs] = _head_rms(yg, ng_ref[:, gs]).astype(o_ref.dtype)

    def do_chunk(s, carry):
        rows = pl.ds(pl.multiple_of(s * chunk, chunk), chunk)
        _round_robin(one_seq(sq, s, rows) for sq in range(nseq))
        return carry

    lax.fori_loop(0, block // chunk, do_chunk, 0)

    @pl.when(blk == pl.num_programs(1) - 1)
    def _():
        sout_ref[...] = s_scr[...]
        cout_ref[...] = cv_scr[:, SUBLANE - (CONV_K - 1):SUBLANE, :]


def _ssd(proj, conv_w, conv_b, alog_row, dtb_row, d_row, ng, s0, c0, chunk, block, valid, nseq):
    nb, seq, _ = proj.shape
    n_pairs = SSD_H // 2

    def col(width, off):
        return pl.BlockSpec((nseq, block, width), lambda b, i: (b, i, off // width))

    def const(shape):
        return pl.BlockSpec(shape, lambda b, i: (0,) * len(shape))

    st_spec = pl.BlockSpec((nseq, n_pairs, 2 * SSD_P, SSD_N), lambda b, i: (b, 0, 0, 0))
    cv_spec = pl.BlockSpec((nseq, CONV_K - 1, SSD_CONV_W), lambda b, i: (b, 0, 0))
    kern = functools.partial(_ssd_kernel, chunk=chunk, block=block, valid=valid, nseq=nseq)
    o, s_new, c_new = pl.pallas_call(
        kern,
        grid=(nb // nseq, seq // block),
        in_specs=[col(SSD_CONV_W, CD_XBC), col(LANE, CD_SMALL), col(SSD_W, CD_Z),
                  const((CONV_K, SSD_CONV_W)), const((1, SSD_CONV_W)), const((1, LANE)), const((1, LANE)),
                  const((1, SSD_W)), const((1, SSD_W)), st_spec, cv_spec],
        out_specs=[pl.BlockSpec((nseq, block, SSD_W), lambda b, i: (b, i, 0)), st_spec, cv_spec],
        out_shape=[jax.ShapeDtypeStruct((nb, seq, SSD_W), _mixer_out_dtype(block)),
                   jax.ShapeDtypeStruct((nb, n_pairs, 2 * SSD_P, SSD_N), F32),
                   jax.ShapeDtypeStruct((nb, CONV_K - 1, SSD_CONV_W), F32)],
        scratch_shapes=[pltpu.VMEM((nseq, n_pairs, 2 * SSD_P, SSD_N), F32),
                        pltpu.VMEM((nseq, SUBLANE + chunk, SSD_CONV_W), F32),
                        pltpu.VMEM((nseq, block // chunk, chunk, LANE), F32),
                        pltpu.VMEM((nseq, block // chunk, chunk, LANE), F32),
                        pltpu.VMEM((nseq, block // chunk, LANE, chunk), F32),
                        pltpu.VMEM((nseq, block // chunk, LANE, chunk), F32)],
        compiler_params=_cparams(2), name="ssd",
    )(proj, proj, proj, conv_w, conv_b.reshape(1, SSD_CONV_W), alog_row, dtb_row, d_row,
      ng.reshape(1, SSD_W), s0.reshape(nb, n_pairs, 2 * SSD_P, SSD_N), c0)
    return o, s_new.reshape(nb, SSD_H, SSD_P, SSD_N), c_new


def _s5_prep_kernel(are_ref, aim_ref, ldt_ref, bre_ref, bim_ref, lbr_ref, lbi_ref, bbr_ref, bbi_ref):
    a_re, a_im = are_ref[...], aim_ref[...]
    dt = jnp.exp(ldt_ref[...])
    mag = jnp.exp(a_re * dt)
    lb_re, lb_im = mag * jnp.cos(a_im * dt), mag * jnp.sin(a_im * dt)
    nr, ni = lb_re - 1.0, lb_im
    den = a_re * a_re + a_im * a_im
    f_re = (nr * a_re + ni * a_im) / den
    f_im = (ni * a_re - nr * a_im) / den
    b_re, b_im = bre_ref[...], bim_ref[...]
    lbr_ref[...] = lb_re
    lbi_ref[...] = lb_im
    bbr_ref[...] = f_re * b_re - f_im * b_im
    bbi_ref[...] = f_re * b_im + f_im * b_re


def _s5_prep(a_re, a_im, log_dt, b_re, b_im):
    g3 = (S5_G, 1, S5_P)
    b3 = (S5_G, S5_GS, S5_P)
    return pl.pallas_call(
        _s5_prep_kernel,
        out_shape=[jax.ShapeDtypeStruct(g3, F32), jax.ShapeDtypeStruct(g3, F32),
                   jax.ShapeDtypeStruct(b3, F32), jax.ShapeDtypeStruct(b3, F32)],
        name="s5_prep",
    )(a_re.reshape(g3), a_im.reshape(g3), log_dt.reshape(S5_G, 1, 1),
      jnp.swapaxes(b_re, 1, 2), jnp.swapaxes(b_im, 1, 2))


def _block_diag(blocks):
    g, r, c = blocks.shape
    per = 8
    b = blocks.reshape(g // per, per, r, 1, c) * jnp.eye(per, dtype=blocks.dtype).reshape(1, per, 1, per, 1)
    return b.reshape(g // per, per * r, per * c)


def _s5_kernel(u_ref, wre_ref, wim_ref, cre_ref, cim_ref, lbr_ref, lbi_ref, d_ref, x0r_ref, x0i_ref,
               y_ref, xfr_ref, xfi_ref, xr_scr, xi_scr, sr_scr, si_scr, *, rows_per_step, steps):
    c = pl.program_id(2)

    @pl.when(c == 0)
    def _():
        xr_scr[...] = x0r_ref[0]
        xi_scr[...] = x0i_ref[0]

    u = u_ref[0]
    sr_scr[...] = _dot(u, wre_ref[0])
    si_scr[...] = _dot(u, wim_ref[0])
    l_re, l_im = lbr_ref[...], lbi_ref[...]

    def step(t, carry):
        rows = pl.ds(pl.multiple_of(t * rows_per_step, rows_per_step), rows_per_step)
        xr, xi = xr_scr[...], xi_scr[...]
        nr = l_re * xr - l_im * xi + sr_scr[rows, :]
        ni = l_re * xi + l_im * xr + si_scr[rows, :]
        xr_scr[...] = nr
        xi_scr[...] = ni
        sr_scr[rows, :] = nr
        si_scr[rows, :] = ni
        return carry

    lax.fori_loop(0, steps, step, 0)
    y_ref[0] = _dot(sr_scr[...], cre_ref[0]) - _dot(si_scr[...], cim_ref[0]) + d_ref[...] * u

    @pl.when(c == pl.num_programs(2) - 1)
    def _():
        xfr_ref[0] = xr_scr[...]
        xfi_ref[0] = xi_scr[...]


def _s5(proj, w_re, w_im, c_re, c_im, lb_re, lb_im, d, x0_re, x0_im, rows_per_step, steps):
    ng, n_tok, _ = proj.shape
    nj = S5_W // LANE
    sw = S5_STATE // nj
    cr = rows_per_step * steps
    kern = functools.partial(_s5_kernel, rows_per_step=rows_per_step, steps=steps)
    x_spec = pl.BlockSpec((1, rows_per_step, sw), lambda g, j, c: (g, 0, j))
    return pl.pallas_call(
        kern,
        grid=(ng, nj, n_tok // cr),
        in_specs=[pl.BlockSpec((1, cr, LANE), lambda g, j, c: (g, c, CD_U // LANE + j)),
                  pl.BlockSpec((1, LANE, sw), lambda g, j, c: (j, 0, 0)),
                  pl.BlockSpec((1, LANE, sw), lambda g, j, c: (j, 0, 0)),
                  pl.BlockSpec((1, sw, LANE), lambda g, j, c: (j, 0, 0)),
                  pl.BlockSpec((1, sw, LANE), lambda g, j, c: (j, 0, 0)),
                  pl.BlockSpec((1, sw), lambda g, j, c: (0, j)),
                  pl.BlockSpec((1, sw), lambda g, j, c: (0, j)),
                  pl.BlockSpec((1, LANE), lambda g, j, c: (0, j)),
                  x_spec, x_spec],
        out_specs=[pl.BlockSpec((1, cr, LANE), lambda g, j, c: (g, c, j)), x_spec, x_spec],
        out_shape=[jax.ShapeDtypeStruct((ng, n_tok, S5_W), F32),
                   jax.ShapeDtypeStruct((ng, rows_per_step, S5_STATE), F32),
                   jax.ShapeDtypeStruct((ng, rows_per_step, S5_STATE), F32)],
        scratch_shapes=[pltpu.VMEM((rows_per_step, sw), F32), pltpu.VMEM((rows_per_step, sw), F32),
                        pltpu.VMEM((cr, sw), F32), pltpu.VMEM((cr, sw), F32)],
        compiler_params=_cparams(3), name="s5_scan",
    )(proj, w_re, w_im, c_re, c_im, lb_re, lb_im, d.reshape(1, S5_W), x0_re, x0_im)


def _s5_pow_kernel(lbr_ref, lbi_ref, pr_ref, pi_ref, *, n_rows):
    l_re, l_im = lbr_ref[...], lbi_ref[...]
    row = lax.broadcasted_iota(jnp.int32, (SUBLANE, 1), 0)
    p_re, p_im = l_re, l_im
    b_re = jnp.broadcast_to(l_re, (SUBLANE, l_re.shape[1]))
    b_im = jnp.broadcast_to(l_im, (SUBLANE, l_re.shape[1]))
    for r in range(1, SUBLANE):
        p_re, p_im = p_re * l_re - p_im * l_im, p_re * l_im + p_im * l_re
        b_re = jnp.where(row >= r, p_re, b_re)
        b_im = jnp.where(row >= r, p_im, b_im)
    q_re, q_im = jnp.ones_like(l_re), jnp.zeros_like(l_re)
    for a in range(n_rows // SUBLANE):
        rows = slice(a * SUBLANE, (a + 1) * SUBLANE)
        pr_ref[rows, :] = b_re * q_re - b_im * q_im
        pi_ref[rows, :] = b_re * q_im + b_im * q_re
        q_re, q_im = q_re * p_re - q_im * p_im, q_re * p_im + q_im * p_re


def _s5_pow_table(lb_re, lb_im, n_rows):
    shape = jax.ShapeDtypeStruct((n_rows, S5_STATE), F32)
    return pl.pallas_call(functools.partial(_s5_pow_kernel, n_rows=n_rows), out_shape=[shape, shape],
                          name="s5_pow")(lb_re, lb_im)


def _s5_seg_kernel(u_ref, wre_ref, wim_ref, cre_ref, cim_ref, pr_ref, pi_ref, d_ref, x0r_ref, x0i_ref,
                   y_ref, xfr_ref, xfi_ref, sr_scr, si_scr, *, seg_len):
    n_seg = SUBLANE
    n_lane_blk = sr_scr.shape[0]
    sw = n_lane_blk * LANE
    lane_blks = [slice(c * LANE, (c + 1) * LANE) for c in range(n_lane_blk)]

    def put(scr, rows, val):
        for c, ls in enumerate(lane_blks):
            scr[c, rows, :] = val[:, ls]

    def get(scr, rows):
        return jnp.concatenate([scr[c, rows, :] for c in range(n_lane_blk)], axis=1)

    for s in range(n_seg):
        us = u_ref[0, s * seg_len:(s + 1) * seg_len, :]
        put(sr_scr, pl.ds(s, seg_len, stride=n_seg), _dot(us, wre_ref[0]))
        put(si_scr, pl.ds(s, seg_len, stride=n_seg), _dot(us, wim_ref[0]))
    l_re, l_im = pr_ref[0:1, :], pi_ref[0:1, :]

    def step(t, carry):
        xr, xi = carry
        rows = pl.ds(pl.multiple_of(t * n_seg, n_seg), n_seg)
        nr = l_re * xr - l_im * xi + get(sr_scr, rows)
        ni = l_re * xi + l_im * xr + get(si_scr, rows)
        put(sr_scr, rows, nr)
        put(si_scr, rows, ni)
        return nr, ni

    zero = jnp.zeros((n_seg, sw), F32)
    end_re, end_im = lax.fori_loop(0, seg_len, step, (zero, zero), unroll=4)
    ln_re, ln_im = pr_ref[seg_len - 1:seg_len, :], pi_ref[seg_len - 1:seg_len, :]
    p_re, p_im = pr_ref[...], pi_ref[...]
    x_re, x_im = x0r_ref[0], x0i_ref[0]
    for s in range(n_seg):
        loc_re = get(sr_scr, pl.ds(s, seg_len, stride=n_seg))
        loc_im = get(si_scr, pl.ds(s, seg_len, stride=n_seg))
        t_re = loc_re + p_re * x_re - p_im * x_im
        t_im = loc_im + p_re * x_im + p_im * x_re
        us = u_ref[0, s * seg_len:(s + 1) * seg_len, :]
        y_ref[0, s * seg_len:(s + 1) * seg_len, :] = (_dot(t_re, cre_ref[0]) - _dot(t_im, cim_ref[0])
                                                      + d_ref[...] * us)
        e_re, e_im = end_re[s:s + 1, :], end_im[s:s + 1, :]
        x_re, x_im = e_re + ln_re * x_re - ln_im * x_im, e_im + ln_re * x_im + ln_im * x_re
    xfr_ref[0] = x_re
    xfi_ref[0] = x_im


def _s5_seg(proj, w_re, w_im, c_re, c_im, pow_re, pow_im, d, x0_re, x0_im):
    nb, seq, _ = proj.shape
    nj = S5_W // LANE
    sw = S5_STATE // nj
    seg_len = seq // SUBLANE
    x_spec = pl.BlockSpec((1, 1, sw), lambda b, j: (b, 0, j))
    return pl.pallas_call(
        functools.partial(_s5_seg_kernel, seg_len=seg_len),
        grid=(nb, nj),
        in_specs=[pl.BlockSpec((1, seq, LANE), lambda b, j: (b, 0, CD_U // LANE + j)),
                  pl.BlockSpec((1, LANE, sw), lambda b, j: (j, 0, 0)),
                  pl.BlockSpec((1, LANE, sw), lambda b, j: (j, 0, 0)),
                  pl.BlockSpec((1, sw, LANE), lambda b, j: (j, 0, 0)),
                  pl.BlockSpec((1, sw, LANE), lambda b, j: (j, 0, 0)),
                  pl.BlockSpec((seg_len, sw), lambda b, j: (0, j)),
                  pl.BlockSpec((seg_len, sw), lambda b, j: (0, j)),
                  pl.BlockSpec((1, LANE), lambda b, j: (0, j)),
                  x_spec, x_spec],
        out_specs=[pl.BlockSpec((1, seq, LANE), lambda b, j: (b, 0, j)), x_spec, x_spec],
        out_shape=[jax.ShapeDtypeStruct((nb, seq, S5_W), F32),
                   jax.ShapeDtypeStruct((nb, 1, S5_STATE), F32),
                   jax.ShapeDtypeStruct((nb, 1, S5_STATE), F32)],
        scratch_shapes=[pltpu.VMEM((sw // LANE, seq, LANE), F32), pltpu.VMEM((sw // LANE, seq, LANE), F32)],
        compiler_params=_cparams(2), name="s5_seg",
    )(proj, w_re, w_im, c_re, c_im, pow_re, pow_im, d.reshape(1, S5_W), x0_re, x0_im)


def _lane_row(vec, lane0):
    return jnp.zeros((1, LANE), F32).at[0, lane0:lane0 + vec.shape[0]].set(vec.astype(F32))


def _realign_kernel(w_ref, tail_ref, o_ref, *, plan):
    lane = lax.broadcasted_iota(jnp.int32, (1, LANE), 1)
    sel_r = lax.broadcasted_iota(jnp.int32, (2 * LANE, LANE), 0)
    sel_c = lax.broadcasted_iota(jnp.int32, (2 * LANE, LANE), 1)
    for t, op in enumerate(plan):
        dst = slice(t * LANE, (t + 1) * LANE)
        if op[0] == 'zero':
            o_ref[:, dst] = jnp.zeros((o_ref.shape[0], LANE), BF16)
        elif op[0] == 'copy':
            o_ref[:, dst] = w_ref[0, :, op[1]:op[1] + LANE].astype(BF16)
        elif op[0] == 'mask':
            acc = jnp.zeros((o_ref.shape[0], LANE), F32)
            for src, lo, hi in op[1]:
                acc = acc + jnp.where((lane >= lo) & (lane < hi), w_ref[0, :, src:src + LANE], 0.0)
            o_ref[:, dst] = acc.astype(BF16)
        else:
            a, d = op[1], op[2]
            if op[0] == 'shift':
                slab = w_ref[0, :, a:a + 2 * LANE]
            else:
                slab = jnp.concatenate([w_ref[0, :, a:a + LANE], tail_ref[...]], axis=1)
            sel = (sel_r == sel_c + d).astype(BF16)
            o_ref[:, dst] = _dot(slab.astype(BF16), sel).astype(BF16)


def _realign_weights(w, segments, masks, n_out):
    _, k, n_src = w.shape
    n_al = n_src // LANE * LANE
    tail = jnp.pad(w[0, :, n_al:], ((0, 0), (0, LANE - (n_src - n_al))))
    plan = [('zero',)] * (n_out // LANE)
    for src0, width, dst0 in segments:
        for i in range(width // LANE):
            src = src0 + i * LANE
            a, d = src // LANE * LANE, src % LANE
            if d == 0:
                op = ('copy', src)
            elif a + 2 * LANE <= n_al:
                op = ('shift', a, d)
            else:
                assert a + LANE == n_al
                op = ('shift_tail', a, d)
            plan[dst0 // LANE + i] = op
    for dst0, pieces in masks.items():
        plan[dst0 // LANE] = ('mask', pieces)
    rows = 256
    return pl.pallas_call(
        functools.partial(_realign_kernel, plan=plan),
        grid=(k // rows,),
        in_specs=[pl.BlockSpec((1, rows, n_src), lambda i: (0, i, 0)), pl.BlockSpec((rows, LANE), lambda i: (i, 0))],
        out_specs=pl.BlockSpec((rows, n_out), lambda i: (i, 0)),
        out_shape=jax.ShapeDtypeStruct((k, n_out), BF16),
        compiler_params=_cparams(1), name="realign_weights",
    )(w, tail)


def _prep_params(p):
    q = {}
    o_lr = 2 * GLA_QK + GLA_V
    o_r = o_lr + GLA_LR
    o_qkv = o_r + GLA_V
    o_a = o_qkv + GDN_CONV_W
    o_g = o_a + 2 * GDN_H
    assert o_lr % LANE == AB_LR_LANE and o_a % LANE == AB_A_LANE
    q['w_in_ab'] = _realign_weights(
        p['w_in_ab'], [(0, o_lr, AB_Q), (o_r, GLA_V, AB_R), (o_qkv, GDN_CONV_W, AB_QKV), (o_g, GDN_V, AB_G)],
        {AB_SMALL: [(o_lr, AB_LR_LANE, AB_LR_LANE + GLA_LR),
                    (o_a - AB_A_LANE, AB_A_LANE, AB_A_LANE + 2 * GDN_H)]}, AB_N)
    q['gla_w2'] = jnp.zeros((LANE, GLA_QK), F32).at[:GLA_LR].set(p['gla_w2'][0])
    q['gdn_alog'] = _lane_row(p['gdn_A_log'][0], AB_A_LANE)
    q['gdn_dtb'] = _lane_row(p['gdn_dt_bias'][0], AB_A_LANE)
    q['w_out_ab'] = p['w_out_ab'].astype(BF16)
    o_xbc = SSD_W
    o_dt = o_xbc + SSD_CONV_W
    o_u = o_dt + SSD_H
    assert o_dt % LANE == 0
    q['w_in_cd'] = _realign_weights(
        p['w_in_cd'], [(o_xbc, SSD_CONV_W, CD_XBC), (0, SSD_W, CD_Z), (o_u, S5_W, CD_U)],
        {CD_SMALL: [(o_dt, 0, SSD_H)]}, CD_N)
    q['ssd_alog'] = _lane_row(p['ssd_A_log'][0], 0)
    q['ssd_dtb'] = _lane_row(p['ssd_dt_bias'][0], 0)
    q['ssd_d_row'] = jnp.repeat(p['ssd_D'][0].astype(F32), SSD_P).reshape(1, SSD_W)
    q['w_out_cd'] = p['w_out_cd'].astype(BF16)
    lb_re, lb_im, bb_re, bb_im = _s5_prep(p['s5_A_re'][0], p['s5_A_im'][0], p['s5_log_dt'][0],
                                          p['s5_B_re'][0], p['s5_B_im'][0])
    q['s5_lb_re'], q['s5_lb_im'] = lb_re.reshape(1, S5_STATE), lb_im.reshape(1, S5_STATE)
    q['s5_w_re'], q['s5_w_im'] = _block_diag(bb_re), _block_diag(bb_im)
    q['s5_c_re'] = _block_diag(jnp.swapaxes(p['s5_C_re'][0], 1, 2))
    q['s5_c_im'] = _block_diag(jnp.swapaxes(p['s5_C_im'][0], 1, 2))
    q['s5_glu_w'] = p['s5_glu_w'][0].astype(BF16)
    q['w_ffn_down'] = p['w_ffn_down'].astype(BF16)
    return q


def _trunk(x, mods, grp, seq_shape, state, p, q):
    nb, seq_len, valid = seq_shape
    s_gla, s_gdn, s_gdnc, s_ssd, s_ssdc, s_re, s_im, s_ffn = state
    prompt = not grp.per_token_mod

    def to_seq(t):
        if prompt:
            return t.reshape(nb, seq_len, t.shape[-1])
        t = jnp.swapaxes(t.reshape(valid, nb, t.shape[-1]), 0, 1)
        return jnp.pad(t, ((0, 0), (0, seq_len - valid), (0, 0)))

    def from_seq(t):
        if prompt:
            return t.reshape(nb * seq_len, t.shape[-1])
        return jnp.swapaxes(t[:, :valid], 0, 1).reshape(valid * nb, t.shape[-1])

    blk = MIX_BLOCK if prompt else seq_len
    chunks = (GLA_CHUNK, GDN_CHUNK, SSD_CHUNK) if prompt else (seq_len,) * 3
    new = {}

    h = _norm_mod(x, p['g_mix'][0], mods[0], grp, 1, 0)
    proj = to_seq(_matmul(h, q['w_in_ab'], grp, AB_N // 3))
    o_a, new['gla'] = _gla(proj, q['gla_w2'], p['gla_b2'][0], p['gla_norm_g'][0], s_gla, chunks[0],
                            GLA_GROUP if prompt else 1, blk, valid, 1 if prompt else SAMPLE_SEQS)
    o_b, new['gdn'], new['gdnc'] = _gdn(proj, p['gdn_conv_w'][0], p['gdn_conv_b'][0], q['gdn_alog'], q['gdn_dtb'],
                                        p['gdn_norm_g'][0], s_gdn, s_gdnc, chunks[1], blk, valid,
                                        1 if prompt else SAMPLE_SEQS)
    x, h = _mm_residual([(from_seq(o_a), q['w_out_ab'], 0, 0), (from_seq(o_b), q['w_out_ab'], 0, 1)],
                        x, mods[0], grp, 2, D_MODEL, norm=(p['g_ffn'][0], 4, 3))
    act, new['ffn0'] = _ffn_up(h, p['w_ffn_up'], p['ffn_conv_w'], p['ffn_conv_b'], 0, s_ffn[0],
                               grp.retiled(FFN_TILE))
    x = _mm_residual([(act, q['w_ffn_down'], 0, 0)], x, mods[0], grp, 5, 1024)

    h = _norm_mod(x, p['g_mix'][1], mods[1], grp, 1, 0)
    proj2 = _matmul(h, q['w_in_cd'], grp, CD_N // 2)
    proj = to_seq(proj2)
    o_c, new['ssd'], new['ssdc'] = _ssd(proj, p['ssd_conv_w'][0], p['ssd_conv_b'][0], q['ssd_alog'], q['ssd_dtb'],
                                        q['ssd_d_row'], p['ssd_norm_g'][0], s_ssd, s_ssdc, chunks[2], blk, valid,
                                        1 if prompt else SAMPLE_SEQS)
    if prompt:
        pow_re, pow_im = _s5_pow_table(q['s5_lb_re'], q['s5_lb_im'], seq_len // SUBLANE)
        yd, new['re'], new['im'] = _s5_seg(proj, q['s5_w_re'], q['s5_w_im'], q['s5_c_re'], q['s5_c_im'],
                                           pow_re, pow_im, p['s5_D'][0], s_re, s_im)
    else:
        yd, new['re'], new['im'] = _s5(proj2.reshape(1, grp.n_tok, CD_N), q['s5_w_re'], q['s5_w_im'],
                                       q['s5_c_re'], q['s5_c_im'], q['s5_lb_re'], q['s5_lb_im'], p['s5_D'][0],
                                       s_re, s_im, nb, valid)
    o_d = _s5_glu(yd.reshape(grp.n_tok, S5_W), q['s5_glu_w'], p['s5_glu_b'][0], grp)
    x, h = _mm_residual([(from_seq(o_c), q['w_out_cd'], 0, 0), (o_d, q['w_out_cd'], 0, 1)], x, mods[1], grp, 2,
                        D_MODEL, norm=(p['g_ffn'][1], 4, 3))
    act, new['ffn1'] = _ffn_up(h, p['w_ffn_up'], p['ffn_conv_w'], p['ffn_conv_b'], 1, s_ffn[1],
                               grp.retiled(FFN_TILE))
    x = _mm_residual([(act, q['w_ffn_down'], 1, 0)], x, mods[1], grp, 5, 1024)
    return _final_rms(x, p['g_final'], grp), new


def kernel(x_prompt, x_sample, c_prompt, c_sample, state_gla, state_gdn, state_gdn_conv, state_ssd, state_ssd_conv, state_s5_re, state_s5_im, state_ffn_conv, w_ada, b_ada, g_mix, g_ffn, w_in_ab, gla_w2, gla_b2, gla_norm_g, gdn_conv_w, gdn_conv_b, gdn_A_log, gdn_dt_bias, gdn_norm_g, w_out_ab, w_in_cd, ssd_conv_w, ssd_conv_b, ssd_A_log, ssd_dt_bias, ssd_D, ssd_norm_g, s5_A_re, s5_A_im, s5_B_re, s5_B_im, s5_C_re, s5_C_im, s5_D, s5_log_dt, s5_glu_w, s5_glu_b, w_out_cd, w_ffn_up, ffn_conv_w, ffn_conv_b, w_ffn_down, g_final):
    p = dict(g_mix=g_mix, g_ffn=g_ffn, w_in_ab=w_in_ab, gla_w2=gla_w2, gla_b2=gla_b2, gla_norm_g=gla_norm_g,
             gdn_conv_w=gdn_conv_w, gdn_conv_b=gdn_conv_b, gdn_A_log=gdn_A_log, gdn_dt_bias=gdn_dt_bias,
             gdn_norm_g=gdn_norm_g, w_out_ab=w_out_ab, w_in_cd=w_in_cd, ssd_conv_w=ssd_conv_w,
             ssd_conv_b=ssd_conv_b, ssd_A_log=ssd_A_log, ssd_dt_bias=ssd_dt_bias, ssd_D=ssd_D,
             ssd_norm_g=ssd_norm_g, s5_A_re=s5_A_re, s5_A_im=s5_A_im, s5_B_re=s5_B_re, s5_B_im=s5_B_im,
             s5_C_re=s5_C_re, s5_C_im=s5_C_im, s5_D=s5_D, s5_log_dt=s5_log_dt, s5_glu_w=s5_glu_w,
             s5_glu_b=s5_glu_b, w_out_cd=w_out_cd, w_ffn_up=w_ffn_up, ffn_conv_w=ffn_conv_w,
             ffn_conv_b=ffn_conv_b, w_ffn_down=w_ffn_down, g_final=g_final)
    bp, lp, d = x_prompt.shape
    bs, ls, _ = x_sample.shape
    q = _prep_params(p)

    bp_pad = -(-bp // SUBLANE) * SUBLANE
    c_all = jnp.concatenate([c_prompt, jnp.zeros((bp_pad - bp, d), F32), c_sample], axis=0)
    mod = _ada_mod(c_all, w_ada, b_ada)
    depth = w_ada.shape[0]
    mods_p = [mod[l, :bp].reshape(bp, 1, 6 * d) for l in range(depth)]
    mods_s = [jnp.tile(mod[l, bp_pad:], (ls, 1)).reshape(1, ls * bs, 6 * d) for l in range(depth)]

    tile_p = 512
    grp_p = _Group(bp * lp, tile_p, False, lp // tile_p, 1)
    zeros = lambda *shape: jnp.zeros(shape, F32)
    state_p = (zeros(bp, GLA_H, GLA_DK, GLA_DV), zeros(bp, GDN_H, GDN_DK, GDN_DV),
               zeros(bp, CONV_K - 1, GDN_CONV_W), zeros(bp, SSD_H, SSD_P, SSD_N),
               zeros(bp, CONV_K - 1, SSD_CONV_W), zeros(bp, 1, S5_STATE), zeros(bp, 1, S5_STATE),
               zeros(depth, bp, FFN_K - 1, 2 * D_FF))
    y_p, new_p = _trunk(x_prompt.reshape(bp * lp, d), mods_p, grp_p, (bp, lp, lp), state_p, p, q)

    grp_s = _Group(bs * ls, bs * ls, True, 1, bs)
    ffn_hist_s = jnp.swapaxes(state_ffn_conv, 1, 2).reshape(depth, 1, (FFN_K - 1) * bs, 2 * D_FF)
    state_s = (state_gla[0], state_gdn[0], state_gdn_conv[0], state_ssd[0], state_ssd_conv[0],
               state_s5_re.reshape(1, bs, S5_STATE), state_s5_im.reshape(1, bs, S5_STATE), ffn_hist_s)
    x_s = jnp.swapaxes(x_sample, 0, 1).reshape(ls * bs, d)
    y_s, new_s = _trunk(x_s, mods_s, grp_s, (bs, SAMPLE_PAD, ls), state_s, p, q)
    y_s = jnp.swapaxes(y_s.reshape(ls, bs, d), 0, 1)

    ffn_p = jnp.stack([new_p['ffn0'], new_p['ffn1']])
    ffn_s = jnp.stack([jnp.swapaxes(new_s[k].reshape(FFN_K - 1, bs, 2 * D_FF), 0, 1) for k in ('ffn0', 'ffn1')])
    s5_shape = lambda t, nb: t.reshape(1, nb, S5_G, S5_P)
    return (y_p.reshape(bp, lp, d), y_s,
            new_p['gla'][None], new_s['gla'][None], new_p['gdn'][None], new_s['gdn'][None],
            new_p['gdnc'][None], new_s['gdnc'][None], new_p['ssd'][None], new_s['ssd'][None],
            new_p['ssdc'][None], new_s['ssdc'][None],
            s5_shape(new_p['re'], bp), s5_shape(new_s['re'], bs), s5_shape(new_p['im'], bp), s5_shape(new_s['im'], bs),
            ffn_p, ffn_s)
```

```python
import functools
import math

import jax
import jax.numpy as jnp
from jax import lax
from jax.experimental import pallas as pl
from jax.experimental.pallas import tpu as pltpu

F32 = jnp.float32
BF16 = jnp.bfloat16
HIGHEST = lax.Precision.HIGHEST
EPS = 1e-6

D_MODEL = 2048
GLA_H, GLA_DK, GLA_DV, GLA_LR = 4, 128, 256, 16
GLA_GATE_NORM = 16.0
GLA_QK, GLA_V = GLA_H * GLA_DK, GLA_H * GLA_DV
GDN_H, GDN_DK, GDN_DV = 8, 128, 128
GDN_QK, GDN_V = GDN_H * GDN_DK, GDN_H * GDN_DV
CONV_K = 4
GDN_CONV_W = 2 * GDN_QK + GDN_V
SSD_P, SSD_H, SSD_G, SSD_N = 64, 16, 2, 128
SSD_W = SSD_H * SSD_P
SSD_CONV_W = SSD_W + 2 * SSD_G * SSD_N
S5_W, S5_GS, S5_G, S5_P = 1024, 16, 64, 64
S5_STATE = S5_G * S5_P
D_FF = 5632
FFN_K = 3

LANE = 128
SUBLANE = 8
VMEM_LIMIT = 48 * 1024 * 1024

AB_Q, AB_K, AB_V, AB_R, AB_QKV, AB_G, AB_SMALL, AB_N = 0, 512, 1024, 2048, 3072, 6144, 7168, 7296
AB_LR_LANE, AB_A_LANE, AB_B_LANE = 0, 16, 24
CD_XBC, CD_SMALL, CD_Z, CD_U, CD_N = 0, 1536, 2048, 3072, 4096

MIX_BLOCK = 256
GLA_CHUNK, GDN_CHUNK, SSD_CHUNK = 16, 128, 128
GLA_GROUP = 4
FFN_TILE = 1024
FFN_SUB = 256
INV_BLOCK = 16
SAMPLE_PAD = 8
SAMPLE_SEQS = 8


def _cparams(n_axes):
    return pltpu.CompilerParams(dimension_semantics=("arbitrary",) * n_axes, vmem_limit_bytes=VMEM_LIMIT)


def _sigmoid(x):
    return 1.0 / (1.0 + jnp.exp(-x))


def _silu(x):
    return x * _sigmoid(x)


def _softplus(x):
    return jnp.maximum(x, 0.0) + jnp.log(1.0 + jnp.exp(-jnp.abs(x)))


def _gelu_tanh(x):
    return 0.5 * x * (1.0 + jnp.tanh(math.sqrt(2.0 / math.pi) * (x + 0.044715 * (x * x * x))))


def _dot(a, b, precision=None):
    return jnp.dot(a, b, precision=precision, preferred_element_type=F32)


def _dot_nt(a, b, precision=None):
    return lax.dot_general(a, b, (((1,), (1,)), ((), ())), precision=precision, preferred_element_type=F32)


def _dot_tn(a, b, precision=None):
    return lax.dot_general(a, b, (((0,), (0,)), ((), ())), precision=precision, preferred_element_type=F32)


def _head_rms(o, g):
    return o * lax.rsqrt(jnp.mean(o * o, -1, keepdims=True) + EPS) * g


class _Group:
    def __init__(self, n_tok, tile, per_token_mod, tiles_per_seq, conv_shift):
        self.n_tok = n_tok
        self.tile = tile
        self.n_tiles = n_tok // tile
        self.per_token_mod = per_token_mod
        self.tiles_per_seq = tiles_per_seq
        self.conv_shift = conv_shift

    def retiled(self, tile):
        seq_rows = self.tile * self.tiles_per_seq
        if seq_rows % tile or self.n_tok % tile:
            return self
        return _Group(self.n_tok, tile, self.per_token_mod, seq_rows // tile, self.conv_shift)

    def mod_spec(self, width, col_block, m_axis):
        if self.per_token_mod:
            return pl.BlockSpec((1, self.tile, width), lambda *g: (0, g[m_axis], col_block(*g)))
        tps = self.tiles_per_seq
        return pl.BlockSpec((1, 1, width), lambda *g: (g[m_axis] // tps, 0, col_block(*g)))


def _ada_kernel(c_ref, w_ref, b_ref, o_ref):
    cs = _silu(c_ref[...]).astype(BF16)
    o_ref[0] = _dot(cs, w_ref[0].astype(BF16)) + b_ref[0]


def _ada_mod(c, w_ada, b_ada):
    depth, d, n = w_ada.shape
    rows = c.shape[0]
    tn = 1024
    return pl.pallas_call(
        _ada_kernel,
        grid=(depth, n // tn),
        in_specs=[pl.BlockSpec((rows, d), lambda l, j: (0, 0)),
                  pl.BlockSpec((1, d, tn), lambda l, j: (l, 0, j)),
                  pl.BlockSpec((1, 1, tn), lambda l, j: (l, 0, j))],
        out_specs=pl.BlockSpec((1, rows, tn), lambda l, j: (l, 0, j)),
        out_shape=jax.ShapeDtypeStruct((depth, rows, n), F32),
        compiler_params=_cparams(2), name="ada_mod",
    )(c, w_ada, b_ada.reshape(depth, 1, n))


def _norm_mod_kernel(x_ref, g_ref, sc_ref, sh_ref, o_ref):
    x = x_ref[...]
    y = x * lax.rsqrt(jnp.mean(x * x, -1, keepdims=True) + EPS) * g_ref[...]
    o_ref[...] = (y * (1.0 + sc_ref[0]) + sh_ref[0]).astype(BF16)


def _norm_mod(x, g, mod, grp, sc_blk, sh_blk):
    d = x.shape[1]
    return pl.pallas_call(
        _norm_mod_kernel,
        grid=(grp.n_tiles,),
        in_specs=[pl.BlockSpec((grp.tile, d), lambda i: (i, 0)),
                  pl.BlockSpec((1, d), lambda i: (0, 0)),
                  grp.mod_spec(d, lambda i: sc_blk, 0),
                  grp.mod_spec(d, lambda i: sh_blk, 0)],
        out_specs=pl.BlockSpec((grp.tile, d), lambda i: (i, 0)),
        out_shape=jax.ShapeDtypeStruct(x.shape, BF16),
        compiler_params=_cparams(1), name="norm_mod",
    )(x, g.reshape(1, d), mod, mod)


def _rms_kernel(x_ref, g_ref, o_ref):
    x = x_ref[...]
    o_ref[...] = x * lax.rsqrt(jnp.mean(x * x, -1, keepdims=True) + EPS) * g_ref[...]


def _final_rms(x, g, grp):
    d = x.shape[1]
    return pl.pallas_call(
        _rms_kernel,
        grid=(grp.n_tiles,),
        in_specs=[pl.BlockSpec((grp.tile, d), lambda i: (i, 0)), pl.BlockSpec((1, d), lambda i: (0, 0))],
        out_specs=pl.BlockSpec((grp.tile, d), lambda i: (i, 0)),
        out_shape=jax.ShapeDtypeStruct(x.shape, F32),
        compiler_params=_cparams(1), name="final_rms",
    )(x, g.reshape(1, d))


def _matmul_kernel(a_ref, wt_ref, o_ref):
    o_ref[...] = _dot_nt(a_ref[...], wt_ref[...])


def _matmul(a, wt, grp, tn):
    n, k = wt.shape
    return pl.pallas_call(
        _matmul_kernel,
        grid=(n // tn, grp.n_tiles),
        in_specs=[pl.BlockSpec((grp.tile, k), lambda j, i: (i, 0)),
                  pl.BlockSpec((tn, k), lambda j, i: (j, 0))],
        out_specs=pl.BlockSpec((grp.tile, tn), lambda j, i: (i, j)),
        out_shape=jax.ShapeDtypeStruct((a.shape[0], n), F32),
        compiler_params=_cparams(2), name="matmul",
    )(a, wt)


def _mm_res_kernel(*refs, n_pairs, with_norm):
    x_ref, gate_ref = refs[2 * n_pairs:2 * n_pairs + 2]
    y = _dot(refs[0][...].astype(BF16), refs[1][0])
    for p in range(1, n_pairs):
        y = y + _dot(refs[2 * p][...].astype(BF16), refs[2 * p + 1][0])
    x_new = x_ref[...] + gate_ref[0] * y
    if not with_norm:
        o_ref, = refs[2 * n_pairs + 2:]
        o_ref[...] = x_new
        return
    g_ref, sc_ref, sh_ref, o_ref, h_ref = refs[2 * n_pairs + 2:]
    o_ref[...] = x_new
    normed = x_new * lax.rsqrt(jnp.mean(x_new * x_new, -1, keepdims=True) + EPS) * g_ref[...]
    h_ref[...] = (normed * (1.0 + sc_ref[0]) + sh_ref[0]).astype(BF16)


def _mm_residual(pairs, x, mod, grp, gate_blk, tn, norm=None):
    n = x.shape[1]
    if norm is not None:
        assert tn == n
        return _mm_residual_norm(pairs, x, mod, grp, gate_blk, norm)
    in_specs, args = [], []
    for a, w, layer, row_blk in pairs:
        k = a.shape[1]
        in_specs += [pl.BlockSpec((grp.tile, k), lambda j, i: (i, 0)),
                     pl.BlockSpec((1, k, tn), lambda j, i, layer=layer, row_blk=row_blk: (layer, row_blk, j))]
        args += [a, w]
    in_specs += [pl.BlockSpec((grp.tile, tn), lambda j, i: (i, j)),
                 grp.mod_spec(tn, lambda j, i: gate_blk * (n // tn) + j, 1)]
    return pl.pallas_call(
        functools.partial(_mm_res_kernel, n_pairs=len(pairs), with_norm=False),
        grid=(n // tn, grp.n_tiles),
        in_specs=in_specs,
        out_specs=pl.BlockSpec((grp.tile, tn), lambda j, i: (i, j)),
        out_shape=jax.ShapeDtypeStruct(x.shape, F32),
        compiler_params=_cparams(2), name="mm_residual",
    )(*args, x, mod)


def _mm_residual_norm(pairs, x, mod, grp, gate_blk, norm):
    n = x.shape[1]
    gain, sc_blk, sh_blk = norm
    in_specs, args = [], []
    for a, w, layer, row_blk in pairs:
        k = a.shape[1]
        in_specs += [pl.BlockSpec((grp.tile, k), lambda j, i: (i, 0)),
                     pl.BlockSpec((1, k, n), lambda j, i, layer=layer, row_blk=row_blk: (layer, row_blk, 0))]
        args += [a, w]
    row_spec = pl.BlockSpec((grp.tile, n), lambda j, i: (i, 0))
    in_specs += [row_spec, grp.mod_spec(n, lambda j, i: gate_blk, 1), pl.BlockSpec((1, n), lambda j, i: (0, 0)),
                 grp.mod_spec(n, lambda j, i: sc_blk, 1), grp.mod_spec(n, lambda j, i: sh_blk, 1)]
    return pl.pallas_call(
        functools.partial(_mm_res_kernel, n_pairs=len(pairs), with_norm=True),
        grid=(1, grp.n_tiles),
        in_specs=in_specs,
        out_specs=[row_spec, row_spec],
        out_shape=[jax.ShapeDtypeStruct(x.shape, F32), jax.ShapeDtypeStruct(x.shape, BF16)],
        compiler_params=_cparams(2), name="mm_residual_norm",
    )(*args, x, mod, gain.reshape(1, n), mod, mod)


def _ffn_up_kernel(h_ref, wa_ref, wg_ref, cwa_ref, cwg_ref, cba_ref, cbg_ref, ha_ref, hg_ref,
                   act_ref, sta_ref, stg_ref, scr_a, scr_g, wba_scr, wbg_scr, *, shift, tile, sub, tiles_per_seq):
    i = pl.program_id(1)
    hist = (FFN_K - 1) * shift
    base = -(-hist // SUBLANE) * SUBLANE

    @pl.when(i == 0)
    def _():
        wba_scr[...] = wa_ref[0].astype(BF16)
        wbg_scr[...] = wg_ref[0].astype(BF16)

    @pl.when(i % tiles_per_seq == 0)
    def _():
        scr_a[base - hist:base, :] = ha_ref[0]
        scr_g[base - hist:base, :] = hg_ref[0]

    def conv(scr, cw_ref, cb_ref, r0):
        y = cb_ref[0]
        for j in range(FFN_K):
            lo = base + r0 - (FFN_K - 1 - j) * shift
            y = y + scr[lo:lo + sub, :] * cw_ref[0, j:j + 1, :]
        return y

    def project(r0):
        h = h_ref[r0:r0 + sub, :]
        scr_a[base + r0:base + r0 + sub, :] = _dot(h, wba_scr[...])
        scr_g[base + r0:base + r0 + sub, :] = _dot(h, wbg_scr[...])

    project(0)
    for r0 in range(0, tile, sub):
        if r0 + sub < tile:
            project(r0 + sub)
        a = conv(scr_a, cwa_ref, cba_ref, r0)
        g = conv(scr_g, cwg_ref, cbg_ref, r0)
        act_ref[r0:r0 + sub, :] = (_silu(g) * a).astype(BF16)
    last_a = scr_a[base + tile - hist:base + tile, :]
    last_g = scr_g[base + tile - hist:base + tile, :]
    sta_ref[0] = last_a
    stg_ref[0] = last_g
    scr_a[base - hist:base, :] = last_a
    scr_g[base - hist:base, :] = last_g


def _ffn_up(h, w_up, conv_w, conv_b, layer, hist0, grp):
    d = h.shape[1]
    tn = 512
    nj = D_FF // tn
    shift = grp.conv_shift
    hist = (FFN_K - 1) * shift
    base = -(-hist // SUBLANE) * SUBLANE
    n_seq = grp.n_tiles // grp.tiles_per_seq
    tps = grp.tiles_per_seq
    cb = conv_b.reshape(conv_b.shape[0], 1, 2 * D_FF)
    kern = functools.partial(_ffn_up_kernel, shift=shift, tile=grp.tile, sub=min(FFN_SUB, grp.tile),
                             tiles_per_seq=tps)
    assert grp.tile % min(FFN_SUB, grp.tile) == 0
    act, st_a, st_g = pl.pallas_call(
        kern,
        grid=(nj, grp.n_tiles),
        in_specs=[pl.BlockSpec((grp.tile, d), lambda j, i: (i, 0)),
                  pl.BlockSpec((1, d, tn), lambda j, i: (layer, 0, j)),
                  pl.BlockSpec((1, d, tn), lambda j, i: (layer, 0, nj + j)),
                  pl.BlockSpec((1, FFN_K, tn), lambda j, i: (layer, 0, j)),
                  pl.BlockSpec((1, FFN_K, tn), lambda j, i: (layer, 0, nj + j)),
                  pl.BlockSpec((1, 1, tn), lambda j, i: (layer, 0, j)),
                  pl.BlockSpec((1, 1, tn), lambda j, i: (layer, 0, nj + j)),
                  pl.BlockSpec((1, hist, tn), lambda j, i: (i // tps, 0, j)),
                  pl.BlockSpec((1, hist, tn), lambda j, i: (i // tps, 0, nj + j))],
        out_specs=[pl.BlockSpec((grp.tile, tn), lambda j, i: (i, j)),
                   pl.BlockSpec((1, hist, tn), lambda j, i: (i // tps, 0, j)),
                   pl.BlockSpec((1, hist, tn), lambda j, i: (i // tps, 0, j))],
        out_shape=[jax.ShapeDtypeStruct((h.shape[0], D_FF), BF16),
                   jax.ShapeDtypeStruct((n_seq, hist, D_FF), F32),
                   jax.ShapeDtypeStruct((n_seq, hist, D_FF), F32)],
        scratch_shapes=[pltpu.VMEM((base + grp.tile, tn), F32), pltpu.VMEM((base + grp.tile, tn), F32),
                        pltpu.VMEM((d, tn), BF16), pltpu.VMEM((d, tn), BF16)],
        compiler_params=_cparams(2), name="ffn_up",
    )(h, w_up, w_up, conv_w, conv_w, cb, cb, hist0, hist0)
    return act, jnp.concatenate([st_a, st_g], axis=-1)


def _glu_kernel(y_ref, w_ref, b_ref, o_ref):
    z5 = _gelu_tanh(y_ref[...])
    o_ref[...] = (z5 * _sigmoid(_dot(z5.astype(BF16), w_ref[...]) + b_ref[...])).astype(o_ref.dtype)


def _s5_glu(yd, w, b, grp):
    n = yd.shape[1]
    return pl.pallas_call(
        _glu_kernel,
        grid=(grp.n_tiles,),
        in_specs=[pl.BlockSpec((grp.tile, n), lambda i: (i, 0)),
                  pl.BlockSpec((n, n), lambda i: (0, 0)),
                  pl.BlockSpec((1, n), lambda i: (0, 0))],
        out_specs=pl.BlockSpec((grp.tile, n), lambda i: (i, 0)),
        out_shape=jax.ShapeDtypeStruct(yd.shape, BF16),
        compiler_params=_cparams(1), name="s5_glu",
    )(yd, w, b.reshape(1, n))


def _round_robin(gens):
    gens = list(gens)
    while gens:
        alive = []
        for gen in gens:
            try:
                next(gen)
                alive.append(gen)
            except StopIteration:
                pass
        gens = alive


def _mixer_out_dtype(block):
    return BF16 if block % (2 * SUBLANE) == 0 else F32


def _causal_conv_chunk(x, cv_scr, cw_ref, cb_ref, chunk, valid):
    base = SUBLANE
    cv_scr[base:base + chunk, :] = x
    y = cb_ref[...]
    for j in range(CONV_K):
        lo = base - (CONV_K - 1) + j
        y = y + cv_scr[lo:lo + chunk, :] * cw_ref[j:j + 1, :]
    last = cv_scr[base + valid - (CONV_K - 1):base + valid, :]
    cv_scr[base - (CONV_K - 1):base, :] = last
    return y


def _gla_kernel(q_ref, k_ref, v_ref, r_ref, sm_ref, w2_ref, b2_ref, ng_ref, s0_ref,
                o_ref, sout_ref, s_scr, b_scr, *, chunk, group, block, valid, nseq):
    blk = pl.program_id(1)

    @pl.when(blk == 0)
    def _():
        s_scr[...] = s0_ref[...]

    span = chunk * group
    lane = lax.broadcasted_iota(jnp.int32, (1, GLA_DK), 1)
    ones_kk = jnp.ones((GLA_DK, GLA_DK), BF16)
    row_in_chunk = lax.broadcasted_iota(jnp.int32, (span, 1), 0) % chunk
    eye = (lax.broadcasted_iota(jnp.int32, (GLA_DK, GLA_DK), 0)
           == lax.broadcasted_iota(jnp.int32, (GLA_DK, GLA_DK), 1))
    n_valid = min(valid, chunk)
    heads = range(GLA_H)
    chunks = range(group)

    def to3(t):
        return t.reshape(group, chunk, t.shape[-1])

    ri = lax.broadcasted_iota(jnp.int32, (block, block), 0)
    ci = lax.broadcasted_iota(jnp.int32, (block, block), 1)
    tri = ((ri // chunk == ci // chunk) & (ri >= ci)).astype(BF16)
    row_blk = lax.broadcasted_iota(jnp.int32, (block, 1), 0) % chunk
    for sq in range(nseq):
        x = _dot(sm_ref[sq].astype(BF16), w2_ref[...].astype(BF16)) + b2_ref[...]
        log_a = (jnp.minimum(x, 0.0) - jnp.log(1.0 + jnp.exp(-jnp.abs(x)))) * (1.0 / GLA_GATE_NORM)
        if valid < chunk:
            log_a = jnp.where(row_blk < valid, log_a, 0.0)
        b_scr[sq] = sum(_dot(tri, part) for part in _split3(log_a))

    def one_seq(sq, rows):
        b = b_scr[sq, rows, :]
        q = q_ref[sq, rows, :] * GLA_DK ** -0.5
        k = k_ref[sq, rows, :]
        v = v_ref[sq, rows, :]
        r = r_ref[sq, rows, :]
        b3, q_in, kv, d_col = [], [], [], []
        for h in heads:
            ks = slice(h * GLA_DK, (h + 1) * GLA_DK)
            bh3 = to3(b[:, ks])
            b_last = bh3[:, chunk - 1:chunk, :]
            k_out = (to3(k[:, ks]) * jnp.exp(b_last - bh3)).reshape(span, GLA_DK)
            vh = v[:, h * GLA_DV:(h + 1) * GLA_DV]
            b3.append(bh3)
            q_in.append(q[:, ks] * jnp.exp(b[:, ks]))
            kv.append([_dot_tn(k_out[c * chunk:(c + 1) * chunk], vh[c * chunk:(c + 1) * chunk]) for c in chunks])
            d_col.append([jnp.sum(jnp.where(eye, jnp.exp(b_last[c]), 0.0), axis=-1, keepdims=True) for c in chunks])
        yield
        st = [s_scr[sq, h] for h in heads]
        o_inter = [[] for _ in heads]
        for c in chunks:
            for h in heads:
                o_inter[h].append(_dot(q_in[h][c * chunk:(c + 1) * chunk], st[h]))
                st[h] = st[h] * d_col[h][c] + kv[h][c]
            yield
        for h in heads:
            ks = slice(h * GLA_DK, (h + 1) * GLA_DK)
            vs = slice(h * GLA_DV, (h + 1) * GLA_DV)
            s_scr[sq, h] = st[h]
            bh3, qh3, kh3, vh3 = b3[h], to3(q[:, ks]), to3(k[:, ks]), to3(v[:, vs])
            prods = []
            for j in range(n_valid):
                e = jnp.exp(bh3 - bh3[:, j:j + 1, :])
                prods.append((qh3 * e * kh3[:, j:j + 1, :]).reshape(span, GLA_DK).astype(BF16))
            sums = _dot(jnp.concatenate(prods, axis=0), ones_kk)
            yield
            att = jnp.zeros((span, GLA_DK), F32)
            for j in range(n_valid):
                att = jnp.where(lane == j, sums[j * span:(j + 1) * span], att)
            att = jnp.where(row_in_chunk >= lane, att, 0.0)[:, :chunk]
            vh = v[:, vs]
            o_intra = [_dot(att[c * chunk:(c + 1) * chunk], vh[c * chunk:(c + 1) * chunk]) for c in chunks]
            o = jnp.concatenate(o_intra, axis=0) + jnp.concatenate(o_inter[h], axis=0)
            o_ref[sq, rows, vs] = (_head_rms(o, ng_ref[...]) * _silu(r[:, vs])).astype(o_ref.dtype)
            yield

    def do_span(s, carry):
        rows = pl.ds(pl.multiple_of(s * span, span), span)
        _round_robin(one_seq(sq, rows) for sq in range(nseq))
        return carry

    lax.fori_loop(0, block // span, do_span, 0)

    @pl.when(blk == pl.num_programs(1) - 1)
    def _():
        sout_ref[...] = s_scr[...]


def _gla(proj, w2p, b2, ng, s0, chunk, group, block, valid, nseq):
    nb, seq, _ = proj.shape

    def col(width, off):
        return pl.BlockSpec((nseq, block, width), lambda b, i: (b, i, off // width))

    st_spec = pl.BlockSpec((nseq, GLA_H, GLA_DK, GLA_DV), lambda b, i: (b, 0, 0, 0))
    kern = functools.partial(_gla_kernel, chunk=chunk, group=group, block=block, valid=valid, nseq=nseq)
    return pl.pallas_call(
        kern,
        grid=(nb // nseq, seq // block),
        in_specs=[col(GLA_QK, AB_Q), col(GLA_QK, AB_K), col(GLA_V, AB_V), col(GLA_V, AB_R), col(LANE, AB_SMALL),
                  pl.BlockSpec((LANE, GLA_QK), lambda b, i: (0, 0)),
                  pl.BlockSpec((1, GLA_QK), lambda b, i: (0, 0)),
                  pl.BlockSpec((1, GLA_DV), lambda b, i: (0, 0)),
                  st_spec],
        out_specs=[pl.BlockSpec((nseq, block, GLA_V), lambda b, i: (b, i, 0)), st_spec],
        out_shape=[jax.ShapeDtypeStruct((nb, seq, GLA_V), _mixer_out_dtype(block)),
                   jax.ShapeDtypeStruct((nb, GLA_H, GLA_DK, GLA_DV), F32)],
        scratch_shapes=[pltpu.VMEM((nseq, GLA_H, GLA_DK, GLA_DV), F32), pltpu.VMEM((nseq, block, GLA_QK), F32)],
        compiler_params=_cparams(2), name="gla",
    )(proj, proj, proj, proj, proj, w2p, b2.reshape(1, GLA_QK), ng.reshape(1, GLA_DV), s0)


def _split2(a):
    hi = a.astype(BF16)
    return hi, (a - hi.astype(F32)).astype(BF16)


def _split3(a):
    hi = a.astype(BF16)
    rest = a - hi.astype(F32)
    mid = rest.astype(BF16)
    return hi, mid, (rest - mid.astype(F32)).astype(BF16)


def _dot3(a, b):
    return _dot(a[0], b[0]) + _dot(a[0], b[1]) + _dot(a[1], b[0])


def _inv_unit_lower_many(mats, n, eye):
    ps = [eye - a for a in mats]
    if n <= 2:
        return ps
    pows = [_split2(a) for a in mats]
    k = 2
    pending = None
    while k < n:
        sq = [_dot3(a, a) for a in pows]
        if pending is not None:
            ps = [p + _dot3(_split2(p), f) for p, f in zip(ps, pending)]
        pows = [_split2(a) for a in sq]
        pending = pows
        k *= 2
    return [p + _dot3(_split2(p), f) for p, f in zip(ps, pending)]


def _gdn_kernel(qkv_ref, sm_ref, gb_ref, cw_ref, cb_ref, alog_ref, dtb_ref, ng_ref, s0_ref, c0_ref,
                o_ref, sout_ref, cout_ref, s_scr, cv_scr, *, chunk, block, valid, nseq):
    blk = pl.program_id(1)

    @pl.when(blk == 0)
    def _():
        s_scr[...] = s0_ref[...]
        cv_scr[:, SUBLANE - (CONV_K - 1):SUBLANE, :] = c0_ref[...]

    ri = lax.broadcasted_iota(jnp.int32, (chunk, chunk), 0)
    ci = lax.broadcasted_iota(jnp.int32, (chunk, chunk), 1)
    causal = ri >= ci
    strict = ri > ci
    eye = (ri == ci).astype(F32)
    tri = causal.astype(F32)
    tri_u = (ri <= ci).astype(F32)
    inv_blk = min(INV_BLOCK, chunk)
    same_blk = (ri // inv_blk) == (ci // inv_blk)
    row = lax.broadcasted_iota(jnp.int32, (chunk, 1), 0)
    n_valid = min(valid, chunk)

    units = [(sq, h) for sq in range(nseq) for h in range(GDN_H)]
    heads = range(len(units))

    def prelude(rows):
        q, k, kb, rhs, dec, gcc = [], [], [], [], [], []
        for sq, h in units:
            if h == 0:
                act = _silu(_causal_conv_chunk(qkv_ref[sq, rows, :], cv_scr.at[sq], cw_ref, cb_ref, chunk, n_valid))
                sm = sm_ref[sq, rows, :]
                g_all = -jnp.exp(alog_ref[...]) * _softplus(sm + dtb_ref[...])
                beta_all = _sigmoid(sm)
                if valid < chunk:
                    g_all = jnp.where(row < valid, g_all, 0.0)
                    beta_all = jnp.where(row < valid, beta_all, 0.0)
                gc = _dot(tri, g_all, HIGHEST)
                gc_r = _dot_tn(g_all, tri_u, HIGHEST)
            qh = act[:, h * GDN_DK:(h + 1) * GDN_DK]
            kh = act[:, GDN_QK + h * GDN_DK:GDN_QK + (h + 1) * GDN_DK]
            vh = act[:, 2 * GDN_QK + h * GDN_DV:2 * GDN_QK + (h + 1) * GDN_DV]
            qh = qh * lax.rsqrt(jnp.sum(qh * qh, -1, keepdims=True) + EPS) * GDN_DK ** -0.5
            kh = kh * lax.rsqrt(jnp.sum(kh * kh, -1, keepdims=True) + EPS)
            beta = beta_all[:, AB_B_LANE + h:AB_B_LANE + h + 1]
            gch = gc[:, AB_A_LANE + h:AB_A_LANE + h + 1]
            gcr = gc_r[AB_A_LANE + h:AB_A_LANE + h + 1, :]
            q.append(qh)
            k.append(kh)
            kb.append(kh * beta)
            rhs.append(_split2(jnp.concatenate([vh * beta, kb[-1] * jnp.exp(gch)], axis=1)))
            dec.append(jnp.exp(jnp.where(causal, gch - gcr, -jnp.inf)))
            gcc.append(gch)
        kbf = [a.astype(BF16) for a in k]
        m = [jnp.where(strict, _dot_nt(kb[h].astype(BF16), kbf[h]) * dec[h], 0.0) for h in heads]
        att = [_dot_nt(q[h].astype(BF16), kbf[h]) * dec[h] for h in heads]
        return q, k, rhs, gcc, m, att

    def finish(pre, rows):
        q, k, rhs, gcc, m, att = pre
        m_diag = [jnp.where(same_blk, a, 0.0) for a in m]
        t = [_split2(a) for a in _inv_unit_lower_many(m_diag, inv_blk, eye)]
        y = [_dot3(t[h], rhs[h]) for h in heads]
        if chunk > inv_blk:
            n_off = [_dot3(t[h], _split2(m[h] - m_diag[h])) for h in heads]
            qn = [_split2(a) for a in _inv_unit_lower_many(n_off, chunk // inv_blk, eye)]
            y = [_dot3(qn[h], _split2(y[h])) for h in heads]
        st = [s_scr[sq, h] for sq, h in units]
        stb = [a.astype(BF16) for a in st]
        v_new = [y[h][:, :GDN_DV] - _dot(y[h][:, GDN_DV:].astype(BF16), stb[h]) for h in heads]
        vnb = [a.astype(BF16) for a in v_new]
        o = [_dot((q[h] * jnp.exp(gcc[h])).astype(BF16), stb[h]) + _dot(att[h].astype(BF16), vnb[h]) for h in heads]
        for u, (sq, h) in enumerate(units):
            hs = slice(h * GDN_DV, (h + 1) * GDN_DV)
            g_last = gcc[u][chunk - 1:chunk, :]
            k_out = (k[u] * jnp.exp(g_last - gcc[u])).astype(BF16)
            s_scr[sq, h] = st[u] * jnp.exp(g_last) + _dot_tn(k_out, vnb[u])
            o_ref[sq, rows, hs] = (_head_rms(o[u], ng_ref[...]) * _silu(gb_ref[sq, rows, hs])).astype(o_ref.dtype)

    n_chunks = block // chunk
    pre = prelude(pl.ds(0, chunk))
    for c in range(n_chunks):
        nxt = prelude(pl.ds((c + 1) * chunk, chunk)) if c + 1 < n_chunks else None
        finish(pre, pl.ds(c * chunk, chunk))
        pre = nxt

    @pl.when(blk == pl.num_programs(1) - 1)
    def _():
        sout_ref[...] = s_scr[...]
        cout_ref[...] = cv_scr[:, SUBLANE - (CONV_K - 1):SUBLANE, :]


def _gdn(proj, conv_w, conv_b, alog_row, dtb_row, ng, s0, c0, chunk, block, valid, nseq):
    nb, seq, _ = proj.shape

    def col(width, off):
        return pl.BlockSpec((nseq, block, width), lambda b, i: (b, i, off // width))

    def const(shape):
        return pl.BlockSpec(shape, lambda b, i: (0,) * len(shape))

    st_spec = pl.BlockSpec((nseq, GDN_H, GDN_DK, GDN_DV), lambda b, i: (b, 0, 0, 0))
    cv_spec = pl.BlockSpec((nseq, CONV_K - 1, GDN_CONV_W), lambda b, i: (b, 0, 0))
    kern = functools.partial(_gdn_kernel, chunk=chunk, block=block, valid=valid, nseq=nseq)
    return pl.pallas_call(
        kern,
        grid=(nb // nseq, seq // block),
        in_specs=[col(GDN_CONV_W, AB_QKV), col(LANE, AB_SMALL), col(GDN_V, AB_G),
                  const((CONV_K, GDN_CONV_W)), const((1, GDN_CONV_W)), const((1, LANE)), const((1, LANE)),
                  const((1, GDN_DV)), st_spec, cv_spec],
        out_specs=[pl.BlockSpec((nseq, block, GDN_V), lambda b, i: (b, i, 0)), st_spec, cv_spec],
        out_shape=[jax.ShapeDtypeStruct((nb, seq, GDN_V), _mixer_out_dtype(block)),
                   jax.ShapeDtypeStruct((nb, GDN_H, GDN_DK, GDN_DV), F32),
                   jax.ShapeDtypeStruct((nb, CONV_K - 1, GDN_CONV_W), F32)],
        scratch_shapes=[pltpu.VMEM((nseq, GDN_H, GDN_DK, GDN_DV), F32),
                        pltpu.VMEM((nseq, SUBLANE + chunk, GDN_CONV_W), F32)],
        compiler_params=_cparams(2), name="gdn",
    )(proj, proj, proj, conv_w, conv_b.reshape(1, GDN_CONV_W), alog_row, dtb_row, ng.reshape(1, GDN_DV), s0, c0)


def _ssd_kernel(xbc_ref, sm_ref, z_ref, cw_ref, cb_ref, alog_ref, dtb_ref, drow_ref, ng_ref, s0_ref, c0_ref,
                o_ref, sout_ref, cout_ref, s_scr, cv_scr, dt_scr, acs_scr, acsr_scr, dtr_scr,
                *, chunk, block, valid, nseq):
    blk = pl.program_id(1)

    @pl.when(blk == 0)
    def _():
        s_scr[...] = s0_ref[...]
        cv_scr[:, SUBLANE - (CONV_K - 1):SUBLANE, :] = c0_ref[...]

    ri = lax.broadcasted_iota(jnp.int32, (chunk, chunk), 0)
    ci = lax.broadcasted_iota(jnp.int32, (chunk, chunk), 1)
    causal = ri >= ci
    eye = (ri == ci).astype(BF16)
    tri = causal.astype(BF16)
    tri_u = (ri <= ci).astype(BF16)
    row = lax.broadcasted_iota(jnp.int32, (chunk, 1), 0)
    lane_lo = lax.broadcasted_iota(jnp.int32, (chunk, LANE), 1) < SSD_P
    row_lo = lax.broadcasted_iota(jnp.int32, (2 * SSD_P, 1), 0) < SSD_P
    n_valid = min(valid, chunk)
    heads_per_group = SSD_H // SSD_G
    gsz = SSD_W // SSD_G

    for sq in range(nseq):
        for c in range(block // chunk):
            dt = _softplus(sm_ref[sq, c * chunk:(c + 1) * chunk, :] + dtb_ref[...])
            if valid < chunk:
                dt = jnp.where(row < valid, dt, 0.0)
            dta = _split3(dt * (-jnp.exp(alog_ref[...])))
            dt_scr[sq, c] = dt
            acs_scr[sq, c] = sum(_dot(tri, part) for part in dta)
            acsr_scr[sq, c] = sum(_dot_tn(part, tri_u) for part in dta)
            dtr_scr[sq, c] = sum(_dot_tn(part, eye) for part in _split3(dt))

    def one_seq(sq, s, rows):
        act = _silu(_causal_conv_chunk(xbc_ref[sq, rows, :], cv_scr.at[sq], cw_ref, cb_ref, chunk, n_valid))
        dt, acs, acs_r, dt_r = dt_scr[sq, s], acs_scr[sq, s], acsr_scr[sq, s], dtr_scr[sq, s]
        z = z_ref[sq, rows, :]
        for g in range(SSD_G):
            bg = act[:, SSD_W + g * SSD_N:SSD_W + (g + 1) * SSD_N]
            cg = act[:, SSD_W + SSD_G * SSD_N + g * SSD_N:SSD_W + SSD_G * SSD_N + (g + 1) * SSD_N]
            cb = _dot_nt(cg, bg)
            yield
            parts = []
            for pr in range(heads_per_group // 2):
                pi = g * (heads_per_group // 2) + pr
                xp = act[:, pi * LANE:(pi + 1) * LANE]
                st = s_scr[sq, pi]
                y_in, e_in, w_out, d_last = [], [], [], []
                for hh in range(2):
                    h = 2 * pi + hh
                    ac = acs[:, h:h + 1]
                    dec = jnp.exp(jnp.where(causal, ac - acs_r[h:h + 1, :], -jnp.inf))
                    y_in.append(_dot(cb * dec * dt_r[h:h + 1, :], xp))
                    a_last = ac[chunk - 1:chunk, :]
                    e_in.append(jnp.exp(ac))
                    w_out.append(jnp.exp(a_last - ac) * dt[:, h:h + 1])
                    d_last.append(jnp.exp(a_last))
                y = jnp.where(lane_lo, y_in[0], y_in[1])
                y = y + _dot_nt(cg, st) * jnp.where(lane_lo, e_in[0], e_in[1])
                y = y + drow_ref[:, pi * LANE:(pi + 1) * LANE] * xp
                x_sc = xp * jnp.where(lane_lo, w_out[0], w_out[1])
                s_scr[sq, pi] = st * jnp.where(row_lo, d_last[0], d_last[1]) + _dot_tn(x_sc, bg)
                parts.append(y)
                yield
            gs = slice(g * gsz, (g + 1) * gsz)
            yg = jnp.concatenate(parts, axis=1) * _silu(z[:, gs])
            o_ref[sq, rows, gs] = _head_rms(yg, ng_ref[:, gs]).astype(o_ref.dtype)

    def do_chunk(s, carry):
        rows = pl.ds(pl.multiple_of(s * chunk, chunk), chunk)
        _round_robin(one_seq(sq, s, rows) for sq in range(nseq))
        return carry

    lax.fori_loop(0, block // chunk, do_chunk, 0)

    @pl.when(blk == pl.num_programs(1) - 1)
    def _():
        sout_ref[...] = s_scr[...]
        cout_ref[...] = cv_scr[:, SUBLANE - (CONV_K - 1):SUBLANE, :]


def _ssd(proj, conv_w, conv_b, alog_row, dtb_row, d_row, ng, s0, c0, chunk, block, valid, nseq):
    nb, seq, _ = proj.shape
    n_pairs = SSD_H // 2

    def col(width, off):
        return pl.BlockSpec((nseq, block, width), lambda b, i: (b, i, off // width))

    def const(shape):
        return pl.BlockSpec(shape, lambda b, i: (0,) * len(shape))

    st_spec = pl.BlockSpec((nseq, n_pairs, 2 * SSD_P, SSD_N), lambda b, i: (b, 0, 0, 0))
    cv_spec = pl.BlockSpec((nseq, CONV_K - 1, SSD_CONV_W), lambda b, i: (b, 0, 0))
    kern = functools.partial(_ssd_kernel, chunk=chunk, block=block, valid=valid, nseq=nseq)
    o, s_new, c_new = pl.pallas_call(
        kern,
        grid=(nb // nseq, seq // block),
        in_specs=[col(SSD_CONV_W, CD_XBC), col(LANE, CD_SMALL), col(SSD_W, CD_Z),
                  const((CONV_K, SSD_CONV_W)), const((1, SSD_CONV_W)), const((1, LANE)), const((1, LANE)),
                  const((1, SSD_W)), const((1, SSD_W)), st_spec, cv_spec],
        out_specs=[pl.BlockSpec((nseq, block, SSD_W), lambda b, i: (b, i, 0)), st_spec, cv_spec],
        out_shape=[jax.ShapeDtypeStruct((nb, seq, SSD_W), _mixer_out_dtype(block)),
                   jax.ShapeDtypeStruct((nb, n_pairs, 2 * SSD_P, SSD_N), F32),
                   jax.ShapeDtypeStruct((nb, CONV_K - 1, SSD_CONV_W), F32)],
        scratch_shapes=[pltpu.VMEM((nseq, n_pairs, 2 * SSD_P, SSD_N), F32),
                        pltpu.VMEM((nseq, SUBLANE + chunk, SSD_CONV_W), F32),
                        pltpu.VMEM((nseq, block // chunk, chunk, LANE), F32),
                        pltpu.VMEM((nseq, block // chunk, chunk, LANE), F32),
                        pltpu.VMEM((nseq, block // chunk, LANE, chunk), F32),
                        pltpu.VMEM((nseq, block // chunk, LANE, chunk), F32)],
        compiler_params=_cparams(2), name="ssd",
    )(proj, proj, proj, conv_w, conv_b.reshape(1, SSD_CONV_W), alog_row, dtb_row, d_row,
      ng.reshape(1, SSD_W), s0.reshape(nb, n_pairs, 2 * SSD_P, SSD_N), c0)
    return o, s_new.reshape(nb, SSD_H, SSD_P, SSD_N), c_new


def _s5_prep_kernel(are_ref, aim_ref, ldt_ref, bre_ref, bim_ref, lbr_ref, lbi_ref, bbr_ref, bbi_ref):
    a_re, a_im = are_ref[...], aim_ref[...]
    dt = jnp.exp(ldt_ref[...])
    mag = jnp.exp(a_re * dt)
    lb_re, lb_im = mag * jnp.cos(a_im * dt), mag * jnp.sin(a_im * dt)
    nr, ni = lb_re - 1.0, lb_im
    den = a_re * a_re + a_im * a_im
    f_re = (nr * a_re + ni * a_im) / den
    f_im = (ni * a_re - nr * a_im) / den
    b_re, b_im = bre_ref[...], bim_ref[...]
    lbr_ref[...] = lb_re
    lbi_ref[...] = lb_im
    bbr_ref[...] = f_re * b_re - f_im * b_im
    bbi_ref[...] = f_re * b_im + f_im * b_re


def _s5_prep(a_re, a_im, log_dt, b_re, b_im):
    g3 = (S5_G, 1, S5_P)
    b3 = (S5_G, S5_GS, S5_P)
    return pl.pallas_call(
        _s5_prep_kernel,
        out_shape=[jax.ShapeDtypeStruct(g3, F32), jax.ShapeDtypeStruct(g3, F32),
                   jax.ShapeDtypeStruct(b3, F32), jax.ShapeDtypeStruct(b3, F32)],
        name="s5_prep",
    )(a_re.reshape(g3), a_im.reshape(g3), log_dt.reshape(S5_G, 1, 1),
      jnp.swapaxes(b_re, 1, 2), jnp.swapaxes(b_im, 1, 2))


def _block_diag(blocks):
    g, r, c = blocks.shape
    per = 8
    b = blocks.reshape(g // per, per, r, 1, c) * jnp.eye(per, dtype=blocks.dtype).reshape(1, per, 1, per, 1)
    return b.reshape(g // per, per * r, per * c)


def _s5_kernel(u_ref, wre_ref, wim_ref, cre_ref, cim_ref, lbr_ref, lbi_ref, d_ref, x0r_ref, x0i_ref,
               y_ref, xfr_ref, xfi_ref, xr_scr, xi_scr, sr_scr, si_scr, *, rows_per_step, steps):
    c = pl.program_id(2)

    @pl.when(c == 0)
    def _():
        xr_scr[...] = x0r_ref[0]
        xi_scr[...] = x0i_ref[0]

    u = u_ref[0]
    sr_scr[...] = _dot(u, wre_ref[0])
    si_scr[...] = _dot(u, wim_ref[0])
    l_re, l_im = lbr_ref[...], lbi_ref[...]

    def step(t, carry):
        rows = pl.ds(pl.multiple_of(t * rows_per_step, rows_per_step), rows_per_step)
        xr, xi = xr_scr[...], xi_scr[...]
        nr = l_re * xr - l_im * xi + sr_scr[rows, :]
        ni = l_re * xi + l_im * xr + si_scr[rows, :]
        xr_scr[...] = nr
        xi_scr[...] = ni
        sr_scr[rows, :] = nr
        si_scr[rows, :] = ni
        return carry

    lax.fori_loop(0, steps, step, 0)
    y_ref[0] = _dot(sr_scr[...], cre_ref[0]) - _dot(si_scr[...], cim_ref[0]) + d_ref[...] * u

    @pl.when(c == pl.num_programs(2) - 1)
    def _():
        xfr_ref[0] = xr_scr[...]
        xfi_ref[0] = xi_scr[...]


def _s5(proj, w_re, w_im, c_re, c_im, lb_re, lb_im, d, x0_re, x0_im, rows_per_step, steps):
    ng, n_tok, _ = proj.shape
    nj = S5_W // LANE
    sw = S5_STATE // nj
    cr = rows_per_step * steps
    kern = functools.partial(_s5_kernel, rows_per_step=rows_per_step, steps=steps)
    x_spec = pl.BlockSpec((1, rows_per_step, sw), lambda g, j, c: (g, 0, j))
    return pl.pallas_call(
        kern,
        grid=(ng, nj, n_tok // cr),
        in_specs=[pl.BlockSpec((1, cr, LANE), lambda g, j, c: (g, c, CD_U // LANE + j)),
                  pl.BlockSpec((1, LANE, sw), lambda g, j, c: (j, 0, 0)),
                  pl.BlockSpec((1, LANE, sw), lambda g, j, c: (j, 0, 0)),
                  pl.BlockSpec((1, sw, LANE), lambda g, j, c: (j, 0, 0)),
                  pl.BlockSpec((1, sw, LANE), lambda g, j, c: (j, 0, 0)),
                  pl.BlockSpec((1, sw), lambda g, j, c: (0, j)),
                  pl.BlockSpec((1, sw), lambda g, j, c: (0, j)),
                  pl.BlockSpec((1, LANE), lambda g, j, c: (0, j)),
                  x_spec, x_spec],
        out_specs=[pl.BlockSpec((1, cr, LANE), lambda g, j, c: (g, c, j)), x_spec, x_spec],
        out_shape=[jax.ShapeDtypeStruct((ng, n_tok, S5_W), F32),
                   jax.ShapeDtypeStruct((ng, rows_per_step, S5_STATE), F32),
                   jax.ShapeDtypeStruct((ng, rows_per_step, S5_STATE), F32)],
        scratch_shapes=[pltpu.VMEM((rows_per_step, sw), F32), pltpu.VMEM((rows_per_step, sw), F32),
                        pltpu.VMEM((cr, sw), F32), pltpu.VMEM((cr, sw), F32)],
        compiler_params=_cparams(3), name="s5_scan",
    )(proj, w_re, w_im, c_re, c_im, lb_re, lb_im, d.reshape(1, S5_W), x0_re, x0_im)


def _s5_pow_kernel(lbr_ref, lbi_ref, pr_ref, pi_ref, *, n_rows):
    l_re, l_im = lbr_ref[...], lbi_ref[...]
    row = lax.broadcasted_iota(jnp.int32, (SUBLANE, 1), 0)
    p_re, p_im = l_re, l_im
    b_re = jnp.broadcast_to(l_re, (SUBLANE, l_re.shape[1]))
    b_im = jnp.broadcast_to(l_im, (SUBLANE, l_re.shape[1]))
    for r in range(1, SUBLANE):
        p_re, p_im = p_re * l_re - p_im * l_im, p_re * l_im + p_im * l_re
        b_re = jnp.where(row >= r, p_re, b_re)
        b_im = jnp.where(row >= r, p_im, b_im)
    q_re, q_im = jnp.ones_like(l_re), jnp.zeros_like(l_re)
    for a in range(n_rows // SUBLANE):
        rows = slice(a * SUBLANE, (a + 1) * SUBLANE)
        pr_ref[rows, :] = b_re * q_re - b_im * q_im
        pi_ref[rows, :] = b_re * q_im + b_im * q_re
        q_re, q_im = q_re * p_re - q_im * p_im, q_re * p_im + q_im * p_re


def _s5_pow_table(lb_re, lb_im, n_rows):
    shape = jax.ShapeDtypeStruct((n_rows, S5_STATE), F32)
    return pl.pallas_call(functools.partial(_s5_pow_kernel, n_rows=n_rows), out_shape=[shape, shape],
                          name="s5_pow")(lb_re, lb_im)


def _s5_seg_kernel(u_ref, wre_ref, wim_ref, cre_ref, cim_ref, pr_ref, pi_ref, d_ref, x0r_ref, x0i_ref,
                   y_ref, xfr_ref, xfi_ref, sr_scr, si_scr, *, seg_len):
    n_seg = SUBLANE
    n_lane_blk = sr_scr.shape[0]
    sw = n_lane_blk * LANE
    lane_blks = [slice(c * LANE, (c + 1) * LANE) for c in range(n_lane_blk)]

    def put(scr, rows, val):
        for c, ls in enumerate(lane_blks):
            scr[c, rows, :] = val[:, ls]

    def get(scr, rows):
        return jnp.concatenate([scr[c, rows, :] for c in range(n_lane_blk)], axis=1)

    for s in range(n_seg):
        us = u_ref[0, s * seg_len:(s + 1) * seg_len, :]
        put(sr_scr, pl.ds(s, seg_len, stride=n_seg), _dot(us, wre_ref[0]))
        put(si_scr, pl.ds(s, seg_len, stride=n_seg), _dot(us, wim_ref[0]))
    l_re, l_im = pr_ref[0:1, :], pi_ref[0:1, :]

    def step(t, carry):
        xr, xi = carry
        rows = pl.ds(pl.multiple_of(t * n_seg, n_seg), n_seg)
        nr = l_re * xr - l_im * xi + get(sr_scr, rows)
        ni = l_re * xi + l_im * xr + get(si_scr, rows)
        put(sr_scr, rows, nr)
        put(si_scr, rows, ni)
        return nr, ni

    zero = jnp.zeros((n_seg, sw), F32)
    end_re, end_im = lax.fori_loop(0, seg_len, step, (zero, zero), unroll=4)
    ln_re, ln_im = pr_ref[seg_len - 1:seg_len, :], pi_ref[seg_len - 1:seg_len, :]
    p_re, p_im = pr_ref[...], pi_ref[...]
    x_re, x_im = x0r_ref[0], x0i_ref[0]
    for s in range(n_seg):
        loc_re = get(sr_scr, pl.ds(s, seg_len, stride=n_seg))
        loc_im = get(si_scr, pl.ds(s, seg_len, stride=n_seg))
        t_re = loc_re + p_re * x_re - p_im * x_im
        t_im = loc_im + p_re * x_im + p_im * x_re
        us = u_ref[0, s * seg_len:(s + 1) * seg_len, :]
        y_ref[0, s * seg_len:(s + 1) * seg_len, :] = (_dot(t_re, cre_ref[0]) - _dot(t_im, cim_ref[0])
                                                      + d_ref[...] * us)
        e_re, e_im = end_re[s:s + 1, :], end_im[s:s + 1, :]
        x_re, x_im = e_re + ln_re * x_re - ln_im * x_im, e_im + ln_re * x_im + ln_im * x_re
    xfr_ref[0] = x_re
    xfi_ref[0] = x_im


def _s5_seg(proj, w_re, w_im, c_re, c_im, pow_re, pow_im, d, x0_re, x0_im):
    nb, seq, _ = proj.shape
    nj = S5_W // LANE
    sw = S5_STATE // nj
    seg_len = seq // SUBLANE
    x_spec = pl.BlockSpec((1, 1, sw), lambda b, j: (b, 0, j))
    return pl.pallas_call(
        functools.partial(_s5_seg_kernel, seg_len=seg_len),
        grid=(nb, nj),
        in_specs=[pl.BlockSpec((1, seq, LANE), lambda b, j: (b, 0, CD_U // LANE + j)),
                  pl.BlockSpec((1, LANE, sw), lambda b, j: (j, 0, 0)),
                  pl.BlockSpec((1, LANE, sw), lambda b, j: (j, 0, 0)),
                  pl.BlockSpec((1, sw, LANE), lambda b, j: (j, 0, 0)),
                  pl.BlockSpec((1, sw, LANE), lambda b, j: (j, 0, 0)),
                  pl.BlockSpec((seg_len, sw), lambda b, j: (0, j)),
                  pl.BlockSpec((seg_len, sw), lambda b, j: (0, j)),
                  pl.BlockSpec((1, LANE), lambda b, j: (0, j)),
                  x_spec, x_spec],
        out_specs=[pl.BlockSpec((1, seq, LANE), lambda b, j: (b, 0, j)), x_spec, x_spec],
        out_shape=[jax.ShapeDtypeStruct((nb, seq, S5_W), F32),
                   jax.ShapeDtypeStruct((nb, 1, S5_STATE), F32),
                   jax.ShapeDtypeStruct((nb, 1, S5_STATE), F32)],
        scratch_shapes=[pltpu.VMEM((sw // LANE, seq, LANE), F32), pltpu.VMEM((sw // LANE, seq, LANE), F32)],
        compiler_params=_cparams(2), name="s5_seg",
    )(proj, w_re, w_im, c_re, c_im, pow_re, pow_im, d.reshape(1, S5_W), x0_re, x0_im)


def _lane_row(vec, lane0):
    return jnp.zeros((1, LANE), F32).at[0, lane0:lane0 + vec.shape[0]].set(vec.astype(F32))


def _realign_kernel(w_ref, o_ref, *, segments):
    o_ref[...] = jnp.zeros_like(o_ref)
    for src, rows, dst in segments:
        o_ref[dst:dst + rows, :] = w_ref[src:src + rows, :].astype(BF16)


def _realign_weights(w, segments, n_out):
    wt = jnp.swapaxes(w, 1, 2)[0]
    n_src, k = wt.shape
    assert all(v % (2 * SUBLANE) == 0 for seg in segments for v in seg)
    cols = 256
    return pl.pallas_call(
        functools.partial(_realign_kernel, segments=segments),
        grid=(k // cols,),
        in_specs=[pl.BlockSpec((n_src, cols), lambda i: (0, i))],
        out_specs=pl.BlockSpec((n_out, cols), lambda i: (0, i)),
        out_shape=jax.ShapeDtypeStruct((n_out, k), BF16),
        compiler_params=_cparams(1), name="realign_weights",
    )(wt)


def _prep_params(p):
    q = {}
    o_lr = 2 * GLA_QK + GLA_V
    o_r = o_lr + GLA_LR
    o_qkv = o_r + GLA_V
    o_a = o_qkv + GDN_CONV_W
    o_g = o_a + 2 * GDN_H
    q['w_in_ab'] = _realign_weights(
        p['w_in_ab'], [(0, o_lr, AB_Q), (o_r, GLA_V, AB_R), (o_qkv, GDN_CONV_W, AB_QKV), (o_g, GDN_V, AB_G),
                       (o_lr, GLA_LR, AB_SMALL + AB_LR_LANE), (o_a, 2 * GDN_H, AB_SMALL + AB_A_LANE)], AB_N)
    q['gla_w2'] = jnp.zeros((LANE, GLA_QK), F32).at[:GLA_LR].set(p['gla_w2'][0])
    q['gdn_alog'] = _lane_row(p['gdn_A_log'][0], AB_A_LANE)
    q['gdn_dtb'] = _lane_row(p['gdn_dt_bias'][0], AB_A_LANE)
    q['w_out_ab'] = p['w_out_ab'].astype(BF16)
    o_xbc = SSD_W
    o_dt = o_xbc + SSD_CONV_W
    o_u = o_dt + SSD_H
    q['w_in_cd'] = _realign_weights(
        p['w_in_cd'], [(o_xbc, SSD_CONV_W, CD_XBC), (0, SSD_W, CD_Z), (o_u, S5_W, CD_U), (o_dt, SSD_H, CD_SMALL)],
        CD_N)
    q['ssd_alog'] = _lane_row(p['ssd_A_log'][0], 0)
    q['ssd_dtb'] = _lane_row(p['ssd_dt_bias'][0], 0)
    q['ssd_d_row'] = jnp.repeat(p['ssd_D'][0].astype(F32), SSD_P).reshape(1, SSD_W)
    q['w_out_cd'] = p['w_out_cd'].astype(BF16)
    lb_re, lb_im, bb_re, bb_im = _s5_prep(p['s5_A_re'][0], p['s5_A_im'][0], p['s5_log_dt'][0],
                                          p['s5_B_re'][0], p['s5_B_im'][0])
    q['s5_lb_re'], q['s5_lb_im'] = lb_re.reshape(1, S5_STATE), lb_im.reshape(1, S5_STATE)
    q['s5_w_re'], q['s5_w_im'] = _block_diag(bb_re), _block_diag(bb_im)
    q['s5_c_re'] = _block_diag(jnp.swapaxes(p['s5_C_re'][0], 1, 2))
    q['s5_c_im'] = _block_diag(jnp.swapaxes(p['s5_C_im'][0], 1, 2))
    q['s5_glu_w'] = p['s5_glu_w'][0].astype(BF16)
    q['w_ffn_down'] = p['w_ffn_down'].astype(BF16)
    return q


def _trunk(x, mods, grp, seq_shape, state, p, q):
    nb, seq_len, valid = seq_shape
    s_gla, s_gdn, s_gdnc, s_ssd, s_ssdc, s_re, s_im, s_ffn = state
    prompt = not grp.per_token_mod

    def to_seq(t):
        if prompt:
            return t.reshape(nb, seq_len, t.shape[-1])
        t = jnp.swapaxes(t.reshape(valid, nb, t.shape[-1]), 0, 1)
        return jnp.pad(t, ((0, 0), (0, seq_len - valid), (0, 0)))

    def from_seq(t):
        if prompt:
            return t.reshape(nb * seq_len, t.shape[-1])
        return jnp.swapaxes(t[:, :valid], 0, 1).reshape(valid * nb, t.shape[-1])

    blk = MIX_BLOCK if prompt else seq_len
    chunks = (GLA_CHUNK, GDN_CHUNK, SSD_CHUNK) if prompt else (seq_len,) * 3
    new = {}

    h = _norm_mod(x, p['g_mix'][0], mods[0], grp, 1, 0)
    proj = to_seq(_matmul(h, q['w_in_ab'], grp, AB_N // 3))
    o_a, new['gla'] = _gla(proj, q['gla_w2'], p['gla_b2'][0], p['gla_norm_g'][0], s_gla, chunks[0],
                            GLA_GROUP if prompt else 1, blk, valid, 1 if prompt else SAMPLE_SEQS)
    o_b, new['gdn'], new['gdnc'] = _gdn(proj, p['gdn_conv_w'][0], p['gdn_conv_b'][0], q['gdn_alog'], q['gdn_dtb'],
                                        p['gdn_norm_g'][0], s_gdn, s_gdnc, chunks[1], blk, valid,
                                        1 if prompt else SAMPLE_SEQS)
    x, h = _mm_residual([(from_seq(o_a), q['w_out_ab'], 0, 0), (from_seq(o_b), q['w_out_ab'], 0, 1)],
                        x, mods[0], grp, 2, D_MODEL, norm=(p['g_ffn'][0], 4, 3))
    act, new['ffn0'] = _ffn_up(h, p['w_ffn_up'], p['ffn_conv_w'], p['ffn_conv_b'], 0, s_ffn[0],
                               grp.retiled(FFN_TILE))
    x = _mm_residual([(act, q['w_ffn_down'], 0, 0)], x, mods[0], grp, 5, 1024)

    h = _norm_mod(x, p['g_mix'][1], mods[1], grp, 1, 0)
    proj2 = _matmul(h, q['w_in_cd'], grp, CD_N // 2)
    proj = to_seq(proj2)
    o_c, new['ssd'], new['ssdc'] = _ssd(proj, p['ssd_conv_w'][0], p['ssd_conv_b'][0], q['ssd_alog'], q['ssd_dtb'],
                                        q['ssd_d_row'], p['ssd_norm_g'][0], s_ssd, s_ssdc, chunks[2], blk, valid,
                                        1 if prompt else SAMPLE_SEQS)
    if prompt:
        pow_re, pow_im = _s5_pow_table(q['s5_lb_re'], q['s5_lb_im'], seq_len // SUBLANE)
        yd, new['re'], new['im'] = _s5_seg(proj, q['s5_w_re'], q['s5_w_im'], q['s5_c_re'], q['s5_c_im'],
                                           pow_re, pow_im, p['s5_D'][0], s_re, s_im)
    else:
        yd, new['re'], new['im'] = _s5(proj2.reshape(1, grp.n_tok, CD_N), q['s5_w_re'], q['s5_w_im'],
                                       q['s5_c_re'], q['s5_c_im'], q['s5_lb_re'], q['s5_lb_im'], p['s5_D'][0],
                                       s_re, s_im, nb, valid)
    o_d = _s5_glu(yd.reshape(grp.n_tok, S5_W), q['s5_glu_w'], p['s5_glu_b'][0], grp)
    x, h = _mm_residual([(from_seq(o_c), q['w_out_cd'], 0, 0), (o_d, q['w_out_cd'], 0, 1)], x, mods[1], grp, 2,
                        D_MODEL, norm=(p['g_ffn'][1], 4, 3))
    act, new['ffn1'] = _ffn_up(h, p['w_ffn_up'], p['ffn_conv_w'], p['ffn_conv_b'], 1, s_ffn[1],
                               grp.retiled(FFN_TILE))
    x = _mm_residual([(act, q['w_ffn_down'], 1, 0)], x, mods[1], grp, 5, 1024)
    return _final_rms(x, p['g_final'], grp), new


def kernel(x_prompt, x_sample, c_prompt, c_sample, state_gla, state_gdn, state_gdn_conv, state_ssd, state_ssd_conv, state_s5_re, state_s5_im, state_ffn_conv, w_ada, b_ada, g_mix, g_ffn, w_in_ab, gla_w2, gla_b2, gla_norm_g, gdn_conv_w, gdn_conv_b, gdn_A_log, gdn_dt_bias, gdn_norm_g, w_out_ab, w_in_cd, ssd_conv_w, ssd_conv_b, ssd_A_log, ssd_dt_bias, ssd_D, ssd_norm_g, s5_A_re, s5_A_im, s5_B_re, s5_B_im, s5_C_re, s5_C_im, s5_D, s5_log_dt, s5_glu_w, s5_glu_b, w_out_cd, w_ffn_up, ffn_conv_w, ffn_conv_b, w_ffn_down, g_final):
    p = dict(g_mix=g_mix, g_ffn=g_ffn, w_in_ab=w_in_ab, gla_w2=gla_w2, gla_b2=gla_b2, gla_norm_g=gla_norm_g,
             gdn_conv_w=gdn_conv_w, gdn_conv_b=gdn_conv_b, gdn_A_log=gdn_A_log, gdn_dt_bias=gdn_dt_bias,
             gdn_norm_g=gdn_norm_g, w_out_ab=w_out_ab, w_in_cd=w_in_cd, ssd_conv_w=ssd_conv_w,
             ssd_conv_b=ssd_conv_b, ssd_A_log=ssd_A_log, ssd_dt_bias=ssd_dt_bias, ssd_D=ssd_D,
             ssd_norm_g=ssd_norm_g, s5_A_re=s5_A_re, s5_A_im=s5_A_im, s5_B_re=s5_B_re, s5_B_im=s5_B_im,
             s5_C_re=s5_C_re, s5_C_im=s5_C_im, s5_D=s5_D, s5_log_dt=s5_log_dt, s5_glu_w=s5_glu_w,
             s5_glu_b=s5_glu_b, w_out_cd=w_out_cd, w_ffn_up=w_ffn_up, ffn_conv_w=ffn_conv_w,
             ffn_conv_b=ffn_conv_b, w_ffn_down=w_ffn_down, g_final=g_final)
    bp, lp, d = x_prompt.shape
    bs, ls, _ = x_sample.shape
    q = _prep_params(p)

    bp_pad = -(-bp // SUBLANE) * SUBLANE
    c_all = jnp.concatenate([c_prompt, jnp.zeros((bp_pad - bp, d), F32), c_sample], axis=0)
    mod = _ada_mod(c_all, w_ada, b_ada)
    depth = w_ada.shape[0]
    mods_p = [mod[l, :bp].reshape(bp, 1, 6 * d) for l in range(depth)]
    mods_s = [jnp.tile(mod[l, bp_pad:], (ls, 1)).reshape(1, ls * bs, 6 * d) for l in range(depth)]

    tile_p = 512
    grp_p = _Group(bp * lp, tile_p, False, lp // tile_p, 1)
    zeros = lambda *shape: jnp.zeros(shape, F32)
    state_p = (zeros(bp, GLA_H, GLA_DK, GLA_DV), zeros(bp, GDN_H, GDN_DK, GDN_DV),
               zeros(bp, CONV_K - 1, GDN_CONV_W), zeros(bp, SSD_H, SSD_P, SSD_N),
               zeros(bp, CONV_K - 1, SSD_CONV_W), zeros(bp, 1, S5_STATE), zeros(bp, 1, S5_STATE),
               zeros(depth, bp, FFN_K - 1, 2 * D_FF))
    y_p, new_p = _trunk(x_prompt.reshape(bp * lp, d), mods_p, grp_p, (bp, lp, lp), state_p, p, q)

    grp_s = _Group(bs * ls, bs * ls, True, 1, bs)
    ffn_hist_s = jnp.swapaxes(state_ffn_conv, 1, 2).reshape(depth, 1, (FFN_K - 1) * bs, 2 * D_FF)
    state_s = (state_gla[0], state_gdn[0], state_gdn_conv[0], state_ssd[0], state_ssd_conv[0],
               state_s5_re.reshape(1, bs, S5_STATE), state_s5_im.reshape(1, bs, S5_STATE), ffn_hist_s)
    x_s = jnp.swapaxes(x_sample, 0, 1).reshape(ls * bs, d)
    y_s, new_s = _trunk(x_s, mods_s, grp_s, (bs, SAMPLE_PAD, ls), state_s, p, q)
    y_s = jnp.swapaxes(y_s.reshape(ls, bs, d), 0, 1)

    ffn_p = jnp.stack([new_p['ffn0'], new_p['ffn1']])
    ffn_s = jnp.stack([jnp.swapaxes(new_s[k].reshape(FFN_K - 1, bs, 2 * D_FF), 0, 1) for k in ('ffn0', 'ffn1')])
    s5_shape = lambda t, nb: t.reshape(1, nb, S5_G, S5_P)
    return (y_p.reshape(bp, lp, d), y_s,
            new_p['gla'][None], new_s['gla'][None], new_p['gdn'][None], new_s['gdn'][None],
            new_p['gdnc'][None], new_s['gdnc'][None], new_p['ssd'][None], new_s['ssd'][None],
            new_p['ssdc'][None], new_s['ssdc'][None],
            s5_shape(new_p['re'], bp), s5_shape(new_s['re'], bs), s5_shape(new_p['im'], bp), s5_shape(new_s['im'], bs),
            ffn_p, ffn_s)
```

```python
import functools
import math

import jax
import jax.numpy as jnp
from jax import lax
from jax.experimental import pallas as pl
from jax.experimental.pallas import tpu as pltpu

F32 = jnp.float32
BF16 = jnp.bfloat16
HIGHEST = lax.Precision.HIGHEST
EPS = 1e-6

D_MODEL = 2048
GLA_H, GLA_DK, GLA_DV, GLA_LR = 4, 128, 256, 16
GLA_GATE_NORM = 16.0
GLA_QK, GLA_V = GLA_H * GLA_DK, GLA_H * GLA_DV
GDN_H, GDN_DK, GDN_DV = 8, 128, 128
GDN_QK, GDN_V = GDN_H * GDN_DK, GDN_H * GDN_DV
CONV_K = 4
GDN_CONV_W = 2 * GDN_QK + GDN_V
SSD_P, SSD_H, SSD_G, SSD_N = 64, 16, 2, 128
SSD_W = SSD_H * SSD_P
SSD_CONV_W = SSD_W + 2 * SSD_G * SSD_N
S5_W, S5_GS, S5_G, S5_P = 1024, 16, 64, 64
S5_STATE = S5_G * S5_P
D_FF = 5632
FFN_K = 3

LANE = 128
SUBLANE = 8
VMEM_LIMIT = 48 * 1024 * 1024

AB_Q, AB_K, AB_V, AB_R, AB_QKV, AB_G, AB_SMALL, AB_N = 0, 512, 1024, 2048, 3072, 6144, 7168, 7296
AB_LR_LANE, AB_A_LANE, AB_B_LANE = 0, 16, 24
CD_XBC, CD_SMALL, CD_Z, CD_U, CD_N = 0, 1536, 2048, 3072, 4096

MIX_BLOCK = 256
GLA_CHUNK, GDN_CHUNK, SSD_CHUNK = 16, 128, 128
GLA_GROUP = 4
FFN_TILE = 1024
NORM_TILE = 1024
FFN_SUB = 256
INV_BLOCK = 16
SAMPLE_PAD = 8
SAMPLE_SEQS = 8


def _cparams(n_axes):
    return pltpu.CompilerParams(dimension_semantics=("arbitrary",) * n_axes, vmem_limit_bytes=VMEM_LIMIT)


def _sigmoid(x):
    return 1.0 / (1.0 + jnp.exp(-x))


def _silu(x):
    return x * _sigmoid(x)


def _softplus(x):
    return jnp.maximum(x, 0.0) + jnp.log(1.0 + jnp.exp(-jnp.abs(x)))


def _gelu_tanh(x):
    return 0.5 * x * (1.0 + jnp.tanh(math.sqrt(2.0 / math.pi) * (x + 0.044715 * (x * x * x))))


def _dot(a, b, precision=None):
    return jnp.dot(a, b, precision=precision, preferred_element_type=F32)


def _dot_nt(a, b, precision=None):
    return lax.dot_general(a, b, (((1,), (1,)), ((), ())), precision=precision, preferred_element_type=F32)


def _dot_tn(a, b, precision=None):
    return lax.dot_general(a, b, (((0,), (0,)), ((), ())), precision=precision, preferred_element_type=F32)


def _head_rms(o, g):
    return o * lax.rsqrt(jnp.mean(o * o, -1, keepdims=True) + EPS) * g


class _Group:
    def __init__(self, n_tok, tile, per_token_mod, tiles_per_seq, conv_shift):
        self.n_tok = n_tok
        self.tile = tile
        self.n_tiles = n_tok // tile
        self.per_token_mod = per_token_mod
        self.tiles_per_seq = tiles_per_seq
        self.conv_shift = conv_shift

    def retiled(self, tile):
        seq_rows = self.tile * self.tiles_per_seq
        if seq_rows % tile or self.n_tok % tile:
            return self
        return _Group(self.n_tok, tile, self.per_token_mod, seq_rows // tile, self.conv_shift)

    def mod_spec(self, width, col_block, m_axis):
        if self.per_token_mod:
            return pl.BlockSpec((1, self.tile, width), lambda *g: (0, g[m_axis], col_block(*g)))
        tps = self.tiles_per_seq
        return pl.BlockSpec((1, 1, width), lambda *g: (g[m_axis] // tps, 0, col_block(*g)))


def _ada_kernel(c_ref, w_ref, b_ref, o_ref):
    cs = _silu(c_ref[...]).astype(BF16)
    o_ref[0] = _dot(cs, w_ref[0].astype(BF16)) + b_ref[0]


def _ada_mod(c, w_ada, b_ada):
    depth, d, n = w_ada.shape
    rows = c.shape[0]
    tn = 1024
    return pl.pallas_call(
        _ada_kernel,
        grid=(depth, n // tn),
        in_specs=[pl.BlockSpec((rows, d), lambda l, j: (0, 0)),
                  pl.BlockSpec((1, d, tn), lambda l, j: (l, 0, j)),
                  pl.BlockSpec((1, 1, tn), lambda l, j: (l, 0, j))],
        out_specs=pl.BlockSpec((1, rows, tn), lambda l, j: (l, 0, j)),
        out_shape=jax.ShapeDtypeStruct((depth, rows, n), F32),
        compiler_params=_cparams(2), name="ada_mod",
    )(c, w_ada, b_ada.reshape(depth, 1, n))


def _norm_mod_kernel(x_ref, g_ref, sc_ref, sh_ref, o_ref):
    x = x_ref[...]
    y = x * lax.rsqrt(jnp.mean(x * x, -1, keepdims=True) + EPS) * g_ref[...]
    o_ref[...] = (y * (1.0 + sc_ref[0]) + sh_ref[0]).astype(BF16)


def _norm_mod(x, g, mod, grp, sc_blk, sh_blk):
    d = x.shape[1]
    return pl.pallas_call(
        _norm_mod_kernel,
        grid=(grp.n_tiles,),
        in_specs=[pl.BlockSpec((grp.tile, d), lambda i: (i, 0)),
                  pl.BlockSpec((1, d), lambda i: (0, 0)),
                  grp.mod_spec(d, lambda i: sc_blk, 0),
                  grp.mod_spec(d, lambda i: sh_blk, 0)],
        out_specs=pl.BlockSpec((grp.tile, d), lambda i: (i, 0)),
        out_shape=jax.ShapeDtypeStruct(x.shape, BF16),
        compiler_params=_cparams(1), name="norm_mod",
    )(x, g.reshape(1, d), mod, mod)


def _rms_kernel(x_ref, g_ref, o_ref):
    x = x_ref[...]
    o_ref[...] = x * lax.rsqrt(jnp.mean(x * x, -1, keepdims=True) + EPS) * g_ref[...]


def _final_rms(x, g, grp):
    d = x.shape[1]
    return pl.pallas_call(
        _rms_kernel,
        grid=(grp.n_tiles,),
        in_specs=[pl.BlockSpec((grp.tile, d), lambda i: (i, 0)), pl.BlockSpec((1, d), lambda i: (0, 0))],
        out_specs=pl.BlockSpec((grp.tile, d), lambda i: (i, 0)),
        out_shape=jax.ShapeDtypeStruct(x.shape, F32),
        compiler_params=_cparams(1), name="final_rms",
    )(x, g.reshape(1, d))


def _matmul_kernel(a_ref, wt_ref, o_ref):
    o_ref[...] = _dot_nt(a_ref[...], wt_ref[...])


def _matmul(a, wt, grp, tn):
    n, k = wt.shape
    return pl.pallas_call(
        _matmul_kernel,
        grid=(n // tn, grp.n_tiles),
        in_specs=[pl.BlockSpec((grp.tile, k), lambda j, i: (i, 0)),
                  pl.BlockSpec((tn, k), lambda j, i: (j, 0))],
        out_specs=pl.BlockSpec((grp.tile, tn), lambda j, i: (i, j)),
        out_shape=jax.ShapeDtypeStruct((a.shape[0], n), F32),
        compiler_params=_cparams(2), name="matmul",
    )(a, wt)


def _mm_res_kernel(*refs, n_pairs, with_norm):
    x_ref, gate_ref = refs[2 * n_pairs:2 * n_pairs + 2]
    y = _dot(refs[0][...].astype(BF16), refs[1][0])
    for p in range(1, n_pairs):
        y = y + _dot(refs[2 * p][...].astype(BF16), refs[2 * p + 1][0])
    x_new = x_ref[...] + gate_ref[0] * y
    if not with_norm:
        o_ref, = refs[2 * n_pairs + 2:]
        o_ref[...] = x_new
        return
    g_ref, sc_ref, sh_ref, o_ref, h_ref = refs[2 * n_pairs + 2:]
    o_ref[...] = x_new
    normed = x_new * lax.rsqrt(jnp.mean(x_new * x_new, -1, keepdims=True) + EPS) * g_ref[...]
    h_ref[...] = (normed * (1.0 + sc_ref[0]) + sh_ref[0]).astype(BF16)


def _mm_residual(pairs, x, mod, grp, gate_blk, tn, norm=None):
    n = x.shape[1]
    if norm is not None:
        assert tn == n
        return _mm_residual_norm(pairs, x, mod, grp, gate_blk, norm)
    in_specs, args = [], []
    for a, w, layer, row_blk in pairs:
        k = a.shape[1]
        in_specs += [pl.BlockSpec((grp.tile, k), lambda j, i: (i, 0)),
                     pl.BlockSpec((1, k, tn), lambda j, i, layer=layer, row_blk=row_blk: (layer, row_blk, j))]
        args += [a, w]
    in_specs += [pl.BlockSpec((grp.tile, tn), lambda j, i: (i, j)),
                 grp.mod_spec(tn, lambda j, i: gate_blk * (n // tn) + j, 1)]
    return pl.pallas_call(
        functools.partial(_mm_res_kernel, n_pairs=len(pairs), with_norm=False),
        grid=(n // tn, grp.n_tiles),
        in_specs=in_specs,
        out_specs=pl.BlockSpec((grp.tile, tn), lambda j, i: (i, j)),
        out_shape=jax.ShapeDtypeStruct(x.shape, F32),
        compiler_params=_cparams(2), name="mm_residual",
    )(*args, x, mod)


def _mm_residual_norm(pairs, x, mod, grp, gate_blk, norm):
    n = x.shape[1]
    gain, sc_blk, sh_blk = norm
    in_specs, args = [], []
    for a, w, layer, row_blk in pairs:
        k = a.shape[1]
        in_specs += [pl.BlockSpec((grp.tile, k), lambda j, i: (i, 0)),
                     pl.BlockSpec((1, k, n), lambda j, i, layer=layer, row_blk=row_blk: (layer, row_blk, 0))]
        args += [a, w]
    row_spec = pl.BlockSpec((grp.tile, n), lambda j, i: (i, 0))
    in_specs += [row_spec, grp.mod_spec(n, lambda j, i: gate_blk, 1), pl.BlockSpec((1, n), lambda j, i: (0, 0)),
                 grp.mod_spec(n, lambda j, i: sc_blk, 1), grp.mod_spec(n, lambda j, i: sh_blk, 1)]
    return pl.pallas_call(
        functools.partial(_mm_res_kernel, n_pairs=len(pairs), with_norm=True),
        grid=(1, grp.n_tiles),
        in_specs=in_specs,
        out_specs=[row_spec, row_spec],
        out_shape=[jax.ShapeDtypeStruct(x.shape, F32), jax.ShapeDtypeStruct(x.shape, BF16)],
        compiler_params=_cparams(2), name="mm_residual_norm",
    )(*args, x, mod, gain.reshape(1, n), mod, mod)


def _ffn_up_kernel(h_ref, wa_ref, wg_ref, cwa_ref, cwg_ref, cba_ref, cbg_ref, ha_ref, hg_ref,
                   act_ref, sta_ref, stg_ref, scr_a, scr_g, wba_scr, wbg_scr, *, shift, tile, sub, tiles_per_seq):
    i = pl.program_id(1)
    hist = (FFN_K - 1) * shift
    base = -(-hist // SUBLANE) * SUBLANE

    @pl.when(i == 0)
    def _():
        wba_scr[...] = wa_ref[0].astype(BF16)
        wbg_scr[...] = wg_ref[0].astype(BF16)

    @pl.when(i % tiles_per_seq == 0)
    def _():
        scr_a[base - hist:base, :] = ha_ref[0]
        scr_g[base - hist:base, :] = hg_ref[0]

    def conv(scr, cw_ref, cb_ref, r0):
        y = cb_ref[0]
        for j in range(FFN_K):
            lo = base + r0 - (FFN_K - 1 - j) * shift
            y = y + scr[lo:lo + sub, :] * cw_ref[0, j:j + 1, :]
        return y

    def project(r0):
        h = h_ref[r0:r0 + sub, :]
        scr_a[base + r0:base + r0 + sub, :] = _dot(h, wba_scr[...])
        scr_g[base + r0:base + r0 + sub, :] = _dot(h, wbg_scr[...])

    project(0)
    for r0 in range(0, tile, sub):
        if r0 + sub < tile:
            project(r0 + sub)
        a = conv(scr_a, cwa_ref, cba_ref, r0)
        g = conv(scr_g, cwg_ref, cbg_ref, r0)
        act_ref[r0:r0 + sub, :] = (_silu(g) * a).astype(BF16)
    last_a = scr_a[base + tile - hist:base + tile, :]
    last_g = scr_g[base + tile - hist:base + tile, :]
    sta_ref[0] = last_a
    stg_ref[0] = last_g
    scr_a[base - hist:base, :] = last_a
    scr_g[base - hist:base, :] = last_g


def _ffn_up(h, w_up, conv_w, conv_b, layer, hist0, grp):
    d = h.shape[1]
    tn = 512
    nj = D_FF // tn
    shift = grp.conv_shift
    hist = (FFN_K - 1) * shift
    base = -(-hist // SUBLANE) * SUBLANE
    n_seq = grp.n_tiles // grp.tiles_per_seq
    tps = grp.tiles_per_seq
    cb = conv_b.reshape(conv_b.shape[0], 1, 2 * D_FF)
    kern = functools.partial(_ffn_up_kernel, shift=shift, tile=grp.tile, sub=min(FFN_SUB, grp.tile),
                             tiles_per_seq=tps)
    assert grp.tile % min(FFN_SUB, grp.tile) == 0
    act, st_a, st_g = pl.pallas_call(
        kern,
        grid=(nj, grp.n_tiles),
        in_specs=[pl.BlockSpec((grp.tile, d), lambda j, i: (i, 0)),
                  pl.BlockSpec((1, d, tn), lambda j, i: (layer, 0, j)),
                  pl.BlockSpec((1, d, tn), lambda j, i: (layer, 0, nj + j)),
                  pl.BlockSpec((1, FFN_K, tn), lambda j, i: (layer, 0, j)),
                  pl.BlockSpec((1, FFN_K, tn), lambda j, i: (layer, 0, nj + j)),
                  pl.BlockSpec((1, 1, tn), lambda j, i: (layer, 0, j)),
                  pl.BlockSpec((1, 1, tn), lambda j, i: (layer, 0, nj + j)),
                  pl.BlockSpec((1, hist, tn), lambda j, i: (i // tps, 0, j)),
                  pl.BlockSpec((1, hist, tn), lambda j, i: (i // tps, 0, nj + j))],
        out_specs=[pl.BlockSpec((grp.tile, tn), lambda j, i: (i, j)),
                   pl.BlockSpec((1, hist, tn), lambda j, i: (i // tps, 0, j)),
                   pl.BlockSpec((1, hist, tn), lambda j, i: (i // tps, 0, j))],
        out_shape=[jax.ShapeDtypeStruct((h.shape[0], D_FF), BF16),
                   jax.ShapeDtypeStruct((n_seq, hist, D_FF), F32),
                   jax.ShapeDtypeStruct((n_seq, hist, D_FF), F32)],
        scratch_shapes=[pltpu.VMEM((base + grp.tile, tn), F32), pltpu.VMEM((base + grp.tile, tn), F32),
                        pltpu.VMEM((d, tn), BF16), pltpu.VMEM((d, tn), BF16)],
        compiler_params=_cparams(2), name="ffn_up",
    )(h, w_up, w_up, conv_w, conv_w, cb, cb, hist0, hist0)
    return act, jnp.concatenate([st_a, st_g], axis=-1)


def _glu_kernel(y_ref, w_ref, b_ref, o_ref):
    z5 = _gelu_tanh(y_ref[...])
    o_ref[...] = (z5 * _sigmoid(_dot(z5.astype(BF16), w_ref[...]) + b_ref[...])).astype(o_ref.dtype)


def _s5_glu(yd, w, b, grp):
    n = yd.shape[1]
    return pl.pallas_call(
        _glu_kernel,
        grid=(grp.n_tiles,),
        in_specs=[pl.BlockSpec((grp.tile, n), lambda i: (i, 0)),
                  pl.BlockSpec((n, n), lambda i: (0, 0)),
                  pl.BlockSpec((1, n), lambda i: (0, 0))],
        out_specs=pl.BlockSpec((grp.tile, n), lambda i: (i, 0)),
        out_shape=jax.ShapeDtypeStruct(yd.shape, BF16),
        compiler_params=_cparams(1), name="s5_glu",
    )(yd, w, b.reshape(1, n))


def _round_robin(gens):
    gens = list(gens)
    while gens:
        alive = []
        for gen in gens:
            try:
                next(gen)
                alive.append(gen)
            except StopIteration:
                pass
        gens = alive


def _mixer_out_dtype(block):
    return BF16 if block % (2 * SUBLANE) == 0 else F32


def _causal_conv_chunk(x, cv_scr, cw_ref, cb_ref, chunk, valid):
    base = SUBLANE
    cv_scr[base:base + chunk, :] = x
    y = cb_ref[...]
    for j in range(CONV_K):
        lo = base - (CONV_K - 1) + j
        y = y + cv_scr[lo:lo + chunk, :] * cw_ref[j:j + 1, :]
    last = cv_scr[base + valid - (CONV_K - 1):base + valid, :]
    cv_scr[base - (CONV_K - 1):base, :] = last
    return y


def _gla_kernel(q_ref, k_ref, v_ref, r_ref, sm_ref, w2_ref, b2_ref, ng_ref, s0_ref,
                o_ref, sout_ref, s_scr, b_scr, *, chunk, group, block, valid, nseq):
    blk = pl.program_id(1)

    @pl.when(blk == 0)
    def _():
        s_scr[...] = s0_ref[...]

    span = chunk * group
    lane = lax.broadcasted_iota(jnp.int32, (1, GLA_DK), 1)
    ones_kk = jnp.ones((GLA_DK, GLA_DK), BF16)
    row_in_chunk = lax.broadcasted_iota(jnp.int32, (span, 1), 0) % chunk
    eye = (lax.broadcasted_iota(jnp.int32, (GLA_DK, GLA_DK), 0)
           == lax.broadcasted_iota(jnp.int32, (GLA_DK, GLA_DK), 1))
    n_valid = min(valid, chunk)
    heads = range(GLA_H)
    chunks = range(group)

    def to3(t):
        return t.reshape(group, chunk, t.shape[-1])

    ri = lax.broadcasted_iota(jnp.int32, (block, block), 0)
    ci = lax.broadcasted_iota(jnp.int32, (block, block), 1)
    tri = ((ri // chunk == ci // chunk) & (ri >= ci)).astype(BF16)
    row_blk = lax.broadcasted_iota(jnp.int32, (block, 1), 0) % chunk
    for sq in range(nseq):
        x = _dot(sm_ref[sq].astype(BF16), w2_ref[...].astype(BF16)) + b2_ref[...]
        log_a = (jnp.minimum(x, 0.0) - jnp.log(1.0 + jnp.exp(-jnp.abs(x)))) * (1.0 / GLA_GATE_NORM)
        if valid < chunk:
            log_a = jnp.where(row_blk < valid, log_a, 0.0)
        b_scr[sq] = sum(_dot(tri, part) for part in _split3(log_a))

    def one_seq(sq, rows):
        b = b_scr[sq, rows, :]
        q = q_ref[sq, rows, :] * GLA_DK ** -0.5
        k = k_ref[sq, rows, :]
        v = v_ref[sq, rows, :]
        r = r_ref[sq, rows, :]
        b3, q_in, kv, d_col = [], [], [], []
        for h in heads:
            ks = slice(h * GLA_DK, (h + 1) * GLA_DK)
            bh3 = to3(b[:, ks])
            b_last = bh3[:, chunk - 1:chunk, :]
            k_out = (to3(k[:, ks]) * jnp.exp(b_last - bh3)).reshape(span, GLA_DK)
            vh = v[:, h * GLA_DV:(h + 1) * GLA_DV]
            b3.append(bh3)
            q_in.append(q[:, ks] * jnp.exp(b[:, ks]))
            kv.append([_dot_tn(k_out[c * chunk:(c + 1) * chunk], vh[c * chunk:(c + 1) * chunk]) for c in chunks])
            d_col.append([jnp.sum(jnp.where(eye, jnp.exp(b_last[c]), 0.0), axis=-1, keepdims=True) for c in chunks])
        yield
        st = [s_scr[sq, h] for h in heads]
        o_inter = [[] for _ in heads]
        for c in chunks:
            for h in heads:
                o_inter[h].append(_dot(q_in[h][c * chunk:(c + 1) * chunk], st[h]))
                st[h] = st[h] * d_col[h][c] + kv[h][c]
            yield
        for h in heads:
            ks = slice(h * GLA_DK, (h + 1) * GLA_DK)
            vs = slice(h * GLA_DV, (h + 1) * GLA_DV)
            s_scr[sq, h] = st[h]
            bh3, qh3, kh3, vh3 = b3[h], to3(q[:, ks]), to3(k[:, ks]), to3(v[:, vs])
            prods = []
            for j in range(n_valid):
                e = jnp.exp(bh3 - bh3[:, j:j + 1, :])
                prods.append((qh3 * e * kh3[:, j:j + 1, :]).reshape(span, GLA_DK).astype(BF16))
            sums = _dot(jnp.concatenate(prods, axis=0), ones_kk)
            yield
            att = jnp.zeros((span, GLA_DK), F32)
            for j in range(n_valid):
                att = jnp.where(lane == j, sums[j * span:(j + 1) * span], att)
            att = jnp.where(row_in_chunk >= lane, att, 0.0)[:, :chunk]
            vh = v[:, vs]
            o_intra = [_dot(att[c * chunk:(c + 1) * chunk], vh[c * chunk:(c + 1) * chunk]) for c in chunks]
            o = jnp.concatenate(o_intra, axis=0) + jnp.concatenate(o_inter[h], axis=0)
            o_ref[sq, rows, vs] = (_head_rms(o, ng_ref[...]) * _silu(r[:, vs])).astype(o_ref.dtype)
            yield

    def do_span(s, carry):
        rows = pl.ds(pl.multiple_of(s * span, span), span)
        _round_robin(one_seq(sq, rows) for sq in range(nseq))
        return carry

    lax.fori_loop(0, block // span, do_span, 0)

    @pl.when(blk == pl.num_programs(1) - 1)
    def _():
        sout_ref[...] = s_scr[...]


def _gla(proj, w2p, b2, ng, s0, chunk, group, block, valid, nseq):
    nb, seq, _ = proj.shape

    def col(width, off):
        return pl.BlockSpec((nseq, block, width), lambda b, i: (b, i, off // width))

    st_spec = pl.BlockSpec((nseq, GLA_H, GLA_DK, GLA_DV), lambda b, i: (b, 0, 0, 0))
    kern = functools.partial(_gla_kernel, chunk=chunk, group=group, block=block, valid=valid, nseq=nseq)
    return pl.pallas_call(
        kern,
        grid=(nb // nseq, seq // block),
        in_specs=[col(GLA_QK, AB_Q), col(GLA_QK, AB_K), col(GLA_V, AB_V), col(GLA_V, AB_R), col(LANE, AB_SMALL),
                  pl.BlockSpec((LANE, GLA_QK), lambda b, i: (0, 0)),
                  pl.BlockSpec((1, GLA_QK), lambda b, i: (0, 0)),
                  pl.BlockSpec((1, GLA_DV), lambda b, i: (0, 0)),
                  st_spec],
        out_specs=[pl.BlockSpec((nseq, block, GLA_V), lambda b, i: (b, i, 0)), st_spec],
        out_shape=[jax.ShapeDtypeStruct((nb, seq, GLA_V), _mixer_out_dtype(block)),
                   jax.ShapeDtypeStruct((nb, GLA_H, GLA_DK, GLA_DV), F32)],
        scratch_shapes=[pltpu.VMEM((nseq, GLA_H, GLA_DK, GLA_DV), F32), pltpu.VMEM((nseq, block, GLA_QK), F32)],
        compiler_params=_cparams(2), name="gla",
    )(proj, proj, proj, proj, proj, w2p, b2.reshape(1, GLA_QK), ng.reshape(1, GLA_DV), s0)


def _split2(a):
    hi = a.astype(BF16)
    return hi, (a - hi.astype(F32)).astype(BF16)


def _split3(a):
    hi = a.astype(BF16)
    rest = a - hi.astype(F32)
    mid = rest.astype(BF16)
    return hi, mid, (rest - mid.astype(F32)).astype(BF16)


def _dot3(a, b):
    return _dot(a[0], b[0]) + _dot(a[0], b[1]) + _dot(a[1], b[0])


def _inv_unit_lower_many(mats, n, eye):
    ps = [eye - a for a in mats]
    if n <= 2:
        return ps
    pows = [_split2(a) for a in mats]
    k = 2
    pending = None
    while k < n:
        sq = [_dot3(a, a) for a in pows]
        if pending is not None:
            ps = [p + _dot3(_split2(p), f) for p, f in zip(ps, pending)]
        pows = [_split2(a) for a in sq]
        pending = pows
        k *= 2
    return [p + _dot3(_split2(p), f) for p, f in zip(ps, pending)]


def _gdn_kernel(qkv_ref, sm_ref, gb_ref, cw_ref, cb_ref, alog_ref, dtb_ref, ng_ref, s0_ref, c0_ref,
                o_ref, sout_ref, cout_ref, s_scr, cv_scr, *, chunk, block, valid, nseq):
    blk = pl.program_id(1)

    @pl.when(blk == 0)
    def _():
        s_scr[...] = s0_ref[...]
        cv_scr[:, SUBLANE - (CONV_K - 1):SUBLANE, :] = c0_ref[...]

    ri = lax.broadcasted_iota(jnp.int32, (chunk, chunk), 0)
    ci = lax.broadcasted_iota(jnp.int32, (chunk, chunk), 1)
    causal = ri >= ci
    strict = ri > ci
    eye = (ri == ci).astype(F32)
    tri = causal.astype(F32)
    tri_u = (ri <= ci).astype(F32)
    inv_blk = min(INV_BLOCK, chunk)
    same_blk = (ri // inv_blk) == (ci // inv_blk)
    row = lax.broadcasted_iota(jnp.int32, (chunk, 1), 0)
    n_valid = min(valid, chunk)

    units = [(sq, h) for sq in range(nseq) for h in range(GDN_H)]
    heads = range(len(units))

    def prelude(rows, out):
        q, k, kb, rhs, dec, gcc = [], [], [], [], [], []
        for sq, h in units:
            yield
            if h == 0:
                act = _silu(_causal_conv_chunk(qkv_ref[sq, rows, :], cv_scr.at[sq], cw_ref, cb_ref, chunk, n_valid))
                sm = sm_ref[sq, rows, :]
                g_all = -jnp.exp(alog_ref[...]) * _softplus(sm + dtb_ref[...])
                beta_all = _sigmoid(sm)
                if valid < chunk:
                    g_all = jnp.where(row < valid, g_all, 0.0)
                    beta_all = jnp.where(row < valid, beta_all, 0.0)
                gc = _dot(tri, g_all, HIGHEST)
                gc_r = _dot_tn(g_all, tri_u, HIGHEST)
            qh = act[:, h * GDN_DK:(h + 1) * GDN_DK]
            kh = act[:, GDN_QK + h * GDN_DK:GDN_QK + (h + 1) * GDN_DK]
            vh = act[:, 2 * GDN_QK + h * GDN_DV:2 * GDN_QK + (h + 1) * GDN_DV]
            qh = qh * lax.rsqrt(jnp.sum(qh * qh, -1, keepdims=True) + EPS) * GDN_DK ** -0.5
            kh = kh * lax.rsqrt(jnp.sum(kh * kh, -1, keepdims=True) + EPS)
            beta = beta_all[:, AB_B_LANE + h:AB_B_LANE + h + 1]
            gch = gc[:, AB_A_LANE + h:AB_A_LANE + h + 1]
            gcr = gc_r[AB_A_LANE + h:AB_A_LANE + h + 1, :]
            q.append(qh)
            k.append(kh)
            kb.append(kh * beta)
            rhs.append(_split2(jnp.concatenate([vh * beta, kb[-1] * jnp.exp(gch)], axis=1)))
            dec.append(jnp.exp(jnp.where(causal, gch - gcr, -jnp.inf)))
            gcc.append(gch)
        yield
        kbf = [a.astype(BF16) for a in k]
        m = [jnp.where(strict, _dot_nt(kb[h].astype(BF16), kbf[h]) * dec[h], 0.0) for h in heads]
        att = [_dot_nt(q[h].astype(BF16), kbf[h]) * dec[h] for h in heads]
        out['pre'] = (q, k, rhs, gcc, m, att)

    def finish(pre, rows):
        q, k, rhs, gcc, m, att = pre
        m_diag = [jnp.where(same_blk, a, 0.0) for a in m]
        t = [_split2(a) for a in _inv_unit_lower_many(m_diag, inv_blk, eye)]
        yield
        y = [_dot3(t[h], rhs[h]) for h in heads]
        if chunk > inv_blk:
            n_off = [_dot3(t[h], _split2(m[h] - m_diag[h])) for h in heads]
            yield
            qn = [_split2(a) for a in _inv_unit_lower_many(n_off, chunk // inv_blk, eye)]
            yield
            y = [_dot3(qn[h], _split2(y[h])) for h in heads]
        yield
        st = [s_scr[sq, h] for sq, h in units]
        stb = [a.astype(BF16) for a in st]
        v_new = [y[h][:, :GDN_DV] - _dot(y[h][:, GDN_DV:].astype(BF16), stb[h]) for h in heads]
        vnb = [a.astype(BF16) for a in v_new]
        yield
        o = [_dot((q[h] * jnp.exp(gcc[h])).astype(BF16), stb[h]) + _dot(att[h].astype(BF16), vnb[h]) for h in heads]
        yield
        for u, (sq, h) in enumerate(units):
            hs = slice(h * GDN_DV, (h + 1) * GDN_DV)
            g_last = gcc[u][chunk - 1:chunk, :]
            k_out = (k[u] * jnp.exp(g_last - gcc[u])).astype(BF16)
            s_scr[sq, h] = st[u] * jnp.exp(g_last) + _dot_tn(k_out, vnb[u])
            o_ref[sq, rows, hs] = (_head_rms(o[u], ng_ref[...]) * _silu(gb_ref[sq, rows, hs])).astype(o_ref.dtype)

    n_chunks = block // chunk
    cur = {}
    _round_robin([prelude(pl.ds(0, chunk), cur)])
    for c in range(n_chunks):
        nxt = {}
        stages = [finish(cur['pre'], pl.ds(c * chunk, chunk))]
        if c + 1 < n_chunks:
            stages.append(prelude(pl.ds((c + 1) * chunk, chunk), nxt))
        _round_robin(stages)
        cur = nxt

    @pl.when(blk == pl.num_programs(1) - 1)
    def _():
        sout_ref[...] = s_scr[...]
        cout_ref[...] = cv_scr[:, SUBLANE - (CONV_K - 1):SUBLANE, :]


def _gdn(proj, conv_w, conv_b, alog_row, dtb_row, ng, s0, c0, chunk, block, valid, nseq):
    nb, seq, _ = proj.shape

    def col(width, off):
        return pl.BlockSpec((nseq, block, width), lambda b, i: (b, i, off // width))

    def const(shape):
        return pl.BlockSpec(shape, lambda b, i: (0,) * len(shape))

    st_spec = pl.BlockSpec((nseq, GDN_H, GDN_DK, GDN_DV), lambda b, i: (b, 0, 0, 0))
    cv_spec = pl.BlockSpec((nseq, CONV_K - 1, GDN_CONV_W), lambda b, i: (b, 0, 0))
    kern = functools.partial(_gdn_kernel, chunk=chunk, block=block, valid=valid, nseq=nseq)
    return pl.pallas_call(
        kern,
        grid=(nb // nseq, seq // block),
        in_specs=[col(GDN_CONV_W, AB_QKV), col(LANE, AB_SMALL), col(GDN_V, AB_G),
                  const((CONV_K, GDN_CONV_W)), const((1, GDN_CONV_W)), const((1, LANE)), const((1, LANE)),
                  const((1, GDN_DV)), st_spec, cv_spec],
        out_specs=[pl.BlockSpec((nseq, block, GDN_V), lambda b, i: (b, i, 0)), st_spec, cv_spec],
        out_shape=[jax.ShapeDtypeStruct((nb, seq, GDN_V), _mixer_out_dtype(block)),
                   jax.ShapeDtypeStruct((nb, GDN_H, GDN_DK, GDN_DV), F32),
                   jax.ShapeDtypeStruct((nb, CONV_K - 1, GDN_CONV_W), F32)],
        scratch_shapes=[pltpu.VMEM((nseq, GDN_H, GDN_DK, GDN_DV), F32),
                        pltpu.VMEM((nseq, SUBLANE + chunk, GDN_CONV_W), F32)],
        compiler_params=_cparams(2), name="gdn",
    )(proj, proj, proj, conv_w, conv_b.reshape(1, GDN_CONV_W), alog_row, dtb_row, ng.reshape(1, GDN_DV), s0, c0)


def _ssd_kernel(xbc_ref, sm_ref, z_ref, cw_ref, cb_ref, alog_ref, dtb_ref, drow_ref, ng_ref, s0_ref, c0_ref,
                o_ref, sout_ref, cout_ref, s_scr, cv_scr, dt_scr, acs_scr, acsr_scr, dtr_scr,
                *, chunk, block, valid, nseq):
    blk = pl.program_id(1)

    @pl.when(blk == 0)
    def _():
        s_scr[...] = s0_ref[...]
        cv_scr[:, SUBLANE - (CONV_K - 1):SUBLANE, :] = c0_ref[...]

    ri = lax.broadcasted_iota(jnp.int32, (chunk, chunk), 0)
    ci = lax.broadcasted_iota(jnp.int32, (chunk, chunk), 1)
    causal = ri >= ci
    eye = (ri == ci).astype(BF16)
    tri = causal.astype(BF16)
    tri_u = (ri <= ci).astype(BF16)
    row = lax.broadcasted_iota(jnp.int32, (chunk, 1), 0)
    lane_lo = lax.broadcasted_iota(jnp.int32, (chunk, LANE), 1) < SSD_P
    row_lo = lax.broadcasted_iota(jnp.int32, (2 * SSD_P, 1), 0) < SSD_P
    n_valid = min(valid, chunk)
    heads_per_group = SSD_H // SSD_G
    gsz = SSD_W // SSD_G

    for sq in range(nseq):
        for c in range(block // chunk):
            dt = _softplus(sm_ref[sq, c * chunk:(c + 1) * chunk, :] + dtb_ref[...])
            if valid < chunk:
                dt = jnp.where(row < valid, dt, 0.0)
            dta = _split3(dt * (-jnp.exp(alog_ref[...])))
            dt_scr[sq, c] = dt
            acs_scr[sq, c] = sum(_dot(tri, part) for part in dta)
            acsr_scr[sq, c] = sum(_dot_tn(part, tri_u) for part in dta)
            dtr_scr[sq, c] = sum(_dot_tn(part, eye) for part in _split3(dt))

    def one_seq(sq, s, rows):
        act = _silu(_causal_conv_chunk(xbc_ref[sq, rows, :], cv_scr.at[sq], cw_ref, cb_ref, chunk, n_valid))
        dt, acs, acs_r, dt_r = dt_scr[sq, s], acs_scr[sq, s], acsr_scr[sq, s], dtr_scr[sq, s]
        z = z_ref[sq, rows, :]
        for g in range(SSD_G):
            bg = act[:, SSD_W + g * SSD_N:SSD_W + (g + 1) * SSD_N]
            cg = act[:, SSD_W + SSD_G * SSD_N + g * SSD_N:SSD_W + SSD_G * SSD_N + (g + 1) * SSD_N]
            cb = _dot_nt(cg, bg)
            yield
            parts = []
            for pr in range(heads_per_group // 2):
                pi = g * (heads_per_group // 2) + pr
                xp = act[:, pi * LANE:(pi + 1) * LANE]
                st = s_scr[sq, pi]
                y_in, e_in, w_out, d_last = [], [], [], []
                for hh in range(2):
                    h = 2 * pi + hh
                    ac = acs[:, h:h + 1]
                    dec = jnp.exp(jnp.where(causal, ac - acs_r[h:h + 1, :], -jnp.inf))
                    y_in.append(_dot(cb * dec * dt_r[h:h + 1, :], xp))
                    a_last = ac[chunk - 1:chunk, :]
                    e_in.append(jnp.exp(ac))
                    w_out.append(jnp.exp(a_last - ac) * dt[:, h:h + 1])
                    d_last.append(jnp.exp(a_last))
                y = jnp.where(lane_lo, y_in[0], y_in[1])
                y = y + _dot_nt(cg, st) * jnp.where(lane_lo, e_in[0], e_in[1])
                y = y + drow_ref[:, pi * LANE:(pi + 1) * LANE] * xp
                x_sc = xp * jnp.where(lane_lo, w_out[0], w_out[1])
                s_scr[sq, pi] = st * jnp.where(row_lo, d_last[0], d_last[1]) + _dot_tn(x_sc, bg)
                parts.append(y)
                yield
            gs = slice(g * gsz, (g + 1) * gsz)
            yg = jnp.concatenate(parts, axis=1) * _silu(z[:, gs])
            o_ref[sq, rows, gs] = _head_rms(yg, ng_ref[:, gs]).astype(o_ref.dtype)

    def do_chunk(s, carry):
        rows = pl.ds(pl.multiple_of(s * chunk, chunk), chunk)
        _round_robin(one_seq(sq, s, rows) for sq in range(nseq))
        return carry

    lax.fori_loop(0, block // chunk, do_chunk, 0)

    @pl.when(blk == pl.num_programs(1) - 1)
    def _():
        sout_ref[...] = s_scr[...]
        cout_ref[...] = cv_scr[:, SUBLANE - (CONV_K - 1):SUBLANE, :]


def _ssd(proj, conv_w, conv_b, alog_row, dtb_row, d_row, ng, s0, c0, chunk, block, valid, nseq):
    nb, seq, _ = proj.shape
    n_pairs = SSD_H // 2

    def col(width, off):
        return pl.BlockSpec((nseq, block, width), lambda b, i: (b, i, off // width))

    def const(shape):
        return pl.BlockSpec(shape, lambda b, i: (0,) * len(shape))

    st_spec = pl.BlockSpec((nseq, n_pairs, 2 * SSD_P, SSD_N), lambda b, i: (b, 0, 0, 0))
    cv_spec = pl.BlockSpec((nseq, CONV_K - 1, SSD_CONV_W), lambda b, i: (b, 0, 0))
    kern = functools.partial(_ssd_kernel, chunk=chunk, block=block, valid=valid, nseq=nseq)
    o, s_new, c_new = pl.pallas_call(
        kern,
        grid=(nb // nseq, seq // block),
        in_specs=[col(SSD_CONV_W, CD_XBC), col(LANE, CD_SMALL), col(SSD_W, CD_Z),
                  const((CONV_K, SSD_CONV_W)), const((1, SSD_CONV_W)), const((1, LANE)), const((1, LANE)),
                  const((1, SSD_W)), const((1, SSD_W)), st_spec, cv_spec],
        out_specs=[pl.BlockSpec((nseq, block, SSD_W), lambda b, i: (b, i, 0)), st_spec, cv_spec],
        out_shape=[jax.ShapeDtypeStruct((nb, seq, SSD_W), _mixer_out_dtype(block)),
                   jax.ShapeDtypeStruct((nb, n_pairs, 2 * SSD_P, SSD_N), F32),
                   jax.ShapeDtypeStruct((nb, CONV_K - 1, SSD_CONV_W), F32)],
        scratch_shapes=[pltpu.VMEM((nseq, n_pairs, 2 * SSD_P, SSD_N), F32),
                        pltpu.VMEM((nseq, SUBLANE + chunk, SSD_CONV_W), F32),
                        pltpu.VMEM((nseq, block // chunk, chunk, LANE), F32),
                        pltpu.VMEM((nseq, block // chunk, chunk, LANE), F32),
                        pltpu.VMEM((nseq, block // chunk, LANE, chunk), F32),
                        pltpu.VMEM((nseq, block // chunk, LANE, chunk), F32)],
        compiler_params=_cparams(2), name="ssd",
    )(proj, proj, proj, conv_w, conv_b.reshape(1, SSD_CONV_W), alog_row, dtb_row, d_row,
      ng.reshape(1, SSD_W), s0.reshape(nb, n_pairs, 2 * SSD_P, SSD_N), c0)
    return o, s_new.reshape(nb, SSD_H, SSD_P, SSD_N), c_new


def _s5_prep_kernel(are_ref, aim_ref, ldt_ref, bre_ref, bim_ref, lbr_ref, lbi_ref, bbr_ref, bbi_ref):
    a_re, a_im = are_ref[...], aim_ref[...]
    dt = jnp.exp(ldt_ref[...])
    mag = jnp.exp(a_re * dt)
    lb_re, lb_im = mag * jnp.cos(a_im * dt), mag * jnp.sin(a_im * dt)
    nr, ni = lb_re - 1.0, lb_im
    den = a_re * a_re + a_im * a_im
    f_re = (nr * a_re + ni * a_im) / den
    f_im = (ni * a_re - nr * a_im) / den
    b_re, b_im = bre_ref[...], bim_ref[...]
    lbr_ref[...] = lb_re
    lbi_ref[...] = lb_im
    bbr_ref[...] = f_re * b_re - f_im * b_im
    bbi_ref[...] = f_re * b_im + f_im * b_re


def _s5_prep(a_re, a_im, log_dt, b_re, b_im):
    g3 = (S5_G, 1, S5_P)
    b3 = (S5_G, S5_GS, S5_P)
    return pl.pallas_call(
        _s5_prep_kernel,
        out_shape=[jax.ShapeDtypeStruct(g3, F32), jax.ShapeDtypeStruct(g3, F32),
                   jax.ShapeDtypeStruct(b3, F32), jax.ShapeDtypeStruct(b3, F32)],
        name="s5_prep",
    )(a_re.reshape(g3), a_im.reshape(g3), log_dt.reshape(S5_G, 1, 1),
      jnp.swapaxes(b_re, 1, 2), jnp.swapaxes(b_im, 1, 2))


def _block_diag(blocks):
    g, r, c = blocks.shape
    per = 8
    b = blocks.reshape(g // per, per, r, 1, c) * jnp.eye(per, dtype=blocks.dtype).reshape(1, per, 1, per, 1)
    return b.reshape(g // per, per * r, per * c)


def _s5_kernel(u_ref, wre_ref, wim_ref, cre_ref, cim_ref, lbr_ref, lbi_ref, d_ref, x0r_ref, x0i_ref,
               y_ref, xfr_ref, xfi_ref, xr_scr, xi_scr, sr_scr, si_scr, *, rows_per_step, steps):
    c = pl.program_id(2)

    @pl.when(c == 0)
    def _():
        xr_scr[...] = x0r_ref[0]
        xi_scr[...] = x0i_ref[0]

    u = u_ref[0]
    sr_scr[...] = _dot(u, wre_ref[0])
    si_scr[...] = _dot(u, wim_ref[0])
    l_re, l_im = lbr_ref[...], lbi_ref[...]

    def step(t, carry):
        rows = pl.ds(pl.multiple_of(t * rows_per_step, rows_per_step), rows_per_step)
        xr, xi = xr_scr[...], xi_scr[...]
        nr = l_re * xr - l_im * xi + sr_scr[rows, :]
        ni = l_re * xi + l_im * xr + si_scr[rows, :]
        xr_scr[...] = nr
        xi_scr[...] = ni
        sr_scr[rows, :] = nr
        si_scr[rows, :] = ni
        return carry

    lax.fori_loop(0, steps, step, 0)
    y_ref[0] = _dot(sr_scr[...], cre_ref[0]) - _dot(si_scr[...], cim_ref[0]) + d_ref[...] * u

    @pl.when(c == pl.num_programs(2) - 1)
    def _():
        xfr_ref[0] = xr_scr[...]
        xfi_ref[0] = xi_scr[...]


def _s5(proj, w_re, w_im, c_re, c_im, lb_re, lb_im, d, x0_re, x0_im, rows_per_step, steps):
    ng, n_tok, _ = proj.shape
    nj = S5_W // LANE
    sw = S5_STATE // nj
    cr = rows_per_step * steps
    kern = functools.partial(_s5_kernel, rows_per_step=rows_per_step, steps=steps)
    x_spec = pl.BlockSpec((1, rows_per_step, sw), lambda g, j, c: (g, 0, j))
    return pl.pallas_call(
        kern,
        grid=(ng, nj, n_tok // cr),
        in_specs=[pl.BlockSpec((1, cr, LANE), lambda g, j, c: (g, c, CD_U // LANE + j)),
                  pl.BlockSpec((1, LANE, sw), lambda g, j, c: (j, 0, 0)),
                  pl.BlockSpec((1, LANE, sw), lambda g, j, c: (j, 0, 0)),
                  pl.BlockSpec((1, sw, LANE), lambda g, j, c: (j, 0, 0)),
                  pl.BlockSpec((1, sw, LANE), lambda g, j, c: (j, 0, 0)),
                  pl.BlockSpec((1, sw), lambda g, j, c: (0, j)),
                  pl.BlockSpec((1, sw), lambda g, j, c: (0, j)),
                  pl.BlockSpec((1, LANE), lambda g, j, c: (0, j)),
                  x_spec, x_spec],
        out_specs=[pl.BlockSpec((1, cr, LANE), lambda g, j, c: (g, c, j)), x_spec, x_spec],
        out_shape=[jax.ShapeDtypeStruct((ng, n_tok, S5_W), F32),
                   jax.ShapeDtypeStruct((ng, rows_per_step, S5_STATE), F32),
                   jax.ShapeDtypeStruct((ng, rows_per_step, S5_STATE), F32)],
        scratch_shapes=[pltpu.VMEM((rows_per_step, sw), F32), pltpu.VMEM((rows_per_step, sw), F32),
                        pltpu.VMEM((cr, sw), F32), pltpu.VMEM((cr, sw), F32)],
        compiler_params=_cparams(3), name="s5_scan",
    )(proj, w_re, w_im, c_re, c_im, lb_re, lb_im, d.reshape(1, S5_W), x0_re, x0_im)


def _s5_pow_kernel(lbr_ref, lbi_ref, pr_ref, pi_ref, *, n_rows):
    l_re, l_im = lbr_ref[...], lbi_ref[...]
    row = lax.broadcasted_iota(jnp.int32, (SUBLANE, 1), 0)
    p_re, p_im = l_re, l_im
    b_re = jnp.broadcast_to(l_re, (SUBLANE, l_re.shape[1]))
    b_im = jnp.broadcast_to(l_im, (SUBLANE, l_re.shape[1]))
    for r in range(1, SUBLANE):
        p_re, p_im = p_re * l_re - p_im * l_im, p_re * l_im + p_im * l_re
        b_re = jnp.where(row >= r, p_re, b_re)
        b_im = jnp.where(row >= r, p_im, b_im)
    q_re, q_im = jnp.ones_like(l_re), jnp.zeros_like(l_re)
    for a in range(n_rows // SUBLANE):
        rows = slice(a * SUBLANE, (a + 1) * SUBLANE)
        pr_ref[rows, :] = b_re * q_re - b_im * q_im
        pi_ref[rows, :] = b_re * q_im + b_im * q_re
        q_re, q_im = q_re * p_re - q_im * p_im, q_re * p_im + q_im * p_re


def _s5_pow_table(lb_re, lb_im, n_rows):
    shape = jax.ShapeDtypeStruct((n_rows, S5_STATE), F32)
    return pl.pallas_call(functools.partial(_s5_pow_kernel, n_rows=n_rows), out_shape=[shape, shape],
                          name="s5_pow")(lb_re, lb_im)


def _s5_seg_kernel(u_ref, wre_ref, wim_ref, cre_ref, cim_ref, pr_ref, pi_ref, d_ref, x0r_ref, x0i_ref,
                   y_ref, xfr_ref, xfi_ref, sr_scr, si_scr, *, seg_len):
    n_seg = SUBLANE
    n_lane_blk = sr_scr.shape[0]
    sw = n_lane_blk * LANE
    lane_blks = [slice(c * LANE, (c + 1) * LANE) for c in range(n_lane_blk)]

    def put(scr, rows, val):
        for c, ls in enumerate(lane_blks):
            scr[c, rows, :] = val[:, ls]

    def get(scr, rows):
        return jnp.concatenate([scr[c, rows, :] for c in range(n_lane_blk)], axis=1)

    for s in range(n_seg):
        us = u_ref[0, s * seg_len:(s + 1) * seg_len, :]
        put(sr_scr, pl.ds(s, seg_len, stride=n_seg), _dot(us, wre_ref[0]))
        put(si_scr, pl.ds(s, seg_len, stride=n_seg), _dot(us, wim_ref[0]))
    l_re, l_im = pr_ref[0:1, :], pi_ref[0:1, :]

    def step(t, carry):
        xr, xi = carry
        rows = pl.ds(pl.multiple_of(t * n_seg, n_seg), n_seg)
        nr = l_re * xr - l_im * xi + get(sr_scr, rows)
        ni = l_re * xi + l_im * xr + get(si_scr, rows)
        put(sr_scr, rows, nr)
        put(si_scr, rows, ni)
        return nr, ni

    zero = jnp.zeros((n_seg, sw), F32)
    end_re, end_im = lax.fori_loop(0, seg_len, step, (zero, zero), unroll=4)
    ln_re, ln_im = pr_ref[seg_len - 1:seg_len, :], pi_ref[seg_len - 1:seg_len, :]
    p_re, p_im = pr_ref[...], pi_ref[...]
    x_re, x_im = x0r_ref[0], x0i_ref[0]
    for s in range(n_seg):
        loc_re = get(sr_scr, pl.ds(s, seg_len, stride=n_seg))
        loc_im = get(si_scr, pl.ds(s, seg_len, stride=n_seg))
        t_re = loc_re + p_re * x_re - p_im * x_im
        t_im = loc_im + p_re * x_im + p_im * x_re
        us = u_ref[0, s * seg_len:(s + 1) * seg_len, :]
        y_ref[0, s * seg_len:(s + 1) * seg_len, :] = (_dot(t_re, cre_ref[0]) - _dot(t_im, cim_ref[0])
                                                      + d_ref[...] * us)
        e_re, e_im = end_re[s:s + 1, :], end_im[s:s + 1, :]
        x_re, x_im = e_re + ln_re * x_re - ln_im * x_im, e_im + ln_re * x_im + ln_im * x_re
    xfr_ref[0] = x_re
    xfi_ref[0] = x_im


def _s5_seg(proj, w_re, w_im, c_re, c_im, pow_re, pow_im, d, x0_re, x0_im):
    nb, seq, _ = proj.shape
    nj = S5_W // LANE
    sw = S5_STATE // nj
    seg_len = seq // SUBLANE
    x_spec = pl.BlockSpec((1, 1, sw), lambda b, j: (b, 0, j))
    return pl.pallas_call(
        functools.partial(_s5_seg_kernel, seg_len=seg_len),
        grid=(nb, nj),
        in_specs=[pl.BlockSpec((1, seq, LANE), lambda b, j: (b, 0, CD_U // LANE + j)),
                  pl.BlockSpec((1, LANE, sw), lambda b, j: (j, 0, 0)),
                  pl.BlockSpec((1, LANE, sw), lambda b, j: (j, 0, 0)),
                  pl.BlockSpec((1, sw, LANE), lambda b, j: (j, 0, 0)),
                  pl.BlockSpec((1, sw, LANE), lambda b, j: (j, 0, 0)),
                  pl.BlockSpec((seg_len, sw), lambda b, j: (0, j)),
                  pl.BlockSpec((seg_len, sw), lambda b, j: (0, j)),
                  pl.BlockSpec((1, LANE), lambda b, j: (0, j)),
                  x_spec, x_spec],
        out_specs=[pl.BlockSpec((1, seq, LANE), lambda b, j: (b, 0, j)), x_spec, x_spec],
        out_shape=[jax.ShapeDtypeStruct((nb, seq, S5_W), F32),
                   jax.ShapeDtypeStruct((nb, 1, S5_STATE), F32),
                   jax.ShapeDtypeStruct((nb, 1, S5_STATE), F32)],
        scratch_shapes=[pltpu.VMEM((sw // LANE, seq, LANE), F32), pltpu.VMEM((sw // LANE, seq, LANE), F32)],
        compiler_params=_cparams(2), name="s5_seg",
    )(proj, w_re, w_im, c_re, c_im, pow_re, pow_im, d.reshape(1, S5_W), x0_re, x0_im)


def _lane_row(vec, lane0):
    return jnp.zeros((1, LANE), F32).at[0, lane0:lane0 + vec.shape[0]].set(vec.astype(F32))


def _realign_kernel(w_ref, o_ref, *, segments):
    o_ref[...] = jnp.zeros_like(o_ref)
    for src, rows, dst in segments:
        o_ref[dst:dst + rows, :] = w_ref[src:src + rows, :].astype(BF16)


def _realign_weights(w, segments, n_out):
    wt = jnp.swapaxes(w, 1, 2)[0]
    n_src, k = wt.shape
    assert all(v % (2 * SUBLANE) == 0 for seg in segments for v in seg)
    cols = 256
    return pl.pallas_call(
        functools.partial(_realign_kernel, segments=segments),
        grid=(k // cols,),
        in_specs=[pl.BlockSpec((n_src, cols), lambda i: (0, i))],
        out_specs=pl.BlockSpec((n_out, cols), lambda i: (0, i)),
        out_shape=jax.ShapeDtypeStruct((n_out, k), BF16),
        compiler_params=_cparams(1), name="realign_weights",
    )(wt)


def _prep_params(p):
    q = {}
    o_lr = 2 * GLA_QK + GLA_V
    o_r = o_lr + GLA_LR
    o_qkv = o_r + GLA_V
    o_a = o_qkv + GDN_CONV_W
    o_g = o_a + 2 * GDN_H
    q['w_in_ab'] = _realign_weights(
        p['w_in_ab'], [(0, o_lr, AB_Q), (o_r, GLA_V, AB_R), (o_qkv, GDN_CONV_W, AB_QKV), (o_g, GDN_V, AB_G),
                       (o_lr, GLA_LR, AB_SMALL + AB_LR_LANE), (o_a, 2 * GDN_H, AB_SMALL + AB_A_LANE)], AB_N)
    q['gla_w2'] = jnp.zeros((LANE, GLA_QK), F32).at[:GLA_LR].set(p['gla_w2'][0])
    q['gdn_alog'] = _lane_row(p['gdn_A_log'][0], AB_A_LANE)
    q['gdn_dtb'] = _lane_row(p['gdn_dt_bias'][0], AB_A_LANE)
    q['w_out_ab'] = p['w_out_ab'].astype(BF16)
    o_xbc = SSD_W
    o_dt = o_xbc + SSD_CONV_W
    o_u = o_dt + SSD_H
    q['w_in_cd'] = _realign_weights(
        p['w_in_cd'], [(o_xbc, SSD_CONV_W, CD_XBC), (0, SSD_W, CD_Z), (o_u, S5_W, CD_U), (o_dt, SSD_H, CD_SMALL)],
        CD_N)
    q['ssd_alog'] = _lane_row(p['ssd_A_log'][0], 0)
    q['ssd_dtb'] = _lane_row(p['ssd_dt_bias'][0], 0)
    q['ssd_d_row'] = jnp.repeat(p['ssd_D'][0].astype(F32), SSD_P).reshape(1, SSD_W)
    q['w_out_cd'] = p['w_out_cd'].astype(BF16)
    lb_re, lb_im, bb_re, bb_im = _s5_prep(p['s5_A_re'][0], p['s5_A_im'][0], p['s5_log_dt'][0],
                                          p['s5_B_re'][0], p['s5_B_im'][0])
    q['s5_lb_re'], q['s5_lb_im'] = lb_re.reshape(1, S5_STATE), lb_im.reshape(1, S5_STATE)
    q['s5_w_re'], q['s5_w_im'] = _block_diag(bb_re), _block_diag(bb_im)
    q['s5_c_re'] = _block_diag(jnp.swapaxes(p['s5_C_re'][0], 1, 2))
    q['s5_c_im'] = _block_diag(jnp.swapaxes(p['s5_C_im'][0], 1, 2))
    q['s5_glu_w'] = p['s5_glu_w'][0].astype(BF16)
    q['w_ffn_down'] = p['w_ffn_down'].astype(BF16)
    return q


def _trunk(x, mods, grp, seq_shape, state, p, q):
    nb, seq_len, valid = seq_shape
    s_gla, s_gdn, s_gdnc, s_ssd, s_ssdc, s_re, s_im, s_ffn = state
    prompt = not grp.per_token_mod

    def to_seq(t):
        if prompt:
            return t.reshape(nb, seq_len, t.shape[-1])
        t = jnp.swapaxes(t.reshape(valid, nb, t.shape[-1]), 0, 1)
        return jnp.pad(t, ((0, 0), (0, seq_len - valid), (0, 0)))

    def from_seq(t):
        if prompt:
            return t.reshape(nb * seq_len, t.shape[-1])
        return jnp.swapaxes(t[:, :valid], 0, 1).reshape(valid * nb, t.shape[-1])

    blk = MIX_BLOCK if prompt else seq_len
    chunks = (GLA_CHUNK, GDN_CHUNK, SSD_CHUNK) if prompt else (seq_len,) * 3
    new = {}

    grp_norm = grp.retiled(NORM_TILE)
    h = _norm_mod(x, p['g_mix'][0], mods[0], grp_norm, 1, 0)
    proj = to_seq(_matmul(h, q['w_in_ab'], grp, AB_N // 3))
    o_a, new['gla'] = _gla(proj, q['gla_w2'], p['gla_b2'][0], p['gla_norm_g'][0], s_gla, chunks[0],
                            GLA_GROUP if prompt else 1, blk, valid, 1 if prompt else SAMPLE_SEQS)
    o_b, new['gdn'], new['gdnc'] = _gdn(proj, p['gdn_conv_w'][0], p['gdn_conv_b'][0], q['gdn_alog'], q['gdn_dtb'],
                                        p['gdn_norm_g'][0], s_gdn, s_gdnc, chunks[1], blk, valid,
                                        1 if prompt else SAMPLE_SEQS)
    x, h = _mm_residual([(from_seq(o_a), q['w_out_ab'], 0, 0), (from_seq(o_b), q['w_out_ab'], 0, 1)],
                        x, mods[0], grp, 2, D_MODEL, norm=(p['g_ffn'][0], 4, 3))
    act, new['ffn0'] = _ffn_up(h, p['w_ffn_up'], p['ffn_conv_w'], p['ffn_conv_b'], 0, s_ffn[0],
                               grp.retiled(FFN_TILE))
    x = _mm_residual([(act, q['w_ffn_down'], 0, 0)], x, mods[0], grp, 5, 1024)

    h = _norm_mod(x, p['g_mix'][1], mods[1], grp_norm, 1, 0)
    proj2 = _matmul(h, q['w_in_cd'], grp, CD_N // 2)
    proj = to_seq(proj2)
    o_c, new['ssd'], new['ssdc'] = _ssd(proj, p['ssd_conv_w'][0], p['ssd_conv_b'][0], q['ssd_alog'], q['ssd_dtb'],
                                        q['ssd_d_row'], p['ssd_norm_g'][0], s_ssd, s_ssdc, chunks[2], blk, valid,
                                        1 if prompt else SAMPLE_SEQS)
    if prompt:
        pow_re, pow_im = _s5_pow_table(q['s5_lb_re'], q['s5_lb_im'], seq_len // SUBLANE)
        yd, new['re'], new['im'] = _s5_seg(proj, q['s5_w_re'], q['s5_w_im'], q['s5_c_re'], q['s5_c_im'],
                                           pow_re, pow_im, p['s5_D'][0], s_re, s_im)
    else:
        yd, new['re'], new['im'] = _s5(proj2.reshape(1, grp.n_tok, CD_N), q['s5_w_re'], q['s5_w_im'],
                                       q['s5_c_re'], q['s5_c_im'], q['s5_lb_re'], q['s5_lb_im'], p['s5_D'][0],
                                       s_re, s_im, nb, valid)
    o_d = _s5_glu(yd.reshape(grp.n_tok, S5_W), q['s5_glu_w'], p['s5_glu_b'][0], grp)
    x, h = _mm_residual([(from_seq(o_c), q['w_out_cd'], 0, 0), (o_d, q['w_out_cd'], 0, 1)], x, mods[1], grp, 2,
                        D_MODEL, norm=(p['g_ffn'][1], 4, 3))
    act, new['ffn1'] = _ffn_up(h, p['w_ffn_up'], p['ffn_conv_w'], p['ffn_conv_b'], 1, s_ffn[1],
                               grp.retiled(FFN_TILE))
    x = _mm_residual([(act, q['w_ffn_down'], 1, 0)], x, mods[1], grp, 5, 1024)
    return _final_rms(x, p['g_final'], grp_norm), new


def kernel(x_prompt, x_sample, c_prompt, c_sample, state_gla, state_gdn, state_gdn_conv, state_ssd, state_ssd_conv, state_s5_re, state_s5_im, state_ffn_conv, w_ada, b_ada, g_mix, g_ffn, w_in_ab, gla_w2, gla_b2, gla_norm_g, gdn_conv_w, gdn_conv_b, gdn_A_log, gdn_dt_bias, gdn_norm_g, w_out_ab, w_in_cd, ssd_conv_w, ssd_conv_b, ssd_A_log, ssd_dt_bias, ssd_D, ssd_norm_g, s5_A_re, s5_A_im, s5_B_re, s5_B_im, s5_C_re, s5_C_im, s5_D, s5_log_dt, s5_glu_w, s5_glu_b, w_out_cd, w_ffn_up, ffn_conv_w, ffn_conv_b, w_ffn_down, g_final):
    p = dict(g_mix=g_mix, g_ffn=g_ffn, w_in_ab=w_in_ab, gla_w2=gla_w2, gla_b2=gla_b2, gla_norm_g=gla_norm_g,
             gdn_conv_w=gdn_conv_w, gdn_conv_b=gdn_conv_b, gdn_A_log=gdn_A_log, gdn_dt_bias=gdn_dt_bias,
             gdn_norm_g=gdn_norm_g, w_out_ab=w_out_ab, w_in_cd=w_in_cd, ssd_conv_w=ssd_conv_w,
             ssd_conv_b=ssd_conv_b, ssd_A_log=ssd_A_log, ssd_dt_bias=ssd_dt_bias, ssd_D=ssd_D,
             ssd_norm_g=ssd_norm_g, s5_A_re=s5_A_re, s5_A_im=s5_A_im, s5_B_re=s5_B_re, s5_B_im=s5_B_im,
             s5_C_re=s5_C_re, s5_C_im=s5_C_im, s5_D=s5_D, s5_log_dt=s5_log_dt, s5_glu_w=s5_glu_w,
             s5_glu_b=s5_glu_b, w_out_cd=w_out_cd, w_ffn_up=w_ffn_up, ffn_conv_w=ffn_conv_w,
             ffn_conv_b=ffn_conv_b, w_ffn_down=w_ffn_down, g_final=g_final)
    bp, lp, d = x_prompt.shape
    bs, ls, _ = x_sample.shape
    q = _prep_params(p)

    bp_pad = -(-bp // SUBLANE) * SUBLANE
    c_all = jnp.concatenate([c_prompt, jnp.zeros((bp_pad - bp, d), F32), c_sample], axis=0)
    mod = _ada_mod(c_all, w_ada, b_ada)
    depth = w_ada.shape[0]
    mods_p = [mod[l, :bp].reshape(bp, 1, 6 * d) for l in range(depth)]
    mods_s = [jnp.tile(mod[l, bp_pad:], (ls, 1)).reshape(1, ls * bs, 6 * d) for l in range(depth)]

    tile_p = 512
    grp_p = _Group(bp * lp, tile_p, False, lp // tile_p, 1)
    zeros = lambda *shape: jnp.zeros(shape, F32)
    state_p = (zeros(bp, GLA_H, GLA_DK, GLA_DV), zeros(bp, GDN_H, GDN_DK, GDN_DV),
               zeros(bp, CONV_K - 1, GDN_CONV_W), zeros(bp, SSD_H, SSD_P, SSD_N),
               zeros(bp, CONV_K - 1, SSD_CONV_W), zeros(bp, 1, S5_STATE), zeros(bp, 1, S5_STATE),
               zeros(depth, bp, FFN_K - 1, 2 * D_FF))
    y_p, new_p = _trunk(x_prompt.reshape(bp * lp, d), mods_p, grp_p, (bp, lp, lp), state_p, p, q)

    grp_s = _Group(bs * ls, bs * ls, True, 1, bs)
    ffn_hist_s = jnp.swapaxes(state_ffn_conv, 1, 2).reshape(depth, 1, (FFN_K - 1) * bs, 2 * D_FF)
    state_s = (state_gla[0], state_gdn[0], state_gdn_conv[0], state_ssd[0], state_ssd_conv[0],
               state_s5_re.reshape(1, bs, S5_STATE), state_s5_im.reshape(1, bs, S5_STATE), ffn_hist_s)
    x_s = jnp.swapaxes(x_sample, 0, 1).reshape(ls * bs, d)
    y_s, new_s = _trunk(x_s, mods_s, grp_s, (bs, SAMPLE_PAD, ls), state_s, p, q)
    y_s = jnp.swapaxes(y_s.reshape(ls, bs, d), 0, 1)

    ffn_p = jnp.stack([new_p['ffn0'], new_p['ffn1']])
    ffn_s = jnp.stack([jnp.swapaxes(new_s[k].reshape(FFN_K - 1, bs, 2 * D_FF), 0, 1) for k in ('ffn0', 'ffn1')])
    s5_shape = lambda t, nb: t.reshape(1, nb, S5_G, S5_P)
    return (y_p.reshape(bp, lp, d), y_s,
            new_p['gla'][None], new_s['gla'][None], new_p['gdn'][None], new_s['gdn'][None],
            new_p['gdnc'][None], new_s['gdnc'][None], new_p['ssd'][None], new_s['ssd'][None],
            new_p['ssdc'][None], new_s['ssdc'][None],
            s5_shape(new_p['re'], bp), s5_shape(new_s['re'], bs), s5_shape(new_p['im'], bp), s5_shape(new_s['im'], bs),
            ffn_p, ffn_s)
```

```python
import functools
import math

import jax
import jax.numpy as jnp
from jax import lax
from jax.experimental import pallas as pl
from jax.experimental.pallas import tpu as pltpu

F32 = jnp.float32
BF16 = jnp.bfloat16
HIGHEST = lax.Precision.HIGHEST
EPS = 1e-6

D_MODEL = 2048
GLA_H, GLA_DK, GLA_DV, GLA_LR = 4, 128, 256, 16
GLA_GATE_NORM = 16.0
GLA_QK, GLA_V = GLA_H * GLA_DK, GLA_H * GLA_DV
GDN_H, GDN_DK, GDN_DV = 8, 128, 128
GDN_QK, GDN_V = GDN_H * GDN_DK, GDN_H * GDN_DV
CONV_K = 4
GDN_CONV_W = 2 * GDN_QK + GDN_V
SSD_P, SSD_H, SSD_G, SSD_N = 64, 16, 2, 128
SSD_W = SSD_H * SSD_P
SSD_CONV_W = SSD_W + 2 * SSD_G * SSD_N
S5_W, S5_GS, S5_G, S5_P = 1024, 16, 64, 64
S5_STATE = S5_G * S5_P
D_FF = 5632
FFN_K = 3

LANE = 128
SUBLANE = 8
VMEM_LIMIT = 48 * 1024 * 1024

AB_Q, AB_K, AB_V, AB_R, AB_QKV, AB_G, AB_SMALL, AB_N = 0, 512, 1024, 2048, 3072, 6144, 7168, 7296
AB_LR_LANE, AB_A_LANE, AB_B_LANE = 0, 16, 24
CD_XBC, CD_SMALL, CD_Z, CD_U, CD_N = 0, 1536, 2048, 3072, 4096

MIX_BLOCK = 256
GLA_CHUNK, GDN_CHUNK, SSD_CHUNK = 16, 128, 128
GLA_GROUP = 4
FFN_TILE = 1024
NORM_TILE = 1024
ADA_TN = 1024
FFN_TN = 512
DOWN_TN = 1024
REALIGN_COLS = 512
FFN_SUB = 256
INV_BLOCK = 16
SAMPLE_PAD = 8
SAMPLE_SEQS = 8


def _cparams(n_axes):
    return pltpu.CompilerParams(dimension_semantics=("arbitrary",) * n_axes, vmem_limit_bytes=VMEM_LIMIT)


def _sigmoid(x):
    return 1.0 / (1.0 + jnp.exp(-x))


def _silu(x):
    return x * _sigmoid(x)


def _softplus(x):
    return jnp.maximum(x, 0.0) + jnp.log(1.0 + jnp.exp(-jnp.abs(x)))


def _gelu_tanh(x):
    return 0.5 * x * (1.0 + jnp.tanh(math.sqrt(2.0 / math.pi) * (x + 0.044715 * (x * x * x))))


def _dot(a, b, precision=None):
    return jnp.dot(a, b, precision=precision, preferred_element_type=F32)


def _dot_nt(a, b, precision=None):
    return lax.dot_general(a, b, (((1,), (1,)), ((), ())), precision=precision, preferred_element_type=F32)


def _dot_tn(a, b, precision=None):
    return lax.dot_general(a, b, (((0,), (0,)), ((), ())), precision=precision, preferred_element_type=F32)


def _head_rms(o, g):
    return o * lax.rsqrt(jnp.mean(o * o, -1, keepdims=True) + EPS) * g


class _Group:
    def __init__(self, n_tok, tile, per_token_mod, tiles_per_seq, conv_shift):
        self.n_tok = n_tok
        self.tile = tile
        self.n_tiles = n_tok // tile
        self.per_token_mod = per_token_mod
        self.tiles_per_seq = tiles_per_seq
        self.conv_shift = conv_shift

    def retiled(self, tile):
        seq_rows = self.tile * self.tiles_per_seq
        if seq_rows % tile or self.n_tok % tile:
            return self
        return _Group(self.n_tok, tile, self.per_token_mod, seq_rows // tile, self.conv_shift)

    def mod_spec(self, width, col_block, m_axis):
        if self.per_token_mod:
            return pl.BlockSpec((1, self.tile, width), lambda *g: (0, g[m_axis], col_block(*g)))
        tps = self.tiles_per_seq
        return pl.BlockSpec((1, 1, width), lambda *g: (g[m_axis] // tps, 0, col_block(*g)))


def _ada_kernel(c_ref, w_ref, b_ref, o_ref):
    cs = _silu(c_ref[...]).astype(BF16)
    o_ref[0] = _dot(cs, w_ref[0].astype(BF16)) + b_ref[0]


def _ada_mod(c, w_ada, b_ada):
    depth, d, n = w_ada.shape
    rows = c.shape[0]
    tn = ADA_TN
    return pl.pallas_call(
        _ada_kernel,
        grid=(depth, n // tn),
        in_specs=[pl.BlockSpec((rows, d), lambda l, j: (0, 0)),
                  pl.BlockSpec((1, d, tn), lambda l, j: (l, 0, j)),
                  pl.BlockSpec((1, 1, tn), lambda l, j: (l, 0, j))],
        out_specs=pl.BlockSpec((1, rows, tn), lambda l, j: (l, 0, j)),
        out_shape=jax.ShapeDtypeStruct((depth, rows, n), F32),
        compiler_params=_cparams(2), name="ada_mod",
    )(c, w_ada, b_ada.reshape(depth, 1, n))


def _norm_mod_kernel(x_ref, g_ref, sc_ref, sh_ref, o_ref):
    x = x_ref[...]
    y = x * lax.rsqrt(jnp.mean(x * x, -1, keepdims=True) + EPS) * g_ref[...]
    o_ref[...] = (y * (1.0 + sc_ref[0]) + sh_ref[0]).astype(BF16)


def _norm_mod(x, g, mod, grp, sc_blk, sh_blk):
    d = x.shape[1]
    return pl.pallas_call(
        _norm_mod_kernel,
        grid=(grp.n_tiles,),
        in_specs=[pl.BlockSpec((grp.tile, d), lambda i: (i, 0)),
                  pl.BlockSpec((1, d), lambda i: (0, 0)),
                  grp.mod_spec(d, lambda i: sc_blk, 0),
                  grp.mod_spec(d, lambda i: sh_blk, 0)],
        out_specs=pl.BlockSpec((grp.tile, d), lambda i: (i, 0)),
        out_shape=jax.ShapeDtypeStruct(x.shape, BF16),
        compiler_params=_cparams(1), name="norm_mod",
    )(x, g.reshape(1, d), mod, mod)


def _rms_kernel(x_ref, g_ref, o_ref):
    x = x_ref[...]
    o_ref[...] = x * lax.rsqrt(jnp.mean(x * x, -1, keepdims=True) + EPS) * g_ref[...]


def _final_rms(x, g, grp):
    d = x.shape[1]
    return pl.pallas_call(
        _rms_kernel,
        grid=(grp.n_tiles,),
        in_specs=[pl.BlockSpec((grp.tile, d), lambda i: (i, 0)), pl.BlockSpec((1, d), lambda i: (0, 0))],
        out_specs=pl.BlockSpec((grp.tile, d), lambda i: (i, 0)),
        out_shape=jax.ShapeDtypeStruct(x.shape, F32),
        compiler_params=_cparams(1), name="final_rms",
    )(x, g.reshape(1, d))


def _matmul_kernel(a_ref, wt_ref, o_ref):
    o_ref[...] = _dot_nt(a_ref[...], wt_ref[...])


def _matmul(a, wt, grp, tn):
    n, k = wt.shape
    return pl.pallas_call(
        _matmul_kernel,
        grid=(n // tn, grp.n_tiles),
        in_specs=[pl.BlockSpec((grp.tile, k), lambda j, i: (i, 0)),
                  pl.BlockSpec((tn, k), lambda j, i: (j, 0))],
        out_specs=pl.BlockSpec((grp.tile, tn), lambda j, i: (i, j)),
        out_shape=jax.ShapeDtypeStruct((a.shape[0], n), F32),
        compiler_params=_cparams(2), name="matmul",
    )(a, wt)


def _mm_res_kernel(*refs, n_pairs, with_norm):
    x_ref, gate_ref = refs[2 * n_pairs:2 * n_pairs + 2]
    y = _dot(refs[0][...].astype(BF16), refs[1][0])
    for p in range(1, n_pairs):
        y = y + _dot(refs[2 * p][...].astype(BF16), refs[2 * p + 1][0])
    x_new = x_ref[...] + gate_ref[0] * y
    if not with_norm:
        o_ref, = refs[2 * n_pairs + 2:]
        o_ref[...] = x_new
        return
    g_ref, sc_ref, sh_ref, o_ref, h_ref = refs[2 * n_pairs + 2:]
    o_ref[...] = x_new
    normed = x_new * lax.rsqrt(jnp.mean(x_new * x_new, -1, keepdims=True) + EPS) * g_ref[...]
    h_ref[...] = (normed * (1.0 + sc_ref[0]) + sh_ref[0]).astype(BF16)


def _mm_residual(pairs, x, mod, grp, gate_blk, tn, norm=None):
    n = x.shape[1]
    if norm is not None:
        assert tn == n
        return _mm_residual_norm(pairs, x, mod, grp, gate_blk, norm)
    in_specs, args = [], []
    for a, w, layer, row_blk in pairs:
        k = a.shape[1]
        in_specs += [pl.BlockSpec((grp.tile, k), lambda j, i: (i, 0)),
                     pl.BlockSpec((1, k, tn), lambda j, i, layer=layer, row_blk=row_blk: (layer, row_blk, j))]
        args += [a, w]
    in_specs += [pl.BlockSpec((grp.tile, tn), lambda j, i: (i, j)),
                 grp.mod_spec(tn, lambda j, i: gate_blk * (n // tn) + j, 1)]
    return pl.pallas_call(
        functools.partial(_mm_res_kernel, n_pairs=len(pairs), with_norm=False),
        grid=(n // tn, grp.n_tiles),
        in_specs=in_specs,
        out_specs=pl.BlockSpec((grp.tile, tn), lambda j, i: (i, j)),
        out_shape=jax.ShapeDtypeStruct(x.shape, F32),
        compiler_params=_cparams(2), name="mm_residual",
    )(*args, x, mod)


def _mm_residual_norm(pairs, x, mod, grp, gate_blk, norm):
    n = x.shape[1]
    gain, sc_blk, sh_blk = norm
    in_specs, args = [], []
    for a, w, layer, row_blk in pairs:
        k = a.shape[1]
        in_specs += [pl.BlockSpec((grp.tile, k), lambda j, i: (i, 0)),
                     pl.BlockSpec((1, k, n), lambda j, i, layer=layer, row_blk=row_blk: (layer, row_blk, 0))]
        args += [a, w]
    row_spec = pl.BlockSpec((grp.tile, n), lambda j, i: (i, 0))
    in_specs += [row_spec, grp.mod_spec(n, lambda j, i: gate_blk, 1), pl.BlockSpec((1, n), lambda j, i: (0, 0)),
                 grp.mod_spec(n, lambda j, i: sc_blk, 1), grp.mod_spec(n, lambda j, i: sh_blk, 1)]
    return pl.pallas_call(
        functools.partial(_mm_res_kernel, n_pairs=len(pairs), with_norm=True),
        grid=(1, grp.n_tiles),
        in_specs=in_specs,
        out_specs=[row_spec, row_spec],
        out_shape=[jax.ShapeDtypeStruct(x.shape, F32), jax.ShapeDtypeStruct(x.shape, BF16)],
        compiler_params=_cparams(2), name="mm_residual_norm",
    )(*args, x, mod, gain.reshape(1, n), mod, mod)


def _ffn_up_kernel(h_ref, wa_ref, wg_ref, cwa_ref, cwg_ref, cba_ref, cbg_ref, ha_ref, hg_ref,
                   act_ref, sta_ref, stg_ref, scr_a, scr_g, wba_scr, wbg_scr, *, shift, tile, sub, tiles_per_seq):
    i = pl.program_id(1)
    hist = (FFN_K - 1) * shift
    base = -(-hist // SUBLANE) * SUBLANE

    @pl.when(i == 0)
    def _():
        wba_scr[...] = wa_ref[0].astype(BF16)
        wbg_scr[...] = wg_ref[0].astype(BF16)

    @pl.when(i % tiles_per_seq == 0)
    def _():
        scr_a[base - hist:base, :] = ha_ref[0]
        scr_g[base - hist:base, :] = hg_ref[0]

    def conv(scr, cw_ref, cb_ref, r0):
        y = cb_ref[0]
        for j in range(FFN_K):
            lo = base + r0 - (FFN_K - 1 - j) * shift
            y = y + scr[lo:lo + sub, :] * cw_ref[0, j:j + 1, :]
        return y

    def project(r0):
        h = h_ref[r0:r0 + sub, :]
        scr_a[base + r0:base + r0 + sub, :] = _dot(h, wba_scr[...])
        scr_g[base + r0:base + r0 + sub, :] = _dot(h, wbg_scr[...])

    project(0)
    for r0 in range(0, tile, sub):
        if r0 + sub < tile:
            project(r0 + sub)
        a = conv(scr_a, cwa_ref, cba_ref, r0)
        g = conv(scr_g, cwg_ref, cbg_ref, r0)
        act_ref[r0:r0 + sub, :] = (_silu(g) * a).astype(BF16)
    last_a = scr_a[base + tile - hist:base + tile, :]
    last_g = scr_g[base + tile - hist:base + tile, :]
    sta_ref[0] = last_a
    stg_ref[0] = last_g
    scr_a[base - hist:base, :] = last_a
    scr_g[base - hist:base, :] = last_g


def _ffn_up(h, w_up, conv_w, conv_b, layer, hist0, grp):
    d = h.shape[1]
    tn = FFN_TN
    nj = D_FF // tn
    shift = grp.conv_shift
    hist = (FFN_K - 1) * shift
    base = -(-hist // SUBLANE) * SUBLANE
    n_seq = grp.n_tiles // grp.tiles_per_seq
    tps = grp.tiles_per_seq
    cb = conv_b.reshape(conv_b.shape[0], 1, 2 * D_FF)
    kern = functools.partial(_ffn_up_kernel, shift=shift, tile=grp.tile, sub=min(FFN_SUB, grp.tile),
                             tiles_per_seq=tps)
    assert grp.tile % min(FFN_SUB, grp.tile) == 0
    act, st_a, st_g = pl.pallas_call(
        kern,
        grid=(nj, grp.n_tiles),
        in_specs=[pl.BlockSpec((grp.tile, d), lambda j, i: (i, 0)),
                  pl.BlockSpec((1, d, tn), lambda j, i: (layer, 0, j)),
                  pl.BlockSpec((1, d, tn), lambda j, i: (layer, 0, nj + j)),
                  pl.BlockSpec((1, FFN_K, tn), lambda j, i: (layer, 0, j)),
                  pl.BlockSpec((1, FFN_K, tn), lambda j, i: (layer, 0, nj + j)),
                  pl.BlockSpec((1, 1, tn), lambda j, i: (layer, 0, j)),
                  pl.BlockSpec((1, 1, tn), lambda j, i: (layer, 0, nj + j)),
                  pl.BlockSpec((1, hist, tn), lambda j, i: (i // tps, 0, j)),
                  pl.BlockSpec((1, hist, tn), lambda j, i: (i // tps, 0, nj + j))],
        out_specs=[pl.BlockSpec((grp.tile, tn), lambda j, i: (i, j)),
                   pl.BlockSpec((1, hist, tn), lambda j, i: (i // tps, 0, j)),
                   pl.BlockSpec((1, hist, tn), lambda j, i: (i // tps, 0, j))],
        out_shape=[jax.ShapeDtypeStruct((h.shape[0], D_FF), BF16),
                   jax.ShapeDtypeStruct((n_seq, hist, D_FF), F32),
                   jax.ShapeDtypeStruct((n_seq, hist, D_FF), F32)],
        scratch_shapes=[pltpu.VMEM((base + grp.tile, tn), F32), pltpu.VMEM((base + grp.tile, tn), F32),
                        pltpu.VMEM((d, tn), BF16), pltpu.VMEM((d, tn), BF16)],
        compiler_params=_cparams(2), name="ffn_up",
    )(h, w_up, w_up, conv_w, conv_w, cb, cb, hist0, hist0)
    return act, jnp.concatenate([st_a, st_g], axis=-1)


def _glu_kernel(y_ref, w_ref, b_ref, o_ref):
    z5 = _gelu_tanh(y_ref[...])
    o_ref[...] = (z5 * _sigmoid(_dot(z5.astype(BF16), w_ref[...]) + b_ref[...])).astype(o_ref.dtype)


def _s5_glu(yd, w, b, grp):
    n = yd.shape[1]
    return pl.pallas_call(
        _glu_kernel,
        grid=(grp.n_tiles,),
        in_specs=[pl.BlockSpec((grp.tile, n), lambda i: (i, 0)),
                  pl.BlockSpec((n, n), lambda i: (0, 0)),
                  pl.BlockSpec((1, n), lambda i: (0, 0))],
        out_specs=pl.BlockSpec((grp.tile, n), lambda i: (i, 0)),
        out_shape=jax.ShapeDtypeStruct(yd.shape, BF16),
        compiler_params=_cparams(1), name="s5_glu",
    )(yd, w, b.reshape(1, n))


def _round_robin(gens):
    gens = list(gens)
    while gens:
        alive = []
        for gen in gens:
            try:
                next(gen)
                alive.append(gen)
            except StopIteration:
                pass
        gens = alive


def _mixer_out_dtype(block):
    return BF16 if block % (2 * SUBLANE) == 0 else F32


def _causal_conv_chunk(x, cv_scr, cw_ref, cb_ref, chunk, valid):
    base = SUBLANE
    cv_scr[base:base + chunk, :] = x
    y = cb_ref[...]
    for j in range(CONV_K):
        lo = base - (CONV_K - 1) + j
        y = y + cv_scr[lo:lo + chunk, :] * cw_ref[j:j + 1, :]
    last = cv_scr[base + valid - (CONV_K - 1):base + valid, :]
    cv_scr[base - (CONV_K - 1):base, :] = last
    return y


def _gla_kernel(q_ref, k_ref, v_ref, r_ref, sm_ref, w2_ref, b2_ref, ng_ref, s0_ref,
                o_ref, sout_ref, s_scr, b_scr, *, chunk, group, block, valid, nseq):
    blk = pl.program_id(1)

    @pl.when(blk == 0)
    def _():
        s_scr[...] = s0_ref[...]

    span = chunk * group
    lane = lax.broadcasted_iota(jnp.int32, (1, GLA_DK), 1)
    ones_kk = jnp.ones((GLA_DK, GLA_DK), BF16)
    row_in_chunk = lax.broadcasted_iota(jnp.int32, (span, 1), 0) % chunk
    eye = (lax.broadcasted_iota(jnp.int32, (GLA_DK, GLA_DK), 0)
           == lax.broadcasted_iota(jnp.int32, (GLA_DK, GLA_DK), 1))
    n_valid = min(valid, chunk)
    heads = range(GLA_H)
    chunks = range(group)

    def to3(t):
        return t.reshape(group, chunk, t.shape[-1])

    ri = lax.broadcasted_iota(jnp.int32, (block, block), 0)
    ci = lax.broadcasted_iota(jnp.int32, (block, block), 1)
    tri = ((ri // chunk == ci // chunk) & (ri >= ci)).astype(BF16)
    row_blk = lax.broadcasted_iota(jnp.int32, (block, 1), 0) % chunk
    for sq in range(nseq):
        x = _dot(sm_ref[sq].astype(BF16), w2_ref[...].astype(BF16)) + b2_ref[...]
        log_a = (jnp.minimum(x, 0.0) - jnp.log(1.0 + jnp.exp(-jnp.abs(x)))) * (1.0 / GLA_GATE_NORM)
        if valid < chunk:
            log_a = jnp.where(row_blk < valid, log_a, 0.0)
        b_scr[sq] = sum(_dot(tri, part) for part in _split3(log_a))

    def one_seq(sq, rows):
        b = b_scr[sq, rows, :]
        q = q_ref[sq, rows, :] * GLA_DK ** -0.5
        k = k_ref[sq, rows, :]
        v = v_ref[sq, rows, :]
        r = r_ref[sq, rows, :]
        b3, q_in, kv, d_col = [], [], [], []
        for h in heads:
            ks = slice(h * GLA_DK, (h + 1) * GLA_DK)
            bh3 = to3(b[:, ks])
            b_last = bh3[:, chunk - 1:chunk, :]
            k_out = (to3(k[:, ks]) * jnp.exp(b_last - bh3)).reshape(span, GLA_DK)
            vh = v[:, h * GLA_DV:(h + 1) * GLA_DV]
            b3.append(bh3)
            q_in.append(q[:, ks] * jnp.exp(b[:, ks]))
            kv.append([_dot_tn(k_out[c * chunk:(c + 1) * chunk], vh[c * chunk:(c + 1) * chunk]) for c in chunks])
            d_col.append([jnp.sum(jnp.where(eye, jnp.exp(b_last[c]), 0.0), axis=-1, keepdims=True) for c in chunks])
        yield
        st = [s_scr[sq, h] for h in heads]
        o_inter = [[] for _ in heads]
        for c in chunks:
            for h in heads:
                o_inter[h].append(_dot(q_in[h][c * chunk:(c + 1) * chunk], st[h]))
                st[h] = st[h] * d_col[h][c] + kv[h][c]
            yield
        for h in heads:
            ks = slice(h * GLA_DK, (h + 1) * GLA_DK)
            vs = slice(h * GLA_DV, (h + 1) * GLA_DV)
            s_scr[sq, h] = st[h]
            bh3, qh3, kh3, vh3 = b3[h], to3(q[:, ks]), to3(k[:, ks]), to3(v[:, vs])
            prods = []
            for j in range(n_valid):
                e = jnp.exp(bh3 - bh3[:, j:j + 1, :])
                prods.append((qh3 * e * kh3[:, j:j + 1, :]).reshape(span, GLA_DK).astype(BF16))
            sums = _dot(jnp.concatenate(prods, axis=0), ones_kk)
            yield
            att = jnp.zeros((span, GLA_DK), F32)
            for j in range(n_valid):
                att = jnp.where(lane == j, sums[j * span:(j + 1) * span], att)
            att = jnp.where(row_in_chunk >= lane, att, 0.0)[:, :chunk]
            vh = v[:, vs]
            o_intra = [_dot(att[c * chunk:(c + 1) * chunk], vh[c * chunk:(c + 1) * chunk]) for c in chunks]
            o = jnp.concatenate(o_intra, axis=0) + jnp.concatenate(o_inter[h], axis=0)
            o_ref[sq, rows, vs] = (_head_rms(o, ng_ref[...]) * _silu(r[:, vs])).astype(o_ref.dtype)
            yield

    def do_span(s, carry):
        rows = pl.ds(pl.multiple_of(s * span, span), span)
        _round_robin(one_seq(sq, rows) for sq in range(nseq))
        return carry

    lax.fori_loop(0, block // span, do_span, 0)

    @pl.when(blk == pl.num_programs(1) - 1)
    def _():
        sout_ref[...] = s_scr[...]


def _gla(proj, w2p, b2, ng, s0, chunk, group, block, valid, nseq):
    nb, seq, _ = proj.shape

    def col(width, off):
        return pl.BlockSpec((nseq, block, width), lambda b, i: (b, i, off // width))

    st_spec = pl.BlockSpec((nseq, GLA_H, GLA_DK, GLA_DV), lambda b, i: (b, 0, 0, 0))
    kern = functools.partial(_gla_kernel, chunk=chunk, group=group, block=block, valid=valid, nseq=nseq)
    return pl.pallas_call(
        kern,
        grid=(nb // nseq, seq // block),
        in_specs=[col(GLA_QK, AB_Q), col(GLA_QK, AB_K), col(GLA_V, AB_V), col(GLA_V, AB_R), col(LANE, AB_SMALL),
                  pl.BlockSpec((LANE, GLA_QK), lambda b, i: (0, 0)),
                  pl.BlockSpec((1, GLA_QK), lambda b, i: (0, 0)),
                  pl.BlockSpec((1, GLA_DV), lambda b, i: (0, 0)),
                  st_spec],
        out_specs=[pl.BlockSpec((nseq, block, GLA_V), lambda b, i: (b, i, 0)), st_spec],
        out_shape=[jax.ShapeDtypeStruct((nb, seq, GLA_V), _mixer_out_dtype(block)),
                   jax.ShapeDtypeStruct((nb, GLA_H, GLA_DK, GLA_DV), F32)],
        scratch_shapes=[pltpu.VMEM((nseq, GLA_H, GLA_DK, GLA_DV), F32), pltpu.VMEM((nseq, block, GLA_QK), F32)],
        compiler_params=_cparams(2), name="gla",
    )(proj, proj, proj, proj, proj, w2p, b2.reshape(1, GLA_QK), ng.reshape(1, GLA_DV), s0)


def _split2(a):
    hi = a.astype(BF16)
    return hi, (a - hi.astype(F32)).astype(BF16)


def _split3(a):
    hi = a.astype(BF16)
    rest = a - hi.astype(F32)
    mid = rest.astype(BF16)
    return hi, mid, (rest - mid.astype(F32)).astype(BF16)


def _dot3(a, b):
    return _dot(a[0], b[0]) + _dot(a[0], b[1]) + _dot(a[1], b[0])


def _inv_unit_lower_many(mats, n, eye):
    ps = [eye - a for a in mats]
    if n <= 2:
        return ps
    pows = [_split2(a) for a in mats]
    k = 2
    pending = None
    while k < n:
        sq = [_dot3(a, a) for a in pows]
        if pending is not None:
            ps = [p + _dot3(_split2(p), f) for p, f in zip(ps, pending)]
        pows = [_split2(a) for a in sq]
        pending = pows
        k *= 2
    return [p + _dot3(_split2(p), f) for p, f in zip(ps, pending)]


def _gdn_kernel(qkv_ref, sm_ref, gb_ref, cw_ref, cb_ref, alog_ref, dtb_ref, ng_ref, s0_ref, c0_ref,
                o_ref, sout_ref, cout_ref, s_scr, cv_scr, *, chunk, block, valid, nseq):
    blk = pl.program_id(1)

    @pl.when(blk == 0)
    def _():
        s_scr[...] = s0_ref[...]
        cv_scr[:, SUBLANE - (CONV_K - 1):SUBLANE, :] = c0_ref[...]

    ri = lax.broadcasted_iota(jnp.int32, (chunk, chunk), 0)
    ci = lax.broadcasted_iota(jnp.int32, (chunk, chunk), 1)
    causal = ri >= ci
    strict = ri > ci
    eye = (ri == ci).astype(F32)
    tri = causal.astype(F32)
    tri_u = (ri <= ci).astype(F32)
    inv_blk = min(INV_BLOCK, chunk)
    same_blk = (ri // inv_blk) == (ci // inv_blk)
    row = lax.broadcasted_iota(jnp.int32, (chunk, 1), 0)
    n_valid = min(valid, chunk)

    units = [(sq, h) for sq in range(nseq) for h in range(GDN_H)]
    heads = range(len(units))

    def prelude(rows, out):
        q, k, kb, rhs, dec, gcc = [], [], [], [], [], []
        for sq, h in units:
            yield
            if h == 0:
                act = _silu(_causal_conv_chunk(qkv_ref[sq, rows, :], cv_scr.at[sq], cw_ref, cb_ref, chunk, n_valid))
                sm = sm_ref[sq, rows, :]
                g_all = -jnp.exp(alog_ref[...]) * _softplus(sm + dtb_ref[...])
                beta_all = _sigmoid(sm)
                if valid < chunk:
                    g_all = jnp.where(row < valid, g_all, 0.0)
                    beta_all = jnp.where(row < valid, beta_all, 0.0)
                gc = _dot(tri, g_all, HIGHEST)
                gc_r = _dot_tn(g_all, tri_u, HIGHEST)
            qh = act[:, h * GDN_DK:(h + 1) * GDN_DK]
            kh = act[:, GDN_QK + h * GDN_DK:GDN_QK + (h + 1) * GDN_DK]
            vh = act[:, 2 * GDN_QK + h * GDN_DV:2 * GDN_QK + (h + 1) * GDN_DV]
            qh = qh * lax.rsqrt(jnp.sum(qh * qh, -1, keepdims=True) + EPS) * GDN_DK ** -0.5
            kh = kh * lax.rsqrt(jnp.sum(kh * kh, -1, keepdims=True) + EPS)
            beta = beta_all[:, AB_B_LANE + h:AB_B_LANE + h + 1]
            gch = gc[:, AB_A_LANE + h:AB_A_LANE + h + 1]
            gcr = gc_r[AB_A_LANE + h:AB_A_LANE + h + 1, :]
            q.append(qh)
            k.append(kh)
            kb.append(kh * beta)
            rhs.append(_split2(jnp.concatenate([vh * beta, kb[-1] * jnp.exp(gch)], axis=1)))
            dec.append(jnp.exp(jnp.where(causal, gch - gcr, -jnp.inf)))
            gcc.append(gch)
        yield
        kbf = [a.astype(BF16) for a in k]
        m = [jnp.where(strict, _dot_nt(kb[h].astype(BF16), kbf[h]) * dec[h], 0.0) for h in heads]
        att = [_dot_nt(q[h].astype(BF16), kbf[h]) * dec[h] for h in heads]
        out['pre'] = (q, k, rhs, gcc, m, att)

    def finish(pre, rows):
        q, k, rhs, gcc, m, att = pre
        m_diag = [jnp.where(same_blk, a, 0.0) for a in m]
        t = [_split2(a) for a in _inv_unit_lower_many(m_diag, inv_blk, eye)]
        yield
        y = [_dot3(t[h], rhs[h]) for h in heads]
        if chunk > inv_blk:
            n_off = [_dot3(t[h], _split2(m[h] - m_diag[h])) for h in heads]
            yield
            qn = [_split2(a) for a in _inv_unit_lower_many(n_off, chunk // inv_blk, eye)]
            yield
            y = [_dot3(qn[h], _split2(y[h])) for h in heads]
        yield
        st = [s_scr[sq, h] for sq, h in units]
        stb = [a.astype(BF16) for a in st]
        v_new = [y[h][:, :GDN_DV] - _dot(y[h][:, GDN_DV:].astype(BF16), stb[h]) for h in heads]
        vnb = [a.astype(BF16) for a in v_new]
        yield
        o = [_dot((q[h] * jnp.exp(gcc[h])).astype(BF16), stb[h]) + _dot(att[h].astype(BF16), vnb[h]) for h in heads]
        yield
        for u, (sq, h) in enumerate(units):
            hs = slice(h * GDN_DV, (h + 1) * GDN_DV)
            g_last = gcc[u][chunk - 1:chunk, :]
            k_out = (k[u] * jnp.exp(g_last - gcc[u])).astype(BF16)
            s_scr[sq, h] = st[u] * jnp.exp(g_last) + _dot_tn(k_out, vnb[u])
            o_ref[sq, rows, hs] = (_head_rms(o[u], ng_ref[...]) * _silu(gb_ref[sq, rows, hs])).astype(o_ref.dtype)

    n_chunks = block // chunk
    cur = {}
    _round_robin([prelude(pl.ds(0, chunk), cur)])
    for c in range(n_chunks):
        nxt = {}
        stages = [finish(cur['pre'], pl.ds(c * chunk, chunk))]
        if c + 1 < n_chunks:
            stages.append(prelude(pl.ds((c + 1) * chunk, chunk), nxt))
        _round_robin(stages)
        cur = nxt

    @pl.when(blk == pl.num_programs(1) - 1)
    def _():
        sout_ref[...] = s_scr[...]
        cout_ref[...] = cv_scr[:, SUBLANE - (CONV_K - 1):SUBLANE, :]


def _gdn(proj, conv_w, conv_b, alog_row, dtb_row, ng, s0, c0, chunk, block, valid, nseq):
    nb, seq, _ = proj.shape

    def col(width, off):
        return pl.BlockSpec((nseq, block, width), lambda b, i: (b, i, off // width))

    def const(shape):
        return pl.BlockSpec(shape, lambda b, i: (0,) * len(shape))

    st_spec = pl.BlockSpec((nseq, GDN_H, GDN_DK, GDN_DV), lambda b, i: (b, 0, 0, 0))
    cv_spec = pl.BlockSpec((nseq, CONV_K - 1, GDN_CONV_W), lambda b, i: (b, 0, 0))
    kern = functools.partial(_gdn_kernel, chunk=chunk, block=block, valid=valid, nseq=nseq)
    return pl.pallas_call(
        kern,
        grid=(nb // nseq, seq // block),
        in_specs=[col(GDN_CONV_W, AB_QKV), col(LANE, AB_SMALL), col(GDN_V, AB_G),
                  const((CONV_K, GDN_CONV_W)), const((1, GDN_CONV_W)), const((1, LANE)), const((1, LANE)),
                  const((1, GDN_DV)), st_spec, cv_spec],
        out_specs=[pl.BlockSpec((nseq, block, GDN_V), lambda b, i: (b, i, 0)), st_spec, cv_spec],
        out_shape=[jax.ShapeDtypeStruct((nb, seq, GDN_V), _mixer_out_dtype(block)),
                   jax.ShapeDtypeStruct((nb, GDN_H, GDN_DK, GDN_DV), F32),
                   jax.ShapeDtypeStruct((nb, CONV_K - 1, GDN_CONV_W), F32)],
        scratch_shapes=[pltpu.VMEM((nseq, GDN_H, GDN_DK, GDN_DV), F32),
                        pltpu.VMEM((nseq, SUBLANE + chunk, GDN_CONV_W), F32)],
        compiler_params=_cparams(2), name="gdn",
    )(proj, proj, proj, conv_w, conv_b.reshape(1, GDN_CONV_W), alog_row, dtb_row, ng.reshape(1, GDN_DV), s0, c0)


def _ssd_kernel(xbc_ref, sm_ref, z_ref, cw_ref, cb_ref, alog_ref, dtb_ref, drow_ref, ng_ref, s0_ref, c0_ref,
                o_ref, sout_ref, cout_ref, s_scr, cv_scr, dt_scr, acs_scr, acsr_scr, dtr_scr,
                *, chunk, block, valid, nseq):
    blk = pl.program_id(1)

    @pl.when(blk == 0)
    def _():
        s_scr[...] = s0_ref[...]
        cv_scr[:, SUBLANE - (CONV_K - 1):SUBLANE, :] = c0_ref[...]

    ri = lax.broadcasted_iota(jnp.int32, (chunk, chunk), 0)
    ci = lax.broadcasted_iota(jnp.int32, (chunk, chunk), 1)
    causal = ri >= ci
    eye = (ri == ci).astype(BF16)
    tri = causal.astype(BF16)
    tri_u = (ri <= ci).astype(BF16)
    row = lax.broadcasted_iota(jnp.int32, (chunk, 1), 0)
    lane_lo = lax.broadcasted_iota(jnp.int32, (chunk, LANE), 1) < SSD_P
    row_lo = lax.broadcasted_iota(jnp.int32, (2 * SSD_P, 1), 0) < SSD_P
    n_valid = min(valid, chunk)
    heads_per_group = SSD_H // SSD_G
    gsz = SSD_W // SSD_G

    for sq in range(nseq):
        for c in range(block // chunk):
            dt = _softplus(sm_ref[sq, c * chunk:(c + 1) * chunk, :] + dtb_ref[...])
            if valid < chunk:
                dt = jnp.where(row < valid, dt, 0.0)
            dta = _split3(dt * (-jnp.exp(alog_ref[...])))
            dt_scr[sq, c] = dt
            acs_scr[sq, c] = sum(_dot(tri, part) for part in dta)
            acsr_scr[sq, c] = sum(_dot_tn(part, tri_u) for part in dta)
            dtr_scr[sq, c] = sum(_dot_tn(part, eye) for part in _split3(dt))

    def one_seq(sq, s, rows):
        act = _silu(_causal_conv_chunk(xbc_ref[sq, rows, :], cv_scr.at[sq], cw_ref, cb_ref, chunk, n_valid))
        dt, acs, acs_r, dt_r = dt_scr[sq, s], acs_scr[sq, s], acsr_scr[sq, s], dtr_scr[sq, s]
        z = z_ref[sq, rows, :]
        for g in range(SSD_G):
            bg = act[:, SSD_W + g * SSD_N:SSD_W + (g + 1) * SSD_N]
            cg = act[:, SSD_W + SSD_G * SSD_N + g * SSD_N:SSD_W + SSD_G * SSD_N + (g + 1) * SSD_N]
            cb = _dot_nt(cg, bg)
            yield
            parts = []
            for pr in range(heads_per_group // 2):
                pi = g * (heads_per_group // 2) + pr
                xp = act[:, pi * LANE:(pi + 1) * LANE]
                st = s_scr[sq, pi]
                y_in, e_in, w_out, d_last = [], [], [], []
                for hh in range(2):
                    h = 2 * pi + hh
                    ac = acs[:, h:h + 1]
                    dec = jnp.exp(jnp.where(causal, ac - acs_r[h:h + 1, :], -jnp.inf))
                    y_in.append(_dot(cb * dec * dt_r[h:h + 1, :], xp))
                    a_last = ac[chunk - 1:chunk, :]
                    e_in.append(jnp.exp(ac))
                    w_out.append(jnp.exp(a_last - ac) * dt[:, h:h + 1])
                    d_last.append(jnp.exp(a_last))
                y = jnp.where(lane_lo, y_in[0], y_in[1])
                y = y + _dot_nt(cg, st) * jnp.where(lane_lo, e_in[0], e_in[1])
                y = y + drow_ref[:, pi * LANE:(pi + 1) * LANE] * xp
                x_sc = xp * jnp.where(lane_lo, w_out[0], w_out[1])
                s_scr[sq, pi] = st * jnp.where(row_lo, d_last[0], d_last[1]) + _dot_tn(x_sc, bg)
                parts.append(y)
                yield
            gs = slice(g * gsz, (g + 1) * gsz)
            yg = jnp.concatenate(parts, axis=1) * _silu(z[:, gs])
            o_ref[sq, rows, gs] = _head_rms(yg, ng_ref[:, gs]).astype(o_ref.dtype)

    def do_chunk(s, carry):
        rows = pl.ds(pl.multiple_of(s * chunk, chunk), chunk)
        _round_robin(one_seq(sq, s, rows) for sq in range(nseq))
        return carry

    lax.fori_loop(0, block // chunk, do_chunk, 0)

    @pl.when(blk == pl.num_programs(1) - 1)
    def _():
        sout_ref[...] = s_scr[...]
        cout_ref[...] = cv_scr[:, SUBLANE - (CONV_K - 1):SUBLANE, :]


def _ssd(proj, conv_w, conv_b, alog_row, dtb_row, d_row, ng, s0, c0, chunk, block, valid, nseq):
    nb, seq, _ = proj.shape
    n_pairs = SSD_H // 2

    def col(width, off):
        return pl.BlockSpec((nseq, block, width), lambda b, i: (b, i, off // width))

    def const(shape):
        return pl.BlockSpec(shape, lambda b, i: (0,) * len(shape))

    st_spec = pl.BlockSpec((nseq, n_pairs, 2 * SSD_P, SSD_N), lambda b, i: (b, 0, 0, 0))
    cv_spec = pl.BlockSpec((nseq, CONV_K - 1, SSD_CONV_W), lambda b, i: (b, 0, 0))
    kern = functools.partial(_ssd_kernel, chunk=chunk, block=block, valid=valid, nseq=nseq)
    o, s_new, c_new = pl.pallas_call(
        kern,
        grid=(nb // nseq, seq // block),
        in_specs=[col(SSD_CONV_W, CD_XBC), col(LANE, CD_SMALL), col(SSD_W, CD_Z),
                  const((CONV_K, SSD_CONV_W)), const((1, SSD_CONV_W)), const((1, LANE)), const((1, LANE)),
                  const((1, SSD_W)), const((1, SSD_W)), st_spec, cv_spec],
        out_specs=[pl.BlockSpec((nseq, block, SSD_W), lambda b, i: (b, i, 0)), st_spec, cv_spec],
        out_shape=[jax.ShapeDtypeStruct((nb, seq, SSD_W), _mixer_out_dtype(block)),
                   jax.ShapeDtypeStruct((nb, n_pairs, 2 * SSD_P, SSD_N), F32),
                   jax.ShapeDtypeStruct((nb, CONV_K - 1, SSD_CONV_W), F32)],
        scratch_shapes=[pltpu.VMEM((nseq, n_pairs, 2 * SSD_P, SSD_N), F32),
                        pltpu.VMEM((nseq, SUBLANE + chunk, SSD_CONV_W), F32),
                        pltpu.VMEM((nseq, block // chunk, chunk, LANE), F32),
                        pltpu.VMEM((nseq, block // chunk, chunk, LANE), F32),
                        pltpu.VMEM((nseq, block // chunk, LANE, chunk), F32),
                        pltpu.VMEM((nseq, block // chunk, LANE, chunk), F32)],
        compiler_params=_cparams(2), name="ssd",
    )(proj, proj, proj, conv_w, conv_b.reshape(1, SSD_CONV_W), alog_row, dtb_row, d_row,
      ng.reshape(1, SSD_W), s0.reshape(nb, n_pairs, 2 * SSD_P, SSD_N), c0)
    return o, s_new.reshape(nb, SSD_H, SSD_P, SSD_N), c_new


def _s5_prep_kernel(are_ref, aim_ref, ldt_ref, bre_ref, bim_ref, lbr_ref, lbi_ref, bbr_ref, bbi_ref):
    a_re, a_im = are_ref[...], aim_ref[...]
    dt = jnp.exp(ldt_ref[...])
    mag = jnp.exp(a_re * dt)
    lb_re, lb_im = mag * jnp.cos(a_im * dt), mag * jnp.sin(a_im * dt)
    nr, ni = lb_re - 1.0, lb_im
    den = a_re * a_re + a_im * a_im
    f_re = (nr * a_re + ni * a_im) / den
    f_im = (ni * a_re - nr * a_im) / den
    b_re, b_im = bre_ref[...], bim_ref[...]
    lbr_ref[...] = lb_re
    lbi_ref[...] = lb_im
    bbr_ref[...] = f_re * b_re - f_im * b_im
    bbi_ref[...] = f_re * b_im + f_im * b_re


def _s5_prep(a_re, a_im, log_dt, b_re, b_im):
    g3 = (S5_G, 1, S5_P)
    b3 = (S5_G, S5_GS, S5_P)
    return pl.pallas_call(
        _s5_prep_kernel,
        out_shape=[jax.ShapeDtypeStruct(g3, F32), jax.ShapeDtypeStruct(g3, F32),
                   jax.ShapeDtypeStruct(b3, F32), jax.ShapeDtypeStruct(b3, F32)],
        name="s5_prep",
    )(a_re.reshape(g3), a_im.reshape(g3), log_dt.reshape(S5_G, 1, 1),
      jnp.swapaxes(b_re, 1, 2), jnp.swapaxes(b_im, 1, 2))


def _block_diag(blocks):
    g, r, c = blocks.shape
    per = 8
    b = blocks.reshape(g // per, per, r, 1, c) * jnp.eye(per, dtype=blocks.dtype).reshape(1, per, 1, per, 1)
    return b.reshape(g // per, per * r, per * c)


def _s5_kernel(u_ref, wre_ref, wim_ref, cre_ref, cim_ref, lbr_ref, lbi_ref, d_ref, x0r_ref, x0i_ref,
               y_ref, xfr_ref, xfi_ref, xr_scr, xi_scr, sr_scr, si_scr, *, rows_per_step, steps):
    c = pl.program_id(2)

    @pl.when(c == 0)
    def _():
        xr_scr[...] = x0r_ref[0]
        xi_scr[...] = x0i_ref[0]

    u = u_ref[0]
    sr_scr[...] = _dot(u, wre_ref[0])
    si_scr[...] = _dot(u, wim_ref[0])
    l_re, l_im = lbr_ref[...], lbi_ref[...]

    def step(t, carry):
        rows = pl.ds(pl.multiple_of(t * rows_per_step, rows_per_step), rows_per_step)
        xr, xi = xr_scr[...], xi_scr[...]
        nr = l_re * xr - l_im * xi + sr_scr[rows, :]
        ni = l_re * xi + l_im * xr + si_scr[rows, :]
        xr_scr[...] = nr
        xi_scr[...] = ni
        sr_scr[rows, :] = nr
        si_scr[rows, :] = ni
        return carry

    lax.fori_loop(0, steps, step, 0)
    y_ref[0] = _dot(sr_scr[...], cre_ref[0]) - _dot(si_scr[...], cim_ref[0]) + d_ref[...] * u

    @pl.when(c == pl.num_programs(2) - 1)
    def _():
        xfr_ref[0] = xr_scr[...]
        xfi_ref[0] = xi_scr[...]


def _s5(proj, w_re, w_im, c_re, c_im, lb_re, lb_im, d, x0_re, x0_im, rows_per_step, steps):
    ng, n_tok, _ = proj.shape
    nj = S5_W // LANE
    sw = S5_STATE // nj
    cr = rows_per_step * steps
    kern = functools.partial(_s5_kernel, rows_per_step=rows_per_step, steps=steps)
    x_spec = pl.BlockSpec((1, rows_per_step, sw), lambda g, j, c: (g, 0, j))
    return pl.pallas_call(
        kern,
        grid=(ng, nj, n_tok // cr),
        in_specs=[pl.BlockSpec((1, cr, LANE), lambda g, j, c: (g, c, CD_U // LANE + j)),
                  pl.BlockSpec((1, LANE, sw), lambda g, j, c: (j, 0, 0)),
                  pl.BlockSpec((1, LANE, sw), lambda g, j, c: (j, 0, 0)),
                  pl.BlockSpec((1, sw, LANE), lambda g, j, c: (j, 0, 0)),
                  pl.BlockSpec((1, sw, LANE), lambda g, j, c: (j, 0, 0)),
                  pl.BlockSpec((1, sw), lambda g, j, c: (0, j)),
                  pl.BlockSpec((1, sw), lambda g, j, c: (0, j)),
                  pl.BlockSpec((1, LANE), lambda g, j, c: (0, j)),
                  x_spec, x_spec],
        out_specs=[pl.BlockSpec((1, cr, LANE), lambda g, j, c: (g, c, j)), x_spec, x_spec],
        out_shape=[jax.ShapeDtypeStruct((ng, n_tok, S5_W), F32),
                   jax.ShapeDtypeStruct((ng, rows_per_step, S5_STATE), F32),
                   jax.ShapeDtypeStruct((ng, rows_per_step, S5_STATE), F32)],
        scratch_shapes=[pltpu.VMEM((rows_per_step, sw), F32), pltpu.VMEM((rows_per_step, sw), F32),
                        pltpu.VMEM((cr, sw), F32), pltpu.VMEM((cr, sw), F32)],
        compiler_params=_cparams(3), name="s5_scan",
    )(proj, w_re, w_im, c_re, c_im, lb_re, lb_im, d.reshape(1, S5_W), x0_re, x0_im)


def _s5_pow_kernel(lbr_ref, lbi_ref, pr_ref, pi_ref, *, n_rows):
    l_re, l_im = lbr_ref[...], lbi_ref[...]
    row = lax.broadcasted_iota(jnp.int32, (SUBLANE, 1), 0)
    p_re, p_im = l_re, l_im
    b_re = jnp.broadcast_to(l_re, (SUBLANE, l_re.shape[1]))
    b_im = jnp.broadcast_to(l_im, (SUBLANE, l_re.shape[1]))
    for r in range(1, SUBLANE):
        p_re, p_im = p_re * l_re - p_im * l_im, p_re * l_im + p_im * l_re
        b_re = jnp.where(row >= r, p_re, b_re)
        b_im = jnp.where(row >= r, p_im, b_im)
    q_re, q_im = jnp.ones_like(l_re), jnp.zeros_like(l_re)
    for a in range(n_rows // SUBLANE):
        rows = slice(a * SUBLANE, (a + 1) * SUBLANE)
        pr_ref[rows, :] = b_re * q_re - b_im * q_im
        pi_ref[rows, :] = b_re * q_im + b_im * q_re
        q_re, q_im = q_re * p_re - q_im * p_im, q_re * p_im + q_im * p_re


def _s5_pow_table(lb_re, lb_im, n_rows):
    shape = jax.ShapeDtypeStruct((n_rows, S5_STATE), F32)
    return pl.pallas_call(functools.partial(_s5_pow_kernel, n_rows=n_rows), out_shape=[shape, shape],
                          name="s5_pow")(lb_re, lb_im)


def _s5_seg_kernel(u_ref, wre_ref, wim_ref, cre_ref, cim_ref, pr_ref, pi_ref, d_ref, x0r_ref, x0i_ref,
                   y_ref, xfr_ref, xfi_ref, sr_scr, si_scr, *, seg_len):
    n_seg = SUBLANE
    n_lane_blk = sr_scr.shape[0]
    sw = n_lane_blk * LANE
    lane_blks = [slice(c * LANE, (c + 1) * LANE) for c in range(n_lane_blk)]

    def put(scr, rows, val):
        for c, ls in enumerate(lane_blks):
            scr[c, rows, :] = val[:, ls]

    def get(scr, rows):
        return jnp.concatenate([scr[c, rows, :] for c in range(n_lane_blk)], axis=1)

    for s in range(n_seg):
        us = u_ref[0, s * seg_len:(s + 1) * seg_len, :]
        put(sr_scr, pl.ds(s, seg_len, stride=n_seg), _dot(us, wre_ref[0]))
        put(si_scr, pl.ds(s, seg_len, stride=n_seg), _dot(us, wim_ref[0]))
    l_re, l_im = pr_ref[0:1, :], pi_ref[0:1, :]

    def step(t, carry):
        xr, xi = carry
        rows = pl.ds(pl.multiple_of(t * n_seg, n_seg), n_seg)
        nr = l_re * xr - l_im * xi + get(sr_scr, rows)
        ni = l_re * xi + l_im * xr + get(si_scr, rows)
        put(sr_scr, rows, nr)
        put(si_scr, rows, ni)
        return nr, ni

    zero = jnp.zeros((n_seg, sw), F32)
    end_re, end_im = lax.fori_loop(0, seg_len, step, (zero, zero), unroll=4)
    ln_re, ln_im = pr_ref[seg_len - 1:seg_len, :], pi_ref[seg_len - 1:seg_len, :]
    p_re, p_im = pr_ref[...], pi_ref[...]
    x_re, x_im = x0r_ref[0], x0i_ref[0]
    for s in range(n_seg):
        loc_re = get(sr_scr, pl.ds(s, seg_len, stride=n_seg))
        loc_im = get(si_scr, pl.ds(s, seg_len, stride=n_seg))
        t_re = loc_re + p_re * x_re - p_im * x_im
        t_im = loc_im + p_re * x_im + p_im * x_re
        us = u_ref[0, s * seg_len:(s + 1) * seg_len, :]
        y_ref[0, s * seg_len:(s + 1) * seg_len, :] = (_dot(t_re, cre_ref[0]) - _dot(t_im, cim_ref[0])
                                                      + d_ref[...] * us)
        e_re, e_im = end_re[s:s + 1, :], end_im[s:s + 1, :]
        x_re, x_im = e_re + ln_re * x_re - ln_im * x_im, e_im + ln_re * x_im + ln_im * x_re
    xfr_ref[0] = x_re
    xfi_ref[0] = x_im


def _s5_seg(proj, w_re, w_im, c_re, c_im, pow_re, pow_im, d, x0_re, x0_im):
    nb, seq, _ = proj.shape
    nj = S5_W // LANE
    sw = S5_STATE // nj
    seg_len = seq // SUBLANE
    x_spec = pl.BlockSpec((1, 1, sw), lambda b, j: (b, 0, j))
    return pl.pallas_call(
        functools.partial(_s5_seg_kernel, seg_len=seg_len),
        grid=(nb, nj),
        in_specs=[pl.BlockSpec((1, seq, LANE), lambda b, j: (b, 0, CD_U // LANE + j)),
                  pl.BlockSpec((1, LANE, sw), lambda b, j: (j, 0, 0)),
                  pl.BlockSpec((1, LANE, sw), lambda b, j: (j, 0, 0)),
                  pl.BlockSpec((1, sw, LANE), lambda b, j: (j, 0, 0)),
                  pl.BlockSpec((1, sw, LANE), lambda b, j: (j, 0, 0)),
                  pl.BlockSpec((seg_len, sw), lambda b, j: (0, j)),
                  pl.BlockSpec((seg_len, sw), lambda b, j: (0, j)),
                  pl.BlockSpec((1, LANE), lambda b, j: (0, j)),
                  x_spec, x_spec],
        out_specs=[pl.BlockSpec((1, seq, LANE), lambda b, j: (b, 0, j)), x_spec, x_spec],
        out_shape=[jax.ShapeDtypeStruct((nb, seq, S5_W), F32),
                   jax.ShapeDtypeStruct((nb, 1, S5_STATE), F32),
                   jax.ShapeDtypeStruct((nb, 1, S5_STATE), F32)],
        scratch_shapes=[pltpu.VMEM((sw // LANE, seq, LANE), F32), pltpu.VMEM((sw // LANE, seq, LANE), F32)],
        compiler_params=_cparams(2), name="s5_seg",
    )(proj, w_re, w_im, c_re, c_im, pow_re, pow_im, d.reshape(1, S5_W), x0_re, x0_im)


def _lane_row(vec, lane0):
    return jnp.zeros((1, LANE), F32).at[0, lane0:lane0 + vec.shape[0]].set(vec.astype(F32))


def _realign_kernel(w_ref, o_ref, *, segments):
    o_ref[...] = jnp.zeros_like(o_ref)
    for src, rows, dst in segments:
        o_ref[dst:dst + rows, :] = w_ref[src:src + rows, :].astype(BF16)


def _realign_weights(w, segments, n_out):
    wt = jnp.swapaxes(w, 1, 2)[0]
    n_src, k = wt.shape
    assert all(v % (2 * SUBLANE) == 0 for seg in segments for v in seg)
    cols = REALIGN_COLS
    return pl.pallas_call(
        functools.partial(_realign_kernel, segments=segments),
        grid=(k // cols,),
        in_specs=[pl.BlockSpec((n_src, cols), lambda i: (0, i))],
        out_specs=pl.BlockSpec((n_out, cols), lambda i: (0, i)),
        out_shape=jax.ShapeDtypeStruct((n_out, k), BF16),
        compiler_params=_cparams(1), name="realign_weights",
    )(wt)


def _prep_params(p):
    q = {}
    o_lr = 2 * GLA_QK + GLA_V
    o_r = o_lr + GLA_LR
    o_qkv = o_r + GLA_V
    o_a = o_qkv + GDN_CONV_W
    o_g = o_a + 2 * GDN_H
    q['w_in_ab'] = _realign_weights(
        p['w_in_ab'], [(0, o_lr, AB_Q), (o_r, GLA_V, AB_R), (o_qkv, GDN_CONV_W, AB_QKV), (o_g, GDN_V, AB_G),
                       (o_lr, GLA_LR, AB_SMALL + AB_LR_LANE), (o_a, 2 * GDN_H, AB_SMALL + AB_A_LANE)], AB_N)
    q['gla_w2'] = jnp.zeros((LANE, GLA_QK), F32).at[:GLA_LR].set(p['gla_w2'][0])
    q['gdn_alog'] = _lane_row(p['gdn_A_log'][0], AB_A_LANE)
    q['gdn_dtb'] = _lane_row(p['gdn_dt_bias'][0], AB_A_LANE)
    q['w_out_ab'] = p['w_out_ab'].astype(BF16)
    o_xbc = SSD_W
    o_dt = o_xbc + SSD_CONV_W
    o_u = o_dt + SSD_H
    q['w_in_cd'] = _realign_weights(
        p['w_in_cd'], [(o_xbc, SSD_CONV_W, CD_XBC), (0, SSD_W, CD_Z), (o_u, S5_W, CD_U), (o_dt, SSD_H, CD_SMALL)],
        CD_N)
    q['ssd_alog'] = _lane_row(p['ssd_A_log'][0], 0)
    q['ssd_dtb'] = _lane_row(p['ssd_dt_bias'][0], 0)
    q['ssd_d_row'] = jnp.repeat(p['ssd_D'][0].astype(F32), SSD_P).reshape(1, SSD_W)
    q['w_out_cd'] = p['w_out_cd'].astype(BF16)
    lb_re, lb_im, bb_re, bb_im = _s5_prep(p['s5_A_re'][0], p['s5_A_im'][0], p['s5_log_dt'][0],
                                          p['s5_B_re'][0], p['s5_B_im'][0])
    q['s5_lb_re'], q['s5_lb_im'] = lb_re.reshape(1, S5_STATE), lb_im.reshape(1, S5_STATE)
    q['s5_w_re'], q['s5_w_im'] = _block_diag(bb_re), _block_diag(bb_im)
    q['s5_c_re'] = _block_diag(jnp.swapaxes(p['s5_C_re'][0], 1, 2))
    q['s5_c_im'] = _block_diag(jnp.swapaxes(p['s5_C_im'][0], 1, 2))
    q['s5_glu_w'] = p['s5_glu_w'][0].astype(BF16)
    q['w_ffn_down'] = p['w_ffn_down'].astype(BF16)
    return q


def _trunk(x, mods, grp, seq_shape, state, p, q):
    nb, seq_len, valid = seq_shape
    s_gla, s_gdn, s_gdnc, s_ssd, s_ssdc, s_re, s_im, s_ffn = state
    prompt = not grp.per_token_mod

    def to_seq(t):
        if prompt:
            return t.reshape(nb, seq_len, t.shape[-1])
        t = jnp.swapaxes(t.reshape(valid, nb, t.shape[-1]), 0, 1)
        return jnp.pad(t, ((0, 0), (0, seq_len - valid), (0, 0)))

    def from_seq(t):
        if prompt:
            return t.reshape(nb * seq_len, t.shape[-1])
        return jnp.swapaxes(t[:, :valid], 0, 1).reshape(valid * nb, t.shape[-1])

    blk = MIX_BLOCK if prompt else seq_len
    chunks = (GLA_CHUNK, GDN_CHUNK, SSD_CHUNK) if prompt else (seq_len,) * 3
    new = {}

    grp_norm = grp.retiled(NORM_TILE)
    h = _norm_mod(x, p['g_mix'][0], mods[0], grp_norm, 1, 0)
    proj = to_seq(_matmul(h, q['w_in_ab'], grp, AB_N // 3))
    o_a, new['gla'] = _gla(proj, q['gla_w2'], p['gla_b2'][0], p['gla_norm_g'][0], s_gla, chunks[0],
                            GLA_GROUP if prompt else 1, blk, valid, 1 if prompt else SAMPLE_SEQS)
    o_b, new['gdn'], new['gdnc'] = _gdn(proj, p['gdn_conv_w'][0], p['gdn_conv_b'][0], q['gdn_alog'], q['gdn_dtb'],
                                        p['gdn_norm_g'][0], s_gdn, s_gdnc, chunks[1], blk, valid,
                                        1 if prompt else SAMPLE_SEQS)
    x, h = _mm_residual([(from_seq(o_a), q['w_out_ab'], 0, 0), (from_seq(o_b), q['w_out_ab'], 0, 1)],
                        x, mods[0], grp, 2, D_MODEL, norm=(p['g_ffn'][0], 4, 3))
    act, new['ffn0'] = _ffn_up(h, p['w_ffn_up'], p['ffn_conv_w'], p['ffn_conv_b'], 0, s_ffn[0],
                               grp.retiled(FFN_TILE))
    x = _mm_residual([(act, q['w_ffn_down'], 0, 0)], x, mods[0], grp, 5, DOWN_TN)

    h = _norm_mod(x, p['g_mix'][1], mods[1], grp_norm, 1, 0)
    proj2 = _matmul(h, q['w_in_cd'], grp, CD_N // 2)
    proj = to_seq(proj2)
    o_c, new['ssd'], new['ssdc'] = _ssd(proj, p['ssd_conv_w'][0], p['ssd_conv_b'][0], q['ssd_alog'], q['ssd_dtb'],
                                        q['ssd_d_row'], p['ssd_norm_g'][0], s_ssd, s_ssdc, chunks[2], blk, valid,
                                        1 if prompt else SAMPLE_SEQS)
    if prompt:
        pow_re, pow_im = _s5_pow_table(q['s5_lb_re'], q['s5_lb_im'], seq_len // SUBLANE)
        yd, new['re'], new['im'] = _s5_seg(proj, q['s5_w_re'], q['s5_w_im'], q['s5_c_re'], q['s5_c_im'],
                                           pow_re, pow_im, p['s5_D'][0], s_re, s_im)
    else:
        yd, new['re'], new['im'] = _s5(proj2.reshape(1, grp.n_tok, CD_N), q['s5_w_re'], q['s5_w_im'],
                                       q['s5_c_re'], q['s5_c_im'], q['s5_lb_re'], q['s5_lb_im'], p['s5_D'][0],
                                       s_re, s_im, nb, valid)
    o_d = _s5_glu(yd.reshape(grp.n_tok, S5_W), q['s5_glu_w'], p['s5_glu_b'][0], grp)
    x, h = _mm_residual([(from_seq(o_c), q['w_out_cd'], 0, 0), (o_d, q['w_out_cd'], 0, 1)], x, mods[1], grp, 2,
                        D_MODEL, norm=(p['g_ffn'][1], 4, 3))
    act, new['ffn1'] = _ffn_up(h, p['w_ffn_up'], p['ffn_conv_w'], p['ffn_conv_b'], 1, s_ffn[1],
                               grp.retiled(FFN_TILE))
    x = _mm_residual([(act, q['w_ffn_down'], 1, 0)], x, mods[1], grp, 5, DOWN_TN)
    return _final_rms(x, p['g_final'], grp_norm), new


def kernel(x_prompt, x_sample, c_prompt, c_sample, state_gla, state_gdn, state_gdn_conv, state_ssd, state_ssd_conv, state_s5_re, state_s5_im, state_ffn_conv, w_ada, b_ada, g_mix, g_ffn, w_in_ab, gla_w2, gla_b2, gla_norm_g, gdn_conv_w, gdn_conv_b, gdn_A_log, gdn_dt_bias, gdn_norm_g, w_out_ab, w_in_cd, ssd_conv_w, ssd_conv_b, ssd_A_log, ssd_dt_bias, ssd_D, ssd_norm_g, s5_A_re, s5_A_im, s5_B_re, s5_B_im, s5_C_re, s5_C_im, s5_D, s5_log_dt, s5_glu_w, s5_glu_b, w_out_cd, w_ffn_up, ffn_conv_w, ffn_conv_b, w_ffn_down, g_final):
    p = dict(g_mix=g_mix, g_ffn=g_ffn, w_in_ab=w_in_ab, gla_w2=gla_w2, gla_b2=gla_b2, gla_norm_g=gla_norm_g,
             gdn_conv_w=gdn_conv_w, gdn_conv_b=gdn_conv_b, gdn_A_log=gdn_A_log, gdn_dt_bias=gdn_dt_bias,
             gdn_norm_g=gdn_norm_g, w_out_ab=w_out_ab, w_in_cd=w_in_cd, ssd_conv_w=ssd_conv_w,
             ssd_conv_b=ssd_conv_b, ssd_A_log=ssd_A_log, ssd_dt_bias=ssd_dt_bias, ssd_D=ssd_D,
             ssd_norm_g=ssd_norm_g, s5_A_re=s5_A_re, s5_A_im=s5_A_im, s5_B_re=s5_B_re, s5_B_im=s5_B_im,
             s5_C_re=s5_C_re, s5_C_im=s5_C_im, s5_D=s5_D, s5_log_dt=s5_log_dt, s5_glu_w=s5_glu_w,
             s5_glu_b=s5_glu_b, w_out_cd=w_out_cd, w_ffn_up=w_ffn_up, ffn_conv_w=ffn_conv_w,
             ffn_conv_b=ffn_conv_b, w_ffn_down=w_ffn_down, g_final=g_final)
    bp, lp, d = x_prompt.shape
    bs, ls, _ = x_sample.shape
    q = _prep_params(p)

    bp_pad = -(-bp // SUBLANE) * SUBLANE
    c_all = jnp.concatenate([c_prompt, jnp.zeros((bp_pad - bp, d), F32), c_sample], axis=0)
    mod = _ada_mod(c_all, w_ada, b_ada)
    depth = w_ada.shape[0]
    mods_p = [mod[l, :bp].reshape(bp, 1, 6 * d) for l in range(depth)]
    mods_s = [jnp.tile(mod[l, bp_pad:], (ls, 1)).reshape(1, ls * bs, 6 * d) for l in range(depth)]

    tile_p = 512
    grp_p = _Group(bp * lp, tile_p, False, lp // tile_p, 1)
    zeros = lambda *shape: jnp.zeros(shape, F32)
    state_p = (zeros(bp, GLA_H, GLA_DK, GLA_DV), zeros(bp, GDN_H, GDN_DK, GDN_DV),
               zeros(bp, CONV_K - 1, GDN_CONV_W), zeros(bp, SSD_H, SSD_P, SSD_N),
               zeros(bp, CONV_K - 1, SSD_CONV_W), zeros(bp, 1, S5_STATE), zeros(bp, 1, S5_STATE),
               zeros(depth, bp, FFN_K - 1, 2 * D_FF))
    y_p, new_p = _trunk(x_prompt.reshape(bp * lp, d), mods_p, grp_p, (bp, lp, lp), state_p, p, q)

    grp_s = _Group(bs * ls, bs * ls, True, 1, bs)
    ffn_hist_s = jnp.swapaxes(state_ffn_conv, 1, 2).reshape(depth, 1, (FFN_K - 1) * bs, 2 * D_FF)
    state_s = (state_gla[0], state_gdn[0], state_gdn_conv[0], state_ssd[0], state_ssd_conv[0],
               state_s5_re.reshape(1, bs, S5_STATE), state_s5_im.reshape(1, bs, S5_STATE), ffn_hist_s)
    x_s = jnp.swapaxes(x_sample, 0, 1).reshape(ls * bs, d)
    y_s, new_s = _trunk(x_s, mods_s, grp_s, (bs, SAMPLE_PAD, ls), state_s, p, q)
    y_s = jnp.swapaxes(y_s.reshape(ls, bs, d), 0, 1)

    ffn_p = jnp.stack([new_p['ffn0'], new_p['ffn1']])
    ffn_s = jnp.stack([jnp.swapaxes(new_s[k].reshape(FFN_K - 1, bs, 2 * D_FF), 0, 1) for k in ('ffn0', 'ffn1')])
    s5_shape = lambda t, nb: t.reshape(1, nb, S5_G, S5_P)
    return (y_p.reshape(bp, lp, d), y_s,
            new_p['gla'][None], new_s['gla'][None], new_p['gdn'][None], new_s['gdn'][None],
            new_p['gdnc'][None], new_s['gdnc'][None], new_p['ssd'][None], new_s['ssd'][None],
            new_p['ssdc'][None], new_s['ssdc'][None],
            s5_shape(new_p['re'], bp), s5_shape(new_s['re'], bs), s5_shape(new_p['im'], bp), s5_shape(new_s['im'], bs),
            ffn_p, ffn_s)
```

```python
import functools
import math

import jax
import jax.numpy as jnp
from jax import lax
from jax.experimental import pallas as pl
from jax.experimental.pallas import tpu as pltpu

F32 = jnp.float32
BF16 = jnp.bfloat16
HIGHEST = lax.Precision.HIGHEST
EPS = 1e-6

D_MODEL = 2048
GLA_H, GLA_DK, GLA_DV, GLA_LR = 4, 128, 256, 16
GLA_GATE_NORM = 16.0
GLA_QK, GLA_V = GLA_H * GLA_DK, GLA_H * GLA_DV
GDN_H, GDN_DK, GDN_DV = 8, 128, 128
GDN_QK, GDN_V = GDN_H * GDN_DK, GDN_H * GDN_DV
CONV_K = 4
GDN_CONV_W = 2 * GDN_QK + GDN_V
SSD_P, SSD_H, SSD_G, SSD_N = 64, 16, 2, 128
SSD_W = SSD_H * SSD_P
SSD_CONV_W = SSD_W + 2 * SSD_G * SSD_N
S5_W, S5_GS, S5_G, S5_P = 1024, 16, 64, 64
S5_STATE = S5_G * S5_P
D_FF = 5632
FFN_K = 3

LANE = 128
SUBLANE = 8
VMEM_LIMIT = 48 * 1024 * 1024

AB_Q, AB_K, AB_V, AB_R, AB_QKV, AB_G, AB_SMALL, AB_N = 0, 512, 1024, 2048, 3072, 6144, 7168, 7296
AB_LR_LANE, AB_A_LANE, AB_B_LANE = 0, 16, 24
CD_XBC, CD_SMALL, CD_Z, CD_U, CD_N = 0, 1536, 2048, 3072, 4096

MIX_BLOCK = 256
GLA_CHUNK, GDN_CHUNK, SSD_CHUNK = 16, 128, 128
GLA_GROUP = 8
FFN_TILE = 1024
NORM_TILE = 1024
ADA_TN = 1024
FFN_TN = 512
DOWN_TN = 1024
REALIGN_COLS = 512
FFN_SUB = 256
INV_BLOCK = 16
SAMPLE_PAD = 8
SAMPLE_SEQS = 8


def _cparams(n_axes):
    return pltpu.CompilerParams(dimension_semantics=("arbitrary",) * n_axes, vmem_limit_bytes=VMEM_LIMIT)


def _sigmoid(x):
    return 1.0 / (1.0 + jnp.exp(-x))


def _silu(x):
    return x * _sigmoid(x)


def _softplus(x):
    return jnp.maximum(x, 0.0) + jnp.log(1.0 + jnp.exp(-jnp.abs(x)))


def _gelu_tanh(x):
    return 0.5 * x * (1.0 + jnp.tanh(math.sqrt(2.0 / math.pi) * (x + 0.044715 * (x * x * x))))


def _dot(a, b, precision=None):
    return jnp.dot(a, b, precision=precision, preferred_element_type=F32)


def _dot_nt(a, b, precision=None):
    return lax.dot_general(a, b, (((1,), (1,)), ((), ())), precision=precision, preferred_element_type=F32)


def _dot_tn(a, b, precision=None):
    return lax.dot_general(a, b, (((0,), (0,)), ((), ())), precision=precision, preferred_element_type=F32)


def _head_rms(o, g):
    return o * lax.rsqrt(jnp.mean(o * o, -1, keepdims=True) + EPS) * g


class _Group:
    def __init__(self, n_tok, tile, per_token_mod, tiles_per_seq, conv_shift):
        self.n_tok = n_tok
        self.tile = tile
        self.n_tiles = n_tok // tile
        self.per_token_mod = per_token_mod
        self.tiles_per_seq = tiles_per_seq
        self.conv_shift = conv_shift

    def retiled(self, tile):
        seq_rows = self.tile * self.tiles_per_seq
        if seq_rows % tile or self.n_tok % tile:
            return self
        return _Group(self.n_tok, tile, self.per_token_mod, seq_rows // tile, self.conv_shift)

    def mod_spec(self, width, col_block, m_axis):
        if self.per_token_mod:
            return pl.BlockSpec((1, self.tile, width), lambda *g: (0, g[m_axis], col_block(*g)))
        tps = self.tiles_per_seq
        return pl.BlockSpec((1, 1, width), lambda *g: (g[m_axis] // tps, 0, col_block(*g)))


def _ada_kernel(c_ref, w_ref, b_ref, o_ref):
    cs = _silu(c_ref[...]).astype(BF16)
    o_ref[0] = _dot(cs, w_ref[0].astype(BF16)) + b_ref[0]


def _ada_mod(c, w_ada, b_ada):
    depth, d, n = w_ada.shape
    rows = c.shape[0]
    tn = ADA_TN
    return pl.pallas_call(
        _ada_kernel,
        grid=(depth, n // tn),
        in_specs=[pl.BlockSpec((rows, d), lambda l, j: (0, 0)),
                  pl.BlockSpec((1, d, tn), lambda l, j: (l, 0, j)),
                  pl.BlockSpec((1, 1, tn), lambda l, j: (l, 0, j))],
        out_specs=pl.BlockSpec((1, rows, tn), lambda l, j: (l, 0, j)),
        out_shape=jax.ShapeDtypeStruct((depth, rows, n), F32),
        compiler_params=_cparams(2), name="ada_mod",
    )(c, w_ada, b_ada.reshape(depth, 1, n))


def _norm_mod_kernel(x_ref, g_ref, sc_ref, sh_ref, o_ref):
    x = x_ref[...]
    y = x * lax.rsqrt(jnp.mean(x * x, -1, keepdims=True) + EPS) * g_ref[...]
    o_ref[...] = (y * (1.0 + sc_ref[0]) + sh_ref[0]).astype(BF16)


def _norm_mod(x, g, mod, grp, sc_blk, sh_blk):
    d = x.shape[1]
    return pl.pallas_call(
        _norm_mod_kernel,
        grid=(grp.n_tiles,),
        in_specs=[pl.BlockSpec((grp.tile, d), lambda i: (i, 0)),
                  pl.BlockSpec((1, d), lambda i: (0, 0)),
                  grp.mod_spec(d, lambda i: sc_blk, 0),
                  grp.mod_spec(d, lambda i: sh_blk, 0)],
        out_specs=pl.BlockSpec((grp.tile, d), lambda i: (i, 0)),
        out_shape=jax.ShapeDtypeStruct(x.shape, BF16),
        compiler_params=_cparams(1), name="norm_mod",
    )(x, g.reshape(1, d), mod, mod)


def _rms_kernel(x_ref, g_ref, o_ref):
    x = x_ref[...]
    o_ref[...] = x * lax.rsqrt(jnp.mean(x * x, -1, keepdims=True) + EPS) * g_ref[...]


def _final_rms(x, g, grp):
    d = x.shape[1]
    return pl.pallas_call(
        _rms_kernel,
        grid=(grp.n_tiles,),
        in_specs=[pl.BlockSpec((grp.tile, d), lambda i: (i, 0)), pl.BlockSpec((1, d), lambda i: (0, 0))],
        out_specs=pl.BlockSpec((grp.tile, d), lambda i: (i, 0)),
        out_shape=jax.ShapeDtypeStruct(x.shape, F32),
        compiler_params=_cparams(1), name="final_rms",
    )(x, g.reshape(1, d))


def _matmul_kernel(a_ref, wt_ref, o_ref):
    o_ref[...] = _dot_nt(a_ref[...], wt_ref[...])


def _matmul(a, wt, grp, tn):
    n, k = wt.shape
    return pl.pallas_call(
        _matmul_kernel,
        grid=(n // tn, grp.n_tiles),
        in_specs=[pl.BlockSpec((grp.tile, k), lambda j, i: (i, 0)),
                  pl.BlockSpec((tn, k), lambda j, i: (j, 0))],
        out_specs=pl.BlockSpec((grp.tile, tn), lambda j, i: (i, j)),
        out_shape=jax.ShapeDtypeStruct((a.shape[0], n), F32),
        compiler_params=_cparams(2), name="matmul",
    )(a, wt)


def _mm_res_kernel(*refs, n_pairs, with_norm):
    x_ref, gate_ref = refs[2 * n_pairs:2 * n_pairs + 2]
    y = _dot(refs[0][...].astype(BF16), refs[1][0])
    for p in range(1, n_pairs):
        y = y + _dot(refs[2 * p][...].astype(BF16), refs[2 * p + 1][0])
    x_new = x_ref[...] + gate_ref[0] * y
    if not with_norm:
        o_ref, = refs[2 * n_pairs + 2:]
        o_ref[...] = x_new
        return
    g_ref, sc_ref, sh_ref, o_ref, h_ref = refs[2 * n_pairs + 2:]
    o_ref[...] = x_new
    normed = x_new * lax.rsqrt(jnp.mean(x_new * x_new, -1, keepdims=True) + EPS) * g_ref[...]
    h_ref[...] = (normed * (1.0 + sc_ref[0]) + sh_ref[0]).astype(BF16)


def _mm_residual(pairs, x, mod, grp, gate_blk, tn, norm=None):
    n = x.shape[1]
    if norm is not None:
        assert tn == n
        return _mm_residual_norm(pairs, x, mod, grp, gate_blk, norm)
    in_specs, args = [], []
    for a, w, layer, row_blk in pairs:
        k = a.shape[1]
        in_specs += [pl.BlockSpec((grp.tile, k), lambda j, i: (i, 0)),
                     pl.BlockSpec((1, k, tn), lambda j, i, layer=layer, row_blk=row_blk: (layer, row_blk, j))]
        args += [a, w]
    in_specs += [pl.BlockSpec((grp.tile, tn), lambda j, i: (i, j)),
                 grp.mod_spec(tn, lambda j, i: gate_blk * (n // tn) + j, 1)]
    return pl.pallas_call(
        functools.partial(_mm_res_kernel, n_pairs=len(pairs), with_norm=False),
        grid=(n // tn, grp.n_tiles),
        in_specs=in_specs,
        out_specs=pl.BlockSpec((grp.tile, tn), lambda j, i: (i, j)),
        out_shape=jax.ShapeDtypeStruct(x.shape, F32),
        compiler_params=_cparams(2), name="mm_residual",
    )(*args, x, mod)


def _mm_residual_norm(pairs, x, mod, grp, gate_blk, norm):
    n = x.shape[1]
    gain, sc_blk, sh_blk = norm
    in_specs, args = [], []
    for a, w, layer, row_blk in pairs:
        k = a.shape[1]
        in_specs += [pl.BlockSpec((grp.tile, k), lambda j, i: (i, 0)),
                     pl.BlockSpec((1, k, n), lambda j, i, layer=layer, row_blk=row_blk: (layer, row_blk, 0))]
        args += [a, w]
    row_spec = pl.BlockSpec((grp.tile, n), lambda j, i: (i, 0))
    in_specs += [row_spec, grp.mod_spec(n, lambda j, i: gate_blk, 1), pl.BlockSpec((1, n), lambda j, i: (0, 0)),
                 grp.mod_spec(n, lambda j, i: sc_blk, 1), grp.mod_spec(n, lambda j, i: sh_blk, 1)]
    return pl.pallas_call(
        functools.partial(_mm_res_kernel, n_pairs=len(pairs), with_norm=True),
        grid=(1, grp.n_tiles),
        in_specs=in_specs,
        out_specs=[row_spec, row_spec],
        out_shape=[jax.ShapeDtypeStruct(x.shape, F32), jax.ShapeDtypeStruct(x.shape, BF16)],
        compiler_params=_cparams(2), name="mm_residual_norm",
    )(*args, x, mod, gain.reshape(1, n), mod, mod)


def _ffn_up_kernel(h_ref, wa_ref, wg_ref, cwa_ref, cwg_ref, cba_ref, cbg_ref, ha_ref, hg_ref,
                   act_ref, sta_ref, stg_ref, scr_a, scr_g, wba_scr, wbg_scr, *, shift, tile, sub, tiles_per_seq):
    i = pl.program_id(1)
    hist = (FFN_K - 1) * shift
    base = -(-hist // SUBLANE) * SUBLANE

    @pl.when(i == 0)
    def _():
        wba_scr[...] = wa_ref[0].astype(BF16)
        wbg_scr[...] = wg_ref[0].astype(BF16)

    @pl.when(i % tiles_per_seq == 0)
    def _():
        scr_a[base - hist:base, :] = ha_ref[0]
        scr_g[base - hist:base, :] = hg_ref[0]

    def conv(scr, cw_ref, cb_ref, r0):
        y = cb_ref[0]
        for j in range(FFN_K):
            lo = base + r0 - (FFN_K - 1 - j) * shift
            y = y + scr[lo:lo + sub, :] * cw_ref[0, j:j + 1, :]
        return y

    def project(r0):
        h = h_ref[r0:r0 + sub, :]
        scr_a[base + r0:base + r0 + sub, :] = _dot(h, wba_scr[...])
        scr_g[base + r0:base + r0 + sub, :] = _dot(h, wbg_scr[...])

    project(0)
    for r0 in range(0, tile, sub):
        if r0 + sub < tile:
            project(r0 + sub)
        a = conv(scr_a, cwa_ref, cba_ref, r0)
        g = conv(scr_g, cwg_ref, cbg_ref, r0)
        act_ref[r0:r0 + sub, :] = (_silu(g) * a).astype(BF16)
    last_a = scr_a[base + tile - hist:base + tile, :]
    last_g = scr_g[base + tile - hist:base + tile, :]
    sta_ref[0] = last_a
    stg_ref[0] = last_g
    scr_a[base - hist:base, :] = last_a
    scr_g[base - hist:base, :] = last_g


def _ffn_up(h, w_up, conv_w, conv_b, layer, hist0, grp):
    d = h.shape[1]
    tn = FFN_TN
    nj = D_FF // tn
    shift = grp.conv_shift
    hist = (FFN_K - 1) * shift
    base = -(-hist // SUBLANE) * SUBLANE
    n_seq = grp.n_tiles // grp.tiles_per_seq
    tps = grp.tiles_per_seq
    cb = conv_b.reshape(conv_b.shape[0], 1, 2 * D_FF)
    kern = functools.partial(_ffn_up_kernel, shift=shift, tile=grp.tile, sub=min(FFN_SUB, grp.tile),
                             tiles_per_seq=tps)
    assert grp.tile % min(FFN_SUB, grp.tile) == 0
    act, st_a, st_g = pl.pallas_call(
        kern,
        grid=(nj, grp.n_tiles),
        in_specs=[pl.BlockSpec((grp.tile, d), lambda j, i: (i, 0)),
                  pl.BlockSpec((1, d, tn), lambda j, i: (layer, 0, j)),
                  pl.BlockSpec((1, d, tn), lambda j, i: (layer, 0, nj + j)),
                  pl.BlockSpec((1, FFN_K, tn), lambda j, i: (layer, 0, j)),
                  pl.BlockSpec((1, FFN_K, tn), lambda j, i: (layer, 0, nj + j)),
                  pl.BlockSpec((1, 1, tn), lambda j, i: (layer, 0, j)),
                  pl.BlockSpec((1, 1, tn), lambda j, i: (layer, 0, nj + j)),
                  pl.BlockSpec((1, hist, tn), lambda j, i: (i // tps, 0, j)),
                  pl.BlockSpec((1, hist, tn), lambda j, i: (i // tps, 0, nj + j))],
        out_specs=[pl.BlockSpec((grp.tile, tn), lambda j, i: (i, j)),
                   pl.BlockSpec((1, hist, tn), lambda j, i: (i // tps, 0, j)),
                   pl.BlockSpec((1, hist, tn), lambda j, i: (i // tps, 0, j))],
        out_shape=[jax.ShapeDtypeStruct((h.shape[0], D_FF), BF16),
                   jax.ShapeDtypeStruct((n_seq, hist, D_FF), F32),
                   jax.ShapeDtypeStruct((n_seq, hist, D_FF), F32)],
        scratch_shapes=[pltpu.VMEM((base + grp.tile, tn), F32), pltpu.VMEM((base + grp.tile, tn), F32),
                        pltpu.VMEM((d, tn), BF16), pltpu.VMEM((d, tn), BF16)],
        compiler_params=_cparams(2), name="ffn_up",
    )(h, w_up, w_up, conv_w, conv_w, cb, cb, hist0, hist0)
    return act, jnp.concatenate([st_a, st_g], axis=-1)


def _glu_kernel(y_ref, w_ref, b_ref, o_ref):
    z5 = _gelu_tanh(y_ref[...])
    o_ref[...] = (z5 * _sigmoid(_dot(z5.astype(BF16), w_ref[...]) + b_ref[...])).astype(o_ref.dtype)


def _s5_glu(yd, w, b, grp):
    n = yd.shape[1]
    return pl.pallas_call(
        _glu_kernel,
        grid=(grp.n_tiles,),
        in_specs=[pl.BlockSpec((grp.tile, n), lambda i: (i, 0)),
                  pl.BlockSpec((n, n), lambda i: (0, 0)),
                  pl.BlockSpec((1, n), lambda i: (0, 0))],
        out_specs=pl.BlockSpec((grp.tile, n), lambda i: (i, 0)),
        out_shape=jax.ShapeDtypeStruct(yd.shape, BF16),
        compiler_params=_cparams(1), name="s5_glu",
    )(yd, w, b.reshape(1, n))


def _round_robin(gens):
    gens = list(gens)
    while gens:
        alive = []
        for gen in gens:
            try:
                next(gen)
                alive.append(gen)
            except StopIteration:
                pass
        gens = alive


def _mixer_out_dtype(block):
    return BF16 if block % (2 * SUBLANE) == 0 else F32


def _causal_conv_chunk(x, cv_scr, cw_ref, cb_ref, chunk, valid):
    base = SUBLANE
    cv_scr[base:base + chunk, :] = x
    y = cb_ref[...]
    for j in range(CONV_K):
        lo = base - (CONV_K - 1) + j
        y = y + cv_scr[lo:lo + chunk, :] * cw_ref[j:j + 1, :]
    last = cv_scr[base + valid - (CONV_K - 1):base + valid, :]
    cv_scr[base - (CONV_K - 1):base, :] = last
    return y


def _gla_kernel(q_ref, k_ref, v_ref, r_ref, sm_ref, w2_ref, b2_ref, ng_ref, s0_ref,
                o_ref, sout_ref, s_scr, b_scr, *, chunk, group, block, valid, nseq):
    blk = pl.program_id(1)

    @pl.when(blk == 0)
    def _():
        s_scr[...] = s0_ref[...]

    span = chunk * group
    lane = lax.broadcasted_iota(jnp.int32, (1, GLA_DK), 1)
    ones_kk = jnp.ones((GLA_DK, GLA_DK), BF16)
    row_in_chunk = lax.broadcasted_iota(jnp.int32, (span, 1), 0) % chunk
    eye = (lax.broadcasted_iota(jnp.int32, (GLA_DK, GLA_DK), 0)
           == lax.broadcasted_iota(jnp.int32, (GLA_DK, GLA_DK), 1))
    n_valid = min(valid, chunk)
    heads = range(GLA_H)
    chunks = range(group)

    def to3(t):
        return t.reshape(group, chunk, t.shape[-1])

    ri = lax.broadcasted_iota(jnp.int32, (block, block), 0)
    ci = lax.broadcasted_iota(jnp.int32, (block, block), 1)
    tri = ((ri // chunk == ci // chunk) & (ri >= ci)).astype(BF16)
    row_blk = lax.broadcasted_iota(jnp.int32, (block, 1), 0) % chunk
    for sq in range(nseq):
        x = _dot(sm_ref[sq].astype(BF16), w2_ref[...].astype(BF16)) + b2_ref[...]
        log_a = (jnp.minimum(x, 0.0) - jnp.log(1.0 + jnp.exp(-jnp.abs(x)))) * (1.0 / GLA_GATE_NORM)
        if valid < chunk:
            log_a = jnp.where(row_blk < valid, log_a, 0.0)
        b_scr[sq] = sum(_dot(tri, part) for part in _split3(log_a))

    def one_seq(sq, rows):
        b = b_scr[sq, rows, :]
        q = q_ref[sq, rows, :] * GLA_DK ** -0.5
        k = k_ref[sq, rows, :]
        v = v_ref[sq, rows, :]
        r = r_ref[sq, rows, :]
        b3, q_in, kv, d_col = [], [], [], []
        for h in heads:
            ks = slice(h * GLA_DK, (h + 1) * GLA_DK)
            bh3 = to3(b[:, ks])
            b_last = bh3[:, chunk - 1:chunk, :]
            k_out = (to3(k[:, ks]) * jnp.exp(b_last - bh3)).reshape(span, GLA_DK)
            vh = v[:, h * GLA_DV:(h + 1) * GLA_DV]
            b3.append(bh3)
            q_in.append(q[:, ks] * jnp.exp(b[:, ks]))
            kv.append([_dot_tn(k_out[c * chunk:(c + 1) * chunk], vh[c * chunk:(c + 1) * chunk]) for c in chunks])
            d_col.append([jnp.sum(jnp.where(eye, jnp.exp(b_last[c]), 0.0), axis=-1, keepdims=True) for c in chunks])
        yield
        st = [s_scr[sq, h] for h in heads]
        o_inter = [[] for _ in heads]
        for c in chunks:
            for h in heads:
                o_inter[h].append(_dot(q_in[h][c * chunk:(c + 1) * chunk], st[h]))
                st[h] = st[h] * d_col[h][c] + kv[h][c]
            yield
        for h in heads:
            ks = slice(h * GLA_DK, (h + 1) * GLA_DK)
            vs = slice(h * GLA_DV, (h + 1) * GLA_DV)
            s_scr[sq, h] = st[h]
            bh3, qh3, kh3, vh3 = b3[h], to3(q[:, ks]), to3(k[:, ks]), to3(v[:, vs])
            prods = []
            for j in range(n_valid):
                e = jnp.exp(bh3 - bh3[:, j:j + 1, :])
                prods.append((qh3 * e * kh3[:, j:j + 1, :]).reshape(span, GLA_DK).astype(BF16))
            sums = _dot(jnp.concatenate(prods, axis=0), ones_kk)
            yield
            att = jnp.zeros((span, GLA_DK), F32)
            for j in range(n_valid):
                att = jnp.where(lane == j, sums[j * span:(j + 1) * span], att)
            att = jnp.where(row_in_chunk >= lane, att, 0.0)[:, :chunk]
            vh = v[:, vs]
            o_intra = [_dot(att[c * chunk:(c + 1) * chunk], vh[c * chunk:(c + 1) * chunk]) for c in chunks]
            o = jnp.concatenate(o_intra, axis=0) + jnp.concatenate(o_inter[h], axis=0)
            o_ref[sq, rows, vs] = (_head_rms(o, ng_ref[...]) * _silu(r[:, vs])).astype(o_ref.dtype)
            yield

    def do_span(s, carry):
        rows = pl.ds(pl.multiple_of(s * span, span), span)
        _round_robin(one_seq(sq, rows) for sq in range(nseq))
        return carry

    lax.fori_loop(0, block // span, do_span, 0)

    @pl.when(blk == pl.num_programs(1) - 1)
    def _():
        sout_ref[...] = s_scr[...]


def _gla(proj, w2p, b2, ng, s0, chunk, group, block, valid, nseq):
    nb, seq, _ = proj.shape

    def col(width, off):
        return pl.BlockSpec((nseq, block, width), lambda b, i: (b, i, off // width))

    st_spec = pl.BlockSpec((nseq, GLA_H, GLA_DK, GLA_DV), lambda b, i: (b, 0, 0, 0))
    kern = functools.partial(_gla_kernel, chunk=chunk, group=group, block=block, valid=valid, nseq=nseq)
    return pl.pallas_call(
        kern,
        grid=(nb // nseq, seq // block),
        in_specs=[col(GLA_QK, AB_Q), col(GLA_QK, AB_K), col(GLA_V, AB_V), col(GLA_V, AB_R), col(LANE, AB_SMALL),
                  pl.BlockSpec((LANE, GLA_QK), lambda b, i: (0, 0)),
                  pl.BlockSpec((1, GLA_QK), lambda b, i: (0, 0)),
                  pl.BlockSpec((1, GLA_DV), lambda b, i: (0, 0)),
                  st_spec],
        out_specs=[pl.BlockSpec((nseq, block, GLA_V), lambda b, i: (b, i, 0)), st_spec],
        out_shape=[jax.ShapeDtypeStruct((nb, seq, GLA_V), _mixer_out_dtype(block)),
                   jax.ShapeDtypeStruct((nb, GLA_H, GLA_DK, GLA_DV), F32)],
        scratch_shapes=[pltpu.VMEM((nseq, GLA_H, GLA_DK, GLA_DV), F32), pltpu.VMEM((nseq, block, GLA_QK), F32)],
        compiler_params=_cparams(2), name="gla",
    )(proj, proj, proj, proj, proj, w2p, b2.reshape(1, GLA_QK), ng.reshape(1, GLA_DV), s0)


def _split2(a):
    hi = a.astype(BF16)
    return hi, (a - hi.astype(F32)).astype(BF16)


def _split3(a):
    hi = a.astype(BF16)
    rest = a - hi.astype(F32)
    mid = rest.astype(BF16)
    return hi, mid, (rest - mid.astype(F32)).astype(BF16)


def _dot3(a, b):
    return _dot(a[0], b[0]) + _dot(a[0], b[1]) + _dot(a[1], b[0])


def _inv_unit_lower_many(mats, n, eye):
    ps = [eye - a for a in mats]
    if n <= 2:
        return ps
    pows = [_split2(a) for a in mats]
    k = 2
    pending = None
    while k < n:
        sq = [_dot3(a, a) for a in pows]
        if pending is not None:
            ps = [p + _dot3(_split2(p), f) for p, f in zip(ps, pending)]
        pows = [_split2(a) for a in sq]
        pending = pows
        k *= 2
    return [p + _dot3(_split2(p), f) for p, f in zip(ps, pending)]


def _gdn_kernel(qkv_ref, sm_ref, gb_ref, cw_ref, cb_ref, alog_ref, dtb_ref, ng_ref, s0_ref, c0_ref,
                o_ref, sout_ref, cout_ref, s_scr, cv_scr, *, chunk, block, valid, nseq):
    blk = pl.program_id(1)

    @pl.when(blk == 0)
    def _():
        s_scr[...] = s0_ref[...]
        cv_scr[:, SUBLANE - (CONV_K - 1):SUBLANE, :] = c0_ref[...]

    ri = lax.broadcasted_iota(jnp.int32, (chunk, chunk), 0)
    ci = lax.broadcasted_iota(jnp.int32, (chunk, chunk), 1)
    causal = ri >= ci
    strict = ri > ci
    eye = (ri == ci).astype(F32)
    tri = causal.astype(F32)
    tri_u = (ri <= ci).astype(F32)
    inv_blk = min(INV_BLOCK, chunk)
    same_blk = (ri // inv_blk) == (ci // inv_blk)
    row = lax.broadcasted_iota(jnp.int32, (chunk, 1), 0)
    n_valid = min(valid, chunk)

    units = [(sq, h) for sq in range(nseq) for h in range(GDN_H)]
    heads = range(len(units))

    def prelude(rows, out):
        q, k, kb, rhs, dec, gcc = [], [], [], [], [], []
        for sq, h in units:
            yield
            if h == 0:
                act = _silu(_causal_conv_chunk(qkv_ref[sq, rows, :], cv_scr.at[sq], cw_ref, cb_ref, chunk, n_valid))
                sm = sm_ref[sq, rows, :]
                g_all = -jnp.exp(alog_ref[...]) * _softplus(sm + dtb_ref[...])
                beta_all = _sigmoid(sm)
                if valid < chunk:
                    g_all = jnp.where(row < valid, g_all, 0.0)
                    beta_all = jnp.where(row < valid, beta_all, 0.0)
                gc = _dot(tri, g_all, HIGHEST)
                gc_r = _dot_tn(g_all, tri_u, HIGHEST)
            qh = act[:, h * GDN_DK:(h + 1) * GDN_DK]
            kh = act[:, GDN_QK + h * GDN_DK:GDN_QK + (h + 1) * GDN_DK]
            vh = act[:, 2 * GDN_QK + h * GDN_DV:2 * GDN_QK + (h + 1) * GDN_DV]
            qh = qh * lax.rsqrt(jnp.sum(qh * qh, -1, keepdims=True) + EPS) * GDN_DK ** -0.5
            kh = kh * lax.rsqrt(jnp.sum(kh * kh, -1, keepdims=True) + EPS)
            beta = beta_all[:, AB_B_LANE + h:AB_B_LANE + h + 1]
            gch = gc[:, AB_A_LANE + h:AB_A_LANE + h + 1]
            gcr = gc_r[AB_A_LANE + h:AB_A_LANE + h + 1, :]
            q.append(qh)
            k.append(kh)
            kb.append(kh * beta)
            rhs.append(_split2(jnp.concatenate([vh * beta, kb[-1] * jnp.exp(gch)], axis=1)))
            dec.append(jnp.exp(jnp.where(causal, gch - gcr, -jnp.inf)))
            gcc.append(gch)
        yield
        kbf = [a.astype(BF16) for a in k]
        m = [jnp.where(strict, _dot_nt(kb[h].astype(BF16), kbf[h]) * dec[h], 0.0) for h in heads]
        att = [_dot_nt(q[h].astype(BF16), kbf[h]) * dec[h] for h in heads]
        out['pre'] = (q, k, rhs, gcc, m, att)

    def finish(pre, rows):
        q, k, rhs, gcc, m, att = pre
        m_diag = [jnp.where(same_blk, a, 0.0) for a in m]
        t = [_split2(a) for a in _inv_unit_lower_many(m_diag, inv_blk, eye)]
        yield
        y = [_dot3(t[h], rhs[h]) for h in heads]
        if chunk > inv_blk:
            n_off = [_dot3(t[h], _split2(m[h] - m_diag[h])) for h in heads]
            yield
            qn = [_split2(a) for a in _inv_unit_lower_many(n_off, chunk // inv_blk, eye)]
            yield
            y = [_dot3(qn[h], _split2(y[h])) for h in heads]
        yield
        st = [s_scr[sq, h] for sq, h in units]
        stb = [a.astype(BF16) for a in st]
        v_new = [y[h][:, :GDN_DV] - _dot(y[h][:, GDN_DV:].astype(BF16), stb[h]) for h in heads]
        vnb = [a.astype(BF16) for a in v_new]
        yield
        o = [_dot((q[h] * jnp.exp(gcc[h])).astype(BF16), stb[h]) + _dot(att[h].astype(BF16), vnb[h]) for h in heads]
        yield
        for u, (sq, h) in enumerate(units):
            hs = slice(h * GDN_DV, (h + 1) * GDN_DV)
            g_last = gcc[u][chunk - 1:chunk, :]
            k_out = (k[u] * jnp.exp(g_last - gcc[u])).astype(BF16)
            s_scr[sq, h] = st[u] * jnp.exp(g_last) + _dot_tn(k_out, vnb[u])
            o_ref[sq, rows, hs] = (_head_rms(o[u], ng_ref[...]) * _silu(gb_ref[sq, rows, hs])).astype(o_ref.dtype)

    n_chunks = block // chunk
    cur = {}
    _round_robin([prelude(pl.ds(0, chunk), cur)])
    for c in range(n_chunks):
        nxt = {}
        stages = [finish(cur['pre'], pl.ds(c * chunk, chunk))]
        if c + 1 < n_chunks:
            stages.append(prelude(pl.ds((c + 1) * chunk, chunk), nxt))
        _round_robin(stages)
        cur = nxt

    @pl.when(blk == pl.num_programs(1) - 1)
    def _():
        sout_ref[...] = s_scr[...]
        cout_ref[...] = cv_scr[:, SUBLANE - (CONV_K - 1):SUBLANE, :]


def _gdn(proj, conv_w, conv_b, alog_row, dtb_row, ng, s0, c0, chunk, block, valid, nseq):
    nb, seq, _ = proj.shape

    def col(width, off):
        return pl.BlockSpec((nseq, block, width), lambda b, i: (b, i, off // width))

    def const(shape):
        return pl.BlockSpec(shape, lambda b, i: (0,) * len(shape))

    st_spec = pl.BlockSpec((nseq, GDN_H, GDN_DK, GDN_DV), lambda b, i: (b, 0, 0, 0))
    cv_spec = pl.BlockSpec((nseq, CONV_K - 1, GDN_CONV_W), lambda b, i: (b, 0, 0))
    kern = functools.partial(_gdn_kernel, chunk=chunk, block=block, valid=valid, nseq=nseq)
    return pl.pallas_call(
        kern,
        grid=(nb // nseq, seq // block),
        in_specs=[col(GDN_CONV_W, AB_QKV), col(LANE, AB_SMALL), col(GDN_V, AB_G),
                  const((CONV_K, GDN_CONV_W)), const((1, GDN_CONV_W)), const((1, LANE)), const((1, LANE)),
                  const((1, GDN_DV)), st_spec, cv_spec],
        out_specs=[pl.BlockSpec((nseq, block, GDN_V), lambda b, i: (b, i, 0)), st_spec, cv_spec],
        out_shape=[jax.ShapeDtypeStruct((nb, seq, GDN_V), _mixer_out_dtype(block)),
                   jax.ShapeDtypeStruct((nb, GDN_H, GDN_DK, GDN_DV), F32),
                   jax.ShapeDtypeStruct((nb, CONV_K - 1, GDN_CONV_W), F32)],
        scratch_shapes=[pltpu.VMEM((nseq, GDN_H, GDN_DK, GDN_DV), F32),
                        pltpu.VMEM((nseq, SUBLANE + chunk, GDN_CONV_W), F32)],
        compiler_params=_cparams(2), name="gdn",
    )(proj, proj, proj, conv_w, conv_b.reshape(1, GDN_CONV_W), alog_row, dtb_row, ng.reshape(1, GDN_DV), s0, c0)


def _ssd_kernel(xbc_ref, sm_ref, z_ref, cw_ref, cb_ref, alog_ref, dtb_ref, drow_ref, ng_ref, s0_ref, c0_ref,
                o_ref, sout_ref, cout_ref, s_scr, cv_scr, dt_scr, acs_scr, acsr_scr, dtr_scr,
                *, chunk, block, valid, nseq):
    blk = pl.program_id(1)

    @pl.when(blk == 0)
    def _():
        s_scr[...] = s0_ref[...]
        cv_scr[:, SUBLANE - (CONV_K - 1):SUBLANE, :] = c0_ref[...]

    ri = lax.broadcasted_iota(jnp.int32, (chunk, chunk), 0)
    ci = lax.broadcasted_iota(jnp.int32, (chunk, chunk), 1)
    causal = ri >= ci
    eye = (ri == ci).astype(BF16)
    tri = causal.astype(BF16)
    tri_u = (ri <= ci).astype(BF16)
    row = lax.broadcasted_iota(jnp.int32, (chunk, 1), 0)
    lane_lo = lax.broadcasted_iota(jnp.int32, (chunk, LANE), 1) < SSD_P
    row_lo = lax.broadcasted_iota(jnp.int32, (2 * SSD_P, 1), 0) < SSD_P
    n_valid = min(valid, chunk)
    heads_per_group = SSD_H // SSD_G
    gsz = SSD_W // SSD_G

    for sq in range(nseq):
        for c in range(block // chunk):
            dt = _softplus(sm_ref[sq, c * chunk:(c + 1) * chunk, :] + dtb_ref[...])
            if valid < chunk:
                dt = jnp.where(row < valid, dt, 0.0)
            dta = _split3(dt * (-jnp.exp(alog_ref[...])))
            dt_scr[sq, c] = dt
            acs_scr[sq, c] = sum(_dot(tri, part) for part in dta)
            acsr_scr[sq, c] = sum(_dot_tn(part, tri_u) for part in dta)
            dtr_scr[sq, c] = sum(_dot_tn(part, eye) for part in _split3(dt))

    def one_seq(sq, s, rows):
        act = _silu(_causal_conv_chunk(xbc_ref[sq, rows, :], cv_scr.at[sq], cw_ref, cb_ref, chunk, n_valid))
        dt, acs, acs_r, dt_r = dt_scr[sq, s], acs_scr[sq, s], acsr_scr[sq, s], dtr_scr[sq, s]
        z = z_ref[sq, rows, :]
        for g in range(SSD_G):
            bg = act[:, SSD_W + g * SSD_N:SSD_W + (g + 1) * SSD_N]
            cg = act[:, SSD_W + SSD_G * SSD_N + g * SSD_N:SSD_W + SSD_G * SSD_N + (g + 1) * SSD_N]
            cb = _dot_nt(cg, bg)
            yield
            parts = []
            for pr in range(heads_per_group // 2):
                pi = g * (heads_per_group // 2) + pr
                xp = act[:, pi * LANE:(pi + 1) * LANE]
                st = s_scr[sq, pi]
                y_in, e_in, w_out, d_last = [], [], [], []
                for hh in range(2):
                    h = 2 * pi + hh
                    ac = acs[:, h:h + 1]
                    dec = jnp.exp(jnp.where(causal, ac - acs_r[h:h + 1, :], -jnp.inf))
                    y_in.append(_dot(cb * dec * dt_r[h:h + 1, :], xp))
                    a_last = ac[chunk - 1:chunk, :]
                    e_in.append(jnp.exp(ac))
                    w_out.append(jnp.exp(a_last - ac) * dt[:, h:h + 1])
                    d_last.append(jnp.exp(a_last))
                y = jnp.where(lane_lo, y_in[0], y_in[1])
                y = y + _dot_nt(cg, st) * jnp.where(lane_lo, e_in[0], e_in[1])
                y = y + drow_ref[:, pi * LANE:(pi + 1) * LANE] * xp
                x_sc = xp * jnp.where(lane_lo, w_out[0], w_out[1])
                s_scr[sq, pi] = st * jnp.where(row_lo, d_last[0], d_last[1]) + _dot_tn(x_sc, bg)
                parts.append(y)
                yield
            gs = slice(g * gsz, (g + 1) * gsz)
            yg = jnp.concatenate(parts, axis=1) * _silu(z[:, gs])
            o_ref[sq, rows, gs] = _head_rms(yg, ng_ref[:, gs]).astype(o_ref.dtype)

    def do_chunk(s, carry):
        rows = pl.ds(pl.multiple_of(s * chunk, chunk), chunk)
        _round_robin(one_seq(sq, s, rows) for sq in range(nseq))
        return carry

    lax.fori_loop(0, block // chunk, do_chunk, 0)

    @pl.when(blk == pl.num_programs(1) - 1)
    def _():
        sout_ref[...] = s_scr[...]
        cout_ref[...] = cv_scr[:, SUBLANE - (CONV_K - 1):SUBLANE, :]


def _ssd(proj, conv_w, conv_b, alog_row, dtb_row, d_row, ng, s0, c0, chunk, block, valid, nseq):
    nb, seq, _ = proj.shape
    n_pairs = SSD_H // 2

    def col(width, off):
        return pl.BlockSpec((nseq, block, width), lambda b, i: (b, i, off // width))

    def const(shape):
        return pl.BlockSpec(shape, lambda b, i: (0,) * len(shape))

    st_spec = pl.BlockSpec((nseq, n_pairs, 2 * SSD_P, SSD_N), lambda b, i: (b, 0, 0, 0))
    cv_spec = pl.BlockSpec((nseq, CONV_K - 1, SSD_CONV_W), lambda b, i: (b, 0, 0))
    kern = functools.partial(_ssd_kernel, chunk=chunk, block=block, valid=valid, nseq=nseq)
    o, s_new, c_new = pl.pallas_call(
        kern,
        grid=(nb // nseq, seq // block),
        in_specs=[col(SSD_CONV_W, CD_XBC), col(LANE, CD_SMALL), col(SSD_W, CD_Z),
                  const((CONV_K, SSD_CONV_W)), const((1, SSD_CONV_W)), const((1, LANE)), const((1, LANE)),
                  const((1, SSD_W)), const((1, SSD_W)), st_spec, cv_spec],
        out_specs=[pl.BlockSpec((nseq, block, SSD_W), lambda b, i: (b, i, 0)), st_spec, cv_spec],
        out_shape=[jax.ShapeDtypeStruct((nb, seq, SSD_W), _mixer_out_dtype(block)),
                   jax.ShapeDtypeStruct((nb, n_pairs, 2 * SSD_P, SSD_N), F32),
                   jax.ShapeDtypeStruct((nb, CONV_K - 1, SSD_CONV_W), F32)],
        scratch_shapes=[pltpu.VMEM((nseq, n_pairs, 2 * SSD_P, SSD_N), F32),
                        pltpu.VMEM((nseq, SUBLANE + chunk, SSD_CONV_W), F32),
                        pltpu.VMEM((nseq, block // chunk, chunk, LANE), F32),
                        pltpu.VMEM((nseq, block // chunk, chunk, LANE), F32),
                        pltpu.VMEM((nseq, block // chunk, LANE, chunk), F32),
                        pltpu.VMEM((nseq, block // chunk, LANE, chunk), F32)],
        compiler_params=_cparams(2), name="ssd",
    )(proj, proj, proj, conv_w, conv_b.reshape(1, SSD_CONV_W), alog_row, dtb_row, d_row,
      ng.reshape(1, SSD_W), s0.reshape(nb, n_pairs, 2 * SSD_P, SSD_N), c0)
    return o, s_new.reshape(nb, SSD_H, SSD_P, SSD_N), c_new


def _s5_prep_kernel(are_ref, aim_ref, ldt_ref, bre_ref, bim_ref, lbr_ref, lbi_ref, bbr_ref, bbi_ref):
    a_re, a_im = are_ref[...], aim_ref[...]
    dt = jnp.exp(ldt_ref[...])
    mag = jnp.exp(a_re * dt)
    lb_re, lb_im = mag * jnp.cos(a_im * dt), mag * jnp.sin(a_im * dt)
    nr, ni = lb_re - 1.0, lb_im
    den = a_re * a_re + a_im * a_im
    f_re = (nr * a_re + ni * a_im) / den
    f_im = (ni * a_re - nr * a_im) / den
    b_re, b_im = bre_ref[...], bim_ref[...]
    lbr_ref[...] = lb_re
    lbi_ref[...] = lb_im
    bbr_ref[...] = f_re * b_re - f_im * b_im
    bbi_ref[...] = f_re * b_im + f_im * b_re


def _s5_prep(a_re, a_im, log_dt, b_re, b_im):
    g3 = (S5_G, 1, S5_P)
    b3 = (S5_G, S5_GS, S5_P)
    return pl.pallas_call(
        _s5_prep_kernel,
        out_shape=[jax.ShapeDtypeStruct(g3, F32), jax.ShapeDtypeStruct(g3, F32),
                   jax.ShapeDtypeStruct(b3, F32), jax.ShapeDtypeStruct(b3, F32)],
        name="s5_prep",
    )(a_re.reshape(g3), a_im.reshape(g3), log_dt.reshape(S5_G, 1, 1),
      jnp.swapaxes(b_re, 1, 2), jnp.swapaxes(b_im, 1, 2))


def _block_diag(blocks):
    g, r, c = blocks.shape
    per = 8
    b = blocks.reshape(g // per, per, r, 1, c) * jnp.eye(per, dtype=blocks.dtype).reshape(1, per, 1, per, 1)
    return b.reshape(g // per, per * r, per * c)


def _s5_kernel(u_ref, wre_ref, wim_ref, cre_ref, cim_ref, lbr_ref, lbi_ref, d_ref, x0r_ref, x0i_ref,
               y_ref, xfr_ref, xfi_ref, xr_scr, xi_scr, sr_scr, si_scr, *, rows_per_step, steps):
    c = pl.program_id(2)

    @pl.when(c == 0)
    def _():
        xr_scr[...] = x0r_ref[0]
        xi_scr[...] = x0i_ref[0]

    u = u_ref[0]
    sr_scr[...] = _dot(u, wre_ref[0])
    si_scr[...] = _dot(u, wim_ref[0])
    l_re, l_im = lbr_ref[...], lbi_ref[...]

    def step(t, carry):
        rows = pl.ds(pl.multiple_of(t * rows_per_step, rows_per_step), rows_per_step)
        xr, xi = xr_scr[...], xi_scr[...]
        nr = l_re * xr - l_im * xi + sr_scr[rows, :]
        ni = l_re * xi + l_im * xr + si_scr[rows, :]
        xr_scr[...] = nr
        xi_scr[...] = ni
        sr_scr[rows, :] = nr
        si_scr[rows, :] = ni
        return carry

    lax.fori_loop(0, steps, step, 0)
    y_ref[0] = _dot(sr_scr[...], cre_ref[0]) - _dot(si_scr[...], cim_ref[0]) + d_ref[...] * u

    @pl.when(c == pl.num_programs(2) - 1)
    def _():
        xfr_ref[0] = xr_scr[...]
        xfi_ref[0] = xi_scr[...]


def _s5(proj, w_re, w_im, c_re, c_im, lb_re, lb_im, d, x0_re, x0_im, rows_per_step, steps):
    ng, n_tok, _ = proj.shape
    nj = S5_W // LANE
    sw = S5_STATE // nj
    cr = rows_per_step * steps
    kern = functools.partial(_s5_kernel, rows_per_step=rows_per_step, steps=steps)
    x_spec = pl.BlockSpec((1, rows_per_step, sw), lambda g, j, c: (g, 0, j))
    return pl.pallas_call(
        kern,
        grid=(ng, nj, n_tok // cr),
        in_specs=[pl.BlockSpec((1, cr, LANE), lambda g, j, c: (g, c, CD_U // LANE + j)),
                  pl.BlockSpec((1, LANE, sw), lambda g, j, c: (j, 0, 0)),
                  pl.BlockSpec((1, LANE, sw), lambda g, j, c: (j, 0, 0)),
                  pl.BlockSpec((1, sw, LANE), lambda g, j, c: (j, 0, 0)),
                  pl.BlockSpec((1, sw, LANE), lambda g, j, c: (j, 0, 0)),
                  pl.BlockSpec((1, sw), lambda g, j, c: (0, j)),
                  pl.BlockSpec((1, sw), lambda g, j, c: (0, j)),
                  pl.BlockSpec((1, LANE), lambda g, j, c: (0, j)),
                  x_spec, x_spec],
        out_specs=[pl.BlockSpec((1, cr, LANE), lambda g, j, c: (g, c, j)), x_spec, x_spec],
        out_shape=[jax.ShapeDtypeStruct((ng, n_tok, S5_W), F32),
                   jax.ShapeDtypeStruct((ng, rows_per_step, S5_STATE), F32),
                   jax.ShapeDtypeStruct((ng, rows_per_step, S5_STATE), F32)],
        scratch_shapes=[pltpu.VMEM((rows_per_step, sw), F32), pltpu.VMEM((rows_per_step, sw), F32),
                        pltpu.VMEM((cr, sw), F32), pltpu.VMEM((cr, sw), F32)],
        compiler_params=_cparams(3), name="s5_scan",
    )(proj, w_re, w_im, c_re, c_im, lb_re, lb_im, d.reshape(1, S5_W), x0_re, x0_im)


def _s5_pow_kernel(lbr_ref, lbi_ref, pr_ref, pi_ref, *, n_rows):
    l_re, l_im = lbr_ref[...], lbi_ref[...]
    row = lax.broadcasted_iota(jnp.int32, (SUBLANE, 1), 0)
    p_re, p_im = l_re, l_im
    b_re = jnp.broadcast_to(l_re, (SUBLANE, l_re.shape[1]))
    b_im = jnp.broadcast_to(l_im, (SUBLANE, l_re.shape[1]))
    for r in range(1, SUBLANE):
        p_re, p_im = p_re * l_re - p_im * l_im, p_re * l_im + p_im * l_re
        b_re = jnp.where(row >= r, p_re, b_re)
        b_im = jnp.where(row >= r, p_im, b_im)
    q_re, q_im = jnp.ones_like(l_re), jnp.zeros_like(l_re)
    for a in range(n_rows // SUBLANE):
        rows = slice(a * SUBLANE, (a + 1) * SUBLANE)
        pr_ref[rows, :] = b_re * q_re - b_im * q_im
        pi_ref[rows, :] = b_re * q_im + b_im * q_re
        q_re, q_im = q_re * p_re - q_im * p_im, q_re * p_im + q_im * p_re


def _s5_pow_table(lb_re, lb_im, n_rows):
    shape = jax.ShapeDtypeStruct((n_rows, S5_STATE), F32)
    return pl.pallas_call(functools.partial(_s5_pow_kernel, n_rows=n_rows), out_shape=[shape, shape],
                          name="s5_pow")(lb_re, lb_im)


def _s5_seg_kernel(u_ref, wre_ref, wim_ref, cre_ref, cim_ref, pr_ref, pi_ref, d_ref, x0r_ref, x0i_ref,
                   y_ref, xfr_ref, xfi_ref, sr_scr, si_scr, *, seg_len):
    n_seg = SUBLANE
    n_lane_blk = sr_scr.shape[0]
    sw = n_lane_blk * LANE
    lane_blks = [slice(c * LANE, (c + 1) * LANE) for c in range(n_lane_blk)]

    def put(scr, rows, val):
        for c, ls in enumerate(lane_blks):
            scr[c, rows, :] = val[:, ls]

    def get(scr, rows):
        return jnp.concatenate([scr[c, rows, :] for c in range(n_lane_blk)], axis=1)

    for s in range(n_seg):
        us = u_ref[0, s * seg_len:(s + 1) * seg_len, :]
        put(sr_scr, pl.ds(s, seg_len, stride=n_seg), _dot(us, wre_ref[0]))
        put(si_scr, pl.ds(s, seg_len, stride=n_seg), _dot(us, wim_ref[0]))
    l_re, l_im = pr_ref[0:1, :], pi_ref[0:1, :]

    def step(t, carry):
        xr, xi = carry
        rows = pl.ds(pl.multiple_of(t * n_seg, n_seg), n_seg)
        nr = l_re * xr - l_im * xi + get(sr_scr, rows)
        ni = l_re * xi + l_im * xr + get(si_scr, rows)
        put(sr_scr, rows, nr)
        put(si_scr, rows, ni)
        return nr, ni

    zero = jnp.zeros((n_seg, sw), F32)
    end_re, end_im = lax.fori_loop(0, seg_len, step, (zero, zero), unroll=4)
    ln_re, ln_im = pr_ref[seg_len - 1:seg_len, :], pi_ref[seg_len - 1:seg_len, :]
    p_re, p_im = pr_ref[...], pi_ref[...]
    x_re, x_im = x0r_ref[0], x0i_ref[0]
    for s in range(n_seg):
        loc_re = get(sr_scr, pl.ds(s, seg_len, stride=n_seg))
        loc_im = get(si_scr, pl.ds(s, seg_len, stride=n_seg))
        t_re = loc_re + p_re * x_re - p_im * x_im
        t_im = loc_im + p_re * x_im + p_im * x_re
        us = u_ref[0, s * seg_len:(s + 1) * seg_len, :]
        y_ref[0, s * seg_len:(s + 1) * seg_len, :] = (_dot(t_re, cre_ref[0]) - _dot(t_im, cim_ref[0])
                                                      + d_ref[...] * us)
        e_re, e_im = end_re[s:s + 1, :], end_im[s:s + 1, :]
        x_re, x_im = e_re + ln_re * x_re - ln_im * x_im, e_im + ln_re * x_im + ln_im * x_re
    xfr_ref[0] = x_re
    xfi_ref[0] = x_im


def _s5_seg(proj, w_re, w_im, c_re, c_im, pow_re, pow_im, d, x0_re, x0_im):
    nb, seq, _ = proj.shape
    nj = S5_W // LANE
    sw = S5_STATE // nj
    seg_len = seq // SUBLANE
    x_spec = pl.BlockSpec((1, 1, sw), lambda b, j: (b, 0, j))
    return pl.pallas_call(
        functools.partial(_s5_seg_kernel, seg_len=seg_len),
        grid=(nb, nj),
        in_specs=[pl.BlockSpec((1, seq, LANE), lambda b, j: (b, 0, CD_U // LANE + j)),
                  pl.BlockSpec((1, LANE, sw), lambda b, j: (j, 0, 0)),
                  pl.BlockSpec((1, LANE, sw), lambda b, j: (j, 0, 0)),
                  pl.BlockSpec((1, sw, LANE), lambda b, j: (j, 0, 0)),
                  pl.BlockSpec((1, sw, LANE), lambda b, j: (j, 0, 0)),
                  pl.BlockSpec((seg_len, sw), lambda b, j: (0, j)),
                  pl.BlockSpec((seg_len, sw), lambda b, j: (0, j)),
                  pl.BlockSpec((1, LANE), lambda b, j: (0, j)),
                  x_spec, x_spec],
        out_specs=[pl.BlockSpec((1, seq, LANE), lambda b, j: (b, 0, j)), x_spec, x_spec],
        out_shape=[jax.ShapeDtypeStruct((nb, seq, S5_W), F32),
                   jax.ShapeDtypeStruct((nb, 1, S5_STATE), F32),
                   jax.ShapeDtypeStruct((nb, 1, S5_STATE), F32)],
        scratch_shapes=[pltpu.VMEM((sw // LANE, seq, LANE), F32), pltpu.VMEM((sw // LANE, seq, LANE), F32)],
        compiler_params=_cparams(2), name="s5_seg",
    )(proj, w_re, w_im, c_re, c_im, pow_re, pow_im, d.reshape(1, S5_W), x0_re, x0_im)


def _lane_row(vec, lane0):
    return jnp.zeros((1, LANE), F32).at[0, lane0:lane0 + vec.shape[0]].set(vec.astype(F32))


def _realign_kernel(w_ref, o_ref, *, segments):
    o_ref[...] = jnp.zeros_like(o_ref)
    for src, rows, dst in segments:
        o_ref[dst:dst + rows, :] = w_ref[src:src + rows, :].astype(BF16)


def _realign_weights(w, segments, n_out):
    wt = jnp.swapaxes(w, 1, 2)[0]
    n_src, k = wt.shape
    assert all(v % (2 * SUBLANE) == 0 for seg in segments for v in seg)
    cols = REALIGN_COLS
    return pl.pallas_call(
        functools.partial(_realign_kernel, segments=segments),
        grid=(k // cols,),
        in_specs=[pl.BlockSpec((n_src, cols), lambda i: (0, i))],
        out_specs=pl.BlockSpec((n_out, cols), lambda i: (0, i)),
        out_shape=jax.ShapeDtypeStruct((n_out, k), BF16),
        compiler_params=_cparams(1), name="realign_weights",
    )(wt)


def _prep_params(p):
    q = {}
    o_lr = 2 * GLA_QK + GLA_V
    o_r = o_lr + GLA_LR
    o_qkv = o_r + GLA_V
    o_a = o_qkv + GDN_CONV_W
    o_g = o_a + 2 * GDN_H
    q['w_in_ab'] = _realign_weights(
        p['w_in_ab'], [(0, o_lr, AB_Q), (o_r, GLA_V, AB_R), (o_qkv, GDN_CONV_W, AB_QKV), (o_g, GDN_V, AB_G),
                       (o_lr, GLA_LR, AB_SMALL + AB_LR_LANE), (o_a, 2 * GDN_H, AB_SMALL + AB_A_LANE)], AB_N)
    q['gla_w2'] = jnp.zeros((LANE, GLA_QK), F32).at[:GLA_LR].set(p['gla_w2'][0])
    q['gdn_alog'] = _lane_row(p['gdn_A_log'][0], AB_A_LANE)
    q['gdn_dtb'] = _lane_row(p['gdn_dt_bias'][0], AB_A_LANE)
    q['w_out_ab'] = p['w_out_ab'].astype(BF16)
    o_xbc = SSD_W
    o_dt = o_xbc + SSD_CONV_W
    o_u = o_dt + SSD_H
    q['w_in_cd'] = _realign_weights(
        p['w_in_cd'], [(o_xbc, SSD_CONV_W, CD_XBC), (0, SSD_W, CD_Z), (o_u, S5_W, CD_U), (o_dt, SSD_H, CD_SMALL)],
        CD_N)
    q['ssd_alog'] = _lane_row(p['ssd_A_log'][0], 0)
    q['ssd_dtb'] = _lane_row(p['ssd_dt_bias'][0], 0)
    q['ssd_d_row'] = jnp.repeat(p['ssd_D'][0].astype(F32), SSD_P).reshape(1, SSD_W)
    q['w_out_cd'] = p['w_out_cd'].astype(BF16)
    lb_re, lb_im, bb_re, bb_im = _s5_prep(p['s5_A_re'][0], p['s5_A_im'][0], p['s5_log_dt'][0],
                                          p['s5_B_re'][0], p['s5_B_im'][0])
    q['s5_lb_re'], q['s5_lb_im'] = lb_re.reshape(1, S5_STATE), lb_im.reshape(1, S5_STATE)
    q['s5_w_re'], q['s5_w_im'] = _block_diag(bb_re), _block_diag(bb_im)
    q['s5_c_re'] = _block_diag(jnp.swapaxes(p['s5_C_re'][0], 1, 2))
    q['s5_c_im'] = _block_diag(jnp.swapaxes(p['s5_C_im'][0], 1, 2))
    q['s5_glu_w'] = p['s5_glu_w'][0].astype(BF16)
    q['w_ffn_down'] = p['w_ffn_down'].astype(BF16)
    return q


def _trunk(x, mods, grp, seq_shape, state, p, q):
    nb, seq_len, valid = seq_shape
    s_gla, s_gdn, s_gdnc, s_ssd, s_ssdc, s_re, s_im, s_ffn = state
    prompt = not grp.per_token_mod

    def to_seq(t):
        if prompt:
            return t.reshape(nb, seq_len, t.shape[-1])
        t = jnp.swapaxes(t.reshape(valid, nb, t.shape[-1]), 0, 1)
        return jnp.pad(t, ((0, 0), (0, seq_len - valid), (0, 0)))

    def from_seq(t):
        if prompt:
            return t.reshape(nb * seq_len, t.shape[-1])
        return jnp.swapaxes(t[:, :valid], 0, 1).reshape(valid * nb, t.shape[-1])

    blk = MIX_BLOCK if prompt else seq_len
    chunks = (GLA_CHUNK, GDN_CHUNK, SSD_CHUNK) if prompt else (seq_len,) * 3
    new = {}

    grp_norm = grp.retiled(NORM_TILE)
    h = _norm_mod(x, p['g_mix'][0], mods[0], grp_norm, 1, 0)
    proj = to_seq(_matmul(h, q['w_in_ab'], grp, AB_N // 3))
    o_a, new['gla'] = _gla(proj, q['gla_w2'], p['gla_b2'][0], p['gla_norm_g'][0], s_gla, chunks[0],
                            GLA_GROUP if prompt else 1, blk, valid, 1 if prompt else SAMPLE_SEQS)
    o_b, new['gdn'], new['gdnc'] = _gdn(proj, p['gdn_conv_w'][0], p['gdn_conv_b'][0], q['gdn_alog'], q['gdn_dtb'],
                                        p['gdn_norm_g'][0], s_gdn, s_gdnc, chunks[1], blk, valid,
                                        1 if prompt else SAMPLE_SEQS)
    x, h = _mm_residual([(from_seq(o_a), q['w_out_ab'], 0, 0), (from_seq(o_b), q['w_out_ab'], 0, 1)],
                        x, mods[0], grp, 2, D_MODEL, norm=(p['g_ffn'][0], 4, 3))
    act, new['ffn0'] = _ffn_up(h, p['w_ffn_up'], p['ffn_conv_w'], p['ffn_conv_b'], 0, s_ffn[0],
                               grp.retiled(FFN_TILE))
    x = _mm_residual([(act, q['w_ffn_down'], 0, 0)], x, mods[0], grp, 5, DOWN_TN)

    h = _norm_mod(x, p['g_mix'][1], mods[1], grp_norm, 1, 0)
    proj2 = _matmul(h, q['w_in_cd'], grp, CD_N // 2)
    proj = to_seq(proj2)
    o_c, new['ssd'], new['ssdc'] = _ssd(proj, p['ssd_conv_w'][0], p['ssd_conv_b'][0], q['ssd_alog'], q['ssd_dtb'],
                                        q['ssd_d_row'], p['ssd_norm_g'][0], s_ssd, s_ssdc, chunks[2], blk, valid,
                                        1 if prompt else SAMPLE_SEQS)
    if prompt:
        pow_re, pow_im = _s5_pow_table(q['s5_lb_re'], q['s5_lb_im'], seq_len // SUBLANE)
        yd, new['re'], new['im'] = _s5_seg(proj, q['s5_w_re'], q['s5_w_im'], q['s5_c_re'], q['s5_c_im'],
                                           pow_re, pow_im, p['s5_D'][0], s_re, s_im)
    else:
        yd, new['re'], new['im'] = _s5(proj2.reshape(1, grp.n_tok, CD_N), q['s5_w_re'], q['s5_w_im'],
                                       q['s5_c_re'], q['s5_c_im'], q['s5_lb_re'], q['s5_lb_im'], p['s5_D'][0],
                                       s_re, s_im, nb, valid)
    o_d = _s5_glu(yd.reshape(grp.n_tok, S5_W), q['s5_glu_w'], p['s5_glu_b'][0], grp)
    x, h = _mm_residual([(from_seq(o_c), q['w_out_cd'], 0, 0), (o_d, q['w_out_cd'], 0, 1)], x, mods[1], grp, 2,
                        D_MODEL, norm=(p['g_ffn'][1], 4, 3))
    act, new['ffn1'] = _ffn_up(h, p['w_ffn_up'], p['ffn_conv_w'], p['ffn_conv_b'], 1, s_ffn[1],
                               grp.retiled(FFN_TILE))
    x = _mm_residual([(act, q['w_ffn_down'], 1, 0)], x, mods[1], grp, 5, DOWN_TN)
    return _final_rms(x, p['g_final'], grp_norm), new


def kernel(x_prompt, x_sample, c_prompt, c_sample, state_gla, state_gdn, state_gdn_conv, state_ssd, state_ssd_conv, state_s5_re, state_s5_im, state_ffn_conv, w_ada, b_ada, g_mix, g_ffn, w_in_ab, gla_w2, gla_b2, gla_norm_g, gdn_conv_w, gdn_conv_b, gdn_A_log, gdn_dt_bias, gdn_norm_g, w_out_ab, w_in_cd, ssd_conv_w, ssd_conv_b, ssd_A_log, ssd_dt_bias, ssd_D, ssd_norm_g, s5_A_re, s5_A_im, s5_B_re, s5_B_im, s5_C_re, s5_C_im, s5_D, s5_log_dt, s5_glu_w, s5_glu_b, w_out_cd, w_ffn_up, ffn_conv_w, ffn_conv_b, w_ffn_down, g_final):
    p = dict(g_mix=g_mix, g_ffn=g_ffn, w_in_ab=w_in_ab, gla_w2=gla_w2, gla_b2=gla_b2, gla_norm_g=gla_norm_g,
             gdn_conv_w=gdn_conv_w, gdn_conv_b=gdn_conv_b, gdn_A_log=gdn_A_log, gdn_dt_bias=gdn_dt_bias,
             gdn_norm_g=gdn_norm_g, w_out_ab=w_out_ab, w_in_cd=w_in_cd, ssd_conv_w=ssd_conv_w,
             ssd_conv_b=ssd_conv_b, ssd_A_log=ssd_A_log, ssd_dt_bias=ssd_dt_bias, ssd_D=ssd_D,
             ssd_norm_g=ssd_norm_g, s5_A_re=s5_A_re, s5_A_im=s5_A_im, s5_B_re=s5_B_re, s5_B_im=s5_B_im,
             s5_C_re=s5_C_re, s5_C_im=s5_C_im, s5_D=s5_D, s5_log_dt=s5_log_dt, s5_glu_w=s5_glu_w,
             s5_glu_b=s5_glu_b, w_out_cd=w_out_cd, w_ffn_up=w_ffn_up, ffn_conv_w=ffn_conv_w,
             ffn_conv_b=ffn_conv_b, w_ffn_down=w_ffn_down, g_final=g_final)
    bp, lp, d = x_prompt.shape
    bs, ls, _ = x_sample.shape
    q = _prep_params(p)

    bp_pad = -(-bp // SUBLANE) * SUBLANE
    c_all = jnp.concatenate([c_prompt, jnp.zeros((bp_pad - bp, d), F32), c_sample], axis=0)
    mod = _ada_mod(c_all, w_ada, b_ada)
    depth = w_ada.shape[0]
    mods_p = [mod[l, :bp].reshape(bp, 1, 6 * d) for l in range(depth)]
    mods_s = [jnp.tile(mod[l, bp_pad:], (ls, 1)).reshape(1, ls * bs, 6 * d) for l in range(depth)]

    tile_p = 512
    grp_p = _Group(bp * lp, tile_p, False, lp // tile_p, 1)
    zeros = lambda *shape: jnp.zeros(shape, F32)
    state_p = (zeros(bp, GLA_H, GLA_DK, GLA_DV), zeros(bp, GDN_H, GDN_DK, GDN_DV),
               zeros(bp, CONV_K - 1, GDN_CONV_W), zeros(bp, SSD_H, SSD_P, SSD_N),
               zeros(bp, CONV_K - 1, SSD_CONV_W), zeros(bp, 1, S5_STATE), zeros(bp, 1, S5_STATE),
               zeros(depth, bp, FFN_K - 1, 2 * D_FF))
    y_p, new_p = _trunk(x_prompt.reshape(bp * lp, d), mods_p, grp_p, (bp, lp, lp), state_p, p, q)

    grp_s = _Group(bs * ls, bs * ls, True, 1, bs)
    ffn_hist_s = jnp.swapaxes(state_ffn_conv, 1, 2).reshape(depth, 1, (FFN_K - 1) * bs, 2 * D_FF)
    state_s = (state_gla[0], state_gdn[0], state_gdn_conv[0], state_ssd[0], state_ssd_conv[0],
               state_s5_re.reshape(1, bs, S5_STATE), state_s5_im.reshape(1, bs, S5_STATE), ffn_hist_s)
    x_s = jnp.swapaxes(x_sample, 0, 1).reshape(ls * bs, d)
    y_s, new_s = _trunk(x_s, mods_s, grp_s, (bs, SAMPLE_PAD, ls), state_s, p, q)
    y_s = jnp.swapaxes(y_s.reshape(ls, bs, d), 0, 1)

    ffn_p = jnp.stack([new_p['ffn0'], new_p['ffn1']])
    ffn_s = jnp.stack([jnp.swapaxes(new_s[k].reshape(FFN_K - 1, bs, 2 * D_FF), 0, 1) for k in ('ffn0', 'ffn1')])
    s5_shape = lambda t, nb: t.reshape(1, nb, S5_G, S5_P)
    return (y_p.reshape(bp, lp, d), y_s,
            new_p['gla'][None], new_s['gla'][None], new_p['gdn'][None], new_s['gdn'][None],
            new_p['gdnc'][None], new_s['gdnc'][None], new_p['ssd'][None], new_s['ssd'][None],
            new_p['ssdc'][None], new_s['ssdc'][None],
            s5_shape(new_p['re'], bp), s5_shape(new_s['re'], bs), s5_shape(new_p['im'], bp), s5_shape(new_s['im'], bs),
            ffn_p, ffn_s)
```

```python
import functools
import math

import jax
import jax.numpy as jnp
from jax import lax
from jax.experimental import pallas as pl
from jax.experimental.pallas import tpu as pltpu

F32 = jnp.float32
BF16 = jnp.bfloat16
HIGHEST = lax.Precision.HIGHEST
EPS = 1e-6

D_MODEL = 2048
GLA_H, GLA_DK, GLA_DV, GLA_LR = 4, 128, 256, 16
GLA_GATE_NORM = 16.0
GLA_QK, GLA_V = GLA_H * GLA_DK, GLA_H * GLA_DV
GDN_H, GDN_DK, GDN_DV = 8, 128, 128
GDN_QK, GDN_V = GDN_H * GDN_DK, GDN_H * GDN_DV
CONV_K = 4
GDN_CONV_W = 2 * GDN_QK + GDN_V
SSD_P, SSD_H, SSD_G, SSD_N = 64, 16, 2, 128
SSD_W = SSD_H * SSD_P
SSD_CONV_W = SSD_W + 2 * SSD_G * SSD_N
S5_W, S5_GS, S5_G, S5_P = 1024, 16, 64, 64
S5_STATE = S5_G * S5_P
D_FF = 5632
FFN_K = 3

LANE = 128
SUBLANE = 8
VMEM_LIMIT = 48 * 1024 * 1024

AB_Q, AB_K, AB_V, AB_R, AB_QKV, AB_G, AB_SMALL, AB_N = 0, 512, 1024, 2048, 3072, 6144, 7168, 7296
AB_LR_LANE, AB_A_LANE, AB_B_LANE = 0, 16, 24
CD_XBC, CD_SMALL, CD_Z, CD_U, CD_N = 0, 1536, 2048, 3072, 4096

MIX_BLOCK = 256
GLA_CHUNK, GDN_CHUNK, SSD_CHUNK = 16, 128, 128
GLA_GROUP = 16
FFN_TILE = 1024
NORM_TILE = 1024
ADA_TN = 1024
FFN_TN = 512
DOWN_TN = 1024
REALIGN_COLS = 512
FFN_SUB = 256
INV_BLOCK = 16
SAMPLE_PAD = 8
SAMPLE_SEQS = 8


def _cparams(n_axes):
    return pltpu.CompilerParams(dimension_semantics=("arbitrary",) * n_axes, vmem_limit_bytes=VMEM_LIMIT)


def _sigmoid(x):
    return 1.0 / (1.0 + jnp.exp(-x))


def _silu(x):
    return x * _sigmoid(x)


def _softplus(x):
    return jnp.maximum(x, 0.0) + jnp.log(1.0 + jnp.exp(-jnp.abs(x)))


def _gelu_tanh(x):
    return 0.5 * x * (1.0 + jnp.tanh(math.sqrt(2.0 / math.pi) * (x + 0.044715 * (x * x * x))))


def _dot(a, b, precision=None):
    return jnp.dot(a, b, precision=precision, preferred_element_type=F32)


def _dot_nt(a, b, precision=None):
    return lax.dot_general(a, b, (((1,), (1,)), ((), ())), precision=precision, preferred_element_type=F32)


def _dot_tn(a, b, precision=None):
    return lax.dot_general(a, b, (((0,), (0,)), ((), ())), precision=precision, preferred_element_type=F32)


def _head_rms(o, g):
    return o * lax.rsqrt(jnp.mean(o * o, -1, keepdims=True) + EPS) * g


class _Group:
    def __init__(self, n_tok, tile, per_token_mod, tiles_per_seq, conv_shift):
        self.n_tok = n_tok
        self.tile = tile
        self.n_tiles = n_tok // tile
        self.per_token_mod = per_token_mod
        self.tiles_per_seq = tiles_per_seq
        self.conv_shift = conv_shift

    def retiled(self, tile):
        seq_rows = self.tile * self.tiles_per_seq
        if seq_rows % tile or self.n_tok % tile:
            return self
        return _Group(self.n_tok, tile, self.per_token_mod, seq_rows // tile, self.conv_shift)

    def mod_spec(self, width, col_block, m_axis):
        if self.per_token_mod:
            return pl.BlockSpec((1, self.tile, width), lambda *g: (0, g[m_axis], col_block(*g)))
        tps = self.tiles_per_seq
        return pl.BlockSpec((1, 1, width), lambda *g: (g[m_axis] // tps, 0, col_block(*g)))


def _ada_kernel(c_ref, w_ref, b_ref, o_ref):
    cs = _silu(c_ref[...]).astype(BF16)
    o_ref[0] = _dot(cs, w_ref[0].astype(BF16)) + b_ref[0]


def _ada_mod(c, w_ada, b_ada):
    depth, d, n = w_ada.shape
    rows = c.shape[0]
    tn = ADA_TN
    return pl.pallas_call(
        _ada_kernel,
        grid=(depth, n // tn),
        in_specs=[pl.BlockSpec((rows, d), lambda l, j: (0, 0)),
                  pl.BlockSpec((1, d, tn), lambda l, j: (l, 0, j)),
                  pl.BlockSpec((1, 1, tn), lambda l, j: (l, 0, j))],
        out_specs=pl.BlockSpec((1, rows, tn), lambda l, j: (l, 0, j)),
        out_shape=jax.ShapeDtypeStruct((depth, rows, n), F32),
        compiler_params=_cparams(2), name="ada_mod",
    )(c, w_ada, b_ada.reshape(depth, 1, n))


def _norm_mod_kernel(x_ref, g_ref, sc_ref, sh_ref, o_ref):
    x = x_ref[...]
    y = x * lax.rsqrt(jnp.mean(x * x, -1, keepdims=True) + EPS) * g_ref[...]
    o_ref[...] = (y * (1.0 + sc_ref[0]) + sh_ref[0]).astype(BF16)


def _norm_mod(x, g, mod, grp, sc_blk, sh_blk):
    d = x.shape[1]
    return pl.pallas_call(
        _norm_mod_kernel,
        grid=(grp.n_tiles,),
        in_specs=[pl.BlockSpec((grp.tile, d), lambda i: (i, 0)),
                  pl.BlockSpec((1, d), lambda i: (0, 0)),
                  grp.mod_spec(d, lambda i: sc_blk, 0),
                  grp.mod_spec(d, lambda i: sh_blk, 0)],
        out_specs=pl.BlockSpec((grp.tile, d), lambda i: (i, 0)),
        out_shape=jax.ShapeDtypeStruct(x.shape, BF16),
        compiler_params=_cparams(1), name="norm_mod",
    )(x, g.reshape(1, d), mod, mod)


def _rms_kernel(x_ref, g_ref, o_ref):
    x = x_ref[...]
    o_ref[...] = x * lax.rsqrt(jnp.mean(x * x, -1, keepdims=True) + EPS) * g_ref[...]


def _final_rms(x, g, grp):
    d = x.shape[1]
    return pl.pallas_call(
        _rms_kernel,
        grid=(grp.n_tiles,),
        in_specs=[pl.BlockSpec((grp.tile, d), lambda i: (i, 0)), pl.BlockSpec((1, d), lambda i: (0, 0))],
        out_specs=pl.BlockSpec((grp.tile, d), lambda i: (i, 0)),
        out_shape=jax.ShapeDtypeStruct(x.shape, F32),
        compiler_params=_cparams(1), name="final_rms",
    )(x, g.reshape(1, d))


def _matmul_kernel(a_ref, wt_ref, o_ref):
    o_ref[...] = _dot_nt(a_ref[...], wt_ref[...])


def _matmul(a, wt, grp, tn):
    n, k = wt.shape
    return pl.pallas_call(
        _matmul_kernel,
        grid=(n // tn, grp.n_tiles),
        in_specs=[pl.BlockSpec((grp.tile, k), lambda j, i: (i, 0)),
                  pl.BlockSpec((tn, k), lambda j, i: (j, 0))],
        out_specs=pl.BlockSpec((grp.tile, tn), lambda j, i: (i, j)),
        out_shape=jax.ShapeDtypeStruct((a.shape[0], n), F32),
        compiler_params=_cparams(2), name="matmul",
    )(a, wt)


def _mm_res_kernel(*refs, n_pairs, with_norm):
    x_ref, gate_ref = refs[2 * n_pairs:2 * n_pairs + 2]
    y = _dot(refs[0][...].astype(BF16), refs[1][0])
    for p in range(1, n_pairs):
        y = y + _dot(refs[2 * p][...].astype(BF16), refs[2 * p + 1][0])
    x_new = x_ref[...] + gate_ref[0] * y
    if not with_norm:
        o_ref, = refs[2 * n_pairs + 2:]
        o_ref[...] = x_new
        return
    g_ref, sc_ref, sh_ref, o_ref, h_ref = refs[2 * n_pairs + 2:]
    o_ref[...] = x_new
    normed = x_new * lax.rsqrt(jnp.mean(x_new * x_new, -1, keepdims=True) + EPS) * g_ref[...]
    h_ref[...] = (normed * (1.0 + sc_ref[0]) + sh_ref[0]).astype(BF16)


def _mm_residual(pairs, x, mod, grp, gate_blk, tn, norm=None):
    n = x.shape[1]
    if norm is not None:
        assert tn == n
        return _mm_residual_norm(pairs, x, mod, grp, gate_blk, norm)
    in_specs, args = [], []
    for a, w, layer, row_blk in pairs:
        k = a.shape[1]
        in_specs += [pl.BlockSpec((grp.tile, k), lambda j, i: (i, 0)),
                     pl.BlockSpec((1, k, tn), lambda j, i, layer=layer, row_blk=row_blk: (layer, row_blk, j))]
        args += [a, w]
    in_specs += [pl.BlockSpec((grp.tile, tn), lambda j, i: (i, j)),
                 grp.mod_spec(tn, lambda j, i: gate_blk * (n // tn) + j, 1)]
    return pl.pallas_call(
        functools.partial(_mm_res_kernel, n_pairs=len(pairs), with_norm=False),
        grid=(n // tn, grp.n_tiles),
        in_specs=in_specs,
        out_specs=pl.BlockSpec((grp.tile, tn), lambda j, i: (i, j)),
        out_shape=jax.ShapeDtypeStruct(x.shape, F32),
        compiler_params=_cparams(2), name="mm_residual",
    )(*args, x, mod)


def _mm_residual_norm(pairs, x, mod, grp, gate_blk, norm):
    n = x.shape[1]
    gain, sc_blk, sh_blk = norm
    in_specs, args = [], []
    for a, w, layer, row_blk in pairs:
        k = a.shape[1]
        in_specs += [pl.BlockSpec((grp.tile, k), lambda j, i: (i, 0)),
                     pl.BlockSpec((1, k, n), lambda j, i, layer=layer, row_blk=row_blk: (layer, row_blk, 0))]
        args += [a, w]
    row_spec = pl.BlockSpec((grp.tile, n), lambda j, i: (i, 0))
    in_specs += [row_spec, grp.mod_spec(n, lambda j, i: gate_blk, 1), pl.BlockSpec((1, n), lambda j, i: (0, 0)),
                 grp.mod_spec(n, lambda j, i: sc_blk, 1), grp.mod_spec(n, lambda j, i: sh_blk, 1)]
    return pl.pallas_call(
        functools.partial(_mm_res_kernel, n_pairs=len(pairs), with_norm=True),
        grid=(1, grp.n_tiles),
        in_specs=in_specs,
        out_specs=[row_spec, row_spec],
        out_shape=[jax.ShapeDtypeStruct(x.shape, F32), jax.ShapeDtypeStruct(x.shape, BF16)],
        compiler_params=_cparams(2), name="mm_residual_norm",
    )(*args, x, mod, gain.reshape(1, n), mod, mod)


def _ffn_up_kernel(h_ref, wa_ref, wg_ref, cwa_ref, cwg_ref, cba_ref, cbg_ref, ha_ref, hg_ref,
                   act_ref, sta_ref, stg_ref, scr_a, scr_g, wba_scr, wbg_scr, *, shift, tile, sub, tiles_per_seq):
    i = pl.program_id(1)
    hist = (FFN_K - 1) * shift
    base = -(-hist // SUBLANE) * SUBLANE

    @pl.when(i == 0)
    def _():
        wba_scr[...] = wa_ref[0].astype(BF16)
        wbg_scr[...] = wg_ref[0].astype(BF16)

    @pl.when(i % tiles_per_seq == 0)
    def _():
        scr_a[base - hist:base, :] = ha_ref[0]
        scr_g[base - hist:base, :] = hg_ref[0]

    def conv(scr, cw_ref, cb_ref, r0):
        y = cb_ref[0]
        for j in range(FFN_K):
            lo = base + r0 - (FFN_K - 1 - j) * shift
            y = y + scr[lo:lo + sub, :] * cw_ref[0, j:j + 1, :]
        return y

    def project(r0):
        h = h_ref[r0:r0 + sub, :]
        scr_a[base + r0:base + r0 + sub, :] = _dot(h, wba_scr[...])
        scr_g[base + r0:base + r0 + sub, :] = _dot(h, wbg_scr[...])

    project(0)
    for r0 in range(0, tile, sub):
        if r0 + sub < tile:
            project(r0 + sub)
        a = conv(scr_a, cwa_ref, cba_ref, r0)
        g = conv(scr_g, cwg_ref, cbg_ref, r0)
        act_ref[r0:r0 + sub, :] = (_silu(g) * a).astype(BF16)
    last_a = scr_a[base + tile - hist:base + tile, :]
    last_g = scr_g[base + tile - hist:base + tile, :]
    sta_ref[0] = last_a
    stg_ref[0] = last_g
    scr_a[base - hist:base, :] = last_a
    scr_g[base - hist:base, :] = last_g


def _ffn_up(h, w_up, conv_w, conv_b, layer, hist0, grp):
    d = h.shape[1]
    tn = FFN_TN
    nj = D_FF // tn
    shift = grp.conv_shift
    hist = (FFN_K - 1) * shift
    base = -(-hist // SUBLANE) * SUBLANE
    n_seq = grp.n_tiles // grp.tiles_per_seq
    tps = grp.tiles_per_seq
    cb = conv_b.reshape(conv_b.shape[0], 1, 2 * D_FF)
    kern = functools.partial(_ffn_up_kernel, shift=shift, tile=grp.tile, sub=min(FFN_SUB, grp.tile),
                             tiles_per_seq=tps)
    assert grp.tile % min(FFN_SUB, grp.tile) == 0
    act, st_a, st_g = pl.pallas_call(
        kern,
        grid=(nj, grp.n_tiles),
        in_specs=[pl.BlockSpec((grp.tile, d), lambda j, i: (i, 0)),
                  pl.BlockSpec((1, d, tn), lambda j, i: (layer, 0, j)),
                  pl.BlockSpec((1, d, tn), lambda j, i: (layer, 0, nj + j)),
                  pl.BlockSpec((1, FFN_K, tn), lambda j, i: (layer, 0, j)),
                  pl.BlockSpec((1, FFN_K, tn), lambda j, i: (layer, 0, nj + j)),
                  pl.BlockSpec((1, 1, tn), lambda j, i: (layer, 0, j)),
                  pl.BlockSpec((1, 1, tn), lambda j, i: (layer, 0, nj + j)),
                  pl.BlockSpec((1, hist, tn), lambda j, i: (i // tps, 0, j)),
                  pl.BlockSpec((1, hist, tn), lambda j, i: (i // tps, 0, nj + j))],
        out_specs=[pl.BlockSpec((grp.tile, tn), lambda j, i: (i, j)),
                   pl.BlockSpec((1, hist, tn), lambda j, i: (i // tps, 0, j)),
                   pl.BlockSpec((1, hist, tn), lambda j, i: (i // tps, 0, j))],
        out_shape=[jax.ShapeDtypeStruct((h.shape[0], D_FF), BF16),
                   jax.ShapeDtypeStruct((n_seq, hist, D_FF), F32),
                   jax.ShapeDtypeStruct((n_seq, hist, D_FF), F32)],
        scratch_shapes=[pltpu.VMEM((base + grp.tile, tn), F32), pltpu.VMEM((base + grp.tile, tn), F32),
                        pltpu.VMEM((d, tn), BF16), pltpu.VMEM((d, tn), BF16)],
        compiler_params=_cparams(2), name="ffn_up",
    )(h, w_up, w_up, conv_w, conv_w, cb, cb, hist0, hist0)
    return act, jnp.concatenate([st_a, st_g], axis=-1)


def _glu_kernel(y_ref, w_ref, b_ref, o_ref):
    z5 = _gelu_tanh(y_ref[...])
    o_ref[...] = (z5 * _sigmoid(_dot(z5.astype(BF16), w_ref[...]) + b_ref[...])).astype(o_ref.dtype)


def _s5_glu(yd, w, b, grp):
    n = yd.shape[1]
    return pl.pallas_call(
        _glu_kernel,
        grid=(grp.n_tiles,),
        in_specs=[pl.BlockSpec((grp.tile, n), lambda i: (i, 0)),
                  pl.BlockSpec((n, n), lambda i: (0, 0)),
                  pl.BlockSpec((1, n), lambda i: (0, 0))],
        out_specs=pl.BlockSpec((grp.tile, n), lambda i: (i, 0)),
        out_shape=jax.ShapeDtypeStruct(yd.shape, BF16),
        compiler_params=_cparams(1), name="s5_glu",
    )(yd, w, b.reshape(1, n))


def _round_robin(gens):
    gens = list(gens)
    while gens:
        alive = []
        for gen in gens:
            try:
                next(gen)
                alive.append(gen)
            except StopIteration:
                pass
        gens = alive


def _mixer_out_dtype(block):
    return BF16 if block % (2 * SUBLANE) == 0 else F32


def _causal_conv_chunk(x, cv_scr, cw_ref, cb_ref, chunk, valid):
    base = SUBLANE
    cv_scr[base:base + chunk, :] = x
    y = cb_ref[...]
    for j in range(CONV_K):
        lo = base - (CONV_K - 1) + j
        y = y + cv_scr[lo:lo + chunk, :] * cw_ref[j:j + 1, :]
    last = cv_scr[base + valid - (CONV_K - 1):base + valid, :]
    cv_scr[base - (CONV_K - 1):base, :] = last
    return y


def _gla_kernel(q_ref, k_ref, v_ref, r_ref, sm_ref, w2_ref, b2_ref, ng_ref, s0_ref,
                o_ref, sout_ref, s_scr, b_scr, *, chunk, group, block, valid, nseq):
    blk = pl.program_id(1)

    @pl.when(blk == 0)
    def _():
        s_scr[...] = s0_ref[...]

    span = chunk * group
    lane = lax.broadcasted_iota(jnp.int32, (1, GLA_DK), 1)
    ones_kk = jnp.ones((GLA_DK, GLA_DK), BF16)
    row_in_chunk = lax.broadcasted_iota(jnp.int32, (span, 1), 0) % chunk
    eye = (lax.broadcasted_iota(jnp.int32, (GLA_DK, GLA_DK), 0)
           == lax.broadcasted_iota(jnp.int32, (GLA_DK, GLA_DK), 1))
    n_valid = min(valid, chunk)
    heads = range(GLA_H)
    chunks = range(group)

    def to3(t):
        return t.reshape(group, chunk, t.shape[-1])

    ri = lax.broadcasted_iota(jnp.int32, (block, block), 0)
    ci = lax.broadcasted_iota(jnp.int32, (block, block), 1)
    tri = ((ri // chunk == ci // chunk) & (ri >= ci)).astype(BF16)
    row_blk = lax.broadcasted_iota(jnp.int32, (block, 1), 0) % chunk
    for sq in range(nseq):
        x = _dot(sm_ref[sq].astype(BF16), w2_ref[...].astype(BF16)) + b2_ref[...]
        log_a = (jnp.minimum(x, 0.0) - jnp.log(1.0 + jnp.exp(-jnp.abs(x)))) * (1.0 / GLA_GATE_NORM)
        if valid < chunk:
            log_a = jnp.where(row_blk < valid, log_a, 0.0)
        b_scr[sq] = sum(_dot(tri, part) for part in _split3(log_a))

    def one_seq(sq, rows):
        b = b_scr[sq, rows, :]
        q = q_ref[sq, rows, :] * GLA_DK ** -0.5
        k = k_ref[sq, rows, :]
        v = v_ref[sq, rows, :]
        r = r_ref[sq, rows, :]
        b3, q_in, kv, d_col = [], [], [], []
        for h in heads:
            ks = slice(h * GLA_DK, (h + 1) * GLA_DK)
            bh3 = to3(b[:, ks])
            b_last = bh3[:, chunk - 1:chunk, :]
            k_out = (to3(k[:, ks]) * jnp.exp(b_last - bh3)).reshape(span, GLA_DK)
            vh = v[:, h * GLA_DV:(h + 1) * GLA_DV]
            b3.append(bh3)
            q_in.append(q[:, ks] * jnp.exp(b[:, ks]))
            kv.append([_dot_tn(k_out[c * chunk:(c + 1) * chunk], vh[c * chunk:(c + 1) * chunk]) for c in chunks])
            d_col.append([jnp.sum(jnp.where(eye, jnp.exp(b_last[c]), 0.0), axis=-1, keepdims=True) for c in chunks])
        yield
        st = [s_scr[sq, h] for h in heads]
        o_inter = [[] for _ in heads]
        for c in chunks:
            for h in heads:
                o_inter[h].append(_dot(q_in[h][c * chunk:(c + 1) * chunk], st[h]))
                st[h] = st[h] * d_col[h][c] + kv[h][c]
            yield
        for h in heads:
            ks = slice(h * GLA_DK, (h + 1) * GLA_DK)
            vs = slice(h * GLA_DV, (h + 1) * GLA_DV)
            s_scr[sq, h] = st[h]
            bh3, qh3, kh3, vh3 = b3[h], to3(q[:, ks]), to3(k[:, ks]), to3(v[:, vs])
            prods = []
            for j in range(n_valid):
                e = jnp.exp(bh3 - bh3[:, j:j + 1, :])
                prods.append((qh3 * e * kh3[:, j:j + 1, :]).reshape(span, GLA_DK).astype(BF16))
            sums = _dot(jnp.concatenate(prods, axis=0), ones_kk)
            yield
            att = jnp.zeros((span, GLA_DK), F32)
            for j in range(n_valid):
                att = jnp.where(lane == j, sums[j * span:(j + 1) * span], att)
            att = jnp.where(row_in_chunk >= lane, att, 0.0)[:, :chunk]
            vh = v[:, vs]
            o_intra = [_dot(att[c * chunk:(c + 1) * chunk], vh[c * chunk:(c + 1) * chunk]) for c in chunks]
            o = jnp.concatenate(o_intra, axis=0) + jnp.concatenate(o_inter[h], axis=0)
            o_ref[sq, rows, vs] = (_head_rms(o, ng_ref[...]) * _silu(r[:, vs])).astype(o_ref.dtype)
            yield

    def do_span(s, carry):
        rows = pl.ds(pl.multiple_of(s * span, span), span)
        _round_robin(one_seq(sq, rows) for sq in range(nseq))
        return carry

    lax.fori_loop(0, block // span, do_span, 0)

    @pl.when(blk == pl.num_programs(1) - 1)
    def _():
        sout_ref[...] = s_scr[...]


def _gla(proj, w2p, b2, ng, s0, chunk, group, block, valid, nseq):
    nb, seq, _ = proj.shape

    def col(width, off):
        return pl.BlockSpec((nseq, block, width), lambda b, i: (b, i, off // width))

    st_spec = pl.BlockSpec((nseq, GLA_H, GLA_DK, GLA_DV), lambda b, i: (b, 0, 0, 0))
    kern = functools.partial(_gla_kernel, chunk=chunk, group=group, block=block, valid=valid, nseq=nseq)
    return pl.pallas_call(
        kern,
        grid=(nb // nseq, seq // block),
        in_specs=[col(GLA_QK, AB_Q), col(GLA_QK, AB_K), col(GLA_V, AB_V), col(GLA_V, AB_R), col(LANE, AB_SMALL),
                  pl.BlockSpec((LANE, GLA_QK), lambda b, i: (0, 0)),
                  pl.BlockSpec((1, GLA_QK), lambda b, i: (0, 0)),
                  pl.BlockSpec((1, GLA_DV), lambda b, i: (0, 0)),
                  st_spec],
        out_specs=[pl.BlockSpec((nseq, block, GLA_V), lambda b, i: (b, i, 0)), st_spec],
        out_shape=[jax.ShapeDtypeStruct((nb, seq, GLA_V), _mixer_out_dtype(block)),
                   jax.ShapeDtypeStruct((nb, GLA_H, GLA_DK, GLA_DV), F32)],
        scratch_shapes=[pltpu.VMEM((nseq, GLA_H, GLA_DK, GLA_DV), F32), pltpu.VMEM((nseq, block, GLA_QK), F32)],
        compiler_params=_cparams(2), name="gla",
    )(proj, proj, proj, proj, proj, w2p, b2.reshape(1, GLA_QK), ng.reshape(1, GLA_DV), s0)


def _split2(a):
    hi = a.astype(BF16)
    return hi, (a - hi.astype(F32)).astype(BF16)


def _split3(a):
    hi = a.astype(BF16)
    rest = a - hi.astype(F32)
    mid = rest.astype(BF16)
    return hi, mid, (rest - mid.astype(F32)).astype(BF16)


def _dot3(a, b):
    return _dot(a[0], b[0]) + _dot(a[0], b[1]) + _dot(a[1], b[0])


def _inv_unit_lower_many(mats, n, eye):
    ps = [eye - a for a in mats]
    if n <= 2:
        return ps
    pows = [_split2(a) for a in mats]
    k = 2
    pending = None
    while k < n:
        sq = [_dot3(a, a) for a in pows]
        if pending is not None:
            ps = [p + _dot3(_split2(p), f) for p, f in zip(ps, pending)]
        pows = [_split2(a) for a in sq]
        pending = pows
        k *= 2
    return [p + _dot3(_split2(p), f) for p, f in zip(ps, pending)]


def _gdn_kernel(qkv_ref, sm_ref, gb_ref, cw_ref, cb_ref, alog_ref, dtb_ref, ng_ref, s0_ref, c0_ref,
                o_ref, sout_ref, cout_ref, s_scr, cv_scr, *, chunk, block, valid, nseq):
    blk = pl.program_id(1)

    @pl.when(blk == 0)
    def _():
        s_scr[...] = s0_ref[...]
        cv_scr[:, SUBLANE - (CONV_K - 1):SUBLANE, :] = c0_ref[...]

    ri = lax.broadcasted_iota(jnp.int32, (chunk, chunk), 0)
    ci = lax.broadcasted_iota(jnp.int32, (chunk, chunk), 1)
    causal = ri >= ci
    strict = ri > ci
    eye = (ri == ci).astype(F32)
    tri = causal.astype(F32)
    tri_u = (ri <= ci).astype(F32)
    inv_blk = min(INV_BLOCK, chunk)
    same_blk = (ri // inv_blk) == (ci // inv_blk)
    row = lax.broadcasted_iota(jnp.int32, (chunk, 1), 0)
    n_valid = min(valid, chunk)

    units = [(sq, h) for sq in range(nseq) for h in range(GDN_H)]
    heads = range(len(units))

    def prelude(rows, out):
        q, k, kb, rhs, dec, gcc = [], [], [], [], [], []
        for sq, h in units:
            yield
            if h == 0:
                act = _silu(_causal_conv_chunk(qkv_ref[sq, rows, :], cv_scr.at[sq], cw_ref, cb_ref, chunk, n_valid))
                sm = sm_ref[sq, rows, :]
                g_all = -jnp.exp(alog_ref[...]) * _softplus(sm + dtb_ref[...])
                beta_all = _sigmoid(sm)
                if valid < chunk:
                    g_all = jnp.where(row < valid, g_all, 0.0)
                    beta_all = jnp.where(row < valid, beta_all, 0.0)
                gc = _dot(tri, g_all, HIGHEST)
                gc_r = _dot_tn(g_all, tri_u, HIGHEST)
            qh = act[:, h * GDN_DK:(h + 1) * GDN_DK]
            kh = act[:, GDN_QK + h * GDN_DK:GDN_QK + (h + 1) * GDN_DK]
            vh = act[:, 2 * GDN_QK + h * GDN_DV:2 * GDN_QK + (h + 1) * GDN_DV]
            qh = qh * lax.rsqrt(jnp.sum(qh * qh, -1, keepdims=True) + EPS) * GDN_DK ** -0.5
            kh = kh * lax.rsqrt(jnp.sum(kh * kh, -1, keepdims=True) + EPS)
            beta = beta_all[:, AB_B_LANE + h:AB_B_LANE + h + 1]
            gch = gc[:, AB_A_LANE + h:AB_A_LANE + h + 1]
            gcr = gc_r[AB_A_LANE + h:AB_A_LANE + h + 1, :]
            q.append(qh)
            k.append(kh)
            kb.append(kh * beta)
            rhs.append(_split2(jnp.concatenate([vh * beta, kb[-1] * jnp.exp(gch)], axis=1)))
            dec.append(jnp.exp(jnp.where(causal, gch - gcr, -jnp.inf)))
            gcc.append(gch)
        yield
        kbf = [a.astype(BF16) for a in k]
        m = [jnp.where(strict, _dot_nt(kb[h].astype(BF16), kbf[h]) * dec[h], 0.0) for h in heads]
        att = [_dot_nt(q[h].astype(BF16), kbf[h]) * dec[h] for h in heads]
        out['pre'] = (q, k, rhs, gcc, m, att)

    def finish(pre, rows):
        q, k, rhs, gcc, m, att = pre
        m_diag = [jnp.where(same_blk, a, 0.0) for a in m]
        t = [_split2(a) for a in _inv_unit_lower_many(m_diag, inv_blk, eye)]
        yield
        y = [_dot3(t[h], rhs[h]) for h in heads]
        if chunk > inv_blk:
            n_off = [_dot3(t[h], _split2(m[h] - m_diag[h])) for h in heads]
            yield
            qn = [_split2(a) for a in _inv_unit_lower_many(n_off, chunk // inv_blk, eye)]
            yield
            y = [_dot3(qn[h], _split2(y[h])) for h in heads]
        yield
        st = [s_scr[sq, h] for sq, h in units]
        stb = [a.astype(BF16) for a in st]
        v_new = [y[h][:, :GDN_DV] - _dot(y[h][:, GDN_DV:].astype(BF16), stb[h]) for h in heads]
        vnb = [a.astype(BF16) for a in v_new]
        yield
        o = [_dot((q[h] * jnp.exp(gcc[h])).astype(BF16), stb[h]) + _dot(att[h].astype(BF16), vnb[h]) for h in heads]
        yield
        for u, (sq, h) in enumerate(units):
            hs = slice(h * GDN_DV, (h + 1) * GDN_DV)
            g_last = gcc[u][chunk - 1:chunk, :]
            k_out = (k[u] * jnp.exp(g_last - gcc[u])).astype(BF16)
            s_scr[sq, h] = st[u] * jnp.exp(g_last) + _dot_tn(k_out, vnb[u])
            o_ref[sq, rows, hs] = (_head_rms(o[u], ng_ref[...]) * _silu(gb_ref[sq, rows, hs])).astype(o_ref.dtype)

    n_chunks = block // chunk
    cur = {}
    _round_robin([prelude(pl.ds(0, chunk), cur)])
    for c in range(n_chunks):
        nxt = {}
        stages = [finish(cur['pre'], pl.ds(c * chunk, chunk))]
        if c + 1 < n_chunks:
            stages.append(prelude(pl.ds((c + 1) * chunk, chunk), nxt))
        _round_robin(stages)
        cur = nxt

    @pl.when(blk == pl.num_programs(1) - 1)
    def _():
        sout_ref[...] = s_scr[...]
        cout_ref[...] = cv_scr[:, SUBLANE - (CONV_K - 1):SUBLANE, :]


def _gdn(proj, conv_w, conv_b, alog_row, dtb_row, ng, s0, c0, chunk, block, valid, nseq):
    nb, seq, _ = proj.shape

    def col(width, off):
        return pl.BlockSpec((nseq, block, width), lambda b, i: (b, i, off // width))

    def const(shape):
        return pl.BlockSpec(shape, lambda b, i: (0,) * len(shape))

    st_spec = pl.BlockSpec((nseq, GDN_H, GDN_DK, GDN_DV), lambda b, i: (b, 0, 0, 0))
    cv_spec = pl.BlockSpec((nseq, CONV_K - 1, GDN_CONV_W), lambda b, i: (b, 0, 0))
    kern = functools.partial(_gdn_kernel, chunk=chunk, block=block, valid=valid, nseq=nseq)
    return pl.pallas_call(
        kern,
        grid=(nb // nseq, seq // block),
        in_specs=[col(GDN_CONV_W, AB_QKV), col(LANE, AB_SMALL), col(GDN_V, AB_G),
                  const((CONV_K, GDN_CONV_W)), const((1, GDN_CONV_W)), const((1, LANE)), const((1, LANE)),
                  const((1, GDN_DV)), st_spec, cv_spec],
        out_specs=[pl.BlockSpec((nseq, block, GDN_V), lambda b, i: (b, i, 0)), st_spec, cv_spec],
        out_shape=[jax.ShapeDtypeStruct((nb, seq, GDN_V), _mixer_out_dtype(block)),
                   jax.ShapeDtypeStruct((nb, GDN_H, GDN_DK, GDN_DV), F32),
                   jax.ShapeDtypeStruct((nb, CONV_K - 1, GDN_CONV_W), F32)],
        scratch_shapes=[pltpu.VMEM((nseq, GDN_H, GDN_DK, GDN_DV), F32),
                        pltpu.VMEM((nseq, SUBLANE + chunk, GDN_CONV_W), F32)],
        compiler_params=_cparams(2), name="gdn",
    )(proj, proj, proj, conv_w, conv_b.reshape(1, GDN_CONV_W), alog_row, dtb_row, ng.reshape(1, GDN_DV), s0, c0)


def _ssd_kernel(xbc_ref, sm_ref, z_ref, cw_ref, cb_ref, alog_ref, dtb_ref, drow_ref, ng_ref, s0_ref, c0_ref,
                o_ref, sout_ref, cout_ref, s_scr, cv_scr, dt_scr, acs_scr, acsr_scr, dtr_scr,
                *, chunk, block, valid, nseq):
    blk = pl.program_id(1)

    @pl.when(blk == 0)
    def _():
        s_scr[...] = s0_ref[...]
        cv_scr[:, SUBLANE - (CONV_K - 1):SUBLANE, :] = c0_ref[...]

    ri = lax.broadcasted_iota(jnp.int32, (chunk, chunk), 0)
    ci = lax.broadcasted_iota(jnp.int32, (chunk, chunk), 1)
    causal = ri >= ci
    eye = (ri == ci).astype(BF16)
    tri = causal.astype(BF16)
    tri_u = (ri <= ci).astype(BF16)
    row = lax.broadcasted_iota(jnp.int32, (chunk, 1), 0)
    lane_lo = lax.broadcasted_iota(jnp.int32, (chunk, LANE), 1) < SSD_P
    row_lo = lax.broadcasted_iota(jnp.int32, (2 * SSD_P, 1), 0) < SSD_P
    n_valid = min(valid, chunk)
    heads_per_group = SSD_H // SSD_G
    gsz = SSD_W // SSD_G

    for sq in range(nseq):
        for c in range(block // chunk):
            dt = _softplus(sm_ref[sq, c * chunk:(c + 1) * chunk, :] + dtb_ref[...])
            if valid < chunk:
                dt = jnp.where(row < valid, dt, 0.0)
            dta = _split3(dt * (-jnp.exp(alog_ref[...])))
            dt_scr[sq, c] = dt
            acs_scr[sq, c] = sum(_dot(tri, part) for part in dta)
            acsr_scr[sq, c] = sum(_dot_tn(part, tri_u) for part in dta)
            dtr_scr[sq, c] = sum(_dot_tn(part, eye) for part in _split3(dt))

    def one_seq(sq, s, rows):
        act = _silu(_causal_conv_chunk(xbc_ref[sq, rows, :], cv_scr.at[sq], cw_ref, cb_ref, chunk, n_valid))
        dt, acs, acs_r, dt_r = dt_scr[sq, s], acs_scr[sq, s], acsr_scr[sq, s], dtr_scr[sq, s]
        z = z_ref[sq, rows, :]
        for g in range(SSD_G):
            bg = act[:, SSD_W + g * SSD_N:SSD_W + (g + 1) * SSD_N]
            cg = act[:, SSD_W + SSD_G * SSD_N + g * SSD_N:SSD_W + SSD_G * SSD_N + (g + 1) * SSD_N]
            cb = _dot_nt(cg, bg)
            yield
            parts = []
            for pr in range(heads_per_group // 2):
                pi = g * (heads_per_group // 2) + pr
                xp = act[:, pi * LANE:(pi + 1) * LANE]
                st = s_scr[sq, pi]
                y_in, e_in, w_out, d_last = [], [], [], []
                for hh in range(2):
                    h = 2 * pi + hh
                    ac = acs[:, h:h + 1]
                    dec = jnp.exp(jnp.where(causal, ac - acs_r[h:h + 1, :], -jnp.inf))
                    y_in.append(_dot(cb * dec * dt_r[h:h + 1, :], xp))
                    a_last = ac[chunk - 1:chunk, :]
                    e_in.append(jnp.exp(ac))
                    w_out.append(jnp.exp(a_last - ac) * dt[:, h:h + 1])
                    d_last.append(jnp.exp(a_last))
                y = jnp.where(lane_lo, y_in[0], y_in[1])
                y = y + _dot_nt(cg, st) * jnp.where(lane_lo, e_in[0], e_in[1])
                y = y + drow_ref[:, pi * LANE:(pi + 1) * LANE] * xp
                x_sc = xp * jnp.where(lane_lo, w_out[0], w_out[1])
                s_scr[sq, pi] = st * jnp.where(row_lo, d_last[0], d_last[1]) + _dot_tn(x_sc, bg)
                parts.append(y)
                yield
            gs = slice(g * gsz, (g + 1) * gsz)
            yg = jnp.concatenate(parts, axis=1) * _silu(z[:, gs])
            o_ref[sq, rows, gs] = _head_rms(yg, ng_ref[:, gs]).astype(o_ref.dtype)

    def do_chunk(s, carry):
        rows = pl.ds(pl.multiple_of(s * chunk, chunk), chunk)
        _round_robin(one_seq(sq, s, rows) for sq in range(nseq))
        return carry

    lax.fori_loop(0, block // chunk, do_chunk, 0)

    @pl.when(blk == pl.num_programs(1) - 1)
    def _():
        sout_ref[...] = s_scr[...]
        cout_ref[...] = cv_scr[:, SUBLANE - (CONV_K - 1):SUBLANE, :]


def _ssd(proj, conv_w, conv_b, alog_row, dtb_row, d_row, ng, s0, c0, chunk, block, valid, nseq):
    nb, seq, _ = proj.shape
    n_pairs = SSD_H // 2

    def col(width, off):
        return pl.BlockSpec((nseq, block, width), lambda b, i: (b, i, off // width))

    def const(shape):
        return pl.BlockSpec(shape, lambda b, i: (0,) * len(shape))

    st_spec = pl.BlockSpec((nseq, n_pairs, 2 * SSD_P, SSD_N), lambda b, i: (b, 0, 0, 0))
    cv_spec = pl.BlockSpec((nseq, CONV_K - 1, SSD_CONV_W), lambda b, i: (b, 0, 0))
    kern = functools.partial(_ssd_kernel, chunk=chunk, block=block, valid=valid, nseq=nseq)
    o, s_new, c_new = pl.pallas_call(
        kern,
        grid=(nb // nseq, seq // block),
        in_specs=[col(SSD_CONV_W, CD_XBC), col(LANE, CD_SMALL), col(SSD_W, CD_Z),
                  const((CONV_K, SSD_CONV_W)), const((1, SSD_CONV_W)), const((1, LANE)), const((1, LANE)),
                  const((1, SSD_W)), const((1, SSD_W)), st_spec, cv_spec],
        out_specs=[pl.BlockSpec((nseq, block, SSD_W), lambda b, i: (b, i, 0)), st_spec, cv_spec],
        out_shape=[jax.ShapeDtypeStruct((nb, seq, SSD_W), _mixer_out_dtype(block)),
                   jax.ShapeDtypeStruct((nb, n_pairs, 2 * SSD_P, SSD_N), F32),
                   jax.ShapeDtypeStruct((nb, CONV_K - 1, SSD_CONV_W), F32)],
        scratch_shapes=[pltpu.VMEM((nseq, n_pairs, 2 * SSD_P, SSD_N), F32),
                        pltpu.VMEM((nseq, SUBLANE + chunk, SSD_CONV_W), F32),
                        pltpu.VMEM((nseq, block // chunk, chunk, LANE), F32),
                        pltpu.VMEM((nseq, block // chunk, chunk, LANE), F32),
                        pltpu.VMEM((nseq, block // chunk, LANE, chunk), F32),
                        pltpu.VMEM((nseq, block // chunk, LANE, chunk), F32)],
        compiler_params=_cparams(2), name="ssd",
    )(proj, proj, proj, conv_w, conv_b.reshape(1, SSD_CONV_W), alog_row, dtb_row, d_row,
      ng.reshape(1, SSD_W), s0.reshape(nb, n_pairs, 2 * SSD_P, SSD_N), c0)
    return o, s_new.reshape(nb, SSD_H, SSD_P, SSD_N), c_new


def _s5_prep_kernel(are_ref, aim_ref, ldt_ref, bre_ref, bim_ref, lbr_ref, lbi_ref, bbr_ref, bbi_ref):
    a_re, a_im = are_ref[...], aim_ref[...]
    dt = jnp.exp(ldt_ref[...])
    mag = jnp.exp(a_re * dt)
    lb_re, lb_im = mag * jnp.cos(a_im * dt), mag * jnp.sin(a_im * dt)
    nr, ni = lb_re - 1.0, lb_im
    den = a_re * a_re + a_im * a_im
    f_re = (nr * a_re + ni * a_im) / den
    f_im = (ni * a_re - nr * a_im) / den
    b_re, b_im = bre_ref[...], bim_ref[...]
    lbr_ref[...] = lb_re
    lbi_ref[...] = lb_im
    bbr_ref[...] = f_re * b_re - f_im * b_im
    bbi_ref[...] = f_re * b_im + f_im * b_re


def _s5_prep(a_re, a_im, log_dt, b_re, b_im):
    g3 = (S5_G, 1, S5_P)
    b3 = (S5_G, S5_GS, S5_P)
    return pl.pallas_call(
        _s5_prep_kernel,
        out_shape=[jax.ShapeDtypeStruct(g3, F32), jax.ShapeDtypeStruct(g3, F32),
                   jax.ShapeDtypeStruct(b3, F32), jax.ShapeDtypeStruct(b3, F32)],
        name="s5_prep",
    )(a_re.reshape(g3), a_im.reshape(g3), log_dt.reshape(S5_G, 1, 1),
      jnp.swapaxes(b_re, 1, 2), jnp.swapaxes(b_im, 1, 2))


def _block_diag(blocks):
    g, r, c = blocks.shape
    per = 8
    b = blocks.reshape(g // per, per, r, 1, c) * jnp.eye(per, dtype=blocks.dtype).reshape(1, per, 1, per, 1)
    return b.reshape(g // per, per * r, per * c)


def _s5_kernel(u_ref, wre_ref, wim_ref, cre_ref, cim_ref, lbr_ref, lbi_ref, d_ref, x0r_ref, x0i_ref,
               y_ref, xfr_ref, xfi_ref, xr_scr, xi_scr, sr_scr, si_scr, *, rows_per_step, steps):
    c = pl.program_id(2)

    @pl.when(c == 0)
    def _():
        xr_scr[...] = x0r_ref[0]
        xi_scr[...] = x0i_ref[0]

    u = u_ref[0]
    sr_scr[...] = _dot(u, wre_ref[0])
    si_scr[...] = _dot(u, wim_ref[0])
    l_re, l_im = lbr_ref[...], lbi_ref[...]

    def step(t, carry):
        rows = pl.ds(pl.multiple_of(t * rows_per_step, rows_per_step), rows_per_step)
        xr, xi = xr_scr[...], xi_scr[...]
        nr = l_re * xr - l_im * xi + sr_scr[rows, :]
        ni = l_re * xi + l_im * xr + si_scr[rows, :]
        xr_scr[...] = nr
        xi_scr[...] = ni
        sr_scr[rows, :] = nr
        si_scr[rows, :] = ni
        return carry

    lax.fori_loop(0, steps, step, 0)
    y_ref[0] = _dot(sr_scr[...], cre_ref[0]) - _dot(si_scr[...], cim_ref[0]) + d_ref[...] * u

    @pl.when(c == pl.num_programs(2) - 1)
    def _():
        xfr_ref[0] = xr_scr[...]
        xfi_ref[0] = xi_scr[...]


def _s5(proj, w_re, w_im, c_re, c_im, lb_re, lb_im, d, x0_re, x0_im, rows_per_step, steps):
    ng, n_tok, _ = proj.shape
    nj = S5_W // LANE
    sw = S5_STATE // nj
    cr = rows_per_step * steps
    kern = functools.partial(_s5_kernel, rows_per_step=rows_per_step, steps=steps)
    x_spec = pl.BlockSpec((1, rows_per_step, sw), lambda g, j, c: (g, 0, j))
    return pl.pallas_call(
        kern,
        grid=(ng, nj, n_tok // cr),
        in_specs=[pl.BlockSpec((1, cr, LANE), lambda g, j, c: (g, c, CD_U // LANE + j)),
                  pl.BlockSpec((1, LANE, sw), lambda g, j, c: (j, 0, 0)),
                  pl.BlockSpec((1, LANE, sw), lambda g, j, c: (j, 0, 0)),
                  pl.BlockSpec((1, sw, LANE), lambda g, j, c: (j, 0, 0)),
                  pl.BlockSpec((1, sw, LANE), lambda g, j, c: (j, 0, 0)),
                  pl.BlockSpec((1, sw), lambda g, j, c: (0, j)),
                  pl.BlockSpec((1, sw), lambda g, j, c: (0, j)),
                  pl.BlockSpec((1, LANE), lambda g, j, c: (0, j)),
                  x_spec, x_spec],
        out_specs=[pl.BlockSpec((1, cr, LANE), lambda g, j, c: (g, c, j)), x_spec, x_spec],
        out_shape=[jax.ShapeDtypeStruct((ng, n_tok, S5_W), F32),
                   jax.ShapeDtypeStruct((ng, rows_per_step, S5_STATE), F32),
                   jax.ShapeDtypeStruct((ng, rows_per_step, S5_STATE), F32)],
        scratch_shapes=[pltpu.VMEM((rows_per_step, sw), F32), pltpu.VMEM((rows_per_step, sw), F32),
                        pltpu.VMEM((cr, sw), F32), pltpu.VMEM((cr, sw), F32)],
        compiler_params=_cparams(3), name="s5_scan",
    )(proj, w_re, w_im, c_re, c_im, lb_re, lb_im, d.reshape(1, S5_W), x0_re, x0_im)


def _s5_pow_kernel(lbr_ref, lbi_ref, pr_ref, pi_ref, *, n_rows):
    l_re, l_im = lbr_ref[...], lbi_ref[...]
    row = lax.broadcasted_iota(jnp.int32, (SUBLANE, 1), 0)
    p_re, p_im = l_re, l_im
    b_re = jnp.broadcast_to(l_re, (SUBLANE, l_re.shape[1]))
    b_im = jnp.broadcast_to(l_im, (SUBLANE, l_re.shape[1]))
    for r in range(1, SUBLANE):
        p_re, p_im = p_re * l_re - p_im * l_im, p_re * l_im + p_im * l_re
        b_re = jnp.where(row >= r, p_re, b_re)
        b_im = jnp.where(row >= r, p_im, b_im)
    q_re, q_im = jnp.ones_like(l_re), jnp.zeros_like(l_re)
    for a in range(n_rows // SUBLANE):
        rows = slice(a * SUBLANE, (a + 1) * SUBLANE)
        pr_ref[rows, :] = b_re * q_re - b_im * q_im
        pi_ref[rows, :] = b_re * q_im + b_im * q_re
        q_re, q_im = q_re * p_re - q_im * p_im, q_re * p_im + q_im * p_re


def _s5_pow_table(lb_re, lb_im, n_rows):
    shape = jax.ShapeDtypeStruct((n_rows, S5_STATE), F32)
    return pl.pallas_call(functools.partial(_s5_pow_kernel, n_rows=n_rows), out_shape=[shape, shape],
                          name="s5_pow")(lb_re, lb_im)


def _s5_seg_kernel(u_ref, wre_ref, wim_ref, cre_ref, cim_ref, pr_ref, pi_ref, d_ref, x0r_ref, x0i_ref,
                   y_ref, xfr_ref, xfi_ref, sr_scr, si_scr, *, seg_len):
    n_seg = SUBLANE
    n_lane_blk = sr_scr.shape[0]
    sw = n_lane_blk * LANE
    lane_blks = [slice(c * LANE, (c + 1) * LANE) for c in range(n_lane_blk)]

    def put(scr, rows, val):
        for c, ls in enumerate(lane_blks):
            scr[c, rows, :] = val[:, ls]

    def get(scr, rows):
        return jnp.concatenate([scr[c, rows, :] for c in range(n_lane_blk)], axis=1)

    for s in range(n_seg):
        us = u_ref[0, s * seg_len:(s + 1) * seg_len, :]
        put(sr_scr, pl.ds(s, seg_len, stride=n_seg), _dot(us, wre_ref[0]))
        put(si_scr, pl.ds(s, seg_len, stride=n_seg), _dot(us, wim_ref[0]))
    l_re, l_im = pr_ref[0:1, :], pi_ref[0:1, :]

    def step(t, carry):
        xr, xi = carry
        rows = pl.ds(pl.multiple_of(t * n_seg, n_seg), n_seg)
        nr = l_re * xr - l_im * xi + get(sr_scr, rows)
        ni = l_re * xi + l_im * xr + get(si_scr, rows)
        put(sr_scr, rows, nr)
        put(si_scr, rows, ni)
        return nr, ni

    zero = jnp.zeros((n_seg, sw), F32)
    end_re, end_im = lax.fori_loop(0, seg_len, step, (zero, zero), unroll=4)
    ln_re, ln_im = pr_ref[seg_len - 1:seg_len, :], pi_ref[seg_len - 1:seg_len, :]
    p_re, p_im = pr_ref[...], pi_ref[...]
    x_re, x_im = x0r_ref[0], x0i_ref[0]
    for s in range(n_seg):
        loc_re = get(sr_scr, pl.ds(s, seg_len, stride=n_seg))
        loc_im = get(si_scr, pl.ds(s, seg_len, stride=n_seg))
        t_re = loc_re + p_re * x_re - p_im * x_im
        t_im = loc_im + p_re * x_im + p_im * x_re
        us = u_ref[0, s * seg_len:(s + 1) * seg_len, :]
        y_ref[0, s * seg_len:(s + 1) * seg_len, :] = (_dot(t_re, cre_ref[0]) - _dot(t_im, cim_ref[0])
                                                      + d_ref[...] * us)
        e_re, e_im = end_re[s:s + 1, :], end_im[s:s + 1, :]
        x_re, x_im = e_re + ln_re * x_re - ln_im * x_im, e_im + ln_re * x_im + ln_im * x_re
    xfr_ref[0] = x_re
    xfi_ref[0] = x_im


def _s5_seg(proj, w_re, w_im, c_re, c_im, pow_re, pow_im, d, x0_re, x0_im):
    nb, seq, _ = proj.shape
    nj = S5_W // LANE
    sw = S5_STATE // nj
    seg_len = seq // SUBLANE
    x_spec = pl.BlockSpec((1, 1, sw), lambda b, j: (b, 0, j))
    return pl.pallas_call(
        functools.partial(_s5_seg_kernel, seg_len=seg_len),
        grid=(nb, nj),
        in_specs=[pl.BlockSpec((1, seq, LANE), lambda b, j: (b, 0, CD_U // LANE + j)),
                  pl.BlockSpec((1, LANE, sw), lambda b, j: (j, 0, 0)),
                  pl.BlockSpec((1, LANE, sw), lambda b, j: (j, 0, 0)),
                  pl.BlockSpec((1, sw, LANE), lambda b, j: (j, 0, 0)),
                  pl.BlockSpec((1, sw, LANE), lambda b, j: (j, 0, 0)),
                  pl.BlockSpec((seg_len, sw), lambda b, j: (0, j)),
                  pl.BlockSpec((seg_len, sw), lambda b, j: (0, j)),
                  pl.BlockSpec((1, LANE), lambda b, j: (0, j)),
                  x_spec, x_spec],
        out_specs=[pl.BlockSpec((1, seq, LANE), lambda b, j: (b, 0, j)), x_spec, x_spec],
        out_shape=[jax.ShapeDtypeStruct((nb, seq, S5_W), F32),
                   jax.ShapeDtypeStruct((nb, 1, S5_STATE), F32),
                   jax.ShapeDtypeStruct((nb, 1, S5_STATE), F32)],
        scratch_shapes=[pltpu.VMEM((sw // LANE, seq, LANE), F32), pltpu.VMEM((sw // LANE, seq, LANE), F32)],
        compiler_params=_cparams(2), name="s5_seg",
    )(proj, w_re, w_im, c_re, c_im, pow_re, pow_im, d.reshape(1, S5_W), x0_re, x0_im)


def _lane_row(vec, lane0):
    return jnp.zeros((1, LANE), F32).at[0, lane0:lane0 + vec.shape[0]].set(vec.astype(F32))


def _realign_kernel(w_ref, o_ref, *, segments):
    o_ref[...] = jnp.zeros_like(o_ref)
    for src, rows, dst in segments:
        o_ref[dst:dst + rows, :] = w_ref[src:src + rows, :].astype(BF16)


def _realign_weights(w, segments, n_out):
    wt = jnp.swapaxes(w, 1, 2)[0]
    n_src, k = wt.shape
    assert all(v % (2 * SUBLANE) == 0 for seg in segments for v in seg)
    cols = REALIGN_COLS
    return pl.pallas_call(
        functools.partial(_realign_kernel, segments=segments),
        grid=(k // cols,),
        in_specs=[pl.BlockSpec((n_src, cols), lambda i: (0, i))],
        out_specs=pl.BlockSpec((n_out, cols), lambda i: (0, i)),
        out_shape=jax.ShapeDtypeStruct((n_out, k), BF16),
        compiler_params=_cparams(1), name="realign_weights",
    )(wt)


def _prep_params(p):
    q = {}
    o_lr = 2 * GLA_QK + GLA_V
    o_r = o_lr + GLA_LR
    o_qkv = o_r + GLA_V
    o_a = o_qkv + GDN_CONV_W
    o_g = o_a + 2 * GDN_H
    q['w_in_ab'] = _realign_weights(
        p['w_in_ab'], [(0, o_lr, AB_Q), (o_r, GLA_V, AB_R), (o_qkv, GDN_CONV_W, AB_QKV), (o_g, GDN_V, AB_G),
                       (o_lr, GLA_LR, AB_SMALL + AB_LR_LANE), (o_a, 2 * GDN_H, AB_SMALL + AB_A_LANE)], AB_N)
    q['gla_w2'] = jnp.zeros((LANE, GLA_QK), F32).at[:GLA_LR].set(p['gla_w2'][0])
    q['gdn_alog'] = _lane_row(p['gdn_A_log'][0], AB_A_LANE)
    q['gdn_dtb'] = _lane_row(p['gdn_dt_bias'][0], AB_A_LANE)
    q['w_out_ab'] = p['w_out_ab'].astype(BF16)
    o_xbc = SSD_W
    o_dt = o_xbc + SSD_CONV_W
    o_u = o_dt + SSD_H
    q['w_in_cd'] = _realign_weights(
        p['w_in_cd'], [(o_xbc, SSD_CONV_W, CD_XBC), (0, SSD_W, CD_Z), (o_u, S5_W, CD_U), (o_dt, SSD_H, CD_SMALL)],
        CD_N)
    q['ssd_alog'] = _lane_row(p['ssd_A_log'][0], 0)
    q['ssd_dtb'] = _lane_row(p['ssd_dt_bias'][0], 0)
    q['ssd_d_row'] = jnp.repeat(p['ssd_D'][0].astype(F32), SSD_P).reshape(1, SSD_W)
    q['w_out_cd'] = p['w_out_cd'].astype(BF16)
    lb_re, lb_im, bb_re, bb_im = _s5_prep(p['s5_A_re'][0], p['s5_A_im'][0], p['s5_log_dt'][0],
                                          p['s5_B_re'][0], p['s5_B_im'][0])
    q['s5_lb_re'], q['s5_lb_im'] = lb_re.reshape(1, S5_STATE), lb_im.reshape(1, S5_STATE)
    q['s5_w_re'], q['s5_w_im'] = _block_diag(bb_re), _block_diag(bb_im)
    q['s5_c_re'] = _block_diag(jnp.swapaxes(p['s5_C_re'][0], 1, 2))
    q['s5_c_im'] = _block_diag(jnp.swapaxes(p['s5_C_im'][0], 1, 2))
    q['s5_glu_w'] = p['s5_glu_w'][0].astype(BF16)
    q['w_ffn_down'] = p['w_ffn_down'].astype(BF16)
    return q


def _trunk(x, mods, grp, seq_shape, state, p, q):
    nb, seq_len, valid = seq_shape
    s_gla, s_gdn, s_gdnc, s_ssd, s_ssdc, s_re, s_im, s_ffn = state
    prompt = not grp.per_token_mod

    def to_seq(t):
        if prompt:
            return t.reshape(nb, seq_len, t.shape[-1])
        t = jnp.swapaxes(t.reshape(valid, nb, t.shape[-1]), 0, 1)
        return jnp.pad(t, ((0, 0), (0, seq_len - valid), (0, 0)))

    def from_seq(t):
        if prompt:
            return t.reshape(nb * seq_len, t.shape[-1])
        return jnp.swapaxes(t[:, :valid], 0, 1).reshape(valid * nb, t.shape[-1])

    blk = MIX_BLOCK if prompt else seq_len
    chunks = (GLA_CHUNK, GDN_CHUNK, SSD_CHUNK) if prompt else (seq_len,) * 3
    new = {}

    grp_norm = grp.retiled(NORM_TILE)
    h = _norm_mod(x, p['g_mix'][0], mods[0], grp_norm, 1, 0)
    proj = to_seq(_matmul(h, q['w_in_ab'], grp, AB_N // 3))
    o_a, new['gla'] = _gla(proj, q['gla_w2'], p['gla_b2'][0], p['gla_norm_g'][0], s_gla, chunks[0],
                            GLA_GROUP if prompt else 1, blk, valid, 1 if prompt else SAMPLE_SEQS)
    o_b, new['gdn'], new['gdnc'] = _gdn(proj, p['gdn_conv_w'][0], p['gdn_conv_b'][0], q['gdn_alog'], q['gdn_dtb'],
                                        p['gdn_norm_g'][0], s_gdn, s_gdnc, chunks[1], blk, valid,
                                        1 if prompt else SAMPLE_SEQS)
    x, h = _mm_residual([(from_seq(o_a), q['w_out_ab'], 0, 0), (from_seq(o_b), q['w_out_ab'], 0, 1)],
                        x, mods[0], grp, 2, D_MODEL, norm=(p['g_ffn'][0], 4, 3))
    act, new['ffn0'] = _ffn_up(h, p['w_ffn_up'], p['ffn_conv_w'], p['ffn_conv_b'], 0, s_ffn[0],
                               grp.retiled(FFN_TILE))
    x = _mm_residual([(act, q['w_ffn_down'], 0, 0)], x, mods[0], grp, 5, DOWN_TN)

    h = _norm_mod(x, p['g_mix'][1], mods[1], grp_norm, 1, 0)
    proj2 = _matmul(h, q['w_in_cd'], grp, CD_N // 2)
    proj = to_seq(proj2)
    o_c, new['ssd'], new['ssdc'] = _ssd(proj, p['ssd_conv_w'][0], p['ssd_conv_b'][0], q['ssd_alog'], q['ssd_dtb'],
                                        q['ssd_d_row'], p['ssd_norm_g'][0], s_ssd, s_ssdc, chunks[2], blk, valid,
                                        1 if prompt else SAMPLE_SEQS)
    if prompt:
        pow_re, pow_im = _s5_pow_table(q['s5_lb_re'], q['s5_lb_im'], seq_len // SUBLANE)
        yd, new['re'], new['im'] = _s5_seg(proj, q['s5_w_re'], q['s5_w_im'], q['s5_c_re'], q['s5_c_im'],
                                           pow_re, pow_im, p['s5_D'][0], s_re, s_im)
    else:
        yd, new['re'], new['im'] = _s5(proj2.reshape(1, grp.n_tok, CD_N), q['s5_w_re'], q['s5_w_im'],
                                       q['s5_c_re'], q['s5_c_im'], q['s5_lb_re'], q['s5_lb_im'], p['s5_D'][0],
                                       s_re, s_im, nb, valid)
    o_d = _s5_glu(yd.reshape(grp.n_tok, S5_W), q['s5_glu_w'], p['s5_glu_b'][0], grp)
    x, h = _mm_residual([(from_seq(o_c), q['w_out_cd'], 0, 0), (o_d, q['w_out_cd'], 0, 1)], x, mods[1], grp, 2,
                        D_MODEL, norm=(p['g_ffn'][1], 4, 3))
    act, new['ffn1'] = _ffn_up(h, p['w_ffn_up'], p['ffn_conv_w'], p['ffn_conv_b'], 1, s_ffn[1],
                               grp.retiled(FFN_TILE))
    x = _mm_residual([(act, q['w_ffn_down'], 1, 0)], x, mods[1], grp, 5, DOWN_TN)
    return _final_rms(x, p['g_final'], grp_norm), new


def kernel(x_prompt, x_sample, c_prompt, c_sample, state_gla, state_gdn, state_gdn_conv, state_ssd, state_ssd_conv, state_s5_re, state_s5_im, state_ffn_conv, w_ada, b_ada, g_mix, g_ffn, w_in_ab, gla_w2, gla_b2, gla_norm_g, gdn_conv_w, gdn_conv_b, gdn_A_log, gdn_dt_bias, gdn_norm_g, w_out_ab, w_in_cd, ssd_conv_w, ssd_conv_b, ssd_A_log, ssd_dt_bias, ssd_D, ssd_norm_g, s5_A_re, s5_A_im, s5_B_re, s5_B_im, s5_C_re, s5_C_im, s5_D, s5_log_dt, s5_glu_w, s5_glu_b, w_out_cd, w_ffn_up, ffn_conv_w, ffn_conv_b, w_ffn_down, g_final):
    p = dict(g_mix=g_mix, g_ffn=g_ffn, w_in_ab=w_in_ab, gla_w2=gla_w2, gla_b2=gla_b2, gla_norm_g=gla_norm_g,
             gdn_conv_w=gdn_conv_w, gdn_conv_b=gdn_conv_b, gdn_A_log=gdn_A_log, gdn_dt_bias=gdn_dt_bias,
             gdn_norm_g=gdn_norm_g, w_out_ab=w_out_ab, w_in_cd=w_in_cd, ssd_conv_w=ssd_conv_w,
             ssd_conv_b=ssd_conv_b, ssd_A_log=ssd_A_log, ssd_dt_bias=ssd_dt_bias, ssd_D=ssd_D,
             ssd_norm_g=ssd_norm_g, s5_A_re=s5_A_re, s5_A_im=s5_A_im, s5_B_re=s5_B_re, s5_B_im=s5_B_im,
             s5_C_re=s5_C_re, s5_C_im=s5_C_im, s5_D=s5_D, s5_log_dt=s5_log_dt, s5_glu_w=s5_glu_w,
             s5_glu_b=s5_glu_b, w_out_cd=w_out_cd, w_ffn_up=w_ffn_up, ffn_conv_w=ffn_conv_w,
             ffn_conv_b=ffn_conv_b, w_ffn_down=w_ffn_down, g_final=g_final)
    bp, lp, d = x_prompt.shape
    bs, ls, _ = x_sample.shape
    q = _prep_params(p)

    bp_pad = -(-bp // SUBLANE) * SUBLANE
    c_all = jnp.concatenate([c_prompt, jnp.zeros((bp_pad - bp, d), F32), c_sample], axis=0)
    mod = _ada_mod(c_all, w_ada, b_ada)
    depth = w_ada.shape[0]
    mods_p = [mod[l, :bp].reshape(bp, 1, 6 * d) for l in range(depth)]
    mods_s = [jnp.tile(mod[l, bp_pad:], (ls, 1)).reshape(1, ls * bs, 6 * d) for l in range(depth)]

    tile_p = 512
    grp_p = _Group(bp * lp, tile_p, False, lp // tile_p, 1)
    zeros = lambda *shape: jnp.zeros(shape, F32)
    state_p = (zeros(bp, GLA_H, GLA_DK, GLA_DV), zeros(bp, GDN_H, GDN_DK, GDN_DV),
               zeros(bp, CONV_K - 1, GDN_CONV_W), zeros(bp, SSD_H, SSD_P, SSD_N),
               zeros(bp, CONV_K - 1, SSD_CONV_W), zeros(bp, 1, S5_STATE), zeros(bp, 1, S5_STATE),
               zeros(depth, bp, FFN_K - 1, 2 * D_FF))
    y_p, new_p = _trunk(x_prompt.reshape(bp * lp, d), mods_p, grp_p, (bp, lp, lp), state_p, p, q)

    grp_s = _Group(bs * ls, bs * ls, True, 1, bs)
    ffn_hist_s = jnp.swapaxes(state_ffn_conv, 1, 2).reshape(depth, 1, (FFN_K - 1) * bs, 2 * D_FF)
    state_s = (state_gla[0], state_gdn[0], state_gdn_conv[0], state_ssd[0], state_ssd_conv[0],
               state_s5_re.reshape(1, bs, S5_STATE), state_s5_im.reshape(1, bs, S5_STATE), ffn_hist_s)
    x_s = jnp.swapaxes(x_sample, 0, 1).reshape(ls * bs, d)
    y_s, new_s = _trunk(x_s, mods_s, grp_s, (bs, SAMPLE_PAD, ls), state_s, p, q)
    y_s = jnp.swapaxes(y_s.reshape(ls, bs, d), 0, 1)

    ffn_p = jnp.stack([new_p['ffn0'], new_p['ffn1']])
    ffn_s = jnp.stack([jnp.swapaxes(new_s[k].reshape(FFN_K - 1, bs, 2 * D_FF), 0, 1) for k in ('ffn0', 'ffn1')])
    s5_shape = lambda t, nb: t.reshape(1, nb, S5_G, S5_P)
    return (y_p.reshape(bp, lp, d), y_s,
            new_p['gla'][None], new_s['gla'][None], new_p['gdn'][None], new_s['gdn'][None],
            new_p['gdnc'][None], new_s['gdnc'][None], new_p['ssd'][None], new_s['ssd'][None],
            new_p['ssdc'][None], new_s['ssdc'][None],
            s5_shape(new_p['re'], bp), s5_shape(new_s['re'], bs), s5_shape(new_p['im'], bp), s5_shape(new_s['im'], bs),
            ffn_p, ffn_s)
```
